```python
import math
import jax, jax.numpy as jnp
from jax import lax
import numpy as np

D_MODEL = 2048
BATCH = 1
SEQ = 16384
DEPTH = 2

Q_BLOCK = 128
DIFF_HEADS = 8
DIFF_HEAD_DIM = 64
DIFF_V_DIM = 2 * DIFF_HEAD_DIM
DIFF_QK = DIFF_HEADS * 2 * DIFF_HEAD_DIM
DIFF_V = DIFF_HEADS * DIFF_V_DIM
GLA_HEADS = 4
GLA_DK = 128
GLA_DV = 256
GLA_QK = GLA_HEADS * GLA_DK
GLA_V = GLA_HEADS * GLA_DV
GLA_GATE_RANK = 16
GLA_GATE_TEMP = 16.0
GLA_CHUNK = 64
SB_HEADS = 8
SB_HEAD_DIM = 128
SB_W = SB_HEADS * SB_HEAD_DIM
S5_CHANNELS = 1024
S5_GROUP = 16
S5_GROUPS = S5_CHANNELS // S5_GROUP
S5_STATE = 64
N_EXPERTS = 16
N_GROUPS = 4
EXPERTS_PER_GROUP = N_EXPERTS // N_GROUPS
TOP_K = 2
D_EXPERT = 640
DEEPNORM_ALPHA = (2.0 * DEPTH) ** 0.25
DEEPNORM_BETA = (8.0 * DEPTH) ** -0.25
LN_EPS = 1e-5

EVEN_SPLITS = (DIFF_QK, DIFF_QK, DIFF_V, GLA_QK, GLA_QK, GLA_V, GLA_GATE_RANK, GLA_V)
EVEN_IN = sum(EVEN_SPLITS)
EVEN_SPLIT_POINTS = tuple(int(v) for v in np.cumsum(EVEN_SPLITS)[:-1])
EVEN_MIX = DIFF_V + GLA_V
ODD_SPLITS = (SB_W, SB_W, SB_W, S5_CHANNELS)
ODD_IN = sum(ODD_SPLITS)
ODD_SPLIT_POINTS = tuple(int(v) for v in np.cumsum(ODD_SPLITS)[:-1])
ODD_MIX = SB_W + S5_CHANNELS

kernel_name = "hybrid_diffattn_gla_stickbreak_s5_groupmoe"


def layer_norm(x, g, b):
    xf = x.astype(jnp.float32)
    mu = jnp.mean(xf, axis=-1, keepdims=True)
    var = jnp.mean(jnp.square(xf - mu), axis=-1, keepdims=True)
    return ((xf - mu) * lax.rsqrt(var + LN_EPS) * g + b).astype(x.dtype)


def rms_norm(x, g):
    xf = x.astype(jnp.float32)
    return (xf * lax.rsqrt(jnp.mean(xf * xf, axis=-1, keepdims=True) + LN_EPS) * g).astype(x.dtype)


def sweep_query_blocks(fn, q, seq):
    n_blk = seq // Q_BLOCK

    def body(i):
        start = i * Q_BLOCK
        return fn(lax.dynamic_slice_in_dim(q, start, Q_BLOCK, axis=1), start)

    out = lax.map(body, jnp.arange(n_blk))
    out = jnp.moveaxis(out, 0, 1)
    return out.reshape((out.shape[0], seq) + out.shape[3:])


def diff_attention(q, k, v, lam_q1, lam_k1, lam_q2, lam_k2, subln_g, lambda_init):
    B, T = q.shape[:2]
    f32 = jnp.float32
    scale = DIFF_HEAD_DIM ** -0.5
    lam = (jnp.exp(jnp.sum(lam_q1.astype(f32) * lam_k1.astype(f32)))
           - jnp.exp(jnp.sum(lam_q2.astype(f32) * lam_k2.astype(f32))) + lambda_init)
    k_pos = jnp.arange(T)

    def block(q_blk, start):
        s = jnp.einsum('bqhmd,bkhmd->bhmqk', q_blk, k).astype(f32) * scale
        q_pos = start + jnp.arange(Q_BLOCK)
        causal = k_pos[None, :] <= q_pos[:, None]
        p = jax.nn.softmax(jnp.where(causal, s, -jnp.inf), axis=-1)
        w = p[:, :, 0] - lam * p[:, :, 1]
        return jnp.einsum('bhqk,bkhv->bqhv', w.astype(v.dtype), v)

    o = sweep_query_blocks(block, q, T)
    o = rms_norm(o, subln_g) * (1.0 - lambda_init)
    return o.reshape(B, T, DIFF_V)


def gla_chunked(q, k, v, log_a):
    B, T, H, dk = q.shape
    dv = v.shape[-1]
    C = GLA_CHUNK
    N = T // C
    f32 = jnp.float32
    qf = (q.astype(f32) * dk ** -0.5).reshape(B, N, C, H, dk)
    kf = k.astype(f32).reshape(B, N, C, H, dk)
    vf = v.astype(f32).reshape(B, N, C, H, dv)
    b = jnp.cumsum(log_a.astype(f32).reshape(B, N, C, H, dk), axis=2)
    b_last = b[:, :, -1:]
    q_dec = qf * jnp.exp(b)
    scores = jnp.einsum('bnthk,bnshk->bnhts', q_dec, kf * jnp.exp(-b))
    causal = jnp.tril(jnp.ones((C, C), dtype=bool))
    o_intra = jnp.einsum('bnhts,bnshv->bnthv', jnp.where(causal, scores, 0.0), vf)
    kv = jnp.einsum('bnshk,bnshv->bnhkv', kf * jnp.exp(b_last - b), vf)
    decay = jnp.exp(b_last[:, :, 0])

    def step(S, inp):
        d_n, kv_n = inp
        return d_n[..., None] * S + kv_n, S

    S0 = jnp.zeros((B, H, dk, dv), f32)
    _, S_prev = lax.scan(step, S0, (jnp.moveaxis(decay, 1, 0), jnp.moveaxis(kv, 1, 0)))
    S_prev = jnp.moveaxis(S_prev, 0, 1)
    o_inter = jnp.einsum('bnthk,bnhkv->bnthv', q_dec, S_prev)
    return (o_intra + o_inter).reshape(B, T, H, dv).astype(v.dtype)


def stick_breaking_attention(q, k, v):
    B, T, H, d = q.shape
    f32 = jnp.float32
    scale = d ** -0.5
    k_pos = jnp.arange(T)

    def block(q_blk, start):
        z = jnp.einsum('bqhd,bkhd->bhqk', q_blk, k).astype(f32) * scale
        q_pos = start + jnp.arange(Q_BLOCK)
        strict = k_pos[None, :] < q_pos[:, None]
        log_1m = jnp.where(strict, jax.nn.log_sigmoid(-z), 0.0)
        rev = lax.cumsum(log_1m, axis=3, reverse=True)
        excl = jnp.concatenate([rev[..., 1:], jnp.zeros_like(rev[..., :1])], axis=-1)
        w = jnp.where(strict, jnp.exp(jax.nn.log_sigmoid(z) + excl), 0.0)
        return jnp.einsum('bhqk,bkhd->bqhd', w.astype(v.dtype), v)

    o = sweep_query_blocks(block, q, T)
    return o.reshape(B, T, H * d)


def s5_layer(u, a_re, a_im, log_step, b_re, b_im, c_re, c_im, d_skip, w_glu, b_glu):
    B, T, _ = u.shape
    f32 = jnp.float32
    uf = u.astype(f32)
    ug = uf.reshape(B, T, S5_GROUPS, S5_GROUP)
    step = jnp.exp(log_step.astype(f32))[:, None]
    lam_re = a_re.astype(f32)
    lam_im = a_im.astype(f32)
    mag = jnp.exp(step * lam_re)
    lb_re = mag * jnp.cos(step * lam_im)
    lb_im = mag * jnp.sin(step * lam_im)
    den = lam_re * lam_re + lam_im * lam_im
    n_re = lb_re - 1.0
    f_re = (n_re * lam_re + lb_im * lam_im) / den
    f_im = (lb_im * lam_re - n_re * lam_im) / den
    bb_re = f_re[..., None] * b_re - f_im[..., None] * b_im
    bb_im = f_re[..., None] * b_im + f_im[..., None] * b_re
    bu_re = jnp.einsum('gpc,btgc->btgp', bb_re, ug)
    bu_im = jnp.einsum('gpc,btgc->btgp', bb_im, ug)
    a_re_t = jnp.broadcast_to(lb_re, bu_re.shape)
    a_im_t = jnp.broadcast_to(lb_im, bu_im.shape)

    def combine(e1, e2):
        a1r, a1i, b1r, b1i = e1
        a2r, a2i, b2r, b2i = e2
        return (a2r * a1r - a2i * a1i, a2r * a1i + a2i * a1r,
                a2r * b1r - a2i * b1i + b2r, a2r * b1i + a2i * b1r + b2i)

    _, _, h_re, h_im = lax.associative_scan(combine, (a_re_t, a_im_t, bu_re, bu_im), axis=1)
    y = jnp.einsum('gcp,btgp->btgc', c_re, h_re) - jnp.einsum('gcp,btgp->btgc', c_im, h_im)
    y = y.reshape(B, T, S5_CHANNELS) + d_skip * uf
    g = jax.nn.gelu(y)
    return (g * jax.nn.sigmoid(g @ w_glu + b_glu)).astype(u.dtype)


def shared_router(x2d, router_w, router_b):
    f32 = jnp.float32
    probs = jax.nn.softmax((x2d @ router_w).astype(f32), axis=-1)
    sel = probs + router_b.astype(f32)
    grp_score = jnp.sum(lax.top_k(sel.reshape(-1, N_GROUPS, EXPERTS_PER_GROUP), TOP_K)[0], axis=-1)
    g_best = jnp.argmax(grp_score, axis=-1)
    in_group = (jnp.arange(N_EXPERTS) // EXPERTS_PER_GROUP)[None, :] == g_best[:, None]
    _, idx = lax.top_k(jnp.where(in_group, sel, -jnp.inf), TOP_K)
    w = jnp.take_along_axis(probs, idx, axis=-1)
    w = w / jnp.sum(w, axis=-1, keepdims=True)
    return jnp.sum(jax.nn.one_hot(idx, N_EXPERTS, dtype=f32) * w[..., None], axis=1)


def moe_ffn(x, gates, w_gate, w_up, w_down):
    B, T, D = x.shape
    x2 = x.reshape(B * T, D)
    out = jnp.zeros((B * T, D), jnp.float32)
    for e in range(N_EXPERTS):
        h = jax.nn.silu(x2 @ w_gate[e]) * (x2 @ w_up[e])
        out = out + gates[:, e:e + 1] * (h @ w_down[e])
    return out.astype(x.dtype).reshape(B, T, D)


def even_mixer(x, w_in, lam_q1, lam_k1, lam_q2, lam_k2, subln_g, gla_w_gate, gla_b_gate, gla_norm_g, lambda_init):
    B, T, _ = x.shape
    h = x @ w_in
    qa, ka, va, qb, kb, vb, gb, rb = jnp.split(h, EVEN_SPLIT_POINTS, axis=-1)
    o_a = diff_attention(qa.reshape(B, T, DIFF_HEADS, 2, DIFF_HEAD_DIM),
                         ka.reshape(B, T, DIFF_HEADS, 2, DIFF_HEAD_DIM),
                         va.reshape(B, T, DIFF_HEADS, DIFF_V_DIM),
                         lam_q1, lam_k1, lam_q2, lam_k2, subln_g, lambda_init)
    log_a = jax.nn.log_sigmoid((gb @ gla_w_gate + gla_b_gate).astype(jnp.float32)) / GLA_GATE_TEMP
    o_b = gla_chunked(qb.reshape(B, T, GLA_HEADS, GLA_DK), kb.reshape(B, T, GLA_HEADS, GLA_DK),
                      vb.reshape(B, T, GLA_HEADS, GLA_DV), log_a.reshape(B, T, GLA_HEADS, GLA_DK))
    o_b = rms_norm(o_b, gla_norm_g).reshape(B, T, GLA_V) * jax.nn.silu(rb)
    return jnp.concatenate([o_a, o_b.astype(o_a.dtype)], axis=-1)


def odd_mixer(x, w_in, a_re, a_im, log_step, b_re, b_im, c_re, c_im, d_skip, w_glu, b_glu):
    B, T, _ = x.shape
    h = x @ w_in
    qc, kc, vc, ud = jnp.split(h, ODD_SPLIT_POINTS, axis=-1)
    shp = (B, T, SB_HEADS, SB_HEAD_DIM)
    o_c = stick_breaking_attention(qc.reshape(shp), kc.reshape(shp), vc.reshape(shp))
    o_d = s5_layer(ud, a_re, a_im, log_step, b_re, b_im, c_re, c_im, d_skip, w_glu, b_glu)
    return jnp.concatenate([o_c, o_d.astype(o_c.dtype)], axis=-1)


def setup_inputs(seed: int = 0) -> dict:
    key = jax.random.key(seed)
    keys = iter(jax.random.split(key, 64))
    f32 = jnp.float32

    def nrm(shape, scale):
        return jax.random.normal(next(keys), shape, f32) * scale

    def gain(n):
        return 1.0 + nrm((n,), 0.02)

    inp = {}
    inp["x"] = nrm((BATCH, SEQ, D_MODEL), 1.0)
    inp["router_w"] = nrm((D_MODEL, N_EXPERTS), D_MODEL ** -0.5)
    inp["router_b"] = nrm((N_EXPERTS,), 0.01)

    def moe_params(prefix):
        inp[prefix + "moe_w_gate"] = nrm((N_EXPERTS, D_MODEL, D_EXPERT), D_MODEL ** -0.5)
        inp[prefix + "moe_w_up"] = nrm((N_EXPERTS, D_MODEL, D_EXPERT), D_MODEL ** -0.5)
        inp[prefix + "moe_w_down"] = nrm((N_EXPERTS, D_EXPERT, D_MODEL), D_EXPERT ** -0.5 * DEEPNORM_BETA)
        inp[prefix + "ln2_g"] = gain(D_MODEL)
        inp[prefix + "ln2_b"] = nrm((D_MODEL,), 0.02)

    inp["l0_w_in"] = nrm((D_MODEL, EVEN_IN), D_MODEL ** -0.5)
    inp["l0_diff_lam_q1"] = nrm((DIFF_HEAD_DIM,), 0.1)
    inp["l0_diff_lam_k1"] = nrm((DIFF_HEAD_DIM,), 0.1)
    inp["l0_diff_lam_q2"] = nrm((DIFF_HEAD_DIM,), 0.1)
    inp["l0_diff_lam_k2"] = nrm((DIFF_HEAD_DIM,), 0.1)
    inp["l0_diff_subln_g"] = gain(DIFF_V_DIM)
    inp["l0_gla_w_gate"] = nrm((GLA_GATE_RANK, GLA_QK), GLA_GATE_RANK ** -0.5)
    inp["l0_gla_b_gate"] = nrm((GLA_QK,), 0.02)
    inp["l0_gla_norm_g"] = gain(GLA_DV)
    inp["l0_w_out"] = nrm((EVEN_MIX, D_MODEL), EVEN_MIX ** -0.5 * DEEPNORM_BETA)
    inp["l0_ln1_g"] = gain(D_MODEL)
    inp["l0_ln1_b"] = nrm((D_MODEL,), 0.02)
    moe_params("l0_")

    inp["l1_w_in"] = nrm((D_MODEL, ODD_IN), D_MODEL ** -0.5)
    inp["l1_s5_a_re"] = -0.5 + nrm((S5_GROUPS, S5_STATE), 0.01)
    inp["l1_s5_a_im"] = (math.pi * jnp.arange(S5_STATE, dtype=f32))[None, :] + nrm((S5_GROUPS, S5_STATE), 0.01)
    inp["l1_s5_log_step"] = jax.random.uniform(next(keys), (S5_GROUPS,), f32, math.log(0.001), math.log(0.1))
    inp["l1_s5_b_re"] = nrm((S5_GROUPS, S5_STATE, S5_GROUP), (2.0 * S5_GROUP) ** -0.5)
    inp["l1_s5_b_im"] = nrm((S5_GROUPS, S5_STATE, S5_GROUP), (2.0 * S5_GROUP) ** -0.5)
    inp["l1_s5_c_re"] = nrm((S5_GROUPS, S5_GROUP, S5_STATE), (2.0 * S5_STATE) ** -0.5)
    inp["l1_s5_c_im"] = nrm((S5_GROUPS, S5_GROUP, S5_STATE), (2.0 * S5_STATE) ** -0.5)
    inp["l1_s5_d"] = nrm((S5_CHANNELS,), 1.0)
    inp["l1_s5_w_glu"] = nrm((S5_CHANNELS, S5_CHANNELS), S5_CHANNELS ** -0.5)
    inp["l1_s5_b_glu"] = nrm((S5_CHANNELS,), 0.02)
    inp["l1_w_out"] = nrm((ODD_MIX, D_MODEL), ODD_MIX ** -0.5 * DEEPNORM_BETA)
    inp["l1_ln1_g"] = gain(D_MODEL)
    inp["l1_ln1_b"] = nrm((D_MODEL,), 0.02)
    moe_params("l1_")
    return inp


def reference(x, router_w, router_b,
              l0_w_in, l0_diff_lam_q1, l0_diff_lam_k1, l0_diff_lam_q2, l0_diff_lam_k2, l0_diff_subln_g,
              l0_gla_w_gate, l0_gla_b_gate, l0_gla_norm_g, l0_w_out, l0_ln1_g, l0_ln1_b,
              l0_moe_w_gate, l0_moe_w_up, l0_moe_w_down, l0_ln2_g, l0_ln2_b,
              l1_w_in, l1_s5_a_re, l1_s5_a_im, l1_s5_log_step, l1_s5_b_re, l1_s5_b_im,
              l1_s5_c_re, l1_s5_c_im, l1_s5_d, l1_s5_w_glu, l1_s5_b_glu, l1_w_out, l1_ln1_g, l1_ln1_b,
              l1_moe_w_gate, l1_moe_w_up, l1_moe_w_down, l1_ln2_g, l1_ln2_b):
    layers = (
        (l0_w_out, l0_ln1_g, l0_ln1_b, l0_moe_w_gate, l0_moe_w_up, l0_moe_w_down, l0_ln2_g, l0_ln2_b),
        (l1_w_out, l1_ln1_g, l1_ln1_b, l1_moe_w_gate, l1_moe_w_up, l1_moe_w_down, l1_ln2_g, l1_ln2_b),
    )
    for layer in range(DEPTH):
        w_out, ln1_g, ln1_b, w_gate, w_up, w_down, ln2_g, ln2_b = layers[layer]
        if layer % 2 == 0:
            lambda_init = 0.8 - 0.6 * math.exp(-0.3 * layer)
            mix = even_mixer(x, l0_w_in, l0_diff_lam_q1, l0_diff_lam_k1, l0_diff_lam_q2, l0_diff_lam_k2,
                             l0_diff_subln_g, l0_gla_w_gate, l0_gla_b_gate, l0_gla_norm_g, lambda_init)
        else:
            mix = odd_mixer(x, l1_w_in, l1_s5_a_re, l1_s5_a_im, l1_s5_log_step, l1_s5_b_re, l1_s5_b_im,
                            l1_s5_c_re, l1_s5_c_im, l1_s5_d, l1_s5_w_glu, l1_s5_b_glu)
        x = layer_norm(DEEPNORM_ALPHA * x + mix @ w_out, ln1_g, ln1_b)
        gates = shared_router(x.reshape(-1, D_MODEL), router_w, router_b)
        x = layer_norm(DEEPNORM_ALPHA * x + moe_ffn(x, gates, w_gate, w_up, w_down), ln2_g, ln2_b)
    return x
```

```python
import functools
import math

import jax
import jax.numpy as jnp
from jax import lax
from jax.experimental import pallas as pl
from jax.experimental.pallas import tpu as pltpu

F32 = jnp.float32
BF16 = jnp.bfloat16

D_MODEL = 2048
DEPTH = 2
DIFF_HEADS = 8
DIFF_HEAD_DIM = 64
DIFF_V_DIM = 128
GLA_HEADS = 4
GLA_DK = 128
GLA_DV = 256
GLA_GATE_RANK = 16
GLA_GATE_TEMP = 16.0
GLA_CHUNK = 64
SB_HEADS = 8
SB_HEAD_DIM = 128
S5_CHANNELS = 1024
S5_GROUP = 16
S5_GROUPS = 64
S5_STATE = 64
N_EXPERTS = 16
N_GROUPS = 4
EXPERTS_PER_GROUP = 4
D_EXPERT = 640
DEEPNORM_ALPHA = (2.0 * DEPTH) ** 0.25
LN_EPS = 1e-5

LANES = 128
SUBLANES = 8
VMEM_LIMIT = 56 * 1024 * 1024
SB_LOG_FLOOR = 120.0

S5_KT = 4
S5_KT_CH = S5_CHANNELS // S5_KT
S5_KT_STATES = (S5_GROUPS // S5_KT) * S5_STATE
S5_SCAN_W = 512


def _params(sem):
    return pltpu.CompilerParams(dimension_semantics=sem, vmem_limit_bytes=VMEM_LIMIT)


def _nt_dot(a, b):
    return lax.dot_general(a, b, (((1,), (1,)), ((), ())), preferred_element_type=F32)


def _dot(a, b):
    return jnp.dot(a, b, preferred_element_type=F32)


def _split_dot(x, m):
    hi = x.astype(BF16)
    lo = (x - hi.astype(F32)).astype(BF16)
    return _dot(hi, m) + _dot(lo, m)


def _split_dot_left(m, x):
    hi = x.astype(BF16)
    lo = (x - hi.astype(F32)).astype(BF16)
    return _dot(m, hi) + _dot(m, lo)


def _layer_norm(v, g, b):
    mu = jnp.mean(v, axis=-1, keepdims=True)
    d = v - mu
    var = jnp.mean(d * d, axis=-1, keepdims=True)
    return d * lax.rsqrt(var + LN_EPS) * g + b


def _mm_kernel(a_ref, b_ref, o_ref):
    o_ref[...] = _dot(a_ref[...], b_ref[...]).astype(o_ref.dtype)


def _matmul(a, b, tm, tn, out_dtype):
    m, k = a.shape
    n = b.shape[1]
    tm = min(tm, m)
    return pl.pallas_call(
        _mm_kernel,
        grid=(m // tm, n // tn),
        in_specs=[pl.BlockSpec((tm, k), lambda i, j: (i, 0)),
                  pl.BlockSpec((k, tn), lambda i, j: (0, j))],
        out_specs=pl.BlockSpec((tm, tn), lambda i, j: (i, j)),
        out_shape=jax.ShapeDtypeStruct((m, n), out_dtype),
        compiler_params=_params(("parallel", "arbitrary")),
        name="in_proj",
    )(a, b)


def _diff_attn_kernel(lam_ref, g_ref, q_ref, k_ref, v_ref, o_ref, m_ref, l_ref, acc_ref, *, tq, lambda_init):
    i = pl.program_id(1)
    j = pl.program_id(2)

    @pl.when(j == 0)
    def _():
        m_ref[...] = jnp.full(m_ref.shape, -jnp.inf, F32)
        l_ref[...] = jnp.zeros(l_ref.shape, F32)
        acc_ref[...] = jnp.zeros(acc_ref.shape, F32)

    def step(masked):
        q = q_ref[...]
        k = k_ref[...]
        v = v_ref[...]
        lane = lax.broadcasted_iota(jnp.int32, q.shape, 1)
        zero = jnp.zeros_like(q)
        halves = (jnp.where(lane < DIFF_HEAD_DIM, q, zero), jnp.where(lane >= DIFF_HEAD_DIM, q, zero))
        if masked:
            row = lax.broadcasted_iota(jnp.int32, (tq, tq), 0)
            col = lax.broadcasted_iota(jnp.int32, (tq, tq), 1)
            causal = col <= row
        for mi, qm in enumerate(halves):
            s = _nt_dot(qm, k) * (DIFF_HEAD_DIM ** -0.5)
            if masked:
                s = jnp.where(causal, s, -jnp.inf)
            m_prev = m_ref[mi]
            m_new = jnp.maximum(m_prev, jnp.max(s, axis=1, keepdims=True))
            p = jnp.exp(s - m_new)
            alpha = jnp.exp(m_prev - m_new)
            l_ref[mi] = alpha * l_ref[mi] + jnp.sum(p, axis=1, keepdims=True)
            acc_ref[mi] = alpha * acc_ref[mi] + _dot(p.astype(BF16), v)
            m_ref[mi] = m_new

    @pl.when(j < i)
    def _():
        step(False)

    @pl.when(j == i)
    def _():
        step(True)
        lv = lam_ref[...]
        lam = (jnp.exp(jnp.sum(lv[0:1] * lv[1:2], axis=1, keepdims=True))
               - jnp.exp(jnp.sum(lv[2:3] * lv[3:4], axis=1, keepdims=True)) + lambda_init)
        o = acc_ref[0] / l_ref[0] - lam * (acc_ref[1] / l_ref[1])
        ms = jnp.mean(o * o, axis=-1, keepdims=True)
        o = o * lax.rsqrt(ms + LN_EPS) * g_ref[...] * (1.0 - lambda_init)
        o_ref[...] = o.astype(o_ref.dtype)


def _diff_attention(h0, lam_vec, subln_g, lambda_init, tq):
    t = h0.shape[0]
    tq = min(tq, t)
    nq = t // tq
    kern = functools.partial(_diff_attn_kernel, tq=tq, lambda_init=lambda_init)
    return pl.pallas_call(
        kern,
        grid=(DIFF_HEADS, nq, nq),
        in_specs=[
            pl.BlockSpec((4, DIFF_HEAD_DIM), lambda h, i, j: (0, 0)),
            pl.BlockSpec((1, DIFF_V_DIM), lambda h, i, j: (0, 0)),
            pl.BlockSpec((tq, LANES), lambda h, i, j: (i, h)),
            pl.BlockSpec((tq, LANES), lambda h, i, j: (jnp.minimum(j, i), DIFF_HEADS + h)),
            pl.BlockSpec((tq, LANES), lambda h, i, j: (jnp.minimum(j, i), 2 * DIFF_HEADS + h)),
        ],
        out_specs=pl.BlockSpec((tq, LANES), lambda h, i, j: (i, h)),
        out_shape=jax.ShapeDtypeStruct((t, DIFF_HEADS * DIFF_V_DIM), BF16),
        scratch_shapes=[pltpu.VMEM((2, tq, 1), F32), pltpu.VMEM((2, tq, 1), F32),
                        pltpu.VMEM((2, tq, DIFF_V_DIM), F32)],
        compiler_params=_params(("parallel", "parallel", "arbitrary")),
        name="diff_attn",
    )(lam_vec, subln_g, h0, h0, h0)


def _gla_kernel(q_ref, k_ref, v_ref, r_ref, gb_ref, wg_ref, bg_ref, ng_ref, o_ref, st_ref, *, tg):
    @pl.when(pl.program_id(0) == 0)
    def _():
        st_ref[...] = jnp.zeros(st_ref.shape, F32)

    c = GLA_CHUNK
    gate = _dot(gb_ref[...], wg_ref[...]) + bg_ref[...]
    log_a = -(jnp.maximum(-gate, 0.0) + jnp.log(1.0 + jnp.exp(-jnp.abs(gate)))) / GLA_GATE_TEMP
    row = lax.broadcasted_iota(jnp.int32, (c, c), 0)
    col = lax.broadcasted_iota(jnp.int32, (c, c), 1)
    tril = row >= col
    tri = jnp.where(tril, 1.0, 0.0).astype(BF16)
    for h in range(GLA_HEADS):
        ksl = slice(h * GLA_DK, (h + 1) * GLA_DK)
        vsl = slice(h * GLA_DV, (h + 1) * GLA_DV)
        for ci in range(tg // c):
            rs = slice(ci * c, (ci + 1) * c)
            la = log_a[rs, ksl]
            b = _split_dot_left(tri, la)
            b_last = b[c - 1:c, :]
            qf = q_ref[rs, ksl].astype(F32) * (GLA_DK ** -0.5)
            kf = k_ref[rs, ksl].astype(F32)
            v = v_ref[rs, vsl]
            q_dec = (qf * jnp.exp(b)).astype(BF16)
            k_dec = (kf * jnp.exp(-b)).astype(BF16)
            k_end = (kf * jnp.exp(b_last - b)).astype(BF16)
            scores = jnp.where(tril, _nt_dot(q_dec, k_dec), 0.0)
            st = st_ref[h]
            o = _dot(scores.astype(BF16), v) + _nt_dot(q_dec, st.astype(BF16))
            kv_t = lax.dot_general(v, k_end, (((0,), (0,)), ((), ())), preferred_element_type=F32)
            st_ref[h] = jnp.exp(b_last) * st + kv_t
            ms = jnp.mean(o * o, axis=-1, keepdims=True)
            o = o * lax.rsqrt(ms + LN_EPS) * ng_ref[...]
            r = r_ref[rs, vsl].astype(F32)
            o_ref[rs, vsl] = (o * (r / (1.0 + jnp.exp(-r)))).astype(o_ref.dtype)


def _gla(h0, wg_pad, b_gate, norm_g, tg):
    t = h0.shape[0]
    tg = min(tg, t)
    qk = GLA_HEADS * GLA_DK
    vw = GLA_HEADS * GLA_DV
    return pl.pallas_call(
        functools.partial(_gla_kernel, tg=tg),
        grid=(t // tg,),
        in_specs=[
            pl.BlockSpec((tg, qk), lambda i: (i, 3072 // qk)),
            pl.BlockSpec((tg, qk), lambda i: (i, 3584 // qk)),
            pl.BlockSpec((tg, vw), lambda i: (i, 4096 // vw)),
            pl.BlockSpec((tg, vw), lambda i: (i, 5120 // vw)),
            pl.BlockSpec((tg, LANES), lambda i: (i, 6144 // LANES)),
            pl.BlockSpec((LANES, qk), lambda i: (0, 0)),
            pl.BlockSpec((1, qk), lambda i: (0, 0)),
            pl.BlockSpec((1, GLA_DV), lambda i: (0, 0)),
        ],
        out_specs=pl.BlockSpec((tg, vw), lambda i: (i, 0)),
        out_shape=jax.ShapeDtypeStruct((t, vw), BF16),
        scratch_shapes=[pltpu.VMEM((GLA_HEADS, GLA_DV, GLA_DK), F32)],
        compiler_params=_params(("arbitrary",)),
        name="gla",
    )(h0, h0, h0, h0, h0, wg_pad, b_gate, norm_g)


def _sb_kernel(q_ref, k_ref, v_ref, o_ref, acc_ref, c_ref, *, tq):
    i = pl.program_id(1)
    acc_ref[...] = jnp.zeros(acc_ref.shape, F32)
    c_ref[...] = jnp.zeros(c_ref.shape, F32)
    q = q_ref[...]
    row = lax.broadcasted_iota(jnp.int32, (tq, tq), 0)
    col = lax.broadcasted_iota(jnp.int32, (tq, tq), 1)
    strict = col < row
    tri = jnp.where(row > col, 1.0, 0.0).astype(BF16)

    def block(jb, masked):
        off = pl.multiple_of(jb * tq, tq)
        k = k_ref[pl.ds(off, tq), :]
        v = v_ref[pl.ds(off, tq), :]
        z = _nt_dot(q, k) * (SB_HEAD_DIM ** -0.5)
        lsn = -(jnp.maximum(z, 0.0) + jnp.log(1.0 + jnp.exp(-jnp.abs(z))))
        lm = jnp.where(strict, lsn, 0.0) if masked else lsn
        excl = _split_dot(lm, tri) + c_ref[...]
        w = jnp.exp(z + lsn + excl)
        if masked:
            w = jnp.where(strict, w, 0.0)
        acc_ref[...] += _dot(w.astype(BF16), v)
        c_new = c_ref[...] + jnp.sum(lm, axis=1, keepdims=True)
        c_ref[...] = c_new
        return jnp.max(c_new)

    cmax = block(i, True)

    def cond(carry):
        jb, cm = carry
        return jnp.logical_and(jb >= 0, cm > -SB_LOG_FLOOR)

    def body(carry):
        jb, _ = carry
        return jb - 1, block(jb, False)

    lax.while_loop(cond, body, (i - 1, cmax))
    o_ref[...] = acc_ref[...].astype(o_ref.dtype)


def _stick_breaking(h1, tq):
    t = h1.shape[0]
    tq = min(tq, t)
    return pl.pallas_call(
        functools.partial(_sb_kernel, tq=tq),
        grid=(SB_HEADS, t // tq),
        in_specs=[
            pl.BlockSpec((tq, LANES), lambda h, i: (i, h)),
            pl.BlockSpec((t, LANES), lambda h, i: (0, SB_HEADS + h)),
            pl.BlockSpec((t, LANES), lambda h, i: (0, 2 * SB_HEADS + h)),
        ],
        out_specs=pl.BlockSpec((tq, LANES), lambda h, i: (i, h)),
        out_shape=jax.ShapeDtypeStruct((t, SB_HEADS * SB_HEAD_DIM), BF16),
        scratch_shapes=[pltpu.VMEM((tq, SB_HEAD_DIM), F32), pltpu.VMEM((tq, 1), F32)],
        compiler_params=_params(("parallel", "arbitrary")),
        name="stick_breaking",
    )(h1, h1, h1)


def _s5_prep_kernel(are_ref, aim_ref, ls_ref, bre_ref, bim_ref, lbr_ref, lbi_ref, bbr_ref, bbi_ref):
    step = jnp.exp(ls_ref[...])
    lam_re = are_ref[...]
    lam_im = aim_ref[...]
    mag = jnp.exp(step * lam_re)
    lb_re = mag * jnp.cos(step * lam_im)
    lb_im = mag * jnp.sin(step * lam_im)
    den = lam_re * lam_re + lam_im * lam_im
    n_re = lb_re - 1.0
    f_re = (n_re * lam_re + lb_im * lam_im) / den
    f_im = (lb_im * lam_re - n_re * lam_im) / den
    lbr_ref[...] = lb_re
    lbi_ref[...] = lb_im
    bbr_ref[...] = f_re * bre_ref[...] - f_im * bim_ref[...]
    bbi_ref[...] = f_re * bim_ref[...] + f_im * bre_ref[...]


def _s5_prep(a_re, a_im, log_step, b_re, b_im):
    rows = S5_GROUPS * S5_GROUP
    rep = lambda p: jnp.repeat(p, S5_GROUP, axis=0)
    to_rows = lambda b: jnp.transpose(b, (0, 2, 1)).reshape(rows, S5_STATE)
    spec = pl.BlockSpec((rows, S5_STATE), lambda: (0, 0))
    shp = jax.ShapeDtypeStruct((rows, S5_STATE), F32)
    return pl.pallas_call(
        _s5_prep_kernel,
        in_specs=[spec, spec, pl.BlockSpec((rows, 1), lambda: (0, 0)), spec, spec],
        out_specs=[spec] * 4,
        out_shape=[shp] * 4,
        name="s5_prep",
    )(rep(a_re), rep(a_im), rep(log_step[:, None]), to_rows(b_re), to_rows(b_im))


def _cmul(ar, ai, br, bi):
    return ar * br - ai * bi, ar * bi + ai * br


def _s5_kernel(u_ref, wbu_ref, wc_ref, lam_ref, d_ref, wglu_ref, bglu_ref, o_ref,
               hs_ref, hp_ref, pw_ref, st_ref, *, tc):
    ns = S5_KT_STATES

    @pl.when(pl.program_id(0) == 0)
    def _():
        hp_ref[...] = jnp.zeros(hp_ref.shape, F32)
        l1r = jnp.broadcast_to(lam_ref[0:1, :], (SUBLANES, lam_ref.shape[1]))
        l1i = jnp.broadcast_to(lam_ref[1:2, :], (SUBLANES, lam_ref.shape[1]))
        l2r, l2i = _cmul(l1r, l1i, l1r, l1i)
        l4r, l4i = _cmul(l2r, l2i, l2r, l2i)
        l8r, l8i = _cmul(l4r, l4i, l4r, l4i)
        ridx = lax.broadcasted_iota(jnp.int32, l1r.shape, 0)
        n = ridx + 1
        pr = jnp.ones_like(l1r)
        pi = jnp.zeros_like(l1r)
        for bit, (lr, li) in ((1, (l1r, l1i)), (2, (l2r, l2i)), (4, (l4r, l4i)), (8, (l8r, l8i))):
            nr, ni = _cmul(pr, pi, lr, li)
            take = (n & bit) != 0
            pr = jnp.where(take, nr, pr)
            pi = jnp.where(take, ni, pi)
        pw_ref[0] = pr
        pw_ref[1] = pi
        for s, (lr, li) in enumerate(((l1r, l1i), (l2r, l2i), (l4r, l4i))):
            keep = ridx >= (1 << s)
            st_ref[2 * s] = jnp.where(keep, lr, 0.0)
            st_ref[2 * s + 1] = jnp.where(keep, li, 0.0)

    u = u_ref[...]
    for kt in range(S5_KT):
        hs_ref[:, kt * 2 * ns:(kt + 1) * 2 * ns] = _dot(u[:, kt * S5_KT_CH:(kt + 1) * S5_KT_CH], wbu_ref[kt])

    w = S5_SCAN_W
    for kt in range(S5_KT):
        for cc in range(ns // w):
            re0 = kt * 2 * ns + cc * w
            im0 = re0 + ns
            l0 = kt * ns + cc * w

            def body(r, carry, re0=re0, im0=im0, l0=l0):
                hpr, hpi = carry
                rows = pl.ds(pl.multiple_of(r * SUBLANES, SUBLANES), SUBLANES)
                xr = hs_ref[rows, re0:re0 + w]
                xi = hs_ref[rows, im0:im0 + w]
                for s in range(3):
                    sr = pltpu.roll(xr, 1 << s, 0)
                    si = pltpu.roll(xi, 1 << s, 0)
                    ar = st_ref[2 * s, :, l0:l0 + w]
                    ai = st_ref[2 * s + 1, :, l0:l0 + w]
                    xr, xi = xr + ar * sr - ai * si, xi + ar * si + ai * sr
                pr = pw_ref[0, :, l0:l0 + w]
                pi = pw_ref[1, :, l0:l0 + w]
                xr, xi = xr + pr * hpr - pi * hpi, xi + pr * hpi + pi * hpr
                hs_ref[rows, re0:re0 + w] = xr
                hs_ref[rows, im0:im0 + w] = xi
                return (jnp.broadcast_to(xr[SUBLANES - 1:SUBLANES, :], (SUBLANES, w)),
                        jnp.broadcast_to(xi[SUBLANES - 1:SUBLANES, :], (SUBLANES, w)))

            hpr, hpi = lax.fori_loop(0, tc // SUBLANES, body,
                                     (hp_ref[:, re0:re0 + w], hp_ref[:, im0:im0 + w]))
            hp_ref[:, re0:re0 + w] = hpr
            hp_ref[:, im0:im0 + w] = hpi

    ys = [_dot(hs_ref[:, kt * 2 * ns:(kt + 1) * 2 * ns].astype(BF16), wc_ref[kt]) for kt in range(S5_KT)]
    y = jnp.concatenate(ys, axis=1) + d_ref[...] * u.astype(F32)
    g = 0.5 * y * (1.0 + jnp.tanh(math.sqrt(2.0 / math.pi) * (y + 0.044715 * (y * y * y))))
    gl = _dot(g.astype(BF16), wglu_ref[...]) + bglu_ref[...]
    o_ref[...] = (g / (1.0 + jnp.exp(-gl))).astype(o_ref.dtype)


def _s5(h1, wbu, wc, lam, d_skip, w_glu, b_glu, tc):
    t = h1.shape[0]
    tc = min(tc, t)
    ns = S5_KT_STATES
    nstate = S5_GROUPS * S5_STATE
    full = lambda shape: pl.BlockSpec(shape, lambda i: (0,) * len(shape))
    return pl.pallas_call(
        functools.partial(_s5_kernel, tc=tc),
        grid=(t // tc,),
        in_specs=[
            pl.BlockSpec((tc, S5_CHANNELS), lambda i: (i, 3)),
            full((S5_KT, S5_KT_CH, 2 * ns)),
            full((S5_KT, 2 * ns, S5_KT_CH)),
            full((2, nstate)),
            full((1, S5_CHANNELS)),
            full((S5_CHANNELS, S5_CHANNELS)),
            full((1, S5_CHANNELS)),
        ],
        out_specs=pl.BlockSpec((tc, S5_CHANNELS), lambda i: (i, 0)),
        out_shape=jax.ShapeDtypeStruct((t, S5_CHANNELS), BF16),
        scratch_shapes=[pltpu.VMEM((tc, 2 * nstate), F32), pltpu.VMEM((SUBLANES, 2 * nstate), F32),
                        pltpu.VMEM((2, SUBLANES, nstate), F32), pltpu.VMEM((6, SUBLANES, nstate), F32)],
        compiler_params=_params(("arbitrary",)),
        name="s5",
    )(h1, wbu, wc, lam, d_skip, w_glu, b_glu)


def _route(probs, sel):
    rows = [sel[e:e + 1, :] for e in range(N_EXPERTS)]
    prow = [probs[e:e + 1, :] for e in range(N_EXPERTS)]
    best_score = None
    for g in range(N_GROUPS):
        a = rows[g * EXPERTS_PER_GROUP:(g + 1) * EXPERTS_PER_GROUP]
        score = None
        for x in range(EXPERTS_PER_GROUP):
            for y in range(x + 1, EXPERTS_PER_GROUP):
                pair = a[x] + a[y]
                score = pair if score is None else jnp.maximum(score, pair)
        if best_score is None:
            best_score, gbest = score, jnp.zeros(score.shape, jnp.int32)
            cs = list(a)
            cp = prow[0:EXPERTS_PER_GROUP]
        else:
            better = score > best_score
            best_score = jnp.where(better, score, best_score)
            gbest = jnp.where(better, g, gbest)
            cs = [jnp.where(better, a[x], cs[x]) for x in range(EXPERTS_PER_GROUP)]
            cp = [jnp.where(better, prow[g * EXPERTS_PER_GROUP + x], cp[x]) for x in range(EXPERTS_PER_GROUP)]

    def first_argmax(vals):
        bv, bi = vals[0], jnp.zeros(vals[0].shape, jnp.int32)
        for x in range(1, len(vals)):
            better = vals[x] > bv
            bv = jnp.where(better, vals[x], bv)
            bi = jnp.where(better, x, bi)
        return bi

    i1 = first_argmax(cs)
    i2 = first_argmax([jnp.where(i1 == x, -jnp.inf, cs[x]) for x in range(EXPERTS_PER_GROUP)])

    def pick(idx):
        out = cp[0]
        for x in range(1, EXPERTS_PER_GROUP):
            out = jnp.where(idx == x, cp[x], out)
        return out

    p1, p2 = pick(i1), pick(i2)
    tot = p1 + p2
    return gbest * EXPERTS_PER_GROUP + i1, gbest * EXPERTS_PER_GROUP + i2, p1 / tot, p2 / tot


def _ln_router_tail(v, g_ref, b_ref, rwt_ref, rb_ref, x_ref, xb_ref, ei_ref, ew_ref):
    xn = _layer_norm(v, g_ref[...], b_ref[...])
    x_ref[...] = xn
    xb_ref[...] = xn.astype(BF16)
    logits = lax.dot_general(rwt_ref[...], xn, (((1,), (1,)), ((), ())), preferred_element_type=F32,
                             precision=lax.Precision.HIGHEST)
    mx = jnp.max(logits, axis=0, keepdims=True)
    ex = jnp.exp(logits - mx)
    probs = ex / jnp.sum(ex, axis=0, keepdims=True)
    e1, e2, w1, w2 = _route(probs, probs + rb_ref[...])
    n = e1.shape[1]
    ei_ref[...] = jnp.concatenate([e1, e2, jnp.zeros((SUBLANES - 2, n), jnp.int32)], axis=0)
    ew_ref[...] = jnp.concatenate([w1, w2, jnp.zeros((SUBLANES - 2, n), F32)], axis=0)


def _outproj_kernel(ma_ref, mb_ref, wa_ref, wb_ref, xr_ref, g_ref, b_ref, rwt_ref, rb_ref,
                    x_ref, xb_ref, ei_ref, ew_ref):
    v = DEEPNORM_ALPHA * xr_ref[...] + _dot(ma_ref[...], wa_ref[...]) + _dot(mb_ref[...], wb_ref[...])
    _ln_router_tail(v, g_ref, b_ref, rwt_ref, rb_ref, x_ref, xb_ref, ei_ref, ew_ref)


def _outproj_ln_router(mix_a, mix_b, w_out, x_res, ln_g, ln_b, rwt, rb, tm):
    t = x_res.shape[0]
    tm = min(tm, t)
    half = w_out.shape[0] // 2
    row = lambda w: pl.BlockSpec((tm, w), lambda i: (i, 0))
    full = lambda shape: pl.BlockSpec(shape, lambda i: (0,) * len(shape))
    return pl.pallas_call(
        _outproj_kernel,
        grid=(t // tm,),
        in_specs=[row(half), row(half),
                  pl.BlockSpec((half, D_MODEL), lambda i: (0, 0)), pl.BlockSpec((half, D_MODEL), lambda i: (1, 0)),
                  row(D_MODEL), full((1, D_MODEL)), full((1, D_MODEL)),
                  full((N_EXPERTS, D_MODEL)), full((N_EXPERTS, 1))],
        out_specs=[row(D_MODEL), row(D_MODEL),
                   pl.BlockSpec((SUBLANES, tm), lambda i: (0, i)), pl.BlockSpec((SUBLANES, tm), lambda i: (0, i))],
        out_shape=[jax.ShapeDtypeStruct((t, D_MODEL), F32), jax.ShapeDtypeStruct((t, D_MODEL), BF16),
                   jax.ShapeDtypeStruct((SUBLANES, t), jnp.int32), jax.ShapeDtypeStruct((SUBLANES, t), F32)],
        compiler_params=_params(("parallel",)),
        name="outproj_ln_router",
    )(mix_a, mix_b, w_out, w_out, x_res, ln_g, ln_b, rwt, rb)


def _moe_kernel(te_ref, nu_ref, xs_ref, wgu_ref, wd_ref, rw_ref, y_ref):
    tile = pl.program_id(0)

    @pl.when(tile < nu_ref[0])
    def _():
        hgu = _dot(xs_ref[...], wgu_ref[...])
        hg = hgu[:, :D_EXPERT]
        hu = hgu[:, D_EXPERT:]
        h = (hg / (1.0 + jnp.exp(-hg))) * hu
        y_ref[...] = rw_ref[:, 0:1] * _dot(h.astype(BF16), wd_ref[...])

    @pl.when(tile >= nu_ref[0])
    def _():
        y_ref[...] = jnp.zeros(y_ref.shape, F32)


def _moe(tile_expert, n_used, xs, wgu, wd, roww, tm):
    t_pad = xs.shape[0]
    grid_spec = pltpu.PrefetchScalarGridSpec(
        num_scalar_prefetch=2,
        grid=(t_pad // tm,),
        in_specs=[
            pl.BlockSpec((tm, D_MODEL), lambda i, te, nu: (i, 0)),
            pl.BlockSpec((None, D_MODEL, 2 * D_EXPERT), lambda i, te, nu: (te[i], 0, 0)),
            pl.BlockSpec((None, D_EXPERT, D_MODEL), lambda i, te, nu: (te[i], 0, 0)),
            pl.BlockSpec((tm, LANES), lambda i, te, nu: (i, 0)),
        ],
        out_specs=pl.BlockSpec((tm, D_MODEL), lambda i, te, nu: (i, 0)),
    )
    return pl.pallas_call(
        _moe_kernel,
        grid_spec=grid_spec,
        out_shape=jax.ShapeDtypeStruct((t_pad, D_MODEL), F32),
        compiler_params=_params(("arbitrary",)),
        name="moe",
    )(tile_expert, n_used, xs, wgu, wd, roww)


def _moe_ln_kernel(x_ref, y1_ref, y2_ref, g_ref, b_ref, o_ref, ob_ref):
    v = DEEPNORM_ALPHA * x_ref[...] + (y1_ref[...] + y2_ref[...])
    xn = _layer_norm(v, g_ref[...], b_ref[...])
    o_ref[...] = xn
    ob_ref[...] = xn.astype(BF16)


def _moe_ln(x, y1, y2, ln_g, ln_b, tm):
    t = x.shape[0]
    tm = min(tm, t)
    row = pl.BlockSpec((tm, D_MODEL), lambda i: (i, 0))
    vec = pl.BlockSpec((1, D_MODEL), lambda i: (0, 0))
    return pl.pallas_call(
        _moe_ln_kernel,
        grid=(t // tm,),
        in_specs=[row, row, row, vec, vec],
        out_specs=[row, row],
        out_shape=[jax.ShapeDtypeStruct((t, D_MODEL), F32), jax.ShapeDtypeStruct((t, D_MODEL), BF16)],
        compiler_params=_params(("parallel",)),
        name="moe_ln",
    )(x, y1, y2, ln_g, ln_b)


def _moe_block(x, xb, eidx, ew, wgu, wd, ln_g, ln_b, tm):
    t = x.shape[0]
    tm = min(tm, t)
    e_flat = eidx[:2].reshape(2 * t)
    w_flat = ew[:2].reshape(2 * t)
    tok = jnp.tile(jnp.arange(t, dtype=jnp.int32), 2)
    onehot = (e_flat[:, None] == jnp.arange(N_EXPERTS, dtype=jnp.int32)[None, :]).astype(jnp.int32)
    rank = jnp.take_along_axis(jnp.cumsum(onehot, axis=0), e_flat[:, None], axis=1)[:, 0] - 1
    counts = jnp.sum(onehot, axis=0)
    padded = ((counts + tm - 1) // tm) * tm
    ends = jnp.cumsum(padded)
    starts = ends - padded
    dest = starts[e_flat] + rank
    t_pad = 2 * t + N_EXPERTS * tm
    src_tok = jnp.zeros((t_pad,), jnp.int32).at[dest].set(tok)
    roww = jnp.zeros((t_pad,), F32).at[dest].set(w_flat)
    tile_start = jnp.arange(t_pad // tm, dtype=jnp.int32) * tm
    tile_expert = jnp.minimum(jnp.searchsorted(ends, tile_start, side="right"), N_EXPERTS - 1).astype(jnp.int32)
    n_used = (ends[-1] // tm).astype(jnp.int32).reshape(1)
    xs = jnp.take(xb, src_tok, axis=0)
    ys = _moe(tile_expert, n_used, xs, wgu, wd, jnp.broadcast_to(roww[:, None], (t_pad, LANES)), tm)
    y1 = jnp.take(ys, dest[:t], axis=0)
    y2 = jnp.take(ys, dest[t:], axis=0)
    return _moe_ln(x, y1, y2, ln_g, ln_b, tm)


def _block_diag_tiles(m, rows_per_group, cols_per_group):
    gpt = S5_GROUPS // S5_KT
    m = m.reshape(S5_KT, gpt, rows_per_group, cols_per_group)
    eye = jnp.eye(gpt, dtype=m.dtype)
    bd = jnp.einsum("kgrc,gh->kgrhc", m, eye)
    return bd.reshape(S5_KT, gpt * rows_per_group, gpt * cols_per_group)


def kernel(x, router_w, router_b, l0_w_in, l0_diff_lam_q1, l0_diff_lam_k1, l0_diff_lam_q2, l0_diff_lam_k2, l0_diff_subln_g, l0_gla_w_gate, l0_gla_b_gate, l0_gla_norm_g, l0_w_out, l0_ln1_g, l0_ln1_b, l0_moe_w_gate, l0_moe_w_up, l0_moe_w_down, l0_ln2_g, l0_ln2_b, l1_w_in, l1_s5_a_re, l1_s5_a_im, l1_s5_log_step, l1_s5_b_re, l1_s5_b_im, l1_s5_c_re, l1_s5_c_im, l1_s5_d, l1_s5_w_glu, l1_s5_b_glu, l1_w_out, l1_ln1_g, l1_ln1_b, l1_moe_w_gate, l1_moe_w_up, l1_moe_w_down, l1_ln2_g, l1_ln2_b):
    bsz, t, d = x.shape
    assert bsz == 1 and d == D_MODEL
    x0 = x.reshape(t, d)
    vec = lambda p: p.reshape(1, -1).astype(F32)
    rwt = router_w.T.astype(F32)
    rb = router_b.reshape(N_EXPERTS, 1).astype(F32)

    lambda_init = 0.8 - 0.6 * math.exp(-0.3 * 0)
    c = [0, 1024, 2048, 3072, 3584, 4096, 5120, 5136, 6160]
    w0 = jnp.concatenate([l0_w_in[:, :c[6]], l0_w_in[:, c[7]:c[8]], l0_w_in[:, c[6]:c[7]],
                          jnp.zeros((d, LANES - GLA_GATE_RANK), l0_w_in.dtype)], axis=1).astype(BF16)
    h0 = _matmul(x0.astype(BF16), w0, 1024, 896, BF16)
    lam_vec = jnp.stack([l0_diff_lam_q1, l0_diff_lam_k1, l0_diff_lam_q2, l0_diff_lam_k2]).astype(F32)
    o_a = _diff_attention(h0, lam_vec, vec(l0_diff_subln_g), lambda_init, 512)
    wg_pad = jnp.concatenate([l0_gla_w_gate, jnp.zeros((LANES - GLA_GATE_RANK, GLA_HEADS * GLA_DK), F32)],
                             axis=0).astype(BF16)
    o_b = _gla(h0, wg_pad, vec(l0_gla_b_gate), vec(l0_gla_norm_g), 256)
    x1, x1b, eidx, ew = _outproj_ln_router(o_a, o_b, l0_w_out.astype(BF16), x0, vec(l0_ln1_g), vec(l0_ln1_b),
                                           rwt, rb, 256)
    wgu0 = jnp.concatenate([l0_moe_w_gate, l0_moe_w_up], axis=2).astype(BF16)
    x2, x2b = _moe_block(x1, x1b, eidx, ew, wgu0, l0_moe_w_down.astype(BF16), vec(l0_ln2_g), vec(l0_ln2_b), 256)

    h1 = _matmul(x2b, l1_w_in.astype(BF16), 1024, 1024, BF16)
    o_c = _stick_breaking(h1, 256)
    lbr, lbi, bbr, bbi = _s5_prep(l1_s5_a_re, l1_s5_a_im, l1_s5_log_step, l1_s5_b_re, l1_s5_b_im)
    lam = jnp.stack([lbr[::S5_GROUP].reshape(-1), lbi[::S5_GROUP].reshape(-1)])
    bbr = bbr.reshape(S5_GROUPS, S5_GROUP, S5_STATE)
    bbi = bbi.reshape(S5_GROUPS, S5_GROUP, S5_STATE)
    wbu = jnp.concatenate([_block_diag_tiles(bbr, S5_GROUP, S5_STATE),
                           _block_diag_tiles(bbi, S5_GROUP, S5_STATE)], axis=2).astype(BF16)
    c_re_t = jnp.transpose(l1_s5_c_re, (0, 2, 1))
    c_im_t = jnp.transpose(l1_s5_c_im, (0, 2, 1))
    wc = jnp.concatenate([_block_diag_tiles(c_re_t, S5_STATE, S5_GROUP),
                          _block_diag_tiles(-c_im_t, S5_STATE, S5_GROUP)], axis=1).astype(BF16)
    o_d = _s5(h1, wbu, wc, lam, vec(l1_s5_d), l1_s5_w_glu.astype(BF16), vec(l1_s5_b_glu), 256)
    x3, x3b, eidx1, ew1 = _outproj_ln_router(o_c, o_d, l1_w_out.astype(BF16), x2, vec(l1_ln1_g), vec(l1_ln1_b),
                                             rwt, rb, 256)
    wgu1 = jnp.concatenate([l1_moe_w_gate, l1_moe_w_up], axis=2).astype(BF16)
    x4, _ = _moe_block(x3, x3b, eidx1, ew1, wgu1, l1_moe_w_down.astype(BF16), vec(l1_ln2_g), vec(l1_ln2_b), 256)
    return x4.reshape(bsz, t, d)
```

```python
import functools
import math

import jax
import jax.numpy as jnp
from jax import lax
from jax.experimental import pallas as pl
from jax.experimental.pallas import tpu as pltpu

F32 = jnp.float32
BF16 = jnp.bfloat16

D_MODEL = 2048
DEPTH = 2
DIFF_HEADS = 8
DIFF_HEAD_DIM = 64
DIFF_V_DIM = 128
GLA_HEADS = 4
GLA_DK = 128
GLA_DV = 256
GLA_GATE_RANK = 16
GLA_GATE_TEMP = 16.0
GLA_CHUNK = 64
SB_HEADS = 8
SB_HEAD_DIM = 128
S5_CHANNELS = 1024
S5_GROUP = 16
S5_GROUPS = 64
S5_STATE = 64
N_EXPERTS = 16
N_GROUPS = 4
EXPERTS_PER_GROUP = 4
D_EXPERT = 640
DEEPNORM_ALPHA = (2.0 * DEPTH) ** 0.25
LN_EPS = 1e-5

LANES = 128
SUBLANES = 8
VMEM_LIMIT = 56 * 1024 * 1024
SB_LOG_FLOOR = 120.0

S5_KT = 4
S5_KT_CH = S5_CHANNELS // S5_KT
S5_KT_STATES = (S5_GROUPS // S5_KT) * S5_STATE
S5_SCAN_W = 512


def _params(sem):
    return pltpu.CompilerParams(dimension_semantics=sem, vmem_limit_bytes=VMEM_LIMIT)


def _nt_dot(a, b):
    return lax.dot_general(a, b, (((1,), (1,)), ((), ())), preferred_element_type=F32)


def _dot(a, b):
    return jnp.dot(a, b, preferred_element_type=F32)


def _split_dot(x, m):
    hi = x.astype(BF16)
    lo = (x - hi.astype(F32)).astype(BF16)
    return _dot(hi, m) + _dot(lo, m)


def _split_dot_left(m, x):
    hi = x.astype(BF16)
    lo = (x - hi.astype(F32)).astype(BF16)
    return _dot(m, hi) + _dot(m, lo)


def _layer_norm(v, g, b):
    mu = jnp.mean(v, axis=-1, keepdims=True)
    d = v - mu
    var = jnp.mean(d * d, axis=-1, keepdims=True)
    return d * lax.rsqrt(var + LN_EPS) * g + b


def _mm_kernel(a_ref, b_ref, o_ref):
    o_ref[...] = _dot(a_ref[...], b_ref[...]).astype(o_ref.dtype)


def _matmul(a, b, tm, tn, out_dtype):
    m, k = a.shape
    n = b.shape[1]
    tm = min(tm, m)
    return pl.pallas_call(
        _mm_kernel,
        grid=(m // tm, n // tn),
        in_specs=[pl.BlockSpec((tm, k), lambda i, j: (i, 0)),
                  pl.BlockSpec((k, tn), lambda i, j: (0, j))],
        out_specs=pl.BlockSpec((tm, tn), lambda i, j: (i, j)),
        out_shape=jax.ShapeDtypeStruct((m, n), out_dtype),
        compiler_params=_params(("parallel", "arbitrary")),
        name="in_proj",
    )(a, b)


def _diff_attn_kernel(lam_ref, g_ref, q_ref, k_ref, v_ref, o_ref, vx_ref, m_ref, acc_ref, *, tq, lambda_init):
    i = pl.program_id(1)
    dv = DIFF_V_DIM

    @pl.when(i == 0)
    def _():
        vx_ref[:, :dv] = v_ref[...]
        vx_ref[:, dv:] = jnp.ones((vx_ref.shape[0], dv), BF16)

    m_ref[...] = jnp.full(m_ref.shape, -jnp.inf, F32)
    acc_ref[...] = jnp.zeros(acc_ref.shape, F32)
    q = q_ref[...]
    lane = lax.broadcasted_iota(jnp.int32, q.shape, 1)
    zero = jnp.zeros_like(q)
    halves = (jnp.where(lane < DIFF_HEAD_DIM, q, zero), jnp.where(lane >= DIFF_HEAD_DIM, q, zero))

    def block(j, masked):
        off = pl.multiple_of(j * tq, tq)
        k = k_ref[pl.ds(off, tq), :]
        vx = vx_ref[pl.ds(off, tq), :]
        if masked:
            row = lax.broadcasted_iota(jnp.int32, (tq, tq), 0)
            col = lax.broadcasted_iota(jnp.int32, (tq, tq), 1)
            causal = col <= row
        for mi, qm in enumerate(halves):
            s = _nt_dot(qm, k)
            if masked:
                s = jnp.where(causal, s, -jnp.inf)
            m_prev = m_ref[mi]
            m_new = jnp.maximum(m_prev, jnp.max(s, axis=1, keepdims=True))
            p = jnp.exp2(s - jnp.tile(m_new, (1, tq // LANES)))
            alpha = jnp.exp2(m_prev - m_new)
            acc_ref[mi] = jnp.tile(alpha, (1, 2)) * acc_ref[mi] + _dot(p.astype(BF16), vx)
            m_ref[mi] = m_new

    def body(j, carry):
        block(j, False)
        return carry

    lax.fori_loop(0, i, body, 0)
    block(i, True)

    lv = lam_ref[...]
    lam = (jnp.exp(jnp.sum(lv[0:1] * lv[1:2], axis=1, keepdims=True))
           - jnp.exp(jnp.sum(lv[2:3] * lv[3:4], axis=1, keepdims=True)) + lambda_init)
    a0 = acc_ref[0]
    a1 = acc_ref[1]
    o = a0[:, :dv] / a0[:, dv:] - lam * (a1[:, :dv] / a1[:, dv:])
    ms = jnp.mean(o * o, axis=-1, keepdims=True)
    o = o * lax.rsqrt(ms + LN_EPS) * g_ref[...] * (1.0 - lambda_init)
    o_ref[...] = o.astype(o_ref.dtype)


def _diff_attention(h0, lam_vec, subln_g, lambda_init, tq):
    t = h0.shape[0]
    tq = min(tq, t)
    nq = t // tq
    kern = functools.partial(_diff_attn_kernel, tq=tq, lambda_init=lambda_init)
    return pl.pallas_call(
        kern,
        grid=(DIFF_HEADS, nq),
        in_specs=[
            pl.BlockSpec((4, DIFF_HEAD_DIM), lambda h, i: (0, 0)),
            pl.BlockSpec((1, DIFF_V_DIM), lambda h, i: (0, 0)),
            pl.BlockSpec((tq, LANES), lambda h, i: (i, h)),
            pl.BlockSpec((t, LANES), lambda h, i: (0, DIFF_HEADS + h)),
            pl.BlockSpec((t, LANES), lambda h, i: (0, 2 * DIFF_HEADS + h)),
        ],
        out_specs=pl.BlockSpec((tq, LANES), lambda h, i: (i, h)),
        out_shape=jax.ShapeDtypeStruct((t, DIFF_HEADS * DIFF_V_DIM), BF16),
        scratch_shapes=[pltpu.VMEM((t, 2 * DIFF_V_DIM), BF16), pltpu.VMEM((2, tq, LANES), F32),
                        pltpu.VMEM((2, tq, 2 * DIFF_V_DIM), F32)],
        compiler_params=_params(("arbitrary", "arbitrary")),
        name="diff_attn",
    )(lam_vec, subln_g, h0, h0, h0)


def _gla_kernel(q_ref, k_ref, v_ref, r_ref, gb_ref, wg_ref, bg_ref, ng_ref, o_ref, st_ref, *, tg):
    @pl.when(pl.program_id(0) == 0)
    def _():
        st_ref[...] = jnp.zeros(st_ref.shape, F32)

    c = GLA_CHUNK
    gate = _dot(gb_ref[...], wg_ref[...]) + bg_ref[...]
    log_a = -(jnp.maximum(-gate, 0.0) + jnp.log(1.0 + jnp.exp(-jnp.abs(gate)))) / GLA_GATE_TEMP
    row = lax.broadcasted_iota(jnp.int32, (c, c), 0)
    col = lax.broadcasted_iota(jnp.int32, (c, c), 1)
    tril = row >= col
    tri = jnp.where(tril, 1.0, 0.0).astype(BF16)
    for h in range(GLA_HEADS):
        ksl = slice(h * GLA_DK, (h + 1) * GLA_DK)
        vsl = slice(h * GLA_DV, (h + 1) * GLA_DV)
        for ci in range(tg // c):
            rs = slice(ci * c, (ci + 1) * c)
            la = log_a[rs, ksl]
            b = _split_dot_left(tri, la)
            b_last = b[c - 1:c, :]
            qf = q_ref[rs, ksl].astype(F32) * (GLA_DK ** -0.5)
            kf = k_ref[rs, ksl].astype(F32)
            v = v_ref[rs, vsl]
            q_dec = (qf * jnp.exp(b)).astype(BF16)
            k_dec = (kf * jnp.exp(-b)).astype(BF16)
            k_end = (kf * jnp.exp(b_last - b)).astype(BF16)
            scores = jnp.where(tril, _nt_dot(q_dec, k_dec), 0.0)
            st = st_ref[h]
            o = _dot(scores.astype(BF16), v) + _nt_dot(q_dec, st.astype(BF16))
            kv_t = lax.dot_general(v, k_end, (((0,), (0,)), ((), ())), preferred_element_type=F32)
            st_ref[h] = jnp.exp(b_last) * st + kv_t
            ms = jnp.mean(o * o, axis=-1, keepdims=True)
            o = o * lax.rsqrt(ms + LN_EPS) * ng_ref[...]
            r = r_ref[rs, vsl].astype(F32)
            o_ref[rs, vsl] = (o * (r / (1.0 + jnp.exp(-r)))).astype(o_ref.dtype)


def _gla(h0, wg_pad, b_gate, norm_g, tg):
    t = h0.shape[0]
    tg = min(tg, t)
    qk = GLA_HEADS * GLA_DK
    vw = GLA_HEADS * GLA_DV
    return pl.pallas_call(
        functools.partial(_gla_kernel, tg=tg),
        grid=(t // tg,),
        in_specs=[
            pl.BlockSpec((tg, qk), lambda i: (i, 3072 // qk)),
            pl.BlockSpec((tg, qk), lambda i: (i, 3584 // qk)),
            pl.BlockSpec((tg, vw), lambda i: (i, 4096 // vw)),
            pl.BlockSpec((tg, vw), lambda i: (i, 5120 // vw)),
            pl.BlockSpec((tg, LANES), lambda i: (i, 6144 // LANES)),
            pl.BlockSpec((LANES, qk), lambda i: (0, 0)),
            pl.BlockSpec((1, qk), lambda i: (0, 0)),
            pl.BlockSpec((1, GLA_DV), lambda i: (0, 0)),
        ],
        out_specs=pl.BlockSpec((tg, vw), lambda i: (i, 0)),
        out_shape=jax.ShapeDtypeStruct((t, vw), BF16),
        scratch_shapes=[pltpu.VMEM((GLA_HEADS, GLA_DV, GLA_DK), F32)],
        compiler_params=_params(("arbitrary",)),
        name="gla",
    )(h0, h0, h0, h0, h0, wg_pad, b_gate, norm_g)


def _sb_kernel(q_ref, k_ref, v_ref, o_ref, acc_ref, c_ref, *, tq):
    i = pl.program_id(1)
    acc_ref[...] = jnp.zeros(acc_ref.shape, F32)
    c_ref[...] = jnp.zeros(c_ref.shape, F32)
    q = q_ref[...]
    row = lax.broadcasted_iota(jnp.int32, (tq, tq), 0)
    col = lax.broadcasted_iota(jnp.int32, (tq, tq), 1)
    strict = col < row
    tri = jnp.where(row > col, 1.0, 0.0).astype(BF16)

    def block(jb, masked):
        off = pl.multiple_of(jb * tq, tq)
        k = k_ref[pl.ds(off, tq), :]
        v = v_ref[pl.ds(off, tq), :]
        z = _nt_dot(q, k) * (SB_HEAD_DIM ** -0.5)
        lsn = -(jnp.maximum(z, 0.0) + jnp.log(1.0 + jnp.exp(-jnp.abs(z))))
        lm = jnp.where(strict, lsn, 0.0) if masked else lsn
        excl = _split_dot(lm, tri) + c_ref[...]
        w = jnp.exp(z + lsn + excl)
        if masked:
            w = jnp.where(strict, w, 0.0)
        acc_ref[...] += _dot(w.astype(BF16), v)
        c_new = c_ref[...] + jnp.sum(lm, axis=1, keepdims=True)
        c_ref[...] = c_new
        return jnp.max(c_new)

    cmax = block(i, True)

    def cond(carry):
        jb, cm = carry
        return jnp.logical_and(jb >= 0, cm > -SB_LOG_FLOOR)

    def body(carry):
        jb, _ = carry
        return jb - 1, block(jb, False)

    lax.while_loop(cond, body, (i - 1, cmax))
    o_ref[...] = acc_ref[...].astype(o_ref.dtype)


def _stick_breaking(h1, tq):
    t = h1.shape[0]
    tq = min(tq, t)
    return pl.pallas_call(
        functools.partial(_sb_kernel, tq=tq),
        grid=(SB_HEADS, t // tq),
        in_specs=[
            pl.BlockSpec((tq, LANES), lambda h, i: (i, h)),
            pl.BlockSpec((t, LANES), lambda h, i: (0, SB_HEADS + h)),
            pl.BlockSpec((t, LANES), lambda h, i: (0, 2 * SB_HEADS + h)),
        ],
        out_specs=pl.BlockSpec((tq, LANES), lambda h, i: (i, h)),
        out_shape=jax.ShapeDtypeStruct((t, SB_HEADS * SB_HEAD_DIM), BF16),
        scratch_shapes=[pltpu.VMEM((tq, SB_HEAD_DIM), F32), pltpu.VMEM((tq, 1), F32)],
        compiler_params=_params(("parallel", "arbitrary")),
        name="stick_breaking",
    )(h1, h1, h1)


def _s5_prep_kernel(are_ref, aim_ref, ls_ref, bre_ref, bim_ref, lbr_ref, lbi_ref, bbr_ref, bbi_ref):
    step = jnp.exp(ls_ref[...])
    lam_re = are_ref[...]
    lam_im = aim_ref[...]
    mag = jnp.exp(step * lam_re)
    lb_re = mag * jnp.cos(step * lam_im)
    lb_im = mag * jnp.sin(step * lam_im)
    den = lam_re * lam_re + lam_im * lam_im
    n_re = lb_re - 1.0
    f_re = (n_re * lam_re + lb_im * lam_im) / den
    f_im = (lb_im * lam_re - n_re * lam_im) / den
    lbr_ref[...] = lb_re
    lbi_ref[...] = lb_im
    bbr_ref[...] = f_re * bre_ref[...] - f_im * bim_ref[...]
    bbi_ref[...] = f_re * bim_ref[...] + f_im * bre_ref[...]


def _s5_prep(a_re, a_im, log_step, b_re, b_im):
    rows = S5_GROUPS * S5_GROUP
    rep = lambda p: jnp.repeat(p, S5_GROUP, axis=0)
    to_rows = lambda b: jnp.transpose(b, (0, 2, 1)).reshape(rows, S5_STATE)
    spec = pl.BlockSpec((rows, S5_STATE), lambda: (0, 0))
    shp = jax.ShapeDtypeStruct((rows, S5_STATE), F32)
    return pl.pallas_call(
        _s5_prep_kernel,
        in_specs=[spec, spec, pl.BlockSpec((rows, 1), lambda: (0, 0)), spec, spec],
        out_specs=[spec] * 4,
        out_shape=[shp] * 4,
        name="s5_prep",
    )(rep(a_re), rep(a_im), rep(log_step[:, None]), to_rows(b_re), to_rows(b_im))


def _cmul(ar, ai, br, bi):
    return ar * br - ai * bi, ar * bi + ai * br


def _s5_kernel(u_ref, wbu_ref, wc_ref, lam_ref, d_ref, wglu_ref, bglu_ref, o_ref,
               hs_ref, hp_ref, pw_ref, st_ref, *, tc):
    ns = S5_KT_STATES

    @pl.when(pl.program_id(0) == 0)
    def _():
        hp_ref[...] = jnp.zeros(hp_ref.shape, F32)
        l1r = jnp.broadcast_to(lam_ref[0:1, :], (SUBLANES, lam_ref.shape[1]))
        l1i = jnp.broadcast_to(lam_ref[1:2, :], (SUBLANES, lam_ref.shape[1]))
        l2r, l2i = _cmul(l1r, l1i, l1r, l1i)
        l4r, l4i = _cmul(l2r, l2i, l2r, l2i)
        l8r, l8i = _cmul(l4r, l4i, l4r, l4i)
        ridx = lax.broadcasted_iota(jnp.int32, l1r.shape, 0)
        n = ridx + 1
        pr = jnp.ones_like(l1r)
        pi = jnp.zeros_like(l1r)
        for bit, (lr, li) in ((1, (l1r, l1i)), (2, (l2r, l2i)), (4, (l4r, l4i)), (8, (l8r, l8i))):
            nr, ni = _cmul(pr, pi, lr, li)
            take = (n & bit) != 0
            pr = jnp.where(take, nr, pr)
            pi = jnp.where(take, ni, pi)
        pw_ref[0] = pr
        pw_ref[1] = pi
        for s, (lr, li) in enumerate(((l1r, l1i), (l2r, l2i), (l4r, l4i))):
            keep = ridx >= (1 << s)
            st_ref[2 * s] = jnp.where(keep, lr, 0.0)
            st_ref[2 * s + 1] = jnp.where(keep, li, 0.0)

    u = u_ref[...]
    for kt in range(S5_KT):
        hs_ref[:, kt * 2 * ns:(kt + 1) * 2 * ns] = _dot(u[:, kt * S5_KT_CH:(kt + 1) * S5_KT_CH], wbu_ref[kt])

    w = S5_SCAN_W
    for kt in range(S5_KT):
        for cc in range(ns // w):
            re0 = kt * 2 * ns + cc * w
            im0 = re0 + ns
            l0 = kt * ns + cc * w

            def body(r, carry, re0=re0, im0=im0, l0=l0):
                hpr, hpi = carry
                rows = pl.ds(pl.multiple_of(r * SUBLANES, SUBLANES), SUBLANES)
                xr = hs_ref[rows, re0:re0 + w]
                xi = hs_ref[rows, im0:im0 + w]
                for s in range(3):
                    sr = pltpu.roll(xr, 1 << s, 0)
                    si = pltpu.roll(xi, 1 << s, 0)
                    ar = st_ref[2 * s, :, l0:l0 + w]
                    ai = st_ref[2 * s + 1, :, l0:l0 + w]
                    xr, xi = xr + ar * sr - ai * si, xi + ar * si + ai * sr
                pr = pw_ref[0, :, l0:l0 + w]
                pi = pw_ref[1, :, l0:l0 + w]
                xr, xi = xr + pr * hpr - pi * hpi, xi + pr * hpi + pi * hpr
                hs_ref[rows, re0:re0 + w] = xr
                hs_ref[rows, im0:im0 + w] = xi
                return (jnp.broadcast_to(xr[SUBLANES - 1:SUBLANES, :], (SUBLANES, w)),
                        jnp.broadcast_to(xi[SUBLANES - 1:SUBLANES, :], (SUBLANES, w)))

            hpr, hpi = lax.fori_loop(0, tc // SUBLANES, body,
                                     (hp_ref[:, re0:re0 + w], hp_ref[:, im0:im0 + w]))
            hp_ref[:, re0:re0 + w] = hpr
            hp_ref[:, im0:im0 + w] = hpi

    ys = [_dot(hs_ref[:, kt * 2 * ns:(kt + 1) * 2 * ns].astype(BF16), wc_ref[kt]) for kt in range(S5_KT)]
    y = jnp.concatenate(ys, axis=1) + d_ref[...] * u.astype(F32)
    g = 0.5 * y * (1.0 + jnp.tanh(math.sqrt(2.0 / math.pi) * (y + 0.044715 * (y * y * y))))
    gl = _dot(g.astype(BF16), wglu_ref[...]) + bglu_ref[...]
    o_ref[...] = (g / (1.0 + jnp.exp(-gl))).astype(o_ref.dtype)


def _s5(h1, wbu, wc, lam, d_skip, w_glu, b_glu, tc):
    t = h1.shape[0]
    tc = min(tc, t)
    ns = S5_KT_STATES
    nstate = S5_GROUPS * S5_STATE
    full = lambda shape: pl.BlockSpec(shape, lambda i: (0,) * len(shape))
    return pl.pallas_call(
        functools.partial(_s5_kernel, tc=tc),
        grid=(t // tc,),
        in_specs=[
            pl.BlockSpec((tc, S5_CHANNELS), lambda i: (i, 3)),
            full((S5_KT, S5_KT_CH, 2 * ns)),
            full((S5_KT, 2 * ns, S5_KT_CH)),
            full((2, nstate)),
            full((1, S5_CHANNELS)),
            full((S5_CHANNELS, S5_CHANNELS)),
            full((1, S5_CHANNELS)),
        ],
        out_specs=pl.BlockSpec((tc, S5_CHANNELS), lambda i: (i, 0)),
        out_shape=jax.ShapeDtypeStruct((t, S5_CHANNELS), BF16),
        scratch_shapes=[pltpu.VMEM((tc, 2 * nstate), F32), pltpu.VMEM((SUBLANES, 2 * nstate), F32),
                        pltpu.VMEM((2, SUBLANES, nstate), F32), pltpu.VMEM((6, SUBLANES, nstate), F32)],
        compiler_params=_params(("arbitrary",)),
        name="s5",
    )(h1, wbu, wc, lam, d_skip, w_glu, b_glu)


def _route(probs, sel):
    rows = [sel[e:e + 1, :] for e in range(N_EXPERTS)]
    prow = [probs[e:e + 1, :] for e in range(N_EXPERTS)]
    best_score = None
    for g in range(N_GROUPS):
        a = rows[g * EXPERTS_PER_GROUP:(g + 1) * EXPERTS_PER_GROUP]
        score = None
        for x in range(EXPERTS_PER_GROUP):
            for y in range(x + 1, EXPERTS_PER_GROUP):
                pair = a[x] + a[y]
                score = pair if score is None else jnp.maximum(score, pair)
        if best_score is None:
            best_score, gbest = score, jnp.zeros(score.shape, jnp.int32)
            cs = list(a)
            cp = prow[0:EXPERTS_PER_GROUP]
        else:
            better = score > best_score
            best_score = jnp.where(better, score, best_score)
            gbest = jnp.where(better, g, gbest)
            cs = [jnp.where(better, a[x], cs[x]) for x in range(EXPERTS_PER_GROUP)]
            cp = [jnp.where(better, prow[g * EXPERTS_PER_GROUP + x], cp[x]) for x in range(EXPERTS_PER_GROUP)]

    def first_argmax(vals):
        bv, bi = vals[0], jnp.zeros(vals[0].shape, jnp.int32)
        for x in range(1, len(vals)):
            better = vals[x] > bv
            bv = jnp.where(better, vals[x], bv)
            bi = jnp.where(better, x, bi)
        return bi

    i1 = first_argmax(cs)
    i2 = first_argmax([jnp.where(i1 == x, -jnp.inf, cs[x]) for x in range(EXPERTS_PER_GROUP)])

    def pick(idx):
        out = cp[0]
        for x in range(1, EXPERTS_PER_GROUP):
            out = jnp.where(idx == x, cp[x], out)
        return out

    p1, p2 = pick(i1), pick(i2)
    tot = p1 + p2
    return gbest * EXPERTS_PER_GROUP + i1, gbest * EXPERTS_PER_GROUP + i2, p1 / tot, p2 / tot


def _ln_router_tail(v, g_ref, b_ref, rwt_ref, rb_ref, x_ref, xb_ref, ei_ref, ew_ref):
    xn = _layer_norm(v, g_ref[...], b_ref[...])
    x_ref[...] = xn
    xb_ref[...] = xn.astype(BF16)
    logits = lax.dot_general(rwt_ref[...], xn, (((1,), (1,)), ((), ())), preferred_element_type=F32,
                             precision=lax.Precision.HIGHEST)
    mx = jnp.max(logits, axis=0, keepdims=True)
    ex = jnp.exp(logits - mx)
    probs = ex / jnp.sum(ex, axis=0, keepdims=True)
    e1, e2, w1, w2 = _route(probs, probs + rb_ref[...])
    n = e1.shape[1]
    ei_ref[...] = jnp.concatenate([e1, e2, jnp.zeros((SUBLANES - 2, n), jnp.int32)], axis=0)
    ew_ref[...] = jnp.concatenate([w1, w2, jnp.zeros((SUBLANES - 2, n), F32)], axis=0)


def _outproj_kernel(ma_ref, mb_ref, wa_ref, wb_ref, xr_ref, g_ref, b_ref, rwt_ref, rb_ref,
                    x_ref, xb_ref, ei_ref, ew_ref):
    v = DEEPNORM_ALPHA * xr_ref[...] + _dot(ma_ref[...], wa_ref[...]) + _dot(mb_ref[...], wb_ref[...])
    _ln_router_tail(v, g_ref, b_ref, rwt_ref, rb_ref, x_ref, xb_ref, ei_ref, ew_ref)


def _outproj_ln_router(mix_a, mix_b, w_out, x_res, ln_g, ln_b, rwt, rb, tm):
    t = x_res.shape[0]
    tm = min(tm, t)
    half = w_out.shape[0] // 2
    row = lambda w: pl.BlockSpec((tm, w), lambda i: (i, 0))
    full = lambda shape: pl.BlockSpec(shape, lambda i: (0,) * len(shape))
    return pl.pallas_call(
        _outproj_kernel,
        grid=(t // tm,),
        in_specs=[row(half), row(half),
                  pl.BlockSpec((half, D_MODEL), lambda i: (0, 0)), pl.BlockSpec((half, D_MODEL), lambda i: (1, 0)),
                  row(D_MODEL), full((1, D_MODEL)), full((1, D_MODEL)),
                  full((N_EXPERTS, D_MODEL)), full((N_EXPERTS, 1))],
        out_specs=[row(D_MODEL), row(D_MODEL),
                   pl.BlockSpec((SUBLANES, tm), lambda i: (0, i)), pl.BlockSpec((SUBLANES, tm), lambda i: (0, i))],
        out_shape=[jax.ShapeDtypeStruct((t, D_MODEL), F32), jax.ShapeDtypeStruct((t, D_MODEL), BF16),
                   jax.ShapeDtypeStruct((SUBLANES, t), jnp.int32), jax.ShapeDtypeStruct((SUBLANES, t), F32)],
        compiler_params=_params(("parallel",)),
        name="outproj_ln_router",
    )(mix_a, mix_b, w_out, w_out, x_res, ln_g, ln_b, rwt, rb)


def _moe_kernel(te_ref, nu_ref, xs_ref, wgu_ref, wd_ref, rw_ref, y_ref):
    tile = pl.program_id(0)

    @pl.when(tile < nu_ref[0])
    def _():
        hgu = _dot(xs_ref[...], wgu_ref[...])
        hg = hgu[:, :D_EXPERT]
        hu = hgu[:, D_EXPERT:]
        h = (hg / (1.0 + jnp.exp(-hg))) * hu
        y_ref[...] = rw_ref[:, 0:1] * _dot(h.astype(BF16), wd_ref[...])

    @pl.when(tile >= nu_ref[0])
    def _():
        y_ref[...] = jnp.zeros(y_ref.shape, F32)


def _moe(tile_expert, n_used, xs, wgu, wd, roww, tm):
    t_pad = xs.shape[0]
    grid_spec = pltpu.PrefetchScalarGridSpec(
        num_scalar_prefetch=2,
        grid=(t_pad // tm,),
        in_specs=[
            pl.BlockSpec((tm, D_MODEL), lambda i, te, nu: (i, 0)),
            pl.BlockSpec((None, D_MODEL, 2 * D_EXPERT), lambda i, te, nu: (te[i], 0, 0)),
            pl.BlockSpec((None, D_EXPERT, D_MODEL), lambda i, te, nu: (te[i], 0, 0)),
            pl.BlockSpec((tm, LANES), lambda i, te, nu: (i, 0)),
        ],
        out_specs=pl.BlockSpec((tm, D_MODEL), lambda i, te, nu: (i, 0)),
    )
    return pl.pallas_call(
        _moe_kernel,
        grid_spec=grid_spec,
        out_shape=jax.ShapeDtypeStruct((t_pad, D_MODEL), F32),
        compiler_params=_params(("arbitrary",)),
        name="moe",
    )(tile_expert, n_used, xs, wgu, wd, roww)


def _moe_ln_kernel(x_ref, y1_ref, y2_ref, g_ref, b_ref, o_ref, ob_ref):
    v = DEEPNORM_ALPHA * x_ref[...] + (y1_ref[...] + y2_ref[...])
    xn = _layer_norm(v, g_ref[...], b_ref[...])
    o_ref[...] = xn
    ob_ref[...] = xn.astype(BF16)


def _moe_ln(x, y1, y2, ln_g, ln_b, tm):
    t = x.shape[0]
    tm = min(tm, t)
    row = pl.BlockSpec((tm, D_MODEL), lambda i: (i, 0))
    vec = pl.BlockSpec((1, D_MODEL), lambda i: (0, 0))
    return pl.pallas_call(
        _moe_ln_kernel,
        grid=(t // tm,),
        in_specs=[row, row, row, vec, vec],
        out_specs=[row, row],
        out_shape=[jax.ShapeDtypeStruct((t, D_MODEL), F32), jax.ShapeDtypeStruct((t, D_MODEL), BF16)],
        compiler_params=_params(("parallel",)),
        name="moe_ln",
    )(x, y1, y2, ln_g, ln_b)


def _moe_block(x, xb, eidx, ew, wgu, wd, ln_g, ln_b, tm):
    t = x.shape[0]
    tm = min(tm, t)
    e_flat = eidx[:2].reshape(2 * t)
    w_flat = ew[:2].reshape(2 * t)
    tok = jnp.tile(jnp.arange(t, dtype=jnp.int32), 2)
    onehot = (e_flat[:, None] == jnp.arange(N_EXPERTS, dtype=jnp.int32)[None, :]).astype(jnp.int32)
    rank = jnp.take_along_axis(jnp.cumsum(onehot, axis=0), e_flat[:, None], axis=1)[:, 0] - 1
    counts = jnp.sum(onehot, axis=0)
    padded = ((counts + tm - 1) // tm) * tm
    ends = jnp.cumsum(padded)
    starts = ends - padded
    dest = starts[e_flat] + rank
    t_pad = 2 * t + N_EXPERTS * tm
    src_tok = jnp.zeros((t_pad,), jnp.int32).at[dest].set(tok)
    roww = jnp.zeros((t_pad,), F32).at[dest].set(w_flat)
    tile_start = jnp.arange(t_pad // tm, dtype=jnp.int32) * tm
    tile_expert = jnp.minimum(jnp.sum((ends[None, :] <= tile_start[:, None]).astype(jnp.int32), axis=1),
                              N_EXPERTS - 1)
    n_used = (ends[-1] // tm).astype(jnp.int32).reshape(1)
    xs = jnp.take(xb, src_tok, axis=0)
    ys = _moe(tile_expert, n_used, xs, wgu, wd, jnp.broadcast_to(roww[:, None], (t_pad, LANES)), tm)
    y1 = jnp.take(ys, dest[:t], axis=0)
    y2 = jnp.take(ys, dest[t:], axis=0)
    return _moe_ln(x, y1, y2, ln_g, ln_b, tm)


def _block_diag_tiles(m, rows_per_group, cols_per_group):
    gpt = S5_GROUPS // S5_KT
    m = m.reshape(S5_KT, gpt, rows_per_group, cols_per_group)
    eye = jnp.eye(gpt, dtype=m.dtype)
    bd = jnp.einsum("kgrc,gh->kgrhc", m, eye)
    return bd.reshape(S5_KT, gpt * rows_per_group, gpt * cols_per_group)


def kernel(x, router_w, router_b, l0_w_in, l0_diff_lam_q1, l0_diff_lam_k1, l0_diff_lam_q2, l0_diff_lam_k2, l0_diff_subln_g, l0_gla_w_gate, l0_gla_b_gate, l0_gla_norm_g, l0_w_out, l0_ln1_g, l0_ln1_b, l0_moe_w_gate, l0_moe_w_up, l0_moe_w_down, l0_ln2_g, l0_ln2_b, l1_w_in, l1_s5_a_re, l1_s5_a_im, l1_s5_log_step, l1_s5_b_re, l1_s5_b_im, l1_s5_c_re, l1_s5_c_im, l1_s5_d, l1_s5_w_glu, l1_s5_b_glu, l1_w_out, l1_ln1_g, l1_ln1_b, l1_moe_w_gate, l1_moe_w_up, l1_moe_w_down, l1_ln2_g, l1_ln2_b):
    bsz, t, d = x.shape
    assert bsz == 1 and d == D_MODEL
    x0 = x.reshape(t, d)
    vec = lambda p: p.reshape(1, -1).astype(F32)
    rwt = router_w.T.astype(F32)
    rb = router_b.reshape(N_EXPERTS, 1).astype(F32)

    lambda_init = 0.8 - 0.6 * math.exp(-0.3 * 0)
    c = [0, 1024, 2048, 3072, 3584, 4096, 5120, 5136, 6160]
    q_scale = math.log2(math.e) * DIFF_HEAD_DIM ** -0.5
    w0 = jnp.concatenate([l0_w_in[:, :c[1]] * q_scale, l0_w_in[:, c[1]:c[6]], l0_w_in[:, c[7]:c[8]],
                          l0_w_in[:, c[6]:c[7]], jnp.zeros((d, LANES - GLA_GATE_RANK), l0_w_in.dtype)],
                         axis=1).astype(BF16)
    h0 = _matmul(x0.astype(BF16), w0, 1024, 896, BF16)
    lam_vec = jnp.stack([l0_diff_lam_q1, l0_diff_lam_k1, l0_diff_lam_q2, l0_diff_lam_k2]).astype(F32)
    o_a = _diff_attention(h0, lam_vec, vec(l0_diff_subln_g), lambda_init, 512)
    wg_pad = jnp.concatenate([l0_gla_w_gate, jnp.zeros((LANES - GLA_GATE_RANK, GLA_HEADS * GLA_DK), F32)],
                             axis=0).astype(BF16)
    o_b = _gla(h0, wg_pad, vec(l0_gla_b_gate), vec(l0_gla_norm_g), 256)
    x1, x1b, eidx, ew = _outproj_ln_router(o_a, o_b, l0_w_out.astype(BF16), x0, vec(l0_ln1_g), vec(l0_ln1_b),
                                           rwt, rb, 256)
    wgu0 = jnp.concatenate([l0_moe_w_gate, l0_moe_w_up], axis=2).astype(BF16)
    x2, x2b = _moe_block(x1, x1b, eidx, ew, wgu0, l0_moe_w_down.astype(BF16), vec(l0_ln2_g), vec(l0_ln2_b), 256)

    h1 = _matmul(x2b, l1_w_in.astype(BF16), 1024, 1024, BF16)
    o_c = _stick_breaking(h1, 256)
    lbr, lbi, bbr, bbi = _s5_prep(l1_s5_a_re, l1_s5_a_im, l1_s5_log_step, l1_s5_b_re, l1_s5_b_im)
    lam = jnp.stack([lbr[::S5_GROUP].reshape(-1), lbi[::S5_GROUP].reshape(-1)])
    bbr = bbr.reshape(S5_GROUPS, S5_GROUP, S5_STATE)
    bbi = bbi.reshape(S5_GROUPS, S5_GROUP, S5_STATE)
    wbu = jnp.concatenate([_block_diag_tiles(bbr, S5_GROUP, S5_STATE),
                           _block_diag_tiles(bbi, S5_GROUP, S5_STATE)], axis=2).astype(BF16)
    c_re_t = jnp.transpose(l1_s5_c_re, (0, 2, 1))
    c_im_t = jnp.transpose(l1_s5_c_im, (0, 2, 1))
    wc = jnp.concatenate([_block_diag_tiles(c_re_t, S5_STATE, S5_GROUP),
                          _block_diag_tiles(-c_im_t, S5_STATE, S5_GROUP)], axis=1).astype(BF16)
    o_d = _s5(h1, wbu, wc, lam, vec(l1_s5_d), l1_s5_w_glu.astype(BF16), vec(l1_s5_b_glu), 256)
    x3, x3b, eidx1, ew1 = _outproj_ln_router(o_c, o_d, l1_w_out.astype(BF16), x2, vec(l1_ln1_g), vec(l1_ln1_b),
                                             rwt, rb, 256)
    wgu1 = jnp.concatenate([l1_moe_w_gate, l1_moe_w_up], axis=2).astype(BF16)
    x4, _ = _moe_block(x3, x3b, eidx1, ew1, wgu1, l1_moe_w_down.astype(BF16), vec(l1_ln2_g), vec(l1_ln2_b), 256)
    return x4.reshape(bsz, t, d)
```

```python
import functools
import math

import jax
import jax.numpy as jnp
from jax import lax
from jax.experimental import pallas as pl
from jax.experimental.pallas import tpu as pltpu

F32 = jnp.float32
BF16 = jnp.bfloat16

D_MODEL = 2048
DEPTH = 2
DIFF_HEADS = 8
DIFF_HEAD_DIM = 64
DIFF_V_DIM = 128
GLA_HEADS = 4
GLA_DK = 128
GLA_DV = 256
GLA_GATE_RANK = 16
GLA_GATE_TEMP = 16.0
GLA_CHUNK = 64
SB_HEADS = 8
SB_HEAD_DIM = 128
S5_CHANNELS = 1024
S5_GROUP = 16
S5_GROUPS = 64
S5_STATE = 64
N_EXPERTS = 16
N_GROUPS = 4
EXPERTS_PER_GROUP = 4
D_EXPERT = 640
DEEPNORM_ALPHA = (2.0 * DEPTH) ** 0.25
LN_EPS = 1e-5

LANES = 128
SUBLANES = 8
VMEM_LIMIT = 56 * 1024 * 1024
SB_LOG_FLOOR = 120.0

S5_KT = 4
S5_KT_CH = S5_CHANNELS // S5_KT
S5_KT_STATES = (S5_GROUPS // S5_KT) * S5_STATE
S5_SCAN_W = 512


def _params(sem, flags=None):
    return pltpu.CompilerParams(dimension_semantics=sem, vmem_limit_bytes=VMEM_LIMIT, flags=flags)


def _nt_dot(a, b):
    return lax.dot_general(a, b, (((1,), (1,)), ((), ())), preferred_element_type=F32)


def _dot(a, b):
    return jnp.dot(a, b, preferred_element_type=F32)


def _split_dot(x, m):
    hi = x.astype(BF16)
    lo = (x - hi.astype(F32)).astype(BF16)
    return _dot(hi, m) + _dot(lo, m)


def _split_dot_left(m, x):
    hi = x.astype(BF16)
    lo = (x - hi.astype(F32)).astype(BF16)
    return _dot(m, hi) + _dot(m, lo)


def _layer_norm(v, g, b):
    mu = jnp.mean(v, axis=-1, keepdims=True)
    d = v - mu
    var = jnp.mean(d * d, axis=-1, keepdims=True)
    return d * lax.rsqrt(var + LN_EPS) * g + b


def _mm_kernel(a_ref, b_ref, o_ref, a_bf):
    @pl.when(pl.program_id(1) == 0)
    def _():
        a_bf[...] = a_ref[...].astype(BF16)

    o_ref[...] = _dot(a_bf[...], b_ref[...]).astype(o_ref.dtype)


def _matmul(a, b, tm, tn, out_dtype):
    m, k = a.shape
    n = b.shape[1]
    tm = min(tm, m)
    return pl.pallas_call(
        _mm_kernel,
        grid=(m // tm, n // tn),
        in_specs=[pl.BlockSpec((tm, k), lambda i, j: (i, 0)),
                  pl.BlockSpec((k, tn), lambda i, j: (0, j))],
        out_specs=pl.BlockSpec((tm, tn), lambda i, j: (i, j)),
        out_shape=jax.ShapeDtypeStruct((m, n), out_dtype),
        scratch_shapes=[pltpu.VMEM((tm, k), BF16)],
        compiler_params=_params(("parallel", "arbitrary")),
        name="in_proj",
    )(a, b)


def _diff_attn_kernel(lam_ref, g_ref, q_ref, k_ref, v_ref, o_ref, vx_ref, m_ref, acc_ref, sa_ref, sb_ref, *, tq,
                      lambda_init):
    i = pl.program_id(1)
    dv = DIFF_V_DIM

    @pl.when(i == 0)
    def _():
        vx_ref[:, :dv] = v_ref[...]
        vx_ref[:, dv:] = jnp.ones((vx_ref.shape[0], dv), BF16)

    m_ref[...] = jnp.full(m_ref.shape, -jnp.inf, F32)
    acc_ref[...] = jnp.zeros(acc_ref.shape, F32)
    q = q_ref[...]
    lane = lax.broadcasted_iota(jnp.int32, q.shape, 1)
    zero = jnp.zeros_like(q)
    halves = (jnp.where(lane < DIFF_HEAD_DIM, q, zero), jnp.where(lane >= DIFF_HEAD_DIM, q, zero))

    def scores(j, mi):
        off = pl.multiple_of(j * tq, tq)
        return _nt_dot(halves[mi], k_ref[pl.ds(off, tq), :])

    def softmax_pv(j, get_s, masked):
        off = pl.multiple_of(j * tq, tq)
        vx = vx_ref[pl.ds(off, tq), :]
        if masked:
            row = lax.broadcasted_iota(jnp.int32, (tq, tq), 0)
            col = lax.broadcasted_iota(jnp.int32, (tq, tq), 1)
            causal = col <= row
        for mi in range(2):
            s = get_s(mi)
            if masked:
                s = jnp.where(causal, s, -jnp.inf)
            m_prev = m_ref[mi]
            m_new = jnp.maximum(m_prev, jnp.max(s, axis=1, keepdims=True))
            p = jnp.exp2(s - jnp.tile(m_new, (1, tq // LANES)))
            alpha = jnp.exp2(m_prev - m_new)
            acc_ref[mi] = jnp.tile(alpha, (1, 2)) * acc_ref[mi] + _dot(p.astype(BF16), vx)
            m_ref[mi] = m_new

    def scores_to(j, s_ref):
        for mi in range(2):
            s_ref[mi] = scores(j, mi)

    scores_to(0, sa_ref)

    def pair(t, carry):
        j = 2 * t
        scores_to(j + 1, sb_ref)
        softmax_pv(j, lambda mi: sa_ref[mi], False)
        scores_to(j + 2, sa_ref)
        softmax_pv(j + 1, lambda mi: sb_ref[mi], False)
        return carry

    lax.fori_loop(0, i // 2, pair, 0)

    @pl.when(i % 2 == 1)
    def _():
        softmax_pv(i - 1, lambda mi: sa_ref[mi], False)
        scores_to(i, sa_ref)

    softmax_pv(i, lambda mi: sa_ref[mi], True)

    lv = lam_ref[...]
    lam = (jnp.exp(jnp.sum(lv[0:1] * lv[1:2], axis=1, keepdims=True))
           - jnp.exp(jnp.sum(lv[2:3] * lv[3:4], axis=1, keepdims=True)) + lambda_init)
    a0 = acc_ref[0]
    a1 = acc_ref[1]
    o = a0[:, :dv] / a0[:, dv:] - lam * (a1[:, :dv] / a1[:, dv:])
    ms = jnp.mean(o * o, axis=-1, keepdims=True)
    o = o * lax.rsqrt(ms + LN_EPS) * g_ref[...] * (1.0 - lambda_init)
    o_ref[...] = o.astype(o_ref.dtype)


def _diff_attention(h0, lam_vec, subln_g, lambda_init, tq):
    t = h0.shape[0]
    tq = min(tq, t)
    nq = t // tq
    kern = functools.partial(_diff_attn_kernel, tq=tq, lambda_init=lambda_init)
    return pl.pallas_call(
        kern,
        grid=(DIFF_HEADS, nq),
        in_specs=[
            pl.BlockSpec((4, DIFF_HEAD_DIM), lambda h, i: (0, 0)),
            pl.BlockSpec((1, DIFF_V_DIM), lambda h, i: (0, 0)),
            pl.BlockSpec((tq, LANES), lambda h, i: (i, h)),
            pl.BlockSpec((t, LANES), lambda h, i: (0, DIFF_HEADS + h)),
            pl.BlockSpec((t, LANES), lambda h, i: (0, 2 * DIFF_HEADS + h)),
        ],
        out_specs=pl.BlockSpec((tq, LANES), lambda h, i: (i, h)),
        out_shape=jax.ShapeDtypeStruct((t, DIFF_HEADS * DIFF_V_DIM), BF16),
        scratch_shapes=[pltpu.VMEM((t, 2 * DIFF_V_DIM), BF16), pltpu.VMEM((2, tq, LANES), F32),
                        pltpu.VMEM((2, tq, 2 * DIFF_V_DIM), F32),
                        pltpu.VMEM((2, tq, tq), F32), pltpu.VMEM((2, tq, tq), F32)],
        compiler_params=_params(("arbitrary", "arbitrary")),
        name="diff_attn",
    )(lam_vec, subln_g, h0, h0, h0)


def _gla_kernel(q_ref, k_ref, v_ref, r_ref, gb_ref, wg_ref, bg_ref, ng_ref, o_ref, st_ref, *, tg):
    @pl.when(pl.program_id(0) == 0)
    def _():
        st_ref[...] = jnp.zeros(st_ref.shape, F32)

    c = GLA_CHUNK
    gate = _dot(gb_ref[...], wg_ref[...]) + bg_ref[...]
    log_a = -(jnp.maximum(-gate, 0.0) + jnp.log(1.0 + jnp.exp(-jnp.abs(gate)))) / GLA_GATE_TEMP
    row = lax.broadcasted_iota(jnp.int32, (c, c), 0)
    col = lax.broadcasted_iota(jnp.int32, (c, c), 1)
    tril = row >= col
    tri = jnp.where(tril, 1.0, 0.0).astype(BF16)
    for h in range(GLA_HEADS):
        ksl = slice(h * GLA_DK, (h + 1) * GLA_DK)
        vsl = slice(h * GLA_DV, (h + 1) * GLA_DV)
        for ci in range(tg // c):
            rs = slice(ci * c, (ci + 1) * c)
            la = log_a[rs, ksl]
            b = _split_dot_left(tri, la)
            b_last = b[c - 1:c, :]
            qf = q_ref[rs, ksl].astype(F32) * (GLA_DK ** -0.5)
            kf = k_ref[rs, ksl].astype(F32)
            v = v_ref[rs, vsl]
            q_dec = (qf * jnp.exp(b)).astype(BF16)
            k_dec = (kf * jnp.exp(-b)).astype(BF16)
            k_end = (kf * jnp.exp(b_last - b)).astype(BF16)
            scores = jnp.where(tril, _nt_dot(q_dec, k_dec), 0.0)
            st = st_ref[h]
            o = _dot(scores.astype(BF16), v) + _nt_dot(q_dec, st.astype(BF16))
            kv_t = lax.dot_general(v, k_end, (((0,), (0,)), ((), ())), preferred_element_type=F32)
            st_ref[h] = jnp.exp(b_last) * st + kv_t
            ms = jnp.mean(o * o, axis=-1, keepdims=True)
            o = o * lax.rsqrt(ms + LN_EPS) * ng_ref[...]
            r = r_ref[rs, vsl].astype(F32)
            o_ref[rs, vsl] = (o * (r / (1.0 + jnp.exp(-r)))).astype(o_ref.dtype)


def _gla(h0, wg_pad, b_gate, norm_g, tg):
    t = h0.shape[0]
    tg = min(tg, t)
    qk = GLA_HEADS * GLA_DK
    vw = GLA_HEADS * GLA_DV
    return pl.pallas_call(
        functools.partial(_gla_kernel, tg=tg),
        grid=(t // tg,),
        in_specs=[
            pl.BlockSpec((tg, qk), lambda i: (i, 3072 // qk)),
            pl.BlockSpec((tg, qk), lambda i: (i, 3584 // qk)),
            pl.BlockSpec((tg, vw), lambda i: (i, 4096 // vw)),
            pl.BlockSpec((tg, vw), lambda i: (i, 5120 // vw)),
            pl.BlockSpec((tg, LANES), lambda i: (i, 6144 // LANES)),
            pl.BlockSpec((LANES, qk), lambda i: (0, 0)),
            pl.BlockSpec((1, qk), lambda i: (0, 0)),
            pl.BlockSpec((1, GLA_DV), lambda i: (0, 0)),
        ],
        out_specs=pl.BlockSpec((tg, vw), lambda i: (i, 0)),
        out_shape=jax.ShapeDtypeStruct((t, vw), BF16),
        scratch_shapes=[pltpu.VMEM((GLA_HEADS, GLA_DV, GLA_DK), F32)],
        compiler_params=_params(("arbitrary",)),
        name="gla",
    )(h0, h0, h0, h0, h0, wg_pad, b_gate, norm_g)


def _sb_kernel(q_ref, k_ref, v_ref, o_ref, acc_ref, c_ref, *, tq):
    i = pl.program_id(1)
    acc_ref[...] = jnp.zeros(acc_ref.shape, F32)
    c_ref[...] = jnp.zeros(c_ref.shape, F32)
    q = q_ref[...]
    row = lax.broadcasted_iota(jnp.int32, (tq, tq), 0)
    col = lax.broadcasted_iota(jnp.int32, (tq, tq), 1)
    strict = col < row
    tri = jnp.where(row > col, 1.0, 0.0).astype(BF16)

    def block(jb, masked):
        off = pl.multiple_of(jb * tq, tq)
        k = k_ref[pl.ds(off, tq), :]
        v = v_ref[pl.ds(off, tq), :]
        z = _nt_dot(q, k) * (SB_HEAD_DIM ** -0.5)
        lsn = -(jnp.maximum(z, 0.0) + jnp.log(1.0 + jnp.exp(-jnp.abs(z))))
        lm = jnp.where(strict, lsn, 0.0) if masked else lsn
        excl = _split_dot(lm, tri) + c_ref[...]
        w = jnp.exp(z + lsn + excl)
        if masked:
            w = jnp.where(strict, w, 0.0)
        acc_ref[...] += _dot(w.astype(BF16), v)
        c_new = c_ref[...] + jnp.sum(lm, axis=1, keepdims=True)
        c_ref[...] = c_new
        return jnp.max(c_new)

    cmax = block(i, True)

    def cond(carry):
        jb, cm = carry
        return jnp.logical_and(jb >= 0, cm > -SB_LOG_FLOOR)

    def body(carry):
        jb, _ = carry
        return jb - 1, block(jb, False)

    lax.while_loop(cond, body, (i - 1, cmax))
    o_ref[...] = acc_ref[...].astype(o_ref.dtype)


def _stick_breaking(h1, tq):
    t = h1.shape[0]
    tq = min(tq, t)
    return pl.pallas_call(
        functools.partial(_sb_kernel, tq=tq),
        grid=(SB_HEADS, t // tq),
        in_specs=[
            pl.BlockSpec((tq, LANES), lambda h, i: (i, h)),
            pl.BlockSpec((t, LANES), lambda h, i: (0, SB_HEADS + h)),
            pl.BlockSpec((t, LANES), lambda h, i: (0, 2 * SB_HEADS + h)),
        ],
        out_specs=pl.BlockSpec((tq, LANES), lambda h, i: (i, h)),
        out_shape=jax.ShapeDtypeStruct((t, SB_HEADS * SB_HEAD_DIM), BF16),
        scratch_shapes=[pltpu.VMEM((tq, SB_HEAD_DIM), F32), pltpu.VMEM((tq, 1), F32)],
        compiler_params=_params(("parallel", "arbitrary")),
        name="stick_breaking",
    )(h1, h1, h1)


def _s5_prep_kernel(are_ref, aim_ref, ls_ref, bre_ref, bim_ref, lbr_ref, lbi_ref, bbr_ref, bbi_ref):
    step = jnp.exp(ls_ref[...])
    lam_re = are_ref[...]
    lam_im = aim_ref[...]
    mag = jnp.exp(step * lam_re)
    lb_re = mag * jnp.cos(step * lam_im)
    lb_im = mag * jnp.sin(step * lam_im)
    den = lam_re * lam_re + lam_im * lam_im
    n_re = lb_re - 1.0
    f_re = (n_re * lam_re + lb_im * lam_im) / den
    f_im = (lb_im * lam_re - n_re * lam_im) / den
    lbr_ref[...] = lb_re
    lbi_ref[...] = lb_im
    bbr_ref[...] = f_re * bre_ref[...] - f_im * bim_ref[...]
    bbi_ref[...] = f_re * bim_ref[...] + f_im * bre_ref[...]


def _s5_prep(a_re, a_im, log_step, b_re, b_im):
    rows = S5_GROUPS * S5_GROUP
    rep = lambda p: jnp.repeat(p, S5_GROUP, axis=0)
    to_rows = lambda b: jnp.transpose(b, (0, 2, 1)).reshape(rows, S5_STATE)
    spec = pl.BlockSpec((rows, S5_STATE), lambda: (0, 0))
    shp = jax.ShapeDtypeStruct((rows, S5_STATE), F32)
    return pl.pallas_call(
        _s5_prep_kernel,
        in_specs=[spec, spec, pl.BlockSpec((rows, 1), lambda: (0, 0)), spec, spec],
        out_specs=[spec] * 4,
        out_shape=[shp] * 4,
        name="s5_prep",
    )(rep(a_re), rep(a_im), rep(log_step[:, None]), to_rows(b_re), to_rows(b_im))


def _cmul(ar, ai, br, bi):
    return ar * br - ai * bi, ar * bi + ai * br


def _s5_kernel(u_ref, wbu_ref, wc_ref, lam_ref, d_ref, wglu_ref, bglu_ref, o_ref,
               hs_ref, hp_ref, pw_ref, st_ref, *, tc):
    ns = S5_KT_STATES

    @pl.when(pl.program_id(0) == 0)
    def _():
        hp_ref[...] = jnp.zeros(hp_ref.shape, F32)
        l1r = jnp.broadcast_to(lam_ref[0:1, :], (SUBLANES, lam_ref.shape[1]))
        l1i = jnp.broadcast_to(lam_ref[1:2, :], (SUBLANES, lam_ref.shape[1]))
        l2r, l2i = _cmul(l1r, l1i, l1r, l1i)
        l4r, l4i = _cmul(l2r, l2i, l2r, l2i)
        l8r, l8i = _cmul(l4r, l4i, l4r, l4i)
        ridx = lax.broadcasted_iota(jnp.int32, l1r.shape, 0)
        n = ridx + 1
        pr = jnp.ones_like(l1r)
        pi = jnp.zeros_like(l1r)
        for bit, (lr, li) in ((1, (l1r, l1i)), (2, (l2r, l2i)), (4, (l4r, l4i)), (8, (l8r, l8i))):
            nr, ni = _cmul(pr, pi, lr, li)
            take = (n & bit) != 0
            pr = jnp.where(take, nr, pr)
            pi = jnp.where(take, ni, pi)
        pw_ref[0] = pr
        pw_ref[1] = pi
        for s, (lr, li) in enumerate(((l1r, l1i), (l2r, l2i), (l4r, l4i))):
            keep = ridx >= (1 << s)
            st_ref[2 * s] = jnp.where(keep, lr, 0.0)
            st_ref[2 * s + 1] = jnp.where(keep, li, 0.0)

    u = u_ref[...]
    for kt in range(S5_KT):
        hs_ref[:, kt * 2 * ns:(kt + 1) * 2 * ns] = _dot(u[:, kt * S5_KT_CH:(kt + 1) * S5_KT_CH], wbu_ref[kt])

    w = S5_SCAN_W
    for kt in range(S5_KT):
        for cc in range(ns // w):
            re0 = kt * 2 * ns + cc * w
            im0 = re0 + ns
            l0 = kt * ns + cc * w

            def body(r, carry, re0=re0, im0=im0, l0=l0):
                hpr, hpi = carry
                rows = pl.ds(pl.multiple_of(r * SUBLANES, SUBLANES), SUBLANES)
                xr = hs_ref[rows, re0:re0 + w]
                xi = hs_ref[rows, im0:im0 + w]
                for s in range(3):
                    sr = pltpu.roll(xr, 1 << s, 0)
                    si = pltpu.roll(xi, 1 << s, 0)
                    ar = st_ref[2 * s, :, l0:l0 + w]
                    ai = st_ref[2 * s + 1, :, l0:l0 + w]
                    xr, xi = xr + ar * sr - ai * si, xi + ar * si + ai * sr
                pr = pw_ref[0, :, l0:l0 + w]
                pi = pw_ref[1, :, l0:l0 + w]
                xr, xi = xr + pr * hpr - pi * hpi, xi + pr * hpi + pi * hpr
                hs_ref[rows, re0:re0 + w] = xr
                hs_ref[rows, im0:im0 + w] = xi
                return (jnp.broadcast_to(xr[SUBLANES - 1:SUBLANES, :], (SUBLANES, w)),
                        jnp.broadcast_to(xi[SUBLANES - 1:SUBLANES, :], (SUBLANES, w)))

            hpr, hpi = lax.fori_loop(0, tc // SUBLANES, body,
                                     (hp_ref[:, re0:re0 + w], hp_ref[:, im0:im0 + w]))
            hp_ref[:, re0:re0 + w] = hpr
            hp_ref[:, im0:im0 + w] = hpi

    ys = [_dot(hs_ref[:, kt * 2 * ns:(kt + 1) * 2 * ns].astype(BF16), wc_ref[kt]) for kt in range(S5_KT)]
    y = jnp.concatenate(ys, axis=1) + d_ref[...] * u.astype(F32)
    g = 0.5 * y * (1.0 + jnp.tanh(math.sqrt(2.0 / math.pi) * (y + 0.044715 * (y * y * y))))
    gl = _dot(g.astype(BF16), wglu_ref[...]) + bglu_ref[...]
    o_ref[...] = (g / (1.0 + jnp.exp(-gl))).astype(o_ref.dtype)


def _s5(h1, wbu, wc, lam, d_skip, w_glu, b_glu, tc):
    t = h1.shape[0]
    tc = min(tc, t)
    ns = S5_KT_STATES
    nstate = S5_GROUPS * S5_STATE
    full = lambda shape: pl.BlockSpec(shape, lambda i: (0,) * len(shape))
    return pl.pallas_call(
        functools.partial(_s5_kernel, tc=tc),
        grid=(t // tc,),
        in_specs=[
            pl.BlockSpec((tc, S5_CHANNELS), lambda i: (i, 3)),
            full((S5_KT, S5_KT_CH, 2 * ns)),
            full((S5_KT, 2 * ns, S5_KT_CH)),
            full((2, nstate)),
            full((1, S5_CHANNELS)),
            full((S5_CHANNELS, S5_CHANNELS)),
            full((1, S5_CHANNELS)),
        ],
        out_specs=pl.BlockSpec((tc, S5_CHANNELS), lambda i: (i, 0)),
        out_shape=jax.ShapeDtypeStruct((t, S5_CHANNELS), BF16),
        scratch_shapes=[pltpu.VMEM((tc, 2 * nstate), F32), pltpu.VMEM((SUBLANES, 2 * nstate), F32),
                        pltpu.VMEM((2, SUBLANES, nstate), F32), pltpu.VMEM((6, SUBLANES, nstate), F32)],
        compiler_params=_params(("arbitrary",)),
        name="s5",
    )(h1, wbu, wc, lam, d_skip, w_glu, b_glu)


def _route(probs, sel):
    rows = [sel[e:e + 1, :] for e in range(N_EXPERTS)]
    prow = [probs[e:e + 1, :] for e in range(N_EXPERTS)]
    best_score = None
    for g in range(N_GROUPS):
        a = rows[g * EXPERTS_PER_GROUP:(g + 1) * EXPERTS_PER_GROUP]
        score = None
        for x in range(EXPERTS_PER_GROUP):
            for y in range(x + 1, EXPERTS_PER_GROUP):
                pair = a[x] + a[y]
                score = pair if score is None else jnp.maximum(score, pair)
        if best_score is None:
            best_score, gbest = score, jnp.zeros(score.shape, jnp.int32)
            cs = list(a)
            cp = prow[0:EXPERTS_PER_GROUP]
        else:
            better = score > best_score
            best_score = jnp.where(better, score, best_score)
            gbest = jnp.where(better, g, gbest)
            cs = [jnp.where(better, a[x], cs[x]) for x in range(EXPERTS_PER_GROUP)]
            cp = [jnp.where(better, prow[g * EXPERTS_PER_GROUP + x], cp[x]) for x in range(EXPERTS_PER_GROUP)]

    def first_argmax(vals):
        bv, bi = vals[0], jnp.zeros(vals[0].shape, jnp.int32)
        for x in range(1, len(vals)):
            better = vals[x] > bv
            bv = jnp.where(better, vals[x], bv)
            bi = jnp.where(better, x, bi)
        return bi

    i1 = first_argmax(cs)
    i2 = first_argmax([jnp.where(i1 == x, -jnp.inf, cs[x]) for x in range(EXPERTS_PER_GROUP)])

    def pick(idx):
        out = cp[0]
        for x in range(1, EXPERTS_PER_GROUP):
            out = jnp.where(idx == x, cp[x], out)
        return out

    p1, p2 = pick(i1), pick(i2)
    tot = p1 + p2
    return gbest * EXPERTS_PER_GROUP + i1, gbest * EXPERTS_PER_GROUP + i2, p1 / tot, p2 / tot


def _ln_router_tail(v, g_ref, b_ref, rwt_ref, rb_ref, x_ref, xb_ref, ei_ref, ew_ref):
    xn = _layer_norm(v, g_ref[...], b_ref[...])
    x_ref[...] = xn
    xb_ref[...] = xn.astype(BF16)
    logits = lax.dot_general(rwt_ref[...], xn, (((1,), (1,)), ((), ())), preferred_element_type=F32,
                             precision=lax.Precision.HIGHEST)
    mx = jnp.max(logits, axis=0, keepdims=True)
    ex = jnp.exp(logits - mx)
    probs = ex / jnp.sum(ex, axis=0, keepdims=True)
    e1, e2, w1, w2 = _route(probs, probs + rb_ref[...])
    n = e1.shape[1]
    ei_ref[...] = jnp.concatenate([e1, e2, jnp.zeros((SUBLANES - 2, n), jnp.int32)], axis=0)
    ew_ref[...] = jnp.concatenate([w1, w2, jnp.zeros((SUBLANES - 2, n), F32)], axis=0)


def _outproj_kernel(ma_ref, mb_ref, wa_ref, wb_ref, xr_ref, g_ref, b_ref, rwt_ref, rb_ref,
                    x_ref, xb_ref, ei_ref, ew_ref):
    v = DEEPNORM_ALPHA * xr_ref[...] + _dot(ma_ref[...], wa_ref[...]) + _dot(mb_ref[...], wb_ref[...])
    _ln_router_tail(v, g_ref, b_ref, rwt_ref, rb_ref, x_ref, xb_ref, ei_ref, ew_ref)


def _outproj_ln_router(mix_a, mix_b, w_out, x_res, ln_g, ln_b, rwt, rb, tm):
    t = x_res.shape[0]
    tm = min(tm, t)
    half = w_out.shape[0] // 2
    row = lambda w: pl.BlockSpec((tm, w), lambda i: (i, 0))
    full = lambda shape: pl.BlockSpec(shape, lambda i: (0,) * len(shape))
    return pl.pallas_call(
        _outproj_kernel,
        grid=(t // tm,),
        in_specs=[row(half), row(half),
                  pl.BlockSpec((half, D_MODEL), lambda i: (0, 0)), pl.BlockSpec((half, D_MODEL), lambda i: (1, 0)),
                  row(D_MODEL), full((1, D_MODEL)), full((1, D_MODEL)),
                  full((N_EXPERTS, D_MODEL)), full((N_EXPERTS, 1))],
        out_specs=[row(D_MODEL), row(D_MODEL),
                   pl.BlockSpec((SUBLANES, tm), lambda i: (0, i)), pl.BlockSpec((SUBLANES, tm), lambda i: (0, i))],
        out_shape=[jax.ShapeDtypeStruct((t, D_MODEL), F32), jax.ShapeDtypeStruct((t, D_MODEL), BF16),
                   jax.ShapeDtypeStruct((SUBLANES, t), jnp.int32), jax.ShapeDtypeStruct((SUBLANES, t), F32)],
        compiler_params=_params(("parallel",)),
        name="outproj_ln_router",
    )(mix_a, mix_b, w_out, w_out, x_res, ln_g, ln_b, rwt, rb)


def _rank_kernel(e_ref, rank_ref, cnt_ref, base_ref, *, tn):
    @pl.when(pl.program_id(0) == 0)
    def _():
        base_ref[...] = jnp.zeros(base_ref.shape, F32)

    e = e_ref[...]
    rows = lax.broadcasted_iota(jnp.int32, (N_EXPERTS, tn), 0)
    oh1 = rows == e[0:1, :]
    oh2 = rows == e[1:2, :]
    oh = jnp.where(jnp.logical_or(oh1, oh2), 1.0, 0.0)
    r = lax.broadcasted_iota(jnp.int32, (tn, tn), 0)
    c = lax.broadcasted_iota(jnp.int32, (tn, tn), 1)
    earlier = jnp.where(r < c, 1.0, 0.0).astype(BF16)
    base = base_ref[...]
    before = _dot(oh.astype(BF16), earlier) + jnp.tile(base, (1, tn // LANES))
    r1 = jnp.sum(jnp.where(oh1, before, 0.0), axis=0, keepdims=True)
    r2 = jnp.sum(jnp.where(oh2, before, 0.0), axis=0, keepdims=True)
    rank_ref[...] = jnp.concatenate([r1, r2, jnp.zeros((SUBLANES - 2, tn), F32)], axis=0).astype(jnp.int32)
    base = base + jnp.sum(oh, axis=1, keepdims=True)
    base_ref[...] = base
    cnt_ref[...] = base.astype(jnp.int32)


def _rank(eidx, tn):
    t = eidx.shape[1]
    tn = min(tn, t)
    return pl.pallas_call(
        functools.partial(_rank_kernel, tn=tn),
        grid=(t // tn,),
        in_specs=[pl.BlockSpec((SUBLANES, tn), lambda i: (0, i))],
        out_specs=[pl.BlockSpec((SUBLANES, tn), lambda i: (0, i)),
                   pl.BlockSpec((N_EXPERTS, LANES), lambda i: (0, 0))],
        out_shape=[jax.ShapeDtypeStruct((SUBLANES, t), jnp.int32),
                   jax.ShapeDtypeStruct((N_EXPERTS, LANES), jnp.int32)],
        scratch_shapes=[pltpu.VMEM((N_EXPERTS, LANES), F32)],
        compiler_params=_params(("arbitrary",)),
        name="moe_rank",
    )(eidx)


def _moe_kernel(te_ref, nu_ref, xs_ref, wg_ref, wu_ref, wd_ref, y_ref, wgu_s, wd_s):
    tile = pl.program_id(0)
    used = tile < nu_ref[0]
    fresh = jnp.logical_or(tile == 0, te_ref[tile] != te_ref[jnp.maximum(tile - 1, 0)])

    @pl.when(jnp.logical_and(used, fresh))
    def _():
        wgu_s[:, :D_EXPERT] = wg_ref[...].astype(BF16)
        wgu_s[:, D_EXPERT:] = wu_ref[...].astype(BF16)
        wd_s[...] = wd_ref[...].astype(BF16)

    @pl.when(used)
    def _():
        hgu = _dot(xs_ref[...], wgu_s[...])
        hg = hgu[:, :D_EXPERT]
        hu = hgu[:, D_EXPERT:]
        h = (hg / (1.0 + jnp.exp(-hg))) * hu
        y_ref[...] = _dot(h.astype(BF16), wd_s[...])

    @pl.when(jnp.logical_not(used))
    def _():
        y_ref[...] = jnp.zeros(y_ref.shape, F32)


def _moe(tile_expert, n_used, xs, w_gate, w_up, w_down, tm):
    t_pad = xs.shape[0]
    grid_spec = pltpu.PrefetchScalarGridSpec(
        num_scalar_prefetch=2,
        grid=(t_pad // tm,),
        in_specs=[
            pl.BlockSpec((tm, D_MODEL), lambda i, te, nu: (i, 0)),
            pl.BlockSpec((None, D_MODEL, D_EXPERT), lambda i, te, nu: (te[i], 0, 0)),
            pl.BlockSpec((None, D_MODEL, D_EXPERT), lambda i, te, nu: (te[i], 0, 0)),
            pl.BlockSpec((None, D_EXPERT, D_MODEL), lambda i, te, nu: (te[i], 0, 0)),
        ],
        out_specs=pl.BlockSpec((tm, D_MODEL), lambda i, te, nu: (i, 0)),
        scratch_shapes=[pltpu.VMEM((D_MODEL, 2 * D_EXPERT), BF16), pltpu.VMEM((D_EXPERT, D_MODEL), BF16)],
    )
    return pl.pallas_call(
        _moe_kernel,
        grid_spec=grid_spec,
        out_shape=jax.ShapeDtypeStruct((t_pad, D_MODEL), F32),
        compiler_params=_params(("arbitrary",)),
        name="moe",
    )(tile_expert, n_used, xs, w_gate, w_up, w_down)


def _moe_ln_kernel(x_ref, y1_ref, y2_ref, w_ref, g_ref, b_ref, o_ref):
    w = w_ref[...]
    v = DEEPNORM_ALPHA * x_ref[...] + (w[:, 0:1] * y1_ref[...] + w[:, 1:2] * y2_ref[...])
    o_ref[...] = _layer_norm(v, g_ref[...], b_ref[...])


def _moe_ln(x, y1, y2, w_cols, ln_g, ln_b, tm):
    t = x.shape[0]
    tm = min(tm, t)
    row = pl.BlockSpec((tm, D_MODEL), lambda i: (i, 0))
    vec = pl.BlockSpec((1, D_MODEL), lambda i: (0, 0))
    return pl.pallas_call(
        _moe_ln_kernel,
        grid=(t // tm,),
        in_specs=[row, row, row, pl.BlockSpec((tm, SUBLANES), lambda i: (i, 0)), vec, vec],
        out_specs=row,
        out_shape=jax.ShapeDtypeStruct((t, D_MODEL), F32),
        compiler_params=_params(("parallel",)),
        name="moe_ln",
    )(x, y1, y2, w_cols, ln_g, ln_b)


def _moe_block(x, xb, eidx, ew, w_gate, w_up, w_down, ln_g, ln_b, tm):
    t = x.shape[0]
    tm = min(tm, t)
    rank, cnt = _rank(eidx, 512)
    counts = cnt[:, 0]
    padded = ((counts + tm - 1) // tm) * tm
    ends = jnp.cumsum(padded)
    starts = ends - padded
    e2 = eidx[:2]
    start_of = jnp.zeros_like(e2)
    for k in range(N_EXPERTS):
        start_of = jnp.where(e2 == k, starts[k], start_of)
    dest = start_of + rank[:2]
    t_pad = 2 * t + N_EXPERTS * tm
    tok = jnp.tile(jnp.arange(t, dtype=jnp.int32), 2)
    src_tok = jnp.zeros((t_pad,), jnp.int32).at[dest.reshape(2 * t)].set(tok)
    tile_start = jnp.arange(t_pad // tm, dtype=jnp.int32) * tm
    tile_expert = jnp.minimum(jnp.sum((ends[None, :] <= tile_start[:, None]).astype(jnp.int32), axis=1),
                              N_EXPERTS - 1)
    n_used = (ends[-1] // tm).astype(jnp.int32).reshape(1)
    xs = jnp.take(xb, src_tok, axis=0)
    ys = _moe(tile_expert, n_used, xs, w_gate, w_up, w_down, tm)
    y1 = jnp.take(ys, dest[0], axis=0)
    y2 = jnp.take(ys, dest[1], axis=0)
    return _moe_ln(x, y1, y2, ew.T, ln_g, ln_b, tm)


def _block_diag_tiles(m, rows_per_group, cols_per_group):
    gpt = S5_GROUPS // S5_KT
    m = m.reshape(S5_KT, gpt, rows_per_group, cols_per_group)
    eye = jnp.eye(gpt, dtype=m.dtype)
    bd = jnp.einsum("kgrc,gh->kgrhc", m, eye)
    return bd.reshape(S5_KT, gpt * rows_per_group, gpt * cols_per_group)


def kernel(x, router_w, router_b, l0_w_in, l0_diff_lam_q1, l0_diff_lam_k1, l0_diff_lam_q2, l0_diff_lam_k2, l0_diff_subln_g, l0_gla_w_gate, l0_gla_b_gate, l0_gla_norm_g, l0_w_out, l0_ln1_g, l0_ln1_b, l0_moe_w_gate, l0_moe_w_up, l0_moe_w_down, l0_ln2_g, l0_ln2_b, l1_w_in, l1_s5_a_re, l1_s5_a_im, l1_s5_log_step, l1_s5_b_re, l1_s5_b_im, l1_s5_c_re, l1_s5_c_im, l1_s5_d, l1_s5_w_glu, l1_s5_b_glu, l1_w_out, l1_ln1_g, l1_ln1_b, l1_moe_w_gate, l1_moe_w_up, l1_moe_w_down, l1_ln2_g, l1_ln2_b):
    bsz, t, d = x.shape
    assert bsz == 1 and d == D_MODEL
    x0 = x.reshape(t, d)
    vec = lambda p: p.reshape(1, -1).astype(F32)
    rwt = router_w.T.astype(F32)
    rb = router_b.reshape(N_EXPERTS, 1).astype(F32)

    lambda_init = 0.8 - 0.6 * math.exp(-0.3 * 0)
    c = [0, 1024, 2048, 3072, 3584, 4096, 5120, 5136, 6160]
    q_scale = math.log2(math.e) * DIFF_HEAD_DIM ** -0.5
    w0 = jnp.concatenate([l0_w_in[:, :c[1]] * q_scale, l0_w_in[:, c[1]:c[6]], l0_w_in[:, c[7]:c[8]],
                          l0_w_in[:, c[6]:c[7]], jnp.zeros((d, LANES - GLA_GATE_RANK), l0_w_in.dtype)],
                         axis=1).astype(BF16)
    h0 = _matmul(x0, w0, 1024, 896, BF16)
    lam_vec = jnp.stack([l0_diff_lam_q1, l0_diff_lam_k1, l0_diff_lam_q2, l0_diff_lam_k2]).astype(F32)
    o_a = _diff_attention(h0, lam_vec, vec(l0_diff_subln_g), lambda_init, 512)
    wg_pad = jnp.concatenate([l0_gla_w_gate, jnp.zeros((LANES - GLA_GATE_RANK, GLA_HEADS * GLA_DK), F32)],
                             axis=0).astype(BF16)
    o_b = _gla(h0, wg_pad, vec(l0_gla_b_gate), vec(l0_gla_norm_g), 256)
    x1, x1b, eidx, ew = _outproj_ln_router(o_a, o_b, l0_w_out.astype(BF16), x0, vec(l0_ln1_g), vec(l0_ln1_b),
                                           rwt, rb, 256)
    x2 = _moe_block(x1, x1b, eidx, ew, l0_moe_w_gate, l0_moe_w_up, l0_moe_w_down, vec(l0_ln2_g), vec(l0_ln2_b), 256)

    h1 = _matmul(x2, l1_w_in.astype(BF16), 1024, 1024, BF16)
    o_c = _stick_breaking(h1, 256)
    lbr, lbi, bbr, bbi = _s5_prep(l1_s5_a_re, l1_s5_a_im, l1_s5_log_step, l1_s5_b_re, l1_s5_b_im)
    lam = jnp.stack([lbr[::S5_GROUP].reshape(-1), lbi[::S5_GROUP].reshape(-1)])
    bbr = bbr.reshape(S5_GROUPS, S5_GROUP, S5_STATE)
    bbi = bbi.reshape(S5_GROUPS, S5_GROUP, S5_STATE)
    wbu = jnp.concatenate([_block_diag_tiles(bbr, S5_GROUP, S5_STATE),
                           _block_diag_tiles(bbi, S5_GROUP, S5_STATE)], axis=2).astype(BF16)
    c_re_t = jnp.transpose(l1_s5_c_re, (0, 2, 1))
    c_im_t = jnp.transpose(l1_s5_c_im, (0, 2, 1))
    wc = jnp.concatenate([_block_diag_tiles(c_re_t, S5_STATE, S5_GROUP),
                          _block_diag_tiles(-c_im_t, S5_STATE, S5_GROUP)], axis=1).astype(BF16)
    o_d = _s5(h1, wbu, wc, lam, vec(l1_s5_d), l1_s5_w_glu.astype(BF16), vec(l1_s5_b_glu), 256)
    x3, x3b, eidx1, ew1 = _outproj_ln_router(o_c, o_d, l1_w_out.astype(BF16), x2, vec(l1_ln1_g), vec(l1_ln1_b),
                                             rwt, rb, 256)
    x4 = _moe_block(x3, x3b, eidx1, ew1, l1_moe_w_gate, l1_moe_w_up, l1_moe_w_down, vec(l1_ln2_g), vec(l1_ln2_b), 256)
    return x4.reshape(bsz, t, d)
```

```python
import functools
import math

import jax
import jax.numpy as jnp
from jax import lax
from jax.experimental import pallas as pl
from jax.experimental.pallas import tpu as pltpu

F32 = jnp.float32
BF16 = jnp.bfloat16

D_MODEL = 2048
DEPTH = 2
DIFF_HEADS = 8
DIFF_HEAD_DIM = 64
DIFF_V_DIM = 128
GLA_HEADS = 4
GLA_DK = 128
GLA_DV = 256
GLA_GATE_RANK = 16
GLA_GATE_TEMP = 16.0
GLA_CHUNK = 64
SB_HEADS = 8
SB_HEAD_DIM = 128
S5_CHANNELS = 1024
S5_GROUP = 16
S5_GROUPS = 64
S5_STATE = 64
N_EXPERTS = 16
N_GROUPS = 4
EXPERTS_PER_GROUP = 4
D_EXPERT = 640
DEEPNORM_ALPHA = (2.0 * DEPTH) ** 0.25
LN_EPS = 1e-5

LANES = 128
SUBLANES = 8
VMEM_LIMIT = 56 * 1024 * 1024
SB_LOG_FLOOR = 120.0
SB_HEADS_PER_STEP = 2

S5_KT = 4
S5_KT_CH = S5_CHANNELS // S5_KT
S5_KT_STATES = (S5_GROUPS // S5_KT) * S5_STATE
S5_SCAN_W = 512


def _params(sem, flags=None):
    return pltpu.CompilerParams(dimension_semantics=sem, vmem_limit_bytes=VMEM_LIMIT, flags=flags)


def _nt_dot(a, b):
    return lax.dot_general(a, b, (((1,), (1,)), ((), ())), preferred_element_type=F32)


def _dot(a, b):
    return jnp.dot(a, b, preferred_element_type=F32)


def _split_dot(x, m):
    hi = x.astype(BF16)
    lo = (x - hi.astype(F32)).astype(BF16)
    return _dot(hi, m) + _dot(lo, m)


def _split_dot_left(m, x):
    hi = x.astype(BF16)
    lo = (x - hi.astype(F32)).astype(BF16)
    return _dot(m, hi) + _dot(m, lo)


def _layer_norm(v, g, b):
    mu = jnp.mean(v, axis=-1, keepdims=True)
    d = v - mu
    var = jnp.mean(d * d, axis=-1, keepdims=True)
    return d * lax.rsqrt(var + LN_EPS) * g + b


def _mm_kernel(a_ref, b_ref, o_ref, a_bf):
    @pl.when(pl.program_id(1) == 0)
    def _():
        a_bf[...] = a_ref[...].astype(BF16)

    o_ref[...] = _dot(a_bf[...], b_ref[...]).astype(o_ref.dtype)


def _matmul(a, b, tm, tn, out_dtype):
    m, k = a.shape
    n = b.shape[1]
    tm = min(tm, m)
    return pl.pallas_call(
        _mm_kernel,
        grid=(m // tm, n // tn),
        in_specs=[pl.BlockSpec((tm, k), lambda i, j: (i, 0)),
                  pl.BlockSpec((k, tn), lambda i, j: (0, j))],
        out_specs=pl.BlockSpec((tm, tn), lambda i, j: (i, j)),
        out_shape=jax.ShapeDtypeStruct((m, n), out_dtype),
        scratch_shapes=[pltpu.VMEM((tm, k), BF16)],
        compiler_params=_params(("parallel", "arbitrary")),
        name="in_proj",
    )(a, b)


def _diff_attn_kernel(lam_ref, g_ref, q_ref, k_ref, v_ref, o_ref, vx_ref, m_ref, acc_ref, sa_ref, sb_ref, *, tq,
                      lambda_init):
    i = pl.program_id(1)
    dv = DIFF_V_DIM

    @pl.when(i == 0)
    def _():
        vx_ref[:, :dv] = v_ref[...]
        vx_ref[:, dv:] = jnp.ones((vx_ref.shape[0], dv), BF16)

    m_ref[...] = jnp.full(m_ref.shape, -jnp.inf, F32)
    acc_ref[...] = jnp.zeros(acc_ref.shape, F32)
    q = q_ref[...]
    lane = lax.broadcasted_iota(jnp.int32, q.shape, 1)
    zero = jnp.zeros_like(q)
    halves = (jnp.where(lane < DIFF_HEAD_DIM, q, zero), jnp.where(lane >= DIFF_HEAD_DIM, q, zero))

    def scores(j, mi):
        off = pl.multiple_of(j * tq, tq)
        return _nt_dot(halves[mi], k_ref[pl.ds(off, tq), :])

    def softmax_pv(j, get_s, masked):
        off = pl.multiple_of(j * tq, tq)
        vx = vx_ref[pl.ds(off, tq), :]
        if masked:
            row = lax.broadcasted_iota(jnp.int32, (tq, tq), 0)
            col = lax.broadcasted_iota(jnp.int32, (tq, tq), 1)
            causal = col <= row
        for mi in range(2):
            s = get_s(mi)
            if masked:
                s = jnp.where(causal, s, -jnp.inf)
            m_prev = m_ref[mi]
            m_new = jnp.maximum(m_prev, jnp.max(s, axis=1, keepdims=True))
            p = jnp.exp2(s - jnp.tile(m_new, (1, tq // LANES)))
            alpha = jnp.exp2(m_prev - m_new)
            acc_ref[mi] = jnp.tile(alpha, (1, 2)) * acc_ref[mi] + _dot(p.astype(BF16), vx)
            m_ref[mi] = m_new

    def scores_to(j, s_ref):
        for mi in range(2):
            s_ref[mi] = scores(j, mi)

    scores_to(0, sa_ref)

    def pair(t, carry):
        j = 2 * t
        scores_to(j + 1, sb_ref)
        softmax_pv(j, lambda mi: sa_ref[mi], False)
        scores_to(j + 2, sa_ref)
        softmax_pv(j + 1, lambda mi: sb_ref[mi], False)
        return carry

    lax.fori_loop(0, i // 2, pair, 0)

    @pl.when(i % 2 == 1)
    def _():
        softmax_pv(i - 1, lambda mi: sa_ref[mi], False)
        scores_to(i, sa_ref)

    softmax_pv(i, lambda mi: sa_ref[mi], True)

    lv = lam_ref[...]
    lam = (jnp.exp(jnp.sum(lv[0:1] * lv[1:2], axis=1, keepdims=True))
           - jnp.exp(jnp.sum(lv[2:3] * lv[3:4], axis=1, keepdims=True)) + lambda_init)
    a0 = acc_ref[0]
    a1 = acc_ref[1]
    o = a0[:, :dv] / a0[:, dv:] - lam * (a1[:, :dv] / a1[:, dv:])
    ms = jnp.mean(o * o, axis=-1, keepdims=True)
    o = o * lax.rsqrt(ms + LN_EPS) * g_ref[...] * (1.0 - lambda_init)
    o_ref[...] = o.astype(o_ref.dtype)


def _diff_attention(h0, lam_vec, subln_g, lambda_init, tq):
    t = h0.shape[0]
    tq = min(tq, t)
    nq = t // tq
    kern = functools.partial(_diff_attn_kernel, tq=tq, lambda_init=lambda_init)
    return pl.pallas_call(
        kern,
        grid=(DIFF_HEADS, nq),
        in_specs=[
            pl.BlockSpec((4, DIFF_HEAD_DIM), lambda h, i: (0, 0)),
            pl.BlockSpec((1, DIFF_V_DIM), lambda h, i: (0, 0)),
            pl.BlockSpec((tq, LANES), lambda h, i: (i, h)),
            pl.BlockSpec((t, LANES), lambda h, i: (0, DIFF_HEADS + h)),
            pl.BlockSpec((t, LANES), lambda h, i: (0, 2 * DIFF_HEADS + h)),
        ],
        out_specs=pl.BlockSpec((tq, LANES), lambda h, i: (i, h)),
        out_shape=jax.ShapeDtypeStruct((t, DIFF_HEADS * DIFF_V_DIM), BF16),
        scratch_shapes=[pltpu.VMEM((t, 2 * DIFF_V_DIM), BF16), pltpu.VMEM((2, tq, LANES), F32),
                        pltpu.VMEM((2, tq, 2 * DIFF_V_DIM), F32),
                        pltpu.VMEM((2, tq, tq), F32), pltpu.VMEM((2, tq, tq), F32)],
        compiler_params=_params(("arbitrary", "arbitrary")),
        name="diff_attn",
    )(lam_vec, subln_g, h0, h0, h0)


def _gla_kernel(q_ref, k_ref, v_ref, r_ref, gb_ref, wg_ref, bg_ref, ng_ref, o_ref, st_ref, *, tg):
    @pl.when(pl.program_id(0) == 0)
    def _():
        st_ref[...] = jnp.zeros(st_ref.shape, F32)

    c = GLA_CHUNK
    gate = _dot(gb_ref[...], wg_ref[...]) + bg_ref[...]
    log_a = -(jnp.maximum(-gate, 0.0) + jnp.log(1.0 + jnp.exp(-jnp.abs(gate)))) / GLA_GATE_TEMP
    row = lax.broadcasted_iota(jnp.int32, (c, c), 0)
    col = lax.broadcasted_iota(jnp.int32, (c, c), 1)
    tril = row >= col
    tri = jnp.where(tril, 1.0, 0.0).astype(BF16)
    for h in range(GLA_HEADS):
        ksl = slice(h * GLA_DK, (h + 1) * GLA_DK)
        vsl = slice(h * GLA_DV, (h + 1) * GLA_DV)
        for ci in range(tg // c):
            rs = slice(ci * c, (ci + 1) * c)
            la = log_a[rs, ksl]
            b = _split_dot_left(tri, la)
            b_last = b[c - 1:c, :]
            qf = q_ref[rs, ksl].astype(F32) * (GLA_DK ** -0.5)
            kf = k_ref[rs, ksl].astype(F32)
            v = v_ref[rs, vsl]
            q_dec = (qf * jnp.exp(b)).astype(BF16)
            k_dec = (kf * jnp.exp(-b)).astype(BF16)
            k_end = (kf * jnp.exp(b_last - b)).astype(BF16)
            scores = jnp.where(tril, _nt_dot(q_dec, k_dec), 0.0)
            st = st_ref[h]
            o = _dot(scores.astype(BF16), v) + _nt_dot(q_dec, st.astype(BF16))
            kv_t = lax.dot_general(v, k_end, (((0,), (0,)), ((), ())), preferred_element_type=F32)
            st_ref[h] = jnp.exp(b_last) * st + kv_t
            ms = jnp.mean(o * o, axis=-1, keepdims=True)
            o = o * lax.rsqrt(ms + LN_EPS) * ng_ref[...]
            r = r_ref[rs, vsl].astype(F32)
            o_ref[rs, vsl] = (o * (r / (1.0 + jnp.exp(-r)))).astype(o_ref.dtype)


def _gla(h0, wg_pad, b_gate, norm_g, tg):
    t = h0.shape[0]
    tg = min(tg, t)
    qk = GLA_HEADS * GLA_DK
    vw = GLA_HEADS * GLA_DV
    return pl.pallas_call(
        functools.partial(_gla_kernel, tg=tg),
        grid=(t // tg,),
        in_specs=[
            pl.BlockSpec((tg, qk), lambda i: (i, 3072 // qk)),
            pl.BlockSpec((tg, qk), lambda i: (i, 3584 // qk)),
            pl.BlockSpec((tg, vw), lambda i: (i, 4096 // vw)),
            pl.BlockSpec((tg, vw), lambda i: (i, 5120 // vw)),
            pl.BlockSpec((tg, LANES), lambda i: (i, 6144 // LANES)),
            pl.BlockSpec((LANES, qk), lambda i: (0, 0)),
            pl.BlockSpec((1, qk), lambda i: (0, 0)),
            pl.BlockSpec((1, GLA_DV), lambda i: (0, 0)),
        ],
        out_specs=pl.BlockSpec((tg, vw), lambda i: (i, 0)),
        out_shape=jax.ShapeDtypeStruct((t, vw), BF16),
        scratch_shapes=[pltpu.VMEM((GLA_HEADS, GLA_DV, GLA_DK), F32)],
        compiler_params=_params(("arbitrary",)),
        name="gla",
    )(h0, h0, h0, h0, h0, wg_pad, b_gate, norm_g)


def _sb_kernel(q_ref, k_ref, v_ref, o_ref, acc_ref, c_ref, *, tq):
    i = pl.program_id(1)
    hd = SB_HEAD_DIM
    acc_ref[...] = jnp.zeros(acc_ref.shape, F32)
    c_ref[...] = jnp.zeros(c_ref.shape, F32)
    row = lax.broadcasted_iota(jnp.int32, (tq, tq), 0)
    col = lax.broadcasted_iota(jnp.int32, (tq, tq), 1)
    strict = col < row
    tri = jnp.where(row > col, 1.0, 0.0).astype(BF16)

    def block(jb, masked):
        off = pl.multiple_of(jb * tq, tq)
        cmax = None
        for hh in range(SB_HEADS_PER_STEP):
            cols = slice(hh * hd, (hh + 1) * hd)
            k = k_ref[pl.ds(off, tq), cols]
            v = v_ref[pl.ds(off, tq), cols]
            z = _nt_dot(q_ref[:, cols], k) * (SB_HEAD_DIM ** -0.5)
            lsn = -(jnp.maximum(z, 0.0) + jnp.log(1.0 + jnp.exp(-jnp.abs(z))))
            lm = jnp.where(strict, lsn, 0.0) if masked else lsn
            excl = _split_dot(lm, tri) + c_ref[hh]
            w = jnp.exp(z + lsn + excl)
            if masked:
                w = jnp.where(strict, w, 0.0)
            acc_ref[:, cols] += _dot(w.astype(BF16), v)
            c_new = c_ref[hh] + jnp.sum(lm, axis=1, keepdims=True)
            c_ref[hh] = c_new
            hmax = jnp.max(c_new)
            cmax = hmax if cmax is None else jnp.maximum(cmax, hmax)
        return cmax

    cmax = block(i, True)

    def cond(carry):
        jb, cm = carry
        return jnp.logical_and(jb >= 0, cm > -SB_LOG_FLOOR)

    def body(carry):
        jb, _ = carry
        return jb - 1, block(jb, False)

    lax.while_loop(cond, body, (i - 1, cmax))
    o_ref[...] = acc_ref[...].astype(o_ref.dtype)


def _stick_breaking(h1, tq):
    t = h1.shape[0]
    tq = min(tq, t)
    ng = SB_HEADS // SB_HEADS_PER_STEP
    wd = SB_HEADS_PER_STEP * SB_HEAD_DIM
    return pl.pallas_call(
        functools.partial(_sb_kernel, tq=tq),
        grid=(ng, t // tq),
        in_specs=[
            pl.BlockSpec((tq, wd), lambda h, i: (i, h)),
            pl.BlockSpec((t, wd), lambda h, i: (0, ng + h)),
            pl.BlockSpec((t, wd), lambda h, i: (0, 2 * ng + h)),
        ],
        out_specs=pl.BlockSpec((tq, wd), lambda h, i: (i, h)),
        out_shape=jax.ShapeDtypeStruct((t, SB_HEADS * SB_HEAD_DIM), BF16),
        scratch_shapes=[pltpu.VMEM((tq, wd), F32), pltpu.VMEM((SB_HEADS_PER_STEP, tq, 1), F32)],
        compiler_params=_params(("parallel", "arbitrary")),
        name="stick_breaking",
    )(h1, h1, h1)


def _s5_prep_kernel(are_ref, aim_ref, ls_ref, bre_ref, bim_ref, lbr_ref, lbi_ref, bbr_ref, bbi_ref):
    step = jnp.exp(ls_ref[...])
    lam_re = are_ref[...]
    lam_im = aim_ref[...]
    mag = jnp.exp(step * lam_re)
    lb_re = mag * jnp.cos(step * lam_im)
    lb_im = mag * jnp.sin(step * lam_im)
    den = lam_re * lam_re + lam_im * lam_im
    n_re = lb_re - 1.0
    f_re = (n_re * lam_re + lb_im * lam_im) / den
    f_im = (lb_im * lam_re - n_re * lam_im) / den
    lbr_ref[...] = lb_re
    lbi_ref[...] = lb_im
    bbr_ref[...] = f_re * bre_ref[...] - f_im * bim_ref[...]
    bbi_ref[...] = f_re * bim_ref[...] + f_im * bre_ref[...]


def _s5_prep(a_re, a_im, log_step, b_re, b_im):
    rows = S5_GROUPS * S5_GROUP
    rep = lambda p: jnp.repeat(p, S5_GROUP, axis=0)
    to_rows = lambda b: jnp.transpose(b, (0, 2, 1)).reshape(rows, S5_STATE)
    spec = pl.BlockSpec((rows, S5_STATE), lambda: (0, 0))
    shp = jax.ShapeDtypeStruct((rows, S5_STATE), F32)
    return pl.pallas_call(
        _s5_prep_kernel,
        in_specs=[spec, spec, pl.BlockSpec((rows, 1), lambda: (0, 0)), spec, spec],
        out_specs=[spec] * 4,
        out_shape=[shp] * 4,
        name="s5_prep",
    )(rep(a_re), rep(a_im), rep(log_step[:, None]), to_rows(b_re), to_rows(b_im))


def _cmul(ar, ai, br, bi):
    return ar * br - ai * bi, ar * bi + ai * br


def _s5_kernel(u_ref, wbu_ref, wc_ref, lam_ref, d_ref, wglu_ref, bglu_ref, o_ref,
               hs_ref, hp_ref, pw_ref, st_ref, *, tc):
    ns = S5_KT_STATES

    @pl.when(pl.program_id(0) == 0)
    def _():
        hp_ref[...] = jnp.zeros(hp_ref.shape, F32)
        l1r = jnp.broadcast_to(lam_ref[0:1, :], (SUBLANES, lam_ref.shape[1]))
        l1i = jnp.broadcast_to(lam_ref[1:2, :], (SUBLANES, lam_ref.shape[1]))
        l2r, l2i = _cmul(l1r, l1i, l1r, l1i)
        l4r, l4i = _cmul(l2r, l2i, l2r, l2i)
        l8r, l8i = _cmul(l4r, l4i, l4r, l4i)
        ridx = lax.broadcasted_iota(jnp.int32, l1r.shape, 0)
        n = ridx + 1
        pr = jnp.ones_like(l1r)
        pi = jnp.zeros_like(l1r)
        for bit, (lr, li) in ((1, (l1r, l1i)), (2, (l2r, l2i)), (4, (l4r, l4i)), (8, (l8r, l8i))):
            nr, ni = _cmul(pr, pi, lr, li)
            take = (n & bit) != 0
            pr = jnp.where(take, nr, pr)
            pi = jnp.where(take, ni, pi)
        pw_ref[0] = pr
        pw_ref[1] = pi
        for s, (lr, li) in enumerate(((l1r, l1i), (l2r, l2i), (l4r, l4i))):
            keep = ridx >= (1 << s)
            st_ref[2 * s] = jnp.where(keep, lr, 0.0)
            st_ref[2 * s + 1] = jnp.where(keep, li, 0.0)

    u = u_ref[...]
    for kt in range(S5_KT):
        hs_ref[:, kt * 2 * ns:(kt + 1) * 2 * ns] = _dot(u[:, kt * S5_KT_CH:(kt + 1) * S5_KT_CH], wbu_ref[kt])

    w = S5_SCAN_W
    for kt in range(S5_KT):
        for cc in range(ns // w):
            re0 = kt * 2 * ns + cc * w
            im0 = re0 + ns
            l0 = kt * ns + cc * w

            def body(r, carry, re0=re0, im0=im0, l0=l0):
                hpr, hpi = carry
                rows = pl.ds(pl.multiple_of(r * SUBLANES, SUBLANES), SUBLANES)
                xr = hs_ref[rows, re0:re0 + w]
                xi = hs_ref[rows, im0:im0 + w]
                for s in range(3):
                    sr = pltpu.roll(xr, 1 << s, 0)
                    si = pltpu.roll(xi, 1 << s, 0)
                    ar = st_ref[2 * s, :, l0:l0 + w]
                    ai = st_ref[2 * s + 1, :, l0:l0 + w]
                    xr, xi = xr + ar * sr - ai * si, xi + ar * si + ai * sr
                pr = pw_ref[0, :, l0:l0 + w]
                pi = pw_ref[1, :, l0:l0 + w]
                xr, xi = xr + pr * hpr - pi * hpi, xi + pr * hpi + pi * hpr
                hs_ref[rows, re0:re0 + w] = xr
                hs_ref[rows, im0:im0 + w] = xi
                return (jnp.broadcast_to(xr[SUBLANES - 1:SUBLANES, :], (SUBLANES, w)),
                        jnp.broadcast_to(xi[SUBLANES - 1:SUBLANES, :], (SUBLANES, w)))

            hpr, hpi = lax.fori_loop(0, tc // SUBLANES, body,
                                     (hp_ref[:, re0:re0 + w], hp_ref[:, im0:im0 + w]))
            hp_ref[:, re0:re0 + w] = hpr
            hp_ref[:, im0:im0 + w] = hpi

    ys = [_dot(hs_ref[:, kt * 2 * ns:(kt + 1) * 2 * ns].astype(BF16), wc_ref[kt]) for kt in range(S5_KT)]
    y = jnp.concatenate(ys, axis=1) + d_ref[...] * u.astype(F32)
    g = 0.5 * y * (1.0 + jnp.tanh(math.sqrt(2.0 / math.pi) * (y + 0.044715 * (y * y * y))))
    gl = _dot(g.astype(BF16), wglu_ref[...]) + bglu_ref[...]
    o_ref[...] = (g / (1.0 + jnp.exp(-gl))).astype(o_ref.dtype)


def _s5(h1, wbu, wc, lam, d_skip, w_glu, b_glu, tc):
    t = h1.shape[0]
    tc = min(tc, t)
    ns = S5_KT_STATES
    nstate = S5_GROUPS * S5_STATE
    full = lambda shape: pl.BlockSpec(shape, lambda i: (0,) * len(shape))
    return pl.pallas_call(
        functools.partial(_s5_kernel, tc=tc),
        grid=(t // tc,),
        in_specs=[
            pl.BlockSpec((tc, S5_CHANNELS), lambda i: (i, 3)),
            full((S5_KT, S5_KT_CH, 2 * ns)),
            full((S5_KT, 2 * ns, S5_KT_CH)),
            full((2, nstate)),
            full((1, S5_CHANNELS)),
            full((S5_CHANNELS, S5_CHANNELS)),
            full((1, S5_CHANNELS)),
        ],
        out_specs=pl.BlockSpec((tc, S5_CHANNELS), lambda i: (i, 0)),
        out_shape=jax.ShapeDtypeStruct((t, S5_CHANNELS), BF16),
        scratch_shapes=[pltpu.VMEM((tc, 2 * nstate), F32), pltpu.VMEM((SUBLANES, 2 * nstate), F32),
                        pltpu.VMEM((2, SUBLANES, nstate), F32), pltpu.VMEM((6, SUBLANES, nstate), F32)],
        compiler_params=_params(("arbitrary",)),
        name="s5",
    )(h1, wbu, wc, lam, d_skip, w_glu, b_glu)


def _route(probs, sel):
    rows = [sel[e:e + 1, :] for e in range(N_EXPERTS)]
    prow = [probs[e:e + 1, :] for e in range(N_EXPERTS)]
    best_score = None
    for g in range(N_GROUPS):
        a = rows[g * EXPERTS_PER_GROUP:(g + 1) * EXPERTS_PER_GROUP]
        score = None
        for x in range(EXPERTS_PER_GROUP):
            for y in range(x + 1, EXPERTS_PER_GROUP):
                pair = a[x] + a[y]
                score = pair if score is None else jnp.maximum(score, pair)
        if best_score is None:
            best_score, gbest = score, jnp.zeros(score.shape, jnp.int32)
            cs = list(a)
            cp = prow[0:EXPERTS_PER_GROUP]
        else:
            better = score > best_score
            best_score = jnp.where(better, score, best_score)
            gbest = jnp.where(better, g, gbest)
            cs = [jnp.where(better, a[x], cs[x]) for x in range(EXPERTS_PER_GROUP)]
            cp = [jnp.where(better, prow[g * EXPERTS_PER_GROUP + x], cp[x]) for x in range(EXPERTS_PER_GROUP)]

    def first_argmax(vals):
        bv, bi = vals[0], jnp.zeros(vals[0].shape, jnp.int32)
        for x in range(1, len(vals)):
            better = vals[x] > bv
            bv = jnp.where(better, vals[x], bv)
            bi = jnp.where(better, x, bi)
        return bi

    i1 = first_argmax(cs)
    i2 = first_argmax([jnp.where(i1 == x, -jnp.inf, cs[x]) for x in range(EXPERTS_PER_GROUP)])

    def pick(idx):
        out = cp[0]
        for x in range(1, EXPERTS_PER_GROUP):
            out = jnp.where(idx == x, cp[x], out)
        return out

    p1, p2 = pick(i1), pick(i2)
    tot = p1 + p2
    return gbest * EXPERTS_PER_GROUP + i1, gbest * EXPERTS_PER_GROUP + i2, p1 / tot, p2 / tot


def _ln_router_tail(v, g_ref, b_ref, rwt_ref, rb_ref):
    xn = _layer_norm(v, g_ref[...], b_ref[...])
    logits = lax.dot_general(rwt_ref[...], xn, (((1,), (1,)), ((), ())), preferred_element_type=F32,
                             precision=lax.Precision.HIGHEST)
    mx = jnp.max(logits, axis=0, keepdims=True)
    ex = jnp.exp(logits - mx)
    probs = ex / jnp.sum(ex, axis=0, keepdims=True)
    e1, e2, w1, w2 = _route(probs, probs + rb_ref[...])
    n = e1.shape[1]
    ei = jnp.concatenate([e1, e2, jnp.zeros((SUBLANES - 2, n), jnp.int32)], axis=0)
    ew = jnp.concatenate([w1, w2, jnp.zeros((SUBLANES - 2, n), F32)], axis=0)
    return xn, ei, ew


def _outproj_kernel(ma_ref, mb_ref, wa_ref, wb_ref, xr_ref, g_ref, b_ref, rwt_ref, rb_ref,
                    x_ref, ei_ref, ew_ref, *, sub):
    for s in range(x_ref.shape[0] // sub):
        rows = slice(s * sub, (s + 1) * sub)
        v = (DEEPNORM_ALPHA * xr_ref[rows, :] + _dot(ma_ref[rows, :], wa_ref[...])
             + _dot(mb_ref[rows, :], wb_ref[...]))
        xn, ei, ew = _ln_router_tail(v, g_ref, b_ref, rwt_ref, rb_ref)
        x_ref[rows, :] = xn
        ei_ref[:, rows] = ei
        ew_ref[:, rows] = ew


def _outproj_ln_router(mix_a, mix_b, w_out, x_res, ln_g, ln_b, rwt, rb, tm, sub):
    t = x_res.shape[0]
    tm = min(tm, t)
    sub = min(sub, tm)
    half = w_out.shape[0] // 2
    row = lambda w: pl.BlockSpec((tm, w), lambda i: (i, 0))
    full = lambda shape: pl.BlockSpec(shape, lambda i: (0,) * len(shape))
    return pl.pallas_call(
        functools.partial(_outproj_kernel, sub=sub),
        grid=(t // tm,),
        in_specs=[row(half), row(half),
                  pl.BlockSpec((half, D_MODEL), lambda i: (0, 0)), pl.BlockSpec((half, D_MODEL), lambda i: (1, 0)),
                  row(D_MODEL), full((1, D_MODEL)), full((1, D_MODEL)),
                  full((N_EXPERTS, D_MODEL)), full((N_EXPERTS, 1))],
        out_specs=[row(D_MODEL),
                   pl.BlockSpec((SUBLANES, tm), lambda i: (0, i)), pl.BlockSpec((SUBLANES, tm), lambda i: (0, i))],
        out_shape=[jax.ShapeDtypeStruct((t, D_MODEL), F32),
                   jax.ShapeDtypeStruct((SUBLANES, t), jnp.int32), jax.ShapeDtypeStruct((SUBLANES, t), F32)],
        compiler_params=_params(("parallel",)),
        name="outproj_ln_router",
    )(mix_a, mix_b, w_out, w_out, x_res, ln_g, ln_b, rwt, rb)


def _rank_kernel(e_ref, rank_ref, cnt_ref, base_ref, *, tn):
    @pl.when(pl.program_id(0) == 0)
    def _():
        base_ref[...] = jnp.zeros(base_ref.shape, F32)

    e = e_ref[...]
    rows = lax.broadcasted_iota(jnp.int32, (N_EXPERTS, tn), 0)
    oh1 = rows == e[0:1, :]
    oh2 = rows == e[1:2, :]
    oh = jnp.where(jnp.logical_or(oh1, oh2), 1.0, 0.0)
    r = lax.broadcasted_iota(jnp.int32, (tn, tn), 0)
    c = lax.broadcasted_iota(jnp.int32, (tn, tn), 1)
    earlier = jnp.where(r < c, 1.0, 0.0).astype(BF16)
    base = base_ref[...]
    before = _dot(oh.astype(BF16), earlier) + jnp.tile(base, (1, tn // LANES))
    r1 = jnp.sum(jnp.where(oh1, before, 0.0), axis=0, keepdims=True)
    r2 = jnp.sum(jnp.where(oh2, before, 0.0), axis=0, keepdims=True)
    rank_ref[...] = jnp.concatenate([r1, r2, jnp.zeros((SUBLANES - 2, tn), F32)], axis=0).astype(jnp.int32)
    base = base + jnp.sum(oh, axis=1, keepdims=True)
    base_ref[...] = base
    cnt_ref[...] = base.astype(jnp.int32)


def _rank(eidx, tn):
    t = eidx.shape[1]
    tn = min(tn, t)
    return pl.pallas_call(
        functools.partial(_rank_kernel, tn=tn),
        grid=(t // tn,),
        in_specs=[pl.BlockSpec((SUBLANES, tn), lambda i: (0, i))],
        out_specs=[pl.BlockSpec((SUBLANES, tn), lambda i: (0, i)),
                   pl.BlockSpec((N_EXPERTS, LANES), lambda i: (0, 0))],
        out_shape=[jax.ShapeDtypeStruct((SUBLANES, t), jnp.int32),
                   jax.ShapeDtypeStruct((N_EXPERTS, LANES), jnp.int32)],
        scratch_shapes=[pltpu.VMEM((N_EXPERTS, LANES), F32)],
        compiler_params=_params(("arbitrary",)),
        name="moe_rank",
    )(eidx)


def _moe_kernel(te_ref, nu_ref, xs_ref, wg_ref, wu_ref, wd_ref, y_ref, wgu_s, wd_s):
    tile = pl.program_id(0)
    used = tile < nu_ref[0]
    fresh = jnp.logical_or(tile == 0, te_ref[tile] != te_ref[jnp.maximum(tile - 1, 0)])

    @pl.when(jnp.logical_and(used, fresh))
    def _():
        wgu_s[:, :D_EXPERT] = wg_ref[...].astype(BF16)
        wgu_s[:, D_EXPERT:] = wu_ref[...].astype(BF16)
        wd_s[...] = wd_ref[...].astype(BF16)

    @pl.when(used)
    def _():
        hgu = _dot(xs_ref[...].astype(BF16), wgu_s[...])
        hg = hgu[:, :D_EXPERT]
        hu = hgu[:, D_EXPERT:]
        h = (hg / (1.0 + jnp.exp(-hg))) * hu
        y_ref[...] = _dot(h.astype(BF16), wd_s[...])

    @pl.when(jnp.logical_not(used))
    def _():
        y_ref[...] = jnp.zeros(y_ref.shape, F32)


def _moe(tile_expert, n_used, xs, w_gate, w_up, w_down, tm):
    t_pad = xs.shape[0]
    grid_spec = pltpu.PrefetchScalarGridSpec(
        num_scalar_prefetch=2,
        grid=(t_pad // tm,),
        in_specs=[
            pl.BlockSpec((tm, D_MODEL), lambda i, te, nu: (i, 0)),
            pl.BlockSpec((None, D_MODEL, D_EXPERT), lambda i, te, nu: (te[i], 0, 0)),
            pl.BlockSpec((None, D_MODEL, D_EXPERT), lambda i, te, nu: (te[i], 0, 0)),
            pl.BlockSpec((None, D_EXPERT, D_MODEL), lambda i, te, nu: (te[i], 0, 0)),
        ],
        out_specs=pl.BlockSpec((tm, D_MODEL), lambda i, te, nu: (i, 0)),
        scratch_shapes=[pltpu.VMEM((D_MODEL, 2 * D_EXPERT), BF16), pltpu.VMEM((D_EXPERT, D_MODEL), BF16)],
    )
    return pl.pallas_call(
        _moe_kernel,
        grid_spec=grid_spec,
        out_shape=jax.ShapeDtypeStruct((t_pad, D_MODEL), F32),
        compiler_params=_params(("arbitrary",)),
        name="moe",
    )(tile_expert, n_used, xs, w_gate, w_up, w_down)


def _moe_ln_kernel(x_ref, y1_ref, y2_ref, w_ref, g_ref, b_ref, o_ref):
    w = w_ref[...]
    v = DEEPNORM_ALPHA * x_ref[...] + (w[:, 0:1] * y1_ref[...] + w[:, 1:2] * y2_ref[...])
    o_ref[...] = _layer_norm(v, g_ref[...], b_ref[...])


def _moe_ln(x, y1, y2, w_cols, ln_g, ln_b, tm):
    t = x.shape[0]
    tm = min(tm, t)
    row = pl.BlockSpec((tm, D_MODEL), lambda i: (i, 0))
    vec = pl.BlockSpec((1, D_MODEL), lambda i: (0, 0))
    return pl.pallas_call(
        _moe_ln_kernel,
        grid=(t // tm,),
        in_specs=[row, row, row, pl.BlockSpec((tm, SUBLANES), lambda i: (i, 0)), vec, vec],
        out_specs=row,
        out_shape=jax.ShapeDtypeStruct((t, D_MODEL), F32),
        compiler_params=_params(("parallel",)),
        name="moe_ln",
    )(x, y1, y2, w_cols, ln_g, ln_b)


def _moe_block(x, eidx, ew, w_gate, w_up, w_down, ln_g, ln_b, tm):
    t = x.shape[0]
    tm = min(tm, t)
    rank, cnt = _rank(eidx, 512)
    counts = cnt[:, 0]
    padded = ((counts + tm - 1) // tm) * tm
    ends = jnp.cumsum(padded)
    starts = ends - padded
    e2 = eidx[:2]
    start_of = jnp.zeros_like(e2)
    for k in range(N_EXPERTS):
        start_of = jnp.where(e2 == k, starts[k], start_of)
    dest = start_of + rank[:2]
    t_pad = 2 * t + N_EXPERTS * tm
    tok = jnp.tile(jnp.arange(t, dtype=jnp.int32), 2)
    src_tok = (jnp.arange(t_pad, dtype=jnp.int32) % t).at[dest.reshape(2 * t)].set(
        tok, mode="promise_in_bounds", unique_indices=True)
    tile_start = jnp.arange(t_pad // tm, dtype=jnp.int32) * tm
    tile_expert = jnp.minimum(jnp.sum((ends[None, :] <= tile_start[:, None]).astype(jnp.int32), axis=1),
                              N_EXPERTS - 1)
    n_used = (ends[-1] // tm).astype(jnp.int32).reshape(1)
    xs = x.at[src_tok].get(mode="promise_in_bounds")
    ys = _moe(tile_expert, n_used, xs, w_gate, w_up, w_down, tm)
    y1 = ys.at[dest[0]].get(mode="promise_in_bounds")
    y2 = ys.at[dest[1]].get(mode="promise_in_bounds")
    return _moe_ln(x, y1, y2, ew.T, ln_g, ln_b, tm)


def _block_diag_tiles(m, rows_per_group, cols_per_group):
    gpt = S5_GROUPS // S5_KT
    m = m.reshape(S5_KT, gpt, rows_per_group, cols_per_group)
    eye = jnp.eye(gpt, dtype=m.dtype)
    bd = jnp.einsum("kgrc,gh->kgrhc", m, eye)
    return bd.reshape(S5_KT, gpt * rows_per_group, gpt * cols_per_group)


def kernel(x, router_w, router_b, l0_w_in, l0_diff_lam_q1, l0_diff_lam_k1, l0_diff_lam_q2, l0_diff_lam_k2, l0_diff_subln_g, l0_gla_w_gate, l0_gla_b_gate, l0_gla_norm_g, l0_w_out, l0_ln1_g, l0_ln1_b, l0_moe_w_gate, l0_moe_w_up, l0_moe_w_down, l0_ln2_g, l0_ln2_b, l1_w_in, l1_s5_a_re, l1_s5_a_im, l1_s5_log_step, l1_s5_b_re, l1_s5_b_im, l1_s5_c_re, l1_s5_c_im, l1_s5_d, l1_s5_w_glu, l1_s5_b_glu, l1_w_out, l1_ln1_g, l1_ln1_b, l1_moe_w_gate, l1_moe_w_up, l1_moe_w_down, l1_ln2_g, l1_ln2_b):
    bsz, t, d = x.shape
    assert bsz == 1 and d == D_MODEL
    x0 = x.reshape(t, d)
    vec = lambda p: p.reshape(1, -1).astype(F32)
    rwt = router_w.T.astype(F32)
    rb = router_b.reshape(N_EXPERTS, 1).astype(F32)

    lambda_init = 0.8 - 0.6 * math.exp(-0.3 * 0)
    c = [0, 1024, 2048, 3072, 3584, 4096, 5120, 5136, 6160]
    q_scale = math.log2(math.e) * DIFF_HEAD_DIM ** -0.5
    w0 = jnp.concatenate([l0_w_in[:, :c[1]] * q_scale, l0_w_in[:, c[1]:c[6]], l0_w_in[:, c[7]:c[8]],
                          l0_w_in[:, c[6]:c[7]], jnp.zeros((d, LANES - GLA_GATE_RANK), l0_w_in.dtype)],
                         axis=1).astype(BF16)
    h0 = _matmul(x0, w0, 1024, 896, BF16)
    lam_vec = jnp.stack([l0_diff_lam_q1, l0_diff_lam_k1, l0_diff_lam_q2, l0_diff_lam_k2]).astype(F32)
    o_a = _diff_attention(h0, lam_vec, vec(l0_diff_subln_g), lambda_init, 512)
    wg_pad = jnp.concatenate([l0_gla_w_gate, jnp.zeros((LANES - GLA_GATE_RANK, GLA_HEADS * GLA_DK), F32)],
                             axis=0).astype(BF16)
    o_b = _gla(h0, wg_pad, vec(l0_gla_b_gate), vec(l0_gla_norm_g), 256)
    x1, eidx, ew = _outproj_ln_router(o_a, o_b, l0_w_out.astype(BF16), x0, vec(l0_ln1_g), vec(l0_ln1_b),
                                      rwt, rb, 512, 256)
    x2 = _moe_block(x1, eidx, ew, l0_moe_w_gate, l0_moe_w_up, l0_moe_w_down, vec(l0_ln2_g), vec(l0_ln2_b), 256)

    h1 = _matmul(x2, l1_w_in.astype(BF16), 1024, 1024, BF16)
    o_c = _stick_breaking(h1, 256)
    lbr, lbi, bbr, bbi = _s5_prep(l1_s5_a_re, l1_s5_a_im, l1_s5_log_step, l1_s5_b_re, l1_s5_b_im)
    lam = jnp.stack([lbr[::S5_GROUP].reshape(-1), lbi[::S5_GROUP].reshape(-1)])
    bbr = bbr.reshape(S5_GROUPS, S5_GROUP, S5_STATE)
    bbi = bbi.reshape(S5_GROUPS, S5_GROUP, S5_STATE)
    wbu = jnp.concatenate([_block_diag_tiles(bbr, S5_GROUP, S5_STATE),
                           _block_diag_tiles(bbi, S5_GROUP, S5_STATE)], axis=2).astype(BF16)
    c_re_t = jnp.transpose(l1_s5_c_re, (0, 2, 1))
    c_im_t = jnp.transpose(l1_s5_c_im, (0, 2, 1))
    wc = jnp.concatenate([_block_diag_tiles(c_re_t, S5_STATE, S5_GROUP),
                          _block_diag_tiles(-c_im_t, S5_STATE, S5_GROUP)], axis=1).astype(BF16)
    o_d = _s5(h1, wbu, wc, lam, vec(l1_s5_d), l1_s5_w_glu.astype(BF16), vec(l1_s5_b_glu), 256)
    x3, eidx1, ew1 = _outproj_ln_router(o_c, o_d, l1_w_out.astype(BF16), x2, vec(l1_ln1_g), vec(l1_ln1_b),
                                        rwt, rb, 512, 256)
    x4 = _moe_block(x3, eidx1, ew1, l1_moe_w_gate, l1_moe_w_up, l1_moe_w_down, vec(l1_ln2_g), vec(l1_ln2_b), 256)
    return x4.reshape(bsz, t, d)
```

```python
import functools
import math

import jax
import jax.numpy as jnp
from jax import lax
from jax.experimental import pallas as pl
from jax.experimental.pallas import tpu as pltpu

F32 = jnp.float32
BF16 = jnp.bfloat16

D_MODEL = 2048
DEPTH = 2
DIFF_HEADS = 8
DIFF_HEAD_DIM = 64
DIFF_V_DIM = 128
GLA_HEADS = 4
GLA_DK = 128
GLA_DV = 256
GLA_GATE_RANK = 16
GLA_GATE_TEMP = 16.0
GLA_CHUNK = 64
SB_HEADS = 8
SB_HEAD_DIM = 128
S5_CHANNELS = 1024
S5_GROUP = 16
S5_GROUPS = 64
S5_STATE = 64
N_EXPERTS = 16
N_GROUPS = 4
EXPERTS_PER_GROUP = 4
D_EXPERT = 640
DEEPNORM_ALPHA = (2.0 * DEPTH) ** 0.25
LN_EPS = 1e-5

LANES = 128
SUBLANES = 8
VMEM_LIMIT = 56 * 1024 * 1024
SB_LOG_FLOOR = 120.0
SB_HEADS_PER_STEP = 2
DIFF_STRIP = 256
DIFF_UNROLL = 4

S5_KT = 4
S5_KT_CH = S5_CHANNELS // S5_KT
S5_KT_STATES = (S5_GROUPS // S5_KT) * S5_STATE
S5_SCAN_W = 512


def _params(sem, flags=None):
    return pltpu.CompilerParams(dimension_semantics=sem, vmem_limit_bytes=VMEM_LIMIT, flags=flags)


def _nt_dot(a, b):
    return lax.dot_general(a, b, (((1,), (1,)), ((), ())), preferred_element_type=F32)


def _dot(a, b):
    return jnp.dot(a, b, preferred_element_type=F32)


def _split_dot(x, m):
    hi = x.astype(BF16)
    lo = (x - hi.astype(F32)).astype(BF16)
    return _dot(hi, m) + _dot(lo, m)


def _split_dot_left(m, x):
    hi = x.astype(BF16)
    lo = (x - hi.astype(F32)).astype(BF16)
    return _dot(m, hi) + _dot(m, lo)


def _layer_norm(v, g, b):
    mu = jnp.mean(v, axis=-1, keepdims=True)
    d = v - mu
    var = jnp.mean(d * d, axis=-1, keepdims=True)
    return d * lax.rsqrt(var + LN_EPS) * g + b


def _mm_kernel(a_ref, b_ref, o_ref, a_bf):
    @pl.when(pl.program_id(1) == 0)
    def _():
        a_bf[...] = a_ref[...].astype(BF16)

    o_ref[...] = _dot(a_bf[...], b_ref[...]).astype(o_ref.dtype)


def _matmul(a, b, tm, tn, out_dtype):
    m, k = a.shape
    n = b.shape[1]
    tm = min(tm, m)
    return pl.pallas_call(
        _mm_kernel,
        grid=(m // tm, n // tn),
        in_specs=[pl.BlockSpec((tm, k), lambda i, j: (i, 0)),
                  pl.BlockSpec((k, tn), lambda i, j: (0, j))],
        out_specs=pl.BlockSpec((tm, tn), lambda i, j: (i, j)),
        out_shape=jax.ShapeDtypeStruct((m, n), out_dtype),
        scratch_shapes=[pltpu.VMEM((tm, k), BF16)],
        compiler_params=_params(("parallel", "arbitrary")),
        name="in_proj",
    )(a, b)


def _diff_attn_kernel(lam_ref, g_ref, qt_ref, k_ref, vt_ref, o_ref, vx_ref, m_ref, acc_ref, sa_ref, sb_ref,
                      pa_ref, pb_ref, ala_ref, alb_ref, *, tq, lambda_init):
    i = pl.program_id(1)
    dv = DIFF_V_DIM
    nr = vx_ref.shape[1]

    @pl.when(i == 0)
    def _():
        for b in range(vx_ref.shape[0]):
            vx_ref[b, :dv, :] = vt_ref[:, b * tq:(b + 1) * tq]
            vx_ref[b, dv:, :] = jnp.ones((nr - dv, tq), BF16)

    m_ref[...] = jnp.full(m_ref.shape, -jnp.inf, F32)
    acc_ref[...] = jnp.zeros(acc_ref.shape, F32)
    qt = qt_ref[...]
    feat = lax.broadcasted_iota(jnp.int32, qt.shape, 0)
    zero = jnp.zeros_like(qt)
    halves = (jnp.where(feat < DIFF_HEAD_DIM, qt, zero), jnp.where(feat >= DIFF_HEAD_DIM, qt, zero))

    s_bufs = (sa_ref, sb_ref)
    p_bufs = (pa_ref, pb_ref)
    al_bufs = (ala_ref, alb_ref)

    units = [(mi, slice(c * DIFF_STRIP, (c + 1) * DIFF_STRIP)) for mi in range(2) for c in range(tq // DIFF_STRIP)]

    def qk_unit(j, par, mi, cols):
        off = pl.multiple_of(j * tq, tq)
        s_bufs[par][mi, :, cols] = _dot(k_ref[pl.ds(off, tq), :], halves[mi][:, cols])

    def softmax_unit(par, masked, mi, cols):
        s = s_bufs[par][mi, :, cols]
        if masked:
            key = lax.broadcasted_iota(jnp.int32, s.shape, 0)
            qry = lax.broadcasted_iota(jnp.int32, s.shape, 1) + cols.start
            s = jnp.where(key <= qry, s, -jnp.inf)
        m_prev = m_ref[mi, :, cols]
        m_new = jnp.maximum(m_prev, jnp.max(s, axis=0, keepdims=True))
        p_bufs[par][mi, :, cols] = jnp.exp2(s - m_new[0:1, :]).astype(BF16)
        al_bufs[par][mi, :, cols] = jnp.exp2(m_prev - m_new)
        m_ref[mi, :, cols] = m_new

    def pv_unit(j, par, mi, cols):
        acc_ref[mi, :, cols] = (al_bufs[par][mi, 0:1, cols] * acc_ref[mi, :, cols]
                                + _dot(vx_ref[j], p_bufs[par][mi, :, cols]))

    def qk(j, par):
        for mi, cols in units:
            qk_unit(j, par, mi, cols)

    def softmax(par, masked):
        for mi, cols in units:
            softmax_unit(par, masked, mi, cols)

    def pv(j, par):
        for mi, cols in units:
            pv_unit(j, par, mi, cols)

    def step(j, par):
        for mi, cols in units:
            pv_unit(j, par, mi, cols)
            softmax_unit(1 - par, False, mi, cols)
            qk_unit(j + 2, par, mi, cols)

    def drain(par):
        for mi, cols in units:
            pv_unit(i - 1, par, mi, cols)
            softmax_unit(1 - par, True, mi, cols)
        pv(i, 1 - par)

    @pl.when(i == 0)
    def _():
        qk(0, 0)
        softmax(0, True)
        pv(0, 0)

    @pl.when(i >= 1)
    def _():
        qk(0, 0)
        softmax(0, False)
        qk(1, 1)

        n_steps = i - 1
        n_loops = n_steps // DIFF_UNROLL

        def body(t, carry):
            for u in range(DIFF_UNROLL):
                step(DIFF_UNROLL * t + u, u % 2)
            return carry

        lax.fori_loop(0, n_loops, body, 0)
        done = n_loops * DIFF_UNROLL
        for u in range(DIFF_UNROLL - 1):
            @pl.when(n_steps - done > u)
            def _(u=u):
                step(done + u, u % 2)

        @pl.when(n_steps % 2 == 1)
        def _():
            drain(1)

        @pl.when(n_steps % 2 == 0)
        def _():
            drain(0)

    lv = lam_ref[...]
    lam = (jnp.exp(jnp.sum(lv[0:1] * lv[1:2], axis=1, keepdims=True))
           - jnp.exp(jnp.sum(lv[2:3] * lv[3:4], axis=1, keepdims=True)) + lambda_init)
    a0 = acc_ref[0]
    a1 = acc_ref[1]
    o = a0[:dv, :] / a0[dv:dv + 1, :] - lam * (a1[:dv, :] / a1[dv:dv + 1, :])
    ms = jnp.mean(o * o, axis=0, keepdims=True)
    o = o * lax.rsqrt(ms + LN_EPS) * g_ref[...] * (1.0 - lambda_init)
    o_ref[...] = o.astype(o_ref.dtype)


def _diff_attention(h0, lam_vec, subln_g, lambda_init, tq):
    t = h0.shape[0]
    tq = min(tq, t)
    nq = t // tq
    qw = DIFF_HEADS * 2 * DIFF_HEAD_DIM
    q_t = h0[:, :qw].T
    v_t = h0[:, 2 * qw:2 * qw + DIFF_HEADS * DIFF_V_DIM].T
    ones_rows = 2 * SUBLANES
    kern = functools.partial(_diff_attn_kernel, tq=tq, lambda_init=lambda_init)
    return pl.pallas_call(
        kern,
        grid=(DIFF_HEADS, nq),
        in_specs=[
            pl.BlockSpec((4, DIFF_HEAD_DIM), lambda h, i: (0, 0)),
            pl.BlockSpec((DIFF_V_DIM, 1), lambda h, i: (0, 0)),
            pl.BlockSpec((LANES, tq), lambda h, i: (h, i)),
            pl.BlockSpec((t, LANES), lambda h, i: (0, DIFF_HEADS + h)),
            pl.BlockSpec((DIFF_V_DIM, t), lambda h, i: (h, 0)),
        ],
        out_specs=pl.BlockSpec((DIFF_V_DIM, tq), lambda h, i: (h, i)),
        out_shape=jax.ShapeDtypeStruct((DIFF_HEADS * DIFF_V_DIM, t), BF16),
        scratch_shapes=[pltpu.VMEM((nq, DIFF_V_DIM + ones_rows, tq), BF16), pltpu.VMEM((2, SUBLANES, tq), F32),
                        pltpu.VMEM((2, DIFF_V_DIM + ones_rows, tq), F32),
                        pltpu.VMEM((2, tq, tq), F32), pltpu.VMEM((2, tq, tq), F32),
                        pltpu.VMEM((2, tq, tq), BF16), pltpu.VMEM((2, tq, tq), BF16),
                        pltpu.VMEM((2, SUBLANES, tq), F32), pltpu.VMEM((2, SUBLANES, tq), F32)],
        compiler_params=_params(("arbitrary", "arbitrary")),
        name="diff_attn",
    )(lam_vec, subln_g.reshape(DIFF_V_DIM, 1), q_t, h0, v_t)


def _gla_kernel(q_ref, k_ref, v_ref, r_ref, gb_ref, wg_ref, bg_ref, ng_ref, o_ref, st_ref, *, tg):
    @pl.when(pl.program_id(0) == 0)
    def _():
        st_ref[...] = jnp.zeros(st_ref.shape, F32)

    c = GLA_CHUNK
    gate = _dot(gb_ref[...], wg_ref[...]) + bg_ref[...]
    log_a = -(jnp.maximum(-gate, 0.0) + jnp.log(1.0 + jnp.exp(-jnp.abs(gate)))) / GLA_GATE_TEMP
    row = lax.broadcasted_iota(jnp.int32, (c, c), 0)
    col = lax.broadcasted_iota(jnp.int32, (c, c), 1)
    tril = row >= col
    tri = jnp.where(tril, 1.0, 0.0).astype(BF16)
    for h in range(GLA_HEADS):
        ksl = slice(h * GLA_DK, (h + 1) * GLA_DK)
        vsl = slice(h * GLA_DV, (h + 1) * GLA_DV)
        for ci in range(tg // c):
            rs = slice(ci * c, (ci + 1) * c)
            la = log_a[rs, ksl]
            b = _split_dot_left(tri, la)
            b_last = b[c - 1:c, :]
            qf = q_ref[rs, ksl].astype(F32) * (GLA_DK ** -0.5)
            kf = k_ref[rs, ksl].astype(F32)
            v = v_ref[rs, vsl]
            q_dec = (qf * jnp.exp(b)).astype(BF16)
            k_dec = (kf * jnp.exp(-b)).astype(BF16)
            k_end = (kf * jnp.exp(b_last - b)).astype(BF16)
            scores = jnp.where(tril, _nt_dot(q_dec, k_dec), 0.0)
            st = st_ref[h]
            o = _dot(scores.astype(BF16), v) + _nt_dot(q_dec, st.astype(BF16))
            kv_t = lax.dot_general(v, k_end, (((0,), (0,)), ((), ())), preferred_element_type=F32)
            st_ref[h] = jnp.exp(b_last) * st + kv_t
            ms = jnp.mean(o * o, axis=-1, keepdims=True)
            o = o * lax.rsqrt(ms + LN_EPS) * ng_ref[...]
            r = r_ref[rs, vsl].astype(F32)
            o_ref[rs, vsl] = (o * (r / (1.0 + jnp.exp(-r)))).astype(o_ref.dtype)


def _gla(h0, wg_pad, b_gate, norm_g, tg):
    t = h0.shape[0]
    tg = min(tg, t)
    qk = GLA_HEADS * GLA_DK
    vw = GLA_HEADS * GLA_DV
    return pl.pallas_call(
        functools.partial(_gla_kernel, tg=tg),
        grid=(t // tg,),
        in_specs=[
            pl.BlockSpec((tg, qk), lambda i: (i, 3072 // qk)),
            pl.BlockSpec((tg, qk), lambda i: (i, 3584 // qk)),
            pl.BlockSpec((tg, vw), lambda i: (i, 4096 // vw)),
            pl.BlockSpec((tg, vw), lambda i: (i, 5120 // vw)),
            pl.BlockSpec((tg, LANES), lambda i: (i, 6144 // LANES)),
            pl.BlockSpec((LANES, qk), lambda i: (0, 0)),
            pl.BlockSpec((1, qk), lambda i: (0, 0)),
            pl.BlockSpec((1, GLA_DV), lambda i: (0, 0)),
        ],
        out_specs=pl.BlockSpec((tg, vw), lambda i: (i, 0)),
        out_shape=jax.ShapeDtypeStruct((t, vw), BF16),
        scratch_shapes=[pltpu.VMEM((GLA_HEADS, GLA_DV, GLA_DK), F32)],
        compiler_params=_params(("arbitrary",)),
        name="gla",
    )(h0, h0, h0, h0, h0, wg_pad, b_gate, norm_g)


def _sb_kernel(q_ref, k_ref, v_ref, o_ref, acc_ref, c_ref, *, tq):
    i = pl.program_id(1)
    hd = SB_HEAD_DIM
    acc_ref[...] = jnp.zeros(acc_ref.shape, F32)
    c_ref[...] = jnp.zeros(c_ref.shape, F32)
    row = lax.broadcasted_iota(jnp.int32, (tq, tq), 0)
    col = lax.broadcasted_iota(jnp.int32, (tq, tq), 1)
    strict = col < row
    tri = jnp.where(row > col, 1.0, 0.0).astype(BF16)

    def block(jb, masked):
        off = pl.multiple_of(jb * tq, tq)
        cmax = None
        for hh in range(SB_HEADS_PER_STEP):
            cols = slice(hh * hd, (hh + 1) * hd)
            k = k_ref[pl.ds(off, tq), cols]
            v = v_ref[pl.ds(off, tq), cols]
            z = _nt_dot(q_ref[:, cols], k) * (SB_HEAD_DIM ** -0.5)
            lsn = -(jnp.maximum(z, 0.0) + jnp.log(1.0 + jnp.exp(-jnp.abs(z))))
            lm = jnp.where(strict, lsn, 0.0) if masked else lsn
            excl = _split_dot(lm, tri) + c_ref[hh]
            w = jnp.exp(z + lsn + excl)
            if masked:
                w = jnp.where(strict, w, 0.0)
            acc_ref[:, cols] += _dot(w.astype(BF16), v)
            c_new = c_ref[hh] + jnp.sum(lm, axis=1, keepdims=True)
            c_ref[hh] = c_new
            hmax = jnp.max(c_new)
            cmax = hmax if cmax is None else jnp.maximum(cmax, hmax)
        return cmax

    cmax = block(i, True)

    def cond(carry):
        jb, cm = carry
        return jnp.logical_and(jb >= 0, cm > -SB_LOG_FLOOR)

    def body(carry):
        jb, _ = carry
        return jb - 1, block(jb, False)

    lax.while_loop(cond, body, (i - 1, cmax))
    o_ref[...] = acc_ref[...].astype(o_ref.dtype)


def _stick_breaking(h1, tq):
    t = h1.shape[0]
    tq = min(tq, t)
    ng = SB_HEADS // SB_HEADS_PER_STEP
    wd = SB_HEADS_PER_STEP * SB_HEAD_DIM
    return pl.pallas_call(
        functools.partial(_sb_kernel, tq=tq),
        grid=(ng, t // tq),
        in_specs=[
            pl.BlockSpec((tq, wd), lambda h, i: (i, h)),
            pl.BlockSpec((t, wd), lambda h, i: (0, ng + h)),
            pl.BlockSpec((t, wd), lambda h, i: (0, 2 * ng + h)),
        ],
        out_specs=pl.BlockSpec((tq, wd), lambda h, i: (i, h)),
        out_shape=jax.ShapeDtypeStruct((t, SB_HEADS * SB_HEAD_DIM), BF16),
        scratch_shapes=[pltpu.VMEM((tq, wd), F32), pltpu.VMEM((SB_HEADS_PER_STEP, tq, 1), F32)],
        compiler_params=_params(("parallel", "arbitrary")),
        name="stick_breaking",
    )(h1, h1, h1)


def _s5_prep_kernel(are_ref, aim_ref, ls_ref, bre_ref, bim_ref, lbr_ref, lbi_ref, bbr_ref, bbi_ref):
    step = jnp.exp(ls_ref[...])
    lam_re = are_ref[...]
    lam_im = aim_ref[...]
    mag = jnp.exp(step * lam_re)
    lb_re = mag * jnp.cos(step * lam_im)
    lb_im = mag * jnp.sin(step * lam_im)
    den = lam_re * lam_re + lam_im * lam_im
    n_re = lb_re - 1.0
    f_re = (n_re * lam_re + lb_im * lam_im) / den
    f_im = (lb_im * lam_re - n_re * lam_im) / den
    lbr_ref[...] = lb_re
    lbi_ref[...] = lb_im
    bbr_ref[...] = f_re * bre_ref[...] - f_im * bim_ref[...]
    bbi_ref[...] = f_re * bim_ref[...] + f_im * bre_ref[...]


def _s5_prep(a_re, a_im, log_step, b_re, b_im):
    rows = S5_GROUPS * S5_GROUP
    rep = lambda p: jnp.repeat(p, S5_GROUP, axis=0)
    to_rows = lambda b: jnp.transpose(b, (0, 2, 1)).reshape(rows, S5_STATE)
    spec = pl.BlockSpec((rows, S5_STATE), lambda: (0, 0))
    shp = jax.ShapeDtypeStruct((rows, S5_STATE), F32)
    return pl.pallas_call(
        _s5_prep_kernel,
        in_specs=[spec, spec, pl.BlockSpec((rows, 1), lambda: (0, 0)), spec, spec],
        out_specs=[spec] * 4,
        out_shape=[shp] * 4,
        name="s5_prep",
    )(rep(a_re), rep(a_im), rep(log_step[:, None]), to_rows(b_re), to_rows(b_im))


def _cmul(ar, ai, br, bi):
    return ar * br - ai * bi, ar * bi + ai * br


def _s5_kernel(u_ref, wbu_ref, wc_ref, lam_ref, d_ref, wglu_ref, bglu_ref, o_ref,
               hs_ref, hp_ref, pw_ref, st_ref, *, tc):
    ns = S5_KT_STATES

    @pl.when(pl.program_id(0) == 0)
    def _():
        hp_ref[...] = jnp.zeros(hp_ref.shape, F32)
        l1r = jnp.broadcast_to(lam_ref[0:1, :], (SUBLANES, lam_ref.shape[1]))
        l1i = jnp.broadcast_to(lam_ref[1:2, :], (SUBLANES, lam_ref.shape[1]))
        l2r, l2i = _cmul(l1r, l1i, l1r, l1i)
        l4r, l4i = _cmul(l2r, l2i, l2r, l2i)
        l8r, l8i = _cmul(l4r, l4i, l4r, l4i)
        ridx = lax.broadcasted_iota(jnp.int32, l1r.shape, 0)
        n = ridx + 1
        pr = jnp.ones_like(l1r)
        pi = jnp.zeros_like(l1r)
        for bit, (lr, li) in ((1, (l1r, l1i)), (2, (l2r, l2i)), (4, (l4r, l4i)), (8, (l8r, l8i))):
            nr, ni = _cmul(pr, pi, lr, li)
            take = (n & bit) != 0
            pr = jnp.where(take, nr, pr)
            pi = jnp.where(take, ni, pi)
        pw_ref[0] = pr
        pw_ref[1] = pi
        for s, (lr, li) in enumerate(((l1r, l1i), (l2r, l2i), (l4r, l4i))):
            keep = ridx >= (1 << s)
            st_ref[2 * s] = jnp.where(keep, lr, 0.0)
            st_ref[2 * s + 1] = jnp.where(keep, li, 0.0)

    u = u_ref[...]
    for kt in range(S5_KT):
        hs_ref[:, kt * 2 * ns:(kt + 1) * 2 * ns] = _dot(u[:, kt * S5_KT_CH:(kt + 1) * S5_KT_CH], wbu_ref[kt])

    w = S5_SCAN_W
    for kt in range(S5_KT):
        for cc in range(ns // w):
            re0 = kt * 2 * ns + cc * w
            im0 = re0 + ns
            l0 = kt * ns + cc * w

            def body(r, carry, re0=re0, im0=im0, l0=l0):
                hpr, hpi = carry
                rows = pl.ds(pl.multiple_of(r * SUBLANES, SUBLANES), SUBLANES)
                xr = hs_ref[rows, re0:re0 + w]
                xi = hs_ref[rows, im0:im0 + w]
                for s in range(3):
                    sr = pltpu.roll(xr, 1 << s, 0)
                    si = pltpu.roll(xi, 1 << s, 0)
                    ar = st_ref[2 * s, :, l0:l0 + w]
                    ai = st_ref[2 * s + 1, :, l0:l0 + w]
                    xr, xi = xr + ar * sr - ai * si, xi + ar * si + ai * sr
                pr = pw_ref[0, :, l0:l0 + w]
                pi = pw_ref[1, :, l0:l0 + w]
                xr, xi = xr + pr * hpr - pi * hpi, xi + pr * hpi + pi * hpr
                hs_ref[rows, re0:re0 + w] = xr
                hs_ref[rows, im0:im0 + w] = xi
                return (jnp.broadcast_to(xr[SUBLANES - 1:SUBLANES, :], (SUBLANES, w)),
                        jnp.broadcast_to(xi[SUBLANES - 1:SUBLANES, :], (SUBLANES, w)))

            hpr, hpi = lax.fori_loop(0, tc // SUBLANES, body,
                                     (hp_ref[:, re0:re0 + w], hp_ref[:, im0:im0 + w]))
            hp_ref[:, re0:re0 + w] = hpr
            hp_ref[:, im0:im0 + w] = hpi

    ys = [_dot(hs_ref[:, kt * 2 * ns:(kt + 1) * 2 * ns].astype(BF16), wc_ref[kt]) for kt in range(S5_KT)]
    y = jnp.concatenate(ys, axis=1) + d_ref[...] * u.astype(F32)
    g = 0.5 * y * (1.0 + jnp.tanh(math.sqrt(2.0 / math.pi) * (y + 0.044715 * (y * y * y))))
    gl = _dot(g.astype(BF16), wglu_ref[...]) + bglu_ref[...]
    o_ref[...] = (g / (1.0 + jnp.exp(-gl))).astype(o_ref.dtype)


def _s5(h1, wbu, wc, lam, d_skip, w_glu, b_glu, tc):
    t = h1.shape[0]
    tc = min(tc, t)
    ns = S5_KT_STATES
    nstate = S5_GROUPS * S5_STATE
    full = lambda shape: pl.BlockSpec(shape, lambda i: (0,) * len(shape))
    return pl.pallas_call(
        functools.partial(_s5_kernel, tc=tc),
        grid=(t // tc,),
        in_specs=[
            pl.BlockSpec((tc, S5_CHANNELS), lambda i: (i, 3)),
            full((S5_KT, S5_KT_CH, 2 * ns)),
            full((S5_KT, 2 * ns, S5_KT_CH)),
            full((2, nstate)),
            full((1, S5_CHANNELS)),
            full((S5_CHANNELS, S5_CHANNELS)),
            full((1, S5_CHANNELS)),
        ],
        out_specs=pl.BlockSpec((tc, S5_CHANNELS), lambda i: (i, 0)),
        out_shape=jax.ShapeDtypeStruct((t, S5_CHANNELS), BF16),
        scratch_shapes=[pltpu.VMEM((tc, 2 * nstate), F32), pltpu.VMEM((SUBLANES, 2 * nstate), F32),
                        pltpu.VMEM((2, SUBLANES, nstate), F32), pltpu.VMEM((6, SUBLANES, nstate), F32)],
        compiler_params=_params(("arbitrary",)),
        name="s5",
    )(h1, wbu, wc, lam, d_skip, w_glu, b_glu)


def _route(probs, sel):
    rows = [sel[e:e + 1, :] for e in range(N_EXPERTS)]
    prow = [probs[e:e + 1, :] for e in range(N_EXPERTS)]
    best_score = None
    for g in range(N_GROUPS):
        a = rows[g * EXPERTS_PER_GROUP:(g + 1) * EXPERTS_PER_GROUP]
        score = None
        for x in range(EXPERTS_PER_GROUP):
            for y in range(x + 1, EXPERTS_PER_GROUP):
                pair = a[x] + a[y]
                score = pair if score is None else jnp.maximum(score, pair)
        if best_score is None:
            best_score, gbest = score, jnp.zeros(score.shape, jnp.int32)
            cs = list(a)
            cp = prow[0:EXPERTS_PER_GROUP]
        else:
            better = score > best_score
            best_score = jnp.where(better, score, best_score)
            gbest = jnp.where(better, g, gbest)
            cs = [jnp.where(better, a[x], cs[x]) for x in range(EXPERTS_PER_GROUP)]
            cp = [jnp.where(better, prow[g * EXPERTS_PER_GROUP + x], cp[x]) for x in range(EXPERTS_PER_GROUP)]

    def first_argmax(vals):
        bv, bi = vals[0], jnp.zeros(vals[0].shape, jnp.int32)
        for x in range(1, len(vals)):
            better = vals[x] > bv
            bv = jnp.where(better, vals[x], bv)
            bi = jnp.where(better, x, bi)
        return bi

    i1 = first_argmax(cs)
    i2 = first_argmax([jnp.where(i1 == x, -jnp.inf, cs[x]) for x in range(EXPERTS_PER_GROUP)])

    def pick(idx):
        out = cp[0]
        for x in range(1, EXPERTS_PER_GROUP):
            out = jnp.where(idx == x, cp[x], out)
        return out

    p1, p2 = pick(i1), pick(i2)
    tot = p1 + p2
    return gbest * EXPERTS_PER_GROUP + i1, gbest * EXPERTS_PER_GROUP + i2, p1 / tot, p2 / tot


def _ln_router_tail(v, g_ref, b_ref, rwt_ref, rb_ref):
    xn = _layer_norm(v, g_ref[...], b_ref[...])
    logits = lax.dot_general(rwt_ref[...], xn, (((1,), (1,)), ((), ())), preferred_element_type=F32,
                             precision=lax.Precision.HIGHEST)
    mx = jnp.max(logits, axis=0, keepdims=True)
    ex = jnp.exp(logits - mx)
    probs = ex / jnp.sum(ex, axis=0, keepdims=True)
    e1, e2, w1, w2 = _route(probs, probs + rb_ref[...])
    n = e1.shape[1]
    ei = jnp.concatenate([e1, e2, jnp.zeros((SUBLANES - 2, n), jnp.int32)], axis=0)
    ew = jnp.concatenate([w1, w2, jnp.zeros((SUBLANES - 2, n), F32)], axis=0)
    return xn, ei, ew


def _outproj_kernel(ma_ref, mb_ref, wa_ref, wb_ref, xr_ref, g_ref, b_ref, rwt_ref, rb_ref,
                    x_ref, ei_ref, ew_ref, *, sub):
    for s in range(x_ref.shape[0] // sub):
        rows = slice(s * sub, (s + 1) * sub)
        v = (DEEPNORM_ALPHA * xr_ref[rows, :] + _dot(ma_ref[rows, :], wa_ref[...])
             + _dot(mb_ref[rows, :], wb_ref[...]))
        xn, ei, ew = _ln_router_tail(v, g_ref, b_ref, rwt_ref, rb_ref)
        x_ref[rows, :] = xn
        ei_ref[:, rows] = ei
        ew_ref[:, rows] = ew


def _outproj_ln_router(mix_a, mix_b, w_out, x_res, ln_g, ln_b, rwt, rb, tm, sub):
    t = x_res.shape[0]
    tm = min(tm, t)
    sub = min(sub, tm)
    half = w_out.shape[0] // 2
    row = lambda w: pl.BlockSpec((tm, w), lambda i: (i, 0))
    full = lambda shape: pl.BlockSpec(shape, lambda i: (0,) * len(shape))
    return pl.pallas_call(
        functools.partial(_outproj_kernel, sub=sub),
        grid=(t // tm,),
        in_specs=[row(half), row(half),
                  pl.BlockSpec((half, D_MODEL), lambda i: (0, 0)), pl.BlockSpec((half, D_MODEL), lambda i: (1, 0)),
                  row(D_MODEL), full((1, D_MODEL)), full((1, D_MODEL)),
                  full((N_EXPERTS, D_MODEL)), full((N_EXPERTS, 1))],
        out_specs=[row(D_MODEL),
                   pl.BlockSpec((SUBLANES, tm), lambda i: (0, i)), pl.BlockSpec((SUBLANES, tm), lambda i: (0, i))],
        out_shape=[jax.ShapeDtypeStruct((t, D_MODEL), F32),
                   jax.ShapeDtypeStruct((SUBLANES, t), jnp.int32), jax.ShapeDtypeStruct((SUBLANES, t), F32)],
        compiler_params=_params(("parallel",)),
        name="outproj_ln_router",
    )(mix_a, mix_b, w_out, w_out, x_res, ln_g, ln_b, rwt, rb)


def _rank_kernel(e_ref, rank_ref, cnt_ref, base_ref, *, tn):
    @pl.when(pl.program_id(0) == 0)
    def _():
        base_ref[...] = jnp.zeros(base_ref.shape, F32)

    e = e_ref[...]
    rows = lax.broadcasted_iota(jnp.int32, (N_EXPERTS, tn), 0)
    oh1 = rows == e[0:1, :]
    oh2 = rows == e[1:2, :]
    oh = jnp.where(jnp.logical_or(oh1, oh2), 1.0, 0.0)
    r = lax.broadcasted_iota(jnp.int32, (tn, tn), 0)
    c = lax.broadcasted_iota(jnp.int32, (tn, tn), 1)
    earlier = jnp.where(r < c, 1.0, 0.0).astype(BF16)
    base = base_ref[...]
    before = _dot(oh.astype(BF16), earlier) + jnp.tile(base, (1, tn // LANES))
    r1 = jnp.sum(jnp.where(oh1, before, 0.0), axis=0, keepdims=True)
    r2 = jnp.sum(jnp.where(oh2, before, 0.0), axis=0, keepdims=True)
    rank_ref[...] = jnp.concatenate([r1, r2, jnp.zeros((SUBLANES - 2, tn), F32)], axis=0).astype(jnp.int32)
    base = base + jnp.sum(oh, axis=1, keepdims=True)
    base_ref[...] = base
    cnt_ref[...] = base.astype(jnp.int32)


def _rank(eidx, tn):
    t = eidx.shape[1]
    tn = min(tn, t)
    return pl.pallas_call(
        functools.partial(_rank_kernel, tn=tn),
        grid=(t // tn,),
        in_specs=[pl.BlockSpec((SUBLANES, tn), lambda i: (0, i))],
        out_specs=[pl.BlockSpec((SUBLANES, tn), lambda i: (0, i)),
                   pl.BlockSpec((N_EXPERTS, LANES), lambda i: (0, 0))],
        out_shape=[jax.ShapeDtypeStruct((SUBLANES, t), jnp.int32),
                   jax.ShapeDtypeStruct((N_EXPERTS, LANES), jnp.int32)],
        scratch_shapes=[pltpu.VMEM((N_EXPERTS, LANES), F32)],
        compiler_params=_params(("arbitrary",)),
        name="moe_rank",
    )(eidx)


def _moe_kernel(te_ref, nu_ref, xs_ref, wg_ref, wu_ref, wd_ref, y_ref, wgu_s, wd_s):
    tile = pl.program_id(0)
    used = tile < nu_ref[0]
    fresh = jnp.logical_or(tile == 0, te_ref[tile] != te_ref[jnp.maximum(tile - 1, 0)])

    @pl.when(jnp.logical_and(used, fresh))
    def _():
        wgu_s[:, :D_EXPERT] = wg_ref[...].astype(BF16)
        wgu_s[:, D_EXPERT:] = wu_ref[...].astype(BF16)
        wd_s[...] = wd_ref[...].astype(BF16)

    @pl.when(used)
    def _():
        hgu = _dot(xs_ref[...].astype(BF16), wgu_s[...])
        hg = hgu[:, :D_EXPERT]
        hu = hgu[:, D_EXPERT:]
        h = (hg / (1.0 + jnp.exp(-hg))) * hu
        y_ref[...] = _dot(h.astype(BF16), wd_s[...])

    @pl.when(jnp.logical_not(used))
    def _():
        y_ref[...] = jnp.zeros(y_ref.shape, F32)


def _moe(tile_expert, n_used, xs, w_gate, w_up, w_down, tm):
    t_pad = xs.shape[0]
    grid_spec = pltpu.PrefetchScalarGridSpec(
        num_scalar_prefetch=2,
        grid=(t_pad // tm,),
        in_specs=[
            pl.BlockSpec((tm, D_MODEL), lambda i, te, nu: (i, 0)),
            pl.BlockSpec((None, D_MODEL, D_EXPERT), lambda i, te, nu: (te[i], 0, 0)),
            pl.BlockSpec((None, D_MODEL, D_EXPERT), lambda i, te, nu: (te[i], 0, 0)),
            pl.BlockSpec((None, D_EXPERT, D_MODEL), lambda i, te, nu: (te[i], 0, 0)),
        ],
        out_specs=pl.BlockSpec((tm, D_MODEL), lambda i, te, nu: (i, 0)),
        scratch_shapes=[pltpu.VMEM((D_MODEL, 2 * D_EXPERT), BF16), pltpu.VMEM((D_EXPERT, D_MODEL), BF16)],
    )
    return pl.pallas_call(
        _moe_kernel,
        grid_spec=grid_spec,
        out_shape=jax.ShapeDtypeStruct((t_pad, D_MODEL), F32),
        compiler_params=_params(("arbitrary",)),
        name="moe",
    )(tile_expert, n_used, xs, w_gate, w_up, w_down)


def _moe_ln_kernel(x_ref, y1_ref, y2_ref, w_ref, g_ref, b_ref, o_ref):
    w = w_ref[...]
    v = DEEPNORM_ALPHA * x_ref[...] + (w[:, 0:1] * y1_ref[...] + w[:, 1:2] * y2_ref[...])
    o_ref[...] = _layer_norm(v, g_ref[...], b_ref[...])


def _moe_ln(x, y1, y2, w_cols, ln_g, ln_b, tm):
    t = x.shape[0]
    tm = min(tm, t)
    row = pl.BlockSpec((tm, D_MODEL), lambda i: (i, 0))
    vec = pl.BlockSpec((1, D_MODEL), lambda i: (0, 0))
    return pl.pallas_call(
        _moe_ln_kernel,
        grid=(t // tm,),
        in_specs=[row, row, row, pl.BlockSpec((tm, SUBLANES), lambda i: (i, 0)), vec, vec],
        out_specs=row,
        out_shape=jax.ShapeDtypeStruct((t, D_MODEL), F32),
        compiler_params=_params(("parallel",)),
        name="moe_ln",
    )(x, y1, y2, w_cols, ln_g, ln_b)


def _moe_block(x, eidx, ew, w_gate, w_up, w_down, ln_g, ln_b, tm):
    t = x.shape[0]
    tm = min(tm, t)
    rank, cnt = _rank(eidx, 512)
    counts = cnt[:, 0]
    padded = ((counts + tm - 1) // tm) * tm
    ends = jnp.cumsum(padded)
    starts = ends - padded
    e2 = eidx[:2]
    start_of = jnp.zeros_like(e2)
    for k in range(N_EXPERTS):
        start_of = jnp.where(e2 == k, starts[k], start_of)
    dest = start_of + rank[:2]
    t_pad = 2 * t + N_EXPERTS * tm
    tok = jnp.tile(jnp.arange(t, dtype=jnp.int32), 2)
    src_tok = (jnp.arange(t_pad, dtype=jnp.int32) % t).at[dest.reshape(2 * t)].set(
        tok, mode="promise_in_bounds", unique_indices=True)
    tile_start = jnp.arange(t_pad // tm, dtype=jnp.int32) * tm
    tile_expert = jnp.minimum(jnp.sum((ends[None, :] <= tile_start[:, None]).astype(jnp.int32), axis=1),
                              N_EXPERTS - 1)
    n_used = (ends[-1] // tm).astype(jnp.int32).reshape(1)
    xs = x.at[src_tok].get(mode="promise_in_bounds")
    ys = _moe(tile_expert, n_used, xs, w_gate, w_up, w_down, tm)
    y1 = ys.at[dest[0]].get(mode="promise_in_bounds")
    y2 = ys.at[dest[1]].get(mode="promise_in_bounds")
    return _moe_ln(x, y1, y2, ew.T, ln_g, ln_b, tm)


def _block_diag_tiles(m, rows_per_group, cols_per_group):
    gpt = S5_GROUPS // S5_KT
    m = m.reshape(S5_KT, gpt, rows_per_group, cols_per_group)
    eye = jnp.eye(gpt, dtype=m.dtype)
    bd = jnp.einsum("kgrc,gh->kgrhc", m, eye)
    return bd.reshape(S5_KT, gpt * rows_per_group, gpt * cols_per_group)


def kernel(x, router_w, router_b, l0_w_in, l0_diff_lam_q1, l0_diff_lam_k1, l0_diff_lam_q2, l0_diff_lam_k2, l0_diff_subln_g, l0_gla_w_gate, l0_gla_b_gate, l0_gla_norm_g, l0_w_out, l0_ln1_g, l0_ln1_b, l0_moe_w_gate, l0_moe_w_up, l0_moe_w_down, l0_ln2_g, l0_ln2_b, l1_w_in, l1_s5_a_re, l1_s5_a_im, l1_s5_log_step, l1_s5_b_re, l1_s5_b_im, l1_s5_c_re, l1_s5_c_im, l1_s5_d, l1_s5_w_glu, l1_s5_b_glu, l1_w_out, l1_ln1_g, l1_ln1_b, l1_moe_w_gate, l1_moe_w_up, l1_moe_w_down, l1_ln2_g, l1_ln2_b):
    bsz, t, d = x.shape
    assert bsz == 1 and d == D_MODEL
    x0 = x.reshape(t, d)
    vec = lambda p: p.reshape(1, -1).astype(F32)
    rwt = router_w.T.astype(F32)
    rb = router_b.reshape(N_EXPERTS, 1).astype(F32)

    lambda_init = 0.8 - 0.6 * math.exp(-0.3 * 0)
    c = [0, 1024, 2048, 3072, 3584, 4096, 5120, 5136, 6160]
    q_scale = math.log2(math.e) * DIFF_HEAD_DIM ** -0.5
    w0 = jnp.concatenate([l0_w_in[:, :c[1]] * q_scale, l0_w_in[:, c[1]:c[6]], l0_w_in[:, c[7]:c[8]],
                          l0_w_in[:, c[6]:c[7]], jnp.zeros((d, LANES - GLA_GATE_RANK), l0_w_in.dtype)],
                         axis=1).astype(BF16)
    h0 = _matmul(x0, w0, 1024, 896, BF16)
    lam_vec = jnp.stack([l0_diff_lam_q1, l0_diff_lam_k1, l0_diff_lam_q2, l0_diff_lam_k2]).astype(F32)
    o_a = _diff_attention(h0, lam_vec, l0_diff_subln_g.astype(F32), lambda_init, 512).T
    wg_pad = jnp.concatenate([l0_gla_w_gate, jnp.zeros((LANES - GLA_GATE_RANK, GLA_HEADS * GLA_DK), F32)],
                             axis=0).astype(BF16)
    o_b = _gla(h0, wg_pad, vec(l0_gla_b_gate), vec(l0_gla_norm_g), 256)
    x1, eidx, ew = _outproj_ln_router(o_a, o_b, l0_w_out.astype(BF16), x0, vec(l0_ln1_g), vec(l0_ln1_b),
                                      rwt, rb, 512, 256)
    x2 = _moe_block(x1, eidx, ew, l0_moe_w_gate, l0_moe_w_up, l0_moe_w_down, vec(l0_ln2_g), vec(l0_ln2_b), 256)

    h1 = _matmul(x2, l1_w_in.astype(BF16), 1024, 1024, BF16)
    o_c = _stick_breaking(h1, 256)
    lbr, lbi, bbr, bbi = _s5_prep(l1_s5_a_re, l1_s5_a_im, l1_s5_log_step, l1_s5_b_re, l1_s5_b_im)
    lam = jnp.stack([lbr[::S5_GROUP].reshape(-1), lbi[::S5_GROUP].reshape(-1)])
    bbr = bbr.reshape(S5_GROUPS, S5_GROUP, S5_STATE)
    bbi = bbi.reshape(S5_GROUPS, S5_GROUP, S5_STATE)
    wbu = jnp.concatenate([_block_diag_tiles(bbr, S5_GROUP, S5_STATE),
                           _block_diag_tiles(bbi, S5_GROUP, S5_STATE)], axis=2).astype(BF16)
    c_re_t = jnp.transpose(l1_s5_c_re, (0, 2, 1))
    c_im_t = jnp.transpose(l1_s5_c_im, (0, 2, 1))
    wc = jnp.concatenate([_block_diag_tiles(c_re_t, S5_STATE, S5_GROUP),
                          _block_diag_tiles(-c_im_t, S5_STATE, S5_GROUP)], axis=1).astype(BF16)
    o_d = _s5(h1, wbu, wc, lam, vec(l1_s5_d), l1_s5_w_glu.astype(BF16), vec(l1_s5_b_glu), 256)
    x3, eidx1, ew1 = _outproj_ln_router(o_c, o_d, l1_w_out.astype(BF16), x2, vec(l1_ln1_g), vec(l1_ln1_b),
                                        rwt, rb, 512, 256)
    x4 = _moe_block(x3, eidx1, ew1, l1_moe_w_gate, l1_moe_w_up, l1_moe_w_down, vec(l1_ln2_g), vec(l1_ln2_b), 256)
    return x4.reshape(bsz, t, d)
```

```python
import functools
import math

import jax
import jax.numpy as jnp
from jax import lax
from jax.experimental import pallas as pl
from jax.experimental.pallas import tpu as pltpu

F32 = jnp.float32
BF16 = jnp.bfloat16

D_MODEL = 2048
DEPTH = 2
DIFF_HEADS = 8
DIFF_HEAD_DIM = 64
DIFF_V_DIM = 128
GLA_HEADS = 4
GLA_DK = 128
GLA_DV = 256
GLA_GATE_RANK = 16
GLA_GATE_TEMP = 16.0
GLA_CHUNK = 64
SB_HEADS = 8
SB_HEAD_DIM = 128
S5_CHANNELS = 1024
S5_GROUP = 16
S5_GROUPS = 64
S5_STATE = 64
N_EXPERTS = 16
N_GROUPS = 4
EXPERTS_PER_GROUP = 4
D_EXPERT = 640
DEEPNORM_ALPHA = (2.0 * DEPTH) ** 0.25
LN_EPS = 1e-5

LANES = 128
SUBLANES = 8
VMEM_LIMIT = 56 * 1024 * 1024
SB_LOG_FLOOR = 120.0
SB_HEADS_PER_STEP = 2
MOE_SUB = 128

S5_KT = 4
S5_KT_CH = S5_CHANNELS // S5_KT
S5_KT_STATES = (S5_GROUPS // S5_KT) * S5_STATE
S5_SCAN_W = 512


def _params(sem, flags=None):
    return pltpu.CompilerParams(dimension_semantics=sem, vmem_limit_bytes=VMEM_LIMIT, flags=flags)


def _nt_dot(a, b):
    return lax.dot_general(a, b, (((1,), (1,)), ((), ())), preferred_element_type=F32)


def _dot(a, b):
    return jnp.dot(a, b, preferred_element_type=F32)


def _split_dot(x, m):
    hi = x.astype(BF16)
    lo = (x - hi.astype(F32)).astype(BF16)
    return _dot(hi, m) + _dot(lo, m)


def _split_dot_left(m, x):
    hi = x.astype(BF16)
    lo = (x - hi.astype(F32)).astype(BF16)
    return _dot(m, hi) + _dot(m, lo)


def _layer_norm(v, g, b):
    mu = jnp.mean(v, axis=-1, keepdims=True)
    d = v - mu
    var = jnp.mean(d * d, axis=-1, keepdims=True)
    return d * lax.rsqrt(var + LN_EPS) * g + b


def _mm_kernel(a_ref, b_ref, o_ref, a_bf):
    @pl.when(pl.program_id(1) == 0)
    def _():
        a_bf[...] = a_ref[...].astype(BF16)

    o_ref[...] = _dot(a_bf[...], b_ref[...]).astype(o_ref.dtype)


def _matmul(a, b, tm, tn, out_dtype):
    m, k = a.shape
    n = b.shape[1]
    tm = min(tm, m)
    return pl.pallas_call(
        _mm_kernel,
        grid=(m // tm, n // tn),
        in_specs=[pl.BlockSpec((tm, k), lambda i, j: (i, 0)),
                  pl.BlockSpec((k, tn), lambda i, j: (0, j))],
        out_specs=pl.BlockSpec((tm, tn), lambda i, j: (i, j)),
        out_shape=jax.ShapeDtypeStruct((m, n), out_dtype),
        scratch_shapes=[pltpu.VMEM((tm, k), BF16)],
        compiler_params=_params(("parallel", "arbitrary")),
        name="in_proj",
    )(a, b)


def _diff_attn_kernel(lam_ref, g_ref, q_ref, k_ref, v_ref, o_ref, vx_ref, m_ref, acc_ref, sa_ref, sb_ref, *, tq,
                      lambda_init):
    i = pl.program_id(1)
    dv = DIFF_V_DIM

    @pl.when(i == 0)
    def _():
        vx_ref[:, :dv] = v_ref[...]
        vx_ref[:, dv:] = jnp.ones((vx_ref.shape[0], dv), BF16)

    m_ref[...] = jnp.full(m_ref.shape, -jnp.inf, F32)
    acc_ref[...] = jnp.zeros(acc_ref.shape, F32)
    q = q_ref[...]
    lane = lax.broadcasted_iota(jnp.int32, q.shape, 1)
    zero = jnp.zeros_like(q)
    halves = (jnp.where(lane < DIFF_HEAD_DIM, q, zero), jnp.where(lane >= DIFF_HEAD_DIM, q, zero))

    def scores(j, mi):
        off = pl.multiple_of(j * tq, tq)
        return _nt_dot(halves[mi], k_ref[pl.ds(off, tq), :])

    def softmax_pv(j, get_s, masked):
        off = pl.multiple_of(j * tq, tq)
        vx = vx_ref[pl.ds(off, tq), :]
        if masked:
            row = lax.broadcasted_iota(jnp.int32, (tq, tq), 0)
            col = lax.broadcasted_iota(jnp.int32, (tq, tq), 1)
            causal = col <= row
        for mi in range(2):
            s = get_s(mi)
            if masked:
                s = jnp.where(causal, s, -jnp.inf)
            m_prev = m_ref[mi]
            m_new = jnp.maximum(m_prev, jnp.max(s, axis=1, keepdims=True))
            p = jnp.exp2(s - jnp.tile(m_new, (1, tq // LANES)))
            alpha = jnp.exp2(m_prev - m_new)
            acc_ref[mi] = jnp.tile(alpha, (1, 2)) * acc_ref[mi] + _dot(p.astype(BF16), vx)
            m_ref[mi] = m_new

    def scores_to(j, s_ref):
        for mi in range(2):
            s_ref[mi] = scores(j, mi)

    scores_to(0, sa_ref)

    def pair(t, carry):
        j = 2 * t
        scores_to(j + 1, sb_ref)
        softmax_pv(j, lambda mi: sa_ref[mi], False)
        scores_to(j + 2, sa_ref)
        softmax_pv(j + 1, lambda mi: sb_ref[mi], False)
        return carry

    lax.fori_loop(0, i // 2, pair, 0)

    @pl.when(i % 2 == 1)
    def _():
        softmax_pv(i - 1, lambda mi: sa_ref[mi], False)
        scores_to(i, sa_ref)

    softmax_pv(i, lambda mi: sa_ref[mi], True)

    lv = lam_ref[...]
    lam = (jnp.exp(jnp.sum(lv[0:1] * lv[1:2], axis=1, keepdims=True))
           - jnp.exp(jnp.sum(lv[2:3] * lv[3:4], axis=1, keepdims=True)) + lambda_init)
    a0 = acc_ref[0]
    a1 = acc_ref[1]
    o = a0[:, :dv] / a0[:, dv:] - lam * (a1[:, :dv] / a1[:, dv:])
    ms = jnp.mean(o * o, axis=-1, keepdims=True)
    o = o * lax.rsqrt(ms + LN_EPS) * g_ref[...] * (1.0 - lambda_init)
    o_ref[...] = o.astype(o_ref.dtype)


def _diff_attention(h0, lam_vec, subln_g, lambda_init, tq):
    t = h0.shape[0]
    tq = min(tq, t)
    nq = t // tq
    kern = functools.partial(_diff_attn_kernel, tq=tq, lambda_init=lambda_init)
    return pl.pallas_call(
        kern,
        grid=(DIFF_HEADS, nq),
        in_specs=[
            pl.BlockSpec((4, DIFF_HEAD_DIM), lambda h, i: (0, 0)),
            pl.BlockSpec((1, DIFF_V_DIM), lambda h, i: (0, 0)),
            pl.BlockSpec((tq, LANES), lambda h, i: (i, h)),
            pl.BlockSpec((t, LANES), lambda h, i: (0, DIFF_HEADS + h)),
            pl.BlockSpec((t, LANES), lambda h, i: (0, 2 * DIFF_HEADS + h)),
        ],
        out_specs=pl.BlockSpec((tq, LANES), lambda h, i: (i, h)),
        out_shape=jax.ShapeDtypeStruct((t, DIFF_HEADS * DIFF_V_DIM), BF16),
        scratch_shapes=[pltpu.VMEM((t, 2 * DIFF_V_DIM), BF16), pltpu.VMEM((2, tq, LANES), F32),
                        pltpu.VMEM((2, tq, 2 * DIFF_V_DIM), F32),
                        pltpu.VMEM((2, tq, tq), F32), pltpu.VMEM((2, tq, tq), F32)],
        compiler_params=_params(("arbitrary", "arbitrary")),
        name="diff_attn",
    )(lam_vec, subln_g, h0, h0, h0)


def _gla_kernel(q_ref, k_ref, v_ref, r_ref, gb_ref, wg_ref, bg_ref, ng_ref, o_ref, st_ref, *, tg):
    @pl.when(pl.program_id(0) == 0)
    def _():
        st_ref[...] = jnp.zeros(st_ref.shape, F32)

    c = GLA_CHUNK
    gate = _dot(gb_ref[...], wg_ref[...]) + bg_ref[...]
    log_a = -(jnp.maximum(-gate, 0.0) + jnp.log(1.0 + jnp.exp(-jnp.abs(gate)))) / GLA_GATE_TEMP
    row = lax.broadcasted_iota(jnp.int32, (c, c), 0)
    col = lax.broadcasted_iota(jnp.int32, (c, c), 1)
    tril = row >= col
    tri = jnp.where(tril, 1.0, 0.0).astype(BF16)
    for h in range(GLA_HEADS):
        ksl = slice(h * GLA_DK, (h + 1) * GLA_DK)
        vsl = slice(h * GLA_DV, (h + 1) * GLA_DV)
        for ci in range(tg // c):
            rs = slice(ci * c, (ci + 1) * c)
            la = log_a[rs, ksl]
            b = _split_dot_left(tri, la)
            b_last = b[c - 1:c, :]
            qf = q_ref[rs, ksl].astype(F32) * (GLA_DK ** -0.5)
            kf = k_ref[rs, ksl].astype(F32)
            v = v_ref[rs, vsl]
            q_dec = (qf * jnp.exp(b)).astype(BF16)
            k_dec = (kf * jnp.exp(-b)).astype(BF16)
            k_end = (kf * jnp.exp(b_last - b)).astype(BF16)
            scores = jnp.where(tril, _nt_dot(q_dec, k_dec), 0.0)
            st = st_ref[h]
            o = _dot(scores.astype(BF16), v) + _nt_dot(q_dec, st.astype(BF16))
            kv_t = lax.dot_general(v, k_end, (((0,), (0,)), ((), ())), preferred_element_type=F32)
            st_ref[h] = jnp.exp(b_last) * st + kv_t
            ms = jnp.mean(o * o, axis=-1, keepdims=True)
            o = o * lax.rsqrt(ms + LN_EPS) * ng_ref[...]
            r = r_ref[rs, vsl].astype(F32)
            o_ref[rs, vsl] = (o * (r / (1.0 + jnp.exp(-r)))).astype(o_ref.dtype)


def _gla(h0, wg_pad, b_gate, norm_g, tg):
    t = h0.shape[0]
    tg = min(tg, t)
    qk = GLA_HEADS * GLA_DK
    vw = GLA_HEADS * GLA_DV
    return pl.pallas_call(
        functools.partial(_gla_kernel, tg=tg),
        grid=(t // tg,),
        in_specs=[
            pl.BlockSpec((tg, qk), lambda i: (i, 3072 // qk)),
            pl.BlockSpec((tg, qk), lambda i: (i, 3584 // qk)),
            pl.BlockSpec((tg, vw), lambda i: (i, 4096 // vw)),
            pl.BlockSpec((tg, vw), lambda i: (i, 5120 // vw)),
            pl.BlockSpec((tg, LANES), lambda i: (i, 6144 // LANES)),
            pl.BlockSpec((LANES, qk), lambda i: (0, 0)),
            pl.BlockSpec((1, qk), lambda i: (0, 0)),
            pl.BlockSpec((1, GLA_DV), lambda i: (0, 0)),
        ],
        out_specs=pl.BlockSpec((tg, vw), lambda i: (i, 0)),
        out_shape=jax.ShapeDtypeStruct((t, vw), BF16),
        scratch_shapes=[pltpu.VMEM((GLA_HEADS, GLA_DV, GLA_DK), F32)],
        compiler_params=_params(("arbitrary",)),
        name="gla",
    )(h0, h0, h0, h0, h0, wg_pad, b_gate, norm_g)


def _sb_kernel(q_ref, k_ref, v_ref, o_ref, acc_ref, c_ref, *, tq):
    i = pl.program_id(1)
    hd = SB_HEAD_DIM
    acc_ref[...] = jnp.zeros(acc_ref.shape, F32)
    c_ref[...] = jnp.zeros(c_ref.shape, F32)
    row = lax.broadcasted_iota(jnp.int32, (tq, tq), 0)
    col = lax.broadcasted_iota(jnp.int32, (tq, tq), 1)
    strict = col < row
    tri = jnp.where(row > col, 1.0, 0.0).astype(BF16)

    def block(jb, masked):
        off = pl.multiple_of(jb * tq, tq)
        cmax = None
        for hh in range(SB_HEADS_PER_STEP):
            cols = slice(hh * hd, (hh + 1) * hd)
            k = k_ref[pl.ds(off, tq), cols]
            v = v_ref[pl.ds(off, tq), cols]
            z = _nt_dot(q_ref[:, cols], k) * (SB_HEAD_DIM ** -0.5)
            lsn = -(jnp.maximum(z, 0.0) + jnp.log(1.0 + jnp.exp(-jnp.abs(z))))
            lm = jnp.where(strict, lsn, 0.0) if masked else lsn
            excl = _split_dot(lm, tri) + c_ref[hh]
            w = jnp.exp(z + lsn + excl)
            if masked:
                w = jnp.where(strict, w, 0.0)
            acc_ref[:, cols] += _dot(w.astype(BF16), v)
            c_new = c_ref[hh] + jnp.sum(lm, axis=1, keepdims=True)
            c_ref[hh] = c_new
            hmax = jnp.max(c_new)
            cmax = hmax if cmax is None else jnp.maximum(cmax, hmax)
        return cmax

    cmax = block(i, True)

    def cond(carry):
        jb, cm = carry
        return jnp.logical_and(jb >= 0, cm > -SB_LOG_FLOOR)

    def body(carry):
        jb, _ = carry
        return jb - 1, block(jb, False)

    lax.while_loop(cond, body, (i - 1, cmax))
    o_ref[...] = acc_ref[...].astype(o_ref.dtype)


def _stick_breaking(h1, tq):
    t = h1.shape[0]
    tq = min(tq, t)
    ng = SB_HEADS // SB_HEADS_PER_STEP
    wd = SB_HEADS_PER_STEP * SB_HEAD_DIM
    return pl.pallas_call(
        functools.partial(_sb_kernel, tq=tq),
        grid=(ng, t // tq),
        in_specs=[
            pl.BlockSpec((tq, wd), lambda h, i: (i, h)),
            pl.BlockSpec((t, wd), lambda h, i: (0, ng + h)),
            pl.BlockSpec((t, wd), lambda h, i: (0, 2 * ng + h)),
        ],
        out_specs=pl.BlockSpec((tq, wd), lambda h, i: (i, h)),
        out_shape=jax.ShapeDtypeStruct((t, SB_HEADS * SB_HEAD_DIM), BF16),
        scratch_shapes=[pltpu.VMEM((tq, wd), F32), pltpu.VMEM((SB_HEADS_PER_STEP, tq, 1), F32)],
        compiler_params=_params(("parallel", "arbitrary")),
        name="stick_breaking",
    )(h1, h1, h1)


def _s5_prep_kernel(are_ref, aim_ref, ls_ref, bre_ref, bim_ref, lbr_ref, lbi_ref, bbr_ref, bbi_ref):
    step = jnp.exp(ls_ref[...])
    lam_re = are_ref[...]
    lam_im = aim_ref[...]
    mag = jnp.exp(step * lam_re)
    lb_re = mag * jnp.cos(step * lam_im)
    lb_im = mag * jnp.sin(step * lam_im)
    den = lam_re * lam_re + lam_im * lam_im
    n_re = lb_re - 1.0
    f_re = (n_re * lam_re + lb_im * lam_im) / den
    f_im = (lb_im * lam_re - n_re * lam_im) / den
    lbr_ref[...] = lb_re
    lbi_ref[...] = lb_im
    bbr_ref[...] = f_re * bre_ref[...] - f_im * bim_ref[...]
    bbi_ref[...] = f_re * bim_ref[...] + f_im * bre_ref[...]


def _s5_prep(a_re, a_im, log_step, b_re, b_im):
    rows = S5_GROUPS * S5_GROUP
    rep = lambda p: jnp.repeat(p, S5_GROUP, axis=0)
    to_rows = lambda b: jnp.transpose(b, (0, 2, 1)).reshape(rows, S5_STATE)
    spec = pl.BlockSpec((rows, S5_STATE), lambda: (0, 0))
    shp = jax.ShapeDtypeStruct((rows, S5_STATE), F32)
    return pl.pallas_call(
        _s5_prep_kernel,
        in_specs=[spec, spec, pl.BlockSpec((rows, 1), lambda: (0, 0)), spec, spec],
        out_specs=[spec] * 4,
        out_shape=[shp] * 4,
        name="s5_prep",
    )(rep(a_re), rep(a_im), rep(log_step[:, None]), to_rows(b_re), to_rows(b_im))


def _cmul(ar, ai, br, bi):
    return ar * br - ai * bi, ar * bi + ai * br


def _s5_kernel(u_ref, wbu_ref, wc_ref, lam_ref, d_ref, wglu_ref, bglu_ref, o_ref,
               hs_ref, hp_ref, pw_ref, st_ref, *, tc):
    ns = S5_KT_STATES

    @pl.when(pl.program_id(0) == 0)
    def _():
        hp_ref[...] = jnp.zeros(hp_ref.shape, F32)
        l1r = jnp.broadcast_to(lam_ref[0:1, :], (SUBLANES, lam_ref.shape[1]))
        l1i = jnp.broadcast_to(lam_ref[1:2, :], (SUBLANES, lam_ref.shape[1]))
        l2r, l2i = _cmul(l1r, l1i, l1r, l1i)
        l4r, l4i = _cmul(l2r, l2i, l2r, l2i)
        l8r, l8i = _cmul(l4r, l4i, l4r, l4i)
        ridx = lax.broadcasted_iota(jnp.int32, l1r.shape, 0)
        n = ridx + 1
        pr = jnp.ones_like(l1r)
        pi = jnp.zeros_like(l1r)
        for bit, (lr, li) in ((1, (l1r, l1i)), (2, (l2r, l2i)), (4, (l4r, l4i)), (8, (l8r, l8i))):
            nr, ni = _cmul(pr, pi, lr, li)
            take = (n & bit) != 0
            pr = jnp.where(take, nr, pr)
            pi = jnp.where(take, ni, pi)
        pw_ref[0] = pr
        pw_ref[1] = pi
        for s, (lr, li) in enumerate(((l1r, l1i), (l2r, l2i), (l4r, l4i))):
            keep = ridx >= (1 << s)
            st_ref[2 * s] = jnp.where(keep, lr, 0.0)
            st_ref[2 * s + 1] = jnp.where(keep, li, 0.0)

    u = u_ref[...]
    for kt in range(S5_KT):
        hs_ref[:, kt * 2 * ns:(kt + 1) * 2 * ns] = _dot(u[:, kt * S5_KT_CH:(kt + 1) * S5_KT_CH], wbu_ref[kt])

    w = S5_SCAN_W
    for kt in range(S5_KT):
        for cc in range(ns // w):
            re0 = kt * 2 * ns + cc * w
            im0 = re0 + ns
            l0 = kt * ns + cc * w

            def body(r, carry, re0=re0, im0=im0, l0=l0):
                hpr, hpi = carry
                rows = pl.ds(pl.multiple_of(r * SUBLANES, SUBLANES), SUBLANES)
                xr = hs_ref[rows, re0:re0 + w]
                xi = hs_ref[rows, im0:im0 + w]
                for s in range(3):
                    sr = pltpu.roll(xr, 1 << s, 0)
                    si = pltpu.roll(xi, 1 << s, 0)
                    ar = st_ref[2 * s, :, l0:l0 + w]
                    ai = st_ref[2 * s + 1, :, l0:l0 + w]
                    xr, xi = xr + ar * sr - ai * si, xi + ar * si + ai * sr
                pr = pw_ref[0, :, l0:l0 + w]
                pi = pw_ref[1, :, l0:l0 + w]
                xr, xi = xr + pr * hpr - pi * hpi, xi + pr * hpi + pi * hpr
                hs_ref[rows, re0:re0 + w] = xr
                hs_ref[rows, im0:im0 + w] = xi
                return (jnp.broadcast_to(xr[SUBLANES - 1:SUBLANES, :], (SUBLANES, w)),
                        jnp.broadcast_to(xi[SUBLANES - 1:SUBLANES, :], (SUBLANES, w)))

            hpr, hpi = lax.fori_loop(0, tc // SUBLANES, body,
                                     (hp_ref[:, re0:re0 + w], hp_ref[:, im0:im0 + w]))
            hp_ref[:, re0:re0 + w] = hpr
            hp_ref[:, im0:im0 + w] = hpi

    ys = [_dot(hs_ref[:, kt * 2 * ns:(kt + 1) * 2 * ns].astype(BF16), wc_ref[kt]) for kt in range(S5_KT)]
    y = jnp.concatenate(ys, axis=1) + d_ref[...] * u.astype(F32)
    g = 0.5 * y * (1.0 + jnp.tanh(math.sqrt(2.0 / math.pi) * (y + 0.044715 * (y * y * y))))
    gl = _dot(g.astype(BF16), wglu_ref[...]) + bglu_ref[...]
    o_ref[...] = (g / (1.0 + jnp.exp(-gl))).astype(o_ref.dtype)


def _s5(h1, wbu, wc, lam, d_skip, w_glu, b_glu, tc):
    t = h1.shape[0]
    tc = min(tc, t)
    ns = S5_KT_STATES
    nstate = S5_GROUPS * S5_STATE
    full = lambda shape: pl.BlockSpec(shape, lambda i: (0,) * len(shape))
    return pl.pallas_call(
        functools.partial(_s5_kernel, tc=tc),
        grid=(t // tc,),
        in_specs=[
            pl.BlockSpec((tc, S5_CHANNELS), lambda i: (i, 3)),
            full((S5_KT, S5_KT_CH, 2 * ns)),
            full((S5_KT, 2 * ns, S5_KT_CH)),
            full((2, nstate)),
            full((1, S5_CHANNELS)),
            full((S5_CHANNELS, S5_CHANNELS)),
            full((1, S5_CHANNELS)),
        ],
        out_specs=pl.BlockSpec((tc, S5_CHANNELS), lambda i: (i, 0)),
        out_shape=jax.ShapeDtypeStruct((t, S5_CHANNELS), BF16),
        scratch_shapes=[pltpu.VMEM((tc, 2 * nstate), F32), pltpu.VMEM((SUBLANES, 2 * nstate), F32),
                        pltpu.VMEM((2, SUBLANES, nstate), F32), pltpu.VMEM((6, SUBLANES, nstate), F32)],
        compiler_params=_params(("arbitrary",)),
        name="s5",
    )(h1, wbu, wc, lam, d_skip, w_glu, b_glu)


def _route(probs, sel):
    rows = [sel[e:e + 1, :] for e in range(N_EXPERTS)]
    prow = [probs[e:e + 1, :] for e in range(N_EXPERTS)]
    best_score = None
    for g in range(N_GROUPS):
        a = rows[g * EXPERTS_PER_GROUP:(g + 1) * EXPERTS_PER_GROUP]
        score = None
        for x in range(EXPERTS_PER_GROUP):
            for y in range(x + 1, EXPERTS_PER_GROUP):
                pair = a[x] + a[y]
                score = pair if score is None else jnp.maximum(score, pair)
        if best_score is None:
            best_score, gbest = score, jnp.zeros(score.shape, jnp.int32)
            cs = list(a)
            cp = prow[0:EXPERTS_PER_GROUP]
        else:
            better = score > best_score
            best_score = jnp.where(better, score, best_score)
            gbest = jnp.where(better, g, gbest)
            cs = [jnp.where(better, a[x], cs[x]) for x in range(EXPERTS_PER_GROUP)]
            cp = [jnp.where(better, prow[g * EXPERTS_PER_GROUP + x], cp[x]) for x in range(EXPERTS_PER_GROUP)]

    def first_argmax(vals):
        bv, bi = vals[0], jnp.zeros(vals[0].shape, jnp.int32)
        for x in range(1, len(vals)):
            better = vals[x] > bv
            bv = jnp.where(better, vals[x], bv)
            bi = jnp.where(better, x, bi)
        return bi

    i1 = first_argmax(cs)
    i2 = first_argmax([jnp.where(i1 == x, -jnp.inf, cs[x]) for x in range(EXPERTS_PER_GROUP)])

    def pick(idx):
        out = cp[0]
        for x in range(1, EXPERTS_PER_GROUP):
            out = jnp.where(idx == x, cp[x], out)
        return out

    p1, p2 = pick(i1), pick(i2)
    tot = p1 + p2
    return gbest * EXPERTS_PER_GROUP + i1, gbest * EXPERTS_PER_GROUP + i2, p1 / tot, p2 / tot


def _ln_router_tail(v, g_ref, b_ref, rwt_ref, rb_ref):
    xn = _layer_norm(v, g_ref[...], b_ref[...])
    logits = lax.dot_general(rwt_ref[...], xn, (((1,), (1,)), ((), ())), preferred_element_type=F32,
                             precision=lax.Precision.HIGHEST)
    mx = jnp.max(logits, axis=0, keepdims=True)
    ex = jnp.exp(logits - mx)
    probs = ex / jnp.sum(ex, axis=0, keepdims=True)
    e1, e2, w1, w2 = _route(probs, probs + rb_ref[...])
    n = e1.shape[1]
    ei = jnp.concatenate([e1, e2, jnp.zeros((SUBLANES - 2, n), jnp.int32)], axis=0)
    ew = jnp.concatenate([w1, w2, jnp.zeros((SUBLANES - 2, n), F32)], axis=0)
    return xn, ei, ew


def _outproj_kernel(ma_ref, mb_ref, wa_ref, wb_ref, xr_ref, g_ref, b_ref, rwt_ref, rb_ref,
                    x_ref, ei_ref, ew_ref, *, sub):
    for s in range(x_ref.shape[0] // sub):
        rows = slice(s * sub, (s + 1) * sub)
        v = (DEEPNORM_ALPHA * xr_ref[rows, :] + _dot(ma_ref[rows, :], wa_ref[...])
             + _dot(mb_ref[rows, :], wb_ref[...]))
        xn, ei, ew = _ln_router_tail(v, g_ref, b_ref, rwt_ref, rb_ref)
        x_ref[rows, :] = xn
        ei_ref[:, rows] = ei
        ew_ref[:, rows] = ew


def _outproj_ln_router(mix_a, mix_b, w_out, x_res, ln_g, ln_b, rwt, rb, tm, sub):
    t = x_res.shape[0]
    tm = min(tm, t)
    sub = min(sub, tm)
    half = w_out.shape[0] // 2
    row = lambda w: pl.BlockSpec((tm, w), lambda i: (i, 0))
    full = lambda shape: pl.BlockSpec(shape, lambda i: (0,) * len(shape))
    return pl.pallas_call(
        functools.partial(_outproj_kernel, sub=sub),
        grid=(t // tm,),
        in_specs=[row(half), row(half),
                  pl.BlockSpec((half, D_MODEL), lambda i: (0, 0)), pl.BlockSpec((half, D_MODEL), lambda i: (1, 0)),
                  row(D_MODEL), full((1, D_MODEL)), full((1, D_MODEL)),
                  full((N_EXPERTS, D_MODEL)), full((N_EXPERTS, 1))],
        out_specs=[row(D_MODEL),
                   pl.BlockSpec((SUBLANES, tm), lambda i: (0, i)), pl.BlockSpec((SUBLANES, tm), lambda i: (0, i))],
        out_shape=[jax.ShapeDtypeStruct((t, D_MODEL), F32),
                   jax.ShapeDtypeStruct((SUBLANES, t), jnp.int32), jax.ShapeDtypeStruct((SUBLANES, t), F32)],
        compiler_params=_params(("parallel",)),
        name="outproj_ln_router",
    )(mix_a, mix_b, w_out, w_out, x_res, ln_g, ln_b, rwt, rb)


def _rank_kernel(e_ref, rank_ref, cnt_ref, base_ref, *, tn):
    @pl.when(pl.program_id(0) == 0)
    def _():
        base_ref[...] = jnp.zeros(base_ref.shape, F32)

    e = e_ref[...]
    rows = lax.broadcasted_iota(jnp.int32, (N_EXPERTS, tn), 0)
    oh1 = rows == e[0:1, :]
    oh2 = rows == e[1:2, :]
    oh = jnp.where(jnp.logical_or(oh1, oh2), 1.0, 0.0)
    r = lax.broadcasted_iota(jnp.int32, (tn, tn), 0)
    c = lax.broadcasted_iota(jnp.int32, (tn, tn), 1)
    earlier = jnp.where(r < c, 1.0, 0.0).astype(BF16)
    base = base_ref[...]
    before = _dot(oh.astype(BF16), earlier) + jnp.tile(base, (1, tn // LANES))
    r1 = jnp.sum(jnp.where(oh1, before, 0.0), axis=0, keepdims=True)
    r2 = jnp.sum(jnp.where(oh2, before, 0.0), axis=0, keepdims=True)
    rank_ref[...] = jnp.concatenate([r1, r2, jnp.zeros((SUBLANES - 2, tn), F32)], axis=0).astype(jnp.int32)
    base = base + jnp.sum(oh, axis=1, keepdims=True)
    base_ref[...] = base
    cnt_ref[...] = base.astype(jnp.int32)


def _rank(eidx, tn):
    t = eidx.shape[1]
    tn = min(tn, t)
    return pl.pallas_call(
        functools.partial(_rank_kernel, tn=tn),
        grid=(t // tn,),
        in_specs=[pl.BlockSpec((SUBLANES, tn), lambda i: (0, i))],
        out_specs=[pl.BlockSpec((SUBLANES, tn), lambda i: (0, i)),
                   pl.BlockSpec((N_EXPERTS, LANES), lambda i: (0, 0))],
        out_shape=[jax.ShapeDtypeStruct((SUBLANES, t), jnp.int32),
                   jax.ShapeDtypeStruct((N_EXPERTS, LANES), jnp.int32)],
        scratch_shapes=[pltpu.VMEM((N_EXPERTS, LANES), F32)],
        compiler_params=_params(("arbitrary",)),
        name="moe_rank",
    )(eidx)


def _moe_kernel(te_ref, nu_ref, xs_ref, wg_ref, wu_ref, wd_ref, y_ref, wgu_s, wd_s):
    tile = pl.program_id(0)
    used = tile < nu_ref[0]
    fresh = jnp.logical_or(tile == 0, te_ref[tile] != te_ref[jnp.maximum(tile - 1, 0)])

    @pl.when(jnp.logical_and(used, fresh))
    def _():
        wgu_s[:, :D_EXPERT] = wg_ref[...].astype(BF16)
        wgu_s[:, D_EXPERT:] = wu_ref[...].astype(BF16)
        wd_s[...] = wd_ref[...].astype(BF16)

    @pl.when(used)
    def _():
        for s in range(xs_ref.shape[0] // MOE_SUB):
            rows = slice(s * MOE_SUB, (s + 1) * MOE_SUB)
            hgu = _dot(xs_ref[rows, :].astype(BF16), wgu_s[...])
            hg = hgu[:, :D_EXPERT]
            hu = hgu[:, D_EXPERT:]
            h = (hg / (1.0 + jnp.exp(-hg))) * hu
            y_ref[rows, :] = _dot(h.astype(BF16), wd_s[...])

    @pl.when(jnp.logical_not(used))
    def _():
        y_ref[...] = jnp.zeros(y_ref.shape, F32)


def _moe(tile_expert, n_used, xs, w_gate, w_up, w_down, tm):
    t_pad = xs.shape[0]
    grid_spec = pltpu.PrefetchScalarGridSpec(
        num_scalar_prefetch=2,
        grid=(t_pad // tm,),
        in_specs=[
            pl.BlockSpec((tm, D_MODEL), lambda i, te, nu: (i, 0)),
            pl.BlockSpec((None, D_MODEL, D_EXPERT), lambda i, te, nu: (te[i], 0, 0)),
            pl.BlockSpec((None, D_MODEL, D_EXPERT), lambda i, te, nu: (te[i], 0, 0)),
            pl.BlockSpec((None, D_EXPERT, D_MODEL), lambda i, te, nu: (te[i], 0, 0)),
        ],
        out_specs=pl.BlockSpec((tm, D_MODEL), lambda i, te, nu: (i, 0)),
        scratch_shapes=[pltpu.VMEM((D_MODEL, 2 * D_EXPERT), BF16), pltpu.VMEM((D_EXPERT, D_MODEL), BF16)],
    )
    return pl.pallas_call(
        _moe_kernel,
        grid_spec=grid_spec,
        out_shape=jax.ShapeDtypeStruct((t_pad, D_MODEL), F32),
        compiler_params=_params(("arbitrary",)),
        name="moe",
    )(tile_expert, n_used, xs, w_gate, w_up, w_down)


def _moe_ln_kernel(x_ref, y1_ref, y2_ref, w_ref, g_ref, b_ref, o_ref):
    w = w_ref[...]
    v = DEEPNORM_ALPHA * x_ref[...] + (w[:, 0:1] * y1_ref[...] + w[:, 1:2] * y2_ref[...])
    o_ref[...] = _layer_norm(v, g_ref[...], b_ref[...])


def _moe_ln(x, y1, y2, w_cols, ln_g, ln_b, tm):
    t = x.shape[0]
    tm = min(tm, t)
    row = pl.BlockSpec((tm, D_MODEL), lambda i: (i, 0))
    vec = pl.BlockSpec((1, D_MODEL), lambda i: (0, 0))
    return pl.pallas_call(
        _moe_ln_kernel,
        grid=(t // tm,),
        in_specs=[row, row, row, pl.BlockSpec((tm, SUBLANES), lambda i: (i, 0)), vec, vec],
        out_specs=row,
        out_shape=jax.ShapeDtypeStruct((t, D_MODEL), F32),
        compiler_params=_params(("parallel",)),
        name="moe_ln",
    )(x, y1, y2, w_cols, ln_g, ln_b)


def _moe_block(x, eidx, ew, w_gate, w_up, w_down, ln_g, ln_b, tm):
    t = x.shape[0]
    tm = min(tm, t)
    rank, cnt = _rank(eidx, 512)
    counts = cnt[:, 0]
    padded = ((counts + tm - 1) // tm) * tm
    ends = jnp.cumsum(padded)
    starts = ends - padded
    e2 = eidx[:2]
    start_of = jnp.zeros_like(e2)
    for k in range(N_EXPERTS):
        start_of = jnp.where(e2 == k, starts[k], start_of)
    dest = start_of + rank[:2]
    t_pad = 2 * t + N_EXPERTS * tm
    tok = jnp.tile(jnp.arange(t, dtype=jnp.int32), 2)
    src_tok = (jnp.arange(t_pad, dtype=jnp.int32) % t).at[dest.reshape(2 * t)].set(
        tok, mode="promise_in_bounds", unique_indices=True)
    tile_start = jnp.arange(t_pad // tm, dtype=jnp.int32) * tm
    tile_expert = jnp.minimum(jnp.sum((ends[None, :] <= tile_start[:, None]).astype(jnp.int32), axis=1),
                              N_EXPERTS - 1)
    n_used = (ends[-1] // tm).astype(jnp.int32).reshape(1)
    xs = x.at[src_tok].get(mode="promise_in_bounds")
    ys = _moe(tile_expert, n_used, xs, w_gate, w_up, w_down, tm)
    y1 = ys.at[dest[0]].get(mode="promise_in_bounds")
    y2 = ys.at[dest[1]].get(mode="promise_in_bounds")
    return _moe_ln(x, y1, y2, ew.T, ln_g, ln_b, tm)


def _block_diag_tiles(m, rows_per_group, cols_per_group):
    gpt = S5_GROUPS // S5_KT
    m = m.reshape(S5_KT, gpt, rows_per_group, cols_per_group)
    eye = jnp.eye(gpt, dtype=m.dtype)
    bd = jnp.einsum("kgrc,gh->kgrhc", m, eye)
    return bd.reshape(S5_KT, gpt * rows_per_group, gpt * cols_per_group)


def kernel(x, router_w, router_b, l0_w_in, l0_diff_lam_q1, l0_diff_lam_k1, l0_diff_lam_q2, l0_diff_lam_k2, l0_diff_subln_g, l0_gla_w_gate, l0_gla_b_gate, l0_gla_norm_g, l0_w_out, l0_ln1_g, l0_ln1_b, l0_moe_w_gate, l0_moe_w_up, l0_moe_w_down, l0_ln2_g, l0_ln2_b, l1_w_in, l1_s5_a_re, l1_s5_a_im, l1_s5_log_step, l1_s5_b_re, l1_s5_b_im, l1_s5_c_re, l1_s5_c_im, l1_s5_d, l1_s5_w_glu, l1_s5_b_glu, l1_w_out, l1_ln1_g, l1_ln1_b, l1_moe_w_gate, l1_moe_w_up, l1_moe_w_down, l1_ln2_g, l1_ln2_b):
    bsz, t, d = x.shape
    assert bsz == 1 and d == D_MODEL
    x0 = x.reshape(t, d)
    vec = lambda p: p.reshape(1, -1).astype(F32)
    rwt = router_w.T.astype(F32)
    rb = router_b.reshape(N_EXPERTS, 1).astype(F32)

    lambda_init = 0.8 - 0.6 * math.exp(-0.3 * 0)
    c = [0, 1024, 2048, 3072, 3584, 4096, 5120, 5136, 6160]
    q_scale = math.log2(math.e) * DIFF_HEAD_DIM ** -0.5
    w0 = jnp.concatenate([l0_w_in[:, :c[1]] * q_scale, l0_w_in[:, c[1]:c[6]], l0_w_in[:, c[7]:c[8]],
                          l0_w_in[:, c[6]:c[7]], jnp.zeros((d, LANES - GLA_GATE_RANK), l0_w_in.dtype)],
                         axis=1).astype(BF16)
    h0 = _matmul(x0, w0, 1024, 896, BF16)
    lam_vec = jnp.stack([l0_diff_lam_q1, l0_diff_lam_k1, l0_diff_lam_q2, l0_diff_lam_k2]).astype(F32)
    o_a = _diff_attention(h0, lam_vec, vec(l0_diff_subln_g), lambda_init, 1024)
    wg_pad = jnp.concatenate([l0_gla_w_gate, jnp.zeros((LANES - GLA_GATE_RANK, GLA_HEADS * GLA_DK), F32)],
                             axis=0).astype(BF16)
    o_b = _gla(h0, wg_pad, vec(l0_gla_b_gate), vec(l0_gla_norm_g), 256)
    x1, eidx, ew = _outproj_ln_router(o_a, o_b, l0_w_out.astype(BF16), x0, vec(l0_ln1_g), vec(l0_ln1_b),
                                      rwt, rb, 512, 256)
    x2 = _moe_block(x1, eidx, ew, l0_moe_w_gate, l0_moe_w_up, l0_moe_w_down, vec(l0_ln2_g), vec(l0_ln2_b), 256)

    h1 = _matmul(x2, l1_w_in.astype(BF16), 1024, 1024, BF16)
    o_c = _stick_breaking(h1, 256)
    lbr, lbi, bbr, bbi = _s5_prep(l1_s5_a_re, l1_s5_a_im, l1_s5_log_step, l1_s5_b_re, l1_s5_b_im)
    lam = jnp.stack([lbr[::S5_GROUP].reshape(-1), lbi[::S5_GROUP].reshape(-1)])
    bbr = bbr.reshape(S5_GROUPS, S5_GROUP, S5_STATE)
    bbi = bbi.reshape(S5_GROUPS, S5_GROUP, S5_STATE)
    wbu = jnp.concatenate([_block_diag_tiles(bbr, S5_GROUP, S5_STATE),
                           _block_diag_tiles(bbi, S5_GROUP, S5_STATE)], axis=2).astype(BF16)
    c_re_t = jnp.transpose(l1_s5_c_re, (0, 2, 1))
    c_im_t = jnp.transpose(l1_s5_c_im, (0, 2, 1))
    wc = jnp.concatenate([_block_diag_tiles(c_re_t, S5_STATE, S5_GROUP),
                          _block_diag_tiles(-c_im_t, S5_STATE, S5_GROUP)], axis=1).astype(BF16)
    o_d = _s5(h1, wbu, wc, lam, vec(l1_s5_d), l1_s5_w_glu.astype(BF16), vec(l1_s5_b_glu), 256)
    x3, eidx1, ew1 = _outproj_ln_router(o_c, o_d, l1_w_out.astype(BF16), x2, vec(l1_ln1_g), vec(l1_ln1_b),
                                        rwt, rb, 512, 256)
    x4 = _moe_block(x3, eidx1, ew1, l1_moe_w_gate, l1_moe_w_up, l1_moe_w_down, vec(l1_ln2_g), vec(l1_ln2_b), 256)
    return x4.reshape(bsz, t, d)
```

```python
import functools
import math

import jax
import jax.numpy as jnp
from jax import lax
from jax.experimental import pallas as pl
from jax.experimental.pallas import tpu as pltpu

F32 = jnp.float32
BF16 = jnp.bfloat16

D_MODEL = 2048
DEPTH = 2
DIFF_HEADS = 8
DIFF_HEAD_DIM = 64
DIFF_V_DIM = 128
GLA_HEADS = 4
GLA_DK = 128
GLA_DV = 256
GLA_GATE_RANK = 16
GLA_GATE_TEMP = 16.0
GLA_CHUNK = 64
SB_HEADS = 8
SB_HEAD_DIM = 128
S5_CHANNELS = 1024
S5_GROUP = 16
S5_GROUPS = 64
S5_STATE = 64
N_EXPERTS = 16
N_GROUPS = 4
EXPERTS_PER_GROUP = 4
D_EXPERT = 640
DEEPNORM_ALPHA = (2.0 * DEPTH) ** 0.25
LN_EPS = 1e-5

LANES = 128
SUBLANES = 8
VMEM_LIMIT = 56 * 1024 * 1024
SB_LOG_FLOOR = 120.0
SB_HEADS_PER_STEP = 4
MOE_SUB = 256

S5_KT = 4
S5_KT_CH = S5_CHANNELS // S5_KT
S5_KT_STATES = (S5_GROUPS // S5_KT) * S5_STATE
S5_SCAN_W = 512


def _params(sem, flags=None):
    return pltpu.CompilerParams(dimension_semantics=sem, vmem_limit_bytes=VMEM_LIMIT, flags=flags)


def _nt_dot(a, b):
    return lax.dot_general(a, b, (((1,), (1,)), ((), ())), preferred_element_type=F32)


def _dot(a, b):
    return jnp.dot(a, b, preferred_element_type=F32)


def _split_dot(x, m):
    hi = x.astype(BF16)
    lo = (x - hi.astype(F32)).astype(BF16)
    return _dot(hi, m) + _dot(lo, m)


def _split_dot_left(m, x):
    hi = x.astype(BF16)
    lo = (x - hi.astype(F32)).astype(BF16)
    return _dot(m, hi) + _dot(m, lo)


def _layer_norm(v, g, b):
    mu = jnp.mean(v, axis=-1, keepdims=True)
    d = v - mu
    var = jnp.mean(d * d, axis=-1, keepdims=True)
    return d * lax.rsqrt(var + LN_EPS) * g + b


def _mm_kernel(a_ref, b_ref, o_ref, a_bf):
    @pl.when(pl.program_id(1) == 0)
    def _():
        a_bf[...] = a_ref[...].astype(BF16)

    o_ref[...] = _dot(a_bf[...], b_ref[...]).astype(o_ref.dtype)


def _matmul(a, b, tm, tn, out_dtype):
    m, k = a.shape
    n = b.shape[1]
    tm = min(tm, m)
    return pl.pallas_call(
        _mm_kernel,
        grid=(m // tm, n // tn),
        in_specs=[pl.BlockSpec((tm, k), lambda i, j: (i, 0)),
                  pl.BlockSpec((k, tn), lambda i, j: (0, j))],
        out_specs=pl.BlockSpec((tm, tn), lambda i, j: (i, j)),
        out_shape=jax.ShapeDtypeStruct((m, n), out_dtype),
        scratch_shapes=[pltpu.VMEM((tm, k), BF16)],
        compiler_params=_params(("parallel", "arbitrary")),
        name="in_proj",
    )(a, b)


def _diff_attn_kernel(lam_ref, g_ref, q_ref, k_ref, v_ref, o_ref, vx_ref, m_ref, acc_ref, sa_ref, sb_ref, *, tq,
                      lambda_init):
    i = pl.program_id(1)
    dv = DIFF_V_DIM

    @pl.when(i == 0)
    def _():
        vx_ref[:, :dv] = v_ref[...]
        vx_ref[:, dv:] = jnp.ones((vx_ref.shape[0], dv), BF16)

    m_ref[...] = jnp.full(m_ref.shape, -jnp.inf, F32)
    acc_ref[...] = jnp.zeros(acc_ref.shape, F32)
    q = q_ref[...]
    lane = lax.broadcasted_iota(jnp.int32, q.shape, 1)
    zero = jnp.zeros_like(q)
    halves = (jnp.where(lane < DIFF_HEAD_DIM, q, zero), jnp.where(lane >= DIFF_HEAD_DIM, q, zero))

    def scores(j, mi):
        off = pl.multiple_of(j * tq, tq)
        return _nt_dot(halves[mi], k_ref[pl.ds(off, tq), :])

    def softmax_pv(j, get_s, masked):
        off = pl.multiple_of(j * tq, tq)
        vx = vx_ref[pl.ds(off, tq), :]
        if masked:
            row = lax.broadcasted_iota(jnp.int32, (tq, tq), 0)
            col = lax.broadcasted_iota(jnp.int32, (tq, tq), 1)
            causal = col <= row
        for mi in range(2):
            s = get_s(mi)
            if masked:
                s = jnp.where(causal, s, -jnp.inf)
            m_prev = m_ref[mi]
            m_new = jnp.maximum(m_prev, jnp.max(s, axis=1, keepdims=True))
            p = jnp.exp2(s - jnp.tile(m_new, (1, tq // LANES)))
            alpha = jnp.exp2(m_prev - m_new)
            acc_ref[mi] = jnp.tile(alpha, (1, 2)) * acc_ref[mi] + _dot(p.astype(BF16), vx)
            m_ref[mi] = m_new

    def scores_to(j, s_ref):
        for mi in range(2):
            s_ref[mi] = scores(j, mi)

    scores_to(0, sa_ref)

    def pair(t, carry):
        j = 2 * t
        scores_to(j + 1, sb_ref)
        softmax_pv(j, lambda mi: sa_ref[mi], False)
        scores_to(j + 2, sa_ref)
        softmax_pv(j + 1, lambda mi: sb_ref[mi], False)
        return carry

    lax.fori_loop(0, i // 2, pair, 0)

    @pl.when(i % 2 == 1)
    def _():
        scores_to(i, sb_ref)
        softmax_pv(i - 1, lambda mi: sa_ref[mi], False)
        softmax_pv(i, lambda mi: sb_ref[mi], True)

    @pl.when(i % 2 == 0)
    def _():
        softmax_pv(i, lambda mi: sa_ref[mi], True)

    lv = lam_ref[...]
    lam = (jnp.exp(jnp.sum(lv[0:1] * lv[1:2], axis=1, keepdims=True))
           - jnp.exp(jnp.sum(lv[2:3] * lv[3:4], axis=1, keepdims=True)) + lambda_init)
    a0 = acc_ref[0]
    a1 = acc_ref[1]
    o = a0[:, :dv] / a0[:, dv:] - lam * (a1[:, :dv] / a1[:, dv:])
    ms = jnp.mean(o * o, axis=-1, keepdims=True)
    o = o * lax.rsqrt(ms + LN_EPS) * g_ref[...] * (1.0 - lambda_init)
    o_ref[...] = o.astype(o_ref.dtype)


def _diff_attention(h0, lam_vec, subln_g, lambda_init, tq):
    t = h0.shape[0]
    tq = min(tq, t)
    nq = t // tq
    kern = functools.partial(_diff_attn_kernel, tq=tq, lambda_init=lambda_init)
    return pl.pallas_call(
        kern,
        grid=(DIFF_HEADS, nq),
        in_specs=[
            pl.BlockSpec((4, DIFF_HEAD_DIM), lambda h, i: (0, 0)),
            pl.BlockSpec((1, DIFF_V_DIM), lambda h, i: (0, 0)),
            pl.BlockSpec((tq, LANES), lambda h, i: (i, h)),
            pl.BlockSpec((t, LANES), lambda h, i: (0, DIFF_HEADS + h)),
            pl.BlockSpec((t, LANES), lambda h, i: (0, 2 * DIFF_HEADS + h)),
        ],
        out_specs=pl.BlockSpec((tq, LANES), lambda h, i: (i, h)),
        out_shape=jax.ShapeDtypeStruct((t, DIFF_HEADS * DIFF_V_DIM), BF16),
        scratch_shapes=[pltpu.VMEM((t, 2 * DIFF_V_DIM), BF16), pltpu.VMEM((2, tq, LANES), F32),
                        pltpu.VMEM((2, tq, 2 * DIFF_V_DIM), F32),
                        pltpu.VMEM((2, tq, tq), F32), pltpu.VMEM((2, tq, tq), F32)],
        compiler_params=_params(("arbitrary", "arbitrary")),
        name="diff_attn",
    )(lam_vec, subln_g, h0, h0, h0)


def _gla_kernel(q_ref, k_ref, v_ref, r_ref, gb_ref, wg_ref, bg_ref, ng_ref, o_ref, st_ref, *, tg):
    @pl.when(pl.program_id(0) == 0)
    def _():
        st_ref[...] = jnp.zeros(st_ref.shape, F32)

    c = GLA_CHUNK
    gate = _dot(gb_ref[...], wg_ref[...]) + bg_ref[...]
    log_a = -(jnp.maximum(-gate, 0.0) + jnp.log(1.0 + jnp.exp(-jnp.abs(gate)))) / GLA_GATE_TEMP
    row = lax.broadcasted_iota(jnp.int32, (c, c), 0)
    col = lax.broadcasted_iota(jnp.int32, (c, c), 1)
    tril = row >= col
    tri = jnp.where(tril, 1.0, 0.0).astype(BF16)
    for h in range(GLA_HEADS):
        ksl = slice(h * GLA_DK, (h + 1) * GLA_DK)
        vsl = slice(h * GLA_DV, (h + 1) * GLA_DV)
        for ci in range(tg // c):
            rs = slice(ci * c, (ci + 1) * c)
            la = log_a[rs, ksl]
            b = _split_dot_left(tri, la)
            b_last = b[c - 1:c, :]
            qf = q_ref[rs, ksl].astype(F32) * (GLA_DK ** -0.5)
            kf = k_ref[rs, ksl].astype(F32)
            v = v_ref[rs, vsl]
            q_dec = (qf * jnp.exp(b)).astype(BF16)
            k_dec = (kf * jnp.exp(-b)).astype(BF16)
            k_end = (kf * jnp.exp(b_last - b)).astype(BF16)
            scores = jnp.where(tril, _nt_dot(q_dec, k_dec), 0.0)
            st = st_ref[h]
            o = _dot(scores.astype(BF16), v) + _nt_dot(q_dec, st.astype(BF16))
            kv_t = lax.dot_general(v, k_end, (((0,), (0,)), ((), ())), preferred_element_type=F32)
            st_ref[h] = jnp.exp(b_last) * st + kv_t
            ms = jnp.mean(o * o, axis=-1, keepdims=True)
            o = o * lax.rsqrt(ms + LN_EPS) * ng_ref[...]
            r = r_ref[rs, vsl].astype(F32)
            o_ref[rs, vsl] = (o * (r / (1.0 + jnp.exp(-r)))).astype(o_ref.dtype)


def _gla(h0, wg_pad, b_gate, norm_g, tg):
    t = h0.shape[0]
    tg = min(tg, t)
    qk = GLA_HEADS * GLA_DK
    vw = GLA_HEADS * GLA_DV
    return pl.pallas_call(
        functools.partial(_gla_kernel, tg=tg),
        grid=(t // tg,),
        in_specs=[
            pl.BlockSpec((tg, qk), lambda i: (i, 3072 // qk)),
            pl.BlockSpec((tg, qk), lambda i: (i, 3584 // qk)),
            pl.BlockSpec((tg, vw), lambda i: (i, 4096 // vw)),
            pl.BlockSpec((tg, vw), lambda i: (i, 5120 // vw)),
            pl.BlockSpec((tg, LANES), lambda i: (i, 6144 // LANES)),
            pl.BlockSpec((LANES, qk), lambda i: (0, 0)),
            pl.BlockSpec((1, qk), lambda i: (0, 0)),
            pl.BlockSpec((1, GLA_DV), lambda i: (0, 0)),
        ],
        out_specs=pl.BlockSpec((tg, vw), lambda i: (i, 0)),
        out_shape=jax.ShapeDtypeStruct((t, vw), BF16),
        scratch_shapes=[pltpu.VMEM((GLA_HEADS, GLA_DV, GLA_DK), F32)],
        compiler_params=_params(("arbitrary",)),
        name="gla",
    )(h0, h0, h0, h0, h0, wg_pad, b_gate, norm_g)


def _sb_kernel(q_ref, k_ref, v_ref, o_ref, acc_ref, c_ref, *, tq):
    i = pl.program_id(1)
    hd = SB_HEAD_DIM
    acc_ref[...] = jnp.zeros(acc_ref.shape, F32)
    c_ref[...] = jnp.zeros(c_ref.shape, F32)
    row = lax.broadcasted_iota(jnp.int32, (tq, tq), 0)
    col = lax.broadcasted_iota(jnp.int32, (tq, tq), 1)
    strict = col < row
    tri = jnp.where(row > col, 1.0, 0.0).astype(BF16)

    def block(jb, masked):
        off = pl.multiple_of(jb * tq, tq)
        cmax = None
        for hh in range(SB_HEADS_PER_STEP):
            cols = slice(hh * hd, (hh + 1) * hd)
            k = k_ref[pl.ds(off, tq), cols]
            v = v_ref[pl.ds(off, tq), cols]
            z = _nt_dot(q_ref[:, cols], k) * (SB_HEAD_DIM ** -0.5)
            lsn = -(jnp.maximum(z, 0.0) + jnp.log(1.0 + jnp.exp(-jnp.abs(z))))
            lm = jnp.where(strict, lsn, 0.0) if masked else lsn
            excl = _split_dot(lm, tri) + c_ref[hh]
            w = jnp.exp(z + lsn + excl)
            if masked:
                w = jnp.where(strict, w, 0.0)
            acc_ref[:, cols] += _dot(w.astype(BF16), v)
            c_new = c_ref[hh] + jnp.sum(lm, axis=1, keepdims=True)
            c_ref[hh] = c_new
            hmax = jnp.max(c_new)
            cmax = hmax if cmax is None else jnp.maximum(cmax, hmax)
        return cmax

    cmax = block(i, True)

    def cond(carry):
        jb, cm = carry
        return jnp.logical_and(jb >= 0, cm > -SB_LOG_FLOOR)

    def body(carry):
        jb, _ = carry
        return jb - 1, block(jb, False)

    lax.while_loop(cond, body, (i - 1, cmax))
    o_ref[...] = acc_ref[...].astype(o_ref.dtype)


def _stick_breaking(h1, tq):
    t = h1.shape[0]
    tq = min(tq, t)
    ng = SB_HEADS // SB_HEADS_PER_STEP
    wd = SB_HEADS_PER_STEP * SB_HEAD_DIM
    return pl.pallas_call(
        functools.partial(_sb_kernel, tq=tq),
        grid=(ng, t // tq),
        in_specs=[
            pl.BlockSpec((tq, wd), lambda h, i: (i, h)),
            pl.BlockSpec((t, wd), lambda h, i: (0, ng + h), pipeline_mode=pl.Buffered(1)),
            pl.BlockSpec((t, wd), lambda h, i: (0, 2 * ng + h), pipeline_mode=pl.Buffered(1)),
        ],
        out_specs=pl.BlockSpec((tq, wd), lambda h, i: (i, h)),
        out_shape=jax.ShapeDtypeStruct((t, SB_HEADS * SB_HEAD_DIM), BF16),
        scratch_shapes=[pltpu.VMEM((tq, wd), F32), pltpu.VMEM((SB_HEADS_PER_STEP, tq, 1), F32)],
        compiler_params=_params(("parallel", "arbitrary")),
        name="stick_breaking",
    )(h1, h1, h1)


def _s5_prep_kernel(are_ref, aim_ref, ls_ref, bre_ref, bim_ref, lbr_ref, lbi_ref, bbr_ref, bbi_ref):
    step = jnp.exp(ls_ref[...])
    lam_re = are_ref[...]
    lam_im = aim_ref[...]
    mag = jnp.exp(step * lam_re)
    lb_re = mag * jnp.cos(step * lam_im)
    lb_im = mag * jnp.sin(step * lam_im)
    den = lam_re * lam_re + lam_im * lam_im
    n_re = lb_re - 1.0
    f_re = (n_re * lam_re + lb_im * lam_im) / den
    f_im = (lb_im * lam_re - n_re * lam_im) / den
    lbr_ref[...] = lb_re
    lbi_ref[...] = lb_im
    bbr_ref[...] = f_re * bre_ref[...] - f_im * bim_ref[...]
    bbi_ref[...] = f_re * bim_ref[...] + f_im * bre_ref[...]


def _s5_prep(a_re, a_im, log_step, b_re, b_im):
    rows = S5_GROUPS * S5_GROUP
    rep = lambda p: jnp.repeat(p, S5_GROUP, axis=0)
    to_rows = lambda b: jnp.transpose(b, (0, 2, 1)).reshape(rows, S5_STATE)
    spec = pl.BlockSpec((rows, S5_STATE), lambda: (0, 0))
    shp = jax.ShapeDtypeStruct((rows, S5_STATE), F32)
    return pl.pallas_call(
        _s5_prep_kernel,
        in_specs=[spec, spec, pl.BlockSpec((rows, 1), lambda: (0, 0)), spec, spec],
        out_specs=[spec] * 4,
        out_shape=[shp] * 4,
        name="s5_prep",
    )(rep(a_re), rep(a_im), rep(log_step[:, None]), to_rows(b_re), to_rows(b_im))


def _cmul(ar, ai, br, bi):
    return ar * br - ai * bi, ar * bi + ai * br


def _s5_kernel(u_ref, wbu_ref, wc_ref, lam_ref, d_ref, wglu_ref, bglu_ref, o_ref,
               hs_ref, hp_ref, pw_ref, st_ref, *, tc):
    ns = S5_KT_STATES

    @pl.when(pl.program_id(0) == 0)
    def _():
        hp_ref[...] = jnp.zeros(hp_ref.shape, F32)
        l1r = jnp.broadcast_to(lam_ref[0:1, :], (SUBLANES, lam_ref.shape[1]))
        l1i = jnp.broadcast_to(lam_ref[1:2, :], (SUBLANES, lam_ref.shape[1]))
        l2r, l2i = _cmul(l1r, l1i, l1r, l1i)
        l4r, l4i = _cmul(l2r, l2i, l2r, l2i)
        l8r, l8i = _cmul(l4r, l4i, l4r, l4i)
        ridx = lax.broadcasted_iota(jnp.int32, l1r.shape, 0)
        n = ridx + 1
        pr = jnp.ones_like(l1r)
        pi = jnp.zeros_like(l1r)
        for bit, (lr, li) in ((1, (l1r, l1i)), (2, (l2r, l2i)), (4, (l4r, l4i)), (8, (l8r, l8i))):
            nr, ni = _cmul(pr, pi, lr, li)
            take = (n & bit) != 0
            pr = jnp.where(take, nr, pr)
            pi = jnp.where(take, ni, pi)
        pw_ref[0] = pr
        pw_ref[1] = pi
        for s, (lr, li) in enumerate(((l1r, l1i), (l2r, l2i), (l4r, l4i))):
            keep = ridx >= (1 << s)
            st_ref[2 * s] = jnp.where(keep, lr, 0.0)
            st_ref[2 * s + 1] = jnp.where(keep, li, 0.0)

    u = u_ref[...]
    for kt in range(S5_KT):
        hs_ref[:, kt * 2 * ns:(kt + 1) * 2 * ns] = _dot(u[:, kt * S5_KT_CH:(kt + 1) * S5_KT_CH], wbu_ref[kt])

    w = S5_SCAN_W
    for kt in range(S5_KT):
        for cc in range(ns // w):
            re0 = kt * 2 * ns + cc * w
            im0 = re0 + ns
            l0 = kt * ns + cc * w

            def body(r, carry, re0=re0, im0=im0, l0=l0):
                hpr, hpi = carry
                rows = pl.ds(pl.multiple_of(r * SUBLANES, SUBLANES), SUBLANES)
                xr = hs_ref[rows, re0:re0 + w]
                xi = hs_ref[rows, im0:im0 + w]
                for s in range(3):
                    sr = pltpu.roll(xr, 1 << s, 0)
                    si = pltpu.roll(xi, 1 << s, 0)
                    ar = st_ref[2 * s, :, l0:l0 + w]
                    ai = st_ref[2 * s + 1, :, l0:l0 + w]
                    xr, xi = xr + ar * sr - ai * si, xi + ar * si + ai * sr
                pr = pw_ref[0, :, l0:l0 + w]
                pi = pw_ref[1, :, l0:l0 + w]
                xr, xi = xr + pr * hpr - pi * hpi, xi + pr * hpi + pi * hpr
                hs_ref[rows, re0:re0 + w] = xr
                hs_ref[rows, im0:im0 + w] = xi
                return (jnp.broadcast_to(xr[SUBLANES - 1:SUBLANES, :], (SUBLANES, w)),
                        jnp.broadcast_to(xi[SUBLANES - 1:SUBLANES, :], (SUBLANES, w)))

            hpr, hpi = lax.fori_loop(0, tc // SUBLANES, body,
                                     (hp_ref[:, re0:re0 + w], hp_ref[:, im0:im0 + w]))
            hp_ref[:, re0:re0 + w] = hpr
            hp_ref[:, im0:im0 + w] = hpi

    ys = [_dot(hs_ref[:, kt * 2 * ns:(kt + 1) * 2 * ns].astype(BF16), wc_ref[kt]) for kt in range(S5_KT)]
    y = jnp.concatenate(ys, axis=1) + d_ref[...] * u.astype(F32)
    g = 0.5 * y * (1.0 + jnp.tanh(math.sqrt(2.0 / math.pi) * (y + 0.044715 * (y * y * y))))
    gl = _dot(g.astype(BF16), wglu_ref[...]) + bglu_ref[...]
    o_ref[...] = (g / (1.0 + jnp.exp(-gl))).astype(o_ref.dtype)


def _s5(h1, wbu, wc, lam, d_skip, w_glu, b_glu, tc):
    t = h1.shape[0]
    tc = min(tc, t)
    ns = S5_KT_STATES
    nstate = S5_GROUPS * S5_STATE
    full = lambda shape: pl.BlockSpec(shape, lambda i: (0,) * len(shape))
    return pl.pallas_call(
        functools.partial(_s5_kernel, tc=tc),
        grid=(t // tc,),
        in_specs=[
            pl.BlockSpec((tc, S5_CHANNELS), lambda i: (i, 3)),
            full((S5_KT, S5_KT_CH, 2 * ns)),
            full((S5_KT, 2 * ns, S5_KT_CH)),
            full((2, nstate)),
            full((1, S5_CHANNELS)),
            full((S5_CHANNELS, S5_CHANNELS)),
            full((1, S5_CHANNELS)),
        ],
        out_specs=pl.BlockSpec((tc, S5_CHANNELS), lambda i: (i, 0)),
        out_shape=jax.ShapeDtypeStruct((t, S5_CHANNELS), BF16),
        scratch_shapes=[pltpu.VMEM((tc, 2 * nstate), F32), pltpu.VMEM((SUBLANES, 2 * nstate), F32),
                        pltpu.VMEM((2, SUBLANES, nstate), F32), pltpu.VMEM((6, SUBLANES, nstate), F32)],
        compiler_params=_params(("arbitrary",)),
        name="s5",
    )(h1, wbu, wc, lam, d_skip, w_glu, b_glu)


def _route(probs, sel):
    rows = [sel[e:e + 1, :] for e in range(N_EXPERTS)]
    prow = [probs[e:e + 1, :] for e in range(N_EXPERTS)]
    best_score = None
    for g in range(N_GROUPS):
        a = rows[g * EXPERTS_PER_GROUP:(g + 1) * EXPERTS_PER_GROUP]
        score = None
        for x in range(EXPERTS_PER_GROUP):
            for y in range(x + 1, EXPERTS_PER_GROUP):
                pair = a[x] + a[y]
                score = pair if score is None else jnp.maximum(score, pair)
        if best_score is None:
            best_score, gbest = score, jnp.zeros(score.shape, jnp.int32)
            cs = list(a)
            cp = prow[0:EXPERTS_PER_GROUP]
        else:
            better = score > best_score
            best_score = jnp.where(better, score, best_score)
            gbest = jnp.where(better, g, gbest)
            cs = [jnp.where(better, a[x], cs[x]) for x in range(EXPERTS_PER_GROUP)]
            cp = [jnp.where(better, prow[g * EXPERTS_PER_GROUP + x], cp[x]) for x in range(EXPERTS_PER_GROUP)]

    def first_argmax(vals):
        bv, bi = vals[0], jnp.zeros(vals[0].shape, jnp.int32)
        for x in range(1, len(vals)):
            better = vals[x] > bv
            bv = jnp.where(better, vals[x], bv)
            bi = jnp.where(better, x, bi)
        return bi

    i1 = first_argmax(cs)
    i2 = first_argmax([jnp.where(i1 == x, -jnp.inf, cs[x]) for x in range(EXPERTS_PER_GROUP)])

    def pick(idx):
        out = cp[0]
        for x in range(1, EXPERTS_PER_GROUP):
            out = jnp.where(idx == x, cp[x], out)
        return out

    p1, p2 = pick(i1), pick(i2)
    tot = p1 + p2
    return gbest * EXPERTS_PER_GROUP + i1, gbest * EXPERTS_PER_GROUP + i2, p1 / tot, p2 / tot


def _ln_router_tail(v, g_ref, b_ref, rwt_ref, rb_ref):
    xn = _layer_norm(v, g_ref[...], b_ref[...])
    logits = lax.dot_general(rwt_ref[...], xn, (((1,), (1,)), ((), ())), preferred_element_type=F32,
                             precision=lax.Precision.HIGHEST)
    mx = jnp.max(logits, axis=0, keepdims=True)
    ex = jnp.exp(logits - mx)
    probs = ex / jnp.sum(ex, axis=0, keepdims=True)
    e1, e2, w1, w2 = _route(probs, probs + rb_ref[...])
    n = e1.shape[1]
    ei = jnp.concatenate([e1, e2, jnp.zeros((SUBLANES - 2, n), jnp.int32)], axis=0)
    ew = jnp.concatenate([w1, w2, jnp.zeros((SUBLANES - 2, n), F32)], axis=0)
    return xn, ei, ew


def _outproj_kernel(ma_ref, mb_ref, wa_ref, wb_ref, xr_ref, g_ref, b_ref, rwt_ref, rb_ref,
                    x_ref, ei_ref, ew_ref, *, sub):
    for s in range(x_ref.shape[0] // sub):
        rows = slice(s * sub, (s + 1) * sub)
        v = (DEEPNORM_ALPHA * xr_ref[rows, :] + _dot(ma_ref[rows, :], wa_ref[...])
             + _dot(mb_ref[rows, :], wb_ref[...]))
        xn, ei, ew = _ln_router_tail(v, g_ref, b_ref, rwt_ref, rb_ref)
        x_ref[rows, :] = xn
        ei_ref[:, rows] = ei
        ew_ref[:, rows] = ew


def _outproj_ln_router(mix_a, mix_b, w_out, x_res, ln_g, ln_b, rwt, rb, tm, sub):
    t = x_res.shape[0]
    tm = min(tm, t)
    sub = min(sub, tm)
    half = w_out.shape[0] // 2
    row = lambda w: pl.BlockSpec((tm, w), lambda i: (i, 0))
    full = lambda shape: pl.BlockSpec(shape, lambda i: (0,) * len(shape))
    return pl.pallas_call(
        functools.partial(_outproj_kernel, sub=sub),
        grid=(t // tm,),
        in_specs=[row(half), row(half),
                  pl.BlockSpec((half, D_MODEL), lambda i: (0, 0)), pl.BlockSpec((half, D_MODEL), lambda i: (1, 0)),
                  row(D_MODEL), full((1, D_MODEL)), full((1, D_MODEL)),
                  full((N_EXPERTS, D_MODEL)), full((N_EXPERTS, 1))],
        out_specs=[row(D_MODEL),
                   pl.BlockSpec((SUBLANES, tm), lambda i: (0, i)), pl.BlockSpec((SUBLANES, tm), lambda i: (0, i))],
        out_shape=[jax.ShapeDtypeStruct((t, D_MODEL), F32),
                   jax.ShapeDtypeStruct((SUBLANES, t), jnp.int32), jax.ShapeDtypeStruct((SUBLANES, t), F32)],
        compiler_params=_params(("parallel",)),
        name="outproj_ln_router",
    )(mix_a, mix_b, w_out, w_out, x_res, ln_g, ln_b, rwt, rb)


def _rank_kernel(e_ref, rank_ref, cnt_ref, base_ref, *, tn):
    @pl.when(pl.program_id(0) == 0)
    def _():
        base_ref[...] = jnp.zeros(base_ref.shape, F32)

    e = e_ref[...]
    rows = lax.broadcasted_iota(jnp.int32, (N_EXPERTS, tn), 0)
    oh1 = rows == e[0:1, :]
    oh2 = rows == e[1:2, :]
    oh = jnp.where(jnp.logical_or(oh1, oh2), 1.0, 0.0)
    r = lax.broadcasted_iota(jnp.int32, (tn, tn), 0)
    c = lax.broadcasted_iota(jnp.int32, (tn, tn), 1)
    earlier = jnp.where(r < c, 1.0, 0.0).astype(BF16)
    base = base_ref[...]
    before = _dot(oh.astype(BF16), earlier) + jnp.tile(base, (1, tn // LANES))
    r1 = jnp.sum(jnp.where(oh1, before, 0.0), axis=0, keepdims=True)
    r2 = jnp.sum(jnp.where(oh2, before, 0.0), axis=0, keepdims=True)
    rank_ref[...] = jnp.concatenate([r1, r2, jnp.zeros((SUBLANES - 2, tn), F32)], axis=0).astype(jnp.int32)
    base = base + jnp.sum(oh, axis=1, keepdims=True)
    base_ref[...] = base
    cnt_ref[...] = base.astype(jnp.int32)


def _rank(eidx, tn):
    t = eidx.shape[1]
    tn = min(tn, t)
    return pl.pallas_call(
        functools.partial(_rank_kernel, tn=tn),
        grid=(t // tn,),
        in_specs=[pl.BlockSpec((SUBLANES, tn), lambda i: (0, i))],
        out_specs=[pl.BlockSpec((SUBLANES, tn), lambda i: (0, i)),
                   pl.BlockSpec((N_EXPERTS, LANES), lambda i: (0, 0))],
        out_shape=[jax.ShapeDtypeStruct((SUBLANES, t), jnp.int32),
                   jax.ShapeDtypeStruct((N_EXPERTS, LANES), jnp.int32)],
        scratch_shapes=[pltpu.VMEM((N_EXPERTS, LANES), F32)],
        compiler_params=_params(("arbitrary",)),
        name="moe_rank",
    )(eidx)


def _moe_kernel(te_ref, nu_ref, xs_ref, wg_ref, wu_ref, wd_ref, y_ref, wgu_s, wd_s):
    tile = pl.program_id(0)
    used = tile < nu_ref[0]
    fresh = jnp.logical_or(tile == 0, te_ref[tile] != te_ref[jnp.maximum(tile - 1, 0)])

    @pl.when(jnp.logical_and(used, fresh))
    def _():
        wgu_s[:, :D_EXPERT] = wg_ref[...].astype(BF16)
        wgu_s[:, D_EXPERT:] = wu_ref[...].astype(BF16)
        wd_s[...] = wd_ref[...].astype(BF16)

    @pl.when(used)
    def _():
        for s in range(xs_ref.shape[0] // MOE_SUB):
            rows = slice(s * MOE_SUB, (s + 1) * MOE_SUB)
            hgu = _dot(xs_ref[rows, :].astype(BF16), wgu_s[...])
            hg = hgu[:, :D_EXPERT]
            hu = hgu[:, D_EXPERT:]
            h = (hg / (1.0 + jnp.exp(-hg))) * hu
            y_ref[rows, :] = _dot(h.astype(BF16), wd_s[...])

    @pl.when(jnp.logical_not(used))
    def _():
        y_ref[...] = jnp.zeros(y_ref.shape, F32)


def _moe(tile_expert, n_used, xs, w_gate, w_up, w_down, tm):
    t_pad = xs.shape[0]
    grid_spec = pltpu.PrefetchScalarGridSpec(
        num_scalar_prefetch=2,
        grid=(t_pad // tm,),
        in_specs=[
            pl.BlockSpec((tm, D_MODEL), lambda i, te, nu: (i, 0)),
            pl.BlockSpec((None, D_MODEL, D_EXPERT), lambda i, te, nu: (te[i], 0, 0)),
            pl.BlockSpec((None, D_MODEL, D_EXPERT), lambda i, te, nu: (te[i], 0, 0)),
            pl.BlockSpec((None, D_EXPERT, D_MODEL), lambda i, te, nu: (te[i], 0, 0)),
        ],
        out_specs=pl.BlockSpec((tm, D_MODEL), lambda i, te, nu: (i, 0)),
        scratch_shapes=[pltpu.VMEM((D_MODEL, 2 * D_EXPERT), BF16), pltpu.VMEM((D_EXPERT, D_MODEL), BF16)],
    )
    return pl.pallas_call(
        _moe_kernel,
        grid_spec=grid_spec,
        out_shape=jax.ShapeDtypeStruct((t_pad, D_MODEL), F32),
        compiler_params=_params(("arbitrary",)),
        name="moe",
    )(tile_expert, n_used, xs, w_gate, w_up, w_down)


def _moe_ln_kernel(x_ref, y1_ref, y2_ref, w_ref, g_ref, b_ref, o_ref):
    w = w_ref[...]
    v = DEEPNORM_ALPHA * x_ref[...] + (w[:, 0:1] * y1_ref[...] + w[:, 1:2] * y2_ref[...])
    o_ref[...] = _layer_norm(v, g_ref[...], b_ref[...])


def _moe_ln(x, y1, y2, w_cols, ln_g, ln_b, tm):
    t = x.shape[0]
    tm = min(tm, t)
    row = pl.BlockSpec((tm, D_MODEL), lambda i: (i, 0))
    vec = pl.BlockSpec((1, D_MODEL), lambda i: (0, 0))
    return pl.pallas_call(
        _moe_ln_kernel,
        grid=(t // tm,),
        in_specs=[row, row, row, pl.BlockSpec((tm, SUBLANES), lambda i: (i, 0)), vec, vec],
        out_specs=row,
        out_shape=jax.ShapeDtypeStruct((t, D_MODEL), F32),
        compiler_params=_params(("parallel",)),
        name="moe_ln",
    )(x, y1, y2, w_cols, ln_g, ln_b)


def _moe_block(x, eidx, ew, w_gate, w_up, w_down, ln_g, ln_b, tm):
    t = x.shape[0]
    tm = min(tm, t)
    rank, cnt = _rank(eidx, 512)
    counts = cnt[:, 0]
    padded = ((counts + tm - 1) // tm) * tm
    ends = jnp.cumsum(padded)
    starts = ends - padded
    e2 = eidx[:2]
    start_of = jnp.zeros_like(e2)
    for k in range(N_EXPERTS):
        start_of = jnp.where(e2 == k, starts[k], start_of)
    dest = start_of + rank[:2]
    t_pad = 2 * t + N_EXPERTS * tm
    tok = jnp.tile(jnp.arange(t, dtype=jnp.int32), 2)
    src_tok = (jnp.arange(t_pad, dtype=jnp.int32) % t).at[dest.reshape(2 * t)].set(
        tok, mode="promise_in_bounds", unique_indices=True)
    tile_start = jnp.arange(t_pad // tm, dtype=jnp.int32) * tm
    tile_expert = jnp.minimum(jnp.sum((ends[None, :] <= tile_start[:, None]).astype(jnp.int32), axis=1),
                              N_EXPERTS - 1)
    n_used = (ends[-1] // tm).astype(jnp.int32).reshape(1)
    xs = x.at[src_tok].get(mode="promise_in_bounds")
    ys = _moe(tile_expert, n_used, xs, w_gate, w_up, w_down, tm)
    y1 = ys.at[dest[0]].get(mode="promise_in_bounds")
    y2 = ys.at[dest[1]].get(mode="promise_in_bounds")
    return _moe_ln(x, y1, y2, ew.T, ln_g, ln_b, tm)


def _block_diag_tiles(m, rows_per_group, cols_per_group):
    gpt = S5_GROUPS // S5_KT
    m = m.reshape(S5_KT, gpt, rows_per_group, cols_per_group)
    eye = jnp.eye(gpt, dtype=m.dtype)
    bd = jnp.einsum("kgrc,gh->kgrhc", m, eye)
    return bd.reshape(S5_KT, gpt * rows_per_group, gpt * cols_per_group)


def kernel(x, router_w, router_b, l0_w_in, l0_diff_lam_q1, l0_diff_lam_k1, l0_diff_lam_q2, l0_diff_lam_k2, l0_diff_subln_g, l0_gla_w_gate, l0_gla_b_gate, l0_gla_norm_g, l0_w_out, l0_ln1_g, l0_ln1_b, l0_moe_w_gate, l0_moe_w_up, l0_moe_w_down, l0_ln2_g, l0_ln2_b, l1_w_in, l1_s5_a_re, l1_s5_a_im, l1_s5_log_step, l1_s5_b_re, l1_s5_b_im, l1_s5_c_re, l1_s5_c_im, l1_s5_d, l1_s5_w_glu, l1_s5_b_glu, l1_w_out, l1_ln1_g, l1_ln1_b, l1_moe_w_gate, l1_moe_w_up, l1_moe_w_down, l1_ln2_g, l1_ln2_b):
    bsz, t, d = x.shape
    assert bsz == 1 and d == D_MODEL
    x0 = x.reshape(t, d)
    vec = lambda p: p.reshape(1, -1).astype(F32)
    rwt = router_w.T.astype(F32)
    rb = router_b.reshape(N_EXPERTS, 1).astype(F32)

    lambda_init = 0.8 - 0.6 * math.exp(-0.3 * 0)
    c = [0, 1024, 2048, 3072, 3584, 4096, 5120, 5136, 6160]
    q_scale = math.log2(math.e) * DIFF_HEAD_DIM ** -0.5
    w0 = jnp.concatenate([l0_w_in[:, :c[1]] * q_scale, l0_w_in[:, c[1]:c[6]], l0_w_in[:, c[7]:c[8]],
                          l0_w_in[:, c[6]:c[7]], jnp.zeros((d, LANES - GLA_GATE_RANK), l0_w_in.dtype)],
                         axis=1).astype(BF16)
    h0 = _matmul(x0, w0, 1024, 896, BF16)
    lam_vec = jnp.stack([l0_diff_lam_q1, l0_diff_lam_k1, l0_diff_lam_q2, l0_diff_lam_k2]).astype(F32)
    o_a = _diff_attention(h0, lam_vec, vec(l0_diff_subln_g), lambda_init, 1024)
    wg_pad = jnp.concatenate([l0_gla_w_gate, jnp.zeros((LANES - GLA_GATE_RANK, GLA_HEADS * GLA_DK), F32)],
                             axis=0).astype(BF16)
    o_b = _gla(h0, wg_pad, vec(l0_gla_b_gate), vec(l0_gla_norm_g), 256)
    x1, eidx, ew = _outproj_ln_router(o_a, o_b, l0_w_out.astype(BF16), x0, vec(l0_ln1_g), vec(l0_ln1_b),
                                      rwt, rb, 512, 256)
    x2 = _moe_block(x1, eidx, ew, l0_moe_w_gate, l0_moe_w_up, l0_moe_w_down, vec(l0_ln2_g), vec(l0_ln2_b), 256)

    h1 = _matmul(x2, l1_w_in.astype(BF16), 1024, 1024, BF16)
    o_c = _stick_breaking(h1, 256)
    lbr, lbi, bbr, bbi = _s5_prep(l1_s5_a_re, l1_s5_a_im, l1_s5_log_step, l1_s5_b_re, l1_s5_b_im)
    lam = jnp.stack([lbr[::S5_GROUP].reshape(-1), lbi[::S5_GROUP].reshape(-1)])
    bbr = bbr.reshape(S5_GROUPS, S5_GROUP, S5_STATE)
    bbi = bbi.reshape(S5_GROUPS, S5_GROUP, S5_STATE)
    wbu = jnp.concatenate([_block_diag_tiles(bbr, S5_GROUP, S5_STATE),
                           _block_diag_tiles(bbi, S5_GROUP, S5_STATE)], axis=2).astype(BF16)
    c_re_t = jnp.transpose(l1_s5_c_re, (0, 2, 1))
    c_im_t = jnp.transpose(l1_s5_c_im, (0, 2, 1))
    wc = jnp.concatenate([_block_diag_tiles(c_re_t, S5_STATE, S5_GROUP),
                          _block_diag_tiles(-c_im_t, S5_STATE, S5_GROUP)], axis=1).astype(BF16)
    o_d = _s5(h1, wbu, wc, lam, vec(l1_s5_d), l1_s5_w_glu.astype(BF16), vec(l1_s5_b_glu), 256)
    x3, eidx1, ew1 = _outproj_ln_router(o_c, o_d, l1_w_out.astype(BF16), x2, vec(l1_ln1_g), vec(l1_ln1_b),
                                        rwt, rb, 512, 256)
    x4 = _moe_block(x3, eidx1, ew1, l1_moe_w_gate, l1_moe_w_up, l1_moe_w_down, vec(l1_ln2_g), vec(l1_ln2_b), 256)
    return x4.reshape(bsz, t, d)
```

```python
import functools
import math

import jax
import jax.numpy as jnp
from jax import lax
from jax.experimental import pallas as pl
from jax.experimental.pallas import tpu as pltpu

F32 = jnp.float32
BF16 = jnp.bfloat16

D_MODEL = 2048
DEPTH = 2
DIFF_HEADS = 8
DIFF_HEAD_DIM = 64
DIFF_V_DIM = 128
GLA_HEADS = 4
GLA_DK = 128
GLA_DV = 256
GLA_GATE_RANK = 16
GLA_GATE_TEMP = 16.0
GLA_CHUNK = 64
SB_HEADS = 8
SB_HEAD_DIM = 128
S5_CHANNELS = 1024
S5_GROUP = 16
S5_GROUPS = 64
S5_STATE = 64
N_EXPERTS = 16
N_GROUPS = 4
EXPERTS_PER_GROUP = 4
D_EXPERT = 640
DEEPNORM_ALPHA = (2.0 * DEPTH) ** 0.25
LN_EPS = 1e-5

LANES = 128
SUBLANES = 8
VMEM_LIMIT = 56 * 1024 * 1024
SB_LOG_FLOOR = 120.0
SB_HEADS_PER_STEP = 4
MOE_SUB = 256

S5_KT = 4
S5_KT_CH = S5_CHANNELS // S5_KT
S5_KT_STATES = (S5_GROUPS // S5_KT) * S5_STATE
S5_SCAN_W = 512


def _params(sem, flags=None):
    return pltpu.CompilerParams(dimension_semantics=sem, vmem_limit_bytes=VMEM_LIMIT, flags=flags)


def _nt_dot(a, b):
    return lax.dot_general(a, b, (((1,), (1,)), ((), ())), preferred_element_type=F32)


def _dot(a, b):
    return jnp.dot(a, b, preferred_element_type=F32)


def _split_dot(x, m):
    hi = x.astype(BF16)
    lo = (x - hi.astype(F32)).astype(BF16)
    return _dot(hi, m) + _dot(lo, m)


def _split_dot_left(m, x):
    hi = x.astype(BF16)
    lo = (x - hi.astype(F32)).astype(BF16)
    return _dot(m, hi) + _dot(m, lo)


def _layer_norm(v, g, b):
    mu = jnp.mean(v, axis=-1, keepdims=True)
    d = v - mu
    var = jnp.mean(d * d, axis=-1, keepdims=True)
    return d * lax.rsqrt(var + LN_EPS) * g + b


def _mm_kernel(a_ref, b_ref, o_ref, a_bf):
    @pl.when(pl.program_id(1) == 0)
    def _():
        a_bf[...] = a_ref[...].astype(BF16)

    o_ref[...] = _dot(a_bf[...], b_ref[...]).astype(o_ref.dtype)


def _matmul(a, b, tm, tn, out_dtype):
    m, k = a.shape
    n = b.shape[1]
    tm = min(tm, m)
    return pl.pallas_call(
        _mm_kernel,
        grid=(m // tm, n // tn),
        in_specs=[pl.BlockSpec((tm, k), lambda i, j: (i, 0)),
                  pl.BlockSpec((k, tn), lambda i, j: (0, j))],
        out_specs=pl.BlockSpec((tm, tn), lambda i, j: (i, j)),
        out_shape=jax.ShapeDtypeStruct((m, n), out_dtype),
        scratch_shapes=[pltpu.VMEM((tm, k), BF16)],
        compiler_params=_params(("parallel", "arbitrary")),
        name="in_proj",
    )(a, b)


def _diff_attn_kernel(lam_ref, g_ref, q_ref, k_ref, v_ref, o_ref, vx_ref, m_ref, acc_ref, sa_ref, sb_ref, *, tq,
                      lambda_init):
    i = pl.program_id(1)
    dv = DIFF_V_DIM

    @pl.when(i == 0)
    def _():
        vx_ref[:, :dv] = v_ref[...]
        vx_ref[:, dv:] = jnp.ones((vx_ref.shape[0], dv), BF16)

    m_ref[...] = jnp.full(m_ref.shape, -jnp.inf, F32)
    acc_ref[...] = jnp.zeros(acc_ref.shape, F32)
    q = q_ref[...]
    lane = lax.broadcasted_iota(jnp.int32, q.shape, 1)
    zero = jnp.zeros_like(q)
    halves = (jnp.where(lane < DIFF_HEAD_DIM, q, zero), jnp.where(lane >= DIFF_HEAD_DIM, q, zero))

    def scores(j, mi):
        off = pl.multiple_of(j * tq, tq)
        return _nt_dot(halves[mi], k_ref[pl.ds(off, tq), :])

    def softmax_pv(j, get_s, masked):
        off = pl.multiple_of(j * tq, tq)
        vx = vx_ref[pl.ds(off, tq), :]
        if masked:
            row = lax.broadcasted_iota(jnp.int32, (tq, tq), 0)
            col = lax.broadcasted_iota(jnp.int32, (tq, tq), 1)
            causal = col <= row
        for mi in range(2):
            s = get_s(mi)
            if masked:
                s = jnp.where(causal, s, -jnp.inf)
            m_prev = m_ref[mi]
            m_new = jnp.maximum(m_prev, jnp.max(s, axis=1, keepdims=True))
            p = jnp.exp2(s - jnp.tile(m_new, (1, tq // LANES)))
            alpha = jnp.exp2(m_prev - m_new)
            acc_ref[mi] = jnp.tile(alpha, (1, 2)) * acc_ref[mi] + _dot(p.astype(BF16), vx)
            m_ref[mi] = m_new

    def scores_to(j, s_ref):
        for mi in range(2):
            s_ref[mi] = scores(j, mi)

    scores_to(0, sa_ref)

    def pair(t, carry):
        j = 2 * t
        scores_to(j + 1, sb_ref)
        softmax_pv(j, lambda mi: sa_ref[mi], False)
        scores_to(j + 2, sa_ref)
        softmax_pv(j + 1, lambda mi: sb_ref[mi], False)
        return carry

    lax.fori_loop(0, i // 2, pair, 0)

    @pl.when(i % 2 == 1)
    def _():
        softmax_pv(i - 1, lambda mi: sa_ref[mi], False)
        scores_to(i, sa_ref)

    softmax_pv(i, lambda mi: sa_ref[mi], True)

    lv = lam_ref[...]
    lam = (jnp.exp(jnp.sum(lv[0:1] * lv[1:2], axis=1, keepdims=True))
           - jnp.exp(jnp.sum(lv[2:3] * lv[3:4], axis=1, keepdims=True)) + lambda_init)
    a0 = acc_ref[0]
    a1 = acc_ref[1]
    o = a0[:, :dv] / a0[:, dv:] - lam * (a1[:, :dv] / a1[:, dv:])
    ms = jnp.mean(o * o, axis=-1, keepdims=True)
    o = o * lax.rsqrt(ms + LN_EPS) * g_ref[...] * (1.0 - lambda_init)
    o_ref[...] = o.astype(o_ref.dtype)


def _diff_attention(h0, lam_vec, subln_g, lambda_init, tq):
    t = h0.shape[0]
    tq = min(tq, t)
    nq = t // tq
    kern = functools.partial(_diff_attn_kernel, tq=tq, lambda_init=lambda_init)
    return pl.pallas_call(
        kern,
        grid=(DIFF_HEADS, nq),
        in_specs=[
            pl.BlockSpec((4, DIFF_HEAD_DIM), lambda h, i: (0, 0)),
            pl.BlockSpec((1, DIFF_V_DIM), lambda h, i: (0, 0)),
            pl.BlockSpec((tq, LANES), lambda h, i: (i, h)),
            pl.BlockSpec((t, LANES), lambda h, i: (0, DIFF_HEADS + h)),
            pl.BlockSpec((t, LANES), lambda h, i: (0, 2 * DIFF_HEADS + h)),
        ],
        out_specs=pl.BlockSpec((tq, LANES), lambda h, i: (i, h)),
        out_shape=jax.ShapeDtypeStruct((t, DIFF_HEADS * DIFF_V_DIM), BF16),
        scratch_shapes=[pltpu.VMEM((t, 2 * DIFF_V_DIM), BF16), pltpu.VMEM((2, tq, LANES), F32),
                        pltpu.VMEM((2, tq, 2 * DIFF_V_DIM), F32),
                        pltpu.VMEM((2, tq, tq), F32), pltpu.VMEM((2, tq, tq), F32)],
        compiler_params=_params(("arbitrary", "arbitrary")),
        name="diff_attn",
    )(lam_vec, subln_g, h0, h0, h0)


def _gla_kernel(q_ref, k_ref, v_ref, r_ref, gb_ref, wg_ref, bg_ref, ng_ref, o_ref, st_ref, *, tg):
    @pl.when(pl.program_id(0) == 0)
    def _():
        st_ref[...] = jnp.zeros(st_ref.shape, F32)

    c = GLA_CHUNK
    gate = _dot(gb_ref[...], wg_ref[...]) + bg_ref[...]
    log_a = -(jnp.maximum(-gate, 0.0) + jnp.log(1.0 + jnp.exp(-jnp.abs(gate)))) / GLA_GATE_TEMP
    row = lax.broadcasted_iota(jnp.int32, (c, c), 0)
    col = lax.broadcasted_iota(jnp.int32, (c, c), 1)
    tril = row >= col
    tri = jnp.where(tril, 1.0, 0.0).astype(BF16)
    for h in range(GLA_HEADS):
        ksl = slice(h * GLA_DK, (h + 1) * GLA_DK)
        vsl = slice(h * GLA_DV, (h + 1) * GLA_DV)
        for ci in range(tg // c):
            rs = slice(ci * c, (ci + 1) * c)
            la = log_a[rs, ksl]
            b = _split_dot_left(tri, la)
            b_last = b[c - 1:c, :]
            qf = q_ref[rs, ksl].astype(F32) * (GLA_DK ** -0.5)
            kf = k_ref[rs, ksl].astype(F32)
            v = v_ref[rs, vsl]
            q_dec = (qf * jnp.exp(b)).astype(BF16)
            k_dec = (kf * jnp.exp(-b)).astype(BF16)
            k_end = (kf * jnp.exp(b_last - b)).astype(BF16)
            scores = jnp.where(tril, _nt_dot(q_dec, k_dec), 0.0)
            st = st_ref[h]
            o = _dot(scores.astype(BF16), v) + _nt_dot(q_dec, st.astype(BF16))
            kv_t = lax.dot_general(v, k_end, (((0,), (0,)), ((), ())), preferred_element_type=F32)
            st_ref[h] = jnp.exp(b_last) * st + kv_t
            ms = jnp.mean(o * o, axis=-1, keepdims=True)
            o = o * lax.rsqrt(ms + LN_EPS) * ng_ref[...]
            r = r_ref[rs, vsl].astype(F32)
            o_ref[rs, vsl] = (o * (r / (1.0 + jnp.exp(-r)))).astype(o_ref.dtype)


def _gla(h0, wg_pad, b_gate, norm_g, tg):
    t = h0.shape[0]
    tg = min(tg, t)
    qk = GLA_HEADS * GLA_DK
    vw = GLA_HEADS * GLA_DV
    return pl.pallas_call(
        functools.partial(_gla_kernel, tg=tg),
        grid=(t // tg,),
        in_specs=[
            pl.BlockSpec((tg, qk), lambda i: (i, 3072 // qk)),
            pl.BlockSpec((tg, qk), lambda i: (i, 3584 // qk)),
            pl.BlockSpec((tg, vw), lambda i: (i, 4096 // vw)),
            pl.BlockSpec((tg, vw), lambda i: (i, 5120 // vw)),
            pl.BlockSpec((tg, LANES), lambda i: (i, 6144 // LANES)),
            pl.BlockSpec((LANES, qk), lambda i: (0, 0)),
            pl.BlockSpec((1, qk), lambda i: (0, 0)),
            pl.BlockSpec((1, GLA_DV), lambda i: (0, 0)),
        ],
        out_specs=pl.BlockSpec((tg, vw), lambda i: (i, 0)),
        out_shape=jax.ShapeDtypeStruct((t, vw), BF16),
        scratch_shapes=[pltpu.VMEM((GLA_HEADS, GLA_DV, GLA_DK), F32)],
        compiler_params=_params(("arbitrary",)),
        name="gla",
    )(h0, h0, h0, h0, h0, wg_pad, b_gate, norm_g)


def _sb_kernel(q_ref, k_ref, v_ref, o_ref, acc_ref, c_ref, *, tq):
    i = pl.program_id(1)
    hd = SB_HEAD_DIM
    acc_ref[...] = jnp.zeros(acc_ref.shape, F32)
    c_ref[...] = jnp.zeros(c_ref.shape, F32)
    row = lax.broadcasted_iota(jnp.int32, (tq, tq), 0)
    col = lax.broadcasted_iota(jnp.int32, (tq, tq), 1)
    strict = col < row
    tri = jnp.where(row > col, 1.0, 0.0).astype(BF16)

    def block(jb, masked):
        off = pl.multiple_of(jb * tq, tq)
        cmax = None
        for hh in range(SB_HEADS_PER_STEP):
            cols = slice(hh * hd, (hh + 1) * hd)
            k = k_ref[pl.ds(off, tq), cols]
            v = v_ref[pl.ds(off, tq), cols]
            z = _nt_dot(q_ref[:, cols], k) * (SB_HEAD_DIM ** -0.5)
            lsn = -(jnp.maximum(z, 0.0) + jnp.log(1.0 + jnp.exp(-jnp.abs(z))))
            lm = jnp.where(strict, lsn, 0.0) if masked else lsn
            excl = _split_dot(lm, tri) + c_ref[hh]
            w = jnp.exp(z + lsn + excl)
            if masked:
                w = jnp.where(strict, w, 0.0)
            acc_ref[:, cols] += _dot(w.astype(BF16), v)
            c_new = c_ref[hh] + jnp.sum(lm, axis=1, keepdims=True)
            c_ref[hh] = c_new
            hmax = jnp.max(c_new)
            cmax = hmax if cmax is None else jnp.maximum(cmax, hmax)
        return cmax

    cmax = block(i, True)

    def cond(carry):
        jb, cm = carry
        return jnp.logical_and(jb >= 0, cm > -SB_LOG_FLOOR)

    def body(carry):
        jb, _ = carry
        return jb - 1, block(jb, False)

    lax.while_loop(cond, body, (i - 1, cmax))
    o_ref[...] = acc_ref[...].astype(o_ref.dtype)


def _stick_breaking(h1, tq):
    t = h1.shape[0]
    tq = min(tq, t)
    ng = SB_HEADS // SB_HEADS_PER_STEP
    wd = SB_HEADS_PER_STEP * SB_HEAD_DIM
    return pl.pallas_call(
        functools.partial(_sb_kernel, tq=tq),
        grid=(ng, t // tq),
        in_specs=[
            pl.BlockSpec((tq, wd), lambda h, i: (i, h)),
            pl.BlockSpec((t, wd), lambda h, i: (0, ng + h), pipeline_mode=pl.Buffered(1)),
            pl.BlockSpec((t, wd), lambda h, i: (0, 2 * ng + h), pipeline_mode=pl.Buffered(1)),
        ],
        out_specs=pl.BlockSpec((tq, wd), lambda h, i: (i, h)),
        out_shape=jax.ShapeDtypeStruct((t, SB_HEADS * SB_HEAD_DIM), BF16),
        scratch_shapes=[pltpu.VMEM((tq, wd), F32), pltpu.VMEM((SB_HEADS_PER_STEP, tq, 1), F32)],
        compiler_params=_params(("parallel", "arbitrary")),
        name="stick_breaking",
    )(h1, h1, h1)


def _s5_prep_kernel(are_ref, aim_ref, ls_ref, bre_ref, bim_ref, lbr_ref, lbi_ref, bbr_ref, bbi_ref):
    step = jnp.exp(ls_ref[...])
    lam_re = are_ref[...]
    lam_im = aim_ref[...]
    mag = jnp.exp(step * lam_re)
    lb_re = mag * jnp.cos(step * lam_im)
    lb_im = mag * jnp.sin(step * lam_im)
    den = lam_re * lam_re + lam_im * lam_im
    n_re = lb_re - 1.0
    f_re = (n_re * lam_re + lb_im * lam_im) / den
    f_im = (lb_im * lam_re - n_re * lam_im) / den
    lbr_ref[...] = lb_re
    lbi_ref[...] = lb_im
    bbr_ref[...] = f_re * bre_ref[...] - f_im * bim_ref[...]
    bbi_ref[...] = f_re * bim_ref[...] + f_im * bre_ref[...]


def _s5_prep(a_re, a_im, log_step, b_re, b_im):
    rows = S5_GROUPS * S5_GROUP
    rep = lambda p: jnp.repeat(p, S5_GROUP, axis=0)
    to_rows = lambda b: jnp.transpose(b, (0, 2, 1)).reshape(rows, S5_STATE)
    spec = pl.BlockSpec((rows, S5_STATE), lambda: (0, 0))
    shp = jax.ShapeDtypeStruct((rows, S5_STATE), F32)
    return pl.pallas_call(
        _s5_prep_kernel,
        in_specs=[spec, spec, pl.BlockSpec((rows, 1), lambda: (0, 0)), spec, spec],
        out_specs=[spec] * 4,
        out_shape=[shp] * 4,
        name="s5_prep",
    )(rep(a_re), rep(a_im), rep(log_step[:, None]), to_rows(b_re), to_rows(b_im))


def _cmul(ar, ai, br, bi):
    return ar * br - ai * bi, ar * bi + ai * br


def _s5_kernel(u_ref, wbu_ref, wc_ref, lam_ref, d_ref, wglu_ref, bglu_ref, o_ref,
               hs_ref, hp_ref, pw_ref, st_ref, *, tc):
    ns = S5_KT_STATES

    @pl.when(pl.program_id(0) == 0)
    def _():
        hp_ref[...] = jnp.zeros(hp_ref.shape, F32)
        l1r = jnp.broadcast_to(lam_ref[0:1, :], (SUBLANES, lam_ref.shape[1]))
        l1i = jnp.broadcast_to(lam_ref[1:2, :], (SUBLANES, lam_ref.shape[1]))
        l2r, l2i = _cmul(l1r, l1i, l1r, l1i)
        l4r, l4i = _cmul(l2r, l2i, l2r, l2i)
        l8r, l8i = _cmul(l4r, l4i, l4r, l4i)
        ridx = lax.broadcasted_iota(jnp.int32, l1r.shape, 0)
        n = ridx + 1
        pr = jnp.ones_like(l1r)
        pi = jnp.zeros_like(l1r)
        for bit, (lr, li) in ((1, (l1r, l1i)), (2, (l2r, l2i)), (4, (l4r, l4i)), (8, (l8r, l8i))):
            nr, ni = _cmul(pr, pi, lr, li)
            take = (n & bit) != 0
            pr = jnp.where(take, nr, pr)
            pi = jnp.where(take, ni, pi)
        pw_ref[0] = pr
        pw_ref[1] = pi
        for s, (lr, li) in enumerate(((l1r, l1i), (l2r, l2i), (l4r, l4i))):
            keep = ridx >= (1 << s)
            st_ref[2 * s] = jnp.where(keep, lr, 0.0)
            st_ref[2 * s + 1] = jnp.where(keep, li, 0.0)

    u = u_ref[...]
    for kt in range(S5_KT):
        hs_ref[:, kt * 2 * ns:(kt + 1) * 2 * ns] = _dot(u[:, kt * S5_KT_CH:(kt + 1) * S5_KT_CH], wbu_ref[kt])

    w = S5_SCAN_W
    for kt in range(S5_KT):
        for cc in range(ns // w):
            re0 = kt * 2 * ns + cc * w
            im0 = re0 + ns
            l0 = kt * ns + cc * w

            def body(r, carry, re0=re0, im0=im0, l0=l0):
                hpr, hpi = carry
                rows = pl.ds(pl.multiple_of(r * SUBLANES, SUBLANES), SUBLANES)
                xr = hs_ref[rows, re0:re0 + w]
                xi = hs_ref[rows, im0:im0 + w]
                for s in range(3):
                    sr = pltpu.roll(xr, 1 << s, 0)
                    si = pltpu.roll(xi, 1 << s, 0)
                    ar = st_ref[2 * s, :, l0:l0 + w]
                    ai = st_ref[2 * s + 1, :, l0:l0 + w]
                    xr, xi = xr + ar * sr - ai * si, xi + ar * si + ai * sr
                pr = pw_ref[0, :, l0:l0 + w]
                pi = pw_ref[1, :, l0:l0 + w]
                xr, xi = xr + pr * hpr - pi * hpi, xi + pr * hpi + pi * hpr
                hs_ref[rows, re0:re0 + w] = xr
                hs_ref[rows, im0:im0 + w] = xi
                return (jnp.broadcast_to(xr[SUBLANES - 1:SUBLANES, :], (SUBLANES, w)),
                        jnp.broadcast_to(xi[SUBLANES - 1:SUBLANES, :], (SUBLANES, w)))

            hpr, hpi = lax.fori_loop(0, tc // SUBLANES, body,
                                     (hp_ref[:, re0:re0 + w], hp_ref[:, im0:im0 + w]))
            hp_ref[:, re0:re0 + w] = hpr
            hp_ref[:, im0:im0 + w] = hpi

    ys = [_dot(hs_ref[:, kt * 2 * ns:(kt + 1) * 2 * ns].astype(BF16), wc_ref[kt]) for kt in range(S5_KT)]
    y = jnp.concatenate(ys, axis=1) + d_ref[...] * u.astype(F32)
    g = 0.5 * y * (1.0 + jnp.tanh(math.sqrt(2.0 / math.pi) * (y + 0.044715 * (y * y * y))))
    gl = _dot(g.astype(BF16), wglu_ref[...]) + bglu_ref[...]
    o_ref[...] = (g / (1.0 + jnp.exp(-gl))).astype(o_ref.dtype)


def _s5(h1, wbu, wc, lam, d_skip, w_glu, b_glu, tc):
    t = h1.shape[0]
    tc = min(tc, t)
    ns = S5_KT_STATES
    nstate = S5_GROUPS * S5_STATE
    full = lambda shape: pl.BlockSpec(shape, lambda i: (0,) * len(shape))
    return pl.pallas_call(
        functools.partial(_s5_kernel, tc=tc),
        grid=(t // tc,),
        in_specs=[
            pl.BlockSpec((tc, S5_CHANNELS), lambda i: (i, 3)),
            full((S5_KT, S5_KT_CH, 2 * ns)),
            full((S5_KT, 2 * ns, S5_KT_CH)),
            full((2, nstate)),
            full((1, S5_CHANNELS)),
            full((S5_CHANNELS, S5_CHANNELS)),
            full((1, S5_CHANNELS)),
        ],
        out_specs=pl.BlockSpec((tc, S5_CHANNELS), lambda i: (i, 0)),
        out_shape=jax.ShapeDtypeStruct((t, S5_CHANNELS), BF16),
        scratch_shapes=[pltpu.VMEM((tc, 2 * nstate), F32), pltpu.VMEM((SUBLANES, 2 * nstate), F32),
                        pltpu.VMEM((2, SUBLANES, nstate), F32), pltpu.VMEM((6, SUBLANES, nstate), F32)],
        compiler_params=_params(("arbitrary",)),
        name="s5",
    )(h1, wbu, wc, lam, d_skip, w_glu, b_glu)


def _route(probs, sel):
    rows = [sel[e:e + 1, :] for e in range(N_EXPERTS)]
    prow = [probs[e:e + 1, :] for e in range(N_EXPERTS)]
    best_score = None
    for g in range(N_GROUPS):
        a = rows[g * EXPERTS_PER_GROUP:(g + 1) * EXPERTS_PER_GROUP]
        score = None
        for x in range(EXPERTS_PER_GROUP):
            for y in range(x + 1, EXPERTS_PER_GROUP):
                pair = a[x] + a[y]
                score = pair if score is None else jnp.maximum(score, pair)
        if best_score is None:
            best_score, gbest = score, jnp.zeros(score.shape, jnp.int32)
            cs = list(a)
            cp = prow[0:EXPERTS_PER_GROUP]
        else:
            better = score > best_score
            best_score = jnp.where(better, score, best_score)
            gbest = jnp.where(better, g, gbest)
            cs = [jnp.where(better, a[x], cs[x]) for x in range(EXPERTS_PER_GROUP)]
            cp = [jnp.where(better, prow[g * EXPERTS_PER_GROUP + x], cp[x]) for x in range(EXPERTS_PER_GROUP)]

    def first_argmax(vals):
        bv, bi = vals[0], jnp.zeros(vals[0].shape, jnp.int32)
        for x in range(1, len(vals)):
            better = vals[x] > bv
            bv = jnp.where(better, vals[x], bv)
            bi = jnp.where(better, x, bi)
        return bi

    i1 = first_argmax(cs)
    i2 = first_argmax([jnp.where(i1 == x, -jnp.inf, cs[x]) for x in range(EXPERTS_PER_GROUP)])

    def pick(idx):
        out = cp[0]
        for x in range(1, EXPERTS_PER_GROUP):
            out = jnp.where(idx == x, cp[x], out)
        return out

    p1, p2 = pick(i1), pick(i2)
    tot = p1 + p2
    return gbest * EXPERTS_PER_GROUP + i1, gbest * EXPERTS_PER_GROUP + i2, p1 / tot, p2 / tot


def _ln_router_tail(v, g_ref, b_ref, rwh_ref, rwl_ref, rb_ref):
    xn = _layer_norm(v, g_ref[...], b_ref[...])
    xh = xn.astype(BF16)
    xl = (xn - xh.astype(F32)).astype(BF16)
    logits = _nt_dot(rwh_ref[...], xh) + (_nt_dot(rwl_ref[...], xh) + _nt_dot(rwh_ref[...], xl))
    mx = jnp.max(logits, axis=0, keepdims=True)
    ex = jnp.exp(logits - mx)
    probs = ex / jnp.sum(ex, axis=0, keepdims=True)
    e1, e2, w1, w2 = _route(probs, probs + rb_ref[...])
    n = e1.shape[1]
    ei = jnp.concatenate([e1, e2, jnp.zeros((SUBLANES - 2, n), jnp.int32)], axis=0)
    ew = jnp.concatenate([w1, w2, jnp.zeros((SUBLANES - 2, n), F32)], axis=0)
    return xn, ei, ew


def _outproj_kernel(ma_ref, mb_ref, wa_ref, wb_ref, xr_ref, g_ref, b_ref, rwh_ref, rwl_ref, rb_ref,
                    x_ref, ei_ref, ew_ref, *, sub):
    for s in range(x_ref.shape[0] // sub):
        rows = slice(s * sub, (s + 1) * sub)
        v = (DEEPNORM_ALPHA * xr_ref[rows, :] + _dot(ma_ref[rows, :], wa_ref[...])
             + _dot(mb_ref[rows, :], wb_ref[...]))
        xn, ei, ew = _ln_router_tail(v, g_ref, b_ref, rwh_ref, rwl_ref, rb_ref)
        x_ref[rows, :] = xn
        ei_ref[:, rows] = ei
        ew_ref[:, rows] = ew


def _outproj_ln_router(mix_a, mix_b, w_out, x_res, ln_g, ln_b, rwt, rb, tm, sub):
    t = x_res.shape[0]
    rwt_hi = rwt.astype(BF16)
    rwt_lo = (rwt - rwt_hi.astype(F32)).astype(BF16)
    tm = min(tm, t)
    sub = min(sub, tm)
    half = w_out.shape[0] // 2
    row = lambda w: pl.BlockSpec((tm, w), lambda i: (i, 0))
    full = lambda shape: pl.BlockSpec(shape, lambda i: (0,) * len(shape))
    return pl.pallas_call(
        functools.partial(_outproj_kernel, sub=sub),
        grid=(t // tm,),
        in_specs=[row(half), row(half),
                  pl.BlockSpec((half, D_MODEL), lambda i: (0, 0)), pl.BlockSpec((half, D_MODEL), lambda i: (1, 0)),
                  row(D_MODEL), full((1, D_MODEL)), full((1, D_MODEL)),
                  full((N_EXPERTS, D_MODEL)), full((N_EXPERTS, D_MODEL)), full((N_EXPERTS, 1))],
        out_specs=[row(D_MODEL),
                   pl.BlockSpec((SUBLANES, tm), lambda i: (0, i)), pl.BlockSpec((SUBLANES, tm), lambda i: (0, i))],
        out_shape=[jax.ShapeDtypeStruct((t, D_MODEL), F32),
                   jax.ShapeDtypeStruct((SUBLANES, t), jnp.int32), jax.ShapeDtypeStruct((SUBLANES, t), F32)],
        compiler_params=_params(("parallel",)),
        name="outproj_ln_router",
    )(mix_a, mix_b, w_out, w_out, x_res, ln_g, ln_b, rwt_hi, rwt_lo, rb)


def _rank_kernel(e_ref, rank_ref, cnt_ref, base_ref, *, tn):
    @pl.when(pl.program_id(0) == 0)
    def _():
        base_ref[...] = jnp.zeros(base_ref.shape, F32)

    e = e_ref[...]
    rows = lax.broadcasted_iota(jnp.int32, (N_EXPERTS, tn), 0)
    oh1 = rows == e[0:1, :]
    oh2 = rows == e[1:2, :]
    oh = jnp.where(jnp.logical_or(oh1, oh2), 1.0, 0.0)
    r = lax.broadcasted_iota(jnp.int32, (tn, tn), 0)
    c = lax.broadcasted_iota(jnp.int32, (tn, tn), 1)
    earlier = jnp.where(r < c, 1.0, 0.0).astype(BF16)
    base = base_ref[...]
    before = _dot(oh.astype(BF16), earlier) + jnp.tile(base, (1, tn // LANES))
    r1 = jnp.sum(jnp.where(oh1, before, 0.0), axis=0, keepdims=True)
    r2 = jnp.sum(jnp.where(oh2, before, 0.0), axis=0, keepdims=True)
    rank_ref[...] = jnp.concatenate([r1, r2, jnp.zeros((SUBLANES - 2, tn), F32)], axis=0).astype(jnp.int32)
    base = base + jnp.sum(oh, axis=1, keepdims=True)
    base_ref[...] = base
    cnt_ref[...] = base.astype(jnp.int32)


def _rank(eidx, tn):
    t = eidx.shape[1]
    tn = min(tn, t)
    return pl.pallas_call(
        functools.partial(_rank_kernel, tn=tn),
        grid=(t // tn,),
        in_specs=[pl.BlockSpec((SUBLANES, tn), lambda i: (0, i))],
        out_specs=[pl.BlockSpec((SUBLANES, tn), lambda i: (0, i)),
                   pl.BlockSpec((N_EXPERTS, LANES), lambda i: (0, 0))],
        out_shape=[jax.ShapeDtypeStruct((SUBLANES, t), jnp.int32),
                   jax.ShapeDtypeStruct((N_EXPERTS, LANES), jnp.int32)],
        scratch_shapes=[pltpu.VMEM((N_EXPERTS, LANES), F32)],
        compiler_params=_params(("arbitrary",)),
        name="moe_rank",
    )(eidx)


def _moe_kernel(te_ref, nu_ref, xs_ref, wg_ref, wu_ref, wd_ref, y_ref, wgu_s, wd_s):
    tile = pl.program_id(0)
    used = tile < nu_ref[0]
    fresh = jnp.logical_or(tile == 0, te_ref[tile] != te_ref[jnp.maximum(tile - 1, 0)])

    @pl.when(jnp.logical_and(used, fresh))
    def _():
        wgu_s[:, :D_EXPERT] = wg_ref[...].astype(BF16)
        wgu_s[:, D_EXPERT:] = wu_ref[...].astype(BF16)
        wd_s[...] = wd_ref[...].astype(BF16)

    @pl.when(used)
    def _():
        for s in range(xs_ref.shape[0] // MOE_SUB):
            rows = slice(s * MOE_SUB, (s + 1) * MOE_SUB)
            hgu = _dot(xs_ref[rows, :].astype(BF16), wgu_s[...])
            hg = hgu[:, :D_EXPERT]
            hu = hgu[:, D_EXPERT:]
            h = (hg / (1.0 + jnp.exp(-hg))) * hu
            y_ref[rows, :] = _dot(h.astype(BF16), wd_s[...])

    @pl.when(jnp.logical_not(used))
    def _():
        y_ref[...] = jnp.zeros(y_ref.shape, F32)


def _moe(tile_expert, n_used, xs, w_gate, w_up, w_down, tm):
    t_pad = xs.shape[0]
    grid_spec = pltpu.PrefetchScalarGridSpec(
        num_scalar_prefetch=2,
        grid=(t_pad // tm,),
        in_specs=[
            pl.BlockSpec((tm, D_MODEL), lambda i, te, nu: (i, 0)),
            pl.BlockSpec((None, D_MODEL, D_EXPERT), lambda i, te, nu: (te[i], 0, 0)),
            pl.BlockSpec((None, D_MODEL, D_EXPERT), lambda i, te, nu: (te[i], 0, 0)),
            pl.BlockSpec((None, D_EXPERT, D_MODEL), lambda i, te, nu: (te[i], 0, 0)),
        ],
        out_specs=pl.BlockSpec((tm, D_MODEL), lambda i, te, nu: (i, 0)),
        scratch_shapes=[pltpu.VMEM((D_MODEL, 2 * D_EXPERT), BF16), pltpu.VMEM((D_EXPERT, D_MODEL), BF16)],
    )
    return pl.pallas_call(
        _moe_kernel,
        grid_spec=grid_spec,
        out_shape=jax.ShapeDtypeStruct((t_pad, D_MODEL), F32),
        compiler_params=_params(("arbitrary",)),
        name="moe",
    )(tile_expert, n_used, xs, w_gate, w_up, w_down)


def _moe_ln_kernel(x_ref, y1_ref, y2_ref, w_ref, g_ref, b_ref, o_ref):
    w = w_ref[...]
    v = DEEPNORM_ALPHA * x_ref[...] + (w[:, 0:1] * y1_ref[...] + w[:, 1:2] * y2_ref[...])
    o_ref[...] = _layer_norm(v, g_ref[...], b_ref[...])


def _moe_ln(x, y1, y2, w_cols, ln_g, ln_b, tm):
    t = x.shape[0]
    tm = min(tm, t)
    row = pl.BlockSpec((tm, D_MODEL), lambda i: (i, 0))
    vec = pl.BlockSpec((1, D_MODEL), lambda i: (0, 0))
    return pl.pallas_call(
        _moe_ln_kernel,
        grid=(t // tm,),
        in_specs=[row, row, row, pl.BlockSpec((tm, SUBLANES), lambda i: (i, 0)), vec, vec],
        out_specs=row,
        out_shape=jax.ShapeDtypeStruct((t, D_MODEL), F32),
        compiler_params=_params(("parallel",)),
        name="moe_ln",
    )(x, y1, y2, w_cols, ln_g, ln_b)


def _moe_block(x, eidx, ew, w_gate, w_up, w_down, ln_g, ln_b, tm):
    t = x.shape[0]
    tm = min(tm, t)
    rank, cnt = _rank(eidx, 512)
    counts = cnt[:, 0]
    padded = ((counts + tm - 1) // tm) * tm
    ends = jnp.cumsum(padded)
    starts = ends - padded
    e2 = eidx[:2]
    start_of = jnp.zeros_like(e2)
    for k in range(N_EXPERTS):
        start_of = jnp.where(e2 == k, starts[k], start_of)
    dest = start_of + rank[:2]
    t_pad = 2 * t + N_EXPERTS * tm
    tok = jnp.tile(jnp.arange(t, dtype=jnp.int32), 2)
    src_tok = (jnp.arange(t_pad, dtype=jnp.int32) % t).at[dest.reshape(2 * t)].set(
        tok, mode="promise_in_bounds", unique_indices=True)
    tile_start = jnp.arange(t_pad // tm, dtype=jnp.int32) * tm
    tile_expert = jnp.minimum(jnp.sum((ends[None, :] <= tile_start[:, None]).astype(jnp.int32), axis=1),
                              N_EXPERTS - 1)
    n_used = (ends[-1] // tm).astype(jnp.int32).reshape(1)
    xs = x.at[src_tok].get(mode="promise_in_bounds")
    ys = _moe(tile_expert, n_used, xs, w_gate, w_up, w_down, tm)
    y1 = ys.at[dest[0]].get(mode="promise_in_bounds")
    y2 = ys.at[dest[1]].get(mode="promise_in_bounds")
    return _moe_ln(x, y1, y2, ew.T, ln_g, ln_b, tm)


def _block_diag_tiles(m, rows_per_group, cols_per_group):
    gpt = S5_GROUPS // S5_KT
    m = m.reshape(S5_KT, gpt, rows_per_group, cols_per_group)
    eye = jnp.eye(gpt, dtype=m.dtype)
    bd = jnp.einsum("kgrc,gh->kgrhc", m, eye)
    return bd.reshape(S5_KT, gpt * rows_per_group, gpt * cols_per_group)


def kernel(x, router_w, router_b, l0_w_in, l0_diff_lam_q1, l0_diff_lam_k1, l0_diff_lam_q2, l0_diff_lam_k2, l0_diff_subln_g, l0_gla_w_gate, l0_gla_b_gate, l0_gla_norm_g, l0_w_out, l0_ln1_g, l0_ln1_b, l0_moe_w_gate, l0_moe_w_up, l0_moe_w_down, l0_ln2_g, l0_ln2_b, l1_w_in, l1_s5_a_re, l1_s5_a_im, l1_s5_log_step, l1_s5_b_re, l1_s5_b_im, l1_s5_c_re, l1_s5_c_im, l1_s5_d, l1_s5_w_glu, l1_s5_b_glu, l1_w_out, l1_ln1_g, l1_ln1_b, l1_moe_w_gate, l1_moe_w_up, l1_moe_w_down, l1_ln2_g, l1_ln2_b):
    bsz, t, d = x.shape
    assert bsz == 1 and d == D_MODEL
    x0 = x.reshape(t, d)
    vec = lambda p: p.reshape(1, -1).astype(F32)
    rwt = router_w.T.astype(F32)
    rb = router_b.reshape(N_EXPERTS, 1).astype(F32)

    lambda_init = 0.8 - 0.6 * math.exp(-0.3 * 0)
    c = [0, 1024, 2048, 3072, 3584, 4096, 5120, 5136, 6160]
    q_scale = math.log2(math.e) * DIFF_HEAD_DIM ** -0.5
    w0 = jnp.concatenate([l0_w_in[:, :c[1]] * q_scale, l0_w_in[:, c[1]:c[6]], l0_w_in[:, c[7]:c[8]],
                          l0_w_in[:, c[6]:c[7]], jnp.zeros((d, LANES - GLA_GATE_RANK), l0_w_in.dtype)],
                         axis=1).astype(BF16)
    h0 = _matmul(x0, w0, 1024, 896, BF16)
    lam_vec = jnp.stack([l0_diff_lam_q1, l0_diff_lam_k1, l0_diff_lam_q2, l0_diff_lam_k2]).astype(F32)
    o_a = _diff_attention(h0, lam_vec, vec(l0_diff_subln_g), lambda_init, 1024)
    wg_pad = jnp.concatenate([l0_gla_w_gate, jnp.zeros((LANES - GLA_GATE_RANK, GLA_HEADS * GLA_DK), F32)],
                             axis=0).astype(BF16)
    o_b = _gla(h0, wg_pad, vec(l0_gla_b_gate), vec(l0_gla_norm_g), 256)
    x1, eidx, ew = _outproj_ln_router(o_a, o_b, l0_w_out.astype(BF16), x0, vec(l0_ln1_g), vec(l0_ln1_b),
                                      rwt, rb, 512, 256)
    x2 = _moe_block(x1, eidx, ew, l0_moe_w_gate, l0_moe_w_up, l0_moe_w_down, vec(l0_ln2_g), vec(l0_ln2_b), 256)

    h1 = _matmul(x2, l1_w_in.astype(BF16), 1024, 1024, BF16)
    o_c = _stick_breaking(h1, 256)
    lbr, lbi, bbr, bbi = _s5_prep(l1_s5_a_re, l1_s5_a_im, l1_s5_log_step, l1_s5_b_re, l1_s5_b_im)
    lam = jnp.stack([lbr[::S5_GROUP].reshape(-1), lbi[::S5_GROUP].reshape(-1)])
    bbr = bbr.reshape(S5_GROUPS, S5_GROUP, S5_STATE)
    bbi = bbi.reshape(S5_GROUPS, S5_GROUP, S5_STATE)
    wbu = jnp.concatenate([_block_diag_tiles(bbr, S5_GROUP, S5_STATE),
                           _block_diag_tiles(bbi, S5_GROUP, S5_STATE)], axis=2).astype(BF16)
    c_re_t = jnp.transpose(l1_s5_c_re, (0, 2, 1))
    c_im_t = jnp.transpose(l1_s5_c_im, (0, 2, 1))
    wc = jnp.concatenate([_block_diag_tiles(c_re_t, S5_STATE, S5_GROUP),
                          _block_diag_tiles(-c_im_t, S5_STATE, S5_GROUP)], axis=1).astype(BF16)
    o_d = _s5(h1, wbu, wc, lam, vec(l1_s5_d), l1_s5_w_glu.astype(BF16), vec(l1_s5_b_glu), 256)
    x3, eidx1, ew1 = _outproj_ln_router(o_c, o_d, l1_w_out.astype(BF16), x2, vec(l1_ln1_g), vec(l1_ln1_b),
                                        rwt, rb, 512, 256)
    x4 = _moe_block(x3, eidx1, ew1, l1_moe_w_gate, l1_moe_w_up, l1_moe_w_down, vec(l1_ln2_g), vec(l1_ln2_b), 256)
    return x4.reshape(bsz, t, d)
```

```python
import functools
import math

import jax
import jax.numpy as jnp
from jax import lax
from jax.experimental import pallas as pl
from jax.experimental.pallas import tpu as pltpu

F32 = jnp.float32
BF16 = jnp.bfloat16

D_MODEL = 2048
DEPTH = 2
DIFF_HEADS = 8
DIFF_HEAD_DIM = 64
DIFF_V_DIM = 128
GLA_HEADS = 4
GLA_DK = 128
GLA_DV = 256
GLA_GATE_RANK = 16
GLA_GATE_TEMP = 16.0
GLA_CHUNK = 64
SB_HEADS = 8
SB_HEAD_DIM = 128
S5_CHANNELS = 1024
S5_GROUP = 16
S5_GROUPS = 64
S5_STATE = 64
N_EXPERTS = 16
N_GROUPS = 4
EXPERTS_PER_GROUP = 4
D_EXPERT = 640
DEEPNORM_ALPHA = (2.0 * DEPTH) ** 0.25
LN_EPS = 1e-5

LANES = 128
SUBLANES = 8
VMEM_LIMIT = 56 * 1024 * 1024
SB_LOG_FLOOR = 120.0
SB_HEADS_PER_STEP = 4
MOE_SUB = 256

S5_KT = 4
S5_KT_CH = S5_CHANNELS // S5_KT
S5_KT_STATES = (S5_GROUPS // S5_KT) * S5_STATE
S5_SCAN_W = 512
S5_L = 8


def _params(sem, flags=None):
    return pltpu.CompilerParams(dimension_semantics=sem, vmem_limit_bytes=VMEM_LIMIT, flags=flags)


def _nt_dot(a, b):
    return lax.dot_general(a, b, (((1,), (1,)), ((), ())), preferred_element_type=F32)


def _dot(a, b):
    return jnp.dot(a, b, preferred_element_type=F32)


def _split_dot(x, m):
    hi = x.astype(BF16)
    lo = (x - hi.astype(F32)).astype(BF16)
    return _dot(hi, m) + _dot(lo, m)


def _split_dot_left(m, x):
    hi = x.astype(BF16)
    lo = (x - hi.astype(F32)).astype(BF16)
    return _dot(m, hi) + _dot(m, lo)


def _layer_norm(v, g, b):
    mu = jnp.mean(v, axis=-1, keepdims=True)
    d = v - mu
    var = jnp.mean(d * d, axis=-1, keepdims=True)
    return d * lax.rsqrt(var + LN_EPS) * g + b


def _mm_kernel(a_ref, b_ref, o_ref, a_bf):
    @pl.when(pl.program_id(1) == 0)
    def _():
        a_bf[...] = a_ref[...].astype(BF16)

    o_ref[...] = _dot(a_bf[...], b_ref[...]).astype(o_ref.dtype)


def _matmul(a, b, tm, tn, out_dtype):
    m, k = a.shape
    n = b.shape[1]
    tm = min(tm, m)
    return pl.pallas_call(
        _mm_kernel,
        grid=(m // tm, n // tn),
        in_specs=[pl.BlockSpec((tm, k), lambda i, j: (i, 0)),
                  pl.BlockSpec((k, tn), lambda i, j: (0, j))],
        out_specs=pl.BlockSpec((tm, tn), lambda i, j: (i, j)),
        out_shape=jax.ShapeDtypeStruct((m, n), out_dtype),
        scratch_shapes=[pltpu.VMEM((tm, k), BF16)],
        compiler_params=_params(("parallel", "arbitrary")),
        name="in_proj",
    )(a, b)


def _diff_attn_kernel(lam_ref, g_ref, q_ref, k_ref, v_ref, o_ref, vx_ref, m_ref, acc_ref, sa_ref, sb_ref, *, tq,
                      lambda_init):
    i = pl.program_id(1)
    dv = DIFF_V_DIM

    @pl.when(i == 0)
    def _():
        vx_ref[:, :dv] = v_ref[...]
        vx_ref[:, dv:] = jnp.ones((vx_ref.shape[0], dv), BF16)

    m_ref[...] = jnp.full(m_ref.shape, -jnp.inf, F32)
    acc_ref[...] = jnp.zeros(acc_ref.shape, F32)
    q = q_ref[...]
    lane = lax.broadcasted_iota(jnp.int32, q.shape, 1)
    zero = jnp.zeros_like(q)
    halves = (jnp.where(lane < DIFF_HEAD_DIM, q, zero), jnp.where(lane >= DIFF_HEAD_DIM, q, zero))

    def scores(j, mi):
        off = pl.multiple_of(j * tq, tq)
        return _nt_dot(halves[mi], k_ref[pl.ds(off, tq), :])

    def softmax_pv(j, get_s, masked):
        off = pl.multiple_of(j * tq, tq)
        vx = vx_ref[pl.ds(off, tq), :]
        if masked:
            row = lax.broadcasted_iota(jnp.int32, (tq, tq), 0)
            col = lax.broadcasted_iota(jnp.int32, (tq, tq), 1)
            causal = col <= row
        for mi in range(2):
            s = get_s(mi)
            if masked:
                s = jnp.where(causal, s, -jnp.inf)
            m_prev = m_ref[mi]
            m_new = jnp.maximum(m_prev, jnp.max(s, axis=1, keepdims=True))
            p = jnp.exp2(s - jnp.tile(m_new, (1, tq // LANES)))
            alpha = jnp.exp2(m_prev - m_new)
            acc_ref[mi] = jnp.tile(alpha, (1, 2)) * acc_ref[mi] + _dot(p.astype(BF16), vx)
            m_ref[mi] = m_new

    def scores_to(j, s_ref):
        for mi in range(2):
            s_ref[mi] = scores(j, mi)

    scores_to(0, sa_ref)

    def pair(t, carry):
        j = 2 * t
        scores_to(j + 1, sb_ref)
        softmax_pv(j, lambda mi: sa_ref[mi], False)
        scores_to(j + 2, sa_ref)
        softmax_pv(j + 1, lambda mi: sb_ref[mi], False)
        return carry

    lax.fori_loop(0, i // 2, pair, 0)

    @pl.when(i % 2 == 1)
    def _():
        softmax_pv(i - 1, lambda mi: sa_ref[mi], False)
        scores_to(i, sa_ref)

    softmax_pv(i, lambda mi: sa_ref[mi], True)

    lv = lam_ref[...]
    lam = (jnp.exp(jnp.sum(lv[0:1] * lv[1:2], axis=1, keepdims=True))
           - jnp.exp(jnp.sum(lv[2:3] * lv[3:4], axis=1, keepdims=True)) + lambda_init)
    a0 = acc_ref[0]
    a1 = acc_ref[1]
    o = a0[:, :dv] / a0[:, dv:] - lam * (a1[:, :dv] / a1[:, dv:])
    ms = jnp.mean(o * o, axis=-1, keepdims=True)
    o = o * lax.rsqrt(ms + LN_EPS) * g_ref[...] * (1.0 - lambda_init)
    o_ref[...] = o.astype(o_ref.dtype)


def _diff_attention(h0, lam_vec, subln_g, lambda_init, tq):
    t = h0.shape[0]
    tq = min(tq, t)
    nq = t // tq
    kern = functools.partial(_diff_attn_kernel, tq=tq, lambda_init=lambda_init)
    return pl.pallas_call(
        kern,
        grid=(DIFF_HEADS, nq),
        in_specs=[
            pl.BlockSpec((4, DIFF_HEAD_DIM), lambda h, i: (0, 0)),
            pl.BlockSpec((1, DIFF_V_DIM), lambda h, i: (0, 0)),
            pl.BlockSpec((tq, LANES), lambda h, i: (i, h)),
            pl.BlockSpec((t, LANES), lambda h, i: (0, DIFF_HEADS + h)),
            pl.BlockSpec((t, LANES), lambda h, i: (0, 2 * DIFF_HEADS + h)),
        ],
        out_specs=pl.BlockSpec((tq, LANES), lambda h, i: (i, h)),
        out_shape=jax.ShapeDtypeStruct((t, DIFF_HEADS * DIFF_V_DIM), BF16),
        scratch_shapes=[pltpu.VMEM((t, 2 * DIFF_V_DIM), BF16), pltpu.VMEM((2, tq, LANES), F32),
                        pltpu.VMEM((2, tq, 2 * DIFF_V_DIM), F32),
                        pltpu.VMEM((2, tq, tq), F32), pltpu.VMEM((2, tq, tq), F32)],
        compiler_params=_params(("arbitrary", "arbitrary")),
        name="diff_attn",
    )(lam_vec, subln_g, h0, h0, h0)


def _gla_kernel(q_ref, k_ref, v_ref, r_ref, gb_ref, wg_ref, bg_ref, ng_ref, o_ref, st_ref, *, tg):
    @pl.when(pl.program_id(0) == 0)
    def _():
        st_ref[...] = jnp.zeros(st_ref.shape, F32)

    c = GLA_CHUNK
    gate = _dot(gb_ref[...], wg_ref[...]) + bg_ref[...]
    log_a = -(jnp.maximum(-gate, 0.0) + jnp.log(1.0 + jnp.exp(-jnp.abs(gate)))) / GLA_GATE_TEMP
    row = lax.broadcasted_iota(jnp.int32, (c, c), 0)
    col = lax.broadcasted_iota(jnp.int32, (c, c), 1)
    tril = row >= col
    tri = jnp.where(tril, 1.0, 0.0).astype(BF16)
    for h in range(GLA_HEADS):
        ksl = slice(h * GLA_DK, (h + 1) * GLA_DK)
        vsl = slice(h * GLA_DV, (h + 1) * GLA_DV)
        for ci in range(tg // c):
            rs = slice(ci * c, (ci + 1) * c)
            la = log_a[rs, ksl]
            b = _split_dot_left(tri, la)
            b_last = b[c - 1:c, :]
            qf = q_ref[rs, ksl].astype(F32) * (GLA_DK ** -0.5)
            kf = k_ref[rs, ksl].astype(F32)
            v = v_ref[rs, vsl]
            q_dec = (qf * jnp.exp(b)).astype(BF16)
            k_dec = (kf * jnp.exp(-b)).astype(BF16)
            k_end = (kf * jnp.exp(b_last - b)).astype(BF16)
            scores = jnp.where(tril, _nt_dot(q_dec, k_dec), 0.0)
            st = st_ref[h]
            o = _dot(scores.astype(BF16), v) + _nt_dot(q_dec, st.astype(BF16))
            kv_t = lax.dot_general(v, k_end, (((0,), (0,)), ((), ())), preferred_element_type=F32)
            st_ref[h] = jnp.exp(b_last) * st + kv_t
            ms = jnp.mean(o * o, axis=-1, keepdims=True)
            o = o * lax.rsqrt(ms + LN_EPS) * ng_ref[...]
            r = r_ref[rs, vsl].astype(F32)
            o_ref[rs, vsl] = (o * (r / (1.0 + jnp.exp(-r)))).astype(o_ref.dtype)


def _gla(h0, wg_pad, b_gate, norm_g, tg):
    t = h0.shape[0]
    tg = min(tg, t)
    qk = GLA_HEADS * GLA_DK
    vw = GLA_HEADS * GLA_DV
    return pl.pallas_call(
        functools.partial(_gla_kernel, tg=tg),
        grid=(t // tg,),
        in_specs=[
            pl.BlockSpec((tg, qk), lambda i: (i, 3072 // qk)),
            pl.BlockSpec((tg, qk), lambda i: (i, 3584 // qk)),
            pl.BlockSpec((tg, vw), lambda i: (i, 4096 // vw)),
            pl.BlockSpec((tg, vw), lambda i: (i, 5120 // vw)),
            pl.BlockSpec((tg, LANES), lambda i: (i, 6144 // LANES)),
            pl.BlockSpec((LANES, qk), lambda i: (0, 0)),
            pl.BlockSpec((1, qk), lambda i: (0, 0)),
            pl.BlockSpec((1, GLA_DV), lambda i: (0, 0)),
        ],
        out_specs=pl.BlockSpec((tg, vw), lambda i: (i, 0)),
        out_shape=jax.ShapeDtypeStruct((t, vw), BF16),
        scratch_shapes=[pltpu.VMEM((GLA_HEADS, GLA_DV, GLA_DK), F32)],
        compiler_params=_params(("arbitrary",)),
        name="gla",
    )(h0, h0, h0, h0, h0, wg_pad, b_gate, norm_g)


def _sb_kernel(q_ref, k_ref, v_ref, o_ref, acc_ref, c_ref, *, tq):
    i = pl.program_id(1)
    hd = SB_HEAD_DIM
    acc_ref[...] = jnp.zeros(acc_ref.shape, F32)
    c_ref[...] = jnp.zeros(c_ref.shape, F32)
    row = lax.broadcasted_iota(jnp.int32, (tq, tq), 0)
    col = lax.broadcasted_iota(jnp.int32, (tq, tq), 1)
    strict = col < row
    tri = jnp.where(row > col, 1.0, 0.0).astype(BF16)

    def block(jb, masked):
        off = pl.multiple_of(jb * tq, tq)
        cmax = None
        for hh in range(SB_HEADS_PER_STEP):
            cols = slice(hh * hd, (hh + 1) * hd)
            k = k_ref[pl.ds(off, tq), cols]
            v = v_ref[pl.ds(off, tq), cols]
            z = _nt_dot(q_ref[:, cols], k) * (SB_HEAD_DIM ** -0.5)
            lsn = -(jnp.maximum(z, 0.0) + jnp.log(1.0 + jnp.exp(-jnp.abs(z))))
            lm = jnp.where(strict, lsn, 0.0) if masked else lsn
            excl = _split_dot(lm, tri) + c_ref[hh]
            w = jnp.exp(z + lsn + excl)
            if masked:
                w = jnp.where(strict, w, 0.0)
            acc_ref[:, cols] += _dot(w.astype(BF16), v)
            c_new = c_ref[hh] + jnp.sum(lm, axis=1, keepdims=True)
            c_ref[hh] = c_new
            hmax = jnp.max(c_new)
            cmax = hmax if cmax is None else jnp.maximum(cmax, hmax)
        return cmax

    cmax = block(i, True)

    def cond(carry):
        jb, cm = carry
        return jnp.logical_and(jb >= 0, cm > -SB_LOG_FLOOR)

    def body(carry):
        jb, _ = carry
        return jb - 1, block(jb, False)

    lax.while_loop(cond, body, (i - 1, cmax))
    o_ref[...] = acc_ref[...].astype(o_ref.dtype)


def _stick_breaking(h1, tq):
    t = h1.shape[0]
    tq = min(tq, t)
    ng = SB_HEADS // SB_HEADS_PER_STEP
    wd = SB_HEADS_PER_STEP * SB_HEAD_DIM
    return pl.pallas_call(
        functools.partial(_sb_kernel, tq=tq),
        grid=(ng, t // tq),
        in_specs=[
            pl.BlockSpec((tq, wd), lambda h, i: (i, h)),
            pl.BlockSpec((t, wd), lambda h, i: (0, ng + h), pipeline_mode=pl.Buffered(1)),
            pl.BlockSpec((t, wd), lambda h, i: (0, 2 * ng + h), pipeline_mode=pl.Buffered(1)),
        ],
        out_specs=pl.BlockSpec((tq, wd), lambda h, i: (i, h)),
        out_shape=jax.ShapeDtypeStruct((t, SB_HEADS * SB_HEAD_DIM), BF16),
        scratch_shapes=[pltpu.VMEM((tq, wd), F32), pltpu.VMEM((SB_HEADS_PER_STEP, tq, 1), F32)],
        compiler_params=_params(("parallel", "arbitrary")),
        name="stick_breaking",
    )(h1, h1, h1)


def _s5_prep_kernel(are_ref, aim_ref, ls_ref, bre_ref, bim_ref, lbr_ref, lbi_ref, bbr_ref, bbi_ref):
    step = jnp.exp(ls_ref[...])
    lam_re = are_ref[...]
    lam_im = aim_ref[...]
    mag = jnp.exp(step * lam_re)
    lb_re = mag * jnp.cos(step * lam_im)
    lb_im = mag * jnp.sin(step * lam_im)
    den = lam_re * lam_re + lam_im * lam_im
    n_re = lb_re - 1.0
    f_re = (n_re * lam_re + lb_im * lam_im) / den
    f_im = (lb_im * lam_re - n_re * lam_im) / den
    lbr_ref[...] = lb_re
    lbi_ref[...] = lb_im
    bbr_ref[...] = f_re * bre_ref[...] - f_im * bim_ref[...]
    bbi_ref[...] = f_re * bim_ref[...] + f_im * bre_ref[...]


def _s5_prep(a_re, a_im, log_step, b_re, b_im):
    rows = S5_GROUPS * S5_GROUP
    rep = lambda p: jnp.repeat(p, S5_GROUP, axis=0)
    to_rows = lambda b: jnp.transpose(b, (0, 2, 1)).reshape(rows, S5_STATE)
    spec = pl.BlockSpec((rows, S5_STATE), lambda: (0, 0))
    shp = jax.ShapeDtypeStruct((rows, S5_STATE), F32)
    return pl.pallas_call(
        _s5_prep_kernel,
        in_specs=[spec, spec, pl.BlockSpec((rows, 1), lambda: (0, 0)), spec, spec],
        out_specs=[spec] * 4,
        out_shape=[shp] * 4,
        name="s5_prep",
    )(rep(a_re), rep(a_im), rep(log_step[:, None]), to_rows(b_re), to_rows(b_im))


def _cmul(ar, ai, br, bi):
    return ar * br - ai * bi, ar * bi + ai * br


def _s5_chunk_weights(lb_re, lb_im, bb_re, bb_im, c_re, c_im):
    hp = lax.Precision.HIGHEST
    L = S5_L
    pr = [jnp.ones_like(lb_re)]
    pi = [jnp.zeros_like(lb_re)]
    for _ in range(L):
        nr, ni = _cmul(pr[-1], pi[-1], lb_re, lb_im)
        pr.append(nr)
        pi.append(ni)
    pr = jnp.stack(pr)
    pi = jnp.stack(pi)
    d_re = pr[:L, :, :, None] * bb_re[None] - pi[:L, :, :, None] * bb_im[None]
    d_im = pr[:L, :, :, None] * bb_im[None] + pi[:L, :, :, None] * bb_re[None]
    kk = (jnp.einsum("gcp,kgpd->kgdc", c_re, d_re, precision=hp)
          - jnp.einsum("gcp,kgpd->kgdc", c_im, d_im, precision=hp))
    gpt = S5_GROUPS // S5_KT
    eye = jnp.eye(gpt, dtype=F32)
    tile = lambda a: a.reshape(a.shape[:1] + (S5_KT, gpt) + a.shape[2:])
    lag = jnp.arange(L)
    sel = (lag[None, :, None] - lag[:, None, None] == lag[None, None, :]).astype(F32)
    mt = jnp.einsum("sik,ktgdc,gh->tsgdihc", sel, tile(kk), eye)
    mt = mt.reshape(S5_KT, L * gpt * S5_GROUP, L * gpt * S5_GROUP)
    d_re_t = tile(d_re)[::-1]
    d_im_t = tile(d_im)[::-1]
    bb_r = jnp.einsum("stgpd,gh->tsgdhp", d_re_t, eye).reshape(S5_KT, L * gpt * S5_GROUP, gpt * S5_STATE)
    bb_i = jnp.einsum("stgpd,gh->tsgdhp", d_im_t, eye).reshape(S5_KT, L * gpt * S5_GROUP, gpt * S5_STATE)
    bbs = jnp.concatenate([bb_r, bb_i], axis=2)
    a = tile(pr[1:L + 1])
    b = tile(pi[1:L + 1])
    cr = c_re.reshape(S5_KT, gpt, S5_GROUP, S5_STATE)
    ci = c_im.reshape(S5_KT, gpt, S5_GROUP, S5_STATE)
    w_re = cr[None] * a[:, :, :, None, :] - ci[None] * b[:, :, :, None, :]
    w_im = -(cr[None] * b[:, :, :, None, :] + ci[None] * a[:, :, :, None, :])
    cc_r = jnp.einsum("itgcp,gh->tgpihc", w_re, eye).reshape(S5_KT, gpt * S5_STATE, L * gpt * S5_GROUP)
    cc_i = jnp.einsum("itgcp,gh->tgpihc", w_im, eye).reshape(S5_KT, gpt * S5_STATE, L * gpt * S5_GROUP)
    ccs = jnp.concatenate([cc_r, cc_i], axis=1)
    lam_l = jnp.stack([pr[L].reshape(S5_KT, -1), pi[L].reshape(S5_KT, -1)], axis=1)
    return mt.astype(BF16), bbs.astype(BF16), ccs.astype(BF16), lam_l


def _s5_core_kernel(u_ref, mt_ref, bbs_ref, ccs_ref, lam_ref, y_ref, hs_ref, hp_ref, pw_ref, st_ref, *, tcr):
    ns = S5_KT_STATES
    pad = SUBLANES

    @pl.when(pl.program_id(1) == 0)
    def _():
        hp_ref[...] = jnp.zeros(hp_ref.shape, F32)
        hs_ref[0:pad, :] = jnp.zeros((pad, 2 * ns), F32)
        l1r = jnp.broadcast_to(lam_ref[0:1, :], (SUBLANES, ns))
        l1i = jnp.broadcast_to(lam_ref[1:2, :], (SUBLANES, ns))
        l2r, l2i = _cmul(l1r, l1i, l1r, l1i)
        l4r, l4i = _cmul(l2r, l2i, l2r, l2i)
        l8r, l8i = _cmul(l4r, l4i, l4r, l4i)
        ridx = lax.broadcasted_iota(jnp.int32, l1r.shape, 0)
        n = ridx + 1
        pr = jnp.ones_like(l1r)
        pi = jnp.zeros_like(l1r)
        for bit, (lr, li) in ((1, (l1r, l1i)), (2, (l2r, l2i)), (4, (l4r, l4i)), (8, (l8r, l8i))):
            nr, ni = _cmul(pr, pi, lr, li)
            take = (n & bit) != 0
            pr = jnp.where(take, nr, pr)
            pi = jnp.where(take, ni, pi)
        pw_ref[0] = pr
        pw_ref[1] = pi
        for s, (lr, li) in enumerate(((l1r, l1i), (l2r, l2i), (l4r, l4i))):
            keep = ridx >= (1 << s)
            st_ref[2 * s] = jnp.where(keep, lr, 0.0)
            st_ref[2 * s + 1] = jnp.where(keep, li, 0.0)

    uc = u_ref[...]
    hs_ref[pad:pad + tcr, :] = _dot(uc, bbs_ref[...])
    w = S5_SCAN_W
    for cc in range(ns // w):
        re0 = cc * w
        im0 = ns + cc * w

        def body(r, carry, re0=re0, im0=im0):
            hpr, hpi = carry
            rows = pl.ds(pl.multiple_of(pad + r * SUBLANES, SUBLANES), SUBLANES)
            xr = hs_ref[rows, re0:re0 + w]
            xi = hs_ref[rows, im0:im0 + w]
            for s in range(3):
                sr = pltpu.roll(xr, 1 << s, 0)
                si = pltpu.roll(xi, 1 << s, 0)
                ar = st_ref[2 * s, :, re0:re0 + w]
                ai = st_ref[2 * s + 1, :, re0:re0 + w]
                xr, xi = xr + ar * sr - ai * si, xi + ar * si + ai * sr
            pr = pw_ref[0, :, re0:re0 + w]
            pi = pw_ref[1, :, re0:re0 + w]
            xr, xi = xr + pr * hpr - pi * hpi, xi + pr * hpi + pi * hpr
            hs_ref[rows, re0:re0 + w] = xr
            hs_ref[rows, im0:im0 + w] = xi
            return (jnp.broadcast_to(xr[SUBLANES - 1:SUBLANES, :], (SUBLANES, w)),
                    jnp.broadcast_to(xi[SUBLANES - 1:SUBLANES, :], (SUBLANES, w)))

        hpr, hpi = lax.fori_loop(0, tcr // SUBLANES, body, (hp_ref[:, re0:re0 + w], hp_ref[:, im0:im0 + w]))
        hp_ref[:, re0:re0 + w] = hpr
        hp_ref[:, im0:im0 + w] = hpi

    h_prev = hs_ref[pad - 1:pad - 1 + tcr, :]
    y_in = _dot(h_prev.astype(BF16), ccs_ref[...])
    cw = S5_KT_CH
    for i in range(S5_L):
        k_hi = (i + 1) * cw
        y_ref[:, i * cw:k_hi] = y_in[:, i * cw:k_hi] + _dot(uc[:, :k_hi], mt_ref[:k_hi, i * cw:k_hi])
    hs_ref[pad - 1:pad, :] = hs_ref[pad + tcr - 1:pad + tcr, :]


def _s5_core(u4, mt, bbs, ccs, lam_l, tcr):
    nkt, nrow, width = u4.shape
    tcr = min(tcr, nrow)
    ns = S5_KT_STATES
    once = pl.Buffered(1)
    return pl.pallas_call(
        functools.partial(_s5_core_kernel, tcr=tcr),
        grid=(nkt, nrow // tcr),
        in_specs=[
            pl.BlockSpec((None, tcr, width), lambda k, i: (k, i, 0)),
            pl.BlockSpec((None, width, width), lambda k, i: (k, 0, 0), pipeline_mode=once),
            pl.BlockSpec((None, width, 2 * ns), lambda k, i: (k, 0, 0), pipeline_mode=once),
            pl.BlockSpec((None, 2 * ns, width), lambda k, i: (k, 0, 0), pipeline_mode=once),
            pl.BlockSpec((None, 2, ns), lambda k, i: (k, 0, 0)),
        ],
        out_specs=pl.BlockSpec((None, tcr, width), lambda k, i: (k, i, 0)),
        out_shape=jax.ShapeDtypeStruct((nkt, nrow, width), F32),
        scratch_shapes=[pltpu.VMEM((tcr + SUBLANES, 2 * ns), F32), pltpu.VMEM((SUBLANES, 2 * ns), F32),
                        pltpu.VMEM((2, SUBLANES, ns), F32), pltpu.VMEM((6, SUBLANES, ns), F32)],
        compiler_params=_params(("arbitrary", "arbitrary")),
        name="s5_core",
    )(u4, mt, bbs, ccs, lam_l)


def _s5_glu_kernel(y_ref, u_ref, d_ref, wglu_ref, bglu_ref, o_ref):
    y = jnp.concatenate([y_ref[k] for k in range(S5_KT)], axis=1) + d_ref[...] * u_ref[...].astype(F32)
    g = 0.5 * y * (1.0 + jnp.tanh(math.sqrt(2.0 / math.pi) * (y + 0.044715 * (y * y * y))))
    gl = _dot(g.astype(BF16), wglu_ref[...]) + bglu_ref[...]
    o_ref[...] = (g / (1.0 + jnp.exp(-gl))).astype(o_ref.dtype)


def _s5_glu(y4, h1, d_skip, w_glu, b_glu, tm):
    t = h1.shape[0]
    tm = min(tm, t)
    full = lambda shape: pl.BlockSpec(shape, lambda i: (0,) * len(shape))
    return pl.pallas_call(
        _s5_glu_kernel,
        grid=(t // tm,),
        in_specs=[pl.BlockSpec((S5_KT, tm, S5_KT_CH), lambda i: (0, i, 0)),
                  pl.BlockSpec((tm, S5_CHANNELS), lambda i: (i, 3)),
                  full((1, S5_CHANNELS)), full((S5_CHANNELS, S5_CHANNELS)), full((1, S5_CHANNELS))],
        out_specs=pl.BlockSpec((tm, S5_CHANNELS), lambda i: (i, 0)),
        out_shape=jax.ShapeDtypeStruct((t, S5_CHANNELS), BF16),
        compiler_params=_params(("parallel",)),
        name="s5_glu",
    )(y4, h1, d_skip, w_glu, b_glu)


def _route(probs, sel):
    rows = [sel[e:e + 1, :] for e in range(N_EXPERTS)]
    prow = [probs[e:e + 1, :] for e in range(N_EXPERTS)]
    best_score = None
    for g in range(N_GROUPS):
        a = rows[g * EXPERTS_PER_GROUP:(g + 1) * EXPERTS_PER_GROUP]
        score = None
        for x in range(EXPERTS_PER_GROUP):
            for y in range(x + 1, EXPERTS_PER_GROUP):
                pair = a[x] + a[y]
                score = pair if score is None else jnp.maximum(score, pair)
        if best_score is None:
            best_score, gbest = score, jnp.zeros(score.shape, jnp.int32)
            cs = list(a)
            cp = prow[0:EXPERTS_PER_GROUP]
        else:
            better = score > best_score
            best_score = jnp.where(better, score, best_score)
            gbest = jnp.where(better, g, gbest)
            cs = [jnp.where(better, a[x], cs[x]) for x in range(EXPERTS_PER_GROUP)]
            cp = [jnp.where(better, prow[g * EXPERTS_PER_GROUP + x], cp[x]) for x in range(EXPERTS_PER_GROUP)]

    def first_argmax(vals):
        bv, bi = vals[0], jnp.zeros(vals[0].shape, jnp.int32)
        for x in range(1, len(vals)):
            better = vals[x] > bv
            bv = jnp.where(better, vals[x], bv)
            bi = jnp.where(better, x, bi)
        return bi

    i1 = first_argmax(cs)
    i2 = first_argmax([jnp.where(i1 == x, -jnp.inf, cs[x]) for x in range(EXPERTS_PER_GROUP)])

    def pick(idx):
        out = cp[0]
        for x in range(1, EXPERTS_PER_GROUP):
            out = jnp.where(idx == x, cp[x], out)
        return out

    p1, p2 = pick(i1), pick(i2)
    tot = p1 + p2
    return gbest * EXPERTS_PER_GROUP + i1, gbest * EXPERTS_PER_GROUP + i2, p1 / tot, p2 / tot


def _ln_router_tail(v, g_ref, b_ref, rwh_ref, rwl_ref, rb_ref):
    xn = _layer_norm(v, g_ref[...], b_ref[...])
    xh = xn.astype(BF16)
    xl = (xn - xh.astype(F32)).astype(BF16)
    logits = _nt_dot(rwh_ref[...], xh) + (_nt_dot(rwl_ref[...], xh) + _nt_dot(rwh_ref[...], xl))
    mx = jnp.max(logits, axis=0, keepdims=True)
    ex = jnp.exp(logits - mx)
    probs = ex / jnp.sum(ex, axis=0, keepdims=True)
    e1, e2, w1, w2 = _route(probs, probs + rb_ref[...])
    n = e1.shape[1]
    ei = jnp.concatenate([e1, e2, jnp.zeros((SUBLANES - 2, n), jnp.int32)], axis=0)
    ew = jnp.concatenate([w1, w2, jnp.zeros((SUBLANES - 2, n), F32)], axis=0)
    return xn, ei, ew


def _outproj_kernel(ma_ref, mb_ref, wa_ref, wb_ref, xr_ref, g_ref, b_ref, rwh_ref, rwl_ref, rb_ref,
                    x_ref, ei_ref, ew_ref, *, sub):
    for s in range(x_ref.shape[0] // sub):
        rows = slice(s * sub, (s + 1) * sub)
        v = (DEEPNORM_ALPHA * xr_ref[rows, :] + _dot(ma_ref[rows, :], wa_ref[...])
             + _dot(mb_ref[rows, :], wb_ref[...]))
        xn, ei, ew = _ln_router_tail(v, g_ref, b_ref, rwh_ref, rwl_ref, rb_ref)
        x_ref[rows, :] = xn
        ei_ref[:, rows] = ei
        ew_ref[:, rows] = ew


def _outproj_ln_router(mix_a, mix_b, w_out, x_res, ln_g, ln_b, rwt, rb, tm, sub):
    t = x_res.shape[0]
    rwt_hi = rwt.astype(BF16)
    rwt_lo = (rwt - rwt_hi.astype(F32)).astype(BF16)
    tm = min(tm, t)
    sub = min(sub, tm)
    half = w_out.shape[0] // 2
    row = lambda w: pl.BlockSpec((tm, w), lambda i: (i, 0))
    full = lambda shape: pl.BlockSpec(shape, lambda i: (0,) * len(shape))
    return pl.pallas_call(
        functools.partial(_outproj_kernel, sub=sub),
        grid=(t // tm,),
        in_specs=[row(half), row(half),
                  pl.BlockSpec((half, D_MODEL), lambda i: (0, 0)), pl.BlockSpec((half, D_MODEL), lambda i: (1, 0)),
                  row(D_MODEL), full((1, D_MODEL)), full((1, D_MODEL)),
                  full((N_EXPERTS, D_MODEL)), full((N_EXPERTS, D_MODEL)), full((N_EXPERTS, 1))],
        out_specs=[row(D_MODEL),
                   pl.BlockSpec((SUBLANES, tm), lambda i: (0, i)), pl.BlockSpec((SUBLANES, tm), lambda i: (0, i))],
        out_shape=[jax.ShapeDtypeStruct((t, D_MODEL), F32),
                   jax.ShapeDtypeStruct((SUBLANES, t), jnp.int32), jax.ShapeDtypeStruct((SUBLANES, t), F32)],
        compiler_params=_params(("parallel",)),
        name="outproj_ln_router",
    )(mix_a, mix_b, w_out, w_out, x_res, ln_g, ln_b, rwt_hi, rwt_lo, rb)


def _rank_kernel(e_ref, rank_ref, cnt_ref, base_ref, *, tn):
    @pl.when(pl.program_id(0) == 0)
    def _():
        base_ref[...] = jnp.zeros(base_ref.shape, F32)

    e = e_ref[...]
    rows = lax.broadcasted_iota(jnp.int32, (N_EXPERTS, tn), 0)
    oh1 = rows == e[0:1, :]
    oh2 = rows == e[1:2, :]
    oh = jnp.where(jnp.logical_or(oh1, oh2), 1.0, 0.0)
    r = lax.broadcasted_iota(jnp.int32, (tn, tn), 0)
    c = lax.broadcasted_iota(jnp.int32, (tn, tn), 1)
    earlier = jnp.where(r < c, 1.0, 0.0).astype(BF16)
    base = base_ref[...]
    before = _dot(oh.astype(BF16), earlier) + jnp.tile(base, (1, tn // LANES))
    r1 = jnp.sum(jnp.where(oh1, before, 0.0), axis=0, keepdims=True)
    r2 = jnp.sum(jnp.where(oh2, before, 0.0), axis=0, keepdims=True)
    rank_ref[...] = jnp.concatenate([r1, r2, jnp.zeros((SUBLANES - 2, tn), F32)], axis=0).astype(jnp.int32)
    base = base + jnp.sum(oh, axis=1, keepdims=True)
    base_ref[...] = base
    cnt_ref[...] = base.astype(jnp.int32)


def _rank(eidx, tn):
    t = eidx.shape[1]
    tn = min(tn, t)
    return pl.pallas_call(
        functools.partial(_rank_kernel, tn=tn),
        grid=(t // tn,),
        in_specs=[pl.BlockSpec((SUBLANES, tn), lambda i: (0, i))],
        out_specs=[pl.BlockSpec((SUBLANES, tn), lambda i: (0, i)),
                   pl.BlockSpec((N_EXPERTS, LANES), lambda i: (0, 0))],
        out_shape=[jax.ShapeDtypeStruct((SUBLANES, t), jnp.int32),
                   jax.ShapeDtypeStruct((N_EXPERTS, LANES), jnp.int32)],
        scratch_shapes=[pltpu.VMEM((N_EXPERTS, LANES), F32)],
        compiler_params=_params(("arbitrary",)),
        name="moe_rank",
    )(eidx)


def _moe_kernel(te_ref, nu_ref, xs_ref, wg_ref, wu_ref, wd_ref, y_ref, wgu_s, wd_s):
    tile = pl.program_id(0)
    used = tile < nu_ref[0]
    fresh = jnp.logical_or(tile == 0, te_ref[tile] != te_ref[jnp.maximum(tile - 1, 0)])

    @pl.when(jnp.logical_and(used, fresh))
    def _():
        wgu_s[:, :D_EXPERT] = wg_ref[...].astype(BF16)
        wgu_s[:, D_EXPERT:] = wu_ref[...].astype(BF16)
        wd_s[...] = wd_ref[...].astype(BF16)

    @pl.when(used)
    def _():
        for s in range(xs_ref.shape[0] // MOE_SUB):
            rows = slice(s * MOE_SUB, (s + 1) * MOE_SUB)
            hgu = _dot(xs_ref[rows, :].astype(BF16), wgu_s[...])
            hg = hgu[:, :D_EXPERT]
            hu = hgu[:, D_EXPERT:]
            h = (hg / (1.0 + jnp.exp(-hg))) * hu
            y_ref[rows, :] = _dot(h.astype(BF16), wd_s[...])

    @pl.when(jnp.logical_not(used))
    def _():
        y_ref[...] = jnp.zeros(y_ref.shape, F32)


def _moe(tile_expert, n_used, xs, w_gate, w_up, w_down, tm):
    t_pad = xs.shape[0]
    grid_spec = pltpu.PrefetchScalarGridSpec(
        num_scalar_prefetch=2,
        grid=(t_pad // tm,),
        in_specs=[
            pl.BlockSpec((tm, D_MODEL), lambda i, te, nu: (i, 0)),
            pl.BlockSpec((None, D_MODEL, D_EXPERT), lambda i, te, nu: (te[i], 0, 0)),
            pl.BlockSpec((None, D_MODEL, D_EXPERT), lambda i, te, nu: (te[i], 0, 0)),
            pl.BlockSpec((None, D_EXPERT, D_MODEL), lambda i, te, nu: (te[i], 0, 0)),
        ],
        out_specs=pl.BlockSpec((tm, D_MODEL), lambda i, te, nu: (i, 0)),
        scratch_shapes=[pltpu.VMEM((D_MODEL, 2 * D_EXPERT), BF16), pltpu.VMEM((D_EXPERT, D_MODEL), BF16)],
    )
    return pl.pallas_call(
        _moe_kernel,
        grid_spec=grid_spec,
        out_shape=jax.ShapeDtypeStruct((t_pad, D_MODEL), F32),
        compiler_params=_params(("arbitrary",)),
        name="moe",
    )(tile_expert, n_used, xs, w_gate, w_up, w_down)


def _moe_ln_kernel(x_ref, y1_ref, y2_ref, w_ref, g_ref, b_ref, o_ref):
    w = w_ref[...]
    v = DEEPNORM_ALPHA * x_ref[...] + (w[:, 0:1] * y1_ref[...] + w[:, 1:2] * y2_ref[...])
    o_ref[...] = _layer_norm(v, g_ref[...], b_ref[...])


def _moe_ln(x, y1, y2, w_cols, ln_g, ln_b, tm):
    t = x.shape[0]
    tm = min(tm, t)
    row = pl.BlockSpec((tm, D_MODEL), lambda i: (i, 0))
    vec = pl.BlockSpec((1, D_MODEL), lambda i: (0, 0))
    return pl.pallas_call(
        _moe_ln_kernel,
        grid=(t // tm,),
        in_specs=[row, row, row, pl.BlockSpec((tm, SUBLANES), lambda i: (i, 0)), vec, vec],
        out_specs=row,
        out_shape=jax.ShapeDtypeStruct((t, D_MODEL), F32),
        compiler_params=_params(("parallel",)),
        name="moe_ln",
    )(x, y1, y2, w_cols, ln_g, ln_b)


def _moe_block(x, eidx, ew, w_gate, w_up, w_down, ln_g, ln_b, tm):
    t = x.shape[0]
    tm = min(tm, t)
    rank, cnt = _rank(eidx, 512)
    counts = cnt[:, 0]
    padded = ((counts + tm - 1) // tm) * tm
    ends = jnp.cumsum(padded)
    starts = ends - padded
    e2 = eidx[:2]
    start_of = jnp.zeros_like(e2)
    for k in range(N_EXPERTS):
        start_of = jnp.where(e2 == k, starts[k], start_of)
    dest = start_of + rank[:2]
    t_pad = 2 * t + N_EXPERTS * tm
    tok = jnp.tile(jnp.arange(t, dtype=jnp.int32), 2)
    src_tok = (jnp.arange(t_pad, dtype=jnp.int32) % t).at[dest.reshape(2 * t)].set(
        tok, mode="promise_in_bounds", unique_indices=True)
    tile_start = jnp.arange(t_pad // tm, dtype=jnp.int32) * tm
    tile_expert = jnp.minimum(jnp.sum((ends[None, :] <= tile_start[:, None]).astype(jnp.int32), axis=1),
                              N_EXPERTS - 1)
    n_used = (ends[-1] // tm).astype(jnp.int32).reshape(1)
    xs = x.at[src_tok].get(mode="promise_in_bounds")
    ys = _moe(tile_expert, n_used, xs, w_gate, w_up, w_down, tm)
    y1 = ys.at[dest[0]].get(mode="promise_in_bounds")
    y2 = ys.at[dest[1]].get(mode="promise_in_bounds")
    return _moe_ln(x, y1, y2, ew.T, ln_g, ln_b, tm)


def kernel(x, router_w, router_b, l0_w_in, l0_diff_lam_q1, l0_diff_lam_k1, l0_diff_lam_q2, l0_diff_lam_k2, l0_diff_subln_g, l0_gla_w_gate, l0_gla_b_gate, l0_gla_norm_g, l0_w_out, l0_ln1_g, l0_ln1_b, l0_moe_w_gate, l0_moe_w_up, l0_moe_w_down, l0_ln2_g, l0_ln2_b, l1_w_in, l1_s5_a_re, l1_s5_a_im, l1_s5_log_step, l1_s5_b_re, l1_s5_b_im, l1_s5_c_re, l1_s5_c_im, l1_s5_d, l1_s5_w_glu, l1_s5_b_glu, l1_w_out, l1_ln1_g, l1_ln1_b, l1_moe_w_gate, l1_moe_w_up, l1_moe_w_down, l1_ln2_g, l1_ln2_b):
    bsz, t, d = x.shape
    assert bsz == 1 and d == D_MODEL
    x0 = x.reshape(t, d)
    vec = lambda p: p.reshape(1, -1).astype(F32)
    rwt = router_w.T.astype(F32)
    rb = router_b.reshape(N_EXPERTS, 1).astype(F32)

    lambda_init = 0.8 - 0.6 * math.exp(-0.3 * 0)
    c = [0, 1024, 2048, 3072, 3584, 4096, 5120, 5136, 6160]
    q_scale = math.log2(math.e) * DIFF_HEAD_DIM ** -0.5
    w0 = jnp.concatenate([l0_w_in[:, :c[1]] * q_scale, l0_w_in[:, c[1]:c[6]], l0_w_in[:, c[7]:c[8]],
                          l0_w_in[:, c[6]:c[7]], jnp.zeros((d, LANES - GLA_GATE_RANK), l0_w_in.dtype)],
                         axis=1).astype(BF16)
    h0 = _matmul(x0, w0, 1024, 896, BF16)
    lam_vec = jnp.stack([l0_diff_lam_q1, l0_diff_lam_k1, l0_diff_lam_q2, l0_diff_lam_k2]).astype(F32)
    o_a = _diff_attention(h0, lam_vec, vec(l0_diff_subln_g), lambda_init, 1024)
    wg_pad = jnp.concatenate([l0_gla_w_gate, jnp.zeros((LANES - GLA_GATE_RANK, GLA_HEADS * GLA_DK), F32)],
                             axis=0).astype(BF16)
    o_b = _gla(h0, wg_pad, vec(l0_gla_b_gate), vec(l0_gla_norm_g), 256)
    x1, eidx, ew = _outproj_ln_router(o_a, o_b, l0_w_out.astype(BF16), x0, vec(l0_ln1_g), vec(l0_ln1_b),
                                      rwt, rb, 512, 256)
    x2 = _moe_block(x1, eidx, ew, l0_moe_w_gate, l0_moe_w_up, l0_moe_w_down, vec(l0_ln2_g), vec(l0_ln2_b), 256)

    h1 = _matmul(x2, l1_w_in.astype(BF16), 1024, 1024, BF16)
    o_c = _stick_breaking(h1, 256)
    lbr, lbi, bbr, bbi = _s5_prep(l1_s5_a_re, l1_s5_a_im, l1_s5_log_step, l1_s5_b_re, l1_s5_b_im)
    bb_re = jnp.transpose(bbr.reshape(S5_GROUPS, S5_GROUP, S5_STATE), (0, 2, 1))
    bb_im = jnp.transpose(bbi.reshape(S5_GROUPS, S5_GROUP, S5_STATE), (0, 2, 1))
    mt, bbs, ccs, lam_l = _s5_chunk_weights(lbr[::S5_GROUP], lbi[::S5_GROUP], bb_re, bb_im,
                                            l1_s5_c_re.astype(F32), l1_s5_c_im.astype(F32))
    u4 = jnp.transpose(h1[:, 3 * S5_CHANNELS:].reshape(t, S5_KT, S5_KT_CH), (1, 0, 2))
    u4 = u4.reshape(S5_KT, t // S5_L, S5_L * S5_KT_CH)
    y4 = _s5_core(u4, mt, bbs, ccs, lam_l, 256).reshape(S5_KT, t, S5_KT_CH)
    o_d = _s5_glu(y4, h1, vec(l1_s5_d), l1_s5_w_glu.astype(BF16), vec(l1_s5_b_glu), 512)
    x3, eidx1, ew1 = _outproj_ln_router(o_c, o_d, l1_w_out.astype(BF16), x2, vec(l1_ln1_g), vec(l1_ln1_b),
                                        rwt, rb, 512, 256)
    x4 = _moe_block(x3, eidx1, ew1, l1_moe_w_gate, l1_moe_w_up, l1_moe_w_down, vec(l1_ln2_g), vec(l1_ln2_b), 256)
    return x4.reshape(bsz, t, d)
```

```python
import functools
import math

import jax
import jax.numpy as jnp
from jax import lax
from jax.experimental import pallas as pl
from jax.experimental.pallas import tpu as pltpu

F32 = jnp.float32
BF16 = jnp.bfloat16

D_MODEL = 2048
DEPTH = 2
DIFF_HEADS = 8
DIFF_HEAD_DIM = 64
DIFF_V_DIM = 128
GLA_HEADS = 4
GLA_DK = 128
GLA_DV = 256
GLA_GATE_RANK = 16
GLA_GATE_TEMP = 16.0
GLA_CHUNK = 64
SB_HEADS = 8
SB_HEAD_DIM = 128
S5_CHANNELS = 1024
S5_GROUP = 16
S5_GROUPS = 64
S5_STATE = 64
N_EXPERTS = 16
N_GROUPS = 4
EXPERTS_PER_GROUP = 4
D_EXPERT = 640
DEEPNORM_ALPHA = (2.0 * DEPTH) ** 0.25
LN_EPS = 1e-5

LANES = 128
SUBLANES = 8
VMEM_LIMIT = 56 * 1024 * 1024
SB_LOG_FLOOR = 120.0
SB_HEADS_PER_STEP = 4
MOE_SUB = 256

S5_KT = 4
S5_KT_CH = S5_CHANNELS // S5_KT
S5_KT_STATES = (S5_GROUPS // S5_KT) * S5_STATE
S5_SCAN_W = 512
S5_L = 8


def _params(sem, flags=None):
    return pltpu.CompilerParams(dimension_semantics=sem, vmem_limit_bytes=VMEM_LIMIT, flags=flags)


def _nt_dot(a, b):
    return lax.dot_general(a, b, (((1,), (1,)), ((), ())), preferred_element_type=F32)


def _dot(a, b):
    return jnp.dot(a, b, preferred_element_type=F32)


def _split_dot(x, m):
    hi = x.astype(BF16)
    lo = (x - hi.astype(F32)).astype(BF16)
    return _dot(hi, m) + _dot(lo, m)


def _split_dot_left(m, x):
    hi = x.astype(BF16)
    lo = (x - hi.astype(F32)).astype(BF16)
    return _dot(m, hi) + _dot(m, lo)


def _layer_norm(v, g, b):
    mu = jnp.mean(v, axis=-1, keepdims=True)
    d = v - mu
    var = jnp.mean(d * d, axis=-1, keepdims=True)
    return d * lax.rsqrt(var + LN_EPS) * g + b


def _mm_kernel(a_ref, b_ref, o_ref, a_bf):
    @pl.when(pl.program_id(1) == 0)
    def _():
        a_bf[...] = a_ref[...].astype(BF16)

    o_ref[...] = _dot(a_bf[...], b_ref[...]).astype(o_ref.dtype)


def _matmul(a, b, tm, tn, out_dtype):
    m, k = a.shape
    n = b.shape[1]
    tm = min(tm, m)
    return pl.pallas_call(
        _mm_kernel,
        grid=(m // tm, n // tn),
        in_specs=[pl.BlockSpec((tm, k), lambda i, j: (i, 0)),
                  pl.BlockSpec((k, tn), lambda i, j: (0, j))],
        out_specs=pl.BlockSpec((tm, tn), lambda i, j: (i, j)),
        out_shape=jax.ShapeDtypeStruct((m, n), out_dtype),
        scratch_shapes=[pltpu.VMEM((tm, k), BF16)],
        compiler_params=_params(("parallel", "arbitrary")),
        name="in_proj",
    )(a, b)


def _diff_attn_kernel(lam_ref, g_ref, q_ref, k_ref, v_ref, o_ref, vx_ref, m_ref, acc_ref, sa_ref, sb_ref, *, tq,
                      lambda_init):
    i = pl.program_id(1)
    dv = DIFF_V_DIM

    @pl.when(i == 0)
    def _():
        vx_ref[:, :dv] = v_ref[...]
        vx_ref[:, dv:] = jnp.ones((vx_ref.shape[0], dv), BF16)

    m_ref[...] = jnp.full(m_ref.shape, -jnp.inf, F32)
    acc_ref[...] = jnp.zeros(acc_ref.shape, F32)
    q = q_ref[...]
    lane = lax.broadcasted_iota(jnp.int32, q.shape, 1)
    zero = jnp.zeros_like(q)
    halves = (jnp.where(lane < DIFF_HEAD_DIM, q, zero), jnp.where(lane >= DIFF_HEAD_DIM, q, zero))

    def scores(j, mi):
        off = pl.multiple_of(j * tq, tq)
        return _nt_dot(halves[mi], k_ref[pl.ds(off, tq), :])

    def softmax_pv(j, get_s, masked):
        off = pl.multiple_of(j * tq, tq)
        vx = vx_ref[pl.ds(off, tq), :]
        if masked:
            row = lax.broadcasted_iota(jnp.int32, (tq, tq), 0)
            col = lax.broadcasted_iota(jnp.int32, (tq, tq), 1)
            causal = col <= row
        for mi in range(2):
            s = get_s(mi)
            if masked:
                s = jnp.where(causal, s, -jnp.inf)
            m_prev = m_ref[mi]
            m_new = jnp.maximum(m_prev, jnp.max(s, axis=1, keepdims=True))
            p = jnp.exp2(s - jnp.tile(m_new, (1, tq // LANES)))
            alpha = jnp.exp2(m_prev - m_new)
            acc_ref[mi] = jnp.tile(alpha, (1, 2)) * acc_ref[mi] + _dot(p.astype(BF16), vx)
            m_ref[mi] = m_new

    def scores_to(j, s_ref):
        for mi in range(2):
            s_ref[mi] = scores(j, mi)

    scores_to(0, sa_ref)

    def pair(t, carry):
        j = 2 * t
        scores_to(j + 1, sb_ref)
        softmax_pv(j, lambda mi: sa_ref[mi], False)
        scores_to(j + 2, sa_ref)
        softmax_pv(j + 1, lambda mi: sb_ref[mi], False)
        return carry

    lax.fori_loop(0, i // 2, pair, 0)

    @pl.when(i % 2 == 1)
    def _():
        softmax_pv(i - 1, lambda mi: sa_ref[mi], False)
        scores_to(i, sa_ref)

    softmax_pv(i, lambda mi: sa_ref[mi], True)

    lv = lam_ref[...]
    lam = (jnp.exp(jnp.sum(lv[0:1] * lv[1:2], axis=1, keepdims=True))
           - jnp.exp(jnp.sum(lv[2:3] * lv[3:4], axis=1, keepdims=True)) + lambda_init)
    a0 = acc_ref[0]
    a1 = acc_ref[1]
    o = a0[:, :dv] / a0[:, dv:] - lam * (a1[:, :dv] / a1[:, dv:])
    ms = jnp.mean(o * o, axis=-1, keepdims=True)
    o = o * lax.rsqrt(ms + LN_EPS) * g_ref[...] * (1.0 - lambda_init)
    o_ref[...] = o.astype(o_ref.dtype)


def _diff_attention(h0, lam_vec, subln_g, lambda_init, tq):
    t = h0.shape[0]
    tq = min(tq, t)
    nq = t // tq
    kern = functools.partial(_diff_attn_kernel, tq=tq, lambda_init=lambda_init)
    return pl.pallas_call(
        kern,
        grid=(DIFF_HEADS, nq),
        in_specs=[
            pl.BlockSpec((4, DIFF_HEAD_DIM), lambda h, i: (0, 0)),
            pl.BlockSpec((1, DIFF_V_DIM), lambda h, i: (0, 0)),
            pl.BlockSpec((tq, LANES), lambda h, i: (i, h)),
            pl.BlockSpec((t, LANES), lambda h, i: (0, DIFF_HEADS + h)),
            pl.BlockSpec((t, LANES), lambda h, i: (0, 2 * DIFF_HEADS + h)),
        ],
        out_specs=pl.BlockSpec((tq, LANES), lambda h, i: (i, h)),
        out_shape=jax.ShapeDtypeStruct((t, DIFF_HEADS * DIFF_V_DIM), BF16),
        scratch_shapes=[pltpu.VMEM((t, 2 * DIFF_V_DIM), BF16), pltpu.VMEM((2, tq, LANES), F32),
                        pltpu.VMEM((2, tq, 2 * DIFF_V_DIM), F32),
                        pltpu.VMEM((2, tq, tq), F32), pltpu.VMEM((2, tq, tq), F32)],
        compiler_params=_params(("arbitrary", "arbitrary")),
        name="diff_attn",
    )(lam_vec, subln_g, h0, h0, h0)


def _gla_kernel(q_ref, k_ref, v_ref, r_ref, gb_ref, wg_ref, bg_ref, ng_ref, o_ref, st_ref, *, tg):
    @pl.when(pl.program_id(0) == 0)
    def _():
        st_ref[...] = jnp.zeros(st_ref.shape, F32)

    c = GLA_CHUNK
    gate = _dot(gb_ref[...], wg_ref[...]) + bg_ref[...]
    log_a = -(jnp.maximum(-gate, 0.0) + jnp.log(1.0 + jnp.exp(-jnp.abs(gate)))) / GLA_GATE_TEMP
    row = lax.broadcasted_iota(jnp.int32, (c, c), 0)
    col = lax.broadcasted_iota(jnp.int32, (c, c), 1)
    tril = row >= col
    tri = jnp.where(tril, 1.0, 0.0).astype(BF16)
    for h in range(GLA_HEADS):
        ksl = slice(h * GLA_DK, (h + 1) * GLA_DK)
        vsl = slice(h * GLA_DV, (h + 1) * GLA_DV)
        for ci in range(tg // c):
            rs = slice(ci * c, (ci + 1) * c)
            la = log_a[rs, ksl]
            b = _split_dot_left(tri, la)
            b_last = b[c - 1:c, :]
            qf = q_ref[rs, ksl].astype(F32) * (GLA_DK ** -0.5)
            kf = k_ref[rs, ksl].astype(F32)
            v = v_ref[rs, vsl]
            q_dec = (qf * jnp.exp(b)).astype(BF16)
            k_dec = (kf * jnp.exp(-b)).astype(BF16)
            k_end = (kf * jnp.exp(b_last - b)).astype(BF16)
            scores = jnp.where(tril, _nt_dot(q_dec, k_dec), 0.0)
            st = st_ref[h]
            o = _dot(scores.astype(BF16), v) + _nt_dot(q_dec, st.astype(BF16))
            kv_t = lax.dot_general(v, k_end, (((0,), (0,)), ((), ())), preferred_element_type=F32)
            st_ref[h] = jnp.exp(b_last) * st + kv_t
            ms = jnp.mean(o * o, axis=-1, keepdims=True)
            o = o * lax.rsqrt(ms + LN_EPS) * ng_ref[...]
            r = r_ref[rs, vsl].astype(F32)
            o_ref[rs, vsl] = (o * (r / (1.0 + jnp.exp(-r)))).astype(o_ref.dtype)


def _gla(h0, wg_pad, b_gate, norm_g, tg):
    t = h0.shape[0]
    tg = min(tg, t)
    qk = GLA_HEADS * GLA_DK
    vw = GLA_HEADS * GLA_DV
    return pl.pallas_call(
        functools.partial(_gla_kernel, tg=tg),
        grid=(t // tg,),
        in_specs=[
            pl.BlockSpec((tg, qk), lambda i: (i, 3072 // qk)),
            pl.BlockSpec((tg, qk), lambda i: (i, 3584 // qk)),
            pl.BlockSpec((tg, vw), lambda i: (i, 4096 // vw)),
            pl.BlockSpec((tg, vw), lambda i: (i, 5120 // vw)),
            pl.BlockSpec((tg, LANES), lambda i: (i, 6144 // LANES)),
            pl.BlockSpec((LANES, qk), lambda i: (0, 0)),
            pl.BlockSpec((1, qk), lambda i: (0, 0)),
            pl.BlockSpec((1, GLA_DV), lambda i: (0, 0)),
        ],
        out_specs=pl.BlockSpec((tg, vw), lambda i: (i, 0)),
        out_shape=jax.ShapeDtypeStruct((t, vw), BF16),
        scratch_shapes=[pltpu.VMEM((GLA_HEADS, GLA_DV, GLA_DK), F32)],
        compiler_params=_params(("arbitrary",)),
        name="gla",
    )(h0, h0, h0, h0, h0, wg_pad, b_gate, norm_g)


def _sb_kernel(q_ref, k_ref, v_ref, o_ref, acc_ref, c_ref, *, tq):
    i = pl.program_id(1)
    hd = SB_HEAD_DIM
    acc_ref[...] = jnp.zeros(acc_ref.shape, F32)
    c_ref[...] = jnp.zeros(c_ref.shape, F32)
    row = lax.broadcasted_iota(jnp.int32, (tq, tq), 0)
    col = lax.broadcasted_iota(jnp.int32, (tq, tq), 1)
    strict = col < row
    tri = jnp.where(row > col, 1.0, 0.0).astype(BF16)

    def block(jb, masked):
        off = pl.multiple_of(jb * tq, tq)
        cmax = None
        for hh in range(SB_HEADS_PER_STEP):
            cols = slice(hh * hd, (hh + 1) * hd)
            k = k_ref[pl.ds(off, tq), cols]
            v = v_ref[pl.ds(off, tq), cols]
            z = _nt_dot(q_ref[:, cols], k) * (SB_HEAD_DIM ** -0.5)
            lsn = -(jnp.maximum(z, 0.0) + jnp.log(1.0 + jnp.exp(-jnp.abs(z))))
            lm = jnp.where(strict, lsn, 0.0) if masked else lsn
            excl = _split_dot(lm, tri) + c_ref[hh]
            w = jnp.exp(z + lsn + excl)
            if masked:
                w = jnp.where(strict, w, 0.0)
            acc_ref[:, cols] += _dot(w.astype(BF16), v)
            c_new = c_ref[hh] + jnp.sum(lm, axis=1, keepdims=True)
            c_ref[hh] = c_new
            hmax = jnp.max(c_new)
            cmax = hmax if cmax is None else jnp.maximum(cmax, hmax)
        return cmax

    cmax = block(i, True)

    def cond(carry):
        jb, cm = carry
        return jnp.logical_and(jb >= 0, cm > -SB_LOG_FLOOR)

    def body(carry):
        jb, _ = carry
        return jb - 1, block(jb, False)

    lax.while_loop(cond, body, (i - 1, cmax))
    o_ref[...] = acc_ref[...].astype(o_ref.dtype)


def _stick_breaking(h1, tq):
    t = h1.shape[0]
    tq = min(tq, t)
    ng = SB_HEADS // SB_HEADS_PER_STEP
    wd = SB_HEADS_PER_STEP * SB_HEAD_DIM
    return pl.pallas_call(
        functools.partial(_sb_kernel, tq=tq),
        grid=(ng, t // tq),
        in_specs=[
            pl.BlockSpec((tq, wd), lambda h, i: (i, h)),
            pl.BlockSpec((t, wd), lambda h, i: (0, ng + h), pipeline_mode=pl.Buffered(1)),
            pl.BlockSpec((t, wd), lambda h, i: (0, 2 * ng + h), pipeline_mode=pl.Buffered(1)),
        ],
        out_specs=pl.BlockSpec((tq, wd), lambda h, i: (i, h)),
        out_shape=jax.ShapeDtypeStruct((t, SB_HEADS * SB_HEAD_DIM), BF16),
        scratch_shapes=[pltpu.VMEM((tq, wd), F32), pltpu.VMEM((SB_HEADS_PER_STEP, tq, 1), F32)],
        compiler_params=_params(("parallel", "arbitrary")),
        name="stick_breaking",
    )(h1, h1, h1)


def _s5_prep_kernel(are_ref, aim_ref, ls_ref, bre_ref, bim_ref, lbr_ref, lbi_ref, bbr_ref, bbi_ref):
    step = jnp.exp(ls_ref[...])
    lam_re = are_ref[...]
    lam_im = aim_ref[...]
    mag = jnp.exp(step * lam_re)
    lb_re = mag * jnp.cos(step * lam_im)
    lb_im = mag * jnp.sin(step * lam_im)
    den = lam_re * lam_re + lam_im * lam_im
    n_re = lb_re - 1.0
    f_re = (n_re * lam_re + lb_im * lam_im) / den
    f_im = (lb_im * lam_re - n_re * lam_im) / den
    lbr_ref[...] = lb_re
    lbi_ref[...] = lb_im
    bbr_ref[...] = f_re * bre_ref[...] - f_im * bim_ref[...]
    bbi_ref[...] = f_re * bim_ref[...] + f_im * bre_ref[...]


def _s5_prep(a_re, a_im, log_step, b_re, b_im):
    rows = S5_GROUPS * S5_GROUP
    rep = lambda p: jnp.repeat(p, S5_GROUP, axis=0)
    to_rows = lambda b: jnp.transpose(b, (0, 2, 1)).reshape(rows, S5_STATE)
    spec = pl.BlockSpec((rows, S5_STATE), lambda: (0, 0))
    shp = jax.ShapeDtypeStruct((rows, S5_STATE), F32)
    return pl.pallas_call(
        _s5_prep_kernel,
        in_specs=[spec, spec, pl.BlockSpec((rows, 1), lambda: (0, 0)), spec, spec],
        out_specs=[spec] * 4,
        out_shape=[shp] * 4,
        name="s5_prep",
    )(rep(a_re), rep(a_im), rep(log_step[:, None]), to_rows(b_re), to_rows(b_im))


def _cmul(ar, ai, br, bi):
    return ar * br - ai * bi, ar * bi + ai * br


def _s5_chunk_weights(lb_re, lb_im, bb_re, bb_im, c_re, c_im):
    hp = lax.Precision.HIGHEST
    L = S5_L
    pr = [jnp.ones_like(lb_re)]
    pi = [jnp.zeros_like(lb_re)]
    for _ in range(L):
        nr, ni = _cmul(pr[-1], pi[-1], lb_re, lb_im)
        pr.append(nr)
        pi.append(ni)
    pr = jnp.stack(pr)
    pi = jnp.stack(pi)
    d_re = pr[:L, :, :, None] * bb_re[None] - pi[:L, :, :, None] * bb_im[None]
    d_im = pr[:L, :, :, None] * bb_im[None] + pi[:L, :, :, None] * bb_re[None]
    kk = (jnp.einsum("gcp,kgpd->kgdc", c_re, d_re, precision=hp)
          - jnp.einsum("gcp,kgpd->kgdc", c_im, d_im, precision=hp))
    gpt = S5_GROUPS // S5_KT
    eye = jnp.eye(gpt, dtype=F32)
    tile = lambda a: a.reshape(a.shape[:1] + (S5_KT, gpt) + a.shape[2:])
    lag = jnp.arange(L)
    sel = (lag[None, :, None] - lag[:, None, None] == lag[None, None, :]).astype(F32)
    tb = jnp.einsum("sik,ktgdc->tsgdic", sel, tile(kk), precision=hp)
    mt = (tb[:, :, :, :, :, None, :] * eye[None, None, :, None, None, :, None]).astype(BF16)
    mt = mt.reshape(S5_KT, L * gpt * S5_GROUP, L * gpt * S5_GROUP)

    def rows_sgd(x):
        x = jnp.transpose(x, (1, 0, 2, 4, 3))
        x = (x[:, :, :, :, None, :] * eye[None, None, :, None, :, None]).astype(BF16)
        return x.reshape(S5_KT, L * gpt * S5_GROUP, gpt * S5_STATE)

    bbs = jnp.concatenate([rows_sgd(tile(d_re)[::-1]), rows_sgd(tile(d_im)[::-1])], axis=2)
    a = tile(pr[1:L + 1])
    b = tile(pi[1:L + 1])
    cr = c_re.reshape(S5_KT, gpt, S5_GROUP, S5_STATE)
    ci = c_im.reshape(S5_KT, gpt, S5_GROUP, S5_STATE)
    w_re = cr[None] * a[:, :, :, None, :] - ci[None] * b[:, :, :, None, :]
    w_im = -(cr[None] * b[:, :, :, None, :] + ci[None] * a[:, :, :, None, :])
    def cols_ihc(x):
        x = jnp.transpose(x, (1, 2, 4, 0, 3))
        x = (x[:, :, :, :, None, :] * eye[None, :, None, None, :, None]).astype(BF16)
        return x.reshape(S5_KT, gpt * S5_STATE, L * gpt * S5_GROUP)

    ccs = jnp.concatenate([cols_ihc(w_re), cols_ihc(w_im)], axis=1)
    lam_l = jnp.stack([pr[L].reshape(S5_KT, -1), pi[L].reshape(S5_KT, -1)], axis=1)
    return mt, bbs, ccs, lam_l


def _s5_core_kernel(u_ref, mt_ref, bbs_ref, ccs_ref, lam_ref, y_ref, hs_ref, hp_ref, pw_ref, st_ref, uf_ref, *,
                    tcr):
    ns = S5_KT_STATES
    pad = SUBLANES

    @pl.when(pl.program_id(1) == 0)
    def _():
        hp_ref[...] = jnp.zeros(hp_ref.shape, F32)
        hs_ref[0:pad, :] = jnp.zeros((pad, 2 * ns), F32)
        l1r = jnp.broadcast_to(lam_ref[0:1, :], (SUBLANES, ns))
        l1i = jnp.broadcast_to(lam_ref[1:2, :], (SUBLANES, ns))
        l2r, l2i = _cmul(l1r, l1i, l1r, l1i)
        l4r, l4i = _cmul(l2r, l2i, l2r, l2i)
        l8r, l8i = _cmul(l4r, l4i, l4r, l4i)
        ridx = lax.broadcasted_iota(jnp.int32, l1r.shape, 0)
        n = ridx + 1
        pr = jnp.ones_like(l1r)
        pi = jnp.zeros_like(l1r)
        for bit, (lr, li) in ((1, (l1r, l1i)), (2, (l2r, l2i)), (4, (l4r, l4i)), (8, (l8r, l8i))):
            nr, ni = _cmul(pr, pi, lr, li)
            take = (n & bit) != 0
            pr = jnp.where(take, nr, pr)
            pi = jnp.where(take, ni, pi)
        pw_ref[0] = pr
        pw_ref[1] = pi
        for s, (lr, li) in enumerate(((l1r, l1i), (l2r, l2i), (l4r, l4i))):
            keep = ridx >= (1 << s)
            st_ref[2 * s] = jnp.where(keep, lr, 0.0)
            st_ref[2 * s + 1] = jnp.where(keep, li, 0.0)

    nl = S5_KT_CH // LANES
    for h in range(nl):
        uf_ref[h] = u_ref[:, h * LANES:(h + 1) * LANES].astype(F32)
    uc = jnp.concatenate([uf_ref[h, pl.ds(s, tcr, stride=S5_L), :].astype(BF16)
                          for s in range(S5_L) for h in range(nl)], axis=1)
    hs_ref[pad:pad + tcr, :] = _dot(uc, bbs_ref[...])
    w = S5_SCAN_W
    for cc in range(ns // w):
        re0 = cc * w
        im0 = ns + cc * w

        def body(r, carry, re0=re0, im0=im0):
            hpr, hpi = carry
            rows = pl.ds(pl.multiple_of(pad + r * SUBLANES, SUBLANES), SUBLANES)
            xr = hs_ref[rows, re0:re0 + w]
            xi = hs_ref[rows, im0:im0 + w]
            for s in range(3):
                sr = pltpu.roll(xr, 1 << s, 0)
                si = pltpu.roll(xi, 1 << s, 0)
                ar = st_ref[2 * s, :, re0:re0 + w]
                ai = st_ref[2 * s + 1, :, re0:re0 + w]
                xr, xi = xr + ar * sr - ai * si, xi + ar * si + ai * sr
            pr = pw_ref[0, :, re0:re0 + w]
            pi = pw_ref[1, :, re0:re0 + w]
            xr, xi = xr + pr * hpr - pi * hpi, xi + pr * hpi + pi * hpr
            hs_ref[rows, re0:re0 + w] = xr
            hs_ref[rows, im0:im0 + w] = xi
            return (jnp.broadcast_to(xr[SUBLANES - 1:SUBLANES, :], (SUBLANES, w)),
                    jnp.broadcast_to(xi[SUBLANES - 1:SUBLANES, :], (SUBLANES, w)))

        hpr, hpi = lax.fori_loop(0, tcr // SUBLANES, body, (hp_ref[:, re0:re0 + w], hp_ref[:, im0:im0 + w]))
        hp_ref[:, re0:re0 + w] = hpr
        hp_ref[:, im0:im0 + w] = hpi

    h_prev = hs_ref[pad - 1:pad - 1 + tcr, :]
    y_in = _dot(h_prev.astype(BF16), ccs_ref[...])
    cw = S5_KT_CH
    for i in range(S5_L):
        k_hi = (i + 1) * cw
        y_i = y_in[:, i * cw:k_hi] + _dot(uc[:, :k_hi], mt_ref[:k_hi, i * cw:k_hi])
        for h in range(nl):
            uf_ref[h, pl.ds(i, tcr, stride=S5_L), :] = y_i[:, h * LANES:(h + 1) * LANES]
    for h in range(nl):
        y_ref[:, h * LANES:(h + 1) * LANES] = uf_ref[h]
    hs_ref[pad - 1:pad, :] = hs_ref[pad + tcr - 1:pad + tcr, :]


def _s5_core(h1, mt, bbs, ccs, lam_l, tcr):
    t = h1.shape[0]
    nkt = S5_KT
    width = S5_L * S5_KT_CH
    tcr = min(tcr, t // S5_L)
    tc = tcr * S5_L
    ns = S5_KT_STATES
    u_col0 = 3 * S5_CHANNELS // S5_KT_CH
    once = pl.Buffered(1)
    return pl.pallas_call(
        functools.partial(_s5_core_kernel, tcr=tcr),
        grid=(nkt, t // tc),
        in_specs=[
            pl.BlockSpec((tc, S5_KT_CH), lambda k, i: (i, u_col0 + k)),
            pl.BlockSpec((None, width, width), lambda k, i: (k, 0, 0), pipeline_mode=once),
            pl.BlockSpec((None, width, 2 * ns), lambda k, i: (k, 0, 0), pipeline_mode=once),
            pl.BlockSpec((None, 2 * ns, width), lambda k, i: (k, 0, 0), pipeline_mode=once),
            pl.BlockSpec((None, 2, ns), lambda k, i: (k, 0, 0)),
        ],
        out_specs=pl.BlockSpec((tc, S5_KT_CH), lambda k, i: (i, k)),
        out_shape=jax.ShapeDtypeStruct((t, S5_CHANNELS), F32),
        scratch_shapes=[pltpu.VMEM((tcr + SUBLANES, 2 * ns), F32), pltpu.VMEM((SUBLANES, 2 * ns), F32),
                        pltpu.VMEM((2, SUBLANES, ns), F32), pltpu.VMEM((6, SUBLANES, ns), F32),
                        pltpu.VMEM((S5_KT_CH // LANES, tc, LANES), F32)],
        compiler_params=_params(("arbitrary", "arbitrary")),
        name="s5_core",
    )(h1, mt, bbs, ccs, lam_l)


def _s5_glu_kernel(y_ref, u_ref, d_ref, wglu_ref, bglu_ref, o_ref):
    y = y_ref[...] + d_ref[...] * u_ref[...].astype(F32)
    g = 0.5 * y * (1.0 + jnp.tanh(math.sqrt(2.0 / math.pi) * (y + 0.044715 * (y * y * y))))
    gl = _dot(g.astype(BF16), wglu_ref[...]) + bglu_ref[...]
    o_ref[...] = (g / (1.0 + jnp.exp(-gl))).astype(o_ref.dtype)


def _s5_glu(y, h1, d_skip, w_glu, b_glu, tm):
    t = h1.shape[0]
    tm = min(tm, t)
    full = lambda shape: pl.BlockSpec(shape, lambda i: (0,) * len(shape))
    return pl.pallas_call(
        _s5_glu_kernel,
        grid=(t // tm,),
        in_specs=[pl.BlockSpec((tm, S5_CHANNELS), lambda i: (i, 0)),
                  pl.BlockSpec((tm, S5_CHANNELS), lambda i: (i, 3)),
                  full((1, S5_CHANNELS)), full((S5_CHANNELS, S5_CHANNELS)), full((1, S5_CHANNELS))],
        out_specs=pl.BlockSpec((tm, S5_CHANNELS), lambda i: (i, 0)),
        out_shape=jax.ShapeDtypeStruct((t, S5_CHANNELS), BF16),
        compiler_params=_params(("parallel",)),
        name="s5_glu",
    )(y, h1, d_skip, w_glu, b_glu)


def _route(probs, sel):
    rows = [sel[e:e + 1, :] for e in range(N_EXPERTS)]
    prow = [probs[e:e + 1, :] for e in range(N_EXPERTS)]
    best_score = None
    for g in range(N_GROUPS):
        a = rows[g * EXPERTS_PER_GROUP:(g + 1) * EXPERTS_PER_GROUP]
        score = None
        for x in range(EXPERTS_PER_GROUP):
            for y in range(x + 1, EXPERTS_PER_GROUP):
                pair = a[x] + a[y]
                score = pair if score is None else jnp.maximum(score, pair)
        if best_score is None:
            best_score, gbest = score, jnp.zeros(score.shape, jnp.int32)
            cs = list(a)
            cp = prow[0:EXPERTS_PER_GROUP]
        else:
            better = score > best_score
            best_score = jnp.where(better, score, best_score)
            gbest = jnp.where(better, g, gbest)
            cs = [jnp.where(better, a[x], cs[x]) for x in range(EXPERTS_PER_GROUP)]
            cp = [jnp.where(better, prow[g * EXPERTS_PER_GROUP + x], cp[x]) for x in range(EXPERTS_PER_GROUP)]

    def first_argmax(vals):
        bv, bi = vals[0], jnp.zeros(vals[0].shape, jnp.int32)
        for x in range(1, len(vals)):
            better = vals[x] > bv
            bv = jnp.where(better, vals[x], bv)
            bi = jnp.where(better, x, bi)
        return bi

    i1 = first_argmax(cs)
    i2 = first_argmax([jnp.where(i1 == x, -jnp.inf, cs[x]) for x in range(EXPERTS_PER_GROUP)])

    def pick(idx):
        out = cp[0]
        for x in range(1, EXPERTS_PER_GROUP):
            out = jnp.where(idx == x, cp[x], out)
        return out

    p1, p2 = pick(i1), pick(i2)
    tot = p1 + p2
    return gbest * EXPERTS_PER_GROUP + i1, gbest * EXPERTS_PER_GROUP + i2, p1 / tot, p2 / tot


def _ln_router_tail(v, g_ref, b_ref, rwh_ref, rwl_ref, rb_ref):
    xn = _layer_norm(v, g_ref[...], b_ref[...])
    xh = xn.astype(BF16)
    xl = (xn - xh.astype(F32)).astype(BF16)
    logits = _nt_dot(rwh_ref[...], xh) + (_nt_dot(rwl_ref[...], xh) + _nt_dot(rwh_ref[...], xl))
    mx = jnp.max(logits, axis=0, keepdims=True)
    ex = jnp.exp(logits - mx)
    probs = ex / jnp.sum(ex, axis=0, keepdims=True)
    e1, e2, w1, w2 = _route(probs, probs + rb_ref[...])
    n = e1.shape[1]
    ei = jnp.concatenate([e1, e2, jnp.zeros((SUBLANES - 2, n), jnp.int32)], axis=0)
    ew = jnp.concatenate([w1, w2, jnp.zeros((SUBLANES - 2, n), F32)], axis=0)
    return xn, ei, ew


def _outproj_kernel(ma_ref, mb_ref, wa_ref, wb_ref, xr_ref, g_ref, b_ref, rwh_ref, rwl_ref, rb_ref,
                    x_ref, ei_ref, ew_ref, *, sub):
    for s in range(x_ref.shape[0] // sub):
        rows = slice(s * sub, (s + 1) * sub)
        v = (DEEPNORM_ALPHA * xr_ref[rows, :] + _dot(ma_ref[rows, :], wa_ref[...])
             + _dot(mb_ref[rows, :], wb_ref[...]))
        xn, ei, ew = _ln_router_tail(v, g_ref, b_ref, rwh_ref, rwl_ref, rb_ref)
        x_ref[rows, :] = xn
        ei_ref[:, rows] = ei
        ew_ref[:, rows] = ew


def _outproj_ln_router(mix_a, mix_b, w_out, x_res, ln_g, ln_b, rwt, rb, tm, sub):
    t = x_res.shape[0]
    rwt_hi = rwt.astype(BF16)
    rwt_lo = (rwt - rwt_hi.astype(F32)).astype(BF16)
    tm = min(tm, t)
    sub = min(sub, tm)
    half = w_out.shape[0] // 2
    row = lambda w: pl.BlockSpec((tm, w), lambda i: (i, 0))
    full = lambda shape: pl.BlockSpec(shape, lambda i: (0,) * len(shape))
    return pl.pallas_call(
        functools.partial(_outproj_kernel, sub=sub),
        grid=(t // tm,),
        in_specs=[row(half), row(half),
                  pl.BlockSpec((half, D_MODEL), lambda i: (0, 0)), pl.BlockSpec((half, D_MODEL), lambda i: (1, 0)),
                  row(D_MODEL), full((1, D_MODEL)), full((1, D_MODEL)),
                  full((N_EXPERTS, D_MODEL)), full((N_EXPERTS, D_MODEL)), full((N_EXPERTS, 1))],
        out_specs=[row(D_MODEL),
                   pl.BlockSpec((SUBLANES, tm), lambda i: (0, i)), pl.BlockSpec((SUBLANES, tm), lambda i: (0, i))],
        out_shape=[jax.ShapeDtypeStruct((t, D_MODEL), F32),
                   jax.ShapeDtypeStruct((SUBLANES, t), jnp.int32), jax.ShapeDtypeStruct((SUBLANES, t), F32)],
        compiler_params=_params(("parallel",)),
        name="outproj_ln_router",
    )(mix_a, mix_b, w_out, w_out, x_res, ln_g, ln_b, rwt_hi, rwt_lo, rb)


def _rank_kernel(e_ref, rank_ref, cnt_ref, base_ref, *, tn):
    @pl.when(pl.program_id(0) == 0)
    def _():
        base_ref[...] = jnp.zeros(base_ref.shape, F32)

    e = e_ref[...]
    rows = lax.broadcasted_iota(jnp.int32, (N_EXPERTS, tn), 0)
    oh1 = rows == e[0:1, :]
    oh2 = rows == e[1:2, :]
    oh = jnp.where(jnp.logical_or(oh1, oh2), 1.0, 0.0)
    r = lax.broadcasted_iota(jnp.int32, (tn, tn), 0)
    c = lax.broadcasted_iota(jnp.int32, (tn, tn), 1)
    earlier = jnp.where(r < c, 1.0, 0.0).astype(BF16)
    base = base_ref[...]
    before = _dot(oh.astype(BF16), earlier) + jnp.tile(base, (1, tn // LANES))
    r1 = jnp.sum(jnp.where(oh1, before, 0.0), axis=0, keepdims=True)
    r2 = jnp.sum(jnp.where(oh2, before, 0.0), axis=0, keepdims=True)
    rank_ref[...] = jnp.concatenate([r1, r2, jnp.zeros((SUBLANES - 2, tn), F32)], axis=0).astype(jnp.int32)
    base = base + jnp.sum(oh, axis=1, keepdims=True)
    base_ref[...] = base
    cnt_ref[...] = base.astype(jnp.int32)


def _rank(eidx, tn):
    t = eidx.shape[1]
    tn = min(tn, t)
    return pl.pallas_call(
        functools.partial(_rank_kernel, tn=tn),
        grid=(t // tn,),
        in_specs=[pl.BlockSpec((SUBLANES, tn), lambda i: (0, i))],
        out_specs=[pl.BlockSpec((SUBLANES, tn), lambda i: (0, i)),
                   pl.BlockSpec((N_EXPERTS, LANES), lambda i: (0, 0))],
        out_shape=[jax.ShapeDtypeStruct((SUBLANES, t), jnp.int32),
                   jax.ShapeDtypeStruct((N_EXPERTS, LANES), jnp.int32)],
        scratch_shapes=[pltpu.VMEM((N_EXPERTS, LANES), F32)],
        compiler_params=_params(("arbitrary",)),
        name="moe_rank",
    )(eidx)


def _moe_kernel(te_ref, nu_ref, xs_ref, wg_ref, wu_ref, wd_ref, y_ref, wgu_s, wd_s):
    tile = pl.program_id(0)
    used = tile < nu_ref[0]
    fresh = jnp.logical_or(tile == 0, te_ref[tile] != te_ref[jnp.maximum(tile - 1, 0)])

    @pl.when(jnp.logical_and(used, fresh))
    def _():
        wgu_s[:, :D_EXPERT] = wg_ref[...].astype(BF16)
        wgu_s[:, D_EXPERT:] = wu_ref[...].astype(BF16)
        wd_s[...] = wd_ref[...].astype(BF16)

    @pl.when(used)
    def _():
        for s in range(xs_ref.shape[0] // MOE_SUB):
            rows = slice(s * MOE_SUB, (s + 1) * MOE_SUB)
            hgu = _dot(xs_ref[rows, :].astype(BF16), wgu_s[...])
            hg = hgu[:, :D_EXPERT]
            hu = hgu[:, D_EXPERT:]
            h = (hg / (1.0 + jnp.exp(-hg))) * hu
            y_ref[rows, :] = _dot(h.astype(BF16), wd_s[...])

    @pl.when(jnp.logical_not(used))
    def _():
        y_ref[...] = jnp.zeros(y_ref.shape, F32)


def _moe(tile_expert, n_used, xs, w_gate, w_up, w_down, tm):
    t_pad = xs.shape[0]
    grid_spec = pltpu.PrefetchScalarGridSpec(
        num_scalar_prefetch=2,
        grid=(t_pad // tm,),
        in_specs=[
            pl.BlockSpec((tm, D_MODEL), lambda i, te, nu: (i, 0)),
            pl.BlockSpec((None, D_MODEL, D_EXPERT), lambda i, te, nu: (te[i], 0, 0)),
            pl.BlockSpec((None, D_MODEL, D_EXPERT), lambda i, te, nu: (te[i], 0, 0)),
            pl.BlockSpec((None, D_EXPERT, D_MODEL), lambda i, te, nu: (te[i], 0, 0)),
        ],
        out_specs=pl.BlockSpec((tm, D_MODEL), lambda i, te, nu: (i, 0)),
        scratch_shapes=[pltpu.VMEM((D_MODEL, 2 * D_EXPERT), BF16), pltpu.VMEM((D_EXPERT, D_MODEL), BF16)],
    )
    return pl.pallas_call(
        _moe_kernel,
        grid_spec=grid_spec,
        out_shape=jax.ShapeDtypeStruct((t_pad, D_MODEL), F32),
        compiler_params=_params(("arbitrary",)),
        name="moe",
    )(tile_expert, n_used, xs, w_gate, w_up, w_down)


def _moe_ln_kernel(x_ref, y1_ref, y2_ref, w_ref, g_ref, b_ref, o_ref):
    w = w_ref[...]
    v = DEEPNORM_ALPHA * x_ref[...] + (w[:, 0:1] * y1_ref[...] + w[:, 1:2] * y2_ref[...])
    o_ref[...] = _layer_norm(v, g_ref[...], b_ref[...])


def _moe_ln(x, y1, y2, w_cols, ln_g, ln_b, tm):
    t = x.shape[0]
    tm = min(tm, t)
    row = pl.BlockSpec((tm, D_MODEL), lambda i: (i, 0))
    vec = pl.BlockSpec((1, D_MODEL), lambda i: (0, 0))
    return pl.pallas_call(
        _moe_ln_kernel,
        grid=(t // tm,),
        in_specs=[row, row, row, pl.BlockSpec((tm, SUBLANES), lambda i: (i, 0)), vec, vec],
        out_specs=row,
        out_shape=jax.ShapeDtypeStruct((t, D_MODEL), F32),
        compiler_params=_params(("parallel",)),
        name="moe_ln",
    )(x, y1, y2, w_cols, ln_g, ln_b)


def _moe_block(x, eidx, ew, w_gate, w_up, w_down, ln_g, ln_b, tm):
    t = x.shape[0]
    tm = min(tm, t)
    rank, cnt = _rank(eidx, 512)
    counts = cnt[:, 0]
    padded = ((counts + tm - 1) // tm) * tm
    ends = jnp.cumsum(padded)
    starts = ends - padded
    e2 = eidx[:2]
    start_of = jnp.zeros_like(e2)
    for k in range(N_EXPERTS):
        start_of = jnp.where(e2 == k, starts[k], start_of)
    dest = start_of + rank[:2]
    t_pad = 2 * t + N_EXPERTS * tm
    tok = jnp.tile(jnp.arange(t, dtype=jnp.int32), 2)
    src_tok = (jnp.arange(t_pad, dtype=jnp.int32) % t).at[dest.reshape(2 * t)].set(
        tok, mode="promise_in_bounds", unique_indices=True)
    tile_start = jnp.arange(t_pad // tm, dtype=jnp.int32) * tm
    tile_expert = jnp.minimum(jnp.sum((ends[None, :] <= tile_start[:, None]).astype(jnp.int32), axis=1),
                              N_EXPERTS - 1)
    n_used = (ends[-1] // tm).astype(jnp.int32).reshape(1)
    xs = x.at[src_tok].get(mode="promise_in_bounds")
    ys = _moe(tile_expert, n_used, xs, w_gate, w_up, w_down, tm)
    y1 = ys.at[dest[0]].get(mode="promise_in_bounds")
    y2 = ys.at[dest[1]].get(mode="promise_in_bounds")
    return _moe_ln(x, y1, y2, ew.T, ln_g, ln_b, tm)


def kernel(x, router_w, router_b, l0_w_in, l0_diff_lam_q1, l0_diff_lam_k1, l0_diff_lam_q2, l0_diff_lam_k2, l0_diff_subln_g, l0_gla_w_gate, l0_gla_b_gate, l0_gla_norm_g, l0_w_out, l0_ln1_g, l0_ln1_b, l0_moe_w_gate, l0_moe_w_up, l0_moe_w_down, l0_ln2_g, l0_ln2_b, l1_w_in, l1_s5_a_re, l1_s5_a_im, l1_s5_log_step, l1_s5_b_re, l1_s5_b_im, l1_s5_c_re, l1_s5_c_im, l1_s5_d, l1_s5_w_glu, l1_s5_b_glu, l1_w_out, l1_ln1_g, l1_ln1_b, l1_moe_w_gate, l1_moe_w_up, l1_moe_w_down, l1_ln2_g, l1_ln2_b):
    bsz, t, d = x.shape
    assert bsz == 1 and d == D_MODEL
    x0 = x.reshape(t, d)
    vec = lambda p: p.reshape(1, -1).astype(F32)
    rwt = router_w.T.astype(F32)
    rb = router_b.reshape(N_EXPERTS, 1).astype(F32)

    lambda_init = 0.8 - 0.6 * math.exp(-0.3 * 0)
    c = [0, 1024, 2048, 3072, 3584, 4096, 5120, 5136, 6160]
    q_scale = math.log2(math.e) * DIFF_HEAD_DIM ** -0.5
    w0 = jnp.concatenate([l0_w_in[:, :c[1]] * q_scale, l0_w_in[:, c[1]:c[6]], l0_w_in[:, c[7]:c[8]],
                          l0_w_in[:, c[6]:c[7]], jnp.zeros((d, LANES - GLA_GATE_RANK), l0_w_in.dtype)],
                         axis=1).astype(BF16)
    h0 = _matmul(x0, w0, 1024, 896, BF16)
    lam_vec = jnp.stack([l0_diff_lam_q1, l0_diff_lam_k1, l0_diff_lam_q2, l0_diff_lam_k2]).astype(F32)
    o_a = _diff_attention(h0, lam_vec, vec(l0_diff_subln_g), lambda_init, 1024)
    wg_pad = jnp.concatenate([l0_gla_w_gate, jnp.zeros((LANES - GLA_GATE_RANK, GLA_HEADS * GLA_DK), F32)],
                             axis=0).astype(BF16)
    o_b = _gla(h0, wg_pad, vec(l0_gla_b_gate), vec(l0_gla_norm_g), 256)
    x1, eidx, ew = _outproj_ln_router(o_a, o_b, l0_w_out.astype(BF16), x0, vec(l0_ln1_g), vec(l0_ln1_b),
                                      rwt, rb, 512, 256)
    x2 = _moe_block(x1, eidx, ew, l0_moe_w_gate, l0_moe_w_up, l0_moe_w_down, vec(l0_ln2_g), vec(l0_ln2_b), 256)

    h1 = _matmul(x2, l1_w_in.astype(BF16), 1024, 1024, BF16)
    o_c = _stick_breaking(h1, 256)
    lbr, lbi, bbr, bbi = _s5_prep(l1_s5_a_re, l1_s5_a_im, l1_s5_log_step, l1_s5_b_re, l1_s5_b_im)
    bb_re = jnp.transpose(bbr.reshape(S5_GROUPS, S5_GROUP, S5_STATE), (0, 2, 1))
    bb_im = jnp.transpose(bbi.reshape(S5_GROUPS, S5_GROUP, S5_STATE), (0, 2, 1))
    mt, bbs, ccs, lam_l = _s5_chunk_weights(lbr[::S5_GROUP], lbi[::S5_GROUP], bb_re, bb_im,
                                            l1_s5_c_re.astype(F32), l1_s5_c_im.astype(F32))
    y_s5 = _s5_core(h1, mt, bbs, ccs, lam_l, 256)
    o_d = _s5_glu(y_s5, h1, vec(l1_s5_d), l1_s5_w_glu.astype(BF16), vec(l1_s5_b_glu), 512)
    x3, eidx1, ew1 = _outproj_ln_router(o_c, o_d, l1_w_out.astype(BF16), x2, vec(l1_ln1_g), vec(l1_ln1_b),
                                        rwt, rb, 512, 256)
    x4 = _moe_block(x3, eidx1, ew1, l1_moe_w_gate, l1_moe_w_up, l1_moe_w_down, vec(l1_ln2_g), vec(l1_ln2_b), 256)
    return x4.reshape(bsz, t, d)
```

```python
import functools
import math

import jax
import jax.numpy as jnp
from jax import lax
from jax.experimental import pallas as pl
from jax.experimental.pallas import tpu as pltpu

F32 = jnp.float32
BF16 = jnp.bfloat16

D_MODEL = 2048
DEPTH = 2
DIFF_HEADS = 8
DIFF_HEAD_DIM = 64
DIFF_V_DIM = 128
GLA_HEADS = 4
GLA_DK = 128
GLA_DV = 256
GLA_GATE_RANK = 16
GLA_GATE_TEMP = 16.0
GLA_CHUNK = 64
SB_HEADS = 8
SB_HEAD_DIM = 128
S5_CHANNELS = 1024
S5_GROUP = 16
S5_GROUPS = 64
S5_STATE = 64
N_EXPERTS = 16
N_GROUPS = 4
EXPERTS_PER_GROUP = 4
D_EXPERT = 640
DEEPNORM_ALPHA = (2.0 * DEPTH) ** 0.25
LN_EPS = 1e-5

LANES = 128
SUBLANES = 8
VMEM_LIMIT = 56 * 1024 * 1024
SB_LOG_FLOOR = 120.0
SB_HEADS_PER_STEP = 4
MOE_SUB = 256

S5_KT = 4
S5_KT_CH = S5_CHANNELS // S5_KT
S5_KT_STATES = (S5_GROUPS // S5_KT) * S5_STATE
S5_SCAN_W = 512
S5_L = 8


def _params(sem, flags=None):
    return pltpu.CompilerParams(dimension_semantics=sem, vmem_limit_bytes=VMEM_LIMIT, flags=flags)


def _nt_dot(a, b):
    return lax.dot_general(a, b, (((1,), (1,)), ((), ())), preferred_element_type=F32)


def _dot(a, b):
    return jnp.dot(a, b, preferred_element_type=F32)


def _split_dot(x, m):
    hi = x.astype(BF16)
    lo = (x - hi.astype(F32)).astype(BF16)
    return _dot(hi, m) + _dot(lo, m)


def _split_dot_left(m, x):
    hi = x.astype(BF16)
    lo = (x - hi.astype(F32)).astype(BF16)
    return _dot(m, hi) + _dot(m, lo)


def _layer_norm(v, g, b):
    mu = jnp.mean(v, axis=-1, keepdims=True)
    d = v - mu
    var = jnp.mean(d * d, axis=-1, keepdims=True)
    return d * lax.rsqrt(var + LN_EPS) * g + b


def _mm_kernel(a_ref, b_ref, o_ref, a_bf):
    @pl.when(pl.program_id(1) == 0)
    def _():
        a_bf[...] = a_ref[...].astype(BF16)

    o_ref[...] = _dot(a_bf[...], b_ref[...]).astype(o_ref.dtype)


def _matmul(a, b, tm, tn, out_dtype):
    m, k = a.shape
    n = b.shape[1]
    tm = min(tm, m)
    return pl.pallas_call(
        _mm_kernel,
        grid=(m // tm, n // tn),
        in_specs=[pl.BlockSpec((tm, k), lambda i, j: (i, 0)),
                  pl.BlockSpec((k, tn), lambda i, j: (0, j))],
        out_specs=pl.BlockSpec((tm, tn), lambda i, j: (i, j)),
        out_shape=jax.ShapeDtypeStruct((m, n), out_dtype),
        scratch_shapes=[pltpu.VMEM((tm, k), BF16)],
        compiler_params=_params(("parallel", "arbitrary")),
        name="in_proj",
    )(a, b)


def _diff_attn_kernel(lam_ref, g_ref, q_ref, k_ref, v_ref, o_ref, vx_ref, m_ref, acc_ref, sa_ref, sb_ref, *, tq,
                      lambda_init):
    i = pl.program_id(1)
    dv = DIFF_V_DIM

    @pl.when(i == 0)
    def _():
        vx_ref[:, :dv] = v_ref[...]
        vx_ref[:, dv:] = jnp.ones((vx_ref.shape[0], dv), BF16)

    m_ref[...] = jnp.full(m_ref.shape, -jnp.inf, F32)
    acc_ref[...] = jnp.zeros(acc_ref.shape, F32)
    q = q_ref[...]
    lane = lax.broadcasted_iota(jnp.int32, q.shape, 1)
    zero = jnp.zeros_like(q)
    halves = (jnp.where(lane < DIFF_HEAD_DIM, q, zero), jnp.where(lane >= DIFF_HEAD_DIM, q, zero))

    def scores(j, mi):
        off = pl.multiple_of(j * tq, tq)
        return _nt_dot(halves[mi], k_ref[pl.ds(off, tq), :])

    def softmax_pv(j, get_s, masked):
        off = pl.multiple_of(j * tq, tq)
        vx = vx_ref[pl.ds(off, tq), :]
        if masked:
            row = lax.broadcasted_iota(jnp.int32, (tq, tq), 0)
            col = lax.broadcasted_iota(jnp.int32, (tq, tq), 1)
            causal = col <= row
        for mi in range(2):
            s = get_s(mi)
            if masked:
                s = jnp.where(causal, s, -jnp.inf)
            m_prev = m_ref[mi]
            m_new = jnp.maximum(m_prev, jnp.max(s, axis=1, keepdims=True))
            p = jnp.exp2(s - jnp.tile(m_new, (1, tq // LANES)))
            alpha = jnp.exp2(m_prev - m_new)
            acc_ref[mi] = jnp.tile(alpha, (1, 2)) * acc_ref[mi] + _dot(p.astype(BF16), vx)
            m_ref[mi] = m_new

    def scores_to(j, s_ref):
        for mi in range(2):
            s_ref[mi] = scores(j, mi)

    scores_to(0, sa_ref)

    def pair(t, carry):
        j = 2 * t
        scores_to(j + 1, sb_ref)
        softmax_pv(j, lambda mi: sa_ref[mi], False)
        scores_to(j + 2, sa_ref)
        softmax_pv(j + 1, lambda mi: sb_ref[mi], False)
        return carry

    lax.fori_loop(0, i // 2, pair, 0)

    @pl.when(i % 2 == 1)
    def _():
        softmax_pv(i - 1, lambda mi: sa_ref[mi], False)
        scores_to(i, sa_ref)

    softmax_pv(i, lambda mi: sa_ref[mi], True)

    lv = lam_ref[...]
    lam = (jnp.exp(jnp.sum(lv[0:1] * lv[1:2], axis=1, keepdims=True))
           - jnp.exp(jnp.sum(lv[2:3] * lv[3:4], axis=1, keepdims=True)) + lambda_init)
    a0 = acc_ref[0]
    a1 = acc_ref[1]
    o = a0[:, :dv] / a0[:, dv:] - lam * (a1[:, :dv] / a1[:, dv:])
    ms = jnp.mean(o * o, axis=-1, keepdims=True)
    o = o * lax.rsqrt(ms + LN_EPS) * g_ref[...] * (1.0 - lambda_init)
    o_ref[...] = o.astype(o_ref.dtype)


def _diff_attention(h0, lam_vec, subln_g, lambda_init, tq):
    t = h0.shape[0]
    tq = min(tq, t)
    nq = t // tq
    kern = functools.partial(_diff_attn_kernel, tq=tq, lambda_init=lambda_init)
    return pl.pallas_call(
        kern,
        grid=(DIFF_HEADS, nq),
        in_specs=[
            pl.BlockSpec((4, DIFF_HEAD_DIM), lambda h, i: (0, 0)),
            pl.BlockSpec((1, DIFF_V_DIM), lambda h, i: (0, 0)),
            pl.BlockSpec((tq, LANES), lambda h, i: (i, h)),
            pl.BlockSpec((t, LANES), lambda h, i: (0, DIFF_HEADS + h)),
            pl.BlockSpec((t, LANES), lambda h, i: (0, 2 * DIFF_HEADS + h)),
        ],
        out_specs=pl.BlockSpec((tq, LANES), lambda h, i: (i, h)),
        out_shape=jax.ShapeDtypeStruct((t, DIFF_HEADS * DIFF_V_DIM), BF16),
        scratch_shapes=[pltpu.VMEM((t, 2 * DIFF_V_DIM), BF16), pltpu.VMEM((2, tq, LANES), F32),
                        pltpu.VMEM((2, tq, 2 * DIFF_V_DIM), F32),
                        pltpu.VMEM((2, tq, tq), F32), pltpu.VMEM((2, tq, tq), F32)],
        compiler_params=_params(("arbitrary", "arbitrary")),
        name="diff_attn",
    )(lam_vec, subln_g, h0, h0, h0)


def _gla_kernel(q_ref, k_ref, v_ref, r_ref, gb_ref, wg_ref, bg_ref, ng_ref, o_ref, st_ref, *, tg):
    @pl.when(pl.program_id(0) == 0)
    def _():
        st_ref[...] = jnp.zeros(st_ref.shape, F32)

    c = GLA_CHUNK
    gate = _dot(gb_ref[...], wg_ref[...]) + bg_ref[...]
    log_a = -(jnp.maximum(-gate, 0.0) + jnp.log(1.0 + jnp.exp(-jnp.abs(gate)))) / GLA_GATE_TEMP
    row = lax.broadcasted_iota(jnp.int32, (c, c), 0)
    col = lax.broadcasted_iota(jnp.int32, (c, c), 1)
    tril = row >= col
    tri = jnp.where(tril, 1.0, 0.0).astype(BF16)
    for h in range(GLA_HEADS):
        ksl = slice(h * GLA_DK, (h + 1) * GLA_DK)
        vsl = slice(h * GLA_DV, (h + 1) * GLA_DV)
        for ci in range(tg // c):
            rs = slice(ci * c, (ci + 1) * c)
            la = log_a[rs, ksl]
            b = _split_dot_left(tri, la)
            b_last = b[c - 1:c, :]
            qf = q_ref[rs, ksl].astype(F32) * (GLA_DK ** -0.5)
            kf = k_ref[rs, ksl].astype(F32)
            v = v_ref[rs, vsl]
            q_dec = (qf * jnp.exp(b)).astype(BF16)
            k_dec = (kf * jnp.exp(-b)).astype(BF16)
            k_end = (kf * jnp.exp(b_last - b)).astype(BF16)
            scores = jnp.where(tril, _nt_dot(q_dec, k_dec), 0.0)
            st = st_ref[h]
            o = _dot(scores.astype(BF16), v) + _nt_dot(q_dec, st.astype(BF16))
            kv_t = lax.dot_general(v, k_end, (((0,), (0,)), ((), ())), preferred_element_type=F32)
            st_ref[h] = jnp.exp(b_last) * st + kv_t
            ms = jnp.mean(o * o, axis=-1, keepdims=True)
            o = o * lax.rsqrt(ms + LN_EPS) * ng_ref[...]
            r = r_ref[rs, vsl].astype(F32)
            o_ref[rs, vsl] = (o * (r / (1.0 + jnp.exp(-r)))).astype(o_ref.dtype)


def _gla(h0, wg_pad, b_gate, norm_g, tg):
    t = h0.shape[0]
    tg = min(tg, t)
    qk = GLA_HEADS * GLA_DK
    vw = GLA_HEADS * GLA_DV
    return pl.pallas_call(
        functools.partial(_gla_kernel, tg=tg),
        grid=(t // tg,),
        in_specs=[
            pl.BlockSpec((tg, qk), lambda i: (i, 3072 // qk)),
            pl.BlockSpec((tg, qk), lambda i: (i, 3584 // qk)),
            pl.BlockSpec((tg, vw), lambda i: (i, 4096 // vw)),
            pl.BlockSpec((tg, vw), lambda i: (i, 5120 // vw)),
            pl.BlockSpec((tg, LANES), lambda i: (i, 6144 // LANES)),
            pl.BlockSpec((LANES, qk), lambda i: (0, 0)),
            pl.BlockSpec((1, qk), lambda i: (0, 0)),
            pl.BlockSpec((1, GLA_DV), lambda i: (0, 0)),
        ],
        out_specs=pl.BlockSpec((tg, vw), lambda i: (i, 0)),
        out_shape=jax.ShapeDtypeStruct((t, vw), BF16),
        scratch_shapes=[pltpu.VMEM((GLA_HEADS, GLA_DV, GLA_DK), F32)],
        compiler_params=_params(("arbitrary",)),
        name="gla",
    )(h0, h0, h0, h0, h0, wg_pad, b_gate, norm_g)


def _sb_kernel(q_ref, k_ref, v_ref, o_ref, acc_ref, c_ref, *, tq):
    i = pl.program_id(1)
    hd = SB_HEAD_DIM
    acc_ref[...] = jnp.zeros(acc_ref.shape, F32)
    c_ref[...] = jnp.zeros(c_ref.shape, F32)
    row = lax.broadcasted_iota(jnp.int32, (tq, tq), 0)
    col = lax.broadcasted_iota(jnp.int32, (tq, tq), 1)
    strict = col < row
    tri = jnp.where(row > col, 1.0, 0.0).astype(BF16)

    def block(jb, masked):
        off = pl.multiple_of(jb * tq, tq)
        cmax = None
        for hh in range(SB_HEADS_PER_STEP):
            cols = slice(hh * hd, (hh + 1) * hd)
            k = k_ref[pl.ds(off, tq), cols]
            v = v_ref[pl.ds(off, tq), cols]
            z = _nt_dot(q_ref[:, cols], k) * (SB_HEAD_DIM ** -0.5)
            lsn = -(jnp.maximum(z, 0.0) + jnp.log(1.0 + jnp.exp(-jnp.abs(z))))
            lm = jnp.where(strict, lsn, 0.0) if masked else lsn
            excl = _split_dot(lm, tri) + c_ref[hh]
            w = jnp.exp(z + lsn + excl)
            if masked:
                w = jnp.where(strict, w, 0.0)
            acc_ref[:, cols] += _dot(w.astype(BF16), v)
            c_new = c_ref[hh] + jnp.sum(lm, axis=1, keepdims=True)
            c_ref[hh] = c_new
            hmax = jnp.max(c_new)
            cmax = hmax if cmax is None else jnp.maximum(cmax, hmax)
        return cmax

    cmax = block(i, True)

    def cond(carry):
        jb, cm = carry
        return jnp.logical_and(jb >= 0, cm > -SB_LOG_FLOOR)

    def body(carry):
        jb, _ = carry
        return jb - 1, block(jb, False)

    lax.while_loop(cond, body, (i - 1, cmax))
    o_ref[...] = acc_ref[...].astype(o_ref.dtype)


def _stick_breaking(h1, tq):
    t = h1.shape[0]
    tq = min(tq, t)
    ng = SB_HEADS // SB_HEADS_PER_STEP
    wd = SB_HEADS_PER_STEP * SB_HEAD_DIM
    return pl.pallas_call(
        functools.partial(_sb_kernel, tq=tq),
        grid=(ng, t // tq),
        in_specs=[
            pl.BlockSpec((tq, wd), lambda h, i: (i, h)),
            pl.BlockSpec((t, wd), lambda h, i: (0, ng + h), pipeline_mode=pl.Buffered(1)),
            pl.BlockSpec((t, wd), lambda h, i: (0, 2 * ng + h), pipeline_mode=pl.Buffered(1)),
        ],
        out_specs=pl.BlockSpec((tq, wd), lambda h, i: (i, h)),
        out_shape=jax.ShapeDtypeStruct((t, SB_HEADS * SB_HEAD_DIM), BF16),
        scratch_shapes=[pltpu.VMEM((tq, wd), F32), pltpu.VMEM((SB_HEADS_PER_STEP, tq, 1), F32)],
        compiler_params=_params(("parallel", "arbitrary")),
        name="stick_breaking",
    )(h1, h1, h1)


def _s5_prep_kernel(are_ref, aim_ref, ls_ref, bre_ref, bim_ref, lbr_ref, lbi_ref, bbr_ref, bbi_ref):
    step = jnp.exp(ls_ref[...])
    lam_re = are_ref[...]
    lam_im = aim_ref[...]
    mag = jnp.exp(step * lam_re)
    lb_re = mag * jnp.cos(step * lam_im)
    lb_im = mag * jnp.sin(step * lam_im)
    den = lam_re * lam_re + lam_im * lam_im
    n_re = lb_re - 1.0
    f_re = (n_re * lam_re + lb_im * lam_im) / den
    f_im = (lb_im * lam_re - n_re * lam_im) / den
    lbr_ref[...] = lb_re
    lbi_ref[...] = lb_im
    bbr_ref[...] = f_re * bre_ref[...] - f_im * bim_ref[...]
    bbi_ref[...] = f_re * bim_ref[...] + f_im * bre_ref[...]


def _s5_prep(a_re, a_im, log_step, b_re, b_im):
    rows = S5_GROUPS * S5_GROUP
    rep = lambda p: jnp.repeat(p, S5_GROUP, axis=0)
    to_rows = lambda b: jnp.transpose(b, (0, 2, 1)).reshape(rows, S5_STATE)
    spec = pl.BlockSpec((rows, S5_STATE), lambda: (0, 0))
    shp = jax.ShapeDtypeStruct((rows, S5_STATE), F32)
    return pl.pallas_call(
        _s5_prep_kernel,
        in_specs=[spec, spec, pl.BlockSpec((rows, 1), lambda: (0, 0)), spec, spec],
        out_specs=[spec] * 4,
        out_shape=[shp] * 4,
        name="s5_prep",
    )(rep(a_re), rep(a_im), rep(log_step[:, None]), to_rows(b_re), to_rows(b_im))


def _cmul(ar, ai, br, bi):
    return ar * br - ai * bi, ar * bi + ai * br


def _block_diag_tiles(m, rows_per_group, cols_per_group):
    gpt = S5_GROUPS // S5_KT
    m = m.reshape(S5_KT, gpt, rows_per_group, cols_per_group)
    eye = jnp.eye(gpt, dtype=m.dtype)
    bd = jnp.einsum("kgrc,gh->kgrhc", m, eye)
    return bd.reshape(S5_KT, gpt * rows_per_group, gpt * cols_per_group)


def _s5_tile_params(lb_re, lb_im, bb_re, bb_im, c_re, c_im):
    bd = jnp.concatenate([_block_diag_tiles(bb_re, S5_GROUP, S5_STATE),
                          _block_diag_tiles(bb_im, S5_GROUP, S5_STATE)], axis=2)
    cd = jnp.concatenate([_block_diag_tiles(jnp.transpose(c_re, (0, 2, 1)), S5_STATE, S5_GROUP),
                          _block_diag_tiles(-jnp.transpose(c_im, (0, 2, 1)), S5_STATE, S5_GROUP)], axis=1)
    pr = [jnp.ones_like(lb_re)]
    pi = [jnp.zeros_like(lb_re)]
    for _ in range(S5_L):
        nr, ni = _cmul(pr[-1], pi[-1], lb_re, lb_im)
        pr.append(nr)
        pi.append(ni)

    def table(p):
        rows = jnp.transpose(jnp.stack(p).reshape(S5_L + 1, S5_KT, S5_KT_STATES), (1, 0, 2))
        return jnp.pad(rows, ((0, 0), (0, LANES - (S5_L + 1)), (0, 0)))

    return bd, cd, jnp.stack([table(pr), table(pi)], axis=1)


def _s5_core_kernel(u_ref, bd_ref, cd_ref, pw8_ref, y_ref, hs_ref, hp_ref, pw_ref, st_ref, uf_ref,
                    mt_s, bbs_s, ccs_s, *, tcr):
    ns = S5_KT_STATES
    pad = SUBLANES
    cw = S5_KT_CH

    @pl.when(pl.program_id(1) == 0)
    def _():
        b_re = bd_ref[:, :ns]
        b_im = bd_ref[:, ns:]
        cd_hi = cd_ref[...].astype(BF16)
        cd_lo = (cd_ref[...] - cd_hi.astype(F32)).astype(BF16)
        for k in range(S5_L):
            a = pw8_ref[0, k:k + 1, :]
            b = pw8_ref[1, k:k + 1, :]
            dk = jnp.concatenate([b_re * a - b_im * b, b_re * b + b_im * a], axis=1)
            s = S5_L - 1 - k
            bbs_s[s * cw:(s + 1) * cw, :] = dk.astype(BF16)
            dk_hi = dk.astype(BF16)
            dk_lo = (dk - dk_hi.astype(F32)).astype(BF16)
            kk = (_dot(dk_hi, cd_hi) + (_dot(dk_lo, cd_hi) + _dot(dk_hi, cd_lo))).astype(BF16)
            for s0 in range(S5_L - k):
                mt_s[s0 * cw:(s0 + 1) * cw, (s0 + k) * cw:(s0 + k + 1) * cw] = kk
        pw_t = (jnp.transpose(pw8_ref[0]), jnp.transpose(pw8_ref[1]))
        c_re = cd_ref[:ns, :]
        c_im = cd_ref[ns:, :]
        for i in range(S5_L):
            a = pw_t[0][:, i + 1:i + 2]
            b = pw_t[1][:, i + 1:i + 2]
            ccs_s[:ns, i * cw:(i + 1) * cw] = (c_re * a + c_im * b).astype(BF16)
            ccs_s[ns:, i * cw:(i + 1) * cw] = (c_im * a - c_re * b).astype(BF16)

        hp_ref[...] = jnp.zeros(hp_ref.shape, F32)
        hs_ref[0:pad, :] = jnp.zeros((pad, 2 * ns), F32)
        l1r = jnp.broadcast_to(pw8_ref[0, S5_L:S5_L + 1, :], (SUBLANES, ns))
        l1i = jnp.broadcast_to(pw8_ref[1, S5_L:S5_L + 1, :], (SUBLANES, ns))
        l2r, l2i = _cmul(l1r, l1i, l1r, l1i)
        l4r, l4i = _cmul(l2r, l2i, l2r, l2i)
        l8r, l8i = _cmul(l4r, l4i, l4r, l4i)
        ridx = lax.broadcasted_iota(jnp.int32, l1r.shape, 0)
        n = ridx + 1
        pr = jnp.ones_like(l1r)
        pi = jnp.zeros_like(l1r)
        for bit, (lr, li) in ((1, (l1r, l1i)), (2, (l2r, l2i)), (4, (l4r, l4i)), (8, (l8r, l8i))):
            nr, ni = _cmul(pr, pi, lr, li)
            take = (n & bit) != 0
            pr = jnp.where(take, nr, pr)
            pi = jnp.where(take, ni, pi)
        pw_ref[0] = pr
        pw_ref[1] = pi
        for s, (lr, li) in enumerate(((l1r, l1i), (l2r, l2i), (l4r, l4i))):
            keep = ridx >= (1 << s)
            st_ref[2 * s] = jnp.where(keep, lr, 0.0)
            st_ref[2 * s + 1] = jnp.where(keep, li, 0.0)

    nl = S5_KT_CH // LANES
    for h in range(nl):
        uf_ref[h] = u_ref[:, h * LANES:(h + 1) * LANES].astype(F32)
    uc = jnp.concatenate([uf_ref[h, pl.ds(s, tcr, stride=S5_L), :].astype(BF16)
                          for s in range(S5_L) for h in range(nl)], axis=1)
    hs_ref[pad:pad + tcr, :] = _dot(uc, bbs_s[...])
    w = S5_SCAN_W
    for cc in range(ns // w):
        re0 = cc * w
        im0 = ns + cc * w

        def body(r, carry, re0=re0, im0=im0):
            hpr, hpi = carry
            rows = pl.ds(pl.multiple_of(pad + r * SUBLANES, SUBLANES), SUBLANES)
            xr = hs_ref[rows, re0:re0 + w]
            xi = hs_ref[rows, im0:im0 + w]
            for s in range(3):
                sr = pltpu.roll(xr, 1 << s, 0)
                si = pltpu.roll(xi, 1 << s, 0)
                ar = st_ref[2 * s, :, re0:re0 + w]
                ai = st_ref[2 * s + 1, :, re0:re0 + w]
                xr, xi = xr + ar * sr - ai * si, xi + ar * si + ai * sr
            pr = pw_ref[0, :, re0:re0 + w]
            pi = pw_ref[1, :, re0:re0 + w]
            xr, xi = xr + pr * hpr - pi * hpi, xi + pr * hpi + pi * hpr
            hs_ref[rows, re0:re0 + w] = xr
            hs_ref[rows, im0:im0 + w] = xi
            return (jnp.broadcast_to(xr[SUBLANES - 1:SUBLANES, :], (SUBLANES, w)),
                    jnp.broadcast_to(xi[SUBLANES - 1:SUBLANES, :], (SUBLANES, w)))

        hpr, hpi = lax.fori_loop(0, tcr // SUBLANES, body, (hp_ref[:, re0:re0 + w], hp_ref[:, im0:im0 + w]))
        hp_ref[:, re0:re0 + w] = hpr
        hp_ref[:, im0:im0 + w] = hpi

    h_prev = hs_ref[pad - 1:pad - 1 + tcr, :]
    y_in = _dot(h_prev.astype(BF16), ccs_s[...])
    for i in range(S5_L):
        k_hi = (i + 1) * cw
        y_i = y_in[:, i * cw:k_hi] + _dot(uc[:, :k_hi], mt_s[:k_hi, i * cw:k_hi])
        for h in range(nl):
            uf_ref[h, pl.ds(i, tcr, stride=S5_L), :] = y_i[:, h * LANES:(h + 1) * LANES]
    for h in range(nl):
        y_ref[:, h * LANES:(h + 1) * LANES] = uf_ref[h]
    hs_ref[pad - 1:pad, :] = hs_ref[pad + tcr - 1:pad + tcr, :]


def _s5_core(h1, bd, cd, pw8, tcr):
    t = h1.shape[0]
    width = S5_L * S5_KT_CH
    tcr = min(tcr, t // S5_L)
    tc = tcr * S5_L
    ns = S5_KT_STATES
    u_col0 = 3 * S5_CHANNELS // S5_KT_CH
    once = pl.Buffered(1)
    return pl.pallas_call(
        functools.partial(_s5_core_kernel, tcr=tcr),
        grid=(S5_KT, t // tc),
        in_specs=[
            pl.BlockSpec((tc, S5_KT_CH), lambda k, i: (i, u_col0 + k)),
            pl.BlockSpec((None, S5_KT_CH, 2 * ns), lambda k, i: (k, 0, 0), pipeline_mode=once),
            pl.BlockSpec((None, 2 * ns, S5_KT_CH), lambda k, i: (k, 0, 0), pipeline_mode=once),
            pl.BlockSpec((None, 2, LANES, ns), lambda k, i: (k, 0, 0, 0), pipeline_mode=once),
        ],
        out_specs=pl.BlockSpec((tc, S5_KT_CH), lambda k, i: (i, k)),
        out_shape=jax.ShapeDtypeStruct((t, S5_CHANNELS), F32),
        scratch_shapes=[pltpu.VMEM((tcr + SUBLANES, 2 * ns), F32), pltpu.VMEM((SUBLANES, 2 * ns), F32),
                        pltpu.VMEM((2, SUBLANES, ns), F32), pltpu.VMEM((6, SUBLANES, ns), F32),
                        pltpu.VMEM((S5_KT_CH // LANES, tc, LANES), F32),
                        pltpu.VMEM((width, width), BF16), pltpu.VMEM((width, 2 * ns), BF16),
                        pltpu.VMEM((2 * ns, width), BF16)],
        compiler_params=_params(("arbitrary", "arbitrary")),
        name="s5_core",
    )(h1, bd, cd, pw8)


def _s5_glu_kernel(y_ref, u_ref, d_ref, wglu_ref, bglu_ref, o_ref):
    y = y_ref[...] + d_ref[...] * u_ref[...].astype(F32)
    g = 0.5 * y * (1.0 + jnp.tanh(math.sqrt(2.0 / math.pi) * (y + 0.044715 * (y * y * y))))
    gl = _dot(g.astype(BF16), wglu_ref[...]) + bglu_ref[...]
    o_ref[...] = (g / (1.0 + jnp.exp(-gl))).astype(o_ref.dtype)


def _s5_glu(y, h1, d_skip, w_glu, b_glu, tm):
    t = h1.shape[0]
    tm = min(tm, t)
    full = lambda shape: pl.BlockSpec(shape, lambda i: (0,) * len(shape))
    return pl.pallas_call(
        _s5_glu_kernel,
        grid=(t // tm,),
        in_specs=[pl.BlockSpec((tm, S5_CHANNELS), lambda i: (i, 0)),
                  pl.BlockSpec((tm, S5_CHANNELS), lambda i: (i, 3)),
                  full((1, S5_CHANNELS)), full((S5_CHANNELS, S5_CHANNELS)), full((1, S5_CHANNELS))],
        out_specs=pl.BlockSpec((tm, S5_CHANNELS), lambda i: (i, 0)),
        out_shape=jax.ShapeDtypeStruct((t, S5_CHANNELS), BF16),
        compiler_params=_params(("parallel",)),
        name="s5_glu",
    )(y, h1, d_skip, w_glu, b_glu)


def _route(probs, sel):
    rows = [sel[e:e + 1, :] for e in range(N_EXPERTS)]
    prow = [probs[e:e + 1, :] for e in range(N_EXPERTS)]
    best_score = None
    for g in range(N_GROUPS):
        a = rows[g * EXPERTS_PER_GROUP:(g + 1) * EXPERTS_PER_GROUP]
        score = None
        for x in range(EXPERTS_PER_GROUP):
            for y in range(x + 1, EXPERTS_PER_GROUP):
                pair = a[x] + a[y]
                score = pair if score is None else jnp.maximum(score, pair)
        if best_score is None:
            best_score, gbest = score, jnp.zeros(score.shape, jnp.int32)
            cs = list(a)
            cp = prow[0:EXPERTS_PER_GROUP]
        else:
            better = score > best_score
            best_score = jnp.where(better, score, best_score)
            gbest = jnp.where(better, g, gbest)
            cs = [jnp.where(better, a[x], cs[x]) for x in range(EXPERTS_PER_GROUP)]
            cp = [jnp.where(better, prow[g * EXPERTS_PER_GROUP + x], cp[x]) for x in range(EXPERTS_PER_GROUP)]

    def first_argmax(vals):
        bv, bi = vals[0], jnp.zeros(vals[0].shape, jnp.int32)
        for x in range(1, len(vals)):
            better = vals[x] > bv
            bv = jnp.where(better, vals[x], bv)
            bi = jnp.where(better, x, bi)
        return bi

    i1 = first_argmax(cs)
    i2 = first_argmax([jnp.where(i1 == x, -jnp.inf, cs[x]) for x in range(EXPERTS_PER_GROUP)])

    def pick(idx):
        out = cp[0]
        for x in range(1, EXPERTS_PER_GROUP):
            out = jnp.where(idx == x, cp[x], out)
        return out

    p1, p2 = pick(i1), pick(i2)
    tot = p1 + p2
    return gbest * EXPERTS_PER_GROUP + i1, gbest * EXPERTS_PER_GROUP + i2, p1 / tot, p2 / tot


def _ln_router_tail(v, g_ref, b_ref, rwh_ref, rwl_ref, rb_ref):
    xn = _layer_norm(v, g_ref[...], b_ref[...])
    xh = xn.astype(BF16)
    xl = (xn - xh.astype(F32)).astype(BF16)
    logits = _nt_dot(rwh_ref[...], xh) + (_nt_dot(rwl_ref[...], xh) + _nt_dot(rwh_ref[...], xl))
    mx = jnp.max(logits, axis=0, keepdims=True)
    ex = jnp.exp(logits - mx)
    probs = ex / jnp.sum(ex, axis=0, keepdims=True)
    e1, e2, w1, w2 = _route(probs, probs + rb_ref[...])
    n = e1.shape[1]
    ei = jnp.concatenate([e1, e2, jnp.zeros((SUBLANES - 2, n), jnp.int32)], axis=0)
    ew = jnp.concatenate([w1, w2, jnp.zeros((SUBLANES - 2, n), F32)], axis=0)
    return xn, ei, ew


def _outproj_kernel(ma_ref, mb_ref, wa_ref, wb_ref, xr_ref, g_ref, b_ref, rwh_ref, rwl_ref, rb_ref,
                    x_ref, ei_ref, ew_ref, *, sub):
    for s in range(x_ref.shape[0] // sub):
        rows = slice(s * sub, (s + 1) * sub)
        v = (DEEPNORM_ALPHA * xr_ref[rows, :] + _dot(ma_ref[rows, :], wa_ref[...])
             + _dot(mb_ref[rows, :], wb_ref[...]))
        xn, ei, ew = _ln_router_tail(v, g_ref, b_ref, rwh_ref, rwl_ref, rb_ref)
        x_ref[rows, :] = xn
        ei_ref[:, rows] = ei
        ew_ref[:, rows] = ew


def _outproj_ln_router(mix_a, mix_b, w_out, x_res, ln_g, ln_b, rwt, rb, tm, sub):
    t = x_res.shape[0]
    rwt_hi = rwt.astype(BF16)
    rwt_lo = (rwt - rwt_hi.astype(F32)).astype(BF16)
    tm = min(tm, t)
    sub = min(sub, tm)
    half = w_out.shape[0] // 2
    row = lambda w: pl.BlockSpec((tm, w), lambda i: (i, 0))
    full = lambda shape: pl.BlockSpec(shape, lambda i: (0,) * len(shape))
    return pl.pallas_call(
        functools.partial(_outproj_kernel, sub=sub),
        grid=(t // tm,),
        in_specs=[row(half), row(half),
                  pl.BlockSpec((half, D_MODEL), lambda i: (0, 0)), pl.BlockSpec((half, D_MODEL), lambda i: (1, 0)),
                  row(D_MODEL), full((1, D_MODEL)), full((1, D_MODEL)),
                  full((N_EXPERTS, D_MODEL)), full((N_EXPERTS, D_MODEL)), full((N_EXPERTS, 1))],
        out_specs=[row(D_MODEL),
                   pl.BlockSpec((SUBLANES, tm), lambda i: (0, i)), pl.BlockSpec((SUBLANES, tm), lambda i: (0, i))],
        out_shape=[jax.ShapeDtypeStruct((t, D_MODEL), F32),
                   jax.ShapeDtypeStruct((SUBLANES, t), jnp.int32), jax.ShapeDtypeStruct((SUBLANES, t), F32)],
        compiler_params=_params(("parallel",)),
        name="outproj_ln_router",
    )(mix_a, mix_b, w_out, w_out, x_res, ln_g, ln_b, rwt_hi, rwt_lo, rb)


def _rank_kernel(e_ref, rank_ref, cnt_ref, base_ref, *, tn):
    @pl.when(pl.program_id(0) == 0)
    def _():
        base_ref[...] = jnp.zeros(base_ref.shape, F32)

    e = e_ref[...]
    rows = lax.broadcasted_iota(jnp.int32, (N_EXPERTS, tn), 0)
    oh1 = rows == e[0:1, :]
    oh2 = rows == e[1:2, :]
    oh = jnp.where(jnp.logical_or(oh1, oh2), 1.0, 0.0)
    r = lax.broadcasted_iota(jnp.int32, (tn, tn), 0)
    c = lax.broadcasted_iota(jnp.int32, (tn, tn), 1)
    earlier = jnp.where(r < c, 1.0, 0.0).astype(BF16)
    base = base_ref[...]
    before = _dot(oh.astype(BF16), earlier) + jnp.tile(base, (1, tn // LANES))
    r1 = jnp.sum(jnp.where(oh1, before, 0.0), axis=0, keepdims=True)
    r2 = jnp.sum(jnp.where(oh2, before, 0.0), axis=0, keepdims=True)
    rank_ref[...] = jnp.concatenate([r1, r2, jnp.zeros((SUBLANES - 2, tn), F32)], axis=0).astype(jnp.int32)
    base = base + jnp.sum(oh, axis=1, keepdims=True)
    base_ref[...] = base
    cnt_ref[...] = base.astype(jnp.int32)


def _rank(eidx, tn):
    t = eidx.shape[1]
    tn = min(tn, t)
    return pl.pallas_call(
        functools.partial(_rank_kernel, tn=tn),
        grid=(t // tn,),
        in_specs=[pl.BlockSpec((SUBLANES, tn), lambda i: (0, i))],
        out_specs=[pl.BlockSpec((SUBLANES, tn), lambda i: (0, i)),
                   pl.BlockSpec((N_EXPERTS, LANES), lambda i: (0, 0))],
        out_shape=[jax.ShapeDtypeStruct((SUBLANES, t), jnp.int32),
                   jax.ShapeDtypeStruct((N_EXPERTS, LANES), jnp.int32)],
        scratch_shapes=[pltpu.VMEM((N_EXPERTS, LANES), F32)],
        compiler_params=_params(("arbitrary",)),
        name="moe_rank",
    )(eidx)


def _moe_kernel(te_ref, nu_ref, xs_ref, wg_ref, wu_ref, wd_ref, y_ref, wgu_s, wd_s):
    tile = pl.program_id(0)
    used = tile < nu_ref[0]
    fresh = jnp.logical_or(tile == 0, te_ref[tile] != te_ref[jnp.maximum(tile - 1, 0)])

    @pl.when(jnp.logical_and(used, fresh))
    def _():
        wgu_s[:, :D_EXPERT] = wg_ref[...].astype(BF16)
        wgu_s[:, D_EXPERT:] = wu_ref[...].astype(BF16)
        wd_s[...] = wd_ref[...].astype(BF16)

    @pl.when(used)
    def _():
        for s in range(xs_ref.shape[0] // MOE_SUB):
            rows = slice(s * MOE_SUB, (s + 1) * MOE_SUB)
            hgu = _dot(xs_ref[rows, :].astype(BF16), wgu_s[...])
            hg = hgu[:, :D_EXPERT]
            hu = hgu[:, D_EXPERT:]
            h = (hg / (1.0 + jnp.exp(-hg))) * hu
            y_ref[rows, :] = _dot(h.astype(BF16), wd_s[...])

    @pl.when(jnp.logical_not(used))
    def _():
        y_ref[...] = jnp.zeros(y_ref.shape, F32)


def _moe(tile_expert, n_used, xs, w_gate, w_up, w_down, tm):
    t_pad = xs.shape[0]
    grid_spec = pltpu.PrefetchScalarGridSpec(
        num_scalar_prefetch=2,
        grid=(t_pad // tm,),
        in_specs=[
            pl.BlockSpec((tm, D_MODEL), lambda i, te, nu: (i, 0)),
            pl.BlockSpec((None, D_MODEL, D_EXPERT), lambda i, te, nu: (te[i], 0, 0)),
            pl.BlockSpec((None, D_MODEL, D_EXPERT), lambda i, te, nu: (te[i], 0, 0)),
            pl.BlockSpec((None, D_EXPERT, D_MODEL), lambda i, te, nu: (te[i], 0, 0)),
        ],
        out_specs=pl.BlockSpec((tm, D_MODEL), lambda i, te, nu: (i, 0)),
        scratch_shapes=[pltpu.VMEM((D_MODEL, 2 * D_EXPERT), BF16), pltpu.VMEM((D_EXPERT, D_MODEL), BF16)],
    )
    return pl.pallas_call(
        _moe_kernel,
        grid_spec=grid_spec,
        out_shape=jax.ShapeDtypeStruct((t_pad, D_MODEL), F32),
        compiler_params=_params(("arbitrary",)),
        name="moe",
    )(tile_expert, n_used, xs, w_gate, w_up, w_down)


def _moe_ln_kernel(x_ref, y1_ref, y2_ref, w_ref, g_ref, b_ref, o_ref):
    w = w_ref[...]
    v = DEEPNORM_ALPHA * x_ref[...] + (w[:, 0:1] * y1_ref[...] + w[:, 1:2] * y2_ref[...])
    o_ref[...] = _layer_norm(v, g_ref[...], b_ref[...])


def _moe_ln(x, y1, y2, w_cols, ln_g, ln_b, tm):
    t = x.shape[0]
    tm = min(tm, t)
    row = pl.BlockSpec((tm, D_MODEL), lambda i: (i, 0))
    vec = pl.BlockSpec((1, D_MODEL), lambda i: (0, 0))
    return pl.pallas_call(
        _moe_ln_kernel,
        grid=(t // tm,),
        in_specs=[row, row, row, pl.BlockSpec((tm, SUBLANES), lambda i: (i, 0)), vec, vec],
        out_specs=row,
        out_shape=jax.ShapeDtypeStruct((t, D_MODEL), F32),
        compiler_params=_params(("parallel",)),
        name="moe_ln",
    )(x, y1, y2, w_cols, ln_g, ln_b)


def _moe_block(x, eidx, ew, w_gate, w_up, w_down, ln_g, ln_b, tm):
    t = x.shape[0]
    tm = min(tm, t)
    rank, cnt = _rank(eidx, 512)
    counts = cnt[:, 0]
    padded = ((counts + tm - 1) // tm) * tm
    ends = jnp.cumsum(padded)
    starts = ends - padded
    e2 = eidx[:2]
    start_of = jnp.zeros_like(e2)
    for k in range(N_EXPERTS):
        start_of = jnp.where(e2 == k, starts[k], start_of)
    dest = start_of + rank[:2]
    t_pad = 2 * t + N_EXPERTS * tm
    tok = jnp.tile(jnp.arange(t, dtype=jnp.int32), 2)
    src_tok = (jnp.arange(t_pad, dtype=jnp.int32) % t).at[dest.reshape(2 * t)].set(
        tok, mode="promise_in_bounds", unique_indices=True)
    tile_start = jnp.arange(t_pad // tm, dtype=jnp.int32) * tm
    tile_expert = jnp.minimum(jnp.sum((ends[None, :] <= tile_start[:, None]).astype(jnp.int32), axis=1),
                              N_EXPERTS - 1)
    n_used = (ends[-1] // tm).astype(jnp.int32).reshape(1)
    xs = x.at[src_tok].get(mode="promise_in_bounds")
    ys = _moe(tile_expert, n_used, xs, w_gate, w_up, w_down, tm)
    y1 = ys.at[dest[0]].get(mode="promise_in_bounds")
    y2 = ys.at[dest[1]].get(mode="promise_in_bounds")
    return _moe_ln(x, y1, y2, ew.T, ln_g, ln_b, tm)


def kernel(x, router_w, router_b, l0_w_in, l0_diff_lam_q1, l0_diff_lam_k1, l0_diff_lam_q2, l0_diff_lam_k2, l0_diff_subln_g, l0_gla_w_gate, l0_gla_b_gate, l0_gla_norm_g, l0_w_out, l0_ln1_g, l0_ln1_b, l0_moe_w_gate, l0_moe_w_up, l0_moe_w_down, l0_ln2_g, l0_ln2_b, l1_w_in, l1_s5_a_re, l1_s5_a_im, l1_s5_log_step, l1_s5_b_re, l1_s5_b_im, l1_s5_c_re, l1_s5_c_im, l1_s5_d, l1_s5_w_glu, l1_s5_b_glu, l1_w_out, l1_ln1_g, l1_ln1_b, l1_moe_w_gate, l1_moe_w_up, l1_moe_w_down, l1_ln2_g, l1_ln2_b):
    bsz, t, d = x.shape
    assert bsz == 1 and d == D_MODEL
    x0 = x.reshape(t, d)
    vec = lambda p: p.reshape(1, -1).astype(F32)
    rwt = router_w.T.astype(F32)
    rb = router_b.reshape(N_EXPERTS, 1).astype(F32)

    lambda_init = 0.8 - 0.6 * math.exp(-0.3 * 0)
    c = [0, 1024, 2048, 3072, 3584, 4096, 5120, 5136, 6160]
    q_scale = math.log2(math.e) * DIFF_HEAD_DIM ** -0.5
    w0 = jnp.concatenate([l0_w_in[:, :c[1]] * q_scale, l0_w_in[:, c[1]:c[6]], l0_w_in[:, c[7]:c[8]],
                          l0_w_in[:, c[6]:c[7]], jnp.zeros((d, LANES - GLA_GATE_RANK), l0_w_in.dtype)],
                         axis=1).astype(BF16)
    h0 = _matmul(x0, w0, 1024, 896, BF16)
    lam_vec = jnp.stack([l0_diff_lam_q1, l0_diff_lam_k1, l0_diff_lam_q2, l0_diff_lam_k2]).astype(F32)
    o_a = _diff_attention(h0, lam_vec, vec(l0_diff_subln_g), lambda_init, 1024)
    wg_pad = jnp.concatenate([l0_gla_w_gate, jnp.zeros((LANES - GLA_GATE_RANK, GLA_HEADS * GLA_DK), F32)],
                             axis=0).astype(BF16)
    o_b = _gla(h0, wg_pad, vec(l0_gla_b_gate), vec(l0_gla_norm_g), 256)
    x1, eidx, ew = _outproj_ln_router(o_a, o_b, l0_w_out.astype(BF16), x0, vec(l0_ln1_g), vec(l0_ln1_b),
                                      rwt, rb, 512, 256)
    x2 = _moe_block(x1, eidx, ew, l0_moe_w_gate, l0_moe_w_up, l0_moe_w_down, vec(l0_ln2_g), vec(l0_ln2_b), 256)

    h1 = _matmul(x2, l1_w_in.astype(BF16), 1024, 1024, BF16)
    o_c = _stick_breaking(h1, 256)
    lbr, lbi, bbr, bbi = _s5_prep(l1_s5_a_re, l1_s5_a_im, l1_s5_log_step, l1_s5_b_re, l1_s5_b_im)
    bd, cd, pw8 = _s5_tile_params(lbr[::S5_GROUP], lbi[::S5_GROUP], bbr.reshape(S5_GROUPS, S5_GROUP, S5_STATE),
                                  bbi.reshape(S5_GROUPS, S5_GROUP, S5_STATE),
                                  l1_s5_c_re.astype(F32), l1_s5_c_im.astype(F32))
    y_s5 = _s5_core(h1, bd, cd, pw8, 256)
    o_d = _s5_glu(y_s5, h1, vec(l1_s5_d), l1_s5_w_glu.astype(BF16), vec(l1_s5_b_glu), 512)
    x3, eidx1, ew1 = _outproj_ln_router(o_c, o_d, l1_w_out.astype(BF16), x2, vec(l1_ln1_g), vec(l1_ln1_b),
                                        rwt, rb, 512, 256)
    x4 = _moe_block(x3, eidx1, ew1, l1_moe_w_gate, l1_moe_w_up, l1_moe_w_down, vec(l1_ln2_g), vec(l1_ln2_b), 256)
    return x4.reshape(bsz, t, d)
```

```python
import functools
import math

import jax
import jax.numpy as jnp
from jax import lax
from jax.experimental import pallas as pl
from jax.experimental.pallas import tpu as pltpu

F32 = jnp.float32
BF16 = jnp.bfloat16

D_MODEL = 2048
DEPTH = 2
DIFF_HEADS = 8
DIFF_HEAD_DIM = 64
DIFF_V_DIM = 128
GLA_HEADS = 4
GLA_DK = 128
GLA_DV = 256
GLA_GATE_RANK = 16
GLA_GATE_TEMP = 16.0
GLA_CHUNK = 64
SB_HEADS = 8
SB_HEAD_DIM = 128
S5_CHANNELS = 1024
S5_GROUP = 16
S5_GROUPS = 64
S5_STATE = 64
N_EXPERTS = 16
N_GROUPS = 4
EXPERTS_PER_GROUP = 4
D_EXPERT = 640
DEEPNORM_ALPHA = (2.0 * DEPTH) ** 0.25
LN_EPS = 1e-5

LANES = 128
SUBLANES = 8
VMEM_LIMIT = 56 * 1024 * 1024
SB_LOG_FLOOR = 120.0
SB_HEADS_PER_STEP = 4
MOE_SUB = 256

MXU_N = 256
L0_GB_PAD = MXU_N
L0_IN_TN = 5 * MXU_N

S5_KT = 4
S5_KT_CH = S5_CHANNELS // S5_KT
S5_KT_STATES = (S5_GROUPS // S5_KT) * S5_STATE
S5_SCAN_W = 512
S5_L = 8


def _params(sem, flags=None):
    return pltpu.CompilerParams(dimension_semantics=sem, vmem_limit_bytes=VMEM_LIMIT, flags=flags)


def _nt_dot(a, b):
    return lax.dot_general(a, b, (((1,), (1,)), ((), ())), preferred_element_type=F32)


def _dot(a, b):
    return jnp.dot(a, b, preferred_element_type=F32)


def _split_dot_left(m, x):
    hi = x.astype(BF16)
    lo = (x - hi.astype(F32)).astype(BF16)
    return _dot(m, hi) + _dot(m, lo)


def _layer_norm(v, g, b):
    mu = jnp.mean(v, axis=-1, keepdims=True)
    d = v - mu
    var = jnp.mean(d * d, axis=-1, keepdims=True)
    return d * lax.rsqrt(var + LN_EPS) * g + b


def _mm_kernel(a_ref, b_ref, o_ref, a_bf):
    @pl.when(pl.program_id(1) == 0)
    def _():
        a_bf[...] = a_ref[...].astype(BF16)

    o_ref[...] = _dot(a_bf[...], b_ref[...]).astype(o_ref.dtype)


def _matmul(a, b, tm, tn, out_dtype):
    m, k = a.shape
    n = b.shape[1]
    tm = min(tm, m)
    return pl.pallas_call(
        _mm_kernel,
        grid=(m // tm, n // tn),
        in_specs=[pl.BlockSpec((tm, k), lambda i, j: (i, 0)),
                  pl.BlockSpec((k, tn), lambda i, j: (0, j))],
        out_specs=pl.BlockSpec((tm, tn), lambda i, j: (i, j)),
        out_shape=jax.ShapeDtypeStruct((m, n), out_dtype),
        scratch_shapes=[pltpu.VMEM((tm, k), BF16)],
        compiler_params=_params(("parallel", "arbitrary")),
        name="in_proj",
    )(a, b)


def _diff_attn_kernel(lam_ref, g_ref, q_ref, k_ref, v_ref, o_ref, vx_ref, m_ref, acc_ref, sa_ref, sb_ref, *, tq,
                      lambda_init):
    i = pl.program_id(1)
    dv = DIFF_V_DIM

    @pl.when(i == 0)
    def _():
        vx_ref[:, :dv] = v_ref[...]
        vx_ref[:, dv:] = jnp.ones((vx_ref.shape[0], dv), BF16)

    m_ref[...] = jnp.full(m_ref.shape, -jnp.inf, F32)
    acc_ref[...] = jnp.zeros(acc_ref.shape, F32)
    q = q_ref[...]
    lane = lax.broadcasted_iota(jnp.int32, q.shape, 1)
    zero = jnp.zeros_like(q)
    halves = (jnp.where(lane < DIFF_HEAD_DIM, q, zero), jnp.where(lane >= DIFF_HEAD_DIM, q, zero))

    def scores(j, mi):
        off = pl.multiple_of(j * tq, tq)
        return _nt_dot(halves[mi], k_ref[pl.ds(off, tq), :])

    def softmax_pv(j, get_s, masked):
        off = pl.multiple_of(j * tq, tq)
        vx = vx_ref[pl.ds(off, tq), :]
        if masked:
            row = lax.broadcasted_iota(jnp.int32, (tq, tq), 0)
            col = lax.broadcasted_iota(jnp.int32, (tq, tq), 1)
            causal = col <= row
        for mi in range(2):
            s = get_s(mi)
            if masked:
                s = jnp.where(causal, s, -jnp.inf)
            m_prev = m_ref[mi]
            m_new = jnp.maximum(m_prev, jnp.max(s, axis=1, keepdims=True))
            p = jnp.exp2(s - jnp.tile(m_new, (1, tq // LANES)))
            alpha = jnp.exp2(m_prev - m_new)
            acc_ref[mi] = jnp.tile(alpha, (1, 2)) * acc_ref[mi] + _dot(p.astype(BF16), vx)
            m_ref[mi] = m_new

    def scores_to(j, s_ref):
        for mi in range(2):
            s_ref[mi] = scores(j, mi)

    scores_to(0, sa_ref)

    def pair(t, carry):
        j = 2 * t
        scores_to(j + 1, sb_ref)
        softmax_pv(j, lambda mi: sa_ref[mi], False)
        scores_to(j + 2, sa_ref)
        softmax_pv(j + 1, lambda mi: sb_ref[mi], False)
        return carry

    lax.fori_loop(0, i // 2, pair, 0)

    @pl.when(i % 2 == 1)
    def _():
        softmax_pv(i - 1, lambda mi: sa_ref[mi], False)
        scores_to(i, sa_ref)

    softmax_pv(i, lambda mi: sa_ref[mi], True)

    lv = lam_ref[...]
    lam = (jnp.exp(jnp.sum(lv[0:1] * lv[1:2], axis=1, keepdims=True))
           - jnp.exp(jnp.sum(lv[2:3] * lv[3:4], axis=1, keepdims=True)) + lambda_init)
    a0 = acc_ref[0]
    a1 = acc_ref[1]
    o = a0[:, :dv] / a0[:, dv:] - lam * (a1[:, :dv] / a1[:, dv:])
    ms = jnp.mean(o * o, axis=-1, keepdims=True)
    o = o * lax.rsqrt(ms + LN_EPS) * g_ref[...] * (1.0 - lambda_init)
    o_ref[...] = o.astype(o_ref.dtype)


def _diff_attention(h0, lam_vec, subln_g, lambda_init, tq):
    t = h0.shape[0]
    tq = min(tq, t)
    nq = t // tq
    kern = functools.partial(_diff_attn_kernel, tq=tq, lambda_init=lambda_init)
    return pl.pallas_call(
        kern,
        grid=(DIFF_HEADS, nq),
        in_specs=[
            pl.BlockSpec((4, DIFF_HEAD_DIM), lambda h, i: (0, 0)),
            pl.BlockSpec((1, DIFF_V_DIM), lambda h, i: (0, 0)),
            pl.BlockSpec((tq, LANES), lambda h, i: (i, h)),
            pl.BlockSpec((t, LANES), lambda h, i: (0, DIFF_HEADS + h)),
            pl.BlockSpec((t, LANES), lambda h, i: (0, 2 * DIFF_HEADS + h)),
        ],
        out_specs=pl.BlockSpec((tq, LANES), lambda h, i: (i, h)),
        out_shape=jax.ShapeDtypeStruct((t, DIFF_HEADS * DIFF_V_DIM), BF16),
        scratch_shapes=[pltpu.VMEM((t, 2 * DIFF_V_DIM), BF16), pltpu.VMEM((2, tq, LANES), F32),
                        pltpu.VMEM((2, tq, 2 * DIFF_V_DIM), F32),
                        pltpu.VMEM((2, tq, tq), F32), pltpu.VMEM((2, tq, tq), F32)],
        compiler_params=_params(("arbitrary", "arbitrary")),
        name="diff_attn",
    )(lam_vec, subln_g, h0, h0, h0)


def _gla_kernel(q_ref, k_ref, v_ref, r_ref, gb_ref, wg_ref, bg_ref, ng_ref, o_ref, st_ref, *, tg):
    @pl.when(pl.program_id(0) == 0)
    def _():
        st_ref[...] = jnp.zeros(st_ref.shape, F32)

    c = GLA_CHUNK
    gate = _dot(gb_ref[...], wg_ref[...]) + bg_ref[...]
    log_a = -(jnp.maximum(-gate, 0.0) + jnp.log(1.0 + jnp.exp(-jnp.abs(gate)))) / GLA_GATE_TEMP
    row = lax.broadcasted_iota(jnp.int32, (c, c), 0)
    col = lax.broadcasted_iota(jnp.int32, (c, c), 1)
    tril = row >= col
    tri = jnp.where(tril, 1.0, 0.0).astype(BF16)
    for h in range(GLA_HEADS):
        ksl = slice(h * GLA_DK, (h + 1) * GLA_DK)
        vsl = slice(h * GLA_DV, (h + 1) * GLA_DV)
        for ci in range(tg // c):
            rs = slice(ci * c, (ci + 1) * c)
            la = log_a[rs, ksl]
            b = _split_dot_left(tri, la)
            b_last = b[c - 1:c, :]
            qf = q_ref[rs, ksl].astype(F32) * (GLA_DK ** -0.5)
            kf = k_ref[rs, ksl].astype(F32)
            v = v_ref[rs, vsl]
            q_dec = (qf * jnp.exp(b)).astype(BF16)
            k_dec = (kf * jnp.exp(-b)).astype(BF16)
            k_end = (kf * jnp.exp(b_last - b)).astype(BF16)
            scores = jnp.where(tril, _nt_dot(q_dec, k_dec), 0.0)
            st = st_ref[h]
            o = _dot(scores.astype(BF16), v) + _nt_dot(q_dec, st.astype(BF16))
            kv_t = lax.dot_general(v, k_end, (((0,), (0,)), ((), ())), preferred_element_type=F32)
            st_ref[h] = jnp.exp(b_last) * st + kv_t
            ms = jnp.mean(o * o, axis=-1, keepdims=True)
            o = o * lax.rsqrt(ms + LN_EPS) * ng_ref[...]
            r = r_ref[rs, vsl].astype(F32)
            o_ref[rs, vsl] = (o * (r / (1.0 + jnp.exp(-r)))).astype(o_ref.dtype)


def _gla(h0, wg_pad, b_gate, norm_g, tg):
    t = h0.shape[0]
    tg = min(tg, t)
    qk = GLA_HEADS * GLA_DK
    vw = GLA_HEADS * GLA_DV
    return pl.pallas_call(
        functools.partial(_gla_kernel, tg=tg),
        grid=(t // tg,),
        in_specs=[
            pl.BlockSpec((tg, qk), lambda i: (i, 3072 // qk)),
            pl.BlockSpec((tg, qk), lambda i: (i, 3584 // qk)),
            pl.BlockSpec((tg, vw), lambda i: (i, 4096 // vw)),
            pl.BlockSpec((tg, vw), lambda i: (i, 5120 // vw)),
            pl.BlockSpec((tg, LANES), lambda i: (i, 6144 // LANES)),
            pl.BlockSpec((LANES, qk), lambda i: (0, 0)),
            pl.BlockSpec((1, qk), lambda i: (0, 0)),
            pl.BlockSpec((1, GLA_DV), lambda i: (0, 0)),
        ],
        out_specs=pl.BlockSpec((tg, vw), lambda i: (i, 0)),
        out_shape=jax.ShapeDtypeStruct((t, vw), BF16),
        scratch_shapes=[pltpu.VMEM((GLA_HEADS, GLA_DV, GLA_DK), F32)],
        compiler_params=_params(("arbitrary",)),
        name="gla",
    )(h0, h0, h0, h0, h0, wg_pad, b_gate, norm_g)


def _sb_kernel(q_ref, k_ref, v_ref, o_ref, acc_ref, c_ref, *, tq):
    i = pl.program_id(1)
    hd = SB_HEAD_DIM
    acc_ref[...] = jnp.zeros(acc_ref.shape, F32)
    c_ref[...] = jnp.zeros(c_ref.shape, F32)
    row = lax.broadcasted_iota(jnp.int32, (tq, tq), 0)
    col = lax.broadcasted_iota(jnp.int32, (tq, tq), 1)
    strict = col < row
    tri = jnp.where(row > col, 1.0, 0.0).astype(BF16)

    def block(jb, masked):
        off = pl.multiple_of(jb * tq, tq)
        cmax = None
        for hh in range(SB_HEADS_PER_STEP):
            cols = slice(hh * hd, (hh + 1) * hd)
            k = k_ref[pl.ds(off, tq), cols]
            v = v_ref[pl.ds(off, tq), cols]
            z = _nt_dot(q_ref[:, cols], k) * (SB_HEAD_DIM ** -0.5)
            lsn = -(jnp.maximum(z, 0.0) + jnp.log(1.0 + jnp.exp(-jnp.abs(z))))
            lm = jnp.where(strict, lsn, 0.0) if masked else lsn
            excl = _dot(lm.astype(BF16), tri) + c_ref[hh]
            w = jnp.exp(z + lsn + excl)
            if masked:
                w = jnp.where(strict, w, 0.0)
            acc_ref[:, cols] += _dot(w.astype(BF16), v)
            c_new = c_ref[hh] + jnp.sum(lm, axis=1, keepdims=True)
            c_ref[hh] = c_new
            hmax = jnp.max(c_new)
            cmax = hmax if cmax is None else jnp.maximum(cmax, hmax)
        return cmax

    cmax = block(i, True)

    def cond(carry):
        jb, cm = carry
        return jnp.logical_and(jb >= 0, cm > -SB_LOG_FLOOR)

    def body(carry):
        jb, _ = carry
        return jb - 1, block(jb, False)

    lax.while_loop(cond, body, (i - 1, cmax))
    o_ref[...] = acc_ref[...].astype(o_ref.dtype)


def _stick_breaking(h1, tq):
    t = h1.shape[0]
    tq = min(tq, t)
    ng = SB_HEADS // SB_HEADS_PER_STEP
    wd = SB_HEADS_PER_STEP * SB_HEAD_DIM
    return pl.pallas_call(
        functools.partial(_sb_kernel, tq=tq),
        grid=(ng, t // tq),
        in_specs=[
            pl.BlockSpec((tq, wd), lambda h, i: (i, h)),
            pl.BlockSpec((t, wd), lambda h, i: (0, ng + h), pipeline_mode=pl.Buffered(1)),
            pl.BlockSpec((t, wd), lambda h, i: (0, 2 * ng + h), pipeline_mode=pl.Buffered(1)),
        ],
        out_specs=pl.BlockSpec((tq, wd), lambda h, i: (i, h)),
        out_shape=jax.ShapeDtypeStruct((t, SB_HEADS * SB_HEAD_DIM), BF16),
        scratch_shapes=[pltpu.VMEM((tq, wd), F32), pltpu.VMEM((SB_HEADS_PER_STEP, tq, 1), F32)],
        compiler_params=_params(("parallel", "arbitrary")),
        name="stick_breaking",
    )(h1, h1, h1)


def _s5_prep_kernel(are_ref, aim_ref, ls_ref, bre_ref, bim_ref, lbr_ref, lbi_ref, bbr_ref, bbi_ref):
    step = jnp.exp(ls_ref[...])
    lam_re = are_ref[...]
    lam_im = aim_ref[...]
    mag = jnp.exp(step * lam_re)
    lb_re = mag * jnp.cos(step * lam_im)
    lb_im = mag * jnp.sin(step * lam_im)
    den = lam_re * lam_re + lam_im * lam_im
    n_re = lb_re - 1.0
    f_re = (n_re * lam_re + lb_im * lam_im) / den
    f_im = (lb_im * lam_re - n_re * lam_im) / den
    lbr_ref[...] = lb_re
    lbi_ref[...] = lb_im
    bbr_ref[...] = f_re * bre_ref[...] - f_im * bim_ref[...]
    bbi_ref[...] = f_re * bim_ref[...] + f_im * bre_ref[...]


def _s5_prep(a_re, a_im, log_step, b_re, b_im):
    rows = S5_GROUPS * S5_GROUP
    rep = lambda p: jnp.repeat(p, S5_GROUP, axis=0)
    to_rows = lambda b: jnp.transpose(b, (0, 2, 1)).reshape(rows, S5_STATE)
    spec = pl.BlockSpec((rows, S5_STATE), lambda: (0, 0))
    shp = jax.ShapeDtypeStruct((rows, S5_STATE), F32)
    return pl.pallas_call(
        _s5_prep_kernel,
        in_specs=[spec, spec, pl.BlockSpec((rows, 1), lambda: (0, 0)), spec, spec],
        out_specs=[spec] * 4,
        out_shape=[shp] * 4,
        name="s5_prep",
    )(rep(a_re), rep(a_im), rep(log_step[:, None]), to_rows(b_re), to_rows(b_im))


def _cmul(ar, ai, br, bi):
    return ar * br - ai * bi, ar * bi + ai * br


def _block_diag_tiles(m, rows_per_group, cols_per_group):
    gpt = S5_GROUPS // S5_KT
    m = m.reshape(S5_KT, gpt, rows_per_group, cols_per_group)
    eye = jnp.eye(gpt, dtype=m.dtype)
    bd = jnp.einsum("kgrc,gh->kgrhc", m, eye)
    return bd.reshape(S5_KT, gpt * rows_per_group, gpt * cols_per_group)


def _s5_tile_params(lb_re, lb_im, bb_re, bb_im, c_re, c_im):
    bd = jnp.concatenate([_block_diag_tiles(bb_re, S5_GROUP, S5_STATE),
                          _block_diag_tiles(bb_im, S5_GROUP, S5_STATE)], axis=2)
    cd = jnp.concatenate([_block_diag_tiles(jnp.transpose(c_re, (0, 2, 1)), S5_STATE, S5_GROUP),
                          _block_diag_tiles(-jnp.transpose(c_im, (0, 2, 1)), S5_STATE, S5_GROUP)], axis=1)
    pr = [jnp.ones_like(lb_re)]
    pi = [jnp.zeros_like(lb_re)]
    for _ in range(S5_L):
        nr, ni = _cmul(pr[-1], pi[-1], lb_re, lb_im)
        pr.append(nr)
        pi.append(ni)

    def table(p):
        rows = jnp.transpose(jnp.stack(p).reshape(S5_L + 1, S5_KT, S5_KT_STATES), (1, 0, 2))
        return jnp.pad(rows, ((0, 0), (0, LANES - (S5_L + 1)), (0, 0)))

    return bd, cd, jnp.stack([table(pr), table(pi)], axis=1)


def _s5_core_kernel(u_ref, bd_ref, cd_ref, pw8_ref, y_ref, hs_ref, hp_ref, pw_ref, st_ref, uf_ref,
                    mt_s, bbs_s, ccs_s, *, tcr):
    ns = S5_KT_STATES
    pad = SUBLANES
    cw = S5_KT_CH

    @pl.when(pl.program_id(1) == 0)
    def _():
        b_re = bd_ref[:, :ns]
        b_im = bd_ref[:, ns:]
        cd_hi = cd_ref[...].astype(BF16)
        cd_lo = (cd_ref[...] - cd_hi.astype(F32)).astype(BF16)
        for k in range(S5_L):
            a = pw8_ref[0, k:k + 1, :]
            b = pw8_ref[1, k:k + 1, :]
            dk = jnp.concatenate([b_re * a - b_im * b, b_re * b + b_im * a], axis=1)
            s = S5_L - 1 - k
            bbs_s[s * cw:(s + 1) * cw, :] = dk.astype(BF16)
            dk_hi = dk.astype(BF16)
            dk_lo = (dk - dk_hi.astype(F32)).astype(BF16)
            kk = (_dot(dk_hi, cd_hi) + (_dot(dk_lo, cd_hi) + _dot(dk_hi, cd_lo))).astype(BF16)
            for s0 in range(S5_L - k):
                mt_s[s0 * cw:(s0 + 1) * cw, (s0 + k) * cw:(s0 + k + 1) * cw] = kk
        pw_t = (jnp.transpose(pw8_ref[0]), jnp.transpose(pw8_ref[1]))
        c_re = cd_ref[:ns, :]
        c_im = cd_ref[ns:, :]
        for i in range(S5_L):
            a = pw_t[0][:, i + 1:i + 2]
            b = pw_t[1][:, i + 1:i + 2]
            ccs_s[:ns, i * cw:(i + 1) * cw] = (c_re * a + c_im * b).astype(BF16)
            ccs_s[ns:, i * cw:(i + 1) * cw] = (c_im * a - c_re * b).astype(BF16)

        hp_ref[...] = jnp.zeros(hp_ref.shape, F32)
        hs_ref[0:pad, :] = jnp.zeros((pad, 2 * ns), F32)
        l1r = jnp.broadcast_to(pw8_ref[0, S5_L:S5_L + 1, :], (SUBLANES, ns))
        l1i = jnp.broadcast_to(pw8_ref[1, S5_L:S5_L + 1, :], (SUBLANES, ns))
        l2r, l2i = _cmul(l1r, l1i, l1r, l1i)
        l4r, l4i = _cmul(l2r, l2i, l2r, l2i)
        l8r, l8i = _cmul(l4r, l4i, l4r, l4i)
        ridx = lax.broadcasted_iota(jnp.int32, l1r.shape, 0)
        n = ridx + 1
        pr = jnp.ones_like(l1r)
        pi = jnp.zeros_like(l1r)
        for bit, (lr, li) in ((1, (l1r, l1i)), (2, (l2r, l2i)), (4, (l4r, l4i)), (8, (l8r, l8i))):
            nr, ni = _cmul(pr, pi, lr, li)
            take = (n & bit) != 0
            pr = jnp.where(take, nr, pr)
            pi = jnp.where(take, ni, pi)
        pw_ref[0] = pr
        pw_ref[1] = pi
        for s, (lr, li) in enumerate(((l1r, l1i), (l2r, l2i), (l4r, l4i))):
            keep = ridx >= (1 << s)
            st_ref[2 * s] = jnp.where(keep, lr, 0.0)
            st_ref[2 * s + 1] = jnp.where(keep, li, 0.0)

    nl = S5_KT_CH // LANES
    for h in range(nl):
        uf_ref[h] = u_ref[:, h * LANES:(h + 1) * LANES].astype(F32)
    uc = jnp.concatenate([uf_ref[h, pl.ds(s, tcr, stride=S5_L), :].astype(BF16)
                          for s in range(S5_L) for h in range(nl)], axis=1)
    hs_ref[pad:pad + tcr, :] = _dot(uc, bbs_s[...])
    w = S5_SCAN_W
    for cc in range(ns // w):
        re0 = cc * w
        im0 = ns + cc * w

        def body(r, carry, re0=re0, im0=im0):
            hpr, hpi = carry
            rows = pl.ds(pl.multiple_of(pad + r * SUBLANES, SUBLANES), SUBLANES)
            xr = hs_ref[rows, re0:re0 + w]
            xi = hs_ref[rows, im0:im0 + w]
            for s in range(3):
                sr = pltpu.roll(xr, 1 << s, 0)
                si = pltpu.roll(xi, 1 << s, 0)
                ar = st_ref[2 * s, :, re0:re0 + w]
                ai = st_ref[2 * s + 1, :, re0:re0 + w]
                xr, xi = xr + ar * sr - ai * si, xi + ar * si + ai * sr
            pr = pw_ref[0, :, re0:re0 + w]
            pi = pw_ref[1, :, re0:re0 + w]
            xr, xi = xr + pr * hpr - pi * hpi, xi + pr * hpi + pi * hpr
            hs_ref[rows, re0:re0 + w] = xr
            hs_ref[rows, im0:im0 + w] = xi
            return (jnp.broadcast_to(xr[SUBLANES - 1:SUBLANES, :], (SUBLANES, w)),
                    jnp.broadcast_to(xi[SUBLANES - 1:SUBLANES, :], (SUBLANES, w)))

        hpr, hpi = lax.fori_loop(0, tcr // SUBLANES, body, (hp_ref[:, re0:re0 + w], hp_ref[:, im0:im0 + w]))
        hp_ref[:, re0:re0 + w] = hpr
        hp_ref[:, im0:im0 + w] = hpi

    h_prev = hs_ref[pad - 1:pad - 1 + tcr, :]
    y_in = _dot(h_prev.astype(BF16), ccs_s[...])
    for i in range(S5_L):
        k_hi = (i + 1) * cw
        y_i = y_in[:, i * cw:k_hi] + _dot(uc[:, :k_hi], mt_s[:k_hi, i * cw:k_hi])
        for h in range(nl):
            uf_ref[h, pl.ds(i, tcr, stride=S5_L), :] = y_i[:, h * LANES:(h + 1) * LANES]
    for h in range(nl):
        y_ref[:, h * LANES:(h + 1) * LANES] = uf_ref[h]
    hs_ref[pad - 1:pad, :] = hs_ref[pad + tcr - 1:pad + tcr, :]


def _s5_core(h1, bd, cd, pw8, tcr):
    t = h1.shape[0]
    width = S5_L * S5_KT_CH
    tcr = min(tcr, t // S5_L)
    tc = tcr * S5_L
    ns = S5_KT_STATES
    u_col0 = 3 * S5_CHANNELS // S5_KT_CH
    once = pl.Buffered(1)
    return pl.pallas_call(
        functools.partial(_s5_core_kernel, tcr=tcr),
        grid=(S5_KT, t // tc),
        in_specs=[
            pl.BlockSpec((tc, S5_KT_CH), lambda k, i: (i, u_col0 + k)),
            pl.BlockSpec((None, S5_KT_CH, 2 * ns), lambda k, i: (k, 0, 0), pipeline_mode=once),
            pl.BlockSpec((None, 2 * ns, S5_KT_CH), lambda k, i: (k, 0, 0), pipeline_mode=once),
            pl.BlockSpec((None, 2, LANES, ns), lambda k, i: (k, 0, 0, 0), pipeline_mode=once),
        ],
        out_specs=pl.BlockSpec((tc, S5_KT_CH), lambda k, i: (i, k)),
        out_shape=jax.ShapeDtypeStruct((t, S5_CHANNELS), F32),
        scratch_shapes=[pltpu.VMEM((tcr + SUBLANES, 2 * ns), F32), pltpu.VMEM((SUBLANES, 2 * ns), F32),
                        pltpu.VMEM((2, SUBLANES, ns), F32), pltpu.VMEM((6, SUBLANES, ns), F32),
                        pltpu.VMEM((S5_KT_CH // LANES, tc, LANES), F32),
                        pltpu.VMEM((width, width), BF16), pltpu.VMEM((width, 2 * ns), BF16),
                        pltpu.VMEM((2 * ns, width), BF16)],
        compiler_params=_params(("arbitrary", "arbitrary")),
        name="s5_core",
    )(h1, bd, cd, pw8)


def _s5_glu_kernel(y_ref, u_ref, d_ref, wglu_ref, bglu_ref, o_ref):
    y = y_ref[...] + d_ref[...] * u_ref[...].astype(F32)
    g = 0.5 * y * (1.0 + jnp.tanh(math.sqrt(2.0 / math.pi) * (y + 0.044715 * (y * y * y))))
    gl = _dot(g.astype(BF16), wglu_ref[...]) + bglu_ref[...]
    o_ref[...] = (g / (1.0 + jnp.exp(-gl))).astype(o_ref.dtype)


def _s5_glu(y, h1, d_skip, w_glu, b_glu, tm):
    t = h1.shape[0]
    tm = min(tm, t)
    full = lambda shape: pl.BlockSpec(shape, lambda i: (0,) * len(shape))
    return pl.pallas_call(
        _s5_glu_kernel,
        grid=(t // tm,),
        in_specs=[pl.BlockSpec((tm, S5_CHANNELS), lambda i: (i, 0)),
                  pl.BlockSpec((tm, S5_CHANNELS), lambda i: (i, 3)),
                  full((1, S5_CHANNELS)), full((S5_CHANNELS, S5_CHANNELS)), full((1, S5_CHANNELS))],
        out_specs=pl.BlockSpec((tm, S5_CHANNELS), lambda i: (i, 0)),
        out_shape=jax.ShapeDtypeStruct((t, S5_CHANNELS), BF16),
        compiler_params=_params(("parallel",)),
        name="s5_glu",
    )(y, h1, d_skip, w_glu, b_glu)


def _route(probs, sel):
    rows = [sel[e:e + 1, :] for e in range(N_EXPERTS)]
    prow = [probs[e:e + 1, :] for e in range(N_EXPERTS)]
    best_score = None
    for g in range(N_GROUPS):
        a = rows[g * EXPERTS_PER_GROUP:(g + 1) * EXPERTS_PER_GROUP]
        score = None
        for x in range(EXPERTS_PER_GROUP):
            for y in range(x + 1, EXPERTS_PER_GROUP):
                pair = a[x] + a[y]
                score = pair if score is None else jnp.maximum(score, pair)
        if best_score is None:
            best_score, gbest = score, jnp.zeros(score.shape, jnp.int32)
            cs = list(a)
            cp = prow[0:EXPERTS_PER_GROUP]
        else:
            better = score > best_score
            best_score = jnp.where(better, score, best_score)
            gbest = jnp.where(better, g, gbest)
            cs = [jnp.where(better, a[x], cs[x]) for x in range(EXPERTS_PER_GROUP)]
            cp = [jnp.where(better, prow[g * EXPERTS_PER_GROUP + x], cp[x]) for x in range(EXPERTS_PER_GROUP)]

    def first_argmax(vals):
        bv, bi = vals[0], jnp.zeros(vals[0].shape, jnp.int32)
        for x in range(1, len(vals)):
            better = vals[x] > bv
            bv = jnp.where(better, vals[x], bv)
            bi = jnp.where(better, x, bi)
        return bi

    i1 = first_argmax(cs)
    i2 = first_argmax([jnp.where(i1 == x, -jnp.inf, cs[x]) for x in range(EXPERTS_PER_GROUP)])

    def pick(idx):
        out = cp[0]
        for x in range(1, EXPERTS_PER_GROUP):
            out = jnp.where(idx == x, cp[x], out)
        return out

    p1, p2 = pick(i1), pick(i2)
    tot = p1 + p2
    return gbest * EXPERTS_PER_GROUP + i1, gbest * EXPERTS_PER_GROUP + i2, p1 / tot, p2 / tot


def _ln_router_tail(v, g_ref, b_ref, rwh_ref, rwl_ref, rb_ref):
    xn = _layer_norm(v, g_ref[...], b_ref[...])
    xh = xn.astype(BF16)
    xl = (xn - xh.astype(F32)).astype(BF16)
    logits = _nt_dot(rwh_ref[...], xh) + (_nt_dot(rwl_ref[...], xh) + _nt_dot(rwh_ref[...], xl))
    mx = jnp.max(logits, axis=0, keepdims=True)
    ex = jnp.exp(logits - mx)
    probs = ex / jnp.sum(ex, axis=0, keepdims=True)
    e1, e2, w1, w2 = _route(probs, probs + rb_ref[...])
    n = e1.shape[1]
    ei = jnp.concatenate([e1, e2, jnp.zeros((SUBLANES - 2, n), jnp.int32)], axis=0)
    ew = jnp.concatenate([w1, w2, jnp.zeros((SUBLANES - 2, n), F32)], axis=0)
    return xn, ei, ew


def _outproj_kernel(ma_ref, mb_ref, wa_ref, wb_ref, xr_ref, g_ref, b_ref, rwh_ref, rwl_ref, rb_ref,
                    x_ref, ei_ref, ew_ref, *, sub):
    for s in range(x_ref.shape[0] // sub):
        rows = slice(s * sub, (s + 1) * sub)
        v = (DEEPNORM_ALPHA * xr_ref[rows, :] + _dot(ma_ref[rows, :], wa_ref[...])
             + _dot(mb_ref[rows, :], wb_ref[...]))
        xn, ei, ew = _ln_router_tail(v, g_ref, b_ref, rwh_ref, rwl_ref, rb_ref)
        x_ref[rows, :] = xn
        ei_ref[:, rows] = ei
        ew_ref[:, rows] = ew


def _outproj_ln_router(mix_a, mix_b, w_out, x_res, ln_g, ln_b, rwt, rb, tm, sub):
    t = x_res.shape[0]
    rwt_hi = rwt.astype(BF16)
    rwt_lo = (rwt - rwt_hi.astype(F32)).astype(BF16)
    tm = min(tm, t)
    sub = min(sub, tm)
    half = w_out.shape[0] // 2
    row = lambda w: pl.BlockSpec((tm, w), lambda i: (i, 0))
    full = lambda shape: pl.BlockSpec(shape, lambda i: (0,) * len(shape))
    return pl.pallas_call(
        functools.partial(_outproj_kernel, sub=sub),
        grid=(t // tm,),
        in_specs=[row(half), row(half),
                  pl.BlockSpec((half, D_MODEL), lambda i: (0, 0)), pl.BlockSpec((half, D_MODEL), lambda i: (1, 0)),
                  row(D_MODEL), full((1, D_MODEL)), full((1, D_MODEL)),
                  full((N_EXPERTS, D_MODEL)), full((N_EXPERTS, D_MODEL)), full((N_EXPERTS, 1))],
        out_specs=[row(D_MODEL),
                   pl.BlockSpec((SUBLANES, tm), lambda i: (0, i)), pl.BlockSpec((SUBLANES, tm), lambda i: (0, i))],
        out_shape=[jax.ShapeDtypeStruct((t, D_MODEL), F32),
                   jax.ShapeDtypeStruct((SUBLANES, t), jnp.int32), jax.ShapeDtypeStruct((SUBLANES, t), F32)],
        compiler_params=_params(("parallel",)),
        name="outproj_ln_router",
    )(mix_a, mix_b, w_out, w_out, x_res, ln_g, ln_b, rwt_hi, rwt_lo, rb)


def _rank_kernel(e_ref, rank_ref, cnt_ref, base_ref, *, tn):
    @pl.when(pl.program_id(0) == 0)
    def _():
        base_ref[...] = jnp.zeros(base_ref.shape, F32)

    e = e_ref[...]
    rows = lax.broadcasted_iota(jnp.int32, (N_EXPERTS, tn), 0)
    oh1 = rows == e[0:1, :]
    oh2 = rows == e[1:2, :]
    oh = jnp.where(jnp.logical_or(oh1, oh2), 1.0, 0.0)
    r = lax.broadcasted_iota(jnp.int32, (tn, tn), 0)
    c = lax.broadcasted_iota(jnp.int32, (tn, tn), 1)
    earlier = jnp.where(r < c, 1.0, 0.0).astype(BF16)
    base = base_ref[...]
    before = _dot(oh.astype(BF16), earlier) + jnp.tile(base, (1, tn // LANES))
    r1 = jnp.sum(jnp.where(oh1, before, 0.0), axis=0, keepdims=True)
    r2 = jnp.sum(jnp.where(oh2, before, 0.0), axis=0, keepdims=True)
    rank_ref[...] = jnp.concatenate([r1, r2, jnp.zeros((SUBLANES - 2, tn), F32)], axis=0).astype(jnp.int32)
    base = base + jnp.sum(oh, axis=1, keepdims=True)
    base_ref[...] = base
    cnt_ref[...] = base.astype(jnp.int32)


def _rank(eidx, tn):
    t = eidx.shape[1]
    tn = min(tn, t)
    return pl.pallas_call(
        functools.partial(_rank_kernel, tn=tn),
        grid=(t // tn,),
        in_specs=[pl.BlockSpec((SUBLANES, tn), lambda i: (0, i))],
        out_specs=[pl.BlockSpec((SUBLANES, tn), lambda i: (0, i)),
                   pl.BlockSpec((N_EXPERTS, LANES), lambda i: (0, 0))],
        out_shape=[jax.ShapeDtypeStruct((SUBLANES, t), jnp.int32),
                   jax.ShapeDtypeStruct((N_EXPERTS, LANES), jnp.int32)],
        scratch_shapes=[pltpu.VMEM((N_EXPERTS, LANES), F32)],
        compiler_params=_params(("arbitrary",)),
        name="moe_rank",
    )(eidx)


def _moe_kernel(te_ref, nu_ref, xs_ref, wg_ref, wu_ref, wd_ref, y_ref, wgu_s, wd_s):
    tile = pl.program_id(0)
    used = tile < nu_ref[0]
    fresh = jnp.logical_or(tile == 0, te_ref[tile] != te_ref[jnp.maximum(tile - 1, 0)])

    @pl.when(jnp.logical_and(used, fresh))
    def _():
        wgu_s[:, :D_EXPERT] = wg_ref[...].astype(BF16)
        wgu_s[:, D_EXPERT:] = wu_ref[...].astype(BF16)
        wd_s[...] = wd_ref[...].astype(BF16)

    @pl.when(used)
    def _():
        for s in range(xs_ref.shape[0] // MOE_SUB):
            rows = slice(s * MOE_SUB, (s + 1) * MOE_SUB)
            hgu = _dot(xs_ref[rows, :].astype(BF16), wgu_s[...])
            hg = hgu[:, :D_EXPERT]
            hu = hgu[:, D_EXPERT:]
            h = (hg / (1.0 + jnp.exp(-hg))) * hu
            y_ref[rows, :] = _dot(h.astype(BF16), wd_s[...])

    @pl.when(jnp.logical_not(used))
    def _():
        y_ref[...] = jnp.zeros(y_ref.shape, F32)


def _moe(tile_expert, n_used, xs, w_gate, w_up, w_down, tm):
    t_pad = xs.shape[0]
    grid_spec = pltpu.PrefetchScalarGridSpec(
        num_scalar_prefetch=2,
        grid=(t_pad // tm,),
        in_specs=[
            pl.BlockSpec((tm, D_MODEL), lambda i, te, nu: (i, 0)),
            pl.BlockSpec((None, D_MODEL, D_EXPERT), lambda i, te, nu: (te[i], 0, 0)),
            pl.BlockSpec((None, D_MODEL, D_EXPERT), lambda i, te, nu: (te[i], 0, 0)),
            pl.BlockSpec((None, D_EXPERT, D_MODEL), lambda i, te, nu: (te[i], 0, 0)),
        ],
        out_specs=pl.BlockSpec((tm, D_MODEL), lambda i, te, nu: (i, 0)),
        scratch_shapes=[pltpu.VMEM((D_MODEL, 2 * D_EXPERT), BF16), pltpu.VMEM((D_EXPERT, D_MODEL), BF16)],
    )
    return pl.pallas_call(
        _moe_kernel,
        grid_spec=grid_spec,
        out_shape=jax.ShapeDtypeStruct((t_pad, D_MODEL), F32),
        compiler_params=_params(("arbitrary",)),
        name="moe",
    )(tile_expert, n_used, xs, w_gate, w_up, w_down)


def _moe_ln_kernel(x_ref, y1_ref, y2_ref, w_ref, g_ref, b_ref, o_ref):
    w = w_ref[...]
    v = DEEPNORM_ALPHA * x_ref[...] + (w[:, 0:1] * y1_ref[...] + w[:, 1:2] * y2_ref[...])
    o_ref[...] = _layer_norm(v, g_ref[...], b_ref[...])


def _moe_ln(x, y1, y2, w_cols, ln_g, ln_b, tm):
    t = x.shape[0]
    tm = min(tm, t)
    row = pl.BlockSpec((tm, D_MODEL), lambda i: (i, 0))
    vec = pl.BlockSpec((1, D_MODEL), lambda i: (0, 0))
    return pl.pallas_call(
        _moe_ln_kernel,
        grid=(t // tm,),
        in_specs=[row, row, row, pl.BlockSpec((tm, SUBLANES), lambda i: (i, 0)), vec, vec],
        out_specs=row,
        out_shape=jax.ShapeDtypeStruct((t, D_MODEL), F32),
        compiler_params=_params(("parallel",)),
        name="moe_ln",
    )(x, y1, y2, w_cols, ln_g, ln_b)


def _moe_block(x, eidx, ew, w_gate, w_up, w_down, ln_g, ln_b, tm):
    t = x.shape[0]
    tm = min(tm, t)
    rank, cnt = _rank(eidx, 512)
    counts = cnt[:, 0]
    padded = ((counts + tm - 1) // tm) * tm
    ends = jnp.cumsum(padded)
    starts = ends - padded
    e2 = eidx[:2]
    start_of = jnp.zeros_like(e2)
    for k in range(N_EXPERTS):
        start_of = jnp.where(e2 == k, starts[k], start_of)
    dest = start_of + rank[:2]
    t_pad = 2 * t + N_EXPERTS * tm
    tok = jnp.tile(jnp.arange(t, dtype=jnp.int32), 2)
    src_tok = (jnp.arange(t_pad, dtype=jnp.int32) % t).at[dest.reshape(2 * t)].set(
        tok, mode="promise_in_bounds", unique_indices=True)
    tile_start = jnp.arange(t_pad // tm, dtype=jnp.int32) * tm
    tile_expert = jnp.minimum(jnp.sum((ends[None, :] <= tile_start[:, None]).astype(jnp.int32), axis=1),
                              N_EXPERTS - 1)
    n_used = (ends[-1] // tm).astype(jnp.int32).reshape(1)
    xs = x.at[src_tok].get(mode="promise_in_bounds")
    ys = _moe(tile_expert, n_used, xs, w_gate, w_up, w_down, tm)
    y1 = ys.at[dest[0]].get(mode="promise_in_bounds")
    y2 = ys.at[dest[1]].get(mode="promise_in_bounds")
    return _moe_ln(x, y1, y2, ew.T, ln_g, ln_b, tm)


def kernel(x, router_w, router_b, l0_w_in, l0_diff_lam_q1, l0_diff_lam_k1, l0_diff_lam_q2, l0_diff_lam_k2, l0_diff_subln_g, l0_gla_w_gate, l0_gla_b_gate, l0_gla_norm_g, l0_w_out, l0_ln1_g, l0_ln1_b, l0_moe_w_gate, l0_moe_w_up, l0_moe_w_down, l0_ln2_g, l0_ln2_b, l1_w_in, l1_s5_a_re, l1_s5_a_im, l1_s5_log_step, l1_s5_b_re, l1_s5_b_im, l1_s5_c_re, l1_s5_c_im, l1_s5_d, l1_s5_w_glu, l1_s5_b_glu, l1_w_out, l1_ln1_g, l1_ln1_b, l1_moe_w_gate, l1_moe_w_up, l1_moe_w_down, l1_ln2_g, l1_ln2_b):
    bsz, t, d = x.shape
    assert bsz == 1 and d == D_MODEL
    x0 = x.reshape(t, d)
    vec = lambda p: p.reshape(1, -1).astype(F32)
    rwt = router_w.T.astype(F32)
    rb = router_b.reshape(N_EXPERTS, 1).astype(F32)

    lambda_init = 0.8 - 0.6 * math.exp(-0.3 * 0)
    c = [0, 1024, 2048, 3072, 3584, 4096, 5120, 5136, 6160]
    q_scale = math.log2(math.e) * DIFF_HEAD_DIM ** -0.5
    w0 = jnp.concatenate([l0_w_in[:, :c[1]] * q_scale, l0_w_in[:, c[1]:c[6]], l0_w_in[:, c[7]:c[8]],
                          l0_w_in[:, c[6]:c[7]], jnp.zeros((d, L0_GB_PAD - GLA_GATE_RANK), l0_w_in.dtype)],
                         axis=1).astype(BF16)
    h0 = _matmul(x0, w0, 1024, L0_IN_TN, BF16)
    lam_vec = jnp.stack([l0_diff_lam_q1, l0_diff_lam_k1, l0_diff_lam_q2, l0_diff_lam_k2]).astype(F32)
    o_a = _diff_attention(h0, lam_vec, vec(l0_diff_subln_g), lambda_init, 1024)
    wg_pad = jnp.concatenate([l0_gla_w_gate, jnp.zeros((LANES - GLA_GATE_RANK, GLA_HEADS * GLA_DK), F32)],
                             axis=0).astype(BF16)
    o_b = _gla(h0, wg_pad, vec(l0_gla_b_gate), vec(l0_gla_norm_g), 256)
    x1, eidx, ew = _outproj_ln_router(o_a, o_b, l0_w_out.astype(BF16), x0, vec(l0_ln1_g), vec(l0_ln1_b),
                                      rwt, rb, 512, 128)
    x2 = _moe_block(x1, eidx, ew, l0_moe_w_gate, l0_moe_w_up, l0_moe_w_down, vec(l0_ln2_g), vec(l0_ln2_b), 256)

    h1 = _matmul(x2, l1_w_in.astype(BF16), 1024, 1024, BF16)
    o_c = _stick_breaking(h1, 256)
    lbr, lbi, bbr, bbi = _s5_prep(l1_s5_a_re, l1_s5_a_im, l1_s5_log_step, l1_s5_b_re, l1_s5_b_im)
    bd, cd, pw8 = _s5_tile_params(lbr[::S5_GROUP], lbi[::S5_GROUP], bbr.reshape(S5_GROUPS, S5_GROUP, S5_STATE),
                                  bbi.reshape(S5_GROUPS, S5_GROUP, S5_STATE),
                                  l1_s5_c_re.astype(F32), l1_s5_c_im.astype(F32))
    y_s5 = _s5_core(h1, bd, cd, pw8, 256)
    o_d = _s5_glu(y_s5, h1, vec(l1_s5_d), l1_s5_w_glu.astype(BF16), vec(l1_s5_b_glu), 512)
    x3, eidx1, ew1 = _outproj_ln_router(o_c, o_d, l1_w_out.astype(BF16), x2, vec(l1_ln1_g), vec(l1_ln1_b),
                                        rwt, rb, 512, 128)
    x4 = _moe_block(x3, eidx1, ew1, l1_moe_w_gate, l1_moe_w_up, l1_moe_w_down, vec(l1_ln2_g), vec(l1_ln2_b), 256)
    return x4.reshape(bsz, t, d)
```

```python
import functools
import math

import jax
import jax.numpy as jnp
from jax import lax
from jax.experimental import pallas as pl
from jax.experimental.pallas import tpu as pltpu

F32 = jnp.float32
BF16 = jnp.bfloat16

D_MODEL = 2048
DEPTH = 2
DIFF_HEADS = 8
DIFF_HEAD_DIM = 64
DIFF_V_DIM = 128
GLA_HEADS = 4
GLA_DK = 128
GLA_DV = 256
GLA_GATE_RANK = 16
GLA_GATE_TEMP = 16.0
GLA_CHUNK = 64
SB_HEADS = 8
SB_HEAD_DIM = 128
S5_CHANNELS = 1024
S5_GROUP = 16
S5_GROUPS = 64
S5_STATE = 64
N_EXPERTS = 16
N_GROUPS = 4
EXPERTS_PER_GROUP = 4
D_EXPERT = 640
DEEPNORM_ALPHA = (2.0 * DEPTH) ** 0.25
LN_EPS = 1e-5

LANES = 128
SUBLANES = 8
VMEM_LIMIT = 56 * 1024 * 1024
SB_LOG_FLOOR = 120.0
SB_HEADS_PER_STEP = 4
MOE_SUB = 256

MXU_N = 256
L0_GB_PAD = MXU_N
L0_IN_TN = 5 * MXU_N

S5_KT = 4
S5_KT_CH = S5_CHANNELS // S5_KT
S5_KT_STATES = (S5_GROUPS // S5_KT) * S5_STATE
S5_SCAN_W = 512
S5_L = 8


def _params(sem, flags=None):
    return pltpu.CompilerParams(dimension_semantics=sem, vmem_limit_bytes=VMEM_LIMIT, flags=flags)


def _nt_dot(a, b):
    return lax.dot_general(a, b, (((1,), (1,)), ((), ())), preferred_element_type=F32)


def _dot(a, b):
    return jnp.dot(a, b, preferred_element_type=F32)


def _split_dot_left(m, x):
    hi = x.astype(BF16)
    lo = (x - hi.astype(F32)).astype(BF16)
    return _dot(m, hi) + _dot(m, lo)


def _layer_norm(v, g, b):
    mu = jnp.mean(v, axis=-1, keepdims=True)
    d = v - mu
    var = jnp.mean(d * d, axis=-1, keepdims=True)
    return d * lax.rsqrt(var + LN_EPS) * g + b


def _mm_kernel(a_ref, b_ref, o_ref, a_bf):
    @pl.when(pl.program_id(1) == 0)
    def _():
        a_bf[...] = a_ref[...].astype(BF16)

    o_ref[...] = _dot(a_bf[...], b_ref[...]).astype(o_ref.dtype)


def _matmul(a, b, tm, tn, out_dtype):
    m, k = a.shape
    n = b.shape[1]
    tm = min(tm, m)
    return pl.pallas_call(
        _mm_kernel,
        grid=(m // tm, n // tn),
        in_specs=[pl.BlockSpec((tm, k), lambda i, j: (i, 0)),
                  pl.BlockSpec((k, tn), lambda i, j: (0, j))],
        out_specs=pl.BlockSpec((tm, tn), lambda i, j: (i, j)),
        out_shape=jax.ShapeDtypeStruct((m, n), out_dtype),
        scratch_shapes=[pltpu.VMEM((tm, k), BF16)],
        compiler_params=_params(("parallel", "arbitrary")),
        name="in_proj",
    )(a, b)


def _diff_attn_kernel(lam_ref, g_ref, q_ref, k_ref, v_ref, o_ref, vx_ref, m_ref, acc_ref, sa_ref, sb_ref, *, tq,
                      lambda_init):
    i = pl.program_id(1)
    dv = DIFF_V_DIM

    @pl.when(i == 0)
    def _():
        vx_ref[:, :dv] = v_ref[...]
        vx_ref[:, dv:] = jnp.ones((vx_ref.shape[0], dv), BF16)

    m_ref[...] = jnp.full(m_ref.shape, -jnp.inf, F32)
    acc_ref[...] = jnp.zeros(acc_ref.shape, F32)
    q = q_ref[...]
    lane = lax.broadcasted_iota(jnp.int32, q.shape, 1)
    zero = jnp.zeros_like(q)
    halves = (jnp.where(lane < DIFF_HEAD_DIM, q, zero), jnp.where(lane >= DIFF_HEAD_DIM, q, zero))

    def scores(j, mi):
        off = pl.multiple_of(j * tq, tq)
        return _nt_dot(halves[mi], k_ref[pl.ds(off, tq), :])

    def softmax_pv(j, get_s, masked):
        off = pl.multiple_of(j * tq, tq)
        vx = vx_ref[pl.ds(off, tq), :]
        if masked:
            row = lax.broadcasted_iota(jnp.int32, (tq, tq), 0)
            col = lax.broadcasted_iota(jnp.int32, (tq, tq), 1)
            causal = col <= row
        for mi in range(2):
            s = get_s(mi)
            if masked:
                s = jnp.where(causal, s, -jnp.inf)
            m_prev = m_ref[mi]
            m_new = jnp.maximum(m_prev, jnp.max(s, axis=1, keepdims=True))
            p = jnp.exp2(s - jnp.tile(m_new, (1, tq // LANES)))
            alpha = jnp.exp2(m_prev - m_new)
            acc_ref[mi] = jnp.tile(alpha, (1, 2)) * acc_ref[mi] + _dot(p.astype(BF16), vx)
            m_ref[mi] = m_new

    def scores_to(j, s_ref):
        for mi in range(2):
            s_ref[mi] = scores(j, mi)

    scores_to(0, sa_ref)

    def pair(t, carry):
        j = 2 * t
        scores_to(j + 1, sb_ref)
        softmax_pv(j, lambda mi: sa_ref[mi], False)
        scores_to(j + 2, sa_ref)
        softmax_pv(j + 1, lambda mi: sb_ref[mi], False)
        return carry

    lax.fori_loop(0, i // 2, pair, 0)

    @pl.when(i % 2 == 1)
    def _():
        softmax_pv(i - 1, lambda mi: sa_ref[mi], False)
        scores_to(i, sa_ref)

    softmax_pv(i, lambda mi: sa_ref[mi], True)

    lv = lam_ref[...]
    lam = (jnp.exp(jnp.sum(lv[0:1] * lv[1:2], axis=1, keepdims=True))
           - jnp.exp(jnp.sum(lv[2:3] * lv[3:4], axis=1, keepdims=True)) + lambda_init)
    a0 = acc_ref[0]
    a1 = acc_ref[1]
    o = a0[:, :dv] / a0[:, dv:] - lam * (a1[:, :dv] / a1[:, dv:])
    ms = jnp.mean(o * o, axis=-1, keepdims=True)
    o = o * lax.rsqrt(ms + LN_EPS) * g_ref[...] * (1.0 - lambda_init)
    o_ref[...] = o.astype(o_ref.dtype)


def _diff_attention(h0, lam_vec, subln_g, lambda_init, tq):
    t = h0.shape[0]
    tq = min(tq, t)
    nq = t // tq
    kern = functools.partial(_diff_attn_kernel, tq=tq, lambda_init=lambda_init)
    return pl.pallas_call(
        kern,
        grid=(DIFF_HEADS, nq),
        in_specs=[
            pl.BlockSpec((4, DIFF_HEAD_DIM), lambda h, i: (0, 0)),
            pl.BlockSpec((1, DIFF_V_DIM), lambda h, i: (0, 0)),
            pl.BlockSpec((tq, LANES), lambda h, i: (i, h)),
            pl.BlockSpec((t, LANES), lambda h, i: (0, DIFF_HEADS + h)),
            pl.BlockSpec((t, LANES), lambda h, i: (0, 2 * DIFF_HEADS + h)),
        ],
        out_specs=pl.BlockSpec((tq, LANES), lambda h, i: (i, h)),
        out_shape=jax.ShapeDtypeStruct((t, DIFF_HEADS * DIFF_V_DIM), BF16),
        scratch_shapes=[pltpu.VMEM((t, 2 * DIFF_V_DIM), BF16), pltpu.VMEM((2, tq, LANES), F32),
                        pltpu.VMEM((2, tq, 2 * DIFF_V_DIM), F32),
                        pltpu.VMEM((2, tq, tq), F32), pltpu.VMEM((2, tq, tq), F32)],
        compiler_params=_params(("arbitrary", "arbitrary")),
        name="diff_attn",
    )(lam_vec, subln_g, h0, h0, h0)


def _gla_kernel(q_ref, k_ref, v_ref, r_ref, gb_ref, wg_ref, bg_ref, ng_ref, o_ref, st_ref, *, tg):
    @pl.when(pl.program_id(0) == 0)
    def _():
        st_ref[...] = jnp.zeros(st_ref.shape, F32)

    c = GLA_CHUNK
    gate = _dot(gb_ref[...], wg_ref[...]) + bg_ref[...]
    log_a = -(jnp.maximum(-gate, 0.0) + jnp.log(1.0 + jnp.exp(-jnp.abs(gate)))) / GLA_GATE_TEMP
    row = lax.broadcasted_iota(jnp.int32, (c, c), 0)
    col = lax.broadcasted_iota(jnp.int32, (c, c), 1)
    tril = row >= col
    tri = jnp.where(tril, 1.0, 0.0).astype(BF16)
    for h in range(GLA_HEADS):
        ksl = slice(h * GLA_DK, (h + 1) * GLA_DK)
        vsl = slice(h * GLA_DV, (h + 1) * GLA_DV)
        for ci in range(tg // c):
            rs = slice(ci * c, (ci + 1) * c)
            la = log_a[rs, ksl]
            b = _split_dot_left(tri, la)
            b_last = b[c - 1:c, :]
            qf = q_ref[rs, ksl].astype(F32) * (GLA_DK ** -0.5)
            kf = k_ref[rs, ksl].astype(F32)
            v = v_ref[rs, vsl]
            q_dec = (qf * jnp.exp(b)).astype(BF16)
            k_dec = (kf * jnp.exp(-b)).astype(BF16)
            k_end = (kf * jnp.exp(b_last - b)).astype(BF16)
            scores = jnp.where(tril, _nt_dot(q_dec, k_dec), 0.0)
            st = st_ref[h]
            o = _dot(scores.astype(BF16), v) + _nt_dot(q_dec, st.astype(BF16))
            kv_t = lax.dot_general(v, k_end, (((0,), (0,)), ((), ())), preferred_element_type=F32)
            st_ref[h] = jnp.exp(b_last) * st + kv_t
            ms = jnp.mean(o * o, axis=-1, keepdims=True)
            o = o * lax.rsqrt(ms + LN_EPS) * ng_ref[...]
            r = r_ref[rs, vsl].astype(F32)
            o_ref[rs, vsl] = (o * (r / (1.0 + jnp.exp(-r)))).astype(o_ref.dtype)


def _gla(h0, wg_pad, b_gate, norm_g, tg):
    t = h0.shape[0]
    tg = min(tg, t)
    qk = GLA_HEADS * GLA_DK
    vw = GLA_HEADS * GLA_DV
    return pl.pallas_call(
        functools.partial(_gla_kernel, tg=tg),
        grid=(t // tg,),
        in_specs=[
            pl.BlockSpec((tg, qk), lambda i: (i, 3072 // qk)),
            pl.BlockSpec((tg, qk), lambda i: (i, 3584 // qk)),
            pl.BlockSpec((tg, vw), lambda i: (i, 4096 // vw)),
            pl.BlockSpec((tg, vw), lambda i: (i, 5120 // vw)),
            pl.BlockSpec((tg, LANES), lambda i: (i, 6144 // LANES)),
            pl.BlockSpec((LANES, qk), lambda i: (0, 0)),
            pl.BlockSpec((1, qk), lambda i: (0, 0)),
            pl.BlockSpec((1, GLA_DV), lambda i: (0, 0)),
        ],
        out_specs=pl.BlockSpec((tg, vw), lambda i: (i, 0)),
        out_shape=jax.ShapeDtypeStruct((t, vw), BF16),
        scratch_shapes=[pltpu.VMEM((GLA_HEADS, GLA_DV, GLA_DK), F32)],
        compiler_params=_params(("arbitrary",)),
        name="gla",
    )(h0, h0, h0, h0, h0, wg_pad, b_gate, norm_g)


def _sb_kernel(q_ref, k_ref, v_ref, o_ref, acc_ref, c_ref, *, tq):
    i = pl.program_id(1)
    hd = SB_HEAD_DIM
    acc_ref[...] = jnp.zeros(acc_ref.shape, F32)
    c_ref[...] = jnp.zeros(c_ref.shape, F32)
    row = lax.broadcasted_iota(jnp.int32, (tq, tq), 0)
    col = lax.broadcasted_iota(jnp.int32, (tq, tq), 1)
    strict = col < row
    tri = jnp.where(row > col, 1.0, 0.0).astype(BF16)

    def block(jb, masked):
        off = pl.multiple_of(jb * tq, tq)
        cmax = None
        for hh in range(SB_HEADS_PER_STEP):
            cols = slice(hh * hd, (hh + 1) * hd)
            k = k_ref[pl.ds(off, tq), cols]
            v = v_ref[pl.ds(off, tq), cols]
            z = _nt_dot(q_ref[:, cols], k) * (SB_HEAD_DIM ** -0.5)
            lsn = -(jnp.maximum(z, 0.0) + jnp.log(1.0 + jnp.exp(-jnp.abs(z))))
            lm = jnp.where(strict, lsn, 0.0) if masked else lsn
            excl = _dot(lm.astype(BF16), tri) + c_ref[hh]
            w = jnp.exp(z + lsn + excl)
            if masked:
                w = jnp.where(strict, w, 0.0)
            acc_ref[:, cols] += _dot(w.astype(BF16), v)
            c_new = c_ref[hh] + jnp.sum(lm, axis=1, keepdims=True)
            c_ref[hh] = c_new
            hmax = jnp.max(c_new)
            cmax = hmax if cmax is None else jnp.maximum(cmax, hmax)
        return cmax

    cmax = block(i, True)

    def cond(carry):
        jb, cm = carry
        return jnp.logical_and(jb >= 0, cm > -SB_LOG_FLOOR)

    def body(carry):
        jb, _ = carry
        return jb - 1, block(jb, False)

    lax.while_loop(cond, body, (i - 1, cmax))
    o_ref[...] = acc_ref[...].astype(o_ref.dtype)


def _stick_breaking(h1, tq):
    t = h1.shape[0]
    tq = min(tq, t)
    ng = SB_HEADS // SB_HEADS_PER_STEP
    wd = SB_HEADS_PER_STEP * SB_HEAD_DIM
    return pl.pallas_call(
        functools.partial(_sb_kernel, tq=tq),
        grid=(ng, t // tq),
        in_specs=[
            pl.BlockSpec((tq, wd), lambda h, i: (i, h)),
            pl.BlockSpec((t, wd), lambda h, i: (0, ng + h), pipeline_mode=pl.Buffered(1)),
            pl.BlockSpec((t, wd), lambda h, i: (0, 2 * ng + h), pipeline_mode=pl.Buffered(1)),
        ],
        out_specs=pl.BlockSpec((tq, wd), lambda h, i: (i, h)),
        out_shape=jax.ShapeDtypeStruct((t, SB_HEADS * SB_HEAD_DIM), BF16),
        scratch_shapes=[pltpu.VMEM((tq, wd), F32), pltpu.VMEM((SB_HEADS_PER_STEP, tq, 1), F32)],
        compiler_params=_params(("parallel", "arbitrary")),
        name="stick_breaking",
    )(h1, h1, h1)


def _s5_prep_kernel(are_ref, aim_ref, ls_ref, bre_ref, bim_ref, lbr_ref, lbi_ref, bbr_ref, bbi_ref):
    step = jnp.exp(ls_ref[...])
    lam_re = are_ref[...]
    lam_im = aim_ref[...]
    mag = jnp.exp(step * lam_re)
    lb_re = mag * jnp.cos(step * lam_im)
    lb_im = mag * jnp.sin(step * lam_im)
    den = lam_re * lam_re + lam_im * lam_im
    n_re = lb_re - 1.0
    f_re = (n_re * lam_re + lb_im * lam_im) / den
    f_im = (lb_im * lam_re - n_re * lam_im) / den
    lbr_ref[...] = lb_re
    lbi_ref[...] = lb_im
    bbr_ref[...] = f_re * bre_ref[...] - f_im * bim_ref[...]
    bbi_ref[...] = f_re * bim_ref[...] + f_im * bre_ref[...]


def _s5_prep(a_re, a_im, log_step, b_re, b_im):
    rows = S5_GROUPS * S5_GROUP
    rep = lambda p: jnp.repeat(p, S5_GROUP, axis=0)
    to_rows = lambda b: jnp.transpose(b, (0, 2, 1)).reshape(rows, S5_STATE)
    spec = pl.BlockSpec((rows, S5_STATE), lambda: (0, 0))
    shp = jax.ShapeDtypeStruct((rows, S5_STATE), F32)
    return pl.pallas_call(
        _s5_prep_kernel,
        in_specs=[spec, spec, pl.BlockSpec((rows, 1), lambda: (0, 0)), spec, spec],
        out_specs=[spec] * 4,
        out_shape=[shp] * 4,
        name="s5_prep",
    )(rep(a_re), rep(a_im), rep(log_step[:, None]), to_rows(b_re), to_rows(b_im))


def _cmul(ar, ai, br, bi):
    return ar * br - ai * bi, ar * bi + ai * br


def _block_diag_tiles(m, rows_per_group, cols_per_group):
    gpt = S5_GROUPS // S5_KT
    m = m.reshape(S5_KT, gpt, rows_per_group, cols_per_group)
    eye = jnp.eye(gpt, dtype=m.dtype)
    bd = jnp.einsum("kgrc,gh->kgrhc", m, eye)
    return bd.reshape(S5_KT, gpt * rows_per_group, gpt * cols_per_group)


def _s5_tile_params(lb_re, lb_im, bb_re, bb_im, c_re, c_im):
    bd = jnp.concatenate([_block_diag_tiles(bb_re, S5_GROUP, S5_STATE),
                          _block_diag_tiles(bb_im, S5_GROUP, S5_STATE)], axis=2)
    cd = jnp.concatenate([_block_diag_tiles(jnp.transpose(c_re, (0, 2, 1)), S5_STATE, S5_GROUP),
                          _block_diag_tiles(-jnp.transpose(c_im, (0, 2, 1)), S5_STATE, S5_GROUP)], axis=1)
    pr = [jnp.ones_like(lb_re)]
    pi = [jnp.zeros_like(lb_re)]
    for _ in range(S5_L):
        nr, ni = _cmul(pr[-1], pi[-1], lb_re, lb_im)
        pr.append(nr)
        pi.append(ni)

    def table(p):
        rows = jnp.transpose(jnp.stack(p).reshape(S5_L + 1, S5_KT, S5_KT_STATES), (1, 0, 2))
        return jnp.pad(rows, ((0, 0), (0, LANES - (S5_L + 1)), (0, 0)))

    return bd, cd, jnp.stack([table(pr), table(pi)], axis=1)


def _s5_core_kernel(u_ref, bd_ref, cd_ref, pw8_ref, y_ref, hs_ref, hp_ref, pw_ref, st_ref, uf_ref,
                    mt_s, bbs_s, ccs_s, *, tcr):
    ns = S5_KT_STATES
    pad = SUBLANES
    cw = S5_KT_CH

    @pl.when(pl.program_id(1) == 0)
    def _():
        b_re = bd_ref[:, :ns]
        b_im = bd_ref[:, ns:]
        cd_hi = cd_ref[...].astype(BF16)
        cd_lo = (cd_ref[...] - cd_hi.astype(F32)).astype(BF16)
        for k in range(S5_L):
            a = pw8_ref[0, k:k + 1, :]
            b = pw8_ref[1, k:k + 1, :]
            dk = jnp.concatenate([b_re * a - b_im * b, b_re * b + b_im * a], axis=1)
            s = S5_L - 1 - k
            bbs_s[s * cw:(s + 1) * cw, :] = dk.astype(BF16)
            dk_hi = dk.astype(BF16)
            dk_lo = (dk - dk_hi.astype(F32)).astype(BF16)
            kk = (_dot(dk_hi, cd_hi) + (_dot(dk_lo, cd_hi) + _dot(dk_hi, cd_lo))).astype(BF16)
            for s0 in range(S5_L - k):
                mt_s[s0 * cw:(s0 + 1) * cw, (s0 + k) * cw:(s0 + k + 1) * cw] = kk
        pw_t = (jnp.transpose(pw8_ref[0]), jnp.transpose(pw8_ref[1]))
        c_re = cd_ref[:ns, :]
        c_im = cd_ref[ns:, :]
        for i in range(S5_L):
            a = pw_t[0][:, i + 1:i + 2]
            b = pw_t[1][:, i + 1:i + 2]
            ccs_s[:ns, i * cw:(i + 1) * cw] = (c_re * a + c_im * b).astype(BF16)
            ccs_s[ns:, i * cw:(i + 1) * cw] = (c_im * a - c_re * b).astype(BF16)

        hp_ref[...] = jnp.zeros(hp_ref.shape, F32)
        hs_ref[0:pad, :] = jnp.zeros((pad, 2 * ns), F32)
        l1r = jnp.broadcast_to(pw8_ref[0, S5_L:S5_L + 1, :], (SUBLANES, ns))
        l1i = jnp.broadcast_to(pw8_ref[1, S5_L:S5_L + 1, :], (SUBLANES, ns))
        l2r, l2i = _cmul(l1r, l1i, l1r, l1i)
        l4r, l4i = _cmul(l2r, l2i, l2r, l2i)
        l8r, l8i = _cmul(l4r, l4i, l4r, l4i)
        ridx = lax.broadcasted_iota(jnp.int32, l1r.shape, 0)
        n = ridx + 1
        pr = jnp.ones_like(l1r)
        pi = jnp.zeros_like(l1r)
        for bit, (lr, li) in ((1, (l1r, l1i)), (2, (l2r, l2i)), (4, (l4r, l4i)), (8, (l8r, l8i))):
            nr, ni = _cmul(pr, pi, lr, li)
            take = (n & bit) != 0
            pr = jnp.where(take, nr, pr)
            pi = jnp.where(take, ni, pi)
        pw_ref[0] = pr
        pw_ref[1] = pi
        for s, (lr, li) in enumerate(((l1r, l1i), (l2r, l2i), (l4r, l4i))):
            keep = ridx >= (1 << s)
            st_ref[2 * s] = jnp.where(keep, lr, 0.0)
            st_ref[2 * s + 1] = jnp.where(keep, li, 0.0)

    nl = S5_KT_CH // LANES
    for h in range(nl):
        uf_ref[h] = u_ref[:, h * LANES:(h + 1) * LANES].astype(F32)
    uc = jnp.concatenate([uf_ref[h, pl.ds(s, tcr, stride=S5_L), :].astype(BF16)
                          for s in range(S5_L) for h in range(nl)], axis=1)
    hs_ref[pad:pad + tcr, :] = _dot(uc, bbs_s[...])
    w = S5_SCAN_W
    for cc in range(ns // w):
        re0 = cc * w
        im0 = ns + cc * w

        def body(r, carry, re0=re0, im0=im0):
            hpr, hpi = carry
            rows = pl.ds(pl.multiple_of(pad + r * SUBLANES, SUBLANES), SUBLANES)
            xr = hs_ref[rows, re0:re0 + w]
            xi = hs_ref[rows, im0:im0 + w]
            for s in range(3):
                sr = pltpu.roll(xr, 1 << s, 0)
                si = pltpu.roll(xi, 1 << s, 0)
                ar = st_ref[2 * s, :, re0:re0 + w]
                ai = st_ref[2 * s + 1, :, re0:re0 + w]
                xr, xi = xr + ar * sr - ai * si, xi + ar * si + ai * sr
            pr = pw_ref[0, :, re0:re0 + w]
            pi = pw_ref[1, :, re0:re0 + w]
            xr, xi = xr + pr * hpr - pi * hpi, xi + pr * hpi + pi * hpr
            hs_ref[rows, re0:re0 + w] = xr
            hs_ref[rows, im0:im0 + w] = xi
            return (jnp.broadcast_to(xr[SUBLANES - 1:SUBLANES, :], (SUBLANES, w)),
                    jnp.broadcast_to(xi[SUBLANES - 1:SUBLANES, :], (SUBLANES, w)))

        hpr, hpi = lax.fori_loop(0, tcr // SUBLANES, body, (hp_ref[:, re0:re0 + w], hp_ref[:, im0:im0 + w]))
        hp_ref[:, re0:re0 + w] = hpr
        hp_ref[:, im0:im0 + w] = hpi

    h_prev = hs_ref[pad - 1:pad - 1 + tcr, :]
    y_in = _dot(h_prev.astype(BF16), ccs_s[...])
    for i in range(S5_L):
        k_hi = (i + 1) * cw
        y_i = y_in[:, i * cw:k_hi] + _dot(uc[:, :k_hi], mt_s[:k_hi, i * cw:k_hi])
        for h in range(nl):
            uf_ref[h, pl.ds(i, tcr, stride=S5_L), :] = y_i[:, h * LANES:(h + 1) * LANES]
    for h in range(nl):
        y_ref[:, h * LANES:(h + 1) * LANES] = uf_ref[h]
    hs_ref[pad - 1:pad, :] = hs_ref[pad + tcr - 1:pad + tcr, :]


def _s5_core(h1, bd, cd, pw8, tcr):
    t = h1.shape[0]
    width = S5_L * S5_KT_CH
    tcr = min(tcr, t // S5_L)
    tc = tcr * S5_L
    ns = S5_KT_STATES
    u_col0 = 3 * S5_CHANNELS // S5_KT_CH
    once = pl.Buffered(1)
    return pl.pallas_call(
        functools.partial(_s5_core_kernel, tcr=tcr),
        grid=(S5_KT, t // tc),
        in_specs=[
            pl.BlockSpec((tc, S5_KT_CH), lambda k, i: (i, u_col0 + k)),
            pl.BlockSpec((None, S5_KT_CH, 2 * ns), lambda k, i: (k, 0, 0), pipeline_mode=once),
            pl.BlockSpec((None, 2 * ns, S5_KT_CH), lambda k, i: (k, 0, 0), pipeline_mode=once),
            pl.BlockSpec((None, 2, LANES, ns), lambda k, i: (k, 0, 0, 0), pipeline_mode=once),
        ],
        out_specs=pl.BlockSpec((tc, S5_KT_CH), lambda k, i: (i, k)),
        out_shape=jax.ShapeDtypeStruct((t, S5_CHANNELS), F32),
        scratch_shapes=[pltpu.VMEM((tcr + SUBLANES, 2 * ns), F32), pltpu.VMEM((SUBLANES, 2 * ns), F32),
                        pltpu.VMEM((2, SUBLANES, ns), F32), pltpu.VMEM((6, SUBLANES, ns), F32),
                        pltpu.VMEM((S5_KT_CH // LANES, tc, LANES), F32),
                        pltpu.VMEM((width, width), BF16), pltpu.VMEM((width, 2 * ns), BF16),
                        pltpu.VMEM((2 * ns, width), BF16)],
        compiler_params=_params(("arbitrary", "arbitrary")),
        name="s5_core",
    )(h1, bd, cd, pw8)


def _s5_glu_kernel(y_ref, u_ref, d_ref, wglu_ref, bglu_ref, o_ref):
    y = y_ref[...] + d_ref[...] * u_ref[...].astype(F32)
    g = 0.5 * y * (1.0 + jnp.tanh(math.sqrt(2.0 / math.pi) * (y + 0.044715 * (y * y * y))))
    gl = _dot(g.astype(BF16), wglu_ref[...]) + bglu_ref[...]
    o_ref[...] = (g / (1.0 + jnp.exp(-gl))).astype(o_ref.dtype)


def _s5_glu(y, h1, d_skip, w_glu, b_glu, tm):
    t = h1.shape[0]
    tm = min(tm, t)
    full = lambda shape: pl.BlockSpec(shape, lambda i: (0,) * len(shape))
    return pl.pallas_call(
        _s5_glu_kernel,
        grid=(t // tm,),
        in_specs=[pl.BlockSpec((tm, S5_CHANNELS), lambda i: (i, 0)),
                  pl.BlockSpec((tm, S5_CHANNELS), lambda i: (i, 3)),
                  full((1, S5_CHANNELS)), full((S5_CHANNELS, S5_CHANNELS)), full((1, S5_CHANNELS))],
        out_specs=pl.BlockSpec((tm, S5_CHANNELS), lambda i: (i, 0)),
        out_shape=jax.ShapeDtypeStruct((t, S5_CHANNELS), BF16),
        compiler_params=_params(("parallel",)),
        name="s5_glu",
    )(y, h1, d_skip, w_glu, b_glu)


def _route(probs, sel):
    rows = [sel[e:e + 1, :] for e in range(N_EXPERTS)]
    prow = [probs[e:e + 1, :] for e in range(N_EXPERTS)]
    best_score = None
    for g in range(N_GROUPS):
        a = rows[g * EXPERTS_PER_GROUP:(g + 1) * EXPERTS_PER_GROUP]
        score = None
        for x in range(EXPERTS_PER_GROUP):
            for y in range(x + 1, EXPERTS_PER_GROUP):
                pair = a[x] + a[y]
                score = pair if score is None else jnp.maximum(score, pair)
        if best_score is None:
            best_score, gbest = score, jnp.zeros(score.shape, jnp.int32)
            cs = list(a)
            cp = prow[0:EXPERTS_PER_GROUP]
        else:
            better = score > best_score
            best_score = jnp.where(better, score, best_score)
            gbest = jnp.where(better, g, gbest)
            cs = [jnp.where(better, a[x], cs[x]) for x in range(EXPERTS_PER_GROUP)]
            cp = [jnp.where(better, prow[g * EXPERTS_PER_GROUP + x], cp[x]) for x in range(EXPERTS_PER_GROUP)]

    def first_argmax(vals):
        bv, bi = vals[0], jnp.zeros(vals[0].shape, jnp.int32)
        for x in range(1, len(vals)):
            better = vals[x] > bv
            bv = jnp.where(better, vals[x], bv)
            bi = jnp.where(better, x, bi)
        return bi

    i1 = first_argmax(cs)
    i2 = first_argmax([jnp.where(i1 == x, -jnp.inf, cs[x]) for x in range(EXPERTS_PER_GROUP)])

    def pick(idx):
        out = cp[0]
        for x in range(1, EXPERTS_PER_GROUP):
            out = jnp.where(idx == x, cp[x], out)
        return out

    p1, p2 = pick(i1), pick(i2)
    tot = p1 + p2
    return gbest * EXPERTS_PER_GROUP + i1, gbest * EXPERTS_PER_GROUP + i2, p1 / tot, p2 / tot


def _ln_router_tail(v, g_ref, b_ref, rwh_ref, rwl_ref, rb_ref):
    xn = _layer_norm(v, g_ref[...], b_ref[...])
    xh = xn.astype(BF16)
    xl = (xn - xh.astype(F32)).astype(BF16)
    logits = _nt_dot(rwh_ref[...], xh) + (_nt_dot(rwl_ref[...], xh) + _nt_dot(rwh_ref[...], xl))
    mx = jnp.max(logits, axis=0, keepdims=True)
    ex = jnp.exp(logits - mx)
    probs = ex / jnp.sum(ex, axis=0, keepdims=True)
    e1, e2, w1, w2 = _route(probs, probs + rb_ref[...])
    n = e1.shape[1]
    ei = jnp.concatenate([e1, e2, jnp.zeros((SUBLANES - 2, n), jnp.int32)], axis=0)
    ew = jnp.concatenate([w1, w2, jnp.zeros((SUBLANES - 2, n), F32)], axis=0)
    return xn, ei, ew


def _outproj_kernel(ma_ref, mb_ref, w_ref, xr_ref, g_ref, b_ref, rwh_ref, rwl_ref, rb_ref,
                    x_ref, ei_ref, ew_ref, w_bf, *, sub):
    @pl.when(pl.program_id(0) == 0)
    def _():
        w_bf[...] = w_ref[...].astype(BF16)

    half = w_bf.shape[0] // 2
    for s in range(x_ref.shape[0] // sub):
        rows = slice(s * sub, (s + 1) * sub)
        v = (DEEPNORM_ALPHA * xr_ref[rows, :] + _dot(ma_ref[rows, :], w_bf[:half, :])
             + _dot(mb_ref[rows, :], w_bf[half:, :]))
        xn, ei, ew = _ln_router_tail(v, g_ref, b_ref, rwh_ref, rwl_ref, rb_ref)
        x_ref[rows, :] = xn
        ei_ref[:, rows] = ei
        ew_ref[:, rows] = ew


def _outproj_ln_router(mix_a, mix_b, w_out, x_res, ln_g, ln_b, rwt, rb, tm, sub):
    t = x_res.shape[0]
    rwt_hi = rwt.astype(BF16)
    rwt_lo = (rwt - rwt_hi.astype(F32)).astype(BF16)
    tm = min(tm, t)
    sub = min(sub, tm)
    half = w_out.shape[0] // 2
    row = lambda w: pl.BlockSpec((tm, w), lambda i: (i, 0))
    full = lambda shape: pl.BlockSpec(shape, lambda i: (0,) * len(shape))
    return pl.pallas_call(
        functools.partial(_outproj_kernel, sub=sub),
        grid=(t // tm,),
        in_specs=[row(half), row(half),
                  pl.BlockSpec((2 * half, D_MODEL), lambda i: (0, 0), pipeline_mode=pl.Buffered(1)),
                  row(D_MODEL), full((1, D_MODEL)), full((1, D_MODEL)),
                  full((N_EXPERTS, D_MODEL)), full((N_EXPERTS, D_MODEL)), full((N_EXPERTS, 1))],
        out_specs=[row(D_MODEL),
                   pl.BlockSpec((SUBLANES, tm), lambda i: (0, i)), pl.BlockSpec((SUBLANES, tm), lambda i: (0, i))],
        out_shape=[jax.ShapeDtypeStruct((t, D_MODEL), F32),
                   jax.ShapeDtypeStruct((SUBLANES, t), jnp.int32), jax.ShapeDtypeStruct((SUBLANES, t), F32)],
        scratch_shapes=[pltpu.VMEM((2 * half, D_MODEL), BF16)],
        compiler_params=_params(("arbitrary",)),
        name="outproj_ln_router",
    )(mix_a, mix_b, w_out, x_res, ln_g, ln_b, rwt_hi, rwt_lo, rb)


def _rank_kernel(e_ref, rank_ref, cnt_ref, base_ref, *, tn):
    @pl.when(pl.program_id(0) == 0)
    def _():
        base_ref[...] = jnp.zeros(base_ref.shape, F32)

    e = e_ref[...]
    rows = lax.broadcasted_iota(jnp.int32, (N_EXPERTS, tn), 0)
    oh1 = rows == e[0:1, :]
    oh2 = rows == e[1:2, :]
    oh = jnp.where(jnp.logical_or(oh1, oh2), 1.0, 0.0)
    r = lax.broadcasted_iota(jnp.int32, (tn, tn), 0)
    c = lax.broadcasted_iota(jnp.int32, (tn, tn), 1)
    earlier = jnp.where(r < c, 1.0, 0.0).astype(BF16)
    base = base_ref[...]
    before = _dot(oh.astype(BF16), earlier) + jnp.tile(base, (1, tn // LANES))
    r1 = jnp.sum(jnp.where(oh1, before, 0.0), axis=0, keepdims=True)
    r2 = jnp.sum(jnp.where(oh2, before, 0.0), axis=0, keepdims=True)
    rank_ref[...] = jnp.concatenate([r1, r2, jnp.zeros((SUBLANES - 2, tn), F32)], axis=0).astype(jnp.int32)
    base = base + jnp.sum(oh, axis=1, keepdims=True)
    base_ref[...] = base
    cnt_ref[...] = base.astype(jnp.int32)


def _rank(eidx, tn):
    t = eidx.shape[1]
    tn = min(tn, t)
    return pl.pallas_call(
        functools.partial(_rank_kernel, tn=tn),
        grid=(t // tn,),
        in_specs=[pl.BlockSpec((SUBLANES, tn), lambda i: (0, i))],
        out_specs=[pl.BlockSpec((SUBLANES, tn), lambda i: (0, i)),
                   pl.BlockSpec((N_EXPERTS, LANES), lambda i: (0, 0))],
        out_shape=[jax.ShapeDtypeStruct((SUBLANES, t), jnp.int32),
                   jax.ShapeDtypeStruct((N_EXPERTS, LANES), jnp.int32)],
        scratch_shapes=[pltpu.VMEM((N_EXPERTS, LANES), F32)],
        compiler_params=_params(("arbitrary",)),
        name="moe_rank",
    )(eidx)


def _moe_kernel(te_ref, nu_ref, xs_ref, wg_ref, wu_ref, wd_ref, y_ref, wgu_s, wd_s):
    tile = pl.program_id(0)
    used = tile < nu_ref[0]
    fresh = jnp.logical_or(tile == 0, te_ref[tile] != te_ref[jnp.maximum(tile - 1, 0)])

    @pl.when(jnp.logical_and(used, fresh))
    def _():
        wgu_s[:, :D_EXPERT] = wg_ref[...].astype(BF16)
        wgu_s[:, D_EXPERT:] = wu_ref[...].astype(BF16)
        wd_s[...] = wd_ref[...].astype(BF16)

    @pl.when(used)
    def _():
        for s in range(xs_ref.shape[0] // MOE_SUB):
            rows = slice(s * MOE_SUB, (s + 1) * MOE_SUB)
            hgu = _dot(xs_ref[rows, :].astype(BF16), wgu_s[...])
            hg = hgu[:, :D_EXPERT]
            hu = hgu[:, D_EXPERT:]
            h = (hg / (1.0 + jnp.exp(-hg))) * hu
            y_ref[rows, :] = _dot(h.astype(BF16), wd_s[...])

    @pl.when(jnp.logical_not(used))
    def _():
        y_ref[...] = jnp.zeros(y_ref.shape, F32)


def _moe(tile_expert, n_used, xs, w_gate, w_up, w_down, tm):
    t_pad = xs.shape[0]
    grid_spec = pltpu.PrefetchScalarGridSpec(
        num_scalar_prefetch=2,
        grid=(t_pad // tm,),
        in_specs=[
            pl.BlockSpec((tm, D_MODEL), lambda i, te, nu: (i, 0)),
            pl.BlockSpec((None, D_MODEL, D_EXPERT), lambda i, te, nu: (te[i], 0, 0)),
            pl.BlockSpec((None, D_MODEL, D_EXPERT), lambda i, te, nu: (te[i], 0, 0)),
            pl.BlockSpec((None, D_EXPERT, D_MODEL), lambda i, te, nu: (te[i], 0, 0)),
        ],
        out_specs=pl.BlockSpec((tm, D_MODEL), lambda i, te, nu: (i, 0)),
        scratch_shapes=[pltpu.VMEM((D_MODEL, 2 * D_EXPERT), BF16), pltpu.VMEM((D_EXPERT, D_MODEL), BF16)],
    )
    return pl.pallas_call(
        _moe_kernel,
        grid_spec=grid_spec,
        out_shape=jax.ShapeDtypeStruct((t_pad, D_MODEL), F32),
        compiler_params=_params(("arbitrary",)),
        name="moe",
    )(tile_expert, n_used, xs, w_gate, w_up, w_down)


def _moe_ln_kernel(x_ref, y1_ref, y2_ref, w_ref, g_ref, b_ref, o_ref):
    w = w_ref[...]
    v = DEEPNORM_ALPHA * x_ref[...] + (w[:, 0:1] * y1_ref[...] + w[:, 1:2] * y2_ref[...])
    o_ref[...] = _layer_norm(v, g_ref[...], b_ref[...])


def _moe_ln(x, y1, y2, w_cols, ln_g, ln_b, tm):
    t = x.shape[0]
    tm = min(tm, t)
    row = pl.BlockSpec((tm, D_MODEL), lambda i: (i, 0))
    vec = pl.BlockSpec((1, D_MODEL), lambda i: (0, 0))
    return pl.pallas_call(
        _moe_ln_kernel,
        grid=(t // tm,),
        in_specs=[row, row, row, pl.BlockSpec((tm, SUBLANES), lambda i: (i, 0)), vec, vec],
        out_specs=row,
        out_shape=jax.ShapeDtypeStruct((t, D_MODEL), F32),
        compiler_params=_params(("parallel",)),
        name="moe_ln",
    )(x, y1, y2, w_cols, ln_g, ln_b)


def _moe_block(x, eidx, ew, w_gate, w_up, w_down, ln_g, ln_b, tm):
    t = x.shape[0]
    tm = min(tm, t)
    rank, cnt = _rank(eidx, 512)
    counts = cnt[:, 0]
    padded = ((counts + tm - 1) // tm) * tm
    ends = jnp.cumsum(padded)
    starts = ends - padded
    e2 = eidx[:2]
    start_of = jnp.zeros_like(e2)
    for k in range(N_EXPERTS):
        start_of = jnp.where(e2 == k, starts[k], start_of)
    dest = start_of + rank[:2]
    t_pad = 2 * t + N_EXPERTS * tm
    tok = jnp.tile(jnp.arange(t, dtype=jnp.int32), 2)
    src_tok = (jnp.arange(t_pad, dtype=jnp.int32) % t).at[dest.reshape(2 * t)].set(
        tok, mode="promise_in_bounds", unique_indices=True)
    tile_start = jnp.arange(t_pad // tm, dtype=jnp.int32) * tm
    tile_expert = jnp.minimum(jnp.sum((ends[None, :] <= tile_start[:, None]).astype(jnp.int32), axis=1),
                              N_EXPERTS - 1)
    n_used = (ends[-1] // tm).astype(jnp.int32).reshape(1)
    xs = x.at[src_tok].get(mode="promise_in_bounds")
    ys = _moe(tile_expert, n_used, xs, w_gate, w_up, w_down, tm)
    y1 = ys.at[dest[0]].get(mode="promise_in_bounds")
    y2 = ys.at[dest[1]].get(mode="promise_in_bounds")
    return _moe_ln(x, y1, y2, ew.T, ln_g, ln_b, tm)


def kernel(x, router_w, router_b, l0_w_in, l0_diff_lam_q1, l0_diff_lam_k1, l0_diff_lam_q2, l0_diff_lam_k2, l0_diff_subln_g, l0_gla_w_gate, l0_gla_b_gate, l0_gla_norm_g, l0_w_out, l0_ln1_g, l0_ln1_b, l0_moe_w_gate, l0_moe_w_up, l0_moe_w_down, l0_ln2_g, l0_ln2_b, l1_w_in, l1_s5_a_re, l1_s5_a_im, l1_s5_log_step, l1_s5_b_re, l1_s5_b_im, l1_s5_c_re, l1_s5_c_im, l1_s5_d, l1_s5_w_glu, l1_s5_b_glu, l1_w_out, l1_ln1_g, l1_ln1_b, l1_moe_w_gate, l1_moe_w_up, l1_moe_w_down, l1_ln2_g, l1_ln2_b):
    bsz, t, d = x.shape
    assert bsz == 1 and d == D_MODEL
    x0 = x.reshape(t, d)
    vec = lambda p: p.reshape(1, -1).astype(F32)
    rwt = router_w.T.astype(F32)
    rb = router_b.reshape(N_EXPERTS, 1).astype(F32)

    lambda_init = 0.8 - 0.6 * math.exp(-0.3 * 0)
    c = [0, 1024, 2048, 3072, 3584, 4096, 5120, 5136, 6160]
    q_scale = math.log2(math.e) * DIFF_HEAD_DIM ** -0.5
    w0 = jnp.concatenate([l0_w_in[:, :c[1]] * q_scale, l0_w_in[:, c[1]:c[6]], l0_w_in[:, c[7]:c[8]],
                          l0_w_in[:, c[6]:c[7]], jnp.zeros((d, L0_GB_PAD - GLA_GATE_RANK), l0_w_in.dtype)],
                         axis=1).astype(BF16)
    h0 = _matmul(x0, w0, 1024, L0_IN_TN, BF16)
    lam_vec = jnp.stack([l0_diff_lam_q1, l0_diff_lam_k1, l0_diff_lam_q2, l0_diff_lam_k2]).astype(F32)
    o_a = _diff_attention(h0, lam_vec, vec(l0_diff_subln_g), lambda_init, 1024)
    wg_pad = jnp.concatenate([l0_gla_w_gate, jnp.zeros((LANES - GLA_GATE_RANK, GLA_HEADS * GLA_DK), F32)],
                             axis=0).astype(BF16)
    o_b = _gla(h0, wg_pad, vec(l0_gla_b_gate), vec(l0_gla_norm_g), 256)
    x1, eidx, ew = _outproj_ln_router(o_a, o_b, l0_w_out, x0, vec(l0_ln1_g), vec(l0_ln1_b),
                                      rwt, rb, 512, 128)
    x2 = _moe_block(x1, eidx, ew, l0_moe_w_gate, l0_moe_w_up, l0_moe_w_down, vec(l0_ln2_g), vec(l0_ln2_b), 256)

    h1 = _matmul(x2, l1_w_in.astype(BF16), 1024, 1024, BF16)
    o_c = _stick_breaking(h1, 256)
    lbr, lbi, bbr, bbi = _s5_prep(l1_s5_a_re, l1_s5_a_im, l1_s5_log_step, l1_s5_b_re, l1_s5_b_im)
    bd, cd, pw8 = _s5_tile_params(lbr[::S5_GROUP], lbi[::S5_GROUP], bbr.reshape(S5_GROUPS, S5_GROUP, S5_STATE),
                                  bbi.reshape(S5_GROUPS, S5_GROUP, S5_STATE),
                                  l1_s5_c_re.astype(F32), l1_s5_c_im.astype(F32))
    y_s5 = _s5_core(h1, bd, cd, pw8, 256)
    o_d = _s5_glu(y_s5, h1, vec(l1_s5_d), l1_s5_w_glu.astype(BF16), vec(l1_s5_b_glu), 512)
    x3, eidx1, ew1 = _outproj_ln_router(o_c, o_d, l1_w_out, x2, vec(l1_ln1_g), vec(l1_ln1_b),
                                        rwt, rb, 512, 128)
    x4 = _moe_block(x3, eidx1, ew1, l1_moe_w_gate, l1_moe_w_up, l1_moe_w_down, vec(l1_ln2_g), vec(l1_ln2_b), 256)
    return x4.reshape(bsz, t, d)
```

```python
import functools
import math

import jax
import jax.numpy as jnp
from jax import lax
from jax.experimental import pallas as pl
from jax.experimental.pallas import tpu as pltpu

F32 = jnp.float32
BF16 = jnp.bfloat16

D_MODEL = 2048
DEPTH = 2
DIFF_HEADS = 8
DIFF_HEAD_DIM = 64
DIFF_V_DIM = 128
GLA_HEADS = 4
GLA_DK = 128
GLA_DV = 256
GLA_GATE_RANK = 16
GLA_GATE_TEMP = 16.0
GLA_CHUNK = 64
SB_HEADS = 8
SB_HEAD_DIM = 128
S5_CHANNELS = 1024
S5_GROUP = 16
S5_GROUPS = 64
S5_STATE = 64
N_EXPERTS = 16
N_GROUPS = 4
EXPERTS_PER_GROUP = 4
D_EXPERT = 640
DEEPNORM_ALPHA = (2.0 * DEPTH) ** 0.25
LN_EPS = 1e-5

LANES = 128
SUBLANES = 8
VMEM_LIMIT = 56 * 1024 * 1024
SB_LOG_FLOOR = 120.0
SB_HEADS_PER_STEP = 4
MOE_PARTS = 2

MXU_N = 256
L0_GB_PAD = MXU_N
L0_IN_TN = 5 * MXU_N

S5_KT = 4
S5_KT_CH = S5_CHANNELS // S5_KT
S5_KT_STATES = (S5_GROUPS // S5_KT) * S5_STATE
S5_SCAN_W = 512
S5_L = 8


def _params(sem, flags=None):
    return pltpu.CompilerParams(dimension_semantics=sem, vmem_limit_bytes=VMEM_LIMIT, flags=flags)


def _nt_dot(a, b):
    return lax.dot_general(a, b, (((1,), (1,)), ((), ())), preferred_element_type=F32)


def _dot(a, b):
    return jnp.dot(a, b, preferred_element_type=F32)


def _split_dot_left(m, x):
    hi = x.astype(BF16)
    lo = (x - hi.astype(F32)).astype(BF16)
    return _dot(m, hi) + _dot(m, lo)


def _layer_norm(v, g, b):
    mu = jnp.mean(v, axis=-1, keepdims=True)
    d = v - mu
    var = jnp.mean(d * d, axis=-1, keepdims=True)
    return d * lax.rsqrt(var + LN_EPS) * g + b


def _mm_kernel(a_ref, b_ref, o_ref, a_bf):
    @pl.when(pl.program_id(1) == 0)
    def _():
        a_bf[...] = a_ref[...].astype(BF16)

    o_ref[...] = _dot(a_bf[...], b_ref[...]).astype(o_ref.dtype)


def _matmul(a, b, tm, tn, out_dtype):
    m, k = a.shape
    n = b.shape[1]
    tm = min(tm, m)
    return pl.pallas_call(
        _mm_kernel,
        grid=(m // tm, n // tn),
        in_specs=[pl.BlockSpec((tm, k), lambda i, j: (i, 0)),
                  pl.BlockSpec((k, tn), lambda i, j: (0, j))],
        out_specs=pl.BlockSpec((tm, tn), lambda i, j: (i, j)),
        out_shape=jax.ShapeDtypeStruct((m, n), out_dtype),
        scratch_shapes=[pltpu.VMEM((tm, k), BF16)],
        compiler_params=_params(("parallel", "arbitrary")),
        name="in_proj",
    )(a, b)


def _diff_attn_kernel(lam_ref, g_ref, q_ref, k_ref, v_ref, o_ref, vx_ref, m_ref, acc_ref, sa_ref, sb_ref, *, tq,
                      lambda_init):
    i = pl.program_id(1)
    dv = DIFF_V_DIM

    @pl.when(i == 0)
    def _():
        vx_ref[:, :dv] = v_ref[...]
        vx_ref[:, dv:] = jnp.ones((vx_ref.shape[0], dv), BF16)

    m_ref[...] = jnp.full(m_ref.shape, -jnp.inf, F32)
    acc_ref[...] = jnp.zeros(acc_ref.shape, F32)
    q = q_ref[...]
    lane = lax.broadcasted_iota(jnp.int32, q.shape, 1)
    zero = jnp.zeros_like(q)
    halves = (jnp.where(lane < DIFF_HEAD_DIM, q, zero), jnp.where(lane >= DIFF_HEAD_DIM, q, zero))

    def scores(j, mi):
        off = pl.multiple_of(j * tq, tq)
        return _nt_dot(halves[mi], k_ref[pl.ds(off, tq), :])

    def softmax_pv(j, get_s, masked):
        off = pl.multiple_of(j * tq, tq)
        vx = vx_ref[pl.ds(off, tq), :]
        if masked:
            row = lax.broadcasted_iota(jnp.int32, (tq, tq), 0)
            col = lax.broadcasted_iota(jnp.int32, (tq, tq), 1)
            causal = col <= row
        for mi in range(2):
            s = get_s(mi)
            if masked:
                s = jnp.where(causal, s, -jnp.inf)
            m_prev = m_ref[mi]
            m_new = jnp.maximum(m_prev, jnp.max(s, axis=1, keepdims=True))
            p = jnp.exp2(s - jnp.tile(m_new, (1, tq // LANES)))
            alpha = jnp.exp2(m_prev - m_new)
            acc_ref[mi] = jnp.tile(alpha, (1, 2)) * acc_ref[mi] + _dot(p.astype(BF16), vx)
            m_ref[mi] = m_new

    def scores_to(j, s_ref):
        for mi in range(2):
            s_ref[mi] = scores(j, mi)

    scores_to(0, sa_ref)

    def pair(t, carry):
        j = 2 * t
        scores_to(j + 1, sb_ref)
        softmax_pv(j, lambda mi: sa_ref[mi], False)
        scores_to(j + 2, sa_ref)
        softmax_pv(j + 1, lambda mi: sb_ref[mi], False)
        return carry

    lax.fori_loop(0, i // 2, pair, 0)

    @pl.when(i % 2 == 1)
    def _():
        softmax_pv(i - 1, lambda mi: sa_ref[mi], False)
        scores_to(i, sa_ref)

    softmax_pv(i, lambda mi: sa_ref[mi], True)

    lv = lam_ref[...]
    lam = (jnp.exp(jnp.sum(lv[0:1] * lv[1:2], axis=1, keepdims=True))
           - jnp.exp(jnp.sum(lv[2:3] * lv[3:4], axis=1, keepdims=True)) + lambda_init)
    a0 = acc_ref[0]
    a1 = acc_ref[1]
    o = a0[:, :dv] / a0[:, dv:] - lam * (a1[:, :dv] / a1[:, dv:])
    ms = jnp.mean(o * o, axis=-1, keepdims=True)
    o = o * lax.rsqrt(ms + LN_EPS) * g_ref[...] * (1.0 - lambda_init)
    o_ref[...] = o.astype(o_ref.dtype)


def _diff_attention(h0, lam_vec, subln_g, lambda_init, tq):
    t = h0.shape[0]
    tq = min(tq, t)
    nq = t // tq
    kern = functools.partial(_diff_attn_kernel, tq=tq, lambda_init=lambda_init)
    return pl.pallas_call(
        kern,
        grid=(DIFF_HEADS, nq),
        in_specs=[
            pl.BlockSpec((4, DIFF_HEAD_DIM), lambda h, i: (0, 0)),
            pl.BlockSpec((1, DIFF_V_DIM), lambda h, i: (0, 0)),
            pl.BlockSpec((tq, LANES), lambda h, i: (i, h)),
            pl.BlockSpec((t, LANES), lambda h, i: (0, DIFF_HEADS + h)),
            pl.BlockSpec((t, LANES), lambda h, i: (0, 2 * DIFF_HEADS + h)),
        ],
        out_specs=pl.BlockSpec((tq, LANES), lambda h, i: (i, h)),
        out_shape=jax.ShapeDtypeStruct((t, DIFF_HEADS * DIFF_V_DIM), BF16),
        scratch_shapes=[pltpu.VMEM((t, 2 * DIFF_V_DIM), BF16), pltpu.VMEM((2, tq, LANES), F32),
                        pltpu.VMEM((2, tq, 2 * DIFF_V_DIM), F32),
                        pltpu.VMEM((2, tq, tq), F32), pltpu.VMEM((2, tq, tq), F32)],
        compiler_params=_params(("arbitrary", "arbitrary")),
        name="diff_attn",
    )(lam_vec, subln_g, h0, h0, h0)


def _gla_kernel(q_ref, k_ref, v_ref, r_ref, gb_ref, wg_ref, bg_ref, ng_ref, o_ref, st_ref, *, tg):
    @pl.when(pl.program_id(0) == 0)
    def _():
        st_ref[...] = jnp.zeros(st_ref.shape, F32)

    c = GLA_CHUNK
    gate = _dot(gb_ref[...], wg_ref[...]) + bg_ref[...]
    log_a = -(jnp.maximum(-gate, 0.0) + jnp.log(1.0 + jnp.exp(-jnp.abs(gate)))) / GLA_GATE_TEMP
    row = lax.broadcasted_iota(jnp.int32, (c, c), 0)
    col = lax.broadcasted_iota(jnp.int32, (c, c), 1)
    tril = row >= col
    tri = jnp.where(tril, 1.0, 0.0).astype(BF16)
    for h in range(GLA_HEADS):
        ksl = slice(h * GLA_DK, (h + 1) * GLA_DK)
        vsl = slice(h * GLA_DV, (h + 1) * GLA_DV)
        for ci in range(tg // c):
            rs = slice(ci * c, (ci + 1) * c)
            la = log_a[rs, ksl]
            b = _split_dot_left(tri, la)
            b_last = b[c - 1:c, :]
            qf = q_ref[rs, ksl].astype(F32) * (GLA_DK ** -0.5)
            kf = k_ref[rs, ksl].astype(F32)
            v = v_ref[rs, vsl]
            q_dec = (qf * jnp.exp(b)).astype(BF16)
            k_dec = (kf * jnp.exp(-b)).astype(BF16)
            k_end = (kf * jnp.exp(b_last - b)).astype(BF16)
            scores = jnp.where(tril, _nt_dot(q_dec, k_dec), 0.0)
            st = st_ref[h]
            o = _dot(scores.astype(BF16), v) + _nt_dot(q_dec, st.astype(BF16))
            kv_t = lax.dot_general(v, k_end, (((0,), (0,)), ((), ())), preferred_element_type=F32)
            st_ref[h] = jnp.exp(b_last) * st + kv_t
            ms = jnp.mean(o * o, axis=-1, keepdims=True)
            o = o * lax.rsqrt(ms + LN_EPS) * ng_ref[...]
            r = r_ref[rs, vsl].astype(F32)
            o_ref[rs, vsl] = (o * (r / (1.0 + jnp.exp(-r)))).astype(o_ref.dtype)


def _gla(h0, wg_pad, b_gate, norm_g, tg):
    t = h0.shape[0]
    tg = min(tg, t)
    qk = GLA_HEADS * GLA_DK
    vw = GLA_HEADS * GLA_DV
    return pl.pallas_call(
        functools.partial(_gla_kernel, tg=tg),
        grid=(t // tg,),
        in_specs=[
            pl.BlockSpec((tg, qk), lambda i: (i, 3072 // qk)),
            pl.BlockSpec((tg, qk), lambda i: (i, 3584 // qk)),
            pl.BlockSpec((tg, vw), lambda i: (i, 4096 // vw)),
            pl.BlockSpec((tg, vw), lambda i: (i, 5120 // vw)),
            pl.BlockSpec((tg, LANES), lambda i: (i, 6144 // LANES)),
            pl.BlockSpec((LANES, qk), lambda i: (0, 0)),
            pl.BlockSpec((1, qk), lambda i: (0, 0)),
            pl.BlockSpec((1, GLA_DV), lambda i: (0, 0)),
        ],
        out_specs=pl.BlockSpec((tg, vw), lambda i: (i, 0)),
        out_shape=jax.ShapeDtypeStruct((t, vw), BF16),
        scratch_shapes=[pltpu.VMEM((GLA_HEADS, GLA_DV, GLA_DK), F32)],
        compiler_params=_params(("arbitrary",)),
        name="gla",
    )(h0, h0, h0, h0, h0, wg_pad, b_gate, norm_g)


def _sb_kernel(q_ref, k_ref, v_ref, o_ref, acc_ref, c_ref, *, tq):
    i = pl.program_id(1)
    hd = SB_HEAD_DIM
    acc_ref[...] = jnp.zeros(acc_ref.shape, F32)
    c_ref[...] = jnp.zeros(c_ref.shape, F32)
    row = lax.broadcasted_iota(jnp.int32, (tq, tq), 0)
    col = lax.broadcasted_iota(jnp.int32, (tq, tq), 1)
    strict = col < row
    tri = jnp.where(row > col, 1.0, 0.0).astype(BF16)

    def block(jb, masked):
        off = pl.multiple_of(jb * tq, tq)
        cmax = None
        for hh in range(SB_HEADS_PER_STEP):
            cols = slice(hh * hd, (hh + 1) * hd)
            k = k_ref[pl.ds(off, tq), cols]
            v = v_ref[pl.ds(off, tq), cols]
            z = _nt_dot(q_ref[:, cols], k) * (SB_HEAD_DIM ** -0.5)
            lsn = -(jnp.maximum(z, 0.0) + jnp.log(1.0 + jnp.exp(-jnp.abs(z))))
            lm = jnp.where(strict, lsn, 0.0) if masked else lsn
            excl = _dot(lm.astype(BF16), tri) + c_ref[hh]
            w = jnp.exp(z + lsn + excl)
            if masked:
                w = jnp.where(strict, w, 0.0)
            acc_ref[:, cols] += _dot(w.astype(BF16), v)
            c_new = c_ref[hh] + jnp.sum(lm, axis=1, keepdims=True)
            c_ref[hh] = c_new
            hmax = jnp.max(c_new)
            cmax = hmax if cmax is None else jnp.maximum(cmax, hmax)
        return cmax

    cmax = block(i, True)

    def cond(carry):
        jb, cm = carry
        return jnp.logical_and(jb >= 0, cm > -SB_LOG_FLOOR)

    def body(carry):
        jb, _ = carry
        return jb - 1, block(jb, False)

    lax.while_loop(cond, body, (i - 1, cmax))
    o_ref[...] = acc_ref[...].astype(o_ref.dtype)


def _stick_breaking(h1, tq):
    t = h1.shape[0]
    tq = min(tq, t)
    ng = SB_HEADS // SB_HEADS_PER_STEP
    wd = SB_HEADS_PER_STEP * SB_HEAD_DIM
    return pl.pallas_call(
        functools.partial(_sb_kernel, tq=tq),
        grid=(ng, t // tq),
        in_specs=[
            pl.BlockSpec((tq, wd), lambda h, i: (i, h)),
            pl.BlockSpec((t, wd), lambda h, i: (0, ng + h), pipeline_mode=pl.Buffered(1)),
            pl.BlockSpec((t, wd), lambda h, i: (0, 2 * ng + h), pipeline_mode=pl.Buffered(1)),
        ],
        out_specs=pl.BlockSpec((tq, wd), lambda h, i: (i, h)),
        out_shape=jax.ShapeDtypeStruct((t, SB_HEADS * SB_HEAD_DIM), BF16),
        scratch_shapes=[pltpu.VMEM((tq, wd), F32), pltpu.VMEM((SB_HEADS_PER_STEP, tq, 1), F32)],
        compiler_params=_params(("parallel", "arbitrary")),
        name="stick_breaking",
    )(h1, h1, h1)


def _s5_prep_kernel(are_ref, aim_ref, ls_ref, bre_ref, bim_ref, lbr_ref, lbi_ref, bbr_ref, bbi_ref):
    step = jnp.exp(ls_ref[...])
    lam_re = are_ref[...]
    lam_im = aim_ref[...]
    mag = jnp.exp(step * lam_re)
    lb_re = mag * jnp.cos(step * lam_im)
    lb_im = mag * jnp.sin(step * lam_im)
    den = lam_re * lam_re + lam_im * lam_im
    n_re = lb_re - 1.0
    f_re = (n_re * lam_re + lb_im * lam_im) / den
    f_im = (lb_im * lam_re - n_re * lam_im) / den
    lbr_ref[...] = lb_re
    lbi_ref[...] = lb_im
    bbr_ref[...] = f_re * bre_ref[...] - f_im * bim_ref[...]
    bbi_ref[...] = f_re * bim_ref[...] + f_im * bre_ref[...]


def _s5_prep(a_re, a_im, log_step, b_re, b_im):
    rows = S5_GROUPS * S5_GROUP
    rep = lambda p: jnp.repeat(p, S5_GROUP, axis=0)
    to_rows = lambda b: jnp.transpose(b, (0, 2, 1)).reshape(rows, S5_STATE)
    spec = pl.BlockSpec((rows, S5_STATE), lambda: (0, 0))
    shp = jax.ShapeDtypeStruct((rows, S5_STATE), F32)
    return pl.pallas_call(
        _s5_prep_kernel,
        in_specs=[spec, spec, pl.BlockSpec((rows, 1), lambda: (0, 0)), spec, spec],
        out_specs=[spec] * 4,
        out_shape=[shp] * 4,
        name="s5_prep",
    )(rep(a_re), rep(a_im), rep(log_step[:, None]), to_rows(b_re), to_rows(b_im))


def _cmul(ar, ai, br, bi):
    return ar * br - ai * bi, ar * bi + ai * br


def _block_diag_tiles(m, rows_per_group, cols_per_group):
    gpt = S5_GROUPS // S5_KT
    m = m.reshape(S5_KT, gpt, rows_per_group, cols_per_group)
    eye = jnp.eye(gpt, dtype=m.dtype)
    bd = jnp.einsum("kgrc,gh->kgrhc", m, eye)
    return bd.reshape(S5_KT, gpt * rows_per_group, gpt * cols_per_group)


def _s5_tile_params(lb_re, lb_im, bb_re, bb_im, c_re, c_im):
    bd = jnp.concatenate([_block_diag_tiles(bb_re, S5_GROUP, S5_STATE),
                          _block_diag_tiles(bb_im, S5_GROUP, S5_STATE)], axis=2)
    cd = jnp.concatenate([_block_diag_tiles(jnp.transpose(c_re, (0, 2, 1)), S5_STATE, S5_GROUP),
                          _block_diag_tiles(-jnp.transpose(c_im, (0, 2, 1)), S5_STATE, S5_GROUP)], axis=1)
    pr = [jnp.ones_like(lb_re)]
    pi = [jnp.zeros_like(lb_re)]
    for _ in range(S5_L):
        nr, ni = _cmul(pr[-1], pi[-1], lb_re, lb_im)
        pr.append(nr)
        pi.append(ni)

    def table(p):
        rows = jnp.transpose(jnp.stack(p).reshape(S5_L + 1, S5_KT, S5_KT_STATES), (1, 0, 2))
        return jnp.pad(rows, ((0, 0), (0, LANES - (S5_L + 1)), (0, 0)))

    return bd, cd, jnp.stack([table(pr), table(pi)], axis=1)


def _s5_core_kernel(u_ref, bd_ref, cd_ref, pw8_ref, y_ref, hs_ref, hp_ref, pw_ref, st_ref, uf_ref,
                    mt_s, bbs_s, ccs_s, *, tcr):
    ns = S5_KT_STATES
    pad = SUBLANES
    cw = S5_KT_CH

    @pl.when(pl.program_id(1) == 0)
    def _():
        b_re = bd_ref[:, :ns]
        b_im = bd_ref[:, ns:]
        cd_hi = cd_ref[...].astype(BF16)
        cd_lo = (cd_ref[...] - cd_hi.astype(F32)).astype(BF16)
        for k in range(S5_L):
            a = pw8_ref[0, k:k + 1, :]
            b = pw8_ref[1, k:k + 1, :]
            dk = jnp.concatenate([b_re * a - b_im * b, b_re * b + b_im * a], axis=1)
            s = S5_L - 1 - k
            bbs_s[s * cw:(s + 1) * cw, :] = dk.astype(BF16)
            dk_hi = dk.astype(BF16)
            dk_lo = (dk - dk_hi.astype(F32)).astype(BF16)
            kk = (_dot(dk_hi, cd_hi) + (_dot(dk_lo, cd_hi) + _dot(dk_hi, cd_lo))).astype(BF16)
            for s0 in range(S5_L - k):
                mt_s[s0 * cw:(s0 + 1) * cw, (s0 + k) * cw:(s0 + k + 1) * cw] = kk
        pw_t = (jnp.transpose(pw8_ref[0]), jnp.transpose(pw8_ref[1]))
        c_re = cd_ref[:ns, :]
        c_im = cd_ref[ns:, :]
        for i in range(S5_L):
            a = pw_t[0][:, i + 1:i + 2]
            b = pw_t[1][:, i + 1:i + 2]
            ccs_s[:ns, i * cw:(i + 1) * cw] = (c_re * a + c_im * b).astype(BF16)
            ccs_s[ns:, i * cw:(i + 1) * cw] = (c_im * a - c_re * b).astype(BF16)

        hp_ref[...] = jnp.zeros(hp_ref.shape, F32)
        hs_ref[0:pad, :] = jnp.zeros((pad, 2 * ns), F32)
        l1r = jnp.broadcast_to(pw8_ref[0, S5_L:S5_L + 1, :], (SUBLANES, ns))
        l1i = jnp.broadcast_to(pw8_ref[1, S5_L:S5_L + 1, :], (SUBLANES, ns))
        l2r, l2i = _cmul(l1r, l1i, l1r, l1i)
        l4r, l4i = _cmul(l2r, l2i, l2r, l2i)
        l8r, l8i = _cmul(l4r, l4i, l4r, l4i)
        ridx = lax.broadcasted_iota(jnp.int32, l1r.shape, 0)
        n = ridx + 1
        pr = jnp.ones_like(l1r)
        pi = jnp.zeros_like(l1r)
        for bit, (lr, li) in ((1, (l1r, l1i)), (2, (l2r, l2i)), (4, (l4r, l4i)), (8, (l8r, l8i))):
            nr, ni = _cmul(pr, pi, lr, li)
            take = (n & bit) != 0
            pr = jnp.where(take, nr, pr)
            pi = jnp.where(take, ni, pi)
        pw_ref[0] = pr
        pw_ref[1] = pi
        for s, (lr, li) in enumerate(((l1r, l1i), (l2r, l2i), (l4r, l4i))):
            keep = ridx >= (1 << s)
            st_ref[2 * s] = jnp.where(keep, lr, 0.0)
            st_ref[2 * s + 1] = jnp.where(keep, li, 0.0)

    nl = S5_KT_CH // LANES
    for h in range(nl):
        uf_ref[h] = u_ref[:, h * LANES:(h + 1) * LANES].astype(F32)
    uc = jnp.concatenate([uf_ref[h, pl.ds(s, tcr, stride=S5_L), :].astype(BF16)
                          for s in range(S5_L) for h in range(nl)], axis=1)
    hs_ref[pad:pad + tcr, :] = _dot(uc, bbs_s[...])
    w = S5_SCAN_W
    for cc in range(ns // w):
        re0 = cc * w
        im0 = ns + cc * w

        def body(r, carry, re0=re0, im0=im0):
            hpr, hpi = carry
            rows = pl.ds(pl.multiple_of(pad + r * SUBLANES, SUBLANES), SUBLANES)
            xr = hs_ref[rows, re0:re0 + w]
            xi = hs_ref[rows, im0:im0 + w]
            for s in range(3):
                sr = pltpu.roll(xr, 1 << s, 0)
                si = pltpu.roll(xi, 1 << s, 0)
                ar = st_ref[2 * s, :, re0:re0 + w]
                ai = st_ref[2 * s + 1, :, re0:re0 + w]
                xr, xi = xr + ar * sr - ai * si, xi + ar * si + ai * sr
            pr = pw_ref[0, :, re0:re0 + w]
            pi = pw_ref[1, :, re0:re0 + w]
            xr, xi = xr + pr * hpr - pi * hpi, xi + pr * hpi + pi * hpr
            hs_ref[rows, re0:re0 + w] = xr
            hs_ref[rows, im0:im0 + w] = xi
            return (jnp.broadcast_to(xr[SUBLANES - 1:SUBLANES, :], (SUBLANES, w)),
                    jnp.broadcast_to(xi[SUBLANES - 1:SUBLANES, :], (SUBLANES, w)))

        hpr, hpi = lax.fori_loop(0, tcr // SUBLANES, body, (hp_ref[:, re0:re0 + w], hp_ref[:, im0:im0 + w]))
        hp_ref[:, re0:re0 + w] = hpr
        hp_ref[:, im0:im0 + w] = hpi

    h_prev = hs_ref[pad - 1:pad - 1 + tcr, :]
    y_in = _dot(h_prev.astype(BF16), ccs_s[...])
    for i in range(S5_L):
        k_hi = (i + 1) * cw
        y_i = y_in[:, i * cw:k_hi] + _dot(uc[:, :k_hi], mt_s[:k_hi, i * cw:k_hi])
        for h in range(nl):
            uf_ref[h, pl.ds(i, tcr, stride=S5_L), :] = y_i[:, h * LANES:(h + 1) * LANES]
    for h in range(nl):
        y_ref[:, h * LANES:(h + 1) * LANES] = uf_ref[h]
    hs_ref[pad - 1:pad, :] = hs_ref[pad + tcr - 1:pad + tcr, :]


def _s5_core(h1, bd, cd, pw8, tcr):
    t = h1.shape[0]
    width = S5_L * S5_KT_CH
    tcr = min(tcr, t // S5_L)
    tc = tcr * S5_L
    ns = S5_KT_STATES
    u_col0 = 3 * S5_CHANNELS // S5_KT_CH
    once = pl.Buffered(1)
    return pl.pallas_call(
        functools.partial(_s5_core_kernel, tcr=tcr),
        grid=(S5_KT, t // tc),
        in_specs=[
            pl.BlockSpec((tc, S5_KT_CH), lambda k, i: (i, u_col0 + k)),
            pl.BlockSpec((None, S5_KT_CH, 2 * ns), lambda k, i: (k, 0, 0), pipeline_mode=once),
            pl.BlockSpec((None, 2 * ns, S5_KT_CH), lambda k, i: (k, 0, 0), pipeline_mode=once),
            pl.BlockSpec((None, 2, LANES, ns), lambda k, i: (k, 0, 0, 0), pipeline_mode=once),
        ],
        out_specs=pl.BlockSpec((tc, S5_KT_CH), lambda k, i: (i, k)),
        out_shape=jax.ShapeDtypeStruct((t, S5_CHANNELS), F32),
        scratch_shapes=[pltpu.VMEM((tcr + SUBLANES, 2 * ns), F32), pltpu.VMEM((SUBLANES, 2 * ns), F32),
                        pltpu.VMEM((2, SUBLANES, ns), F32), pltpu.VMEM((6, SUBLANES, ns), F32),
                        pltpu.VMEM((S5_KT_CH // LANES, tc, LANES), F32),
                        pltpu.VMEM((width, width), BF16), pltpu.VMEM((width, 2 * ns), BF16),
                        pltpu.VMEM((2 * ns, width), BF16)],
        compiler_params=_params(("arbitrary", "arbitrary")),
        name="s5_core",
    )(h1, bd, cd, pw8)


def _s5_glu_kernel(y_ref, u_ref, d_ref, wglu_ref, bglu_ref, o_ref):
    y = y_ref[...] + d_ref[...] * u_ref[...].astype(F32)
    g = 0.5 * y * (1.0 + jnp.tanh(math.sqrt(2.0 / math.pi) * (y + 0.044715 * (y * y * y))))
    gl = _dot(g.astype(BF16), wglu_ref[...]) + bglu_ref[...]
    o_ref[...] = (g / (1.0 + jnp.exp(-gl))).astype(o_ref.dtype)


def _s5_glu(y, h1, d_skip, w_glu, b_glu, tm):
    t = h1.shape[0]
    tm = min(tm, t)
    full = lambda shape: pl.BlockSpec(shape, lambda i: (0,) * len(shape))
    return pl.pallas_call(
        _s5_glu_kernel,
        grid=(t // tm,),
        in_specs=[pl.BlockSpec((tm, S5_CHANNELS), lambda i: (i, 0)),
                  pl.BlockSpec((tm, S5_CHANNELS), lambda i: (i, 3)),
                  full((1, S5_CHANNELS)), full((S5_CHANNELS, S5_CHANNELS)), full((1, S5_CHANNELS))],
        out_specs=pl.BlockSpec((tm, S5_CHANNELS), lambda i: (i, 0)),
        out_shape=jax.ShapeDtypeStruct((t, S5_CHANNELS), BF16),
        compiler_params=_params(("parallel",)),
        name="s5_glu",
    )(y, h1, d_skip, w_glu, b_glu)


def _route(probs, sel):
    rows = [sel[e:e + 1, :] for e in range(N_EXPERTS)]
    prow = [probs[e:e + 1, :] for e in range(N_EXPERTS)]
    best_score = None
    for g in range(N_GROUPS):
        a = rows[g * EXPERTS_PER_GROUP:(g + 1) * EXPERTS_PER_GROUP]
        score = None
        for x in range(EXPERTS_PER_GROUP):
            for y in range(x + 1, EXPERTS_PER_GROUP):
                pair = a[x] + a[y]
                score = pair if score is None else jnp.maximum(score, pair)
        if best_score is None:
            best_score, gbest = score, jnp.zeros(score.shape, jnp.int32)
            cs = list(a)
            cp = prow[0:EXPERTS_PER_GROUP]
        else:
            better = score > best_score
            best_score = jnp.where(better, score, best_score)
            gbest = jnp.where(better, g, gbest)
            cs = [jnp.where(better, a[x], cs[x]) for x in range(EXPERTS_PER_GROUP)]
            cp = [jnp.where(better, prow[g * EXPERTS_PER_GROUP + x], cp[x]) for x in range(EXPERTS_PER_GROUP)]

    def first_argmax(vals):
        bv, bi = vals[0], jnp.zeros(vals[0].shape, jnp.int32)
        for x in range(1, len(vals)):
            better = vals[x] > bv
            bv = jnp.where(better, vals[x], bv)
            bi = jnp.where(better, x, bi)
        return bi

    i1 = first_argmax(cs)
    i2 = first_argmax([jnp.where(i1 == x, -jnp.inf, cs[x]) for x in range(EXPERTS_PER_GROUP)])

    def pick(idx):
        out = cp[0]
        for x in range(1, EXPERTS_PER_GROUP):
            out = jnp.where(idx == x, cp[x], out)
        return out

    p1, p2 = pick(i1), pick(i2)
    tot = p1 + p2
    return gbest * EXPERTS_PER_GROUP + i1, gbest * EXPERTS_PER_GROUP + i2, p1 / tot, p2 / tot


def _ln_router_tail(v, g_ref, b_ref, rwh_ref, rwl_ref, rb_ref):
    xn = _layer_norm(v, g_ref[...], b_ref[...])
    xh = xn.astype(BF16)
    xl = (xn - xh.astype(F32)).astype(BF16)
    logits = _nt_dot(rwh_ref[...], xh) + (_nt_dot(rwl_ref[...], xh) + _nt_dot(rwh_ref[...], xl))
    mx = jnp.max(logits, axis=0, keepdims=True)
    ex = jnp.exp(logits - mx)
    probs = ex / jnp.sum(ex, axis=0, keepdims=True)
    e1, e2, w1, w2 = _route(probs, probs + rb_ref[...])
    n = e1.shape[1]
    ei = jnp.concatenate([e1, e2, jnp.zeros((SUBLANES - 2, n), jnp.int32)], axis=0)
    ew = jnp.concatenate([w1, w2, jnp.zeros((SUBLANES - 2, n), F32)], axis=0)
    return xn, ei, ew


def _outproj_kernel(ma_ref, mb_ref, w_ref, xr_ref, g_ref, b_ref, rwh_ref, rwl_ref, rb_ref,
                    x_ref, ei_ref, ew_ref, w_bf, *, sub):
    @pl.when(pl.program_id(0) == 0)
    def _():
        w_bf[...] = w_ref[...].astype(BF16)

    half = w_bf.shape[0] // 2
    for s in range(x_ref.shape[0] // sub):
        rows = slice(s * sub, (s + 1) * sub)
        v = (DEEPNORM_ALPHA * xr_ref[rows, :] + _dot(ma_ref[rows, :], w_bf[:half, :])
             + _dot(mb_ref[rows, :], w_bf[half:, :]))
        xn, ei, ew = _ln_router_tail(v, g_ref, b_ref, rwh_ref, rwl_ref, rb_ref)
        x_ref[rows, :] = xn
        ei_ref[:, rows] = ei
        ew_ref[:, rows] = ew


def _outproj_ln_router(mix_a, mix_b, w_out, x_res, ln_g, ln_b, rwt, rb, tm, sub):
    t = x_res.shape[0]
    rwt_hi = rwt.astype(BF16)
    rwt_lo = (rwt - rwt_hi.astype(F32)).astype(BF16)
    tm = min(tm, t)
    sub = min(sub, tm)
    half = w_out.shape[0] // 2
    row = lambda w: pl.BlockSpec((tm, w), lambda i: (i, 0))
    full = lambda shape: pl.BlockSpec(shape, lambda i: (0,) * len(shape))
    return pl.pallas_call(
        functools.partial(_outproj_kernel, sub=sub),
        grid=(t // tm,),
        in_specs=[row(half), row(half),
                  pl.BlockSpec((2 * half, D_MODEL), lambda i: (0, 0), pipeline_mode=pl.Buffered(1)),
                  row(D_MODEL), full((1, D_MODEL)), full((1, D_MODEL)),
                  full((N_EXPERTS, D_MODEL)), full((N_EXPERTS, D_MODEL)), full((N_EXPERTS, 1))],
        out_specs=[row(D_MODEL),
                   pl.BlockSpec((SUBLANES, tm), lambda i: (0, i)), pl.BlockSpec((SUBLANES, tm), lambda i: (0, i))],
        out_shape=[jax.ShapeDtypeStruct((t, D_MODEL), F32),
                   jax.ShapeDtypeStruct((SUBLANES, t), jnp.int32), jax.ShapeDtypeStruct((SUBLANES, t), F32)],
        scratch_shapes=[pltpu.VMEM((2 * half, D_MODEL), BF16)],
        compiler_params=_params(("arbitrary",)),
        name="outproj_ln_router",
    )(mix_a, mix_b, w_out, x_res, ln_g, ln_b, rwt_hi, rwt_lo, rb)


def _rank_kernel(e_ref, rank_ref, cnt_ref, base_ref, *, tn):
    @pl.when(pl.program_id(0) == 0)
    def _():
        base_ref[...] = jnp.zeros(base_ref.shape, F32)

    e = e_ref[...]
    rows = lax.broadcasted_iota(jnp.int32, (N_EXPERTS, tn), 0)
    oh1 = rows == e[0:1, :]
    oh2 = rows == e[1:2, :]
    oh = jnp.where(jnp.logical_or(oh1, oh2), 1.0, 0.0)
    r = lax.broadcasted_iota(jnp.int32, (tn, tn), 0)
    c = lax.broadcasted_iota(jnp.int32, (tn, tn), 1)
    earlier = jnp.where(r < c, 1.0, 0.0).astype(BF16)
    base = base_ref[...]
    before = _dot(oh.astype(BF16), earlier) + jnp.tile(base, (1, tn // LANES))
    r1 = jnp.sum(jnp.where(oh1, before, 0.0), axis=0, keepdims=True)
    r2 = jnp.sum(jnp.where(oh2, before, 0.0), axis=0, keepdims=True)
    rank_ref[...] = jnp.concatenate([r1, r2, jnp.zeros((SUBLANES - 2, tn), F32)], axis=0).astype(jnp.int32)
    base = base + jnp.sum(oh, axis=1, keepdims=True)
    base_ref[...] = base
    cnt_ref[...] = base.astype(jnp.int32)


def _rank(eidx, tn):
    t = eidx.shape[1]
    tn = min(tn, t)
    return pl.pallas_call(
        functools.partial(_rank_kernel, tn=tn),
        grid=(t // tn,),
        in_specs=[pl.BlockSpec((SUBLANES, tn), lambda i: (0, i))],
        out_specs=[pl.BlockSpec((SUBLANES, tn), lambda i: (0, i)),
                   pl.BlockSpec((N_EXPERTS, LANES), lambda i: (0, 0))],
        out_shape=[jax.ShapeDtypeStruct((SUBLANES, t), jnp.int32),
                   jax.ShapeDtypeStruct((N_EXPERTS, LANES), jnp.int32)],
        scratch_shapes=[pltpu.VMEM((N_EXPERTS, LANES), F32)],
        compiler_params=_params(("arbitrary",)),
        name="moe_rank",
    )(eidx)


def _moe_kernel(te_ref, nu_ref, xs_ref, wg_ref, wu_ref, wd_ref, *rest, tile0):
    y_ref, wgu_s, wd_s = rest[-3:]
    i = pl.program_id(0)
    tile = tile0 + i
    used = tile < nu_ref[0]
    fresh = jnp.logical_or(i == 0, te_ref[tile] != te_ref[jnp.maximum(tile - 1, 0)])

    @pl.when(jnp.logical_and(used, fresh))
    def _():
        wgu_s[:, :D_EXPERT] = wg_ref[...].astype(BF16)
        wgu_s[:, D_EXPERT:] = wu_ref[...].astype(BF16)
        wd_s[...] = wd_ref[...].astype(BF16)

    @pl.when(used)
    def _():
        hgu = _dot(xs_ref[...].astype(BF16), wgu_s[...])
        hg = hgu[:, :D_EXPERT]
        hu = hgu[:, D_EXPERT:]
        h = (hg / (1.0 + jnp.exp(-hg))) * hu
        y_ref[...] = _dot(h.astype(BF16), wd_s[...])

    @pl.when(jnp.logical_not(used))
    def _():
        y_ref[...] = jnp.zeros(y_ref.shape, F32)


def _moe_part(tile_expert, n_used, xs_part, w_gate, w_up, w_down, ys_prev, tile0, n_tiles, tm):
    nt = xs_part.shape[0] // tm
    in_specs = [
        pl.BlockSpec((tm, D_MODEL), lambda i, te, nu: (i, 0)),
        pl.BlockSpec((None, D_MODEL, D_EXPERT), lambda i, te, nu: (te[tile0 + i], 0, 0)),
        pl.BlockSpec((None, D_MODEL, D_EXPERT), lambda i, te, nu: (te[tile0 + i], 0, 0)),
        pl.BlockSpec((None, D_EXPERT, D_MODEL), lambda i, te, nu: (te[tile0 + i], 0, 0)),
    ]
    args = [tile_expert, n_used, xs_part, w_gate, w_up, w_down]
    aliases = {}
    if ys_prev is not None:
        in_specs.append(pl.BlockSpec(memory_space=pl.ANY))
        args.append(ys_prev)
        aliases = {len(args) - 1: 0}
    grid_spec = pltpu.PrefetchScalarGridSpec(
        num_scalar_prefetch=2,
        grid=(nt,),
        in_specs=in_specs,
        out_specs=pl.BlockSpec((tm, D_MODEL), lambda i, te, nu: (tile0 + i, 0)),
        scratch_shapes=[pltpu.VMEM((D_MODEL, 2 * D_EXPERT), BF16), pltpu.VMEM((D_EXPERT, D_MODEL), BF16)],
    )
    return pl.pallas_call(
        functools.partial(_moe_kernel, tile0=tile0),
        grid_spec=grid_spec,
        out_shape=jax.ShapeDtypeStruct((n_tiles * tm, D_MODEL), F32),
        input_output_aliases=aliases,
        compiler_params=_params(("arbitrary",)),
        name="moe",
    )(*args)


def _moe_ln_kernel(x_ref, y1_ref, y2_ref, w_ref, g_ref, b_ref, *rest):
    o_ref = rest[-1]
    w = w_ref[...]
    v = DEEPNORM_ALPHA * x_ref[...] + (w[:, 0:1] * y1_ref[...] + w[:, 1:2] * y2_ref[...])
    o_ref[...] = _layer_norm(v, g_ref[...], b_ref[...])


def _moe_ln_part(x, y1, y2, w_cols, ln_g, ln_b, out_prev, row0, tm):
    t = x.shape[0]
    b0 = row0 // tm
    here = pl.BlockSpec((tm, D_MODEL), lambda i: (i, 0))
    there = pl.BlockSpec((tm, D_MODEL), lambda i: (b0 + i, 0))
    vec = pl.BlockSpec((1, D_MODEL), lambda i: (0, 0))
    in_specs = [there, here, here, pl.BlockSpec((tm, SUBLANES), lambda i: (b0 + i, 0)), vec, vec]
    args = [x, y1, y2, w_cols, ln_g, ln_b]
    aliases = {}
    if out_prev is not None:
        in_specs.append(pl.BlockSpec(memory_space=pl.ANY))
        args.append(out_prev)
        aliases = {len(args) - 1: 0}
    return pl.pallas_call(
        _moe_ln_kernel,
        grid=(y1.shape[0] // tm,),
        in_specs=in_specs,
        out_specs=there,
        out_shape=jax.ShapeDtypeStruct((t, D_MODEL), F32),
        input_output_aliases=aliases,
        compiler_params=_params(("parallel",)),
        name="moe_ln",
    )(*args)


def _moe_block(x, eidx, ew, w_gate, w_up, w_down, ln_g, ln_b, tm):
    t = x.shape[0]
    tm = min(tm, t)
    rank, cnt = _rank(eidx, 512)
    counts = cnt[:, 0]
    padded = ((counts + tm - 1) // tm) * tm
    ends = jnp.cumsum(padded)
    starts = ends - padded
    e2 = eidx[:2]
    start_of = jnp.zeros_like(e2)
    for k in range(N_EXPERTS):
        start_of = jnp.where(e2 == k, starts[k], start_of)
    dest = start_of + rank[:2]
    t_pad = 2 * t + N_EXPERTS * tm
    n_tiles = t_pad // tm
    tok = jnp.tile(jnp.arange(t, dtype=jnp.int32), 2)
    src_tok = (jnp.arange(t_pad, dtype=jnp.int32) % t).at[dest.reshape(2 * t)].set(
        tok, mode="promise_in_bounds", unique_indices=True)
    tile_start = jnp.arange(n_tiles, dtype=jnp.int32) * tm
    tile_expert = jnp.minimum(jnp.sum((ends[None, :] <= tile_start[:, None]).astype(jnp.int32), axis=1),
                              N_EXPERTS - 1)
    n_used = (ends[-1] // tm).astype(jnp.int32).reshape(1)
    ys = None
    tiles_per = n_tiles // MOE_PARTS
    for p in range(MOE_PARTS):
        lo = p * tiles_per * tm
        xs_p = x.at[src_tok[lo:lo + tiles_per * tm]].get(mode="promise_in_bounds")
        ys = _moe_part(tile_expert, n_used, xs_p, w_gate, w_up, w_down, ys, p * tiles_per, n_tiles, tm)
    out = None
    rows_per = t // MOE_PARTS
    w_cols = ew.T
    for p in range(MOE_PARTS):
        lo = p * rows_per
        y1 = ys.at[dest[0, lo:lo + rows_per]].get(mode="promise_in_bounds")
        y2 = ys.at[dest[1, lo:lo + rows_per]].get(mode="promise_in_bounds")
        out = _moe_ln_part(x, y1, y2, w_cols, ln_g, ln_b, out, lo, tm)
    return out


def kernel(x, router_w, router_b, l0_w_in, l0_diff_lam_q1, l0_diff_lam_k1, l0_diff_lam_q2, l0_diff_lam_k2, l0_diff_subln_g, l0_gla_w_gate, l0_gla_b_gate, l0_gla_norm_g, l0_w_out, l0_ln1_g, l0_ln1_b, l0_moe_w_gate, l0_moe_w_up, l0_moe_w_down, l0_ln2_g, l0_ln2_b, l1_w_in, l1_s5_a_re, l1_s5_a_im, l1_s5_log_step, l1_s5_b_re, l1_s5_b_im, l1_s5_c_re, l1_s5_c_im, l1_s5_d, l1_s5_w_glu, l1_s5_b_glu, l1_w_out, l1_ln1_g, l1_ln1_b, l1_moe_w_gate, l1_moe_w_up, l1_moe_w_down, l1_ln2_g, l1_ln2_b):
    bsz, t, d = x.shape
    assert bsz == 1 and d == D_MODEL
    x0 = x.reshape(t, d)
    vec = lambda p: p.reshape(1, -1).astype(F32)
    rwt = router_w.T.astype(F32)
    rb = router_b.reshape(N_EXPERTS, 1).astype(F32)

    lambda_init = 0.8 - 0.6 * math.exp(-0.3 * 0)
    c = [0, 1024, 2048, 3072, 3584, 4096, 5120, 5136, 6160]
    q_scale = math.log2(math.e) * DIFF_HEAD_DIM ** -0.5
    w0 = jnp.concatenate([l0_w_in[:, :c[1]] * q_scale, l0_w_in[:, c[1]:c[6]], l0_w_in[:, c[7]:c[8]],
                          l0_w_in[:, c[6]:c[7]], jnp.zeros((d, L0_GB_PAD - GLA_GATE_RANK), l0_w_in.dtype)],
                         axis=1).astype(BF16)
    h0 = _matmul(x0, w0, 1024, L0_IN_TN, BF16)
    lam_vec = jnp.stack([l0_diff_lam_q1, l0_diff_lam_k1, l0_diff_lam_q2, l0_diff_lam_k2]).astype(F32)
    o_a = _diff_attention(h0, lam_vec, vec(l0_diff_subln_g), lambda_init, 1024)
    wg_pad = jnp.concatenate([l0_gla_w_gate, jnp.zeros((LANES - GLA_GATE_RANK, GLA_HEADS * GLA_DK), F32)],
                             axis=0).astype(BF16)
    o_b = _gla(h0, wg_pad, vec(l0_gla_b_gate), vec(l0_gla_norm_g), 256)
    x1, eidx, ew = _outproj_ln_router(o_a, o_b, l0_w_out, x0, vec(l0_ln1_g), vec(l0_ln1_b),
                                      rwt, rb, 512, 128)
    x2 = _moe_block(x1, eidx, ew, l0_moe_w_gate, l0_moe_w_up, l0_moe_w_down, vec(l0_ln2_g), vec(l0_ln2_b), 256)

    h1 = _matmul(x2, l1_w_in.astype(BF16), 1024, 1024, BF16)
    o_c = _stick_breaking(h1, 256)
    lbr, lbi, bbr, bbi = _s5_prep(l1_s5_a_re, l1_s5_a_im, l1_s5_log_step, l1_s5_b_re, l1_s5_b_im)
    bd, cd, pw8 = _s5_tile_params(lbr[::S5_GROUP], lbi[::S5_GROUP], bbr.reshape(S5_GROUPS, S5_GROUP, S5_STATE),
                                  bbi.reshape(S5_GROUPS, S5_GROUP, S5_STATE),
                                  l1_s5_c_re.astype(F32), l1_s5_c_im.astype(F32))
    y_s5 = _s5_core(h1, bd, cd, pw8, 256)
    o_d = _s5_glu(y_s5, h1, vec(l1_s5_d), l1_s5_w_glu.astype(BF16), vec(l1_s5_b_glu), 512)
    x3, eidx1, ew1 = _outproj_ln_router(o_c, o_d, l1_w_out, x2, vec(l1_ln1_g), vec(l1_ln1_b),
                                        rwt, rb, 512, 128)
    x4 = _moe_block(x3, eidx1, ew1, l1_moe_w_gate, l1_moe_w_up, l1_moe_w_down, vec(l1_ln2_g), vec(l1_ln2_b), 256)
    return x4.reshape(bsz, t, d)
```

```python
import functools
import math

import jax
import jax.numpy as jnp
from jax import lax
from jax.experimental import pallas as pl
from jax.experimental.pallas import tpu as pltpu

F32 = jnp.float32
BF16 = jnp.bfloat16

D_MODEL = 2048
DEPTH = 2
DIFF_HEADS = 8
DIFF_HEAD_DIM = 64
DIFF_V_DIM = 128
GLA_HEADS = 4
GLA_DK = 128
GLA_DV = 256
GLA_GATE_RANK = 16
GLA_GATE_TEMP = 16.0
GLA_CHUNK = 64
SB_HEADS = 8
SB_HEAD_DIM = 128
S5_CHANNELS = 1024
S5_GROUP = 16
S5_GROUPS = 64
S5_STATE = 64
N_EXPERTS = 16
N_GROUPS = 4
EXPERTS_PER_GROUP = 4
D_EXPERT = 640
DEEPNORM_ALPHA = (2.0 * DEPTH) ** 0.25
LN_EPS = 1e-5

LANES = 128
SUBLANES = 8
VMEM_LIMIT = 56 * 1024 * 1024
SB_LOG_FLOOR = 120.0
SB_HEADS_PER_STEP = 4
MOE_PARTS = 4

MXU_N = 256
L0_GB_PAD = MXU_N
L0_IN_TN = 5 * MXU_N

S5_KT = 4
S5_KT_CH = S5_CHANNELS // S5_KT
S5_KT_STATES = (S5_GROUPS // S5_KT) * S5_STATE
S5_SCAN_W = 512
S5_L = 8


def _params(sem, flags=None):
    return pltpu.CompilerParams(dimension_semantics=sem, vmem_limit_bytes=VMEM_LIMIT, flags=flags)


def _nt_dot(a, b):
    return lax.dot_general(a, b, (((1,), (1,)), ((), ())), preferred_element_type=F32)


def _dot(a, b):
    return jnp.dot(a, b, preferred_element_type=F32)


def _split_dot_left(m, x):
    hi = x.astype(BF16)
    lo = (x - hi.astype(F32)).astype(BF16)
    return _dot(m, hi) + _dot(m, lo)


def _layer_norm(v, g, b):
    mu = jnp.mean(v, axis=-1, keepdims=True)
    d = v - mu
    var = jnp.mean(d * d, axis=-1, keepdims=True)
    return d * lax.rsqrt(var + LN_EPS) * g + b


def _mm_kernel(a_ref, b_ref, o_ref, a_bf):
    @pl.when(pl.program_id(1) == 0)
    def _():
        a_bf[...] = a_ref[...].astype(BF16)

    o_ref[...] = _dot(a_bf[...], b_ref[...]).astype(o_ref.dtype)


def _matmul(a, b, tm, tn, out_dtype):
    m, k = a.shape
    n = b.shape[1]
    tm = min(tm, m)
    return pl.pallas_call(
        _mm_kernel,
        grid=(m // tm, n // tn),
        in_specs=[pl.BlockSpec((tm, k), lambda i, j: (i, 0)),
                  pl.BlockSpec((k, tn), lambda i, j: (0, j))],
        out_specs=pl.BlockSpec((tm, tn), lambda i, j: (i, j)),
        out_shape=jax.ShapeDtypeStruct((m, n), out_dtype),
        scratch_shapes=[pltpu.VMEM((tm, k), BF16)],
        compiler_params=_params(("parallel", "arbitrary")),
        name="in_proj",
    )(a, b)


def _diff_attn_kernel(lam_ref, g_ref, q_ref, k_ref, v_ref, o_ref, vx_ref, m_ref, acc_ref, sa_ref, sb_ref, *, tq,
                      lambda_init):
    i = pl.program_id(1)
    dv = DIFF_V_DIM

    @pl.when(i == 0)
    def _():
        vx_ref[:, :dv] = v_ref[...]
        vx_ref[:, dv:] = jnp.ones((vx_ref.shape[0], dv), BF16)

    m_ref[...] = jnp.full(m_ref.shape, -jnp.inf, F32)
    acc_ref[...] = jnp.zeros(acc_ref.shape, F32)
    q = q_ref[...]
    lane = lax.broadcasted_iota(jnp.int32, q.shape, 1)
    zero = jnp.zeros_like(q)
    halves = (jnp.where(lane < DIFF_HEAD_DIM, q, zero), jnp.where(lane >= DIFF_HEAD_DIM, q, zero))

    def scores(j, mi):
        off = pl.multiple_of(j * tq, tq)
        return _nt_dot(halves[mi], k_ref[pl.ds(off, tq), :])

    def softmax_pv(j, get_s, masked):
        off = pl.multiple_of(j * tq, tq)
        vx = vx_ref[pl.ds(off, tq), :]
        if masked:
            row = lax.broadcasted_iota(jnp.int32, (tq, tq), 0)
            col = lax.broadcasted_iota(jnp.int32, (tq, tq), 1)
            causal = col <= row
        for mi in range(2):
            s = get_s(mi)
            if masked:
                s = jnp.where(causal, s, -jnp.inf)
            m_prev = m_ref[mi]
            m_new = jnp.maximum(m_prev, jnp.max(s, axis=1, keepdims=True))
            p = jnp.exp2(s - jnp.tile(m_new, (1, tq // LANES)))
            alpha = jnp.exp2(m_prev - m_new)
            acc_ref[mi] = jnp.tile(alpha, (1, 2)) * acc_ref[mi] + _dot(p.astype(BF16), vx)
            m_ref[mi] = m_new

    def scores_to(j, s_ref):
        for mi in range(2):
            s_ref[mi] = scores(j, mi)

    scores_to(0, sa_ref)

    def pair(t, carry):
        j = 2 * t
        scores_to(j + 1, sb_ref)
        softmax_pv(j, lambda mi: sa_ref[mi], False)
        scores_to(j + 2, sa_ref)
        softmax_pv(j + 1, lambda mi: sb_ref[mi], False)
        return carry

    lax.fori_loop(0, i // 2, pair, 0)

    @pl.when(i % 2 == 1)
    def _():
        softmax_pv(i - 1, lambda mi: sa_ref[mi], False)
        scores_to(i, sa_ref)

    softmax_pv(i, lambda mi: sa_ref[mi], True)

    lv = lam_ref[...]
    lam = (jnp.exp(jnp.sum(lv[0:1] * lv[1:2], axis=1, keepdims=True))
           - jnp.exp(jnp.sum(lv[2:3] * lv[3:4], axis=1, keepdims=True)) + lambda_init)
    a0 = acc_ref[0]
    a1 = acc_ref[1]
    o = a0[:, :dv] / a0[:, dv:] - lam * (a1[:, :dv] / a1[:, dv:])
    ms = jnp.mean(o * o, axis=-1, keepdims=True)
    o = o * lax.rsqrt(ms + LN_EPS) * g_ref[...] * (1.0 - lambda_init)
    o_ref[...] = o.astype(o_ref.dtype)


def _diff_attention(h0, lam_vec, subln_g, lambda_init, tq):
    t = h0.shape[0]
    tq = min(tq, t)
    nq = t // tq
    kern = functools.partial(_diff_attn_kernel, tq=tq, lambda_init=lambda_init)
    return pl.pallas_call(
        kern,
        grid=(DIFF_HEADS, nq),
        in_specs=[
            pl.BlockSpec((4, DIFF_HEAD_DIM), lambda h, i: (0, 0)),
            pl.BlockSpec((1, DIFF_V_DIM), lambda h, i: (0, 0)),
            pl.BlockSpec((tq, LANES), lambda h, i: (i, h)),
            pl.BlockSpec((t, LANES), lambda h, i: (0, DIFF_HEADS + h)),
            pl.BlockSpec((t, LANES), lambda h, i: (0, 2 * DIFF_HEADS + h)),
        ],
        out_specs=pl.BlockSpec((tq, LANES), lambda h, i: (i, h)),
        out_shape=jax.ShapeDtypeStruct((t, DIFF_HEADS * DIFF_V_DIM), BF16),
        scratch_shapes=[pltpu.VMEM((t, 2 * DIFF_V_DIM), BF16), pltpu.VMEM((2, tq, LANES), F32),
                        pltpu.VMEM((2, tq, 2 * DIFF_V_DIM), F32),
                        pltpu.VMEM((2, tq, tq), F32), pltpu.VMEM((2, tq, tq), F32)],
        compiler_params=_params(("arbitrary", "arbitrary")),
        name="diff_attn",
    )(lam_vec, subln_g, h0, h0, h0)


def _gla_kernel(q_ref, k_ref, v_ref, r_ref, gb_ref, wg_ref, bg_ref, ng_ref, o_ref, st_ref, *, tg):
    @pl.when(pl.program_id(0) == 0)
    def _():
        st_ref[...] = jnp.zeros(st_ref.shape, F32)

    c = GLA_CHUNK
    gate = _dot(gb_ref[...], wg_ref[...]) + bg_ref[...]
    log_a = -(jnp.maximum(-gate, 0.0) + jnp.log(1.0 + jnp.exp(-jnp.abs(gate)))) / GLA_GATE_TEMP
    row = lax.broadcasted_iota(jnp.int32, (c, c), 0)
    col = lax.broadcasted_iota(jnp.int32, (c, c), 1)
    tril = row >= col
    tri = jnp.where(tril, 1.0, 0.0).astype(BF16)
    for h in range(GLA_HEADS):
        ksl = slice(h * GLA_DK, (h + 1) * GLA_DK)
        vsl = slice(h * GLA_DV, (h + 1) * GLA_DV)
        for ci in range(tg // c):
            rs = slice(ci * c, (ci + 1) * c)
            la = log_a[rs, ksl]
            b = _split_dot_left(tri, la)
            b_last = b[c - 1:c, :]
            qf = q_ref[rs, ksl].astype(F32) * (GLA_DK ** -0.5)
            kf = k_ref[rs, ksl].astype(F32)
            v = v_ref[rs, vsl]
            q_dec = (qf * jnp.exp(b)).astype(BF16)
            k_dec = (kf * jnp.exp(-b)).astype(BF16)
            k_end = (kf * jnp.exp(b_last - b)).astype(BF16)
            scores = jnp.where(tril, _nt_dot(q_dec, k_dec), 0.0)
            st = st_ref[h]
            o = _dot(scores.astype(BF16), v) + _nt_dot(q_dec, st.astype(BF16))
            kv_t = lax.dot_general(v, k_end, (((0,), (0,)), ((), ())), preferred_element_type=F32)
            st_ref[h] = jnp.exp(b_last) * st + kv_t
            ms = jnp.mean(o * o, axis=-1, keepdims=True)
            o = o * lax.rsqrt(ms + LN_EPS) * ng_ref[...]
            r = r_ref[rs, vsl].astype(F32)
            o_ref[rs, vsl] = (o * (r / (1.0 + jnp.exp(-r)))).astype(o_ref.dtype)


def _gla(h0, wg_pad, b_gate, norm_g, tg):
    t = h0.shape[0]
    tg = min(tg, t)
    qk = GLA_HEADS * GLA_DK
    vw = GLA_HEADS * GLA_DV
    return pl.pallas_call(
        functools.partial(_gla_kernel, tg=tg),
        grid=(t // tg,),
        in_specs=[
            pl.BlockSpec((tg, qk), lambda i: (i, 3072 // qk)),
            pl.BlockSpec((tg, qk), lambda i: (i, 3584 // qk)),
            pl.BlockSpec((tg, vw), lambda i: (i, 4096 // vw)),
            pl.BlockSpec((tg, vw), lambda i: (i, 5120 // vw)),
            pl.BlockSpec((tg, LANES), lambda i: (i, 6144 // LANES)),
            pl.BlockSpec((LANES, qk), lambda i: (0, 0)),
            pl.BlockSpec((1, qk), lambda i: (0, 0)),
            pl.BlockSpec((1, GLA_DV), lambda i: (0, 0)),
        ],
        out_specs=pl.BlockSpec((tg, vw), lambda i: (i, 0)),
        out_shape=jax.ShapeDtypeStruct((t, vw), BF16),
        scratch_shapes=[pltpu.VMEM((GLA_HEADS, GLA_DV, GLA_DK), F32)],
        compiler_params=_params(("arbitrary",)),
        name="gla",
    )(h0, h0, h0, h0, h0, wg_pad, b_gate, norm_g)


def _sb_kernel(q_ref, k_ref, v_ref, o_ref, acc_ref, c_ref, *, tq):
    i = pl.program_id(1)
    hd = SB_HEAD_DIM
    acc_ref[...] = jnp.zeros(acc_ref.shape, F32)
    c_ref[...] = jnp.zeros(c_ref.shape, F32)
    row = lax.broadcasted_iota(jnp.int32, (tq, tq), 0)
    col = lax.broadcasted_iota(jnp.int32, (tq, tq), 1)
    strict = col < row
    tri = jnp.where(row > col, 1.0, 0.0).astype(BF16)

    def block(jb, masked):
        off = pl.multiple_of(jb * tq, tq)
        cmax = None
        for hh in range(SB_HEADS_PER_STEP):
            cols = slice(hh * hd, (hh + 1) * hd)
            k = k_ref[pl.ds(off, tq), cols]
            v = v_ref[pl.ds(off, tq), cols]
            z = _nt_dot(q_ref[:, cols], k) * (SB_HEAD_DIM ** -0.5)
            lsn = -(jnp.maximum(z, 0.0) + jnp.log(1.0 + jnp.exp(-jnp.abs(z))))
            lm = jnp.where(strict, lsn, 0.0) if masked else lsn
            excl = _dot(lm.astype(BF16), tri) + c_ref[hh]
            w = jnp.exp(z + lsn + excl)
            if masked:
                w = jnp.where(strict, w, 0.0)
            acc_ref[:, cols] += _dot(w.astype(BF16), v)
            c_new = c_ref[hh] + jnp.sum(lm, axis=1, keepdims=True)
            c_ref[hh] = c_new
            hmax = jnp.max(c_new)
            cmax = hmax if cmax is None else jnp.maximum(cmax, hmax)
        return cmax

    cmax = block(i, True)

    def cond(carry):
        jb, cm = carry
        return jnp.logical_and(jb >= 0, cm > -SB_LOG_FLOOR)

    def body(carry):
        jb, _ = carry
        return jb - 1, block(jb, False)

    lax.while_loop(cond, body, (i - 1, cmax))
    o_ref[...] = acc_ref[...].astype(o_ref.dtype)


def _stick_breaking(h1, tq):
    t = h1.shape[0]
    tq = min(tq, t)
    ng = SB_HEADS // SB_HEADS_PER_STEP
    wd = SB_HEADS_PER_STEP * SB_HEAD_DIM
    return pl.pallas_call(
        functools.partial(_sb_kernel, tq=tq),
        grid=(ng, t // tq),
        in_specs=[
            pl.BlockSpec((tq, wd), lambda h, i: (i, h)),
            pl.BlockSpec((t, wd), lambda h, i: (0, ng + h), pipeline_mode=pl.Buffered(1)),
            pl.BlockSpec((t, wd), lambda h, i: (0, 2 * ng + h), pipeline_mode=pl.Buffered(1)),
        ],
        out_specs=pl.BlockSpec((tq, wd), lambda h, i: (i, h)),
        out_shape=jax.ShapeDtypeStruct((t, SB_HEADS * SB_HEAD_DIM), BF16),
        scratch_shapes=[pltpu.VMEM((tq, wd), F32), pltpu.VMEM((SB_HEADS_PER_STEP, tq, 1), F32)],
        compiler_params=_params(("parallel", "arbitrary")),
        name="stick_breaking",
    )(h1, h1, h1)


def _s5_prep_kernel(are_ref, aim_ref, ls_ref, bre_ref, bim_ref, lbr_ref, lbi_ref, bbr_ref, bbi_ref):
    step = jnp.exp(ls_ref[...])
    lam_re = are_ref[...]
    lam_im = aim_ref[...]
    mag = jnp.exp(step * lam_re)
    lb_re = mag * jnp.cos(step * lam_im)
    lb_im = mag * jnp.sin(step * lam_im)
    den = lam_re * lam_re + lam_im * lam_im
    n_re = lb_re - 1.0
    f_re = (n_re * lam_re + lb_im * lam_im) / den
    f_im = (lb_im * lam_re - n_re * lam_im) / den
    lbr_ref[...] = lb_re
    lbi_ref[...] = lb_im
    bbr_ref[...] = f_re * bre_ref[...] - f_im * bim_ref[...]
    bbi_ref[...] = f_re * bim_ref[...] + f_im * bre_ref[...]


def _s5_prep(a_re, a_im, log_step, b_re, b_im):
    rows = S5_GROUPS * S5_GROUP
    rep = lambda p: jnp.repeat(p, S5_GROUP, axis=0)
    to_rows = lambda b: jnp.transpose(b, (0, 2, 1)).reshape(rows, S5_STATE)
    spec = pl.BlockSpec((rows, S5_STATE), lambda: (0, 0))
    shp = jax.ShapeDtypeStruct((rows, S5_STATE), F32)
    return pl.pallas_call(
        _s5_prep_kernel,
        in_specs=[spec, spec, pl.BlockSpec((rows, 1), lambda: (0, 0)), spec, spec],
        out_specs=[spec] * 4,
        out_shape=[shp] * 4,
        name="s5_prep",
    )(rep(a_re), rep(a_im), rep(log_step[:, None]), to_rows(b_re), to_rows(b_im))


def _cmul(ar, ai, br, bi):
    return ar * br - ai * bi, ar * bi + ai * br


def _block_diag_tiles(m, rows_per_group, cols_per_group):
    gpt = S5_GROUPS // S5_KT
    m = m.reshape(S5_KT, gpt, rows_per_group, cols_per_group)
    eye = jnp.eye(gpt, dtype=m.dtype)
    bd = jnp.einsum("kgrc,gh->kgrhc", m, eye)
    return bd.reshape(S5_KT, gpt * rows_per_group, gpt * cols_per_group)


def _s5_tile_params(lb_re, lb_im, bb_re, bb_im, c_re, c_im):
    bd = jnp.concatenate([_block_diag_tiles(bb_re, S5_GROUP, S5_STATE),
                          _block_diag_tiles(bb_im, S5_GROUP, S5_STATE)], axis=2)
    cd = jnp.concatenate([_block_diag_tiles(jnp.transpose(c_re, (0, 2, 1)), S5_STATE, S5_GROUP),
                          _block_diag_tiles(-jnp.transpose(c_im, (0, 2, 1)), S5_STATE, S5_GROUP)], axis=1)
    pr = [jnp.ones_like(lb_re)]
    pi = [jnp.zeros_like(lb_re)]
    for _ in range(S5_L):
        nr, ni = _cmul(pr[-1], pi[-1], lb_re, lb_im)
        pr.append(nr)
        pi.append(ni)

    def table(p):
        rows = jnp.transpose(jnp.stack(p).reshape(S5_L + 1, S5_KT, S5_KT_STATES), (1, 0, 2))
        return jnp.pad(rows, ((0, 0), (0, LANES - (S5_L + 1)), (0, 0)))

    return bd, cd, jnp.stack([table(pr), table(pi)], axis=1)


def _s5_core_kernel(u_ref, bd_ref, cd_ref, pw8_ref, y_ref, hs_ref, hp_ref, pw_ref, st_ref, uf_ref,
                    mt_s, bbs_s, ccs_s, *, tcr):
    ns = S5_KT_STATES
    pad = SUBLANES
    cw = S5_KT_CH

    @pl.when(pl.program_id(1) == 0)
    def _():
        b_re = bd_ref[:, :ns]
        b_im = bd_ref[:, ns:]
        cd_hi = cd_ref[...].astype(BF16)
        cd_lo = (cd_ref[...] - cd_hi.astype(F32)).astype(BF16)
        for k in range(S5_L):
            a = pw8_ref[0, k:k + 1, :]
            b = pw8_ref[1, k:k + 1, :]
            dk = jnp.concatenate([b_re * a - b_im * b, b_re * b + b_im * a], axis=1)
            s = S5_L - 1 - k
            bbs_s[s * cw:(s + 1) * cw, :] = dk.astype(BF16)
            dk_hi = dk.astype(BF16)
            dk_lo = (dk - dk_hi.astype(F32)).astype(BF16)
            kk = (_dot(dk_hi, cd_hi) + (_dot(dk_lo, cd_hi) + _dot(dk_hi, cd_lo))).astype(BF16)
            for s0 in range(S5_L - k):
                mt_s[s0 * cw:(s0 + 1) * cw, (s0 + k) * cw:(s0 + k + 1) * cw] = kk
        pw_t = (jnp.transpose(pw8_ref[0]), jnp.transpose(pw8_ref[1]))
        c_re = cd_ref[:ns, :]
        c_im = cd_ref[ns:, :]
        for i in range(S5_L):
            a = pw_t[0][:, i + 1:i + 2]
            b = pw_t[1][:, i + 1:i + 2]
            ccs_s[:ns, i * cw:(i + 1) * cw] = (c_re * a + c_im * b).astype(BF16)
            ccs_s[ns:, i * cw:(i + 1) * cw] = (c_im * a - c_re * b).astype(BF16)

        hp_ref[...] = jnp.zeros(hp_ref.shape, F32)
        hs_ref[0:pad, :] = jnp.zeros((pad, 2 * ns), F32)
        l1r = jnp.broadcast_to(pw8_ref[0, S5_L:S5_L + 1, :], (SUBLANES, ns))
        l1i = jnp.broadcast_to(pw8_ref[1, S5_L:S5_L + 1, :], (SUBLANES, ns))
        l2r, l2i = _cmul(l1r, l1i, l1r, l1i)
        l4r, l4i = _cmul(l2r, l2i, l2r, l2i)
        l8r, l8i = _cmul(l4r, l4i, l4r, l4i)
        ridx = lax.broadcasted_iota(jnp.int32, l1r.shape, 0)
        n = ridx + 1
        pr = jnp.ones_like(l1r)
        pi = jnp.zeros_like(l1r)
        for bit, (lr, li) in ((1, (l1r, l1i)), (2, (l2r, l2i)), (4, (l4r, l4i)), (8, (l8r, l8i))):
            nr, ni = _cmul(pr, pi, lr, li)
            take = (n & bit) != 0
            pr = jnp.where(take, nr, pr)
            pi = jnp.where(take, ni, pi)
        pw_ref[0] = pr
        pw_ref[1] = pi
        for s, (lr, li) in enumerate(((l1r, l1i), (l2r, l2i), (l4r, l4i))):
            keep = ridx >= (1 << s)
            st_ref[2 * s] = jnp.where(keep, lr, 0.0)
            st_ref[2 * s + 1] = jnp.where(keep, li, 0.0)

    nl = S5_KT_CH // LANES
    for h in range(nl):
        uf_ref[h] = u_ref[:, h * LANES:(h + 1) * LANES].astype(F32)
    uc = jnp.concatenate([uf_ref[h, pl.ds(s, tcr, stride=S5_L), :].astype(BF16)
                          for s in range(S5_L) for h in range(nl)], axis=1)
    hs_ref[pad:pad + tcr, :] = _dot(uc, bbs_s[...])
    w = S5_SCAN_W
    for cc in range(ns // w):
        re0 = cc * w
        im0 = ns + cc * w

        def body(r, carry, re0=re0, im0=im0):
            hpr, hpi = carry
            rows = pl.ds(pl.multiple_of(pad + r * SUBLANES, SUBLANES), SUBLANES)
            xr = hs_ref[rows, re0:re0 + w]
            xi = hs_ref[rows, im0:im0 + w]
            for s in range(3):
                sr = pltpu.roll(xr, 1 << s, 0)
                si = pltpu.roll(xi, 1 << s, 0)
                ar = st_ref[2 * s, :, re0:re0 + w]
                ai = st_ref[2 * s + 1, :, re0:re0 + w]
                xr, xi = xr + ar * sr - ai * si, xi + ar * si + ai * sr
            pr = pw_ref[0, :, re0:re0 + w]
            pi = pw_ref[1, :, re0:re0 + w]
            xr, xi = xr + pr * hpr - pi * hpi, xi + pr * hpi + pi * hpr
            hs_ref[rows, re0:re0 + w] = xr
            hs_ref[rows, im0:im0 + w] = xi
            return (jnp.broadcast_to(xr[SUBLANES - 1:SUBLANES, :], (SUBLANES, w)),
                    jnp.broadcast_to(xi[SUBLANES - 1:SUBLANES, :], (SUBLANES, w)))

        hpr, hpi = lax.fori_loop(0, tcr // SUBLANES, body, (hp_ref[:, re0:re0 + w], hp_ref[:, im0:im0 + w]))
        hp_ref[:, re0:re0 + w] = hpr
        hp_ref[:, im0:im0 + w] = hpi

    h_prev = hs_ref[pad - 1:pad - 1 + tcr, :]
    y_in = _dot(h_prev.astype(BF16), ccs_s[...])
    for i in range(S5_L):
        k_hi = (i + 1) * cw
        y_i = y_in[:, i * cw:k_hi] + _dot(uc[:, :k_hi], mt_s[:k_hi, i * cw:k_hi])
        for h in range(nl):
            uf_ref[h, pl.ds(i, tcr, stride=S5_L), :] = y_i[:, h * LANES:(h + 1) * LANES]
    for h in range(nl):
        y_ref[:, h * LANES:(h + 1) * LANES] = uf_ref[h]
    hs_ref[pad - 1:pad, :] = hs_ref[pad + tcr - 1:pad + tcr, :]


def _s5_core(h1, bd, cd, pw8, tcr):
    t = h1.shape[0]
    width = S5_L * S5_KT_CH
    tcr = min(tcr, t // S5_L)
    tc = tcr * S5_L
    ns = S5_KT_STATES
    u_col0 = 3 * S5_CHANNELS // S5_KT_CH
    once = pl.Buffered(1)
    return pl.pallas_call(
        functools.partial(_s5_core_kernel, tcr=tcr),
        grid=(S5_KT, t // tc),
        in_specs=[
            pl.BlockSpec((tc, S5_KT_CH), lambda k, i: (i, u_col0 + k)),
            pl.BlockSpec((None, S5_KT_CH, 2 * ns), lambda k, i: (k, 0, 0), pipeline_mode=once),
            pl.BlockSpec((None, 2 * ns, S5_KT_CH), lambda k, i: (k, 0, 0), pipeline_mode=once),
            pl.BlockSpec((None, 2, LANES, ns), lambda k, i: (k, 0, 0, 0), pipeline_mode=once),
        ],
        out_specs=pl.BlockSpec((tc, S5_KT_CH), lambda k, i: (i, k)),
        out_shape=jax.ShapeDtypeStruct((t, S5_CHANNELS), F32),
        scratch_shapes=[pltpu.VMEM((tcr + SUBLANES, 2 * ns), F32), pltpu.VMEM((SUBLANES, 2 * ns), F32),
                        pltpu.VMEM((2, SUBLANES, ns), F32), pltpu.VMEM((6, SUBLANES, ns), F32),
                        pltpu.VMEM((S5_KT_CH // LANES, tc, LANES), F32),
                        pltpu.VMEM((width, width), BF16), pltpu.VMEM((width, 2 * ns), BF16),
                        pltpu.VMEM((2 * ns, width), BF16)],
        compiler_params=_params(("arbitrary", "arbitrary")),
        name="s5_core",
    )(h1, bd, cd, pw8)


def _s5_glu_kernel(y_ref, u_ref, d_ref, wglu_ref, bglu_ref, o_ref):
    y = y_ref[...] + d_ref[...] * u_ref[...].astype(F32)
    g = 0.5 * y * (1.0 + jnp.tanh(math.sqrt(2.0 / math.pi) * (y + 0.044715 * (y * y * y))))
    gl = _dot(g.astype(BF16), wglu_ref[...]) + bglu_ref[...]
    o_ref[...] = (g / (1.0 + jnp.exp(-gl))).astype(o_ref.dtype)


def _s5_glu(y, h1, d_skip, w_glu, b_glu, tm):
    t = h1.shape[0]
    tm = min(tm, t)
    full = lambda shape: pl.BlockSpec(shape, lambda i: (0,) * len(shape))
    return pl.pallas_call(
        _s5_glu_kernel,
        grid=(t // tm,),
        in_specs=[pl.BlockSpec((tm, S5_CHANNELS), lambda i: (i, 0)),
                  pl.BlockSpec((tm, S5_CHANNELS), lambda i: (i, 3)),
                  full((1, S5_CHANNELS)), full((S5_CHANNELS, S5_CHANNELS)), full((1, S5_CHANNELS))],
        out_specs=pl.BlockSpec((tm, S5_CHANNELS), lambda i: (i, 0)),
        out_shape=jax.ShapeDtypeStruct((t, S5_CHANNELS), BF16),
        compiler_params=_params(("parallel",)),
        name="s5_glu",
    )(y, h1, d_skip, w_glu, b_glu)


def _route(probs, sel):
    rows = [sel[e:e + 1, :] for e in range(N_EXPERTS)]
    prow = [probs[e:e + 1, :] for e in range(N_EXPERTS)]
    best_score = None
    for g in range(N_GROUPS):
        a = rows[g * EXPERTS_PER_GROUP:(g + 1) * EXPERTS_PER_GROUP]
        score = None
        for x in range(EXPERTS_PER_GROUP):
            for y in range(x + 1, EXPERTS_PER_GROUP):
                pair = a[x] + a[y]
                score = pair if score is None else jnp.maximum(score, pair)
        if best_score is None:
            best_score, gbest = score, jnp.zeros(score.shape, jnp.int32)
            cs = list(a)
            cp = prow[0:EXPERTS_PER_GROUP]
        else:
            better = score > best_score
            best_score = jnp.where(better, score, best_score)
            gbest = jnp.where(better, g, gbest)
            cs = [jnp.where(better, a[x], cs[x]) for x in range(EXPERTS_PER_GROUP)]
            cp = [jnp.where(better, prow[g * EXPERTS_PER_GROUP + x], cp[x]) for x in range(EXPERTS_PER_GROUP)]

    def first_argmax(vals):
        bv, bi = vals[0], jnp.zeros(vals[0].shape, jnp.int32)
        for x in range(1, len(vals)):
            better = vals[x] > bv
            bv = jnp.where(better, vals[x], bv)
            bi = jnp.where(better, x, bi)
        return bi

    i1 = first_argmax(cs)
    i2 = first_argmax([jnp.where(i1 == x, -jnp.inf, cs[x]) for x in range(EXPERTS_PER_GROUP)])

    def pick(idx):
        out = cp[0]
        for x in range(1, EXPERTS_PER_GROUP):
            out = jnp.where(idx == x, cp[x], out)
        return out

    p1, p2 = pick(i1), pick(i2)
    tot = p1 + p2
    return gbest * EXPERTS_PER_GROUP + i1, gbest * EXPERTS_PER_GROUP + i2, p1 / tot, p2 / tot


def _ln_router_tail(v, g_ref, b_ref, rwh_ref, rwl_ref, rb_ref):
    xn = _layer_norm(v, g_ref[...], b_ref[...])
    xh = xn.astype(BF16)
    xl = (xn - xh.astype(F32)).astype(BF16)
    logits = _nt_dot(rwh_ref[...], xh) + (_nt_dot(rwl_ref[...], xh) + _nt_dot(rwh_ref[...], xl))
    mx = jnp.max(logits, axis=0, keepdims=True)
    ex = jnp.exp(logits - mx)
    probs = ex / jnp.sum(ex, axis=0, keepdims=True)
    e1, e2, w1, w2 = _route(probs, probs + rb_ref[...])
    n = e1.shape[1]
    ei = jnp.concatenate([e1, e2, jnp.zeros((SUBLANES - 2, n), jnp.int32)], axis=0)
    ew = jnp.concatenate([w1, w2, jnp.zeros((SUBLANES - 2, n), F32)], axis=0)
    return xn, ei, ew


def _outproj_kernel(ma_ref, mb_ref, w_ref, xr_ref, g_ref, b_ref, rwh_ref, rwl_ref, rb_ref,
                    x_ref, ei_ref, ew_ref, w_bf, *, sub):
    @pl.when(pl.program_id(0) == 0)
    def _():
        w_bf[...] = w_ref[...].astype(BF16)

    half = w_bf.shape[0] // 2
    for s in range(x_ref.shape[0] // sub):
        rows = slice(s * sub, (s + 1) * sub)
        v = (DEEPNORM_ALPHA * xr_ref[rows, :] + _dot(ma_ref[rows, :], w_bf[:half, :])
             + _dot(mb_ref[rows, :], w_bf[half:, :]))
        xn, ei, ew = _ln_router_tail(v, g_ref, b_ref, rwh_ref, rwl_ref, rb_ref)
        x_ref[rows, :] = xn
        ei_ref[:, rows] = ei
        ew_ref[:, rows] = ew


def _outproj_ln_router(mix_a, mix_b, w_out, x_res, ln_g, ln_b, rwt, rb, tm, sub):
    t = x_res.shape[0]
    rwt_hi = rwt.astype(BF16)
    rwt_lo = (rwt - rwt_hi.astype(F32)).astype(BF16)
    tm = min(tm, t)
    sub = min(sub, tm)
    half = w_out.shape[0] // 2
    row = lambda w: pl.BlockSpec((tm, w), lambda i: (i, 0))
    full = lambda shape: pl.BlockSpec(shape, lambda i: (0,) * len(shape))
    return pl.pallas_call(
        functools.partial(_outproj_kernel, sub=sub),
        grid=(t // tm,),
        in_specs=[row(half), row(half),
                  pl.BlockSpec((2 * half, D_MODEL), lambda i: (0, 0), pipeline_mode=pl.Buffered(1)),
                  row(D_MODEL), full((1, D_MODEL)), full((1, D_MODEL)),
                  full((N_EXPERTS, D_MODEL)), full((N_EXPERTS, D_MODEL)), full((N_EXPERTS, 1))],
        out_specs=[row(D_MODEL),
                   pl.BlockSpec((SUBLANES, tm), lambda i: (0, i)), pl.BlockSpec((SUBLANES, tm), lambda i: (0, i))],
        out_shape=[jax.ShapeDtypeStruct((t, D_MODEL), F32),
                   jax.ShapeDtypeStruct((SUBLANES, t), jnp.int32), jax.ShapeDtypeStruct((SUBLANES, t), F32)],
        scratch_shapes=[pltpu.VMEM((2 * half, D_MODEL), BF16)],
        compiler_params=_params(("arbitrary",)),
        name="outproj_ln_router",
    )(mix_a, mix_b, w_out, x_res, ln_g, ln_b, rwt_hi, rwt_lo, rb)


def _rank_kernel(e_ref, rank_ref, cnt_ref, base_ref, *, tn):
    @pl.when(pl.program_id(0) == 0)
    def _():
        base_ref[...] = jnp.zeros(base_ref.shape, F32)

    e = e_ref[...]
    rows = lax.broadcasted_iota(jnp.int32, (N_EXPERTS, tn), 0)
    oh1 = rows == e[0:1, :]
    oh2 = rows == e[1:2, :]
    oh = jnp.where(jnp.logical_or(oh1, oh2), 1.0, 0.0)
    r = lax.broadcasted_iota(jnp.int32, (tn, tn), 0)
    c = lax.broadcasted_iota(jnp.int32, (tn, tn), 1)
    earlier = jnp.where(r < c, 1.0, 0.0).astype(BF16)
    base = base_ref[...]
    before = _dot(oh.astype(BF16), earlier) + jnp.tile(base, (1, tn // LANES))
    r1 = jnp.sum(jnp.where(oh1, before, 0.0), axis=0, keepdims=True)
    r2 = jnp.sum(jnp.where(oh2, before, 0.0), axis=0, keepdims=True)
    rank_ref[...] = jnp.concatenate([r1, r2, jnp.zeros((SUBLANES - 2, tn), F32)], axis=0).astype(jnp.int32)
    base = base + jnp.sum(oh, axis=1, keepdims=True)
    base_ref[...] = base
    cnt_ref[...] = base.astype(jnp.int32)


def _rank(eidx, tn):
    t = eidx.shape[1]
    tn = min(tn, t)
    return pl.pallas_call(
        functools.partial(_rank_kernel, tn=tn),
        grid=(t // tn,),
        in_specs=[pl.BlockSpec((SUBLANES, tn), lambda i: (0, i))],
        out_specs=[pl.BlockSpec((SUBLANES, tn), lambda i: (0, i)),
                   pl.BlockSpec((N_EXPERTS, LANES), lambda i: (0, 0))],
        out_shape=[jax.ShapeDtypeStruct((SUBLANES, t), jnp.int32),
                   jax.ShapeDtypeStruct((N_EXPERTS, LANES), jnp.int32)],
        scratch_shapes=[pltpu.VMEM((N_EXPERTS, LANES), F32)],
        compiler_params=_params(("arbitrary",)),
        name="moe_rank",
    )(eidx)


def _moe_kernel(te_ref, nu_ref, xs_ref, wg_ref, wu_ref, wd_ref, *rest, tile0):
    y_ref, wgu_s, wd_s = rest[-3:]
    i = pl.program_id(0)
    tile = tile0 + i
    used = tile < nu_ref[0]
    fresh = jnp.logical_or(i == 0, te_ref[tile] != te_ref[jnp.maximum(tile - 1, 0)])

    @pl.when(jnp.logical_and(used, fresh))
    def _():
        wgu_s[:, :D_EXPERT] = wg_ref[...].astype(BF16)
        wgu_s[:, D_EXPERT:] = wu_ref[...].astype(BF16)
        wd_s[...] = wd_ref[...].astype(BF16)

    @pl.when(used)
    def _():
        hgu = _dot(xs_ref[...].astype(BF16), wgu_s[...])
        hg = hgu[:, :D_EXPERT]
        hu = hgu[:, D_EXPERT:]
        h = (hg / (1.0 + jnp.exp(-hg))) * hu
        y_ref[...] = _dot(h.astype(BF16), wd_s[...])

    @pl.when(jnp.logical_not(used))
    def _():
        y_ref[...] = jnp.zeros(y_ref.shape, F32)


def _moe_part(tile_expert, n_used, xs_part, w_gate, w_up, w_down, ys_prev, tile0, n_tiles, tm):
    nt = xs_part.shape[0] // tm
    in_specs = [
        pl.BlockSpec((tm, D_MODEL), lambda i, te, nu: (i, 0)),
        pl.BlockSpec((None, D_MODEL, D_EXPERT), lambda i, te, nu: (te[tile0 + i], 0, 0)),
        pl.BlockSpec((None, D_MODEL, D_EXPERT), lambda i, te, nu: (te[tile0 + i], 0, 0)),
        pl.BlockSpec((None, D_EXPERT, D_MODEL), lambda i, te, nu: (te[tile0 + i], 0, 0)),
    ]
    args = [tile_expert, n_used, xs_part, w_gate, w_up, w_down]
    aliases = {}
    if ys_prev is not None:
        in_specs.append(pl.BlockSpec(memory_space=pl.ANY))
        args.append(ys_prev)
        aliases = {len(args) - 1: 0}
    grid_spec = pltpu.PrefetchScalarGridSpec(
        num_scalar_prefetch=2,
        grid=(nt,),
        in_specs=in_specs,
        out_specs=pl.BlockSpec((tm, D_MODEL), lambda i, te, nu: (tile0 + i, 0)),
        scratch_shapes=[pltpu.VMEM((D_MODEL, 2 * D_EXPERT), BF16), pltpu.VMEM((D_EXPERT, D_MODEL), BF16)],
    )
    return pl.pallas_call(
        functools.partial(_moe_kernel, tile0=tile0),
        grid_spec=grid_spec,
        out_shape=jax.ShapeDtypeStruct((n_tiles * tm, D_MODEL), F32),
        input_output_aliases=aliases,
        compiler_params=_params(("arbitrary",)),
        name="moe",
    )(*args)


def _moe_ln_kernel(x_ref, y1_ref, y2_ref, w_ref, g_ref, b_ref, *rest):
    o_ref = rest[-1]
    w = w_ref[...]
    v = DEEPNORM_ALPHA * x_ref[...] + (w[:, 0:1] * y1_ref[...] + w[:, 1:2] * y2_ref[...])
    o_ref[...] = _layer_norm(v, g_ref[...], b_ref[...])


def _moe_ln_part(x, y1, y2, w_cols, ln_g, ln_b, out_prev, row0, tm):
    t = x.shape[0]
    b0 = row0 // tm
    here = pl.BlockSpec((tm, D_MODEL), lambda i: (i, 0))
    there = pl.BlockSpec((tm, D_MODEL), lambda i: (b0 + i, 0))
    vec = pl.BlockSpec((1, D_MODEL), lambda i: (0, 0))
    in_specs = [there, here, here, pl.BlockSpec((tm, SUBLANES), lambda i: (b0 + i, 0)), vec, vec]
    args = [x, y1, y2, w_cols, ln_g, ln_b]
    aliases = {}
    if out_prev is not None:
        in_specs.append(pl.BlockSpec(memory_space=pl.ANY))
        args.append(out_prev)
        aliases = {len(args) - 1: 0}
    return pl.pallas_call(
        _moe_ln_kernel,
        grid=(y1.shape[0] // tm,),
        in_specs=in_specs,
        out_specs=there,
        out_shape=jax.ShapeDtypeStruct((t, D_MODEL), F32),
        input_output_aliases=aliases,
        compiler_params=_params(("parallel",)),
        name="moe_ln",
    )(*args)


def _moe_block(x, eidx, ew, w_gate, w_up, w_down, ln_g, ln_b, tm):
    t = x.shape[0]
    tm = min(tm, t)
    rank, cnt = _rank(eidx, 512)
    counts = cnt[:, 0]
    padded = ((counts + tm - 1) // tm) * tm
    ends = jnp.cumsum(padded)
    starts = ends - padded
    e2 = eidx[:2]
    start_of = jnp.zeros_like(e2)
    for k in range(N_EXPERTS):
        start_of = jnp.where(e2 == k, starts[k], start_of)
    dest = start_of + rank[:2]
    t_pad = 2 * t + N_EXPERTS * tm
    n_tiles = t_pad // tm
    tok = jnp.tile(jnp.arange(t, dtype=jnp.int32), 2)
    src_tok = (jnp.arange(t_pad, dtype=jnp.int32) % t).at[dest.reshape(2 * t)].set(
        tok, mode="promise_in_bounds", unique_indices=True)
    tile_start = jnp.arange(n_tiles, dtype=jnp.int32) * tm
    tile_expert = jnp.minimum(jnp.sum((ends[None, :] <= tile_start[:, None]).astype(jnp.int32), axis=1),
                              N_EXPERTS - 1)
    n_used = (ends[-1] // tm).astype(jnp.int32).reshape(1)
    ys = None
    tiles_per = n_tiles // MOE_PARTS
    for p in range(MOE_PARTS):
        lo = p * tiles_per * tm
        xs_p = x.at[src_tok[lo:lo + tiles_per * tm]].get(mode="promise_in_bounds")
        ys = _moe_part(tile_expert, n_used, xs_p, w_gate, w_up, w_down, ys, p * tiles_per, n_tiles, tm)
    out = None
    rows_per = t // MOE_PARTS
    w_cols = ew.T
    for p in range(MOE_PARTS):
        lo = p * rows_per
        y1 = ys.at[dest[0, lo:lo + rows_per]].get(mode="promise_in_bounds")
        y2 = ys.at[dest[1, lo:lo + rows_per]].get(mode="promise_in_bounds")
        out = _moe_ln_part(x, y1, y2, w_cols, ln_g, ln_b, out, lo, tm)
    return out


def kernel(x, router_w, router_b, l0_w_in, l0_diff_lam_q1, l0_diff_lam_k1, l0_diff_lam_q2, l0_diff_lam_k2, l0_diff_subln_g, l0_gla_w_gate, l0_gla_b_gate, l0_gla_norm_g, l0_w_out, l0_ln1_g, l0_ln1_b, l0_moe_w_gate, l0_moe_w_up, l0_moe_w_down, l0_ln2_g, l0_ln2_b, l1_w_in, l1_s5_a_re, l1_s5_a_im, l1_s5_log_step, l1_s5_b_re, l1_s5_b_im, l1_s5_c_re, l1_s5_c_im, l1_s5_d, l1_s5_w_glu, l1_s5_b_glu, l1_w_out, l1_ln1_g, l1_ln1_b, l1_moe_w_gate, l1_moe_w_up, l1_moe_w_down, l1_ln2_g, l1_ln2_b):
    bsz, t, d = x.shape
    assert bsz == 1 and d == D_MODEL
    x0 = x.reshape(t, d)
    vec = lambda p: p.reshape(1, -1).astype(F32)
    rwt = router_w.T.astype(F32)
    rb = router_b.reshape(N_EXPERTS, 1).astype(F32)

    lambda_init = 0.8 - 0.6 * math.exp(-0.3 * 0)
    c = [0, 1024, 2048, 3072, 3584, 4096, 5120, 5136, 6160]
    q_scale = math.log2(math.e) * DIFF_HEAD_DIM ** -0.5
    w0 = jnp.concatenate([l0_w_in[:, :c[1]] * q_scale, l0_w_in[:, c[1]:c[6]], l0_w_in[:, c[7]:c[8]],
                          l0_w_in[:, c[6]:c[7]], jnp.zeros((d, L0_GB_PAD - GLA_GATE_RANK), l0_w_in.dtype)],
                         axis=1).astype(BF16)
    h0 = _matmul(x0, w0, 1024, L0_IN_TN, BF16)
    lam_vec = jnp.stack([l0_diff_lam_q1, l0_diff_lam_k1, l0_diff_lam_q2, l0_diff_lam_k2]).astype(F32)
    o_a = _diff_attention(h0, lam_vec, vec(l0_diff_subln_g), lambda_init, 1024)
    wg_pad = jnp.concatenate([l0_gla_w_gate, jnp.zeros((LANES - GLA_GATE_RANK, GLA_HEADS * GLA_DK), F32)],
                             axis=0).astype(BF16)
    o_b = _gla(h0, wg_pad, vec(l0_gla_b_gate), vec(l0_gla_norm_g), 256)
    x1, eidx, ew = _outproj_ln_router(o_a, o_b, l0_w_out, x0, vec(l0_ln1_g), vec(l0_ln1_b),
                                      rwt, rb, 512, 128)
    x2 = _moe_block(x1, eidx, ew, l0_moe_w_gate, l0_moe_w_up, l0_moe_w_down, vec(l0_ln2_g), vec(l0_ln2_b), 256)

    h1 = _matmul(x2, l1_w_in.astype(BF16), 1024, 1024, BF16)
    o_c = _stick_breaking(h1, 256)
    lbr, lbi, bbr, bbi = _s5_prep(l1_s5_a_re, l1_s5_a_im, l1_s5_log_step, l1_s5_b_re, l1_s5_b_im)
    bd, cd, pw8 = _s5_tile_params(lbr[::S5_GROUP], lbi[::S5_GROUP], bbr.reshape(S5_GROUPS, S5_GROUP, S5_STATE),
                                  bbi.reshape(S5_GROUPS, S5_GROUP, S5_STATE),
                                  l1_s5_c_re.astype(F32), l1_s5_c_im.astype(F32))
    y_s5 = _s5_core(h1, bd, cd, pw8, 256)
    o_d = _s5_glu(y_s5, h1, vec(l1_s5_d), l1_s5_w_glu.astype(BF16), vec(l1_s5_b_glu), 512)
    x3, eidx1, ew1 = _outproj_ln_router(o_c, o_d, l1_w_out, x2, vec(l1_ln1_g), vec(l1_ln1_b),
                                        rwt, rb, 512, 128)
    x4 = _moe_block(x3, eidx1, ew1, l1_moe_w_gate, l1_moe_w_up, l1_moe_w_down, vec(l1_ln2_g), vec(l1_ln2_b), 256)
    return x4.reshape(bsz, t, d)
```

```python
import functools
import math

import jax
import jax.numpy as jnp
from jax import lax
from jax.experimental import pallas as pl
from jax.experimental.pallas import tpu as pltpu

F32 = jnp.float32
BF16 = jnp.bfloat16

D_MODEL = 2048
DEPTH = 2
DIFF_HEADS = 8
DIFF_HEAD_DIM = 64
DIFF_V_DIM = 128
GLA_HEADS = 4
GLA_DK = 128
GLA_DV = 256
GLA_GATE_RANK = 16
GLA_GATE_TEMP = 16.0
GLA_CHUNK = 64
SB_HEADS = 8
SB_HEAD_DIM = 128
S5_CHANNELS = 1024
S5_GROUP = 16
S5_GROUPS = 64
S5_STATE = 64
N_EXPERTS = 16
N_GROUPS = 4
EXPERTS_PER_GROUP = 4
D_EXPERT = 640
DEEPNORM_ALPHA = (2.0 * DEPTH) ** 0.25
LN_EPS = 1e-5

LANES = 128
SUBLANES = 8
VMEM_LIMIT = 56 * 1024 * 1024
SB_LOG_FLOOR = 120.0
SB_HEADS_PER_STEP = 4
MOE_PARTS = 2

MXU_N = 256
L0_GB_PAD = MXU_N
L0_IN_TN = 5 * MXU_N

S5_KT = 4
S5_KT_CH = S5_CHANNELS // S5_KT
S5_KT_STATES = (S5_GROUPS // S5_KT) * S5_STATE
S5_SCAN_W = 512
S5_L = 8


def _params(sem, flags=None):
    return pltpu.CompilerParams(dimension_semantics=sem, vmem_limit_bytes=VMEM_LIMIT, flags=flags)


def _nt_dot(a, b):
    return lax.dot_general(a, b, (((1,), (1,)), ((), ())), preferred_element_type=F32)


def _dot(a, b):
    return jnp.dot(a, b, preferred_element_type=F32)


def _split_dot_left(m, x):
    hi = x.astype(BF16)
    lo = (x - hi.astype(F32)).astype(BF16)
    return _dot(m, hi) + _dot(m, lo)


def _layer_norm(v, g, b):
    mu = jnp.mean(v, axis=-1, keepdims=True)
    d = v - mu
    var = jnp.mean(d * d, axis=-1, keepdims=True)
    return d * lax.rsqrt(var + LN_EPS) * g + b


def _mm_kernel(a_ref, b_ref, o_ref, a_bf):
    @pl.when(pl.program_id(1) == 0)
    def _():
        a_bf[...] = a_ref[...].astype(BF16)

    o_ref[...] = _dot(a_bf[...], b_ref[...]).astype(o_ref.dtype)


def _matmul(a, b, tm, tn, out_dtype):
    m, k = a.shape
    n = b.shape[1]
    tm = min(tm, m)
    return pl.pallas_call(
        _mm_kernel,
        grid=(m // tm, n // tn),
        in_specs=[pl.BlockSpec((tm, k), lambda i, j: (i, 0)),
                  pl.BlockSpec((k, tn), lambda i, j: (0, j))],
        out_specs=pl.BlockSpec((tm, tn), lambda i, j: (i, j)),
        out_shape=jax.ShapeDtypeStruct((m, n), out_dtype),
        scratch_shapes=[pltpu.VMEM((tm, k), BF16)],
        compiler_params=_params(("parallel", "arbitrary")),
        name="in_proj",
    )(a, b)


def _diff_attn_kernel(lam_ref, g_ref, q_ref, k_ref, v_ref, o_ref, vx_ref, m_ref, acc_ref, sa_ref, sb_ref, *, tq,
                      lambda_init):
    i = pl.program_id(1)
    dv = DIFF_V_DIM

    @pl.when(i == 0)
    def _():
        vx_ref[:, :dv] = v_ref[...]
        vx_ref[:, dv:] = jnp.ones((vx_ref.shape[0], dv), BF16)

    m_ref[...] = jnp.full(m_ref.shape, -jnp.inf, F32)
    acc_ref[...] = jnp.zeros(acc_ref.shape, F32)
    q = q_ref[...]
    lane = lax.broadcasted_iota(jnp.int32, q.shape, 1)
    zero = jnp.zeros_like(q)
    halves = (jnp.where(lane < DIFF_HEAD_DIM, q, zero), jnp.where(lane >= DIFF_HEAD_DIM, q, zero))

    def scores(j, mi):
        off = pl.multiple_of(j * tq, tq)
        return _nt_dot(halves[mi], k_ref[pl.ds(off, tq), :])

    def softmax_pv(j, s_ref, masked, r0=0, c0=0, c1=None):
        c1 = tq if c1 is None else c1
        nr, nc = tq - r0, c1 - c0
        off = pl.multiple_of(j * tq, tq)
        vx = vx_ref[pl.ds(off + c0, nc), :]
        if masked:
            row = lax.broadcasted_iota(jnp.int32, (nr, nc), 0) + r0
            col = lax.broadcasted_iota(jnp.int32, (nr, nc), 1) + c0
            causal = col <= row
        for mi in range(2):
            s = s_ref[mi, r0:, c0:c1]
            if masked:
                s = jnp.where(causal, s, -jnp.inf)
            m_prev = m_ref[mi, r0:, :]
            m_new = jnp.maximum(m_prev, jnp.max(s, axis=1, keepdims=True))
            p = jnp.exp2(s - jnp.tile(m_new, (1, nc // LANES)))
            alpha = jnp.exp2(m_prev - m_new)
            acc_ref[mi, r0:, :] = jnp.tile(alpha, (1, 2)) * acc_ref[mi, r0:, :] + _dot(p.astype(BF16), vx)
            m_ref[mi, r0:, :] = m_new

    def diagonal(j, s_ref):
        half = tq // 2
        softmax_pv(j, s_ref, True, c1=half)
        softmax_pv(j, s_ref, True, r0=half, c0=half)

    def scores_to(j, s_ref):
        for mi in range(2):
            s_ref[mi] = scores(j, mi)

    scores_to(0, sa_ref)

    def pair(t, carry):
        j = 2 * t
        scores_to(j + 1, sb_ref)
        softmax_pv(j, sa_ref, False)
        scores_to(j + 2, sa_ref)
        softmax_pv(j + 1, sb_ref, False)
        return carry

    lax.fori_loop(0, i // 2, pair, 0)

    @pl.when(i % 2 == 1)
    def _():
        softmax_pv(i - 1, sa_ref, False)
        scores_to(i, sa_ref)

    diagonal(i, sa_ref)

    lv = lam_ref[...]
    lam = (jnp.exp(jnp.sum(lv[0:1] * lv[1:2], axis=1, keepdims=True))
           - jnp.exp(jnp.sum(lv[2:3] * lv[3:4], axis=1, keepdims=True)) + lambda_init)
    a0 = acc_ref[0]
    a1 = acc_ref[1]
    o = a0[:, :dv] / a0[:, dv:] - lam * (a1[:, :dv] / a1[:, dv:])
    ms = jnp.mean(o * o, axis=-1, keepdims=True)
    o = o * lax.rsqrt(ms + LN_EPS) * g_ref[...] * (1.0 - lambda_init)
    o_ref[...] = o.astype(o_ref.dtype)


def _diff_attention(h0, lam_vec, subln_g, lambda_init, tq):
    t = h0.shape[0]
    tq = min(tq, t)
    nq = t // tq
    kern = functools.partial(_diff_attn_kernel, tq=tq, lambda_init=lambda_init)
    return pl.pallas_call(
        kern,
        grid=(DIFF_HEADS, nq),
        in_specs=[
            pl.BlockSpec((4, DIFF_HEAD_DIM), lambda h, i: (0, 0)),
            pl.BlockSpec((1, DIFF_V_DIM), lambda h, i: (0, 0)),
            pl.BlockSpec((tq, LANES), lambda h, i: (i, h)),
            pl.BlockSpec((t, LANES), lambda h, i: (0, DIFF_HEADS + h)),
            pl.BlockSpec((t, LANES), lambda h, i: (0, 2 * DIFF_HEADS + h)),
        ],
        out_specs=pl.BlockSpec((tq, LANES), lambda h, i: (i, h)),
        out_shape=jax.ShapeDtypeStruct((t, DIFF_HEADS * DIFF_V_DIM), BF16),
        scratch_shapes=[pltpu.VMEM((t, 2 * DIFF_V_DIM), BF16), pltpu.VMEM((2, tq, LANES), F32),
                        pltpu.VMEM((2, tq, 2 * DIFF_V_DIM), F32),
                        pltpu.VMEM((2, tq, tq), F32), pltpu.VMEM((2, tq, tq), F32)],
        compiler_params=_params(("arbitrary", "arbitrary")),
        name="diff_attn",
    )(lam_vec, subln_g, h0, h0, h0)


def _gla_kernel(q_ref, k_ref, v_ref, r_ref, gb_ref, wg_ref, bg_ref, ng_ref, o_ref, st_ref, *, tg):
    @pl.when(pl.program_id(0) == 0)
    def _():
        st_ref[...] = jnp.zeros(st_ref.shape, F32)

    c = GLA_CHUNK
    gate = _dot(gb_ref[...], wg_ref[...]) + bg_ref[...]
    log_a = -(jnp.maximum(-gate, 0.0) + jnp.log(1.0 + jnp.exp(-jnp.abs(gate)))) / GLA_GATE_TEMP
    row = lax.broadcasted_iota(jnp.int32, (c, c), 0)
    col = lax.broadcasted_iota(jnp.int32, (c, c), 1)
    tril = row >= col
    tri = jnp.where(tril, 1.0, 0.0).astype(BF16)
    for h in range(GLA_HEADS):
        ksl = slice(h * GLA_DK, (h + 1) * GLA_DK)
        vsl = slice(h * GLA_DV, (h + 1) * GLA_DV)
        for ci in range(tg // c):
            rs = slice(ci * c, (ci + 1) * c)
            la = log_a[rs, ksl]
            b = _split_dot_left(tri, la)
            b_last = b[c - 1:c, :]
            qf = q_ref[rs, ksl].astype(F32) * (GLA_DK ** -0.5)
            kf = k_ref[rs, ksl].astype(F32)
            v = v_ref[rs, vsl]
            q_dec = (qf * jnp.exp(b)).astype(BF16)
            k_dec = (kf * jnp.exp(-b)).astype(BF16)
            k_end = (kf * jnp.exp(b_last - b)).astype(BF16)
            scores = jnp.where(tril, _nt_dot(q_dec, k_dec), 0.0)
            st = st_ref[h]
            o = _dot(scores.astype(BF16), v) + _nt_dot(q_dec, st.astype(BF16))
            kv_t = lax.dot_general(v, k_end, (((0,), (0,)), ((), ())), preferred_element_type=F32)
            st_ref[h] = jnp.exp(b_last) * st + kv_t
            ms = jnp.mean(o * o, axis=-1, keepdims=True)
            o = o * lax.rsqrt(ms + LN_EPS) * ng_ref[...]
            r = r_ref[rs, vsl].astype(F32)
            o_ref[rs, vsl] = (o * (r / (1.0 + jnp.exp(-r)))).astype(o_ref.dtype)


def _gla(h0, wg_pad, b_gate, norm_g, tg):
    t = h0.shape[0]
    tg = min(tg, t)
    qk = GLA_HEADS * GLA_DK
    vw = GLA_HEADS * GLA_DV
    return pl.pallas_call(
        functools.partial(_gla_kernel, tg=tg),
        grid=(t // tg,),
        in_specs=[
            pl.BlockSpec((tg, qk), lambda i: (i, 3072 // qk)),
            pl.BlockSpec((tg, qk), lambda i: (i, 3584 // qk)),
            pl.BlockSpec((tg, vw), lambda i: (i, 4096 // vw)),
            pl.BlockSpec((tg, vw), lambda i: (i, 5120 // vw)),
            pl.BlockSpec((tg, LANES), lambda i: (i, 6144 // LANES)),
            pl.BlockSpec((LANES, qk), lambda i: (0, 0)),
            pl.BlockSpec((1, qk), lambda i: (0, 0)),
            pl.BlockSpec((1, GLA_DV), lambda i: (0, 0)),
        ],
        out_specs=pl.BlockSpec((tg, vw), lambda i: (i, 0)),
        out_shape=jax.ShapeDtypeStruct((t, vw), BF16),
        scratch_shapes=[pltpu.VMEM((GLA_HEADS, GLA_DV, GLA_DK), F32)],
        compiler_params=_params(("arbitrary",)),
        name="gla",
    )(h0, h0, h0, h0, h0, wg_pad, b_gate, norm_g)


def _sb_kernel(q_ref, k_ref, v_ref, o_ref, acc_ref, c_ref, *, tq):
    i = pl.program_id(1)
    hd = SB_HEAD_DIM
    acc_ref[...] = jnp.zeros(acc_ref.shape, F32)
    c_ref[...] = jnp.zeros(c_ref.shape, F32)
    row = lax.broadcasted_iota(jnp.int32, (tq, tq), 0)
    col = lax.broadcasted_iota(jnp.int32, (tq, tq), 1)
    strict = col < row
    tri = jnp.where(row > col, 1.0, 0.0).astype(BF16)

    def block(jb, masked):
        off = pl.multiple_of(jb * tq, tq)
        cmax = None
        for hh in range(SB_HEADS_PER_STEP):
            cols = slice(hh * hd, (hh + 1) * hd)
            k = k_ref[pl.ds(off, tq), cols]
            v = v_ref[pl.ds(off, tq), cols]
            z = _nt_dot(q_ref[:, cols], k) * (SB_HEAD_DIM ** -0.5)
            lsn = -(jnp.maximum(z, 0.0) + jnp.log(1.0 + jnp.exp(-jnp.abs(z))))
            lm = jnp.where(strict, lsn, 0.0) if masked else lsn
            excl = _dot(lm.astype(BF16), tri) + c_ref[hh]
            w = jnp.exp(z + lsn + excl)
            if masked:
                w = jnp.where(strict, w, 0.0)
            acc_ref[:, cols] += _dot(w.astype(BF16), v)
            c_new = c_ref[hh] + jnp.sum(lm, axis=1, keepdims=True)
            c_ref[hh] = c_new
            hmax = jnp.max(c_new)
            cmax = hmax if cmax is None else jnp.maximum(cmax, hmax)
        return cmax

    cmax = block(i, True)

    def cond(carry):
        jb, cm = carry
        return jnp.logical_and(jb >= 0, cm > -SB_LOG_FLOOR)

    def body(carry):
        jb, _ = carry
        return jb - 1, block(jb, False)

    lax.while_loop(cond, body, (i - 1, cmax))
    o_ref[...] = acc_ref[...].astype(o_ref.dtype)


def _stick_breaking(h1, tq):
    t = h1.shape[0]
    tq = min(tq, t)
    ng = SB_HEADS // SB_HEADS_PER_STEP
    wd = SB_HEADS_PER_STEP * SB_HEAD_DIM
    return pl.pallas_call(
        functools.partial(_sb_kernel, tq=tq),
        grid=(ng, t // tq),
        in_specs=[
            pl.BlockSpec((tq, wd), lambda h, i: (i, h)),
            pl.BlockSpec((t, wd), lambda h, i: (0, ng + h), pipeline_mode=pl.Buffered(1)),
            pl.BlockSpec((t, wd), lambda h, i: (0, 2 * ng + h), pipeline_mode=pl.Buffered(1)),
        ],
        out_specs=pl.BlockSpec((tq, wd), lambda h, i: (i, h)),
        out_shape=jax.ShapeDtypeStruct((t, SB_HEADS * SB_HEAD_DIM), BF16),
        scratch_shapes=[pltpu.VMEM((tq, wd), F32), pltpu.VMEM((SB_HEADS_PER_STEP, tq, 1), F32)],
        compiler_params=_params(("parallel", "arbitrary")),
        name="stick_breaking",
    )(h1, h1, h1)


def _s5_prep_kernel(are_ref, aim_ref, ls_ref, bre_ref, bim_ref, lbr_ref, lbi_ref, bbr_ref, bbi_ref):
    step = jnp.exp(ls_ref[...])
    lam_re = are_ref[...]
    lam_im = aim_ref[...]
    mag = jnp.exp(step * lam_re)
    lb_re = mag * jnp.cos(step * lam_im)
    lb_im = mag * jnp.sin(step * lam_im)
    den = lam_re * lam_re + lam_im * lam_im
    n_re = lb_re - 1.0
    f_re = (n_re * lam_re + lb_im * lam_im) / den
    f_im = (lb_im * lam_re - n_re * lam_im) / den
    lbr_ref[...] = lb_re
    lbi_ref[...] = lb_im
    bbr_ref[...] = f_re * bre_ref[...] - f_im * bim_ref[...]
    bbi_ref[...] = f_re * bim_ref[...] + f_im * bre_ref[...]


def _s5_prep(a_re, a_im, log_step, b_re, b_im):
    rows = S5_GROUPS * S5_GROUP
    rep = lambda p: jnp.repeat(p, S5_GROUP, axis=0)
    to_rows = lambda b: jnp.transpose(b, (0, 2, 1)).reshape(rows, S5_STATE)
    spec = pl.BlockSpec((rows, S5_STATE), lambda: (0, 0))
    shp = jax.ShapeDtypeStruct((rows, S5_STATE), F32)
    return pl.pallas_call(
        _s5_prep_kernel,
        in_specs=[spec, spec, pl.BlockSpec((rows, 1), lambda: (0, 0)), spec, spec],
        out_specs=[spec] * 4,
        out_shape=[shp] * 4,
        name="s5_prep",
    )(rep(a_re), rep(a_im), rep(log_step[:, None]), to_rows(b_re), to_rows(b_im))


def _cmul(ar, ai, br, bi):
    return ar * br - ai * bi, ar * bi + ai * br


def _block_diag_tiles(m, rows_per_group, cols_per_group):
    gpt = S5_GROUPS // S5_KT
    m = m.reshape(S5_KT, gpt, rows_per_group, cols_per_group)
    eye = jnp.eye(gpt, dtype=m.dtype)
    bd = jnp.einsum("kgrc,gh->kgrhc", m, eye)
    return bd.reshape(S5_KT, gpt * rows_per_group, gpt * cols_per_group)


def _s5_tile_params(lb_re, lb_im, bb_re, bb_im, c_re, c_im):
    bd = jnp.concatenate([_block_diag_tiles(bb_re, S5_GROUP, S5_STATE),
                          _block_diag_tiles(bb_im, S5_GROUP, S5_STATE)], axis=2)
    cd = jnp.concatenate([_block_diag_tiles(jnp.transpose(c_re, (0, 2, 1)), S5_STATE, S5_GROUP),
                          _block_diag_tiles(-jnp.transpose(c_im, (0, 2, 1)), S5_STATE, S5_GROUP)], axis=1)
    pr = [jnp.ones_like(lb_re)]
    pi = [jnp.zeros_like(lb_re)]
    for _ in range(S5_L):
        nr, ni = _cmul(pr[-1], pi[-1], lb_re, lb_im)
        pr.append(nr)
        pi.append(ni)

    def table(p):
        rows = jnp.transpose(jnp.stack(p).reshape(S5_L + 1, S5_KT, S5_KT_STATES), (1, 0, 2))
        return jnp.pad(rows, ((0, 0), (0, LANES - (S5_L + 1)), (0, 0)))

    return bd, cd, jnp.stack([table(pr), table(pi)], axis=1)


def _s5_core_kernel(u_ref, bd_ref, cd_ref, pw8_ref, y_ref, hs_ref, hp_ref, pw_ref, st_ref, uf_ref,
                    mt_s, bbs_s, ccs_s, *, tcr):
    ns = S5_KT_STATES
    pad = SUBLANES
    cw = S5_KT_CH

    @pl.when(pl.program_id(1) == 0)
    def _():
        b_re = bd_ref[:, :ns]
        b_im = bd_ref[:, ns:]
        cd_hi = cd_ref[...].astype(BF16)
        cd_lo = (cd_ref[...] - cd_hi.astype(F32)).astype(BF16)
        for k in range(S5_L):
            a = pw8_ref[0, k:k + 1, :]
            b = pw8_ref[1, k:k + 1, :]
            dk = jnp.concatenate([b_re * a - b_im * b, b_re * b + b_im * a], axis=1)
            s = S5_L - 1 - k
            bbs_s[s * cw:(s + 1) * cw, :] = dk.astype(BF16)
            dk_hi = dk.astype(BF16)
            dk_lo = (dk - dk_hi.astype(F32)).astype(BF16)
            kk = (_dot(dk_hi, cd_hi) + (_dot(dk_lo, cd_hi) + _dot(dk_hi, cd_lo))).astype(BF16)
            for s0 in range(S5_L - k):
                mt_s[s0 * cw:(s0 + 1) * cw, (s0 + k) * cw:(s0 + k + 1) * cw] = kk
        pw_t = (jnp.transpose(pw8_ref[0]), jnp.transpose(pw8_ref[1]))
        c_re = cd_ref[:ns, :]
        c_im = cd_ref[ns:, :]
        for i in range(S5_L):
            a = pw_t[0][:, i + 1:i + 2]
            b = pw_t[1][:, i + 1:i + 2]
            ccs_s[:ns, i * cw:(i + 1) * cw] = (c_re * a + c_im * b).astype(BF16)
            ccs_s[ns:, i * cw:(i + 1) * cw] = (c_im * a - c_re * b).astype(BF16)

        hp_ref[...] = jnp.zeros(hp_ref.shape, F32)
        hs_ref[0:pad, :] = jnp.zeros((pad, 2 * ns), F32)
        l1r = jnp.broadcast_to(pw8_ref[0, S5_L:S5_L + 1, :], (SUBLANES, ns))
        l1i = jnp.broadcast_to(pw8_ref[1, S5_L:S5_L + 1, :], (SUBLANES, ns))
        l2r, l2i = _cmul(l1r, l1i, l1r, l1i)
        l4r, l4i = _cmul(l2r, l2i, l2r, l2i)
        l8r, l8i = _cmul(l4r, l4i, l4r, l4i)
        ridx = lax.broadcasted_iota(jnp.int32, l1r.shape, 0)
        n = ridx + 1
        pr = jnp.ones_like(l1r)
        pi = jnp.zeros_like(l1r)
        for bit, (lr, li) in ((1, (l1r, l1i)), (2, (l2r, l2i)), (4, (l4r, l4i)), (8, (l8r, l8i))):
            nr, ni = _cmul(pr, pi, lr, li)
            take = (n & bit) != 0
            pr = jnp.where(take, nr, pr)
            pi = jnp.where(take, ni, pi)
        pw_ref[0] = pr
        pw_ref[1] = pi
        for s, (lr, li) in enumerate(((l1r, l1i), (l2r, l2i), (l4r, l4i))):
            keep = ridx >= (1 << s)
            st_ref[2 * s] = jnp.where(keep, lr, 0.0)
            st_ref[2 * s + 1] = jnp.where(keep, li, 0.0)

    nl = S5_KT_CH // LANES
    for h in range(nl):
        uf_ref[h] = u_ref[:, h * LANES:(h + 1) * LANES].astype(F32)
    uc = jnp.concatenate([uf_ref[h, pl.ds(s, tcr, stride=S5_L), :].astype(BF16)
                          for s in range(S5_L) for h in range(nl)], axis=1)
    hs_ref[pad:pad + tcr, :] = _dot(uc, bbs_s[...])
    w = S5_SCAN_W
    for cc in range(ns // w):
        re0 = cc * w
        im0 = ns + cc * w

        def body(r, carry, re0=re0, im0=im0):
            hpr, hpi = carry
            rows = pl.ds(pl.multiple_of(pad + r * SUBLANES, SUBLANES), SUBLANES)
            xr = hs_ref[rows, re0:re0 + w]
            xi = hs_ref[rows, im0:im0 + w]
            for s in range(3):
                sr = pltpu.roll(xr, 1 << s, 0)
                si = pltpu.roll(xi, 1 << s, 0)
                ar = st_ref[2 * s, :, re0:re0 + w]
                ai = st_ref[2 * s + 1, :, re0:re0 + w]
                xr, xi = xr + ar * sr - ai * si, xi + ar * si + ai * sr
            pr = pw_ref[0, :, re0:re0 + w]
            pi = pw_ref[1, :, re0:re0 + w]
            xr, xi = xr + pr * hpr - pi * hpi, xi + pr * hpi + pi * hpr
            hs_ref[rows, re0:re0 + w] = xr
            hs_ref[rows, im0:im0 + w] = xi
            return (jnp.broadcast_to(xr[SUBLANES - 1:SUBLANES, :], (SUBLANES, w)),
                    jnp.broadcast_to(xi[SUBLANES - 1:SUBLANES, :], (SUBLANES, w)))

        hpr, hpi = lax.fori_loop(0, tcr // SUBLANES, body, (hp_ref[:, re0:re0 + w], hp_ref[:, im0:im0 + w]))
        hp_ref[:, re0:re0 + w] = hpr
        hp_ref[:, im0:im0 + w] = hpi

    h_prev = hs_ref[pad - 1:pad - 1 + tcr, :]
    y_in = _dot(h_prev.astype(BF16), ccs_s[...])
    for i in range(S5_L):
        k_hi = (i + 1) * cw
        y_i = y_in[:, i * cw:k_hi] + _dot(uc[:, :k_hi], mt_s[:k_hi, i * cw:k_hi])
        for h in range(nl):
            uf_ref[h, pl.ds(i, tcr, stride=S5_L), :] = y_i[:, h * LANES:(h + 1) * LANES]
    for h in range(nl):
        y_ref[:, h * LANES:(h + 1) * LANES] = uf_ref[h]
    hs_ref[pad - 1:pad, :] = hs_ref[pad + tcr - 1:pad + tcr, :]


def _s5_core(h1, bd, cd, pw8, tcr):
    t = h1.shape[0]
    width = S5_L * S5_KT_CH
    tcr = min(tcr, t // S5_L)
    tc = tcr * S5_L
    ns = S5_KT_STATES
    u_col0 = 3 * S5_CHANNELS // S5_KT_CH
    once = pl.Buffered(1)
    return pl.pallas_call(
        functools.partial(_s5_core_kernel, tcr=tcr),
        grid=(S5_KT, t // tc),
        in_specs=[
            pl.BlockSpec((tc, S5_KT_CH), lambda k, i: (i, u_col0 + k)),
            pl.BlockSpec((None, S5_KT_CH, 2 * ns), lambda k, i: (k, 0, 0), pipeline_mode=once),
            pl.BlockSpec((None, 2 * ns, S5_KT_CH), lambda k, i: (k, 0, 0), pipeline_mode=once),
            pl.BlockSpec((None, 2, LANES, ns), lambda k, i: (k, 0, 0, 0), pipeline_mode=once),
        ],
        out_specs=pl.BlockSpec((tc, S5_KT_CH), lambda k, i: (i, k)),
        out_shape=jax.ShapeDtypeStruct((t, S5_CHANNELS), F32),
        scratch_shapes=[pltpu.VMEM((tcr + SUBLANES, 2 * ns), F32), pltpu.VMEM((SUBLANES, 2 * ns), F32),
                        pltpu.VMEM((2, SUBLANES, ns), F32), pltpu.VMEM((6, SUBLANES, ns), F32),
                        pltpu.VMEM((S5_KT_CH // LANES, tc, LANES), F32),
                        pltpu.VMEM((width, width), BF16), pltpu.VMEM((width, 2 * ns), BF16),
                        pltpu.VMEM((2 * ns, width), BF16)],
        compiler_params=_params(("arbitrary", "arbitrary")),
        name="s5_core",
    )(h1, bd, cd, pw8)


def _s5_glu_kernel(y_ref, u_ref, d_ref, wglu_ref, bglu_ref, o_ref):
    y = y_ref[...] + d_ref[...] * u_ref[...].astype(F32)
    g = 0.5 * y * (1.0 + jnp.tanh(math.sqrt(2.0 / math.pi) * (y + 0.044715 * (y * y * y))))
    gl = _dot(g.astype(BF16), wglu_ref[...]) + bglu_ref[...]
    o_ref[...] = (g / (1.0 + jnp.exp(-gl))).astype(o_ref.dtype)


def _s5_glu(y, h1, d_skip, w_glu, b_glu, tm):
    t = h1.shape[0]
    tm = min(tm, t)
    full = lambda shape: pl.BlockSpec(shape, lambda i: (0,) * len(shape))
    return pl.pallas_call(
        _s5_glu_kernel,
        grid=(t // tm,),
        in_specs=[pl.BlockSpec((tm, S5_CHANNELS), lambda i: (i, 0)),
                  pl.BlockSpec((tm, S5_CHANNELS), lambda i: (i, 3)),
                  full((1, S5_CHANNELS)), full((S5_CHANNELS, S5_CHANNELS)), full((1, S5_CHANNELS))],
        out_specs=pl.BlockSpec((tm, S5_CHANNELS), lambda i: (i, 0)),
        out_shape=jax.ShapeDtypeStruct((t, S5_CHANNELS), BF16),
        compiler_params=_params(("parallel",)),
        name="s5_glu",
    )(y, h1, d_skip, w_glu, b_glu)


def _route(probs, sel):
    rows = [sel[e:e + 1, :] for e in range(N_EXPERTS)]
    prow = [probs[e:e + 1, :] for e in range(N_EXPERTS)]
    best_score = None
    for g in range(N_GROUPS):
        a = rows[g * EXPERTS_PER_GROUP:(g + 1) * EXPERTS_PER_GROUP]
        score = None
        for x in range(EXPERTS_PER_GROUP):
            for y in range(x + 1, EXPERTS_PER_GROUP):
                pair = a[x] + a[y]
                score = pair if score is None else jnp.maximum(score, pair)
        if best_score is None:
            best_score, gbest = score, jnp.zeros(score.shape, jnp.int32)
            cs = list(a)
            cp = prow[0:EXPERTS_PER_GROUP]
        else:
            better = score > best_score
            best_score = jnp.where(better, score, best_score)
            gbest = jnp.where(better, g, gbest)
            cs = [jnp.where(better, a[x], cs[x]) for x in range(EXPERTS_PER_GROUP)]
            cp = [jnp.where(better, prow[g * EXPERTS_PER_GROUP + x], cp[x]) for x in range(EXPERTS_PER_GROUP)]

    def first_argmax(vals):
        bv, bi = vals[0], jnp.zeros(vals[0].shape, jnp.int32)
        for x in range(1, len(vals)):
            better = vals[x] > bv
            bv = jnp.where(better, vals[x], bv)
            bi = jnp.where(better, x, bi)
        return bi

    i1 = first_argmax(cs)
    i2 = first_argmax([jnp.where(i1 == x, -jnp.inf, cs[x]) for x in range(EXPERTS_PER_GROUP)])

    def pick(idx):
        out = cp[0]
        for x in range(1, EXPERTS_PER_GROUP):
            out = jnp.where(idx == x, cp[x], out)
        return out

    p1, p2 = pick(i1), pick(i2)
    tot = p1 + p2
    return gbest * EXPERTS_PER_GROUP + i1, gbest * EXPERTS_PER_GROUP + i2, p1 / tot, p2 / tot


def _ln_router_tail(v, g_ref, b_ref, rwh_ref, rwl_ref, rb_ref):
    xn = _layer_norm(v, g_ref[...], b_ref[...])
    xh = xn.astype(BF16)
    xl = (xn - xh.astype(F32)).astype(BF16)
    logits = _nt_dot(rwh_ref[...], xh) + (_nt_dot(rwl_ref[...], xh) + _nt_dot(rwh_ref[...], xl))
    mx = jnp.max(logits, axis=0, keepdims=True)
    ex = jnp.exp(logits - mx)
    probs = ex / jnp.sum(ex, axis=0, keepdims=True)
    e1, e2, w1, w2 = _route(probs, probs + rb_ref[...])
    n = e1.shape[1]
    ei = jnp.concatenate([e1, e2, jnp.zeros((SUBLANES - 2, n), jnp.int32)], axis=0)
    ew = jnp.concatenate([w1, w2, jnp.zeros((SUBLANES - 2, n), F32)], axis=0)
    return xn, ei, ew


def _outproj_kernel(ma_ref, mb_ref, w_ref, xr_ref, g_ref, b_ref, rwh_ref, rwl_ref, rb_ref,
                    x_ref, ei_ref, ew_ref, w_bf, *, sub):
    @pl.when(pl.program_id(0) == 0)
    def _():
        w_bf[...] = w_ref[...].astype(BF16)

    half = w_bf.shape[0] // 2
    for s in range(x_ref.shape[0] // sub):
        rows = slice(s * sub, (s + 1) * sub)
        v = (DEEPNORM_ALPHA * xr_ref[rows, :] + _dot(ma_ref[rows, :], w_bf[:half, :])
             + _dot(mb_ref[rows, :], w_bf[half:, :]))
        xn, ei, ew = _ln_router_tail(v, g_ref, b_ref, rwh_ref, rwl_ref, rb_ref)
        x_ref[rows, :] = xn
        ei_ref[:, rows] = ei
        ew_ref[:, rows] = ew


def _outproj_ln_router(mix_a, mix_b, w_out, x_res, ln_g, ln_b, rwt, rb, tm, sub):
    t = x_res.shape[0]
    rwt_hi = rwt.astype(BF16)
    rwt_lo = (rwt - rwt_hi.astype(F32)).astype(BF16)
    tm = min(tm, t)
    sub = min(sub, tm)
    half = w_out.shape[0] // 2
    row = lambda w: pl.BlockSpec((tm, w), lambda i: (i, 0))
    full = lambda shape: pl.BlockSpec(shape, lambda i: (0,) * len(shape))
    return pl.pallas_call(
        functools.partial(_outproj_kernel, sub=sub),
        grid=(t // tm,),
        in_specs=[row(half), row(half),
                  pl.BlockSpec((2 * half, D_MODEL), lambda i: (0, 0), pipeline_mode=pl.Buffered(1)),
                  row(D_MODEL), full((1, D_MODEL)), full((1, D_MODEL)),
                  full((N_EXPERTS, D_MODEL)), full((N_EXPERTS, D_MODEL)), full((N_EXPERTS, 1))],
        out_specs=[row(D_MODEL),
                   pl.BlockSpec((SUBLANES, tm), lambda i: (0, i)), pl.BlockSpec((SUBLANES, tm), lambda i: (0, i))],
        out_shape=[jax.ShapeDtypeStruct((t, D_MODEL), F32),
                   jax.ShapeDtypeStruct((SUBLANES, t), jnp.int32), jax.ShapeDtypeStruct((SUBLANES, t), F32)],
        scratch_shapes=[pltpu.VMEM((2 * half, D_MODEL), BF16)],
        compiler_params=_params(("arbitrary",)),
        name="outproj_ln_router",
    )(mix_a, mix_b, w_out, x_res, ln_g, ln_b, rwt_hi, rwt_lo, rb)


def _rank_kernel(e_ref, rank_ref, cnt_ref, base_ref, *, tn):
    @pl.when(pl.program_id(0) == 0)
    def _():
        base_ref[...] = jnp.zeros(base_ref.shape, F32)

    e = e_ref[...]
    rows = lax.broadcasted_iota(jnp.int32, (N_EXPERTS, tn), 0)
    oh1 = rows == e[0:1, :]
    oh2 = rows == e[1:2, :]
    oh = jnp.where(jnp.logical_or(oh1, oh2), 1.0, 0.0)
    r = lax.broadcasted_iota(jnp.int32, (tn, tn), 0)
    c = lax.broadcasted_iota(jnp.int32, (tn, tn), 1)
    earlier = jnp.where(r < c, 1.0, 0.0).astype(BF16)
    base = base_ref[...]
    before = _dot(oh.astype(BF16), earlier) + jnp.tile(base, (1, tn // LANES))
    r1 = jnp.sum(jnp.where(oh1, before, 0.0), axis=0, keepdims=True)
    r2 = jnp.sum(jnp.where(oh2, before, 0.0), axis=0, keepdims=True)
    rank_ref[...] = jnp.concatenate([r1, r2, jnp.zeros((SUBLANES - 2, tn), F32)], axis=0).astype(jnp.int32)
    base = base + jnp.sum(oh, axis=1, keepdims=True)
    base_ref[...] = base
    cnt_ref[...] = base.astype(jnp.int32)


def _rank(eidx, tn):
    t = eidx.shape[1]
    tn = min(tn, t)
    return pl.pallas_call(
        functools.partial(_rank_kernel, tn=tn),
        grid=(t // tn,),
        in_specs=[pl.BlockSpec((SUBLANES, tn), lambda i: (0, i))],
        out_specs=[pl.BlockSpec((SUBLANES, tn), lambda i: (0, i)),
                   pl.BlockSpec((N_EXPERTS, LANES), lambda i: (0, 0))],
        out_shape=[jax.ShapeDtypeStruct((SUBLANES, t), jnp.int32),
                   jax.ShapeDtypeStruct((N_EXPERTS, LANES), jnp.int32)],
        scratch_shapes=[pltpu.VMEM((N_EXPERTS, LANES), F32)],
        compiler_params=_params(("arbitrary",)),
        name="moe_rank",
    )(eidx)


def _moe_kernel(te_ref, nu_ref, xs_ref, wg_ref, wu_ref, wd_ref, *rest, tile0):
    y_ref, wgu_s, wd_s = rest[-3:]
    i = pl.program_id(0)
    tile = tile0 + i
    used = tile < nu_ref[0]
    fresh = jnp.logical_or(i == 0, te_ref[tile] != te_ref[jnp.maximum(tile - 1, 0)])

    @pl.when(jnp.logical_and(used, fresh))
    def _():
        wgu_s[:, :D_EXPERT] = wg_ref[...].astype(BF16)
        wgu_s[:, D_EXPERT:] = wu_ref[...].astype(BF16)
        wd_s[...] = wd_ref[...].astype(BF16)

    @pl.when(used)
    def _():
        hgu = _dot(xs_ref[...].astype(BF16), wgu_s[...])
        hg = hgu[:, :D_EXPERT]
        hu = hgu[:, D_EXPERT:]
        h = (hg / (1.0 + jnp.exp(-hg))) * hu
        y_ref[...] = _dot(h.astype(BF16), wd_s[...])

    @pl.when(jnp.logical_not(used))
    def _():
        y_ref[...] = jnp.zeros(y_ref.shape, F32)


def _moe_part(tile_expert, n_used, xs_part, w_gate, w_up, w_down, ys_prev, tile0, n_tiles, tm):
    nt = xs_part.shape[0] // tm
    in_specs = [
        pl.BlockSpec((tm, D_MODEL), lambda i, te, nu: (i, 0)),
        pl.BlockSpec((None, D_MODEL, D_EXPERT), lambda i, te, nu: (te[tile0 + i], 0, 0)),
        pl.BlockSpec((None, D_MODEL, D_EXPERT), lambda i, te, nu: (te[tile0 + i], 0, 0)),
        pl.BlockSpec((None, D_EXPERT, D_MODEL), lambda i, te, nu: (te[tile0 + i], 0, 0)),
    ]
    args = [tile_expert, n_used, xs_part, w_gate, w_up, w_down]
    aliases = {}
    if ys_prev is not None:
        in_specs.append(pl.BlockSpec(memory_space=pl.ANY))
        args.append(ys_prev)
        aliases = {len(args) - 1: 0}
    grid_spec = pltpu.PrefetchScalarGridSpec(
        num_scalar_prefetch=2,
        grid=(nt,),
        in_specs=in_specs,
        out_specs=pl.BlockSpec((tm, D_MODEL), lambda i, te, nu: (tile0 + i, 0)),
        scratch_shapes=[pltpu.VMEM((D_MODEL, 2 * D_EXPERT), BF16), pltpu.VMEM((D_EXPERT, D_MODEL), BF16)],
    )
    return pl.pallas_call(
        functools.partial(_moe_kernel, tile0=tile0),
        grid_spec=grid_spec,
        out_shape=jax.ShapeDtypeStruct((n_tiles * tm, D_MODEL), F32),
        input_output_aliases=aliases,
        compiler_params=_params(("arbitrary",)),
        name="moe",
    )(*args)


def _moe_ln_kernel(x_ref, y1_ref, y2_ref, w_ref, g_ref, b_ref, *rest):
    o_ref = rest[-1]
    w = w_ref[...]
    v = DEEPNORM_ALPHA * x_ref[...] + (w[:, 0:1] * y1_ref[...] + w[:, 1:2] * y2_ref[...])
    o_ref[...] = _layer_norm(v, g_ref[...], b_ref[...])


def _moe_ln_part(x, y1, y2, w_cols, ln_g, ln_b, out_prev, row0, tm):
    t = x.shape[0]
    b0 = row0 // tm
    here = pl.BlockSpec((tm, D_MODEL), lambda i: (i, 0))
    there = pl.BlockSpec((tm, D_MODEL), lambda i: (b0 + i, 0))
    vec = pl.BlockSpec((1, D_MODEL), lambda i: (0, 0))
    in_specs = [there, here, here, pl.BlockSpec((tm, SUBLANES), lambda i: (b0 + i, 0)), vec, vec]
    args = [x, y1, y2, w_cols, ln_g, ln_b]
    aliases = {}
    if out_prev is not None:
        in_specs.append(pl.BlockSpec(memory_space=pl.ANY))
        args.append(out_prev)
        aliases = {len(args) - 1: 0}
    return pl.pallas_call(
        _moe_ln_kernel,
        grid=(y1.shape[0] // tm,),
        in_specs=in_specs,
        out_specs=there,
        out_shape=jax.ShapeDtypeStruct((t, D_MODEL), F32),
        input_output_aliases=aliases,
        compiler_params=_params(("parallel",)),
        name="moe_ln",
    )(*args)


def _moe_block(x, eidx, ew, w_gate, w_up, w_down, ln_g, ln_b, tm):
    t = x.shape[0]
    tm = min(tm, t)
    rank, cnt = _rank(eidx, 512)
    counts = cnt[:, 0]
    padded = ((counts + tm - 1) // tm) * tm
    ends = jnp.cumsum(padded)
    starts = ends - padded
    e2 = eidx[:2]
    start_of = jnp.zeros_like(e2)
    for k in range(N_EXPERTS):
        start_of = jnp.where(e2 == k, starts[k], start_of)
    dest = start_of + rank[:2]
    t_pad = 2 * t + N_EXPERTS * tm
    n_tiles = t_pad // tm
    tok = jnp.tile(jnp.arange(t, dtype=jnp.int32), 2)
    src_tok = (jnp.arange(t_pad, dtype=jnp.int32) % t).at[dest.reshape(2 * t)].set(
        tok, mode="promise_in_bounds", unique_indices=True)
    tile_start = jnp.arange(n_tiles, dtype=jnp.int32) * tm
    tile_expert = jnp.minimum(jnp.sum((ends[None, :] <= tile_start[:, None]).astype(jnp.int32), axis=1),
                              N_EXPERTS - 1)
    n_used = (ends[-1] // tm).astype(jnp.int32).reshape(1)
    ys = None
    tiles_per = n_tiles // MOE_PARTS
    for p in range(MOE_PARTS):
        lo = p * tiles_per * tm
        xs_p = x.at[src_tok[lo:lo + tiles_per * tm]].get(mode="promise_in_bounds")
        ys = _moe_part(tile_expert, n_used, xs_p, w_gate, w_up, w_down, ys, p * tiles_per, n_tiles, tm)
    out = None
    rows_per = t // MOE_PARTS
    w_cols = ew.T
    for p in range(MOE_PARTS):
        lo = p * rows_per
        y1 = ys.at[dest[0, lo:lo + rows_per]].get(mode="promise_in_bounds")
        y2 = ys.at[dest[1, lo:lo + rows_per]].get(mode="promise_in_bounds")
        out = _moe_ln_part(x, y1, y2, w_cols, ln_g, ln_b, out, lo, tm)
    return out


def kernel(x, router_w, router_b, l0_w_in, l0_diff_lam_q1, l0_diff_lam_k1, l0_diff_lam_q2, l0_diff_lam_k2, l0_diff_subln_g, l0_gla_w_gate, l0_gla_b_gate, l0_gla_norm_g, l0_w_out, l0_ln1_g, l0_ln1_b, l0_moe_w_gate, l0_moe_w_up, l0_moe_w_down, l0_ln2_g, l0_ln2_b, l1_w_in, l1_s5_a_re, l1_s5_a_im, l1_s5_log_step, l1_s5_b_re, l1_s5_b_im, l1_s5_c_re, l1_s5_c_im, l1_s5_d, l1_s5_w_glu, l1_s5_b_glu, l1_w_out, l1_ln1_g, l1_ln1_b, l1_moe_w_gate, l1_moe_w_up, l1_moe_w_down, l1_ln2_g, l1_ln2_b):
    bsz, t, d = x.shape
    assert bsz == 1 and d == D_MODEL
    x0 = x.reshape(t, d)
    vec = lambda p: p.reshape(1, -1).astype(F32)
    rwt = router_w.T.astype(F32)
    rb = router_b.reshape(N_EXPERTS, 1).astype(F32)

    lambda_init = 0.8 - 0.6 * math.exp(-0.3 * 0)
    c = [0, 1024, 2048, 3072, 3584, 4096, 5120, 5136, 6160]
    q_scale = math.log2(math.e) * DIFF_HEAD_DIM ** -0.5
    w0 = jnp.concatenate([l0_w_in[:, :c[1]] * q_scale, l0_w_in[:, c[1]:c[6]], l0_w_in[:, c[7]:c[8]],
                          l0_w_in[:, c[6]:c[7]], jnp.zeros((d, L0_GB_PAD - GLA_GATE_RANK), l0_w_in.dtype)],
                         axis=1).astype(BF16)
    h0 = _matmul(x0, w0, 1024, L0_IN_TN, BF16)
    lam_vec = jnp.stack([l0_diff_lam_q1, l0_diff_lam_k1, l0_diff_lam_q2, l0_diff_lam_k2]).astype(F32)
    o_a = _diff_attention(h0, lam_vec, vec(l0_diff_subln_g), lambda_init, 1024)
    wg_pad = jnp.concatenate([l0_gla_w_gate, jnp.zeros((LANES - GLA_GATE_RANK, GLA_HEADS * GLA_DK), F32)],
                             axis=0).astype(BF16)
    o_b = _gla(h0, wg_pad, vec(l0_gla_b_gate), vec(l0_gla_norm_g), 256)
    x1, eidx, ew = _outproj_ln_router(o_a, o_b, l0_w_out, x0, vec(l0_ln1_g), vec(l0_ln1_b),
                                      rwt, rb, 512, 128)
    x2 = _moe_block(x1, eidx, ew, l0_moe_w_gate, l0_moe_w_up, l0_moe_w_down, vec(l0_ln2_g), vec(l0_ln2_b), 256)

    h1 = _matmul(x2, l1_w_in.astype(BF16), 1024, 1024, BF16)
    o_c = _stick_breaking(h1, 256)
    lbr, lbi, bbr, bbi = _s5_prep(l1_s5_a_re, l1_s5_a_im, l1_s5_log_step, l1_s5_b_re, l1_s5_b_im)
    bd, cd, pw8 = _s5_tile_params(lbr[::S5_GROUP], lbi[::S5_GROUP], bbr.reshape(S5_GROUPS, S5_GROUP, S5_STATE),
                                  bbi.reshape(S5_GROUPS, S5_GROUP, S5_STATE),
                                  l1_s5_c_re.astype(F32), l1_s5_c_im.astype(F32))
    y_s5 = _s5_core(h1, bd, cd, pw8, 256)
    o_d = _s5_glu(y_s5, h1, vec(l1_s5_d), l1_s5_w_glu.astype(BF16), vec(l1_s5_b_glu), 512)
    x3, eidx1, ew1 = _outproj_ln_router(o_c, o_d, l1_w_out, x2, vec(l1_ln1_g), vec(l1_ln1_b),
                                        rwt, rb, 512, 128)
    x4 = _moe_block(x3, eidx1, ew1, l1_moe_w_gate, l1_moe_w_up, l1_moe_w_down, vec(l1_ln2_g), vec(l1_ln2_b), 256)
    return x4.reshape(bsz, t, d)
```

```python
import functools
import math

import jax
import jax.numpy as jnp
from jax import lax
from jax.experimental import pallas as pl
from jax.experimental.pallas import tpu as pltpu

F32 = jnp.float32
BF16 = jnp.bfloat16

D_MODEL = 2048
DEPTH = 2
DIFF_HEADS = 8
DIFF_HEAD_DIM = 64
DIFF_V_DIM = 128
GLA_HEADS = 4
GLA_DK = 128
GLA_DV = 256
GLA_GATE_RANK = 16
GLA_GATE_TEMP = 16.0
GLA_CHUNK = 64
SB_HEADS = 8
SB_HEAD_DIM = 128
S5_CHANNELS = 1024
S5_GROUP = 16
S5_GROUPS = 64
S5_STATE = 64
N_EXPERTS = 16
N_GROUPS = 4
EXPERTS_PER_GROUP = 4
D_EXPERT = 640
DEEPNORM_ALPHA = (2.0 * DEPTH) ** 0.25
LN_EPS = 1e-5

LANES = 128
SUBLANES = 8
VMEM_LIMIT = 56 * 1024 * 1024
SB_LOG_FLOOR = 120.0
SB_HEADS_PER_STEP = 4
MOE_PARTS = 2

MXU_N = 256
L0_GB_PAD = MXU_N
L0_COL = {"qa": 0, "ka": 1024, "va": 2048, "qb": 3072, "kb": 3584, "vb": 4096, "rb": 5120, "gb": 6144}
L0_IN_TN = 5 * MXU_N

S5_KT = 4
S5_KT_CH = S5_CHANNELS // S5_KT
S5_KT_STATES = (S5_GROUPS // S5_KT) * S5_STATE
S5_SCAN_W = 512
S5_L = 8


def _params(sem, flags=None):
    return pltpu.CompilerParams(dimension_semantics=sem, vmem_limit_bytes=VMEM_LIMIT, flags=flags)


def _nt_dot(a, b):
    return lax.dot_general(a, b, (((1,), (1,)), ((), ())), preferred_element_type=F32)


def _dot(a, b):
    return jnp.dot(a, b, preferred_element_type=F32)


def _split_dot_left(m, x):
    hi = x.astype(BF16)
    lo = (x - hi.astype(F32)).astype(BF16)
    return _dot(m, hi) + _dot(m, lo)


def _layer_norm(v, g, b):
    mu = jnp.mean(v, axis=-1, keepdims=True)
    d = v - mu
    var = jnp.mean(d * d, axis=-1, keepdims=True)
    return d * lax.rsqrt(var + LN_EPS) * g + b


def _mm_kernel(a_ref, b_ref, o_ref, a_bf):
    @pl.when(pl.program_id(1) == 0)
    def _():
        a_bf[...] = a_ref[...].astype(BF16)

    o_ref[...] = _dot(a_bf[...], b_ref[...]).astype(o_ref.dtype)


def _matmul(a, b, tm, tn, out_dtype):
    m, k = a.shape
    n = b.shape[1]
    tm = min(tm, m)
    return pl.pallas_call(
        _mm_kernel,
        grid=(m // tm, n // tn),
        in_specs=[pl.BlockSpec((tm, k), lambda i, j: (i, 0)),
                  pl.BlockSpec((k, tn), lambda i, j: (0, j))],
        out_specs=pl.BlockSpec((tm, tn), lambda i, j: (i, j)),
        out_shape=jax.ShapeDtypeStruct((m, n), out_dtype),
        scratch_shapes=[pltpu.VMEM((tm, k), BF16)],
        compiler_params=_params(("parallel", "arbitrary")),
        name="in_proj",
    )(a, b)


def _diff_attn_kernel(lam_ref, g_ref, q_ref, k_ref, v_ref, o_ref, vx_ref, m_ref, acc_ref, sa_ref, sb_ref, *, tq,
                      lambda_init):
    i = pl.program_id(1)
    dv = DIFF_V_DIM

    @pl.when(i == 0)
    def _():
        vx_ref[:, :dv] = v_ref[...]
        vx_ref[:, dv:] = jnp.ones((vx_ref.shape[0], dv), BF16)

    m_ref[...] = jnp.full(m_ref.shape, -jnp.inf, F32)
    acc_ref[...] = jnp.zeros(acc_ref.shape, F32)
    q = q_ref[...]
    lane = lax.broadcasted_iota(jnp.int32, q.shape, 1)
    zero = jnp.zeros_like(q)
    halves = (jnp.where(lane < DIFF_HEAD_DIM, q, zero), jnp.where(lane >= DIFF_HEAD_DIM, q, zero))

    def scores(j, mi):
        off = pl.multiple_of(j * tq, tq)
        return _nt_dot(halves[mi], k_ref[pl.ds(off, tq), :])

    def softmax_pv(j, get_s, masked):
        off = pl.multiple_of(j * tq, tq)
        vx = vx_ref[pl.ds(off, tq), :]
        if masked:
            row = lax.broadcasted_iota(jnp.int32, (tq, tq), 0)
            col = lax.broadcasted_iota(jnp.int32, (tq, tq), 1)
            causal = col <= row
        for mi in range(2):
            s = get_s(mi)
            if masked:
                s = jnp.where(causal, s, -jnp.inf)
            m_prev = m_ref[mi]
            m_new = jnp.maximum(m_prev, jnp.max(s, axis=1, keepdims=True))
            p = jnp.exp2(s - jnp.tile(m_new, (1, tq // LANES)))
            alpha = jnp.exp2(m_prev - m_new)
            acc_ref[mi] = jnp.tile(alpha, (1, 2)) * acc_ref[mi] + _dot(p.astype(BF16), vx)
            m_ref[mi] = m_new

    def scores_to(j, s_ref):
        for mi in range(2):
            s_ref[mi] = scores(j, mi)

    scores_to(0, sa_ref)

    def pair(t, carry):
        j = 2 * t
        scores_to(j + 1, sb_ref)
        softmax_pv(j, lambda mi: sa_ref[mi], False)
        scores_to(j + 2, sa_ref)
        softmax_pv(j + 1, lambda mi: sb_ref[mi], False)
        return carry

    lax.fori_loop(0, i // 2, pair, 0)

    @pl.when(i % 2 == 1)
    def _():
        softmax_pv(i - 1, lambda mi: sa_ref[mi], False)
        scores_to(i, sa_ref)

    softmax_pv(i, lambda mi: sa_ref[mi], True)

    lv = lam_ref[...]
    lam = (jnp.exp(jnp.sum(lv[0:1] * lv[1:2], axis=1, keepdims=True))
           - jnp.exp(jnp.sum(lv[2:3] * lv[3:4], axis=1, keepdims=True)) + lambda_init)
    a0 = acc_ref[0]
    a1 = acc_ref[1]
    o = a0[:, :dv] / a0[:, dv:] - lam * (a1[:, :dv] / a1[:, dv:])
    ms = jnp.mean(o * o, axis=-1, keepdims=True)
    o = o * lax.rsqrt(ms + LN_EPS) * g_ref[...] * (1.0 - lambda_init)
    o_ref[...] = o.astype(o_ref.dtype)


def _diff_attention(h0, lam_vec, subln_g, lambda_init, tq):
    t = h0.shape[0]
    tq = min(tq, t)
    nq = t // tq
    kern = functools.partial(_diff_attn_kernel, tq=tq, lambda_init=lambda_init)
    return pl.pallas_call(
        kern,
        grid=(DIFF_HEADS, nq),
        in_specs=[
            pl.BlockSpec((4, DIFF_HEAD_DIM), lambda h, i: (0, 0)),
            pl.BlockSpec((1, DIFF_V_DIM), lambda h, i: (0, 0)),
            pl.BlockSpec((tq, LANES), lambda h, i: (i, h)),
            pl.BlockSpec((t, LANES), lambda h, i: (0, DIFF_HEADS + h)),
            pl.BlockSpec((t, LANES), lambda h, i: (0, 2 * DIFF_HEADS + h)),
        ],
        out_specs=pl.BlockSpec((tq, LANES), lambda h, i: (i, h)),
        out_shape=jax.ShapeDtypeStruct((t, DIFF_HEADS * DIFF_V_DIM), BF16),
        scratch_shapes=[pltpu.VMEM((t, 2 * DIFF_V_DIM), BF16), pltpu.VMEM((2, tq, LANES), F32),
                        pltpu.VMEM((2, tq, 2 * DIFF_V_DIM), F32),
                        pltpu.VMEM((2, tq, tq), F32), pltpu.VMEM((2, tq, tq), F32)],
        compiler_params=_params(("arbitrary", "arbitrary")),
        name="diff_attn",
    )(lam_vec, subln_g, h0, h0, h0)


def _gla_kernel(q_ref, k_ref, v_ref, r_ref, gb_ref, wg_ref, bg_ref, ng_ref, o_ref, st_ref, *, tg):
    @pl.when(pl.program_id(0) == 0)
    def _():
        st_ref[...] = jnp.zeros(st_ref.shape, F32)

    c = GLA_CHUNK
    gate = _dot(gb_ref[...], wg_ref[...]) + bg_ref[...]
    log_a = -(jnp.maximum(-gate, 0.0) + jnp.log(1.0 + jnp.exp(-jnp.abs(gate)))) / GLA_GATE_TEMP
    row = lax.broadcasted_iota(jnp.int32, (c, c), 0)
    col = lax.broadcasted_iota(jnp.int32, (c, c), 1)
    tril = row >= col
    tri = jnp.where(tril, 1.0, 0.0).astype(BF16)
    for h in range(GLA_HEADS):
        ksl = slice(h * GLA_DK, (h + 1) * GLA_DK)
        vsl = slice(h * GLA_DV, (h + 1) * GLA_DV)
        for ci in range(tg // c):
            rs = slice(ci * c, (ci + 1) * c)
            la = log_a[rs, ksl]
            b = _split_dot_left(tri, la)
            b_last = b[c - 1:c, :]
            qf = q_ref[rs, ksl].astype(F32) * (GLA_DK ** -0.5)
            kf = k_ref[rs, ksl].astype(F32)
            v = v_ref[rs, vsl]
            q_dec = (qf * jnp.exp(b)).astype(BF16)
            k_dec = (kf * jnp.exp(-b)).astype(BF16)
            k_end = (kf * jnp.exp(b_last - b)).astype(BF16)
            scores = jnp.where(tril, _nt_dot(q_dec, k_dec), 0.0)
            st = st_ref[h]
            o = _dot(scores.astype(BF16), v) + _nt_dot(q_dec, st.astype(BF16))
            kv_t = lax.dot_general(v, k_end, (((0,), (0,)), ((), ())), preferred_element_type=F32)
            st_ref[h] = jnp.exp(b_last) * st + kv_t
            ms = jnp.mean(o * o, axis=-1, keepdims=True)
            o = o * lax.rsqrt(ms + LN_EPS) * ng_ref[...]
            r = r_ref[rs, vsl].astype(F32)
            o_ref[rs, vsl] = (o * (r / (1.0 + jnp.exp(-r)))).astype(o_ref.dtype)


def _gla(h0, wg_pad, b_gate, norm_g, tg):
    t = h0.shape[0]
    tg = min(tg, t)
    qk = GLA_HEADS * GLA_DK
    vw = GLA_HEADS * GLA_DV
    return pl.pallas_call(
        functools.partial(_gla_kernel, tg=tg),
        grid=(t // tg,),
        in_specs=[
            pl.BlockSpec((tg, qk), lambda i: (i, L0_COL["qb"] // qk)),
            pl.BlockSpec((tg, qk), lambda i: (i, L0_COL["kb"] // qk)),
            pl.BlockSpec((tg, vw), lambda i: (i, L0_COL["vb"] // vw)),
            pl.BlockSpec((tg, vw), lambda i: (i, L0_COL["rb"] // vw)),
            pl.BlockSpec((tg, LANES), lambda i: (i, L0_COL["gb"] // LANES)),
            pl.BlockSpec((LANES, qk), lambda i: (0, 0)),
            pl.BlockSpec((1, qk), lambda i: (0, 0)),
            pl.BlockSpec((1, GLA_DV), lambda i: (0, 0)),
        ],
        out_specs=pl.BlockSpec((tg, vw), lambda i: (i, 0)),
        out_shape=jax.ShapeDtypeStruct((t, vw), BF16),
        scratch_shapes=[pltpu.VMEM((GLA_HEADS, GLA_DV, GLA_DK), F32)],
        compiler_params=_params(("arbitrary",)),
        name="gla",
    )(h0, h0, h0, h0, h0, wg_pad, b_gate, norm_g)


def _sb_kernel(q_ref, k_ref, v_ref, o_ref, acc_ref, c_ref, *, tq):
    i = pl.program_id(1)
    hd = SB_HEAD_DIM
    acc_ref[...] = jnp.zeros(acc_ref.shape, F32)
    c_ref[...] = jnp.zeros(c_ref.shape, F32)
    row = lax.broadcasted_iota(jnp.int32, (tq, tq), 0)
    col = lax.broadcasted_iota(jnp.int32, (tq, tq), 1)
    strict = col < row
    tri = jnp.where(row > col, 1.0, 0.0).astype(BF16)

    def block(jb, masked):
        off = pl.multiple_of(jb * tq, tq)
        cmax = None
        for hh in range(SB_HEADS_PER_STEP):
            cols = slice(hh * hd, (hh + 1) * hd)
            k = k_ref[pl.ds(off, tq), cols]
            v = v_ref[pl.ds(off, tq), cols]
            z = _nt_dot(q_ref[:, cols], k)
            lsn = -(jnp.maximum(z, 0.0) + jnp.log(1.0 + jnp.exp(-jnp.abs(z))))
            lm = jnp.where(strict, lsn, 0.0) if masked else lsn
            excl = _dot(lm.astype(BF16), tri) + c_ref[hh]
            w = jnp.exp(z + lsn + excl)
            if masked:
                w = jnp.where(strict, w, 0.0)
            acc_ref[:, cols] += _dot(w.astype(BF16), v)
            c_new = c_ref[hh] + jnp.sum(lm, axis=1, keepdims=True)
            c_ref[hh] = c_new
            hmax = jnp.max(c_new)
            cmax = hmax if cmax is None else jnp.maximum(cmax, hmax)
        return cmax

    cmax = block(i, True)

    def cond(carry):
        jb, cm = carry
        return jnp.logical_and(jb >= 0, cm > -SB_LOG_FLOOR)

    def body(carry):
        jb, _ = carry
        return jb - 1, block(jb, False)

    lax.while_loop(cond, body, (i - 1, cmax))
    o_ref[...] = acc_ref[...].astype(o_ref.dtype)


def _stick_breaking(h1, tq):
    t = h1.shape[0]
    tq = min(tq, t)
    ng = SB_HEADS // SB_HEADS_PER_STEP
    wd = SB_HEADS_PER_STEP * SB_HEAD_DIM
    return pl.pallas_call(
        functools.partial(_sb_kernel, tq=tq),
        grid=(ng, t // tq),
        in_specs=[
            pl.BlockSpec((tq, wd), lambda h, i: (i, h)),
            pl.BlockSpec((t, wd), lambda h, i: (0, ng + h), pipeline_mode=pl.Buffered(1)),
            pl.BlockSpec((t, wd), lambda h, i: (0, 2 * ng + h), pipeline_mode=pl.Buffered(1)),
        ],
        out_specs=pl.BlockSpec((tq, wd), lambda h, i: (i, h)),
        out_shape=jax.ShapeDtypeStruct((t, SB_HEADS * SB_HEAD_DIM), BF16),
        scratch_shapes=[pltpu.VMEM((tq, wd), F32), pltpu.VMEM((SB_HEADS_PER_STEP, tq, 1), F32)],
        compiler_params=_params(("parallel", "arbitrary")),
        name="stick_breaking",
    )(h1, h1, h1)


def _s5_prep_kernel(are_ref, aim_ref, ls_ref, bre_ref, bim_ref, lbr_ref, lbi_ref, bbr_ref, bbi_ref):
    step = jnp.exp(ls_ref[...])
    lam_re = are_ref[...]
    lam_im = aim_ref[...]
    mag = jnp.exp(step * lam_re)
    lb_re = mag * jnp.cos(step * lam_im)
    lb_im = mag * jnp.sin(step * lam_im)
    den = lam_re * lam_re + lam_im * lam_im
    n_re = lb_re - 1.0
    f_re = (n_re * lam_re + lb_im * lam_im) / den
    f_im = (lb_im * lam_re - n_re * lam_im) / den
    lbr_ref[...] = lb_re
    lbi_ref[...] = lb_im
    bbr_ref[...] = f_re * bre_ref[...] - f_im * bim_ref[...]
    bbi_ref[...] = f_re * bim_ref[...] + f_im * bre_ref[...]


def _s5_prep(a_re, a_im, log_step, b_re, b_im):
    rows = S5_GROUPS * S5_GROUP
    rep = lambda p: jnp.repeat(p, S5_GROUP, axis=0)
    to_rows = lambda b: jnp.transpose(b, (0, 2, 1)).reshape(rows, S5_STATE)
    spec = pl.BlockSpec((rows, S5_STATE), lambda: (0, 0))
    shp = jax.ShapeDtypeStruct((rows, S5_STATE), F32)
    return pl.pallas_call(
        _s5_prep_kernel,
        in_specs=[spec, spec, pl.BlockSpec((rows, 1), lambda: (0, 0)), spec, spec],
        out_specs=[spec] * 4,
        out_shape=[shp] * 4,
        name="s5_prep",
    )(rep(a_re), rep(a_im), rep(log_step[:, None]), to_rows(b_re), to_rows(b_im))


def _cmul(ar, ai, br, bi):
    return ar * br - ai * bi, ar * bi + ai * br


def _block_diag_tiles(m, rows_per_group, cols_per_group):
    gpt = S5_GROUPS // S5_KT
    m = m.reshape(S5_KT, gpt, rows_per_group, cols_per_group)
    eye = jnp.eye(gpt, dtype=m.dtype)
    bd = jnp.einsum("kgrc,gh->kgrhc", m, eye)
    return bd.reshape(S5_KT, gpt * rows_per_group, gpt * cols_per_group)


def _s5_tile_params(lb_re, lb_im, bb_re, bb_im, c_re, c_im):
    bd = jnp.concatenate([_block_diag_tiles(bb_re, S5_GROUP, S5_STATE),
                          _block_diag_tiles(bb_im, S5_GROUP, S5_STATE)], axis=2)
    cd = jnp.concatenate([_block_diag_tiles(jnp.transpose(c_re, (0, 2, 1)), S5_STATE, S5_GROUP),
                          _block_diag_tiles(-jnp.transpose(c_im, (0, 2, 1)), S5_STATE, S5_GROUP)], axis=1)
    pr = [jnp.ones_like(lb_re)]
    pi = [jnp.zeros_like(lb_re)]
    for _ in range(S5_L):
        nr, ni = _cmul(pr[-1], pi[-1], lb_re, lb_im)
        pr.append(nr)
        pi.append(ni)

    def table(p):
        rows = jnp.transpose(jnp.stack(p).reshape(S5_L + 1, S5_KT, S5_KT_STATES), (1, 0, 2))
        return jnp.pad(rows, ((0, 0), (0, LANES - (S5_L + 1)), (0, 0)))

    return bd, cd, jnp.stack([table(pr), table(pi)], axis=1)


def _s5_core_kernel(u_ref, bd_ref, cd_ref, pw8_ref, y_ref, hs_ref, hp_ref, pw_ref, st_ref, uf_ref,
                    mt_s, bbs_s, ccs_s, *, tcr):
    ns = S5_KT_STATES
    pad = SUBLANES
    cw = S5_KT_CH

    @pl.when(pl.program_id(1) == 0)
    def _():
        b_re = bd_ref[:, :ns]
        b_im = bd_ref[:, ns:]
        cd_hi = cd_ref[...].astype(BF16)
        cd_lo = (cd_ref[...] - cd_hi.astype(F32)).astype(BF16)
        for k in range(S5_L):
            a = pw8_ref[0, k:k + 1, :]
            b = pw8_ref[1, k:k + 1, :]
            dk = jnp.concatenate([b_re * a - b_im * b, b_re * b + b_im * a], axis=1)
            s = S5_L - 1 - k
            bbs_s[s * cw:(s + 1) * cw, :] = dk.astype(BF16)
            dk_hi = dk.astype(BF16)
            dk_lo = (dk - dk_hi.astype(F32)).astype(BF16)
            kk = (_dot(dk_hi, cd_hi) + (_dot(dk_lo, cd_hi) + _dot(dk_hi, cd_lo))).astype(BF16)
            for s0 in range(S5_L - k):
                mt_s[s0 * cw:(s0 + 1) * cw, (s0 + k) * cw:(s0 + k + 1) * cw] = kk
        pw_t = (jnp.transpose(pw8_ref[0]), jnp.transpose(pw8_ref[1]))
        c_re = cd_ref[:ns, :]
        c_im = cd_ref[ns:, :]
        for i in range(S5_L):
            a = pw_t[0][:, i + 1:i + 2]
            b = pw_t[1][:, i + 1:i + 2]
            ccs_s[:ns, i * cw:(i + 1) * cw] = (c_re * a + c_im * b).astype(BF16)
            ccs_s[ns:, i * cw:(i + 1) * cw] = (c_im * a - c_re * b).astype(BF16)

        hp_ref[...] = jnp.zeros(hp_ref.shape, F32)
        hs_ref[0:pad, :] = jnp.zeros((pad, 2 * ns), F32)
        l1r = jnp.broadcast_to(pw8_ref[0, S5_L:S5_L + 1, :], (SUBLANES, ns))
        l1i = jnp.broadcast_to(pw8_ref[1, S5_L:S5_L + 1, :], (SUBLANES, ns))
        l2r, l2i = _cmul(l1r, l1i, l1r, l1i)
        l4r, l4i = _cmul(l2r, l2i, l2r, l2i)
        l8r, l8i = _cmul(l4r, l4i, l4r, l4i)
        ridx = lax.broadcasted_iota(jnp.int32, l1r.shape, 0)
        n = ridx + 1
        pr = jnp.ones_like(l1r)
        pi = jnp.zeros_like(l1r)
        for bit, (lr, li) in ((1, (l1r, l1i)), (2, (l2r, l2i)), (4, (l4r, l4i)), (8, (l8r, l8i))):
            nr, ni = _cmul(pr, pi, lr, li)
            take = (n & bit) != 0
            pr = jnp.where(take, nr, pr)
            pi = jnp.where(take, ni, pi)
        pw_ref[0] = pr
        pw_ref[1] = pi
        for s, (lr, li) in enumerate(((l1r, l1i), (l2r, l2i), (l4r, l4i))):
            keep = ridx >= (1 << s)
            st_ref[2 * s] = jnp.where(keep, lr, 0.0)
            st_ref[2 * s + 1] = jnp.where(keep, li, 0.0)

    nl = S5_KT_CH // LANES
    for h in range(nl):
        uf_ref[h] = u_ref[:, h * LANES:(h + 1) * LANES].astype(F32)
    uc = jnp.concatenate([uf_ref[h, pl.ds(s, tcr, stride=S5_L), :].astype(BF16)
                          for s in range(S5_L) for h in range(nl)], axis=1)
    hs_ref[pad:pad + tcr, :] = _dot(uc, bbs_s[...])
    w = S5_SCAN_W
    for cc in range(ns // w):
        re0 = cc * w
        im0 = ns + cc * w

        def body(r, carry, re0=re0, im0=im0):
            hpr, hpi = carry
            rows = pl.ds(pl.multiple_of(pad + r * SUBLANES, SUBLANES), SUBLANES)
            xr = hs_ref[rows, re0:re0 + w]
            xi = hs_ref[rows, im0:im0 + w]
            for s in range(3):
                sr = pltpu.roll(xr, 1 << s, 0)
                si = pltpu.roll(xi, 1 << s, 0)
                ar = st_ref[2 * s, :, re0:re0 + w]
                ai = st_ref[2 * s + 1, :, re0:re0 + w]
                xr, xi = xr + ar * sr - ai * si, xi + ar * si + ai * sr
            pr = pw_ref[0, :, re0:re0 + w]
            pi = pw_ref[1, :, re0:re0 + w]
            xr, xi = xr + pr * hpr - pi * hpi, xi + pr * hpi + pi * hpr
            hs_ref[rows, re0:re0 + w] = xr
            hs_ref[rows, im0:im0 + w] = xi
            return (jnp.broadcast_to(xr[SUBLANES - 1:SUBLANES, :], (SUBLANES, w)),
                    jnp.broadcast_to(xi[SUBLANES - 1:SUBLANES, :], (SUBLANES, w)))

        hpr, hpi = lax.fori_loop(0, tcr // SUBLANES, body, (hp_ref[:, re0:re0 + w], hp_ref[:, im0:im0 + w]))
        hp_ref[:, re0:re0 + w] = hpr
        hp_ref[:, im0:im0 + w] = hpi

    h_prev = hs_ref[pad - 1:pad - 1 + tcr, :]
    y_in = _dot(h_prev.astype(BF16), ccs_s[...])
    for i in range(S5_L):
        k_hi = (i + 1) * cw
        y_i = y_in[:, i * cw:k_hi] + _dot(uc[:, :k_hi], mt_s[:k_hi, i * cw:k_hi])
        for h in range(nl):
            uf_ref[h, pl.ds(i, tcr, stride=S5_L), :] = y_i[:, h * LANES:(h + 1) * LANES]
    for h in range(nl):
        y_ref[:, h * LANES:(h + 1) * LANES] = uf_ref[h]
    hs_ref[pad - 1:pad, :] = hs_ref[pad + tcr - 1:pad + tcr, :]


def _s5_core(h1, bd, cd, pw8, tcr):
    t = h1.shape[0]
    width = S5_L * S5_KT_CH
    tcr = min(tcr, t // S5_L)
    tc = tcr * S5_L
    ns = S5_KT_STATES
    u_col0 = 3 * S5_CHANNELS // S5_KT_CH
    once = pl.Buffered(1)
    return pl.pallas_call(
        functools.partial(_s5_core_kernel, tcr=tcr),
        grid=(S5_KT, t // tc),
        in_specs=[
            pl.BlockSpec((tc, S5_KT_CH), lambda k, i: (i, u_col0 + k)),
            pl.BlockSpec((None, S5_KT_CH, 2 * ns), lambda k, i: (k, 0, 0), pipeline_mode=once),
            pl.BlockSpec((None, 2 * ns, S5_KT_CH), lambda k, i: (k, 0, 0), pipeline_mode=once),
            pl.BlockSpec((None, 2, LANES, ns), lambda k, i: (k, 0, 0, 0), pipeline_mode=once),
        ],
        out_specs=pl.BlockSpec((tc, S5_KT_CH), lambda k, i: (i, k)),
        out_shape=jax.ShapeDtypeStruct((t, S5_CHANNELS), F32),
        scratch_shapes=[pltpu.VMEM((tcr + SUBLANES, 2 * ns), F32), pltpu.VMEM((SUBLANES, 2 * ns), F32),
                        pltpu.VMEM((2, SUBLANES, ns), F32), pltpu.VMEM((6, SUBLANES, ns), F32),
                        pltpu.VMEM((S5_KT_CH // LANES, tc, LANES), F32),
                        pltpu.VMEM((width, width), BF16), pltpu.VMEM((width, 2 * ns), BF16),
                        pltpu.VMEM((2 * ns, width), BF16)],
        compiler_params=_params(("arbitrary", "arbitrary")),
        name="s5_core",
    )(h1, bd, cd, pw8)


def _s5_glu_kernel(y_ref, u_ref, d_ref, wglu_ref, bglu_ref, o_ref):
    y = y_ref[...] + d_ref[...] * u_ref[...].astype(F32)
    g = 0.5 * y * (1.0 + jnp.tanh(math.sqrt(2.0 / math.pi) * (y + 0.044715 * (y * y * y))))
    gl = _dot(g.astype(BF16), wglu_ref[...]) + bglu_ref[...]
    o_ref[...] = (g / (1.0 + jnp.exp(-gl))).astype(o_ref.dtype)


def _s5_glu(y, h1, d_skip, w_glu, b_glu, tm):
    t = h1.shape[0]
    tm = min(tm, t)
    full = lambda shape: pl.BlockSpec(shape, lambda i: (0,) * len(shape))
    return pl.pallas_call(
        _s5_glu_kernel,
        grid=(t // tm,),
        in_specs=[pl.BlockSpec((tm, S5_CHANNELS), lambda i: (i, 0)),
                  pl.BlockSpec((tm, S5_CHANNELS), lambda i: (i, 3)),
                  full((1, S5_CHANNELS)), full((S5_CHANNELS, S5_CHANNELS)), full((1, S5_CHANNELS))],
        out_specs=pl.BlockSpec((tm, S5_CHANNELS), lambda i: (i, 0)),
        out_shape=jax.ShapeDtypeStruct((t, S5_CHANNELS), BF16),
        compiler_params=_params(("parallel",)),
        name="s5_glu",
    )(y, h1, d_skip, w_glu, b_glu)


def _route(probs, sel):
    rows = [sel[e:e + 1, :] for e in range(N_EXPERTS)]
    prow = [probs[e:e + 1, :] for e in range(N_EXPERTS)]
    best_score = None
    for g in range(N_GROUPS):
        a = rows[g * EXPERTS_PER_GROUP:(g + 1) * EXPERTS_PER_GROUP]
        score = None
        for x in range(EXPERTS_PER_GROUP):
            for y in range(x + 1, EXPERTS_PER_GROUP):
                pair = a[x] + a[y]
                score = pair if score is None else jnp.maximum(score, pair)
        if best_score is None:
            best_score, gbest = score, jnp.zeros(score.shape, jnp.int32)
            cs = list(a)
            cp = prow[0:EXPERTS_PER_GROUP]
        else:
            better = score > best_score
            best_score = jnp.where(better, score, best_score)
            gbest = jnp.where(better, g, gbest)
            cs = [jnp.where(better, a[x], cs[x]) for x in range(EXPERTS_PER_GROUP)]
            cp = [jnp.where(better, prow[g * EXPERTS_PER_GROUP + x], cp[x]) for x in range(EXPERTS_PER_GROUP)]

    def first_argmax(vals):
        bv, bi = vals[0], jnp.zeros(vals[0].shape, jnp.int32)
        for x in range(1, len(vals)):
            better = vals[x] > bv
            bv = jnp.where(better, vals[x], bv)
            bi = jnp.where(better, x, bi)
        return bi

    i1 = first_argmax(cs)
    i2 = first_argmax([jnp.where(i1 == x, -jnp.inf, cs[x]) for x in range(EXPERTS_PER_GROUP)])

    def pick(idx):
        out = cp[0]
        for x in range(1, EXPERTS_PER_GROUP):
            out = jnp.where(idx == x, cp[x], out)
        return out

    p1, p2 = pick(i1), pick(i2)
    tot = p1 + p2
    return gbest * EXPERTS_PER_GROUP + i1, gbest * EXPERTS_PER_GROUP + i2, p1 / tot, p2 / tot


def _ln_router_tail(v, g_ref, b_ref, rwh_ref, rwl_ref, rb_ref):
    xn = _layer_norm(v, g_ref[...], b_ref[...])
    xh = xn.astype(BF16)
    xl = (xn - xh.astype(F32)).astype(BF16)
    logits = _nt_dot(rwh_ref[...], xh) + (_nt_dot(rwl_ref[...], xh) + _nt_dot(rwh_ref[...], xl))
    mx = jnp.max(logits, axis=0, keepdims=True)
    ex = jnp.exp(logits - mx)
    probs = ex / jnp.sum(ex, axis=0, keepdims=True)
    e1, e2, w1, w2 = _route(probs, probs + rb_ref[...])
    n = e1.shape[1]
    ei = jnp.concatenate([e1, e2, jnp.zeros((SUBLANES - 2, n), jnp.int32)], axis=0)
    ew = jnp.concatenate([w1, w2, jnp.zeros((SUBLANES - 2, n), F32)], axis=0)
    return xn, ei, ew


def _outproj_kernel(ma_ref, mb_ref, w_ref, xr_ref, g_ref, b_ref, rwh_ref, rwl_ref, rb_ref,
                    x_ref, ei_ref, ew_ref, w_bf, *, sub):
    @pl.when(pl.program_id(0) == 0)
    def _():
        w_bf[...] = w_ref[...].astype(BF16)

    half = w_bf.shape[0] // 2
    for s in range(x_ref.shape[0] // sub):
        rows = slice(s * sub, (s + 1) * sub)
        v = (DEEPNORM_ALPHA * xr_ref[rows, :] + _dot(ma_ref[rows, :], w_bf[:half, :])
             + _dot(mb_ref[rows, :], w_bf[half:, :]))
        xn, ei, ew = _ln_router_tail(v, g_ref, b_ref, rwh_ref, rwl_ref, rb_ref)
        x_ref[rows, :] = xn
        ei_ref[:, rows] = ei
        ew_ref[:, rows] = ew


def _outproj_ln_router(mix_a, mix_b, w_out, x_res, ln_g, ln_b, rwt, rb, tm, sub):
    t = x_res.shape[0]
    rwt_hi = rwt.astype(BF16)
    rwt_lo = (rwt - rwt_hi.astype(F32)).astype(BF16)
    tm = min(tm, t)
    sub = min(sub, tm)
    half = w_out.shape[0] // 2
    row = lambda w: pl.BlockSpec((tm, w), lambda i: (i, 0))
    full = lambda shape: pl.BlockSpec(shape, lambda i: (0,) * len(shape))
    return pl.pallas_call(
        functools.partial(_outproj_kernel, sub=sub),
        grid=(t // tm,),
        in_specs=[row(half), row(half),
                  pl.BlockSpec((2 * half, D_MODEL), lambda i: (0, 0), pipeline_mode=pl.Buffered(1)),
                  row(D_MODEL), full((1, D_MODEL)), full((1, D_MODEL)),
                  full((N_EXPERTS, D_MODEL)), full((N_EXPERTS, D_MODEL)), full((N_EXPERTS, 1))],
        out_specs=[row(D_MODEL),
                   pl.BlockSpec((SUBLANES, tm), lambda i: (0, i)), pl.BlockSpec((SUBLANES, tm), lambda i: (0, i))],
        out_shape=[jax.ShapeDtypeStruct((t, D_MODEL), F32),
                   jax.ShapeDtypeStruct((SUBLANES, t), jnp.int32), jax.ShapeDtypeStruct((SUBLANES, t), F32)],
        scratch_shapes=[pltpu.VMEM((2 * half, D_MODEL), BF16)],
        compiler_params=_params(("arbitrary",)),
        name="outproj_ln_router",
    )(mix_a, mix_b, w_out, x_res, ln_g, ln_b, rwt_hi, rwt_lo, rb)


def _rank_kernel(e_ref, rank_ref, cnt_ref, base_ref, *, tn):
    @pl.when(pl.program_id(0) == 0)
    def _():
        base_ref[...] = jnp.zeros(base_ref.shape, F32)

    e = e_ref[...]
    rows = lax.broadcasted_iota(jnp.int32, (N_EXPERTS, tn), 0)
    oh1 = rows == e[0:1, :]
    oh2 = rows == e[1:2, :]
    oh = jnp.where(jnp.logical_or(oh1, oh2), 1.0, 0.0)
    r = lax.broadcasted_iota(jnp.int32, (tn, tn), 0)
    c = lax.broadcasted_iota(jnp.int32, (tn, tn), 1)
    earlier = jnp.where(r < c, 1.0, 0.0).astype(BF16)
    base = base_ref[...]
    before = _dot(oh.astype(BF16), earlier) + jnp.tile(base, (1, tn // LANES))
    r1 = jnp.sum(jnp.where(oh1, before, 0.0), axis=0, keepdims=True)
    r2 = jnp.sum(jnp.where(oh2, before, 0.0), axis=0, keepdims=True)
    rank_ref[...] = jnp.concatenate([r1, r2, jnp.zeros((SUBLANES - 2, tn), F32)], axis=0).astype(jnp.int32)
    base = base + jnp.sum(oh, axis=1, keepdims=True)
    base_ref[...] = base
    cnt_ref[...] = base.astype(jnp.int32)


def _rank(eidx, tn):
    t = eidx.shape[1]
    tn = min(tn, t)
    return pl.pallas_call(
        functools.partial(_rank_kernel, tn=tn),
        grid=(t // tn,),
        in_specs=[pl.BlockSpec((SUBLANES, tn), lambda i: (0, i))],
        out_specs=[pl.BlockSpec((SUBLANES, tn), lambda i: (0, i)),
                   pl.BlockSpec((N_EXPERTS, LANES), lambda i: (0, 0))],
        out_shape=[jax.ShapeDtypeStruct((SUBLANES, t), jnp.int32),
                   jax.ShapeDtypeStruct((N_EXPERTS, LANES), jnp.int32)],
        scratch_shapes=[pltpu.VMEM((N_EXPERTS, LANES), F32)],
        compiler_params=_params(("arbitrary",)),
        name="moe_rank",
    )(eidx)


def _moe_kernel(te_ref, nu_ref, xs_ref, wg_ref, wu_ref, wd_ref, *rest, tile0):
    y_ref, wgu_s, wd_s = rest[-3:]
    i = pl.program_id(0)
    tile = tile0 + i
    used = tile < nu_ref[0]
    fresh = jnp.logical_or(i == 0, te_ref[tile] != te_ref[jnp.maximum(tile - 1, 0)])

    @pl.when(jnp.logical_and(used, fresh))
    def _():
        wgu_s[:, :D_EXPERT] = wg_ref[...].astype(BF16)
        wgu_s[:, D_EXPERT:] = wu_ref[...].astype(BF16)
        wd_s[...] = wd_ref[...].astype(BF16)

    @pl.when(used)
    def _():
        hgu = _dot(xs_ref[...].astype(BF16), wgu_s[...])
        hg = hgu[:, :D_EXPERT]
        hu = hgu[:, D_EXPERT:]
        h = (hg / (1.0 + jnp.exp(-hg))) * hu
        y_ref[...] = _dot(h.astype(BF16), wd_s[...])

    @pl.when(jnp.logical_not(used))
    def _():
        y_ref[...] = jnp.zeros(y_ref.shape, F32)


def _moe_part(tile_expert, n_used, xs_part, w_gate, w_up, w_down, ys_prev, tile0, n_tiles, tm):
    nt = xs_part.shape[0] // tm
    in_specs = [
        pl.BlockSpec((tm, D_MODEL), lambda i, te, nu: (i, 0)),
        pl.BlockSpec((None, D_MODEL, D_EXPERT), lambda i, te, nu: (te[tile0 + i], 0, 0)),
        pl.BlockSpec((None, D_MODEL, D_EXPERT), lambda i, te, nu: (te[tile0 + i], 0, 0)),
        pl.BlockSpec((None, D_EXPERT, D_MODEL), lambda i, te, nu: (te[tile0 + i], 0, 0)),
    ]
    args = [tile_expert, n_used, xs_part, w_gate, w_up, w_down]
    aliases = {}
    if ys_prev is not None:
        in_specs.append(pl.BlockSpec(memory_space=pl.ANY))
        args.append(ys_prev)
        aliases = {len(args) - 1: 0}
    grid_spec = pltpu.PrefetchScalarGridSpec(
        num_scalar_prefetch=2,
        grid=(nt,),
        in_specs=in_specs,
        out_specs=pl.BlockSpec((tm, D_MODEL), lambda i, te, nu: (tile0 + i, 0)),
        scratch_shapes=[pltpu.VMEM((D_MODEL, 2 * D_EXPERT), BF16), pltpu.VMEM((D_EXPERT, D_MODEL), BF16)],
    )
    return pl.pallas_call(
        functools.partial(_moe_kernel, tile0=tile0),
        grid_spec=grid_spec,
        out_shape=jax.ShapeDtypeStruct((n_tiles * tm, D_MODEL), F32),
        input_output_aliases=aliases,
        compiler_params=_params(("arbitrary",)),
        name="moe",
    )(*args)


def _moe_ln_kernel(x_ref, y1_ref, y2_ref, w_ref, g_ref, b_ref, *rest):
    o_ref = rest[-1]
    w = w_ref[...]
    v = DEEPNORM_ALPHA * x_ref[...] + (w[:, 0:1] * y1_ref[...] + w[:, 1:2] * y2_ref[...])
    o_ref[...] = _layer_norm(v, g_ref[...], b_ref[...])


def _moe_ln_part(x, y1, y2, w_cols, ln_g, ln_b, out_prev, row0, tm):
    t = x.shape[0]
    b0 = row0 // tm
    here = pl.BlockSpec((tm, D_MODEL), lambda i: (i, 0))
    there = pl.BlockSpec((tm, D_MODEL), lambda i: (b0 + i, 0))
    vec = pl.BlockSpec((1, D_MODEL), lambda i: (0, 0))
    in_specs = [there, here, here, pl.BlockSpec((tm, SUBLANES), lambda i: (b0 + i, 0)), vec, vec]
    args = [x, y1, y2, w_cols, ln_g, ln_b]
    aliases = {}
    if out_prev is not None:
        in_specs.append(pl.BlockSpec(memory_space=pl.ANY))
        args.append(out_prev)
        aliases = {len(args) - 1: 0}
    return pl.pallas_call(
        _moe_ln_kernel,
        grid=(y1.shape[0] // tm,),
        in_specs=in_specs,
        out_specs=there,
        out_shape=jax.ShapeDtypeStruct((t, D_MODEL), F32),
        input_output_aliases=aliases,
        compiler_params=_params(("parallel",)),
        name="moe_ln",
    )(*args)


def _moe_block(x, eidx, ew, w_gate, w_up, w_down, ln_g, ln_b, tm):
    t = x.shape[0]
    tm = min(tm, t)
    rank, cnt = _rank(eidx, 512)
    counts = cnt[:, 0]
    padded = ((counts + tm - 1) // tm) * tm
    ends = jnp.cumsum(padded)
    starts = ends - padded
    e2 = eidx[:2]
    start_of = jnp.zeros_like(e2)
    for k in range(N_EXPERTS):
        start_of = jnp.where(e2 == k, starts[k], start_of)
    dest = start_of + rank[:2]
    t_pad = 2 * t + N_EXPERTS * tm
    n_tiles = t_pad // tm
    tok = jnp.tile(jnp.arange(t, dtype=jnp.int32), 2)
    src_tok = (jnp.arange(t_pad, dtype=jnp.int32) % t).at[dest.reshape(2 * t)].set(
        tok, mode="promise_in_bounds", unique_indices=True)
    tile_start = jnp.arange(n_tiles, dtype=jnp.int32) * tm
    tile_expert = jnp.minimum(jnp.sum((ends[None, :] <= tile_start[:, None]).astype(jnp.int32), axis=1),
                              N_EXPERTS - 1)
    n_used = (ends[-1] // tm).astype(jnp.int32).reshape(1)
    ys = None
    tiles_per = n_tiles // MOE_PARTS
    for p in range(MOE_PARTS):
        lo = p * tiles_per * tm
        xs_p = x.at[src_tok[lo:lo + tiles_per * tm]].get(mode="promise_in_bounds")
        ys = _moe_part(tile_expert, n_used, xs_p, w_gate, w_up, w_down, ys, p * tiles_per, n_tiles, tm)
    out = None
    rows_per = t // MOE_PARTS
    w_cols = ew.T
    for p in range(MOE_PARTS):
        lo = p * rows_per
        y1 = ys.at[dest[0, lo:lo + rows_per]].get(mode="promise_in_bounds")
        y2 = ys.at[dest[1, lo:lo + rows_per]].get(mode="promise_in_bounds")
        out = _moe_ln_part(x, y1, y2, w_cols, ln_g, ln_b, out, lo, tm)
    return out


def kernel(x, router_w, router_b, l0_w_in, l0_diff_lam_q1, l0_diff_lam_k1, l0_diff_lam_q2, l0_diff_lam_k2, l0_diff_subln_g, l0_gla_w_gate, l0_gla_b_gate, l0_gla_norm_g, l0_w_out, l0_ln1_g, l0_ln1_b, l0_moe_w_gate, l0_moe_w_up, l0_moe_w_down, l0_ln2_g, l0_ln2_b, l1_w_in, l1_s5_a_re, l1_s5_a_im, l1_s5_log_step, l1_s5_b_re, l1_s5_b_im, l1_s5_c_re, l1_s5_c_im, l1_s5_d, l1_s5_w_glu, l1_s5_b_glu, l1_w_out, l1_ln1_g, l1_ln1_b, l1_moe_w_gate, l1_moe_w_up, l1_moe_w_down, l1_ln2_g, l1_ln2_b):
    bsz, t, d = x.shape
    assert bsz == 1 and d == D_MODEL
    x0 = x.reshape(t, d)
    vec = lambda p: p.reshape(1, -1).astype(F32)
    rwt = router_w.T.astype(F32)
    rb = router_b.reshape(N_EXPERTS, 1).astype(F32)

    lambda_init = 0.8 - 0.6 * math.exp(-0.3 * 0)
    diff_w = DIFF_HEADS * DIFF_V_DIM
    gla_qk, gla_v = GLA_HEADS * GLA_DK, GLA_HEADS * GLA_DV
    c = [0]
    for width in (diff_w, diff_w, diff_w, gla_qk, gla_qk, gla_v, GLA_GATE_RANK, gla_v):
        c.append(c[-1] + width)
    q_scale = math.log2(math.e) * DIFF_HEAD_DIM ** -0.5
    w0 = jnp.concatenate([l0_w_in[:, :c[1]] * q_scale, l0_w_in[:, c[1]:c[6]], l0_w_in[:, c[7]:c[8]],
                          l0_w_in[:, c[6]:c[7]], jnp.zeros((d, L0_GB_PAD - GLA_GATE_RANK), l0_w_in.dtype)],
                         axis=1).astype(BF16)
    h0 = _matmul(x0, w0, 1024, L0_IN_TN, BF16)
    lam_vec = jnp.stack([l0_diff_lam_q1, l0_diff_lam_k1, l0_diff_lam_q2, l0_diff_lam_k2]).astype(F32)
    o_a = _diff_attention(h0, lam_vec, vec(l0_diff_subln_g), lambda_init, 1024)
    wg_pad = jnp.concatenate([l0_gla_w_gate, jnp.zeros((LANES - GLA_GATE_RANK, GLA_HEADS * GLA_DK), F32)],
                             axis=0).astype(BF16)
    o_b = _gla(h0, wg_pad, vec(l0_gla_b_gate), vec(l0_gla_norm_g), 256)
    x1, eidx, ew = _outproj_ln_router(o_a, o_b, l0_w_out, x0, vec(l0_ln1_g), vec(l0_ln1_b),
                                      rwt, rb, 512, 128)
    x2 = _moe_block(x1, eidx, ew, l0_moe_w_gate, l0_moe_w_up, l0_moe_w_down, vec(l0_ln2_g), vec(l0_ln2_b), 256)

    sb_w = SB_HEADS * SB_HEAD_DIM
    w1 = jnp.concatenate([l1_w_in[:, :sb_w] * SB_HEAD_DIM ** -0.5, l1_w_in[:, sb_w:]], axis=1).astype(BF16)
    h1 = _matmul(x2, w1, 1024, 1024, BF16)
    o_c = _stick_breaking(h1, 256)
    lbr, lbi, bbr, bbi = _s5_prep(l1_s5_a_re, l1_s5_a_im, l1_s5_log_step, l1_s5_b_re, l1_s5_b_im)
    bd, cd, pw8 = _s5_tile_params(lbr[::S5_GROUP], lbi[::S5_GROUP], bbr.reshape(S5_GROUPS, S5_GROUP, S5_STATE),
                                  bbi.reshape(S5_GROUPS, S5_GROUP, S5_STATE),
                                  l1_s5_c_re.astype(F32), l1_s5_c_im.astype(F32))
    y_s5 = _s5_core(h1, bd, cd, pw8, 256)
    o_d = _s5_glu(y_s5, h1, vec(l1_s5_d), l1_s5_w_glu.astype(BF16), vec(l1_s5_b_glu), 512)
    x3, eidx1, ew1 = _outproj_ln_router(o_c, o_d, l1_w_out, x2, vec(l1_ln1_g), vec(l1_ln1_b),
                                        rwt, rb, 512, 128)
    x4 = _moe_block(x3, eidx1, ew1, l1_moe_w_gate, l1_moe_w_up, l1_moe_w_down, vec(l1_ln2_g), vec(l1_ln2_b), 256)
    return x4.reshape(bsz, t, d)
```

```python
import functools
import math

import jax
import jax.numpy as jnp
from jax import lax
from jax.experimental import pallas as pl
from jax.experimental.pallas import tpu as pltpu

F32 = jnp.float32
BF16 = jnp.bfloat16

D_MODEL = 2048
DEPTH = 2
DIFF_HEADS = 8
DIFF_HEAD_DIM = 64
DIFF_V_DIM = 128
GLA_HEADS = 4
GLA_DK = 128
GLA_DV = 256
GLA_GATE_RANK = 16
GLA_GATE_TEMP = 16.0
GLA_CHUNK = 64
SB_HEADS = 8
SB_HEAD_DIM = 128
S5_CHANNELS = 1024
S5_GROUP = 16
S5_GROUPS = 64
S5_STATE = 64
N_EXPERTS = 16
N_GROUPS = 4
EXPERTS_PER_GROUP = 4
D_EXPERT = 640
DEEPNORM_ALPHA = (2.0 * DEPTH) ** 0.25
LN_EPS = 1e-5

LANES = 128
SUBLANES = 8
VMEM_LIMIT = 56 * 1024 * 1024
SB_LOG_FLOOR = 120.0
SB_HEADS_PER_STEP = 4
MOE_PARTS = 2

MXU_N = 256
L0_GB_PAD = MXU_N
L0_COL = {"qa": 0, "ka": 1024, "va": 2048, "qb": 3072, "kb": 3584, "vb": 4096, "rb": 5120, "gb": 6144}
L0_IN_TN = 5 * MXU_N

S5_KT = 4
S5_KT_CH = S5_CHANNELS // S5_KT
S5_KT_STATES = (S5_GROUPS // S5_KT) * S5_STATE
S5_SCAN_W = 512
S5_L = 8


def _params(sem, flags=None):
    return pltpu.CompilerParams(dimension_semantics=sem, vmem_limit_bytes=VMEM_LIMIT, flags=flags)


def _nt_dot(a, b):
    return lax.dot_general(a, b, (((1,), (1,)), ((), ())), preferred_element_type=F32)


def _dot(a, b):
    return jnp.dot(a, b, preferred_element_type=F32)


def _split_dot_left(m, x):
    hi = x.astype(BF16)
    lo = (x - hi.astype(F32)).astype(BF16)
    return _dot(m, hi) + _dot(m, lo)


def _layer_norm(v, g, b):
    mu = jnp.mean(v, axis=-1, keepdims=True)
    d = v - mu
    var = jnp.mean(d * d, axis=-1, keepdims=True)
    return d * lax.rsqrt(var + LN_EPS) * g + b


def _mm_kernel(a_ref, b_ref, o_ref, a_bf):
    @pl.when(pl.program_id(1) == 0)
    def _():
        a_bf[...] = a_ref[...].astype(BF16)

    o_ref[...] = _dot(a_bf[...], b_ref[...]).astype(o_ref.dtype)


def _matmul(a, b, tm, tn, out_dtype):
    m, k = a.shape
    n = b.shape[1]
    tm = min(tm, m)
    return pl.pallas_call(
        _mm_kernel,
        grid=(m // tm, n // tn),
        in_specs=[pl.BlockSpec((tm, k), lambda i, j: (i, 0)),
                  pl.BlockSpec((k, tn), lambda i, j: (0, j))],
        out_specs=pl.BlockSpec((tm, tn), lambda i, j: (i, j)),
        out_shape=jax.ShapeDtypeStruct((m, n), out_dtype),
        scratch_shapes=[pltpu.VMEM((tm, k), BF16)],
        compiler_params=_params(("parallel", "arbitrary")),
        name="in_proj",
    )(a, b)


def _diff_attn_kernel(lam_ref, g_ref, q_ref, k_ref, v_ref, o_ref, vx_ref, m_ref, acc_ref, sa_ref, sb_ref, *, tq,
                      lambda_init):
    i = pl.program_id(1)
    dv = DIFF_V_DIM

    @pl.when(i == 0)
    def _():
        vx_ref[:, :dv] = v_ref[...]
        vx_ref[:, dv:] = jnp.ones((vx_ref.shape[0], dv), BF16)

    m_ref[...] = jnp.full(m_ref.shape, -jnp.inf, F32)
    acc_ref[...] = jnp.zeros(acc_ref.shape, F32)
    q = q_ref[...]
    lane = lax.broadcasted_iota(jnp.int32, q.shape, 1)
    zero = jnp.zeros_like(q)
    halves = (jnp.where(lane < DIFF_HEAD_DIM, q, zero), jnp.where(lane >= DIFF_HEAD_DIM, q, zero))

    def scores(j, mi):
        off = pl.multiple_of(j * tq, tq)
        return _nt_dot(halves[mi], k_ref[pl.ds(off, tq), :])

    def softmax_pv(j, get_s, masked):
        off = pl.multiple_of(j * tq, tq)
        vx = vx_ref[pl.ds(off, tq), :]
        if masked:
            row = lax.broadcasted_iota(jnp.int32, (tq, tq), 0)
            col = lax.broadcasted_iota(jnp.int32, (tq, tq), 1)
            causal = col <= row
        for mi in range(2):
            s = get_s(mi)
            if masked:
                s = jnp.where(causal, s, -jnp.inf)
            m_prev = m_ref[mi]
            m_new = jnp.maximum(m_prev, jnp.max(s, axis=1, keepdims=True))
            p = jnp.exp2(s - jnp.tile(m_new, (1, tq // LANES)))
            alpha = jnp.exp2(m_prev - m_new)
            acc_ref[mi] = jnp.tile(alpha, (1, 2)) * acc_ref[mi] + _dot(p.astype(BF16), vx)
            m_ref[mi] = m_new

    def scores_to(j, s_ref):
        for mi in range(2):
            s_ref[mi] = scores(j, mi)

    scores_to(0, sa_ref)

    def pair(t, carry):
        j = 2 * t
        scores_to(j + 1, sb_ref)
        softmax_pv(j, lambda mi: sa_ref[mi], False)
        scores_to(j + 2, sa_ref)
        softmax_pv(j + 1, lambda mi: sb_ref[mi], False)
        return carry

    lax.fori_loop(0, i // 2, pair, 0)

    @pl.when(i % 2 == 1)
    def _():
        softmax_pv(i - 1, lambda mi: sa_ref[mi], False)
        scores_to(i, sa_ref)

    softmax_pv(i, lambda mi: sa_ref[mi], True)

    lv = lam_ref[...]
    lam = (jnp.exp(jnp.sum(lv[0:1] * lv[1:2], axis=1, keepdims=True))
           - jnp.exp(jnp.sum(lv[2:3] * lv[3:4], axis=1, keepdims=True)) + lambda_init)
    a0 = acc_ref[0]
    a1 = acc_ref[1]
    o = a0[:, :dv] / a0[:, dv:] - lam * (a1[:, :dv] / a1[:, dv:])
    ms = jnp.mean(o * o, axis=-1, keepdims=True)
    o = o * lax.rsqrt(ms + LN_EPS) * g_ref[...] * (1.0 - lambda_init)
    o_ref[...] = o.astype(o_ref.dtype)


def _diff_attention(h0, lam_vec, subln_g, lambda_init, tq):
    t = h0.shape[0]
    tq = min(tq, t)
    nq = t // tq
    kern = functools.partial(_diff_attn_kernel, tq=tq, lambda_init=lambda_init)
    return pl.pallas_call(
        kern,
        grid=(DIFF_HEADS, nq),
        in_specs=[
            pl.BlockSpec((4, DIFF_HEAD_DIM), lambda h, i: (0, 0)),
            pl.BlockSpec((1, DIFF_V_DIM), lambda h, i: (0, 0)),
            pl.BlockSpec((tq, LANES), lambda h, i: (i, h)),
            pl.BlockSpec((t, LANES), lambda h, i: (0, DIFF_HEADS + h)),
            pl.BlockSpec((t, LANES), lambda h, i: (0, 2 * DIFF_HEADS + h)),
        ],
        out_specs=pl.BlockSpec((tq, LANES), lambda h, i: (i, h)),
        out_shape=jax.ShapeDtypeStruct((t, DIFF_HEADS * DIFF_V_DIM), BF16),
        scratch_shapes=[pltpu.VMEM((t, 2 * DIFF_V_DIM), BF16), pltpu.VMEM((2, tq, LANES), F32),
                        pltpu.VMEM((2, tq, 2 * DIFF_V_DIM), F32),
                        pltpu.VMEM((2, tq, tq), F32), pltpu.VMEM((2, tq, tq), F32)],
        compiler_params=_params(("arbitrary", "arbitrary")),
        name="diff_attn",
    )(lam_vec, subln_g, h0, h0, h0)


def _gla_kernel(q_ref, k_ref, v_ref, r_ref, gb_ref, wg_ref, bg_ref, ng_ref, o_ref, st_ref, *, tg):
    @pl.when(pl.program_id(0) == 0)
    def _():
        st_ref[...] = jnp.zeros(st_ref.shape, F32)

    c = GLA_CHUNK
    gate = _dot(gb_ref[...], wg_ref[...]) + bg_ref[...]
    log_a = -(jnp.maximum(-gate, 0.0) + jnp.log(1.0 + jnp.exp(-jnp.abs(gate)))) / GLA_GATE_TEMP
    row = lax.broadcasted_iota(jnp.int32, (c, c), 0)
    col = lax.broadcasted_iota(jnp.int32, (c, c), 1)
    tril = row >= col
    tri = jnp.where(tril, 1.0, 0.0).astype(BF16)
    for h in range(GLA_HEADS):
        ksl = slice(h * GLA_DK, (h + 1) * GLA_DK)
        vsl = slice(h * GLA_DV, (h + 1) * GLA_DV)
        for ci in range(tg // c):
            rs = slice(ci * c, (ci + 1) * c)
            la = log_a[rs, ksl]
            b = _split_dot_left(tri, la)
            b_last = b[c - 1:c, :]
            qf = q_ref[rs, ksl].astype(F32) * (GLA_DK ** -0.5)
            kf = k_ref[rs, ksl].astype(F32)
            v = v_ref[rs, vsl]
            q_dec = (qf * jnp.exp(b)).astype(BF16)
            k_dec = (kf * jnp.exp(-b)).astype(BF16)
            k_end = (kf * jnp.exp(b_last - b)).astype(BF16)
            scores = jnp.where(tril, _nt_dot(q_dec, k_dec), 0.0)
            st = st_ref[h]
            o = _dot(scores.astype(BF16), v) + _nt_dot(q_dec, st.astype(BF16))
            kv_t = lax.dot_general(v, k_end, (((0,), (0,)), ((), ())), preferred_element_type=F32)
            st_ref[h] = jnp.exp(b_last) * st + kv_t
            ms = jnp.mean(o * o, axis=-1, keepdims=True)
            o = o * lax.rsqrt(ms + LN_EPS) * ng_ref[...]
            r = r_ref[rs, vsl].astype(F32)
            o_ref[rs, vsl] = (o * (r / (1.0 + jnp.exp(-r)))).astype(o_ref.dtype)


def _gla(h0, wg_pad, b_gate, norm_g, tg):
    t = h0.shape[0]
    tg = min(tg, t)
    qk = GLA_HEADS * GLA_DK
    vw = GLA_HEADS * GLA_DV
    return pl.pallas_call(
        functools.partial(_gla_kernel, tg=tg),
        grid=(t // tg,),
        in_specs=[
            pl.BlockSpec((tg, qk), lambda i: (i, L0_COL["qb"] // qk)),
            pl.BlockSpec((tg, qk), lambda i: (i, L0_COL["kb"] // qk)),
            pl.BlockSpec((tg, vw), lambda i: (i, L0_COL["vb"] // vw)),
            pl.BlockSpec((tg, vw), lambda i: (i, L0_COL["rb"] // vw)),
            pl.BlockSpec((tg, LANES), lambda i: (i, L0_COL["gb"] // LANES)),
            pl.BlockSpec((LANES, qk), lambda i: (0, 0)),
            pl.BlockSpec((1, qk), lambda i: (0, 0)),
            pl.BlockSpec((1, GLA_DV), lambda i: (0, 0)),
        ],
        out_specs=pl.BlockSpec((tg, vw), lambda i: (i, 0)),
        out_shape=jax.ShapeDtypeStruct((t, vw), BF16),
        scratch_shapes=[pltpu.VMEM((GLA_HEADS, GLA_DV, GLA_DK), F32)],
        compiler_params=_params(("arbitrary",)),
        name="gla",
    )(h0, h0, h0, h0, h0, wg_pad, b_gate, norm_g)


def _sb_kernel(q_ref, k_ref, v_ref, o_ref, acc_ref, c_ref, *, tq):
    i = pl.program_id(1)
    hd = SB_HEAD_DIM
    acc_ref[...] = jnp.zeros(acc_ref.shape, F32)
    c_ref[...] = jnp.zeros(c_ref.shape, F32)
    row = lax.broadcasted_iota(jnp.int32, (tq, tq), 0)
    col = lax.broadcasted_iota(jnp.int32, (tq, tq), 1)
    strict = col < row
    tri = jnp.where(row > col, 1.0, 0.0).astype(BF16)

    def block(jb, masked):
        off = pl.multiple_of(jb * tq, tq)
        cmax = None
        for hh in range(SB_HEADS_PER_STEP):
            cols = slice(hh * hd, (hh + 1) * hd)
            k = k_ref[pl.ds(off, tq), cols]
            v = v_ref[pl.ds(off, tq), cols]
            z = _nt_dot(q_ref[:, cols], k)
            lsn = -(jnp.maximum(z, 0.0) + jnp.log(1.0 + jnp.exp(-jnp.abs(z))))
            lm = jnp.where(strict, lsn, 0.0) if masked else lsn
            excl = _dot(lm.astype(BF16), tri) + c_ref[hh]
            w = jnp.exp(z + lsn + excl)
            if masked:
                w = jnp.where(strict, w, 0.0)
            acc_ref[:, cols] += _dot(w.astype(BF16), v)
            c_new = c_ref[hh] + jnp.sum(lm, axis=1, keepdims=True)
            c_ref[hh] = c_new
            hmax = jnp.max(c_new)
            cmax = hmax if cmax is None else jnp.maximum(cmax, hmax)
        return cmax

    cmax = block(i, True)

    def cond(carry):
        jb, cm = carry
        return jnp.logical_and(jb >= 0, cm > -SB_LOG_FLOOR)

    def body(carry):
        jb, _ = carry
        return jb - 1, block(jb, False)

    lax.while_loop(cond, body, (i - 1, cmax))
    o_ref[...] = acc_ref[...].astype(o_ref.dtype)


def _stick_breaking(h1, tq):
    t = h1.shape[0]
    tq = min(tq, t)
    ng = SB_HEADS // SB_HEADS_PER_STEP
    wd = SB_HEADS_PER_STEP * SB_HEAD_DIM
    return pl.pallas_call(
        functools.partial(_sb_kernel, tq=tq),
        grid=(ng, t // tq),
        in_specs=[
            pl.BlockSpec((tq, wd), lambda h, i: (i, h)),
            pl.BlockSpec((t, wd), lambda h, i: (0, ng + h), pipeline_mode=pl.Buffered(1)),
            pl.BlockSpec((t, wd), lambda h, i: (0, 2 * ng + h), pipeline_mode=pl.Buffered(1)),
        ],
        out_specs=pl.BlockSpec((tq, wd), lambda h, i: (i, h)),
        out_shape=jax.ShapeDtypeStruct((t, SB_HEADS * SB_HEAD_DIM), BF16),
        scratch_shapes=[pltpu.VMEM((tq, wd), F32), pltpu.VMEM((SB_HEADS_PER_STEP, tq, 1), F32)],
        compiler_params=_params(("parallel", "arbitrary")),
        name="stick_breaking",
    )(h1, h1, h1)


def _s5_prep_kernel(are_ref, aim_ref, ls_ref, bre_ref, bim_ref, lbr_ref, lbi_ref, bbr_ref, bbi_ref):
    step = jnp.exp(ls_ref[...])
    lam_re = are_ref[...]
    lam_im = aim_ref[...]
    mag = jnp.exp(step * lam_re)
    lb_re = mag * jnp.cos(step * lam_im)
    lb_im = mag * jnp.sin(step * lam_im)
    den = lam_re * lam_re + lam_im * lam_im
    n_re = lb_re - 1.0
    f_re = (n_re * lam_re + lb_im * lam_im) / den
    f_im = (lb_im * lam_re - n_re * lam_im) / den
    lbr_ref[...] = lb_re
    lbi_ref[...] = lb_im
    bbr_ref[...] = f_re * bre_ref[...] - f_im * bim_ref[...]
    bbi_ref[...] = f_re * bim_ref[...] + f_im * bre_ref[...]


def _s5_prep(a_re, a_im, log_step, b_re, b_im):
    rows = S5_GROUPS * S5_GROUP
    rep = lambda p: jnp.repeat(p, S5_GROUP, axis=0)
    to_rows = lambda b: jnp.transpose(b, (0, 2, 1)).reshape(rows, S5_STATE)
    spec = pl.BlockSpec((rows, S5_STATE), lambda: (0, 0))
    shp = jax.ShapeDtypeStruct((rows, S5_STATE), F32)
    return pl.pallas_call(
        _s5_prep_kernel,
        in_specs=[spec, spec, pl.BlockSpec((rows, 1), lambda: (0, 0)), spec, spec],
        out_specs=[spec] * 4,
        out_shape=[shp] * 4,
        name="s5_prep",
    )(rep(a_re), rep(a_im), rep(log_step[:, None]), to_rows(b_re), to_rows(b_im))


def _cmul(ar, ai, br, bi):
    return ar * br - ai * bi, ar * bi + ai * br


def _block_diag_tiles(m, rows_per_group, cols_per_group):
    gpt = S5_GROUPS // S5_KT
    m = m.reshape(S5_KT, gpt, rows_per_group, cols_per_group)
    eye = jnp.eye(gpt, dtype=m.dtype)
    bd = jnp.einsum("kgrc,gh->kgrhc", m, eye)
    return bd.reshape(S5_KT, gpt * rows_per_group, gpt * cols_per_group)


def _s5_tile_params(lb_re, lb_im, bb_re, bb_im, c_re, c_im):
    bd = jnp.concatenate([_block_diag_tiles(bb_re, S5_GROUP, S5_STATE),
                          _block_diag_tiles(bb_im, S5_GROUP, S5_STATE)], axis=2)
    cd = jnp.concatenate([_block_diag_tiles(jnp.transpose(c_re, (0, 2, 1)), S5_STATE, S5_GROUP),
                          _block_diag_tiles(-jnp.transpose(c_im, (0, 2, 1)), S5_STATE, S5_GROUP)], axis=1)
    pr = [jnp.ones_like(lb_re)]
    pi = [jnp.zeros_like(lb_re)]
    for _ in range(S5_L):
        nr, ni = _cmul(pr[-1], pi[-1], lb_re, lb_im)
        pr.append(nr)
        pi.append(ni)

    def table(p):
        rows = jnp.transpose(jnp.stack(p).reshape(S5_L + 1, S5_KT, S5_KT_STATES), (1, 0, 2))
        return jnp.pad(rows, ((0, 0), (0, LANES - (S5_L + 1)), (0, 0)))

    return bd, cd, jnp.stack([table(pr), table(pi)], axis=1)


def _s5_core_kernel(u_ref, bd_ref, cd_ref, pw8_ref, y_ref, hs_ref, hp_ref, pw_ref, st_ref, uf_ref,
                    mt_s, bbs_s, ccs_s, *, tcr):
    ns = S5_KT_STATES
    pad = SUBLANES
    cw = S5_KT_CH

    @pl.when(pl.program_id(1) == 0)
    def _():
        b_re = bd_ref[:, :ns]
        b_im = bd_ref[:, ns:]
        cd_hi = cd_ref[...].astype(BF16)
        cd_lo = (cd_ref[...] - cd_hi.astype(F32)).astype(BF16)
        for k in range(S5_L):
            a = pw8_ref[0, k:k + 1, :]
            b = pw8_ref[1, k:k + 1, :]
            dk = jnp.concatenate([b_re * a - b_im * b, b_re * b + b_im * a], axis=1)
            s = S5_L - 1 - k
            bbs_s[s * cw:(s + 1) * cw, :] = dk.astype(BF16)
            dk_hi = dk.astype(BF16)
            dk_lo = (dk - dk_hi.astype(F32)).astype(BF16)
            kk = (_dot(dk_hi, cd_hi) + (_dot(dk_lo, cd_hi) + _dot(dk_hi, cd_lo))).astype(BF16)
            for s0 in range(S5_L - k):
                mt_s[s0 * cw:(s0 + 1) * cw, (s0 + k) * cw:(s0 + k + 1) * cw] = kk
        pw_t = (jnp.transpose(pw8_ref[0]), jnp.transpose(pw8_ref[1]))
        c_re = cd_ref[:ns, :]
        c_im = cd_ref[ns:, :]
        for i in range(S5_L):
            a = pw_t[0][:, i + 1:i + 2]
            b = pw_t[1][:, i + 1:i + 2]
            ccs_s[:ns, i * cw:(i + 1) * cw] = (c_re * a + c_im * b).astype(BF16)
            ccs_s[ns:, i * cw:(i + 1) * cw] = (c_im * a - c_re * b).astype(BF16)

        hp_ref[...] = jnp.zeros(hp_ref.shape, F32)
        hs_ref[0:pad, :] = jnp.zeros((pad, 2 * ns), F32)
        l1r = jnp.broadcast_to(pw8_ref[0, S5_L:S5_L + 1, :], (SUBLANES, ns))
        l1i = jnp.broadcast_to(pw8_ref[1, S5_L:S5_L + 1, :], (SUBLANES, ns))
        l2r, l2i = _cmul(l1r, l1i, l1r, l1i)
        l4r, l4i = _cmul(l2r, l2i, l2r, l2i)
        l8r, l8i = _cmul(l4r, l4i, l4r, l4i)
        ridx = lax.broadcasted_iota(jnp.int32, l1r.shape, 0)
        n = ridx + 1
        pr = jnp.ones_like(l1r)
        pi = jnp.zeros_like(l1r)
        for bit, (lr, li) in ((1, (l1r, l1i)), (2, (l2r, l2i)), (4, (l4r, l4i)), (8, (l8r, l8i))):
            nr, ni = _cmul(pr, pi, lr, li)
            take = (n & bit) != 0
            pr = jnp.where(take, nr, pr)
            pi = jnp.where(take, ni, pi)
        pw_ref[0] = pr
        pw_ref[1] = pi
        for s, (lr, li) in enumerate(((l1r, l1i), (l2r, l2i), (l4r, l4i))):
            keep = ridx >= (1 << s)
            st_ref[2 * s] = jnp.where(keep, lr, 0.0)
            st_ref[2 * s + 1] = jnp.where(keep, li, 0.0)

    nl = S5_KT_CH // LANES
    for h in range(nl):
        uf_ref[h] = u_ref[:, h * LANES:(h + 1) * LANES].astype(F32)
    uc = jnp.concatenate([uf_ref[h, pl.ds(s, tcr, stride=S5_L), :].astype(BF16)
                          for s in range(S5_L) for h in range(nl)], axis=1)
    hs_ref[pad:pad + tcr, :] = _dot(uc, bbs_s[...])
    w = S5_SCAN_W
    for cc in range(ns // w):
        re0 = cc * w
        im0 = ns + cc * w

        def body(r, carry, re0=re0, im0=im0):
            hpr, hpi = carry
            rows = pl.ds(pl.multiple_of(pad + r * SUBLANES, SUBLANES), SUBLANES)
            xr = hs_ref[rows, re0:re0 + w]
            xi = hs_ref[rows, im0:im0 + w]
            for s in range(3):
                sr = pltpu.roll(xr, 1 << s, 0)
                si = pltpu.roll(xi, 1 << s, 0)
                ar = st_ref[2 * s, :, re0:re0 + w]
                ai = st_ref[2 * s + 1, :, re0:re0 + w]
                xr, xi = xr + ar * sr - ai * si, xi + ar * si + ai * sr
            pr = pw_ref[0, :, re0:re0 + w]
            pi = pw_ref[1, :, re0:re0 + w]
            xr, xi = xr + pr * hpr - pi * hpi, xi + pr * hpi + pi * hpr
            hs_ref[rows, re0:re0 + w] = xr
            hs_ref[rows, im0:im0 + w] = xi
            return (jnp.broadcast_to(xr[SUBLANES - 1:SUBLANES, :], (SUBLANES, w)),
                    jnp.broadcast_to(xi[SUBLANES - 1:SUBLANES, :], (SUBLANES, w)))

        hpr, hpi = lax.fori_loop(0, tcr // SUBLANES, body, (hp_ref[:, re0:re0 + w], hp_ref[:, im0:im0 + w]))
        hp_ref[:, re0:re0 + w] = hpr
        hp_ref[:, im0:im0 + w] = hpi

    h_prev = hs_ref[pad - 1:pad - 1 + tcr, :]
    y_in = _dot(h_prev.astype(BF16), ccs_s[...])
    for i in range(S5_L):
        k_hi = (i + 1) * cw
        y_i = y_in[:, i * cw:k_hi] + _dot(uc[:, :k_hi], mt_s[:k_hi, i * cw:k_hi])
        for h in range(nl):
            uf_ref[h, pl.ds(i, tcr, stride=S5_L), :] = y_i[:, h * LANES:(h + 1) * LANES]
    for h in range(nl):
        y_ref[:, h * LANES:(h + 1) * LANES] = uf_ref[h]
    hs_ref[pad - 1:pad, :] = hs_ref[pad + tcr - 1:pad + tcr, :]


def _s5_core(h1, bd, cd, pw8, tcr):
    t = h1.shape[0]
    width = S5_L * S5_KT_CH
    tcr = min(tcr, t // S5_L)
    tc = tcr * S5_L
    ns = S5_KT_STATES
    u_col0 = 3 * S5_CHANNELS // S5_KT_CH
    once = pl.Buffered(1)
    return pl.pallas_call(
        functools.partial(_s5_core_kernel, tcr=tcr),
        grid=(S5_KT, t // tc),
        in_specs=[
            pl.BlockSpec((tc, S5_KT_CH), lambda k, i: (i, u_col0 + k)),
            pl.BlockSpec((None, S5_KT_CH, 2 * ns), lambda k, i: (k, 0, 0), pipeline_mode=once),
            pl.BlockSpec((None, 2 * ns, S5_KT_CH), lambda k, i: (k, 0, 0), pipeline_mode=once),
            pl.BlockSpec((None, 2, LANES, ns), lambda k, i: (k, 0, 0, 0), pipeline_mode=once),
        ],
        out_specs=pl.BlockSpec((tc, S5_KT_CH), lambda k, i: (i, k)),
        out_shape=jax.ShapeDtypeStruct((t, S5_CHANNELS), F32),
        scratch_shapes=[pltpu.VMEM((tcr + SUBLANES, 2 * ns), F32), pltpu.VMEM((SUBLANES, 2 * ns), F32),
                        pltpu.VMEM((2, SUBLANES, ns), F32), pltpu.VMEM((6, SUBLANES, ns), F32),
                        pltpu.VMEM((S5_KT_CH // LANES, tc, LANES), F32),
                        pltpu.VMEM((width, width), BF16), pltpu.VMEM((width, 2 * ns), BF16),
                        pltpu.VMEM((2 * ns, width), BF16)],
        compiler_params=_params(("arbitrary", "arbitrary")),
        name="s5_core",
    )(h1, bd, cd, pw8)


def _s5_glu_kernel(y_ref, u_ref, d_ref, wglu_ref, bglu_ref, o_ref):
    y = y_ref[...] + d_ref[...] * u_ref[...].astype(F32)
    g = 0.5 * y * (1.0 + jnp.tanh(math.sqrt(2.0 / math.pi) * (y + 0.044715 * (y * y * y))))
    gl = _dot(g.astype(BF16), wglu_ref[...]) + bglu_ref[...]
    o_ref[...] = (g / (1.0 + jnp.exp(-gl))).astype(o_ref.dtype)


def _s5_glu(y, h1, d_skip, w_glu, b_glu, tm):
    t = h1.shape[0]
    tm = min(tm, t)
    full = lambda shape: pl.BlockSpec(shape, lambda i: (0,) * len(shape))
    return pl.pallas_call(
        _s5_glu_kernel,
        grid=(t // tm,),
        in_specs=[pl.BlockSpec((tm, S5_CHANNELS), lambda i: (i, 0)),
                  pl.BlockSpec((tm, S5_CHANNELS), lambda i: (i, 3)),
                  full((1, S5_CHANNELS)), full((S5_CHANNELS, S5_CHANNELS)), full((1, S5_CHANNELS))],
        out_specs=pl.BlockSpec((tm, S5_CHANNELS), lambda i: (i, 0)),
        out_shape=jax.ShapeDtypeStruct((t, S5_CHANNELS), BF16),
        compiler_params=_params(("parallel",)),
        name="s5_glu",
    )(y, h1, d_skip, w_glu, b_glu)


def _route(probs, sel):
    rows = [sel[e:e + 1, :] for e in range(N_EXPERTS)]
    prow = [probs[e:e + 1, :] for e in range(N_EXPERTS)]
    best_score = None
    for g in range(N_GROUPS):
        a = rows[g * EXPERTS_PER_GROUP:(g + 1) * EXPERTS_PER_GROUP]
        score = None
        for x in range(EXPERTS_PER_GROUP):
            for y in range(x + 1, EXPERTS_PER_GROUP):
                pair = a[x] + a[y]
                score = pair if score is None else jnp.maximum(score, pair)
        if best_score is None:
            best_score, gbest = score, jnp.zeros(score.shape, jnp.int32)
            cs = list(a)
            cp = prow[0:EXPERTS_PER_GROUP]
        else:
            better = score > best_score
            best_score = jnp.where(better, score, best_score)
            gbest = jnp.where(better, g, gbest)
            cs = [jnp.where(better, a[x], cs[x]) for x in range(EXPERTS_PER_GROUP)]
            cp = [jnp.where(better, prow[g * EXPERTS_PER_GROUP + x], cp[x]) for x in range(EXPERTS_PER_GROUP)]

    def first_argmax(vals):
        bv, bi = vals[0], jnp.zeros(vals[0].shape, jnp.int32)
        for x in range(1, len(vals)):
            better = vals[x] > bv
            bv = jnp.where(better, vals[x], bv)
            bi = jnp.where(better, x, bi)
        return bi

    i1 = first_argmax(cs)
    i2 = first_argmax([jnp.where(i1 == x, -jnp.inf, cs[x]) for x in range(EXPERTS_PER_GROUP)])

    def pick(idx):
        out = cp[0]
        for x in range(1, EXPERTS_PER_GROUP):
            out = jnp.where(idx == x, cp[x], out)
        return out

    p1, p2 = pick(i1), pick(i2)
    tot = p1 + p2
    return gbest * EXPERTS_PER_GROUP + i1, gbest * EXPERTS_PER_GROUP + i2, p1 / tot, p2 / tot


def _ln_router_tail(v, g_ref, b_ref, rwh_ref, rwl_ref, rb_ref):
    xn = _layer_norm(v, g_ref[...], b_ref[...])
    xh = xn.astype(BF16)
    xl = (xn - xh.astype(F32)).astype(BF16)
    logits = _nt_dot(rwh_ref[...], xh) + (_nt_dot(rwl_ref[...], xh) + _nt_dot(rwh_ref[...], xl))
    mx = jnp.max(logits, axis=0, keepdims=True)
    ex = jnp.exp(logits - mx)
    probs = ex / jnp.sum(ex, axis=0, keepdims=True)
    e1, e2, w1, w2 = _route(probs, probs + rb_ref[...])
    n = e1.shape[1]
    ei = jnp.concatenate([e1, e2, jnp.zeros((SUBLANES - 2, n), jnp.int32)], axis=0)
    ew = jnp.concatenate([w1, w2, jnp.zeros((SUBLANES - 2, n), F32)], axis=0)
    return xn, ei, ew


def _outproj_kernel(ma_ref, mb_ref, w_ref, xr_ref, g_ref, b_ref, rwh_ref, rwl_ref, rb_ref,
                    x_ref, ei_ref, ew_ref, w_bf, *, sub):
    @pl.when(pl.program_id(0) == 0)
    def _():
        w_bf[...] = w_ref[...].astype(BF16)

    half = w_bf.shape[0] // 2
    for s in range(x_ref.shape[0] // sub):
        rows = slice(s * sub, (s + 1) * sub)
        v = (DEEPNORM_ALPHA * xr_ref[rows, :] + _dot(ma_ref[rows, :], w_bf[:half, :])
             + _dot(mb_ref[rows, :], w_bf[half:, :]))
        xn, ei, ew = _ln_router_tail(v, g_ref, b_ref, rwh_ref, rwl_ref, rb_ref)
        x_ref[rows, :] = xn
        ei_ref[:, rows] = ei
        ew_ref[:, rows] = ew


def _outproj_ln_router(mix_a, mix_b, w_out, x_res, ln_g, ln_b, rwt, rb, tm, sub):
    t = x_res.shape[0]
    rwt_hi = rwt.astype(BF16)
    rwt_lo = (rwt - rwt_hi.astype(F32)).astype(BF16)
    tm = min(tm, t)
    sub = min(sub, tm)
    half = w_out.shape[0] // 2
    row = lambda w: pl.BlockSpec((tm, w), lambda i: (i, 0))
    full = lambda shape: pl.BlockSpec(shape, lambda i: (0,) * len(shape))
    return pl.pallas_call(
        functools.partial(_outproj_kernel, sub=sub),
        grid=(t // tm,),
        in_specs=[row(half), row(half),
                  pl.BlockSpec((2 * half, D_MODEL), lambda i: (0, 0), pipeline_mode=pl.Buffered(1)),
                  row(D_MODEL), full((1, D_MODEL)), full((1, D_MODEL)),
                  full((N_EXPERTS, D_MODEL)), full((N_EXPERTS, D_MODEL)), full((N_EXPERTS, 1))],
        out_specs=[row(D_MODEL),
                   pl.BlockSpec((SUBLANES, tm), lambda i: (0, i)), pl.BlockSpec((SUBLANES, tm), lambda i: (0, i))],
        out_shape=[jax.ShapeDtypeStruct((t, D_MODEL), F32),
                   jax.ShapeDtypeStruct((SUBLANES, t), jnp.int32), jax.ShapeDtypeStruct((SUBLANES, t), F32)],
        scratch_shapes=[pltpu.VMEM((2 * half, D_MODEL), BF16)],
        compiler_params=_params(("arbitrary",)),
        name="outproj_ln_router",
    )(mix_a, mix_b, w_out, x_res, ln_g, ln_b, rwt_hi, rwt_lo, rb)


def _rank_kernel(e_ref, rank_ref, cnt_ref, base_ref, *, tn):
    @pl.when(pl.program_id(0) == 0)
    def _():
        base_ref[...] = jnp.zeros(base_ref.shape, F32)

    e = e_ref[...]
    rows = lax.broadcasted_iota(jnp.int32, (N_EXPERTS, tn), 0)
    oh1 = rows == e[0:1, :]
    oh2 = rows == e[1:2, :]
    oh = jnp.where(jnp.logical_or(oh1, oh2), 1.0, 0.0)
    r = lax.broadcasted_iota(jnp.int32, (tn, tn), 0)
    c = lax.broadcasted_iota(jnp.int32, (tn, tn), 1)
    earlier = jnp.where(r < c, 1.0, 0.0).astype(BF16)
    base = base_ref[...]
    before = _dot(oh.astype(BF16), earlier) + jnp.tile(base, (1, tn // LANES))
    r1 = jnp.sum(jnp.where(oh1, before, 0.0), axis=0, keepdims=True)
    r2 = jnp.sum(jnp.where(oh2, before, 0.0), axis=0, keepdims=True)
    rank_ref[...] = jnp.concatenate([r1, r2, jnp.zeros((SUBLANES - 2, tn), F32)], axis=0).astype(jnp.int32)
    base = base + jnp.sum(oh, axis=1, keepdims=True)
    base_ref[...] = base
    cnt_ref[...] = base.astype(jnp.int32)


def _rank(eidx, tn):
    t = eidx.shape[1]
    tn = min(tn, t)
    return pl.pallas_call(
        functools.partial(_rank_kernel, tn=tn),
        grid=(t // tn,),
        in_specs=[pl.BlockSpec((SUBLANES, tn), lambda i: (0, i))],
        out_specs=[pl.BlockSpec((SUBLANES, tn), lambda i: (0, i)),
                   pl.BlockSpec((N_EXPERTS, LANES), lambda i: (0, 0))],
        out_shape=[jax.ShapeDtypeStruct((SUBLANES, t), jnp.int32),
                   jax.ShapeDtypeStruct((N_EXPERTS, LANES), jnp.int32)],
        scratch_shapes=[pltpu.VMEM((N_EXPERTS, LANES), F32)],
        compiler_params=_params(("arbitrary",)),
        name="moe_rank",
    )(eidx)


def _moe_kernel(te_ref, nu_ref, xs_ref, wg_ref, wu_ref, wd_ref, *rest, tile0):
    y_ref, wgu_s, wd_s = rest[-3:]
    i = pl.program_id(0)
    tile = tile0 + i
    used = tile < nu_ref[0]
    fresh = jnp.logical_or(i == 0, te_ref[tile] != te_ref[jnp.maximum(tile - 1, 0)])

    @pl.when(jnp.logical_and(used, fresh))
    def _():
        wgu_s[:, :D_EXPERT] = wg_ref[...].astype(BF16)
        wgu_s[:, D_EXPERT:] = wu_ref[...].astype(BF16)
        wd_s[...] = wd_ref[...].astype(BF16)

    @pl.when(used)
    def _():
        hgu = _dot(xs_ref[...].astype(BF16), wgu_s[...])
        hg = hgu[:, :D_EXPERT]
        hu = hgu[:, D_EXPERT:]
        h = (hg / (1.0 + jnp.exp(-hg))) * hu
        y_ref[...] = _dot(h.astype(BF16), wd_s[...])

    @pl.when(jnp.logical_not(used))
    def _():
        y_ref[...] = jnp.zeros(y_ref.shape, F32)


def _moe_part(tile_expert, n_used, xs_part, w_gate, w_up, w_down, ys_prev, tile0, n_tiles, tm):
    nt = xs_part.shape[0] // tm
    in_specs = [
        pl.BlockSpec((tm, D_MODEL), lambda i, te, nu: (i, 0)),
        pl.BlockSpec((None, D_MODEL, D_EXPERT), lambda i, te, nu: (te[tile0 + i], 0, 0)),
        pl.BlockSpec((None, D_MODEL, D_EXPERT), lambda i, te, nu: (te[tile0 + i], 0, 0)),
        pl.BlockSpec((None, D_EXPERT, D_MODEL), lambda i, te, nu: (te[tile0 + i], 0, 0)),
    ]
    args = [tile_expert, n_used, xs_part, w_gate, w_up, w_down]
    aliases = {}
    if ys_prev is not None:
        in_specs.append(pl.BlockSpec(memory_space=pl.ANY))
        args.append(ys_prev)
        aliases = {len(args) - 1: 0}
    grid_spec = pltpu.PrefetchScalarGridSpec(
        num_scalar_prefetch=2,
        grid=(nt,),
        in_specs=in_specs,
        out_specs=pl.BlockSpec((tm, D_MODEL), lambda i, te, nu: (tile0 + i, 0)),
        scratch_shapes=[pltpu.VMEM((D_MODEL, 2 * D_EXPERT), BF16), pltpu.VMEM((D_EXPERT, D_MODEL), BF16)],
    )
    return pl.pallas_call(
        functools.partial(_moe_kernel, tile0=tile0),
        grid_spec=grid_spec,
        out_shape=jax.ShapeDtypeStruct((n_tiles * tm, D_MODEL), F32),
        input_output_aliases=aliases,
        compiler_params=_params(("arbitrary",)),
        name="moe",
    )(*args)


def _moe_ln_kernel(x_ref, y1_ref, y2_ref, w_ref, g_ref, b_ref, *rest):
    o_ref = rest[-1]
    w = w_ref[...]
    v = DEEPNORM_ALPHA * x_ref[...] + (w[:, 0:1] * y1_ref[...] + w[:, 1:2] * y2_ref[...])
    o_ref[...] = _layer_norm(v, g_ref[...], b_ref[...])


def _moe_ln_part(x, y12, w_cols, ln_g, ln_b, out_prev, row0, tm):
    t = x.shape[0]
    b0 = row0 // tm
    nb = y12.shape[0] // (2 * tm)
    here = pl.BlockSpec((tm, D_MODEL), lambda i: (i, 0))
    second = pl.BlockSpec((tm, D_MODEL), lambda i: (nb + i, 0))
    there = pl.BlockSpec((tm, D_MODEL), lambda i: (b0 + i, 0))
    vec = pl.BlockSpec((1, D_MODEL), lambda i: (0, 0))
    in_specs = [there, here, second, pl.BlockSpec((tm, SUBLANES), lambda i: (b0 + i, 0)), vec, vec]
    args = [x, y12, y12, w_cols, ln_g, ln_b]
    aliases = {}
    if out_prev is not None:
        in_specs.append(pl.BlockSpec(memory_space=pl.ANY))
        args.append(out_prev)
        aliases = {len(args) - 1: 0}
    return pl.pallas_call(
        _moe_ln_kernel,
        grid=(nb,),
        in_specs=in_specs,
        out_specs=there,
        out_shape=jax.ShapeDtypeStruct((t, D_MODEL), F32),
        input_output_aliases=aliases,
        compiler_params=_params(("parallel",)),
        name="moe_ln",
    )(*args)


def _moe_block(x, eidx, ew, w_gate, w_up, w_down, ln_g, ln_b, tm):
    t = x.shape[0]
    tm = min(tm, t)
    rank, cnt = _rank(eidx, 512)
    counts = cnt[:, 0]
    padded = ((counts + tm - 1) // tm) * tm
    ends = jnp.cumsum(padded)
    starts = ends - padded
    e2 = eidx[:2]
    start_of = jnp.zeros_like(e2)
    for k in range(N_EXPERTS):
        start_of = jnp.where(e2 == k, starts[k], start_of)
    dest = start_of + rank[:2]
    t_pad = 2 * t + N_EXPERTS * tm
    n_tiles = t_pad // tm
    tok = jnp.tile(jnp.arange(t, dtype=jnp.int32), 2)
    src_tok = (jnp.arange(t_pad, dtype=jnp.int32) % t).at[dest.reshape(2 * t)].set(
        tok, mode="promise_in_bounds", unique_indices=True)
    tile_start = jnp.arange(n_tiles, dtype=jnp.int32) * tm
    tile_expert = jnp.minimum(jnp.sum((ends[None, :] <= tile_start[:, None]).astype(jnp.int32), axis=1),
                              N_EXPERTS - 1)
    n_used = (ends[-1] // tm).astype(jnp.int32).reshape(1)
    ys = None
    tiles_per = n_tiles // MOE_PARTS
    for p in range(MOE_PARTS):
        lo = p * tiles_per * tm
        xs_p = x.at[src_tok[lo:lo + tiles_per * tm]].get(mode="promise_in_bounds")
        ys = _moe_part(tile_expert, n_used, xs_p, w_gate, w_up, w_down, ys, p * tiles_per, n_tiles, tm)
    out = None
    rows_per = t // MOE_PARTS
    w_cols = ew.T
    for p in range(MOE_PARTS):
        lo = p * rows_per
        y12 = ys.at[dest[:, lo:lo + rows_per].reshape(2 * rows_per)].get(mode="promise_in_bounds")
        out = _moe_ln_part(x, y12, w_cols, ln_g, ln_b, out, lo, tm)
    return out


def kernel(x, router_w, router_b, l0_w_in, l0_diff_lam_q1, l0_diff_lam_k1, l0_diff_lam_q2, l0_diff_lam_k2, l0_diff_subln_g, l0_gla_w_gate, l0_gla_b_gate, l0_gla_norm_g, l0_w_out, l0_ln1_g, l0_ln1_b, l0_moe_w_gate, l0_moe_w_up, l0_moe_w_down, l0_ln2_g, l0_ln2_b, l1_w_in, l1_s5_a_re, l1_s5_a_im, l1_s5_log_step, l1_s5_b_re, l1_s5_b_im, l1_s5_c_re, l1_s5_c_im, l1_s5_d, l1_s5_w_glu, l1_s5_b_glu, l1_w_out, l1_ln1_g, l1_ln1_b, l1_moe_w_gate, l1_moe_w_up, l1_moe_w_down, l1_ln2_g, l1_ln2_b):
    bsz, t, d = x.shape
    assert bsz == 1 and d == D_MODEL
    x0 = x.reshape(t, d)
    vec = lambda p: p.reshape(1, -1).astype(F32)
    rwt = router_w.T.astype(F32)
    rb = router_b.reshape(N_EXPERTS, 1).astype(F32)

    lambda_init = 0.8 - 0.6 * math.exp(-0.3 * 0)
    diff_w = DIFF_HEADS * DIFF_V_DIM
    gla_qk, gla_v = GLA_HEADS * GLA_DK, GLA_HEADS * GLA_DV
    c = [0]
    for width in (diff_w, diff_w, diff_w, gla_qk, gla_qk, gla_v, GLA_GATE_RANK, gla_v):
        c.append(c[-1] + width)
    q_scale = math.log2(math.e) * DIFF_HEAD_DIM ** -0.5
    w0 = jnp.concatenate([l0_w_in[:, :c[1]] * q_scale, l0_w_in[:, c[1]:c[6]], l0_w_in[:, c[7]:c[8]],
                          l0_w_in[:, c[6]:c[7]], jnp.zeros((d, L0_GB_PAD - GLA_GATE_RANK), l0_w_in.dtype)],
                         axis=1).astype(BF16)
    h0 = _matmul(x0, w0, 1024, L0_IN_TN, BF16)
    lam_vec = jnp.stack([l0_diff_lam_q1, l0_diff_lam_k1, l0_diff_lam_q2, l0_diff_lam_k2]).astype(F32)
    o_a = _diff_attention(h0, lam_vec, vec(l0_diff_subln_g), lambda_init, 1024)
    wg_pad = jnp.concatenate([l0_gla_w_gate, jnp.zeros((LANES - GLA_GATE_RANK, GLA_HEADS * GLA_DK), F32)],
                             axis=0).astype(BF16)
    o_b = _gla(h0, wg_pad, vec(l0_gla_b_gate), vec(l0_gla_norm_g), 256)
    x1, eidx, ew = _outproj_ln_router(o_a, o_b, l0_w_out, x0, vec(l0_ln1_g), vec(l0_ln1_b),
                                      rwt, rb, 512, 128)
    x2 = _moe_block(x1, eidx, ew, l0_moe_w_gate, l0_moe_w_up, l0_moe_w_down, vec(l0_ln2_g), vec(l0_ln2_b), 256)

    sb_w = SB_HEADS * SB_HEAD_DIM
    w1 = jnp.concatenate([l1_w_in[:, :sb_w] * SB_HEAD_DIM ** -0.5, l1_w_in[:, sb_w:]], axis=1).astype(BF16)
    h1 = _matmul(x2, w1, 1024, 1024, BF16)
    o_c = _stick_breaking(h1, 256)
    lbr, lbi, bbr, bbi = _s5_prep(l1_s5_a_re, l1_s5_a_im, l1_s5_log_step, l1_s5_b_re, l1_s5_b_im)
    bd, cd, pw8 = _s5_tile_params(lbr[::S5_GROUP], lbi[::S5_GROUP], bbr.reshape(S5_GROUPS, S5_GROUP, S5_STATE),
                                  bbi.reshape(S5_GROUPS, S5_GROUP, S5_STATE),
                                  l1_s5_c_re.astype(F32), l1_s5_c_im.astype(F32))
    y_s5 = _s5_core(h1, bd, cd, pw8, 256)
    o_d = _s5_glu(y_s5, h1, vec(l1_s5_d), l1_s5_w_glu.astype(BF16), vec(l1_s5_b_glu), 512)
    x3, eidx1, ew1 = _outproj_ln_router(o_c, o_d, l1_w_out, x2, vec(l1_ln1_g), vec(l1_ln1_b),
                                        rwt, rb, 512, 128)
    x4 = _moe_block(x3, eidx1, ew1, l1_moe_w_gate, l1_moe_w_up, l1_moe_w_down, vec(l1_ln2_g), vec(l1_ln2_b), 256)
    return x4.reshape(bsz, t, d)
```

```python
import functools
import math

import jax
import jax.numpy as jnp
from jax import lax
from jax.experimental import pallas as pl
from jax.experimental.pallas import tpu as pltpu

F32 = jnp.float32
BF16 = jnp.bfloat16

D_MODEL = 2048
DEPTH = 2
DIFF_HEADS = 8
DIFF_HEAD_DIM = 64
DIFF_V_DIM = 128
GLA_HEADS = 4
GLA_DK = 128
GLA_DV = 256
GLA_GATE_RANK = 16
GLA_GATE_TEMP = 16.0
GLA_CHUNK = 64
SB_HEADS = 8
SB_HEAD_DIM = 128
S5_CHANNELS = 1024
S5_GROUP = 16
S5_GROUPS = 64
S5_STATE = 64
N_EXPERTS = 16
N_GROUPS = 4
EXPERTS_PER_GROUP = 4
D_EXPERT = 640
DEEPNORM_ALPHA = (2.0 * DEPTH) ** 0.25
LN_EPS = 1e-5

LANES = 128
SUBLANES = 8
VMEM_LIMIT = 56 * 1024 * 1024
SB_LOG_FLOOR = 120.0
SB_HEADS_PER_STEP = 4

MXU_N = 256
L0_GB_PAD = MXU_N
L0_COL = {"qa": 0, "ka": 1024, "va": 2048, "qb": 3072, "kb": 3584, "vb": 4096, "rb": 5120, "gb": 6144}
L0_IN_TN = 5 * MXU_N

S5_KT = 4
S5_KT_CH = S5_CHANNELS // S5_KT
S5_KT_STATES = (S5_GROUPS // S5_KT) * S5_STATE
S5_SCAN_W = 512
S5_L = 8


def _params(sem, flags=None):
    return pltpu.CompilerParams(dimension_semantics=sem, vmem_limit_bytes=VMEM_LIMIT, flags=flags)


def _nt_dot(a, b):
    return lax.dot_general(a, b, (((1,), (1,)), ((), ())), preferred_element_type=F32)


def _dot(a, b):
    return jnp.dot(a, b, preferred_element_type=F32)


def _split_dot_left(m, x):
    hi = x.astype(BF16)
    lo = (x - hi.astype(F32)).astype(BF16)
    return _dot(m, hi) + _dot(m, lo)


def _layer_norm(v, g, b):
    mu = jnp.mean(v, axis=-1, keepdims=True)
    d = v - mu
    var = jnp.mean(d * d, axis=-1, keepdims=True)
    return d * lax.rsqrt(var + LN_EPS) * g + b


def _mm_kernel(a_ref, b_ref, o_ref, a_bf):
    @pl.when(pl.program_id(1) == 0)
    def _():
        a_bf[...] = a_ref[...].astype(BF16)

    o_ref[...] = _dot(a_bf[...], b_ref[...]).astype(o_ref.dtype)


def _matmul(a, b, tm, tn, out_dtype):
    m, k = a.shape
    n = b.shape[1]
    tm = min(tm, m)
    return pl.pallas_call(
        _mm_kernel,
        grid=(m // tm, n // tn),
        in_specs=[pl.BlockSpec((tm, k), lambda i, j: (i, 0)),
                  pl.BlockSpec((k, tn), lambda i, j: (0, j))],
        out_specs=pl.BlockSpec((tm, tn), lambda i, j: (i, j)),
        out_shape=jax.ShapeDtypeStruct((m, n), out_dtype),
        scratch_shapes=[pltpu.VMEM((tm, k), BF16)],
        compiler_params=_params(("parallel", "arbitrary")),
        name="in_proj",
    )(a, b)


def _diff_attn_kernel(lam_ref, g_ref, q_ref, k_ref, v_ref, o_ref, vx_ref, m_ref, acc_ref, sa_ref, sb_ref, *, tq,
                      lambda_init):
    i = pl.program_id(1)
    dv = DIFF_V_DIM

    @pl.when(i == 0)
    def _():
        vx_ref[:, :dv] = v_ref[...]
        vx_ref[:, dv:] = jnp.ones((vx_ref.shape[0], dv), BF16)

    m_ref[...] = jnp.full(m_ref.shape, -jnp.inf, F32)
    acc_ref[...] = jnp.zeros(acc_ref.shape, F32)
    q = q_ref[...]
    lane = lax.broadcasted_iota(jnp.int32, q.shape, 1)
    zero = jnp.zeros_like(q)
    halves = (jnp.where(lane < DIFF_HEAD_DIM, q, zero), jnp.where(lane >= DIFF_HEAD_DIM, q, zero))

    def scores(j, mi):
        off = pl.multiple_of(j * tq, tq)
        return _nt_dot(halves[mi], k_ref[pl.ds(off, tq), :])

    def softmax_pv(j, get_s, masked):
        off = pl.multiple_of(j * tq, tq)
        vx = vx_ref[pl.ds(off, tq), :]
        if masked:
            row = lax.broadcasted_iota(jnp.int32, (tq, tq), 0)
            col = lax.broadcasted_iota(jnp.int32, (tq, tq), 1)
            causal = col <= row
        for mi in range(2):
            s = get_s(mi)
            if masked:
                s = jnp.where(causal, s, -jnp.inf)
            m_prev = m_ref[mi]
            m_new = jnp.maximum(m_prev, jnp.max(s, axis=1, keepdims=True))
            p = jnp.exp2(s - jnp.tile(m_new, (1, tq // LANES)))
            alpha = jnp.exp2(m_prev - m_new)
            acc_ref[mi] = jnp.tile(alpha, (1, 2)) * acc_ref[mi] + _dot(p.astype(BF16), vx)
            m_ref[mi] = m_new

    def scores_to(j, s_ref):
        for mi in range(2):
            s_ref[mi] = scores(j, mi)

    scores_to(0, sa_ref)

    def pair(t, carry):
        j = 2 * t
        scores_to(j + 1, sb_ref)
        softmax_pv(j, lambda mi: sa_ref[mi], False)
        scores_to(j + 2, sa_ref)
        softmax_pv(j + 1, lambda mi: sb_ref[mi], False)
        return carry

    lax.fori_loop(0, i // 2, pair, 0)

    @pl.when(i % 2 == 1)
    def _():
        softmax_pv(i - 1, lambda mi: sa_ref[mi], False)
        scores_to(i, sa_ref)

    softmax_pv(i, lambda mi: sa_ref[mi], True)

    lv = lam_ref[...]
    lam = (jnp.exp(jnp.sum(lv[0:1] * lv[1:2], axis=1, keepdims=True))
           - jnp.exp(jnp.sum(lv[2:3] * lv[3:4], axis=1, keepdims=True)) + lambda_init)
    a0 = acc_ref[0]
    a1 = acc_ref[1]
    o = a0[:, :dv] / a0[:, dv:] - lam * (a1[:, :dv] / a1[:, dv:])
    ms = jnp.mean(o * o, axis=-1, keepdims=True)
    o = o * lax.rsqrt(ms + LN_EPS) * g_ref[...] * (1.0 - lambda_init)
    o_ref[...] = o.astype(o_ref.dtype)


def _diff_attention(h0, lam_vec, subln_g, lambda_init, tq):
    t = h0.shape[0]
    tq = min(tq, t)
    nq = t // tq
    kern = functools.partial(_diff_attn_kernel, tq=tq, lambda_init=lambda_init)
    return pl.pallas_call(
        kern,
        grid=(DIFF_HEADS, nq),
        in_specs=[
            pl.BlockSpec((4, DIFF_HEAD_DIM), lambda h, i: (0, 0)),
            pl.BlockSpec((1, DIFF_V_DIM), lambda h, i: (0, 0)),
            pl.BlockSpec((tq, LANES), lambda h, i: (i, h)),
            pl.BlockSpec((t, LANES), lambda h, i: (0, DIFF_HEADS + h)),
            pl.BlockSpec((t, LANES), lambda h, i: (0, 2 * DIFF_HEADS + h)),
        ],
        out_specs=pl.BlockSpec((tq, LANES), lambda h, i: (i, h)),
        out_shape=jax.ShapeDtypeStruct((t, DIFF_HEADS * DIFF_V_DIM), BF16),
        scratch_shapes=[pltpu.VMEM((t, 2 * DIFF_V_DIM), BF16), pltpu.VMEM((2, tq, LANES), F32),
                        pltpu.VMEM((2, tq, 2 * DIFF_V_DIM), F32),
                        pltpu.VMEM((2, tq, tq), F32), pltpu.VMEM((2, tq, tq), F32)],
        compiler_params=_params(("arbitrary", "arbitrary")),
        name="diff_attn",
    )(lam_vec, subln_g, h0, h0, h0)


def _gla_kernel(q_ref, k_ref, v_ref, r_ref, gb_ref, wg_ref, bg_ref, ng_ref, o_ref, st_ref, *, tg):
    @pl.when(pl.program_id(0) == 0)
    def _():
        st_ref[...] = jnp.zeros(st_ref.shape, F32)

    c = GLA_CHUNK
    gate = _dot(gb_ref[...], wg_ref[...]) + bg_ref[...]
    log_a = -(jnp.maximum(-gate, 0.0) + jnp.log(1.0 + jnp.exp(-jnp.abs(gate)))) / GLA_GATE_TEMP
    row = lax.broadcasted_iota(jnp.int32, (c, c), 0)
    col = lax.broadcasted_iota(jnp.int32, (c, c), 1)
    tril = row >= col
    tri = jnp.where(tril, 1.0, 0.0).astype(BF16)
    for h in range(GLA_HEADS):
        ksl = slice(h * GLA_DK, (h + 1) * GLA_DK)
        vsl = slice(h * GLA_DV, (h + 1) * GLA_DV)
        for ci in range(tg // c):
            rs = slice(ci * c, (ci + 1) * c)
            la = log_a[rs, ksl]
            b = _split_dot_left(tri, la)
            b_last = b[c - 1:c, :]
            qf = q_ref[rs, ksl].astype(F32) * (GLA_DK ** -0.5)
            kf = k_ref[rs, ksl].astype(F32)
            v = v_ref[rs, vsl]
            q_dec = (qf * jnp.exp(b)).astype(BF16)
            k_dec = (kf * jnp.exp(-b)).astype(BF16)
            k_end = (kf * jnp.exp(b_last - b)).astype(BF16)
            scores = jnp.where(tril, _nt_dot(q_dec, k_dec), 0.0)
            st = st_ref[h]
            o = _dot(scores.astype(BF16), v) + _nt_dot(q_dec, st.astype(BF16))
            kv_t = lax.dot_general(v, k_end, (((0,), (0,)), ((), ())), preferred_element_type=F32)
            st_ref[h] = jnp.exp(b_last) * st + kv_t
            ms = jnp.mean(o * o, axis=-1, keepdims=True)
            o = o * lax.rsqrt(ms + LN_EPS) * ng_ref[...]
            r = r_ref[rs, vsl].astype(F32)
            o_ref[rs, vsl] = (o * (r / (1.0 + jnp.exp(-r)))).astype(o_ref.dtype)


def _gla(h0, wg_pad, b_gate, norm_g, tg):
    t = h0.shape[0]
    tg = min(tg, t)
    qk = GLA_HEADS * GLA_DK
    vw = GLA_HEADS * GLA_DV
    return pl.pallas_call(
        functools.partial(_gla_kernel, tg=tg),
        grid=(t // tg,),
        in_specs=[
            pl.BlockSpec((tg, qk), lambda i: (i, L0_COL["qb"] // qk)),
            pl.BlockSpec((tg, qk), lambda i: (i, L0_COL["kb"] // qk)),
            pl.BlockSpec((tg, vw), lambda i: (i, L0_COL["vb"] // vw)),
            pl.BlockSpec((tg, vw), lambda i: (i, L0_COL["rb"] // vw)),
            pl.BlockSpec((tg, LANES), lambda i: (i, L0_COL["gb"] // LANES)),
            pl.BlockSpec((LANES, qk), lambda i: (0, 0)),
            pl.BlockSpec((1, qk), lambda i: (0, 0)),
            pl.BlockSpec((1, GLA_DV), lambda i: (0, 0)),
        ],
        out_specs=pl.BlockSpec((tg, vw), lambda i: (i, 0)),
        out_shape=jax.ShapeDtypeStruct((t, vw), BF16),
        scratch_shapes=[pltpu.VMEM((GLA_HEADS, GLA_DV, GLA_DK), F32)],
        compiler_params=_params(("arbitrary",)),
        name="gla",
    )(h0, h0, h0, h0, h0, wg_pad, b_gate, norm_g)


def _sb_kernel(q_ref, k_ref, v_ref, o_ref, acc_ref, c_ref, *, tq):
    i = pl.program_id(1)
    hd = SB_HEAD_DIM
    acc_ref[...] = jnp.zeros(acc_ref.shape, F32)
    c_ref[...] = jnp.zeros(c_ref.shape, F32)
    row = lax.broadcasted_iota(jnp.int32, (tq, tq), 0)
    col = lax.broadcasted_iota(jnp.int32, (tq, tq), 1)
    strict = col < row
    tri = jnp.where(row > col, 1.0, 0.0).astype(BF16)

    def block(jb, masked):
        off = pl.multiple_of(jb * tq, tq)
        cmax = None
        for hh in range(SB_HEADS_PER_STEP):
            cols = slice(hh * hd, (hh + 1) * hd)
            k = k_ref[pl.ds(off, tq), cols]
            v = v_ref[pl.ds(off, tq), cols]
            z = _nt_dot(q_ref[:, cols], k)
            lsn = -(jnp.maximum(z, 0.0) + jnp.log(1.0 + jnp.exp(-jnp.abs(z))))
            lm = jnp.where(strict, lsn, 0.0) if masked else lsn
            excl = _dot(lm.astype(BF16), tri) + c_ref[hh]
            w = jnp.exp(z + lsn + excl)
            if masked:
                w = jnp.where(strict, w, 0.0)
            acc_ref[:, cols] += _dot(w.astype(BF16), v)
            c_new = c_ref[hh] + jnp.sum(lm, axis=1, keepdims=True)
            c_ref[hh] = c_new
            hmax = jnp.max(c_new)
            cmax = hmax if cmax is None else jnp.maximum(cmax, hmax)
        return cmax

    cmax = block(i, True)

    def cond(carry):
        jb, cm = carry
        return jnp.logical_and(jb >= 0, cm > -SB_LOG_FLOOR)

    def body(carry):
        jb, _ = carry
        return jb - 1, block(jb, False)

    lax.while_loop(cond, body, (i - 1, cmax))
    o_ref[...] = acc_ref[...].astype(o_ref.dtype)


def _stick_breaking(h1, tq):
    t = h1.shape[0]
    tq = min(tq, t)
    ng = SB_HEADS // SB_HEADS_PER_STEP
    wd = SB_HEADS_PER_STEP * SB_HEAD_DIM
    return pl.pallas_call(
        functools.partial(_sb_kernel, tq=tq),
        grid=(ng, t // tq),
        in_specs=[
            pl.BlockSpec((tq, wd), lambda h, i: (i, h)),
            pl.BlockSpec((t, wd), lambda h, i: (0, ng + h), pipeline_mode=pl.Buffered(1)),
            pl.BlockSpec((t, wd), lambda h, i: (0, 2 * ng + h), pipeline_mode=pl.Buffered(1)),
        ],
        out_specs=pl.BlockSpec((tq, wd), lambda h, i: (i, h)),
        out_shape=jax.ShapeDtypeStruct((t, SB_HEADS * SB_HEAD_DIM), BF16),
        scratch_shapes=[pltpu.VMEM((tq, wd), F32), pltpu.VMEM((SB_HEADS_PER_STEP, tq, 1), F32)],
        compiler_params=_params(("parallel", "arbitrary")),
        name="stick_breaking",
    )(h1, h1, h1)


def _s5_prep_kernel(are_ref, aim_ref, ls_ref, bre_ref, bim_ref, lbr_ref, lbi_ref, bbr_ref, bbi_ref):
    step = jnp.exp(ls_ref[...])
    lam_re = are_ref[...]
    lam_im = aim_ref[...]
    mag = jnp.exp(step * lam_re)
    lb_re = mag * jnp.cos(step * lam_im)
    lb_im = mag * jnp.sin(step * lam_im)
    den = lam_re * lam_re + lam_im * lam_im
    n_re = lb_re - 1.0
    f_re = (n_re * lam_re + lb_im * lam_im) / den
    f_im = (lb_im * lam_re - n_re * lam_im) / den
    lbr_ref[...] = lb_re
    lbi_ref[...] = lb_im
    bbr_ref[...] = f_re * bre_ref[...] - f_im * bim_ref[...]
    bbi_ref[...] = f_re * bim_ref[...] + f_im * bre_ref[...]


def _s5_prep(a_re, a_im, log_step, b_re, b_im):
    rows = S5_GROUPS * S5_GROUP
    rep = lambda p: jnp.repeat(p, S5_GROUP, axis=0)
    to_rows = lambda b: jnp.transpose(b, (0, 2, 1)).reshape(rows, S5_STATE)
    spec = pl.BlockSpec((rows, S5_STATE), lambda: (0, 0))
    shp = jax.ShapeDtypeStruct((rows, S5_STATE), F32)
    return pl.pallas_call(
        _s5_prep_kernel,
        in_specs=[spec, spec, pl.BlockSpec((rows, 1), lambda: (0, 0)), spec, spec],
        out_specs=[spec] * 4,
        out_shape=[shp] * 4,
        name="s5_prep",
    )(rep(a_re), rep(a_im), rep(log_step[:, None]), to_rows(b_re), to_rows(b_im))


def _cmul(ar, ai, br, bi):
    return ar * br - ai * bi, ar * bi + ai * br


def _block_diag_tiles(m, rows_per_group, cols_per_group):
    gpt = S5_GROUPS // S5_KT
    m = m.reshape(S5_KT, gpt, rows_per_group, cols_per_group)
    eye = jnp.eye(gpt, dtype=m.dtype)
    bd = jnp.einsum("kgrc,gh->kgrhc", m, eye)
    return bd.reshape(S5_KT, gpt * rows_per_group, gpt * cols_per_group)


def _s5_tile_params(lb_re, lb_im, bb_re, bb_im, c_re, c_im):
    bd = jnp.concatenate([_block_diag_tiles(bb_re, S5_GROUP, S5_STATE),
                          _block_diag_tiles(bb_im, S5_GROUP, S5_STATE)], axis=2)
    cd = jnp.concatenate([_block_diag_tiles(jnp.transpose(c_re, (0, 2, 1)), S5_STATE, S5_GROUP),
                          _block_diag_tiles(-jnp.transpose(c_im, (0, 2, 1)), S5_STATE, S5_GROUP)], axis=1)
    pr = [jnp.ones_like(lb_re)]
    pi = [jnp.zeros_like(lb_re)]
    for _ in range(S5_L):
        nr, ni = _cmul(pr[-1], pi[-1], lb_re, lb_im)
        pr.append(nr)
        pi.append(ni)

    def table(p):
        rows = jnp.transpose(jnp.stack(p).reshape(S5_L + 1, S5_KT, S5_KT_STATES), (1, 0, 2))
        return jnp.pad(rows, ((0, 0), (0, LANES - (S5_L + 1)), (0, 0)))

    return bd, cd, jnp.stack([table(pr), table(pi)], axis=1)


def _s5_core_kernel(u_ref, bd_ref, cd_ref, pw8_ref, y_ref, hs_ref, hp_ref, pw_ref, st_ref, uf_ref,
                    mt_s, bbs_s, ccs_s, *, tcr):
    ns = S5_KT_STATES
    pad = SUBLANES
    cw = S5_KT_CH

    @pl.when(pl.program_id(1) == 0)
    def _():
        b_re = bd_ref[:, :ns]
        b_im = bd_ref[:, ns:]
        cd_hi = cd_ref[...].astype(BF16)
        cd_lo = (cd_ref[...] - cd_hi.astype(F32)).astype(BF16)
        for k in range(S5_L):
            a = pw8_ref[0, k:k + 1, :]
            b = pw8_ref[1, k:k + 1, :]
            dk = jnp.concatenate([b_re * a - b_im * b, b_re * b + b_im * a], axis=1)
            s = S5_L - 1 - k
            bbs_s[s * cw:(s + 1) * cw, :] = dk.astype(BF16)
            dk_hi = dk.astype(BF16)
            dk_lo = (dk - dk_hi.astype(F32)).astype(BF16)
            kk = (_dot(dk_hi, cd_hi) + (_dot(dk_lo, cd_hi) + _dot(dk_hi, cd_lo))).astype(BF16)
            for s0 in range(S5_L - k):
                mt_s[s0 * cw:(s0 + 1) * cw, (s0 + k) * cw:(s0 + k + 1) * cw] = kk
        pw_t = (jnp.transpose(pw8_ref[0]), jnp.transpose(pw8_ref[1]))
        c_re = cd_ref[:ns, :]
        c_im = cd_ref[ns:, :]
        for i in range(S5_L):
            a = pw_t[0][:, i + 1:i + 2]
            b = pw_t[1][:, i + 1:i + 2]
            ccs_s[:ns, i * cw:(i + 1) * cw] = (c_re * a + c_im * b).astype(BF16)
            ccs_s[ns:, i * cw:(i + 1) * cw] = (c_im * a - c_re * b).astype(BF16)

        hp_ref[...] = jnp.zeros(hp_ref.shape, F32)
        hs_ref[0:pad, :] = jnp.zeros((pad, 2 * ns), F32)
        l1r = jnp.broadcast_to(pw8_ref[0, S5_L:S5_L + 1, :], (SUBLANES, ns))
        l1i = jnp.broadcast_to(pw8_ref[1, S5_L:S5_L + 1, :], (SUBLANES, ns))
        l2r, l2i = _cmul(l1r, l1i, l1r, l1i)
        l4r, l4i = _cmul(l2r, l2i, l2r, l2i)
        l8r, l8i = _cmul(l4r, l4i, l4r, l4i)
        ridx = lax.broadcasted_iota(jnp.int32, l1r.shape, 0)
        n = ridx + 1
        pr = jnp.ones_like(l1r)
        pi = jnp.zeros_like(l1r)
        for bit, (lr, li) in ((1, (l1r, l1i)), (2, (l2r, l2i)), (4, (l4r, l4i)), (8, (l8r, l8i))):
            nr, ni = _cmul(pr, pi, lr, li)
            take = (n & bit) != 0
            pr = jnp.where(take, nr, pr)
            pi = jnp.where(take, ni, pi)
        pw_ref[0] = pr
        pw_ref[1] = pi
        for s, (lr, li) in enumerate(((l1r, l1i), (l2r, l2i), (l4r, l4i))):
            keep = ridx >= (1 << s)
            st_ref[2 * s] = jnp.where(keep, lr, 0.0)
            st_ref[2 * s + 1] = jnp.where(keep, li, 0.0)

    nl = S5_KT_CH // LANES
    for h in range(nl):
        uf_ref[h] = u_ref[:, h * LANES:(h + 1) * LANES].astype(F32)
    uc = jnp.concatenate([uf_ref[h, pl.ds(s, tcr, stride=S5_L), :].astype(BF16)
                          for s in range(S5_L) for h in range(nl)], axis=1)
    hs_ref[pad:pad + tcr, :] = _dot(uc, bbs_s[...])
    w = S5_SCAN_W
    for cc in range(ns // w):
        re0 = cc * w
        im0 = ns + cc * w

        def body(r, carry, re0=re0, im0=im0):
            hpr, hpi = carry
            rows = pl.ds(pl.multiple_of(pad + r * SUBLANES, SUBLANES), SUBLANES)
            xr = hs_ref[rows, re0:re0 + w]
            xi = hs_ref[rows, im0:im0 + w]
            for s in range(3):
                sr = pltpu.roll(xr, 1 << s, 0)
                si = pltpu.roll(xi, 1 << s, 0)
                ar = st_ref[2 * s, :, re0:re0 + w]
                ai = st_ref[2 * s + 1, :, re0:re0 + w]
                xr, xi = xr + ar * sr - ai * si, xi + ar * si + ai * sr
            pr = pw_ref[0, :, re0:re0 + w]
            pi = pw_ref[1, :, re0:re0 + w]
            xr, xi = xr + pr * hpr - pi * hpi, xi + pr * hpi + pi * hpr
            hs_ref[rows, re0:re0 + w] = xr
            hs_ref[rows, im0:im0 + w] = xi
            return (jnp.broadcast_to(xr[SUBLANES - 1:SUBLANES, :], (SUBLANES, w)),
                    jnp.broadcast_to(xi[SUBLANES - 1:SUBLANES, :], (SUBLANES, w)))

        hpr, hpi = lax.fori_loop(0, tcr // SUBLANES, body, (hp_ref[:, re0:re0 + w], hp_ref[:, im0:im0 + w]))
        hp_ref[:, re0:re0 + w] = hpr
        hp_ref[:, im0:im0 + w] = hpi

    h_prev = hs_ref[pad - 1:pad - 1 + tcr, :]
    y_in = _dot(h_prev.astype(BF16), ccs_s[...])
    for i in range(S5_L):
        k_hi = (i + 1) * cw
        y_i = y_in[:, i * cw:k_hi] + _dot(uc[:, :k_hi], mt_s[:k_hi, i * cw:k_hi])
        for h in range(nl):
            uf_ref[h, pl.ds(i, tcr, stride=S5_L), :] = y_i[:, h * LANES:(h + 1) * LANES]
    for h in range(nl):
        y_ref[:, h * LANES:(h + 1) * LANES] = uf_ref[h]
    hs_ref[pad - 1:pad, :] = hs_ref[pad + tcr - 1:pad + tcr, :]


def _s5_core(h1, bd, cd, pw8, tcr):
    t = h1.shape[0]
    width = S5_L * S5_KT_CH
    tcr = min(tcr, t // S5_L)
    tc = tcr * S5_L
    ns = S5_KT_STATES
    u_col0 = 3 * S5_CHANNELS // S5_KT_CH
    once = pl.Buffered(1)
    return pl.pallas_call(
        functools.partial(_s5_core_kernel, tcr=tcr),
        grid=(S5_KT, t // tc),
        in_specs=[
            pl.BlockSpec((tc, S5_KT_CH), lambda k, i: (i, u_col0 + k)),
            pl.BlockSpec((None, S5_KT_CH, 2 * ns), lambda k, i: (k, 0, 0), pipeline_mode=once),
            pl.BlockSpec((None, 2 * ns, S5_KT_CH), lambda k, i: (k, 0, 0), pipeline_mode=once),
            pl.BlockSpec((None, 2, LANES, ns), lambda k, i: (k, 0, 0, 0), pipeline_mode=once),
        ],
        out_specs=pl.BlockSpec((tc, S5_KT_CH), lambda k, i: (i, k)),
        out_shape=jax.ShapeDtypeStruct((t, S5_CHANNELS), F32),
        scratch_shapes=[pltpu.VMEM((tcr + SUBLANES, 2 * ns), F32), pltpu.VMEM((SUBLANES, 2 * ns), F32),
                        pltpu.VMEM((2, SUBLANES, ns), F32), pltpu.VMEM((6, SUBLANES, ns), F32),
                        pltpu.VMEM((S5_KT_CH // LANES, tc, LANES), F32),
                        pltpu.VMEM((width, width), BF16), pltpu.VMEM((width, 2 * ns), BF16),
                        pltpu.VMEM((2 * ns, width), BF16)],
        compiler_params=_params(("arbitrary", "arbitrary")),
        name="s5_core",
    )(h1, bd, cd, pw8)


def _s5_glu_kernel(y_ref, u_ref, d_ref, wglu_ref, bglu_ref, o_ref):
    y = y_ref[...] + d_ref[...] * u_ref[...].astype(F32)
    g = 0.5 * y * (1.0 + jnp.tanh(math.sqrt(2.0 / math.pi) * (y + 0.044715 * (y * y * y))))
    gl = _dot(g.astype(BF16), wglu_ref[...]) + bglu_ref[...]
    o_ref[...] = (g / (1.0 + jnp.exp(-gl))).astype(o_ref.dtype)


def _s5_glu(y, h1, d_skip, w_glu, b_glu, tm):
    t = h1.shape[0]
    tm = min(tm, t)
    full = lambda shape: pl.BlockSpec(shape, lambda i: (0,) * len(shape))
    return pl.pallas_call(
        _s5_glu_kernel,
        grid=(t // tm,),
        in_specs=[pl.BlockSpec((tm, S5_CHANNELS), lambda i: (i, 0)),
                  pl.BlockSpec((tm, S5_CHANNELS), lambda i: (i, 3)),
                  full((1, S5_CHANNELS)), full((S5_CHANNELS, S5_CHANNELS)), full((1, S5_CHANNELS))],
        out_specs=pl.BlockSpec((tm, S5_CHANNELS), lambda i: (i, 0)),
        out_shape=jax.ShapeDtypeStruct((t, S5_CHANNELS), BF16),
        compiler_params=_params(("parallel",)),
        name="s5_glu",
    )(y, h1, d_skip, w_glu, b_glu)


def _route(probs, sel):
    rows = [sel[e:e + 1, :] for e in range(N_EXPERTS)]
    prow = [probs[e:e + 1, :] for e in range(N_EXPERTS)]
    best_score = None
    for g in range(N_GROUPS):
        a = rows[g * EXPERTS_PER_GROUP:(g + 1) * EXPERTS_PER_GROUP]
        score = None
        for x in range(EXPERTS_PER_GROUP):
            for y in range(x + 1, EXPERTS_PER_GROUP):
                pair = a[x] + a[y]
                score = pair if score is None else jnp.maximum(score, pair)
        if best_score is None:
            best_score, gbest = score, jnp.zeros(score.shape, jnp.int32)
            cs = list(a)
            cp = prow[0:EXPERTS_PER_GROUP]
        else:
            better = score > best_score
            best_score = jnp.where(better, score, best_score)
            gbest = jnp.where(better, g, gbest)
            cs = [jnp.where(better, a[x], cs[x]) for x in range(EXPERTS_PER_GROUP)]
            cp = [jnp.where(better, prow[g * EXPERTS_PER_GROUP + x], cp[x]) for x in range(EXPERTS_PER_GROUP)]

    def first_argmax(vals):
        bv, bi = vals[0], jnp.zeros(vals[0].shape, jnp.int32)
        for x in range(1, len(vals)):
            better = vals[x] > bv
            bv = jnp.where(better, vals[x], bv)
            bi = jnp.where(better, x, bi)
        return bi

    i1 = first_argmax(cs)
    i2 = first_argmax([jnp.where(i1 == x, -jnp.inf, cs[x]) for x in range(EXPERTS_PER_GROUP)])

    def pick(idx):
        out = cp[0]
        for x in range(1, EXPERTS_PER_GROUP):
            out = jnp.where(idx == x, cp[x], out)
        return out

    p1, p2 = pick(i1), pick(i2)
    tot = p1 + p2
    return gbest * EXPERTS_PER_GROUP + i1, gbest * EXPERTS_PER_GROUP + i2, p1 / tot, p2 / tot


def _ln_router_tail(v, g_ref, b_ref, rwh_ref, rwl_ref, rb_ref):
    xn = _layer_norm(v, g_ref[...], b_ref[...])
    xh = xn.astype(BF16)
    xl = (xn - xh.astype(F32)).astype(BF16)
    logits = _nt_dot(rwh_ref[...], xh) + (_nt_dot(rwl_ref[...], xh) + _nt_dot(rwh_ref[...], xl))
    mx = jnp.max(logits, axis=0, keepdims=True)
    ex = jnp.exp(logits - mx)
    probs = ex / jnp.sum(ex, axis=0, keepdims=True)
    e1, e2, w1, w2 = _route(probs, probs + rb_ref[...])
    n = e1.shape[1]
    ei = jnp.concatenate([e1, e2, jnp.zeros((SUBLANES - 2, n), jnp.int32)], axis=0)
    ew = jnp.concatenate([w1, w2, jnp.zeros((SUBLANES - 2, n), F32)], axis=0)
    return xn, ei, ew


def _outproj_kernel(ma_ref, mb_ref, w_ref, xr_ref, g_ref, b_ref, rwh_ref, rwl_ref, rb_ref,
                    x_ref, ei_ref, ew_ref, w_bf, *, sub):
    @pl.when(pl.program_id(0) == 0)
    def _():
        w_bf[...] = w_ref[...].astype(BF16)

    half = w_bf.shape[0] // 2
    for s in range(x_ref.shape[0] // sub):
        rows = slice(s * sub, (s + 1) * sub)
        v = (DEEPNORM_ALPHA * xr_ref[rows, :] + _dot(ma_ref[rows, :], w_bf[:half, :])
             + _dot(mb_ref[rows, :], w_bf[half:, :]))
        xn, ei, ew = _ln_router_tail(v, g_ref, b_ref, rwh_ref, rwl_ref, rb_ref)
        x_ref[rows, :] = xn
        ei_ref[:, rows] = ei
        ew_ref[:, rows] = ew


def _outproj_ln_router(mix_a, mix_b, w_out, x_res, ln_g, ln_b, rwt, rb, tm, sub):
    t = x_res.shape[0]
    rwt_hi = rwt.astype(BF16)
    rwt_lo = (rwt - rwt_hi.astype(F32)).astype(BF16)
    tm = min(tm, t)
    sub = min(sub, tm)
    half = w_out.shape[0] // 2
    row = lambda w: pl.BlockSpec((tm, w), lambda i: (i, 0))
    full = lambda shape: pl.BlockSpec(shape, lambda i: (0,) * len(shape))
    return pl.pallas_call(
        functools.partial(_outproj_kernel, sub=sub),
        grid=(t // tm,),
        in_specs=[row(half), row(half),
                  pl.BlockSpec((2 * half, D_MODEL), lambda i: (0, 0), pipeline_mode=pl.Buffered(1)),
                  row(D_MODEL), full((1, D_MODEL)), full((1, D_MODEL)),
                  full((N_EXPERTS, D_MODEL)), full((N_EXPERTS, D_MODEL)), full((N_EXPERTS, 1))],
        out_specs=[row(D_MODEL),
                   pl.BlockSpec((SUBLANES, tm), lambda i: (0, i)), pl.BlockSpec((SUBLANES, tm), lambda i: (0, i))],
        out_shape=[jax.ShapeDtypeStruct((t, D_MODEL), F32),
                   jax.ShapeDtypeStruct((SUBLANES, t), jnp.int32), jax.ShapeDtypeStruct((SUBLANES, t), F32)],
        scratch_shapes=[pltpu.VMEM((2 * half, D_MODEL), BF16)],
        compiler_params=_params(("arbitrary",)),
        name="outproj_ln_router",
    )(mix_a, mix_b, w_out, x_res, ln_g, ln_b, rwt_hi, rwt_lo, rb)


def _rank_kernel(e_ref, rank_ref, cnt_ref, base_ref, *, tn):
    @pl.when(pl.program_id(0) == 0)
    def _():
        base_ref[...] = jnp.zeros(base_ref.shape, F32)

    e = e_ref[...]
    rows = lax.broadcasted_iota(jnp.int32, (N_EXPERTS, tn), 0)
    oh1 = rows == e[0:1, :]
    oh2 = rows == e[1:2, :]
    oh = jnp.where(jnp.logical_or(oh1, oh2), 1.0, 0.0)
    r = lax.broadcasted_iota(jnp.int32, (tn, tn), 0)
    c = lax.broadcasted_iota(jnp.int32, (tn, tn), 1)
    earlier = jnp.where(r < c, 1.0, 0.0).astype(BF16)
    base = base_ref[...]
    before = _dot(oh.astype(BF16), earlier) + jnp.tile(base, (1, tn // LANES))
    r1 = jnp.sum(jnp.where(oh1, before, 0.0), axis=0, keepdims=True)
    r2 = jnp.sum(jnp.where(oh2, before, 0.0), axis=0, keepdims=True)
    rank_ref[...] = jnp.concatenate([r1, r2, jnp.zeros((SUBLANES - 2, tn), F32)], axis=0).astype(jnp.int32)
    base = base + jnp.sum(oh, axis=1, keepdims=True)
    base_ref[...] = base
    cnt_ref[...] = base.astype(jnp.int32)


def _rank(eidx, tn):
    t = eidx.shape[1]
    tn = min(tn, t)
    return pl.pallas_call(
        functools.partial(_rank_kernel, tn=tn),
        grid=(t // tn,),
        in_specs=[pl.BlockSpec((SUBLANES, tn), lambda i: (0, i))],
        out_specs=[pl.BlockSpec((SUBLANES, tn), lambda i: (0, i)),
                   pl.BlockSpec((N_EXPERTS, LANES), lambda i: (0, 0))],
        out_shape=[jax.ShapeDtypeStruct((SUBLANES, t), jnp.int32),
                   jax.ShapeDtypeStruct((N_EXPERTS, LANES), jnp.int32)],
        scratch_shapes=[pltpu.VMEM((N_EXPERTS, LANES), F32)],
        compiler_params=_params(("arbitrary",)),
        name="moe_rank",
    )(eidx)


def _moe_kernel(te_ref, nu_ref, xs_ref, wg_ref, wu_ref, wd_ref, y_ref, wgu_s, wd_s):
    tile = pl.program_id(0)
    used = tile < nu_ref[0]
    fresh = jnp.logical_or(tile == 0, te_ref[tile] != te_ref[jnp.maximum(tile - 1, 0)])

    @pl.when(jnp.logical_and(used, fresh))
    def _():
        wgu_s[:, :D_EXPERT] = wg_ref[...].astype(BF16)
        wgu_s[:, D_EXPERT:] = wu_ref[...].astype(BF16)
        wd_s[...] = wd_ref[...].astype(BF16)

    @pl.when(used)
    def _():
        hgu = _dot(xs_ref[...].astype(BF16), wgu_s[...])
        hg = hgu[:, :D_EXPERT]
        hu = hgu[:, D_EXPERT:]
        h = (hg / (1.0 + jnp.exp(-hg))) * hu
        y_ref[...] = _dot(h.astype(BF16), wd_s[...])

    @pl.when(jnp.logical_not(used))
    def _():
        y_ref[...] = jnp.zeros(y_ref.shape, F32)


def _moe(tile_expert, n_used, xs, w_gate, w_up, w_down, tm):
    t_pad = xs.shape[0]
    grid_spec = pltpu.PrefetchScalarGridSpec(
        num_scalar_prefetch=2,
        grid=(t_pad // tm,),
        in_specs=[
            pl.BlockSpec((tm, D_MODEL), lambda i, te, nu: (i, 0)),
            pl.BlockSpec((None, D_MODEL, D_EXPERT), lambda i, te, nu: (te[i], 0, 0)),
            pl.BlockSpec((None, D_MODEL, D_EXPERT), lambda i, te, nu: (te[i], 0, 0)),
            pl.BlockSpec((None, D_EXPERT, D_MODEL), lambda i, te, nu: (te[i], 0, 0)),
        ],
        out_specs=pl.BlockSpec((tm, D_MODEL), lambda i, te, nu: (i, 0)),
        scratch_shapes=[pltpu.VMEM((D_MODEL, 2 * D_EXPERT), BF16), pltpu.VMEM((D_EXPERT, D_MODEL), BF16)],
    )
    return pl.pallas_call(
        _moe_kernel,
        grid_spec=grid_spec,
        out_shape=jax.ShapeDtypeStruct((t_pad, D_MODEL), F32),
        compiler_params=_params(("arbitrary",)),
        name="moe",
    )(tile_expert, n_used, xs, w_gate, w_up, w_down)


def _moe_ln_kernel(x_ref, y1_ref, y2_ref, w_ref, g_ref, b_ref, o_ref):
    w = w_ref[...]
    v = DEEPNORM_ALPHA * x_ref[...] + (w[:, 0:1] * y1_ref[...] + w[:, 1:2] * y2_ref[...])
    o_ref[...] = _layer_norm(v, g_ref[...], b_ref[...])


def _moe_ln(x, y12, w_cols, ln_g, ln_b, tm):
    t = x.shape[0]
    tm = min(tm, t)
    nb = t // tm
    row = pl.BlockSpec((tm, D_MODEL), lambda i: (i, 0))
    second = pl.BlockSpec((tm, D_MODEL), lambda i: (nb + i, 0))
    vec = pl.BlockSpec((1, D_MODEL), lambda i: (0, 0))
    return pl.pallas_call(
        _moe_ln_kernel,
        grid=(nb,),
        in_specs=[row, row, second, pl.BlockSpec((tm, SUBLANES), lambda i: (i, 0)), vec, vec],
        out_specs=row,
        out_shape=jax.ShapeDtypeStruct((t, D_MODEL), F32),
        compiler_params=_params(("parallel",)),
        name="moe_ln",
    )(x, y12, y12, w_cols, ln_g, ln_b)


def _moe_block(x, eidx, ew, w_gate, w_up, w_down, ln_g, ln_b, tm):
    t = x.shape[0]
    tm = min(tm, t)
    rank, cnt = _rank(eidx, 512)
    counts = cnt[:, 0]
    padded = ((counts + tm - 1) // tm) * tm
    ends = jnp.cumsum(padded)
    starts = ends - padded
    e2 = eidx[:2]
    start_of = jnp.zeros_like(e2)
    for k in range(N_EXPERTS):
        start_of = jnp.where(e2 == k, starts[k], start_of)
    dest = start_of + rank[:2]
    t_pad = 2 * t + N_EXPERTS * tm
    tok = jnp.tile(jnp.arange(t, dtype=jnp.int32), 2)
    src_tok = (jnp.arange(t_pad, dtype=jnp.int32) % t).at[dest.reshape(2 * t)].set(
        tok, mode="promise_in_bounds", unique_indices=True)
    tile_start = jnp.arange(t_pad // tm, dtype=jnp.int32) * tm
    tile_expert = jnp.minimum(jnp.sum((ends[None, :] <= tile_start[:, None]).astype(jnp.int32), axis=1),
                              N_EXPERTS - 1)
    n_used = (ends[-1] // tm).astype(jnp.int32).reshape(1)
    xs = x.at[src_tok].get(mode="promise_in_bounds")
    ys = _moe(tile_expert, n_used, xs, w_gate, w_up, w_down, tm)
    y12 = ys.at[dest.reshape(2 * t)].get(mode="promise_in_bounds")
    return _moe_ln(x, y12, ew.T, ln_g, ln_b, tm)


def kernel(x, router_w, router_b, l0_w_in, l0_diff_lam_q1, l0_diff_lam_k1, l0_diff_lam_q2, l0_diff_lam_k2, l0_diff_subln_g, l0_gla_w_gate, l0_gla_b_gate, l0_gla_norm_g, l0_w_out, l0_ln1_g, l0_ln1_b, l0_moe_w_gate, l0_moe_w_up, l0_moe_w_down, l0_ln2_g, l0_ln2_b, l1_w_in, l1_s5_a_re, l1_s5_a_im, l1_s5_log_step, l1_s5_b_re, l1_s5_b_im, l1_s5_c_re, l1_s5_c_im, l1_s5_d, l1_s5_w_glu, l1_s5_b_glu, l1_w_out, l1_ln1_g, l1_ln1_b, l1_moe_w_gate, l1_moe_w_up, l1_moe_w_down, l1_ln2_g, l1_ln2_b):
    bsz, t, d = x.shape
    assert bsz == 1 and d == D_MODEL
    x0 = x.reshape(t, d)
    vec = lambda p: p.reshape(1, -1).astype(F32)
    rwt = router_w.T.astype(F32)
    rb = router_b.reshape(N_EXPERTS, 1).astype(F32)

    lambda_init = 0.8 - 0.6 * math.exp(-0.3 * 0)
    diff_w = DIFF_HEADS * DIFF_V_DIM
    gla_qk, gla_v = GLA_HEADS * GLA_DK, GLA_HEADS * GLA_DV
    c = [0]
    for width in (diff_w, diff_w, diff_w, gla_qk, gla_qk, gla_v, GLA_GATE_RANK, gla_v):
        c.append(c[-1] + width)
    q_scale = math.log2(math.e) * DIFF_HEAD_DIM ** -0.5
    w0 = jnp.concatenate([l0_w_in[:, :c[1]] * q_scale, l0_w_in[:, c[1]:c[6]], l0_w_in[:, c[7]:c[8]],
                          l0_w_in[:, c[6]:c[7]], jnp.zeros((d, L0_GB_PAD - GLA_GATE_RANK), l0_w_in.dtype)],
                         axis=1).astype(BF16)
    h0 = _matmul(x0, w0, 1024, L0_IN_TN, BF16)
    lam_vec = jnp.stack([l0_diff_lam_q1, l0_diff_lam_k1, l0_diff_lam_q2, l0_diff_lam_k2]).astype(F32)
    o_a = _diff_attention(h0, lam_vec, vec(l0_diff_subln_g), lambda_init, 1024)
    wg_pad = jnp.concatenate([l0_gla_w_gate, jnp.zeros((LANES - GLA_GATE_RANK, GLA_HEADS * GLA_DK), F32)],
                             axis=0).astype(BF16)
    o_b = _gla(h0, wg_pad, vec(l0_gla_b_gate), vec(l0_gla_norm_g), 256)
    x1, eidx, ew = _outproj_ln_router(o_a, o_b, l0_w_out, x0, vec(l0_ln1_g), vec(l0_ln1_b),
                                      rwt, rb, 512, 128)
    x2 = _moe_block(x1, eidx, ew, l0_moe_w_gate, l0_moe_w_up, l0_moe_w_down, vec(l0_ln2_g), vec(l0_ln2_b), 256)

    sb_w = SB_HEADS * SB_HEAD_DIM
    w1 = jnp.concatenate([l1_w_in[:, :sb_w] * SB_HEAD_DIM ** -0.5, l1_w_in[:, sb_w:]], axis=1).astype(BF16)
    h1 = _matmul(x2, w1, 1024, 1024, BF16)
    o_c = _stick_breaking(h1, 256)
    lbr, lbi, bbr, bbi = _s5_prep(l1_s5_a_re, l1_s5_a_im, l1_s5_log_step, l1_s5_b_re, l1_s5_b_im)
    bd, cd, pw8 = _s5_tile_params(lbr[::S5_GROUP], lbi[::S5_GROUP], bbr.reshape(S5_GROUPS, S5_GROUP, S5_STATE),
                                  bbi.reshape(S5_GROUPS, S5_GROUP, S5_STATE),
                                  l1_s5_c_re.astype(F32), l1_s5_c_im.astype(F32))
    y_s5 = _s5_core(h1, bd, cd, pw8, 256)
    o_d = _s5_glu(y_s5, h1, vec(l1_s5_d), l1_s5_w_glu.astype(BF16), vec(l1_s5_b_glu), 512)
    x3, eidx1, ew1 = _outproj_ln_router(o_c, o_d, l1_w_out, x2, vec(l1_ln1_g), vec(l1_ln1_b),
                                        rwt, rb, 512, 128)
    x4 = _moe_block(x3, eidx1, ew1, l1_moe_w_gate, l1_moe_w_up, l1_moe_w_down, vec(l1_ln2_g), vec(l1_ln2_b), 256)
    return x4.reshape(bsz, t, d)
```

```python
import functools
import math

import jax
import jax.numpy as jnp
from jax import lax
from jax.experimental import pallas as pl
from jax.experimental.pallas import tpu as pltpu

F32 = jnp.float32
BF16 = jnp.bfloat16

D_MODEL = 2048
DEPTH = 2
DIFF_HEADS = 8
DIFF_HEAD_DIM = 64
DIFF_V_DIM = 128
GLA_HEADS = 4
GLA_DK = 128
GLA_DV = 256
GLA_GATE_RANK = 16
GLA_GATE_TEMP = 16.0
GLA_CHUNK = 64
SB_HEADS = 8
SB_HEAD_DIM = 128
S5_CHANNELS = 1024
S5_GROUP = 16
S5_GROUPS = 64
S5_STATE = 64
N_EXPERTS = 16
N_GROUPS = 4
EXPERTS_PER_GROUP = 4
D_EXPERT = 640
DEEPNORM_ALPHA = (2.0 * DEPTH) ** 0.25
LN_EPS = 1e-5

LANES = 128
SUBLANES = 8
VMEM_LIMIT = 56 * 1024 * 1024
SB_LOG_FLOOR = 120.0
SB_HEADS_PER_STEP = 4

MXU_N = 256
L0_GB_PAD = MXU_N
L0_COL = {"qa": 0, "ka": 1024, "va": 2048, "qb": 3072, "kb": 3584, "vb": 4096, "rb": 5120, "gb": 6144}
L0_IN_TN = 5 * MXU_N

S5_KT = 4
S5_KT_CH = S5_CHANNELS // S5_KT
S5_KT_STATES = (S5_GROUPS // S5_KT) * S5_STATE
S5_SCAN_W = 512
S5_L = 8


def _params(sem, flags=None):
    return pltpu.CompilerParams(dimension_semantics=sem, vmem_limit_bytes=VMEM_LIMIT, flags=flags)


def _nt_dot(a, b):
    return lax.dot_general(a, b, (((1,), (1,)), ((), ())), preferred_element_type=F32)


def _dot(a, b):
    return jnp.dot(a, b, preferred_element_type=F32)


def _split_dot_left(m, x):
    hi = x.astype(BF16)
    lo = (x - hi.astype(F32)).astype(BF16)
    return _dot(m, hi) + _dot(m, lo)


def _layer_norm(v, g, b):
    mu = jnp.mean(v, axis=-1, keepdims=True)
    d = v - mu
    var = jnp.mean(d * d, axis=-1, keepdims=True)
    return d * lax.rsqrt(var + LN_EPS) * g + b


def _mm_kernel(a_ref, b_ref, o_ref, a_bf):
    @pl.when(pl.program_id(1) == 0)
    def _():
        a_bf[...] = a_ref[...].astype(BF16)

    o_ref[...] = _dot(a_bf[...], b_ref[...]).astype(o_ref.dtype)


def _matmul(a, b, tm, tn, out_dtype):
    m, k = a.shape
    n = b.shape[1]
    tm = min(tm, m)
    return pl.pallas_call(
        _mm_kernel,
        grid=(m // tm, n // tn),
        in_specs=[pl.BlockSpec((tm, k), lambda i, j: (i, 0)),
                  pl.BlockSpec((k, tn), lambda i, j: (0, j))],
        out_specs=pl.BlockSpec((tm, tn), lambda i, j: (i, j)),
        out_shape=jax.ShapeDtypeStruct((m, n), out_dtype),
        scratch_shapes=[pltpu.VMEM((tm, k), BF16)],
        compiler_params=_params(("parallel", "arbitrary")),
        name="in_proj",
    )(a, b)


def _diff_attn_kernel(lam_ref, g_ref, q_ref, k_ref, v_ref, o_ref, vx_ref, m_ref, acc_ref, sa_ref, sb_ref, *, tq,
                      lambda_init):
    i = pl.program_id(1)
    dv = DIFF_V_DIM

    @pl.when(i == 0)
    def _():
        vx_ref[:, :dv] = v_ref[...]
        vx_ref[:, dv:] = jnp.ones((vx_ref.shape[0], dv), BF16)

    m_ref[...] = jnp.full(m_ref.shape, -jnp.inf, F32)
    acc_ref[...] = jnp.zeros(acc_ref.shape, F32)
    q = q_ref[...]
    lane = lax.broadcasted_iota(jnp.int32, q.shape, 1)
    zero = jnp.zeros_like(q)
    halves = (jnp.where(lane < DIFF_HEAD_DIM, q, zero), jnp.where(lane >= DIFF_HEAD_DIM, q, zero))

    def scores(j, mi):
        off = pl.multiple_of(j * tq, tq)
        return _nt_dot(halves[mi], k_ref[pl.ds(off, tq), :])

    def softmax_pv(j, get_s, masked):
        off = pl.multiple_of(j * tq, tq)
        vx = vx_ref[pl.ds(off, tq), :]
        if masked:
            row = lax.broadcasted_iota(jnp.int32, (tq, tq), 0)
            col = lax.broadcasted_iota(jnp.int32, (tq, tq), 1)
            causal = col <= row
        for mi in range(2):
            s = get_s(mi)
            if masked:
                s = jnp.where(causal, s, -jnp.inf)
            m_prev = m_ref[mi]
            m_new = jnp.maximum(m_prev, jnp.max(s, axis=1, keepdims=True))
            p = jnp.exp2(s - jnp.tile(m_new, (1, tq // LANES)))
            alpha = jnp.exp2(m_prev - m_new)
            acc_ref[mi] = jnp.tile(alpha, (1, 2)) * acc_ref[mi] + _dot(p.astype(BF16), vx)
            m_ref[mi] = m_new

    def scores_to(j, s_ref):
        for mi in range(2):
            s_ref[mi] = scores(j, mi)

    scores_to(0, sa_ref)

    def pair(t, carry):
        j = 2 * t
        scores_to(j + 1, sb_ref)
        softmax_pv(j, lambda mi: sa_ref[mi], False)
        scores_to(j + 2, sa_ref)
        softmax_pv(j + 1, lambda mi: sb_ref[mi], False)
        return carry

    lax.fori_loop(0, i // 2, pair, 0)

    @pl.when(i % 2 == 1)
    def _():
        softmax_pv(i - 1, lambda mi: sa_ref[mi], False)
        scores_to(i, sa_ref)

    softmax_pv(i, lambda mi: sa_ref[mi], True)

    lv = lam_ref[...]
    lam = (jnp.exp(jnp.sum(lv[0:1] * lv[1:2], axis=1, keepdims=True))
           - jnp.exp(jnp.sum(lv[2:3] * lv[3:4], axis=1, keepdims=True)) + lambda_init)
    a0 = acc_ref[0]
    a1 = acc_ref[1]
    o = a0[:, :dv] / a0[:, dv:] - lam * (a1[:, :dv] / a1[:, dv:])
    ms = jnp.mean(o * o, axis=-1, keepdims=True)
    o = o * lax.rsqrt(ms + LN_EPS) * g_ref[...] * (1.0 - lambda_init)
    o_ref[...] = o.astype(o_ref.dtype)


def _diff_attention(h0, lam_vec, subln_g, lambda_init, tq):
    t = h0.shape[0]
    tq = min(tq, t)
    nq = t // tq
    kern = functools.partial(_diff_attn_kernel, tq=tq, lambda_init=lambda_init)
    return pl.pallas_call(
        kern,
        grid=(DIFF_HEADS, nq),
        in_specs=[
            pl.BlockSpec((4, DIFF_HEAD_DIM), lambda h, i: (0, 0)),
            pl.BlockSpec((1, DIFF_V_DIM), lambda h, i: (0, 0)),
            pl.BlockSpec((tq, LANES), lambda h, i: (i, h)),
            pl.BlockSpec((t, LANES), lambda h, i: (0, DIFF_HEADS + h)),
            pl.BlockSpec((t, LANES), lambda h, i: (0, 2 * DIFF_HEADS + h)),
        ],
        out_specs=pl.BlockSpec((tq, LANES), lambda h, i: (i, h)),
        out_shape=jax.ShapeDtypeStruct((t, DIFF_HEADS * DIFF_V_DIM), BF16),
        scratch_shapes=[pltpu.VMEM((t, 2 * DIFF_V_DIM), BF16), pltpu.VMEM((2, tq, LANES), F32),
                        pltpu.VMEM((2, tq, 2 * DIFF_V_DIM), F32),
                        pltpu.VMEM((2, tq, tq), F32), pltpu.VMEM((2, tq, tq), F32)],
        compiler_params=_params(("arbitrary", "arbitrary")),
        name="diff_attn",
    )(lam_vec, subln_g, h0, h0, h0)


def _gla_kernel(q_ref, k_ref, v_ref, r_ref, gb_ref, wg_ref, bg_ref, ng_ref, o_ref, st_ref, *, tg):
    @pl.when(pl.program_id(0) == 0)
    def _():
        st_ref[...] = jnp.zeros(st_ref.shape, F32)

    c = GLA_CHUNK
    gate = _dot(gb_ref[...], wg_ref[...]) + bg_ref[...]
    log_a = -(jnp.maximum(-gate, 0.0) + jnp.log(1.0 + jnp.exp(-jnp.abs(gate)))) / GLA_GATE_TEMP
    row = lax.broadcasted_iota(jnp.int32, (c, c), 0)
    col = lax.broadcasted_iota(jnp.int32, (c, c), 1)
    tril = row >= col
    tri = jnp.where(tril, 1.0, 0.0).astype(BF16)
    for h in range(GLA_HEADS):
        ksl = slice(h * GLA_DK, (h + 1) * GLA_DK)
        vsl = slice(h * GLA_DV, (h + 1) * GLA_DV)
        for ci in range(tg // c):
            rs = slice(ci * c, (ci + 1) * c)
            la = log_a[rs, ksl]
            b = _split_dot_left(tri, la)
            b_last = b[c - 1:c, :]
            qf = q_ref[rs, ksl].astype(F32) * (GLA_DK ** -0.5)
            kf = k_ref[rs, ksl].astype(F32)
            v = v_ref[rs, vsl]
            q_dec = (qf * jnp.exp(b)).astype(BF16)
            k_dec = (kf * jnp.exp(-b)).astype(BF16)
            k_end = (kf * jnp.exp(b_last - b)).astype(BF16)
            scores = jnp.where(tril, _nt_dot(q_dec, k_dec), 0.0)
            st = st_ref[h]
            o = _dot(scores.astype(BF16), v) + _nt_dot(q_dec, st.astype(BF16))
            kv_t = lax.dot_general(v, k_end, (((0,), (0,)), ((), ())), preferred_element_type=F32)
            st_ref[h] = jnp.exp(b_last) * st + kv_t
            ms = jnp.mean(o * o, axis=-1, keepdims=True)
            o = o * lax.rsqrt(ms + LN_EPS) * ng_ref[...]
            r = r_ref[rs, vsl].astype(F32)
            o_ref[rs, vsl] = (o * (r / (1.0 + jnp.exp(-r)))).astype(o_ref.dtype)


def _gla(h0, wg_pad, b_gate, norm_g, tg):
    t = h0.shape[0]
    tg = min(tg, t)
    qk = GLA_HEADS * GLA_DK
    vw = GLA_HEADS * GLA_DV
    return pl.pallas_call(
        functools.partial(_gla_kernel, tg=tg),
        grid=(t // tg,),
        in_specs=[
            pl.BlockSpec((tg, qk), lambda i: (i, L0_COL["qb"] // qk)),
            pl.BlockSpec((tg, qk), lambda i: (i, L0_COL["kb"] // qk)),
            pl.BlockSpec((tg, vw), lambda i: (i, L0_COL["vb"] // vw)),
            pl.BlockSpec((tg, vw), lambda i: (i, L0_COL["rb"] // vw)),
            pl.BlockSpec((tg, LANES), lambda i: (i, L0_COL["gb"] // LANES)),
            pl.BlockSpec((LANES, qk), lambda i: (0, 0)),
            pl.BlockSpec((1, qk), lambda i: (0, 0)),
            pl.BlockSpec((1, GLA_DV), lambda i: (0, 0)),
        ],
        out_specs=pl.BlockSpec((tg, vw), lambda i: (i, 0)),
        out_shape=jax.ShapeDtypeStruct((t, vw), BF16),
        scratch_shapes=[pltpu.VMEM((GLA_HEADS, GLA_DV, GLA_DK), F32)],
        compiler_params=_params(("arbitrary",)),
        name="gla",
    )(h0, h0, h0, h0, h0, wg_pad, b_gate, norm_g)


def _sb_kernel(q_ref, k_ref, v_ref, o_ref, acc_ref, c_ref, *, tq):
    i = pl.program_id(1)
    hd = SB_HEAD_DIM
    acc_ref[...] = jnp.zeros(acc_ref.shape, F32)
    c_ref[...] = jnp.zeros(c_ref.shape, F32)
    row = lax.broadcasted_iota(jnp.int32, (tq, tq), 0)
    col = lax.broadcasted_iota(jnp.int32, (tq, tq), 1)
    strict = col < row
    tri = jnp.where(row > col, 1.0, 0.0).astype(BF16)

    def block(jb, masked):
        off = pl.multiple_of(jb * tq, tq)
        cmax = None
        for hh in range(SB_HEADS_PER_STEP):
            cols = slice(hh * hd, (hh + 1) * hd)
            k = k_ref[pl.ds(off, tq), cols]
            v = v_ref[pl.ds(off, tq), cols]
            z = _nt_dot(q_ref[:, cols], k)
            lsn = -(jnp.maximum(z, 0.0) + jnp.log(1.0 + jnp.exp(-jnp.abs(z))))
            lm = jnp.where(strict, lsn, 0.0) if masked else lsn
            excl = _dot(lm.astype(BF16), tri) + c_ref[hh]
            w = jnp.exp(z + lsn + excl)
            if masked:
                w = jnp.where(strict, w, 0.0)
            acc_ref[:, cols] += _dot(w.astype(BF16), v)
            c_new = c_ref[hh] + jnp.sum(lm, axis=1, keepdims=True)
            c_ref[hh] = c_new
            hmax = jnp.max(c_new)
            cmax = hmax if cmax is None else jnp.maximum(cmax, hmax)
        return cmax

    cmax = block(i, True)

    def cond(carry):
        jb, cm = carry
        return jnp.logical_and(jb >= 0, cm > -SB_LOG_FLOOR)

    def body(carry):
        jb, _ = carry
        return jb - 1, block(jb, False)

    lax.while_loop(cond, body, (i - 1, cmax))
    o_ref[...] = acc_ref[...].astype(o_ref.dtype)


def _stick_breaking(h1, tq):
    t = h1.shape[0]
    tq = min(tq, t)
    ng = SB_HEADS // SB_HEADS_PER_STEP
    wd = SB_HEADS_PER_STEP * SB_HEAD_DIM
    return pl.pallas_call(
        functools.partial(_sb_kernel, tq=tq),
        grid=(ng, t // tq),
        in_specs=[
            pl.BlockSpec((tq, wd), lambda h, i: (i, h)),
            pl.BlockSpec((t, wd), lambda h, i: (0, ng + h), pipeline_mode=pl.Buffered(1)),
            pl.BlockSpec((t, wd), lambda h, i: (0, 2 * ng + h), pipeline_mode=pl.Buffered(1)),
        ],
        out_specs=pl.BlockSpec((tq, wd), lambda h, i: (i, h)),
        out_shape=jax.ShapeDtypeStruct((t, SB_HEADS * SB_HEAD_DIM), BF16),
        scratch_shapes=[pltpu.VMEM((tq, wd), F32), pltpu.VMEM((SB_HEADS_PER_STEP, tq, 1), F32)],
        compiler_params=_params(("parallel", "arbitrary")),
        name="stick_breaking",
    )(h1, h1, h1)


def _s5_prep_kernel(are_ref, aim_ref, ls_ref, bre_ref, bim_ref, lbr_ref, lbi_ref, bbr_ref, bbi_ref):
    step = jnp.exp(ls_ref[...])
    lam_re = are_ref[...]
    lam_im = aim_ref[...]
    mag = jnp.exp(step * lam_re)
    lb_re = mag * jnp.cos(step * lam_im)
    lb_im = mag * jnp.sin(step * lam_im)
    den = lam_re * lam_re + lam_im * lam_im
    n_re = lb_re - 1.0
    f_re = (n_re * lam_re + lb_im * lam_im) / den
    f_im = (lb_im * lam_re - n_re * lam_im) / den
    lbr_ref[...] = lb_re
    lbi_ref[...] = lb_im
    bbr_ref[...] = f_re * bre_ref[...] - f_im * bim_ref[...]
    bbi_ref[...] = f_re * bim_ref[...] + f_im * bre_ref[...]


def _s5_prep(a_re, a_im, log_step, b_re, b_im):
    rows = S5_GROUPS * S5_GROUP
    rep = lambda p: jnp.repeat(p, S5_GROUP, axis=0)
    to_rows = lambda b: jnp.transpose(b, (0, 2, 1)).reshape(rows, S5_STATE)
    spec = pl.BlockSpec((rows, S5_STATE), lambda: (0, 0))
    shp = jax.ShapeDtypeStruct((rows, S5_STATE), F32)
    return pl.pallas_call(
        _s5_prep_kernel,
        in_specs=[spec, spec, pl.BlockSpec((rows, 1), lambda: (0, 0)), spec, spec],
        out_specs=[spec] * 4,
        out_shape=[shp] * 4,
        name="s5_prep",
    )(rep(a_re), rep(a_im), rep(log_step[:, None]), to_rows(b_re), to_rows(b_im))


def _cmul(ar, ai, br, bi):
    return ar * br - ai * bi, ar * bi + ai * br


def _block_diag_tiles(m, rows_per_group, cols_per_group):
    gpt = S5_GROUPS // S5_KT
    m = m.reshape(S5_KT, gpt, rows_per_group, cols_per_group)
    eye = jnp.eye(gpt, dtype=m.dtype)
    bd = jnp.einsum("kgrc,gh->kgrhc", m, eye)
    return bd.reshape(S5_KT, gpt * rows_per_group, gpt * cols_per_group)


def _s5_tile_params(lb_re, lb_im, bb_re, bb_im, c_re, c_im):
    bd = jnp.concatenate([_block_diag_tiles(bb_re, S5_GROUP, S5_STATE),
                          _block_diag_tiles(bb_im, S5_GROUP, S5_STATE)], axis=2)
    cd = jnp.concatenate([_block_diag_tiles(jnp.transpose(c_re, (0, 2, 1)), S5_STATE, S5_GROUP),
                          _block_diag_tiles(-jnp.transpose(c_im, (0, 2, 1)), S5_STATE, S5_GROUP)], axis=1)
    pr = [jnp.ones_like(lb_re)]
    pi = [jnp.zeros_like(lb_re)]
    for _ in range(S5_L):
        nr, ni = _cmul(pr[-1], pi[-1], lb_re, lb_im)
        pr.append(nr)
        pi.append(ni)

    def table(p):
        rows = jnp.transpose(jnp.stack(p).reshape(S5_L + 1, S5_KT, S5_KT_STATES), (1, 0, 2))
        return jnp.pad(rows, ((0, 0), (0, LANES - (S5_L + 1)), (0, 0)))

    return bd, cd, jnp.stack([table(pr), table(pi)], axis=1)


def _s5_core_kernel(u_ref, bd_ref, cd_ref, pw8_ref, y_ref, hs_ref, hp_ref, pw_ref, st_ref, uf_ref,
                    mt_s, bbs_s, ccs_s, *, tcr):
    ns = S5_KT_STATES
    pad = SUBLANES
    cw = S5_KT_CH

    @pl.when(pl.program_id(1) == 0)
    def _():
        b_re = bd_ref[:, :ns]
        b_im = bd_ref[:, ns:]
        cd_hi = cd_ref[...].astype(BF16)
        cd_lo = (cd_ref[...] - cd_hi.astype(F32)).astype(BF16)
        for k in range(S5_L):
            a = pw8_ref[0, k:k + 1, :]
            b = pw8_ref[1, k:k + 1, :]
            dk = jnp.concatenate([b_re * a - b_im * b, b_re * b + b_im * a], axis=1)
            s = S5_L - 1 - k
            bbs_s[s * cw:(s + 1) * cw, :] = dk.astype(BF16)
            dk_hi = dk.astype(BF16)
            dk_lo = (dk - dk_hi.astype(F32)).astype(BF16)
            kk = (_dot(dk_hi, cd_hi) + (_dot(dk_lo, cd_hi) + _dot(dk_hi, cd_lo))).astype(BF16)
            for s0 in range(S5_L - k):
                mt_s[s0 * cw:(s0 + 1) * cw, (s0 + k) * cw:(s0 + k + 1) * cw] = kk
        pw_t = (jnp.transpose(pw8_ref[0]), jnp.transpose(pw8_ref[1]))
        c_re = cd_ref[:ns, :]
        c_im = cd_ref[ns:, :]
        for i in range(S5_L):
            a = pw_t[0][:, i + 1:i + 2]
            b = pw_t[1][:, i + 1:i + 2]
            ccs_s[:ns, i * cw:(i + 1) * cw] = (c_re * a + c_im * b).astype(BF16)
            ccs_s[ns:, i * cw:(i + 1) * cw] = (c_im * a - c_re * b).astype(BF16)

        hp_ref[...] = jnp.zeros(hp_ref.shape, F32)
        hs_ref[0:pad, :] = jnp.zeros((pad, 2 * ns), F32)
        l1r = jnp.broadcast_to(pw8_ref[0, S5_L:S5_L + 1, :], (SUBLANES, ns))
        l1i = jnp.broadcast_to(pw8_ref[1, S5_L:S5_L + 1, :], (SUBLANES, ns))
        l2r, l2i = _cmul(l1r, l1i, l1r, l1i)
        l4r, l4i = _cmul(l2r, l2i, l2r, l2i)
        l8r, l8i = _cmul(l4r, l4i, l4r, l4i)
        ridx = lax.broadcasted_iota(jnp.int32, l1r.shape, 0)
        n = ridx + 1
        pr = jnp.ones_like(l1r)
        pi = jnp.zeros_like(l1r)
        for bit, (lr, li) in ((1, (l1r, l1i)), (2, (l2r, l2i)), (4, (l4r, l4i)), (8, (l8r, l8i))):
            nr, ni = _cmul(pr, pi, lr, li)
            take = (n & bit) != 0
            pr = jnp.where(take, nr, pr)
            pi = jnp.where(take, ni, pi)
        pw_ref[0] = pr
        pw_ref[1] = pi
        for s, (lr, li) in enumerate(((l1r, l1i), (l2r, l2i), (l4r, l4i))):
            keep = ridx >= (1 << s)
            st_ref[2 * s] = jnp.where(keep, lr, 0.0)
            st_ref[2 * s + 1] = jnp.where(keep, li, 0.0)

    nl = S5_KT_CH // LANES
    for h in range(nl):
        uf_ref[h] = u_ref[:, h * LANES:(h + 1) * LANES].astype(F32)
    uc = jnp.concatenate([uf_ref[h, pl.ds(s, tcr, stride=S5_L), :].astype(BF16)
                          for s in range(S5_L) for h in range(nl)], axis=1)
    hs_ref[pad:pad + tcr, :] = _dot(uc, bbs_s[...])
    w = S5_SCAN_W
    for cc in range(ns // w):
        re0 = cc * w
        im0 = ns + cc * w

        def body(r, carry, re0=re0, im0=im0):
            hpr, hpi = carry
            rows = pl.ds(pl.multiple_of(pad + r * SUBLANES, SUBLANES), SUBLANES)
            xr = hs_ref[rows, re0:re0 + w]
            xi = hs_ref[rows, im0:im0 + w]
            for s in range(3):
                sr = pltpu.roll(xr, 1 << s, 0)
                si = pltpu.roll(xi, 1 << s, 0)
                ar = st_ref[2 * s, :, re0:re0 + w]
                ai = st_ref[2 * s + 1, :, re0:re0 + w]
                xr, xi = xr + ar * sr - ai * si, xi + ar * si + ai * sr
            pr = pw_ref[0, :, re0:re0 + w]
            pi = pw_ref[1, :, re0:re0 + w]
            xr, xi = xr + pr * hpr - pi * hpi, xi + pr * hpi + pi * hpr
            hs_ref[rows, re0:re0 + w] = xr
            hs_ref[rows, im0:im0 + w] = xi
            return (jnp.broadcast_to(xr[SUBLANES - 1:SUBLANES, :], (SUBLANES, w)),
                    jnp.broadcast_to(xi[SUBLANES - 1:SUBLANES, :], (SUBLANES, w)))

        hpr, hpi = lax.fori_loop(0, tcr // SUBLANES, body, (hp_ref[:, re0:re0 + w], hp_ref[:, im0:im0 + w]))
        hp_ref[:, re0:re0 + w] = hpr
        hp_ref[:, im0:im0 + w] = hpi

    h_prev = hs_ref[pad - 1:pad - 1 + tcr, :]
    y_in = _dot(h_prev.astype(BF16), ccs_s[...])
    for i in range(S5_L):
        k_hi = (i + 1) * cw
        y_i = y_in[:, i * cw:k_hi] + _dot(uc[:, :k_hi], mt_s[:k_hi, i * cw:k_hi])
        for h in range(nl):
            uf_ref[h, pl.ds(i, tcr, stride=S5_L), :] = y_i[:, h * LANES:(h + 1) * LANES]
    for h in range(nl):
        y_ref[:, h * LANES:(h + 1) * LANES] = uf_ref[h]
    hs_ref[pad - 1:pad, :] = hs_ref[pad + tcr - 1:pad + tcr, :]


def _s5_core(h1, bd, cd, pw8, tcr):
    t = h1.shape[0]
    width = S5_L * S5_KT_CH
    tcr = min(tcr, t // S5_L)
    tc = tcr * S5_L
    ns = S5_KT_STATES
    u_col0 = 3 * S5_CHANNELS // S5_KT_CH
    once = pl.Buffered(1)
    return pl.pallas_call(
        functools.partial(_s5_core_kernel, tcr=tcr),
        grid=(S5_KT, t // tc),
        in_specs=[
            pl.BlockSpec((tc, S5_KT_CH), lambda k, i: (i, u_col0 + k)),
            pl.BlockSpec((None, S5_KT_CH, 2 * ns), lambda k, i: (k, 0, 0), pipeline_mode=once),
            pl.BlockSpec((None, 2 * ns, S5_KT_CH), lambda k, i: (k, 0, 0), pipeline_mode=once),
            pl.BlockSpec((None, 2, LANES, ns), lambda k, i: (k, 0, 0, 0), pipeline_mode=once),
        ],
        out_specs=pl.BlockSpec((tc, S5_KT_CH), lambda k, i: (i, k)),
        out_shape=jax.ShapeDtypeStruct((t, S5_CHANNELS), F32),
        scratch_shapes=[pltpu.VMEM((tcr + SUBLANES, 2 * ns), F32), pltpu.VMEM((SUBLANES, 2 * ns), F32),
                        pltpu.VMEM((2, SUBLANES, ns), F32), pltpu.VMEM((6, SUBLANES, ns), F32),
                        pltpu.VMEM((S5_KT_CH // LANES, tc, LANES), F32),
                        pltpu.VMEM((width, width), BF16), pltpu.VMEM((width, 2 * ns), BF16),
                        pltpu.VMEM((2 * ns, width), BF16)],
        compiler_params=_params(("arbitrary", "arbitrary")),
        name="s5_core",
    )(h1, bd, cd, pw8)


def _s5_glu_kernel(y_ref, u_ref, d_ref, wglu_ref, bglu_ref, o_ref):
    y = y_ref[...] + d_ref[...] * u_ref[...].astype(F32)
    g = 0.5 * y * (1.0 + jnp.tanh(math.sqrt(2.0 / math.pi) * (y + 0.044715 * (y * y * y))))
    gl = _dot(g.astype(BF16), wglu_ref[...]) + bglu_ref[...]
    o_ref[...] = (g / (1.0 + jnp.exp(-gl))).astype(o_ref.dtype)


def _s5_glu(y, h1, d_skip, w_glu, b_glu, tm):
    t = h1.shape[0]
    tm = min(tm, t)
    full = lambda shape: pl.BlockSpec(shape, lambda i: (0,) * len(shape))
    return pl.pallas_call(
        _s5_glu_kernel,
        grid=(t // tm,),
        in_specs=[pl.BlockSpec((tm, S5_CHANNELS), lambda i: (i, 0)),
                  pl.BlockSpec((tm, S5_CHANNELS), lambda i: (i, 3)),
                  full((1, S5_CHANNELS)), full((S5_CHANNELS, S5_CHANNELS)), full((1, S5_CHANNELS))],
        out_specs=pl.BlockSpec((tm, S5_CHANNELS), lambda i: (i, 0)),
        out_shape=jax.ShapeDtypeStruct((t, S5_CHANNELS), BF16),
        compiler_params=_params(("parallel",)),
        name="s5_glu",
    )(y, h1, d_skip, w_glu, b_glu)


def _route(probs, sel):
    rows = [sel[e:e + 1, :] for e in range(N_EXPERTS)]
    prow = [probs[e:e + 1, :] for e in range(N_EXPERTS)]
    best_score = None
    for g in range(N_GROUPS):
        a = rows[g * EXPERTS_PER_GROUP:(g + 1) * EXPERTS_PER_GROUP]
        score = None
        for x in range(EXPERTS_PER_GROUP):
            for y in range(x + 1, EXPERTS_PER_GROUP):
                pair = a[x] + a[y]
                score = pair if score is None else jnp.maximum(score, pair)
        if best_score is None:
            best_score, gbest = score, jnp.zeros(score.shape, jnp.int32)
            cs = list(a)
            cp = prow[0:EXPERTS_PER_GROUP]
        else:
            better = score > best_score
            best_score = jnp.where(better, score, best_score)
            gbest = jnp.where(better, g, gbest)
            cs = [jnp.where(better, a[x], cs[x]) for x in range(EXPERTS_PER_GROUP)]
            cp = [jnp.where(better, prow[g * EXPERTS_PER_GROUP + x], cp[x]) for x in range(EXPERTS_PER_GROUP)]

    def first_argmax(vals):
        bv, bi = vals[0], jnp.zeros(vals[0].shape, jnp.int32)
        for x in range(1, len(vals)):
            better = vals[x] > bv
            bv = jnp.where(better, vals[x], bv)
            bi = jnp.where(better, x, bi)
        return bi

    i1 = first_argmax(cs)
    i2 = first_argmax([jnp.where(i1 == x, -jnp.inf, cs[x]) for x in range(EXPERTS_PER_GROUP)])

    def pick(idx):
        out = cp[0]
        for x in range(1, EXPERTS_PER_GROUP):
            out = jnp.where(idx == x, cp[x], out)
        return out

    p1, p2 = pick(i1), pick(i2)
    tot = p1 + p2
    return gbest * EXPERTS_PER_GROUP + i1, gbest * EXPERTS_PER_GROUP + i2, p1 / tot, p2 / tot


def _ln_router_tail(v, g_ref, b_ref, rwh_ref, rwl_ref, rb_ref):
    xn = _layer_norm(v, g_ref[...], b_ref[...])
    xh = xn.astype(BF16)
    xl = (xn - xh.astype(F32)).astype(BF16)
    logits = _nt_dot(rwh_ref[...], xh) + (_nt_dot(rwl_ref[...], xh) + _nt_dot(rwh_ref[...], xl))
    mx = jnp.max(logits, axis=0, keepdims=True)
    ex = jnp.exp(logits - mx)
    probs = ex / jnp.sum(ex, axis=0, keepdims=True)
    e1, e2, w1, w2 = _route(probs, probs + rb_ref[...])
    n = e1.shape[1]
    ei = jnp.concatenate([e1, e2, jnp.zeros((SUBLANES - 2, n), jnp.int32)], axis=0)
    ew = jnp.concatenate([w1, w2, jnp.zeros((SUBLANES - 2, n), F32)], axis=0)
    return xn, ei, ew


def _outproj_kernel(ma_ref, mb_ref, w_ref, xr_ref, g_ref, b_ref, rwh_ref, rwl_ref, rb_ref,
                    x_ref, ei_ref, ew_ref, w_bf, *, sub):
    @pl.when(pl.program_id(0) == 0)
    def _():
        w_bf[...] = w_ref[...].astype(BF16)

    half = w_bf.shape[0] // 2
    for s in range(x_ref.shape[0] // sub):
        rows = slice(s * sub, (s + 1) * sub)
        v = (DEEPNORM_ALPHA * xr_ref[rows, :] + _dot(ma_ref[rows, :], w_bf[:half, :])
             + _dot(mb_ref[rows, :], w_bf[half:, :]))
        xn, ei, ew = _ln_router_tail(v, g_ref, b_ref, rwh_ref, rwl_ref, rb_ref)
        x_ref[rows, :] = xn
        ei_ref[:, rows] = ei
        ew_ref[:, rows] = ew


def _outproj_ln_router(mix_a, mix_b, w_out, x_res, ln_g, ln_b, rwt, rb, tm, sub):
    t = x_res.shape[0]
    rwt_hi = rwt.astype(BF16)
    rwt_lo = (rwt - rwt_hi.astype(F32)).astype(BF16)
    tm = min(tm, t)
    sub = min(sub, tm)
    half = w_out.shape[0] // 2
    row = lambda w: pl.BlockSpec((tm, w), lambda i: (i, 0))
    full = lambda shape: pl.BlockSpec(shape, lambda i: (0,) * len(shape))
    return pl.pallas_call(
        functools.partial(_outproj_kernel, sub=sub),
        grid=(t // tm,),
        in_specs=[row(half), row(half),
                  pl.BlockSpec((2 * half, D_MODEL), lambda i: (0, 0), pipeline_mode=pl.Buffered(1)),
                  row(D_MODEL), full((1, D_MODEL)), full((1, D_MODEL)),
                  full((N_EXPERTS, D_MODEL)), full((N_EXPERTS, D_MODEL)), full((N_EXPERTS, 1))],
        out_specs=[row(D_MODEL),
                   pl.BlockSpec((SUBLANES, tm), lambda i: (0, i)), pl.BlockSpec((SUBLANES, tm), lambda i: (0, i))],
        out_shape=[jax.ShapeDtypeStruct((t, D_MODEL), F32),
                   jax.ShapeDtypeStruct((SUBLANES, t), jnp.int32), jax.ShapeDtypeStruct((SUBLANES, t), F32)],
        scratch_shapes=[pltpu.VMEM((2 * half, D_MODEL), BF16)],
        compiler_params=_params(("arbitrary",)),
        name="outproj_ln_router",
    )(mix_a, mix_b, w_out, x_res, ln_g, ln_b, rwt_hi, rwt_lo, rb)


def _rank_kernel(e_ref, rank_ref, cnt_ref, base_ref, *, tn):
    @pl.when(pl.program_id(0) == 0)
    def _():
        base_ref[...] = jnp.zeros(base_ref.shape, F32)

    e = e_ref[...]
    rows = lax.broadcasted_iota(jnp.int32, (N_EXPERTS, tn), 0)
    r = lax.broadcasted_iota(jnp.int32, (tn, tn), 0)
    c = lax.broadcasted_iota(jnp.int32, (tn, tn), 1)
    earlier = jnp.where(r < c, 1.0, 0.0).astype(BF16)
    ranks = []
    for k in range(2):
        oh = rows == e[k:k + 1, :]
        ohf = jnp.where(oh, 1.0, 0.0)
        base = base_ref[k]
        before = _dot(ohf.astype(BF16), earlier) + jnp.tile(base, (1, tn // LANES))
        ranks.append(jnp.sum(jnp.where(oh, before, 0.0), axis=0, keepdims=True))
        base = base + jnp.sum(ohf, axis=1, keepdims=True)
        base_ref[k] = base
        cnt_ref[k] = base.astype(jnp.int32)
    rank_ref[...] = jnp.concatenate(ranks + [jnp.zeros((SUBLANES - 2, tn), F32)], axis=0).astype(jnp.int32)


def _rank(eidx, tn):
    t = eidx.shape[1]
    tn = min(tn, t)
    return pl.pallas_call(
        functools.partial(_rank_kernel, tn=tn),
        grid=(t // tn,),
        in_specs=[pl.BlockSpec((SUBLANES, tn), lambda i: (0, i))],
        out_specs=[pl.BlockSpec((SUBLANES, tn), lambda i: (0, i)),
                   pl.BlockSpec((2, N_EXPERTS, LANES), lambda i: (0, 0, 0))],
        out_shape=[jax.ShapeDtypeStruct((SUBLANES, t), jnp.int32),
                   jax.ShapeDtypeStruct((2, N_EXPERTS, LANES), jnp.int32)],
        scratch_shapes=[pltpu.VMEM((2, N_EXPERTS, LANES), F32)],
        compiler_params=_params(("arbitrary",)),
        name="moe_rank",
    )(eidx)


def _moe_kernel(te_ref, nu_ref, xs_ref, wg_ref, wu_ref, wd_ref, y_ref, wgu_s, wd_s):
    tile = pl.program_id(0)
    used = tile < nu_ref[0]
    fresh = jnp.logical_or(tile == 0, te_ref[tile] != te_ref[jnp.maximum(tile - 1, 0)])

    @pl.when(jnp.logical_and(used, fresh))
    def _():
        wgu_s[:, :D_EXPERT] = wg_ref[...].astype(BF16)
        wgu_s[:, D_EXPERT:] = wu_ref[...].astype(BF16)
        wd_s[...] = wd_ref[...].astype(BF16)

    @pl.when(used)
    def _():
        hgu = _dot(xs_ref[...].astype(BF16), wgu_s[...])
        hg = hgu[:, :D_EXPERT]
        hu = hgu[:, D_EXPERT:]
        h = (hg / (1.0 + jnp.exp(-hg))) * hu
        y_ref[...] = _dot(h.astype(BF16), wd_s[...])

    @pl.when(jnp.logical_not(used))
    def _():
        y_ref[...] = jnp.zeros(y_ref.shape, F32)


def _moe(tile_expert, n_used, xs, w_gate, w_up, w_down, tm):
    t_pad = xs.shape[0]
    grid_spec = pltpu.PrefetchScalarGridSpec(
        num_scalar_prefetch=2,
        grid=(t_pad // tm,),
        in_specs=[
            pl.BlockSpec((tm, D_MODEL), lambda i, te, nu: (i, 0)),
            pl.BlockSpec((None, D_MODEL, D_EXPERT), lambda i, te, nu: (te[i], 0, 0)),
            pl.BlockSpec((None, D_MODEL, D_EXPERT), lambda i, te, nu: (te[i], 0, 0)),
            pl.BlockSpec((None, D_EXPERT, D_MODEL), lambda i, te, nu: (te[i], 0, 0)),
        ],
        out_specs=pl.BlockSpec((tm, D_MODEL), lambda i, te, nu: (i, 0)),
        scratch_shapes=[pltpu.VMEM((D_MODEL, 2 * D_EXPERT), BF16), pltpu.VMEM((D_EXPERT, D_MODEL), BF16)],
    )
    return pl.pallas_call(
        _moe_kernel,
        grid_spec=grid_spec,
        out_shape=jax.ShapeDtypeStruct((t_pad, D_MODEL), F32),
        compiler_params=_params(("arbitrary",)),
        name="moe",
    )(tile_expert, n_used, xs, w_gate, w_up, w_down)


def _moe_ln_kernel(x_ref, y1_ref, y2_ref, w_ref, g_ref, b_ref, o_ref):
    w = w_ref[...]
    v = DEEPNORM_ALPHA * x_ref[...] + (w[:, 0:1] * y1_ref[...] + w[:, 1:2] * y2_ref[...])
    o_ref[...] = _layer_norm(v, g_ref[...], b_ref[...])


def _moe_ln(x, y1, y2, w_cols, ln_g, ln_b, tm):
    t = x.shape[0]
    tm = min(tm, t)
    row = pl.BlockSpec((tm, D_MODEL), lambda i: (i, 0))
    vec = pl.BlockSpec((1, D_MODEL), lambda i: (0, 0))
    return pl.pallas_call(
        _moe_ln_kernel,
        grid=(t // tm,),
        in_specs=[row, row, row, pl.BlockSpec((tm, SUBLANES), lambda i: (i, 0)), vec, vec],
        out_specs=row,
        out_shape=jax.ShapeDtypeStruct((t, D_MODEL), F32),
        compiler_params=_params(("parallel",)),
        name="moe_ln",
    )(x, y1, y2, w_cols, ln_g, ln_b)


def _moe_block(x, eidx, ew, w_gate, w_up, w_down, ln_g, ln_b, tm):
    t = x.shape[0]
    tm = min(tm, t)
    rank, cnt = _rank(eidx, 512)
    t_pad = t + N_EXPERTS * tm
    tile_start = jnp.arange(t_pad // tm, dtype=jnp.int32) * tm
    ys = []
    for k in range(2):
        counts = cnt[k, :, 0]
        padded = ((counts + tm - 1) // tm) * tm
        ends = jnp.cumsum(padded)
        starts = ends - padded
        start_of = jnp.zeros((t,), jnp.int32)
        for e in range(N_EXPERTS):
            start_of = jnp.where(eidx[k] == e, starts[e], start_of)
        dest = start_of + rank[k]
        src_tok = (jnp.arange(t_pad, dtype=jnp.int32) % t).at[dest].set(
            jnp.arange(t, dtype=jnp.int32), mode="promise_in_bounds", unique_indices=True)
        tile_expert = jnp.minimum(jnp.sum((ends[None, :] <= tile_start[:, None]).astype(jnp.int32), axis=1),
                                  N_EXPERTS - 1)
        n_used = (ends[-1] // tm).astype(jnp.int32).reshape(1)
        xs = x.at[src_tok].get(mode="promise_in_bounds")
        y_sorted = _moe(tile_expert, n_used, xs, w_gate, w_up, w_down, tm)
        ys.append(y_sorted.at[dest].get(mode="promise_in_bounds"))
    return _moe_ln(x, ys[0], ys[1], ew.T, ln_g, ln_b, tm)


def kernel(x, router_w, router_b, l0_w_in, l0_diff_lam_q1, l0_diff_lam_k1, l0_diff_lam_q2, l0_diff_lam_k2, l0_diff_subln_g, l0_gla_w_gate, l0_gla_b_gate, l0_gla_norm_g, l0_w_out, l0_ln1_g, l0_ln1_b, l0_moe_w_gate, l0_moe_w_up, l0_moe_w_down, l0_ln2_g, l0_ln2_b, l1_w_in, l1_s5_a_re, l1_s5_a_im, l1_s5_log_step, l1_s5_b_re, l1_s5_b_im, l1_s5_c_re, l1_s5_c_im, l1_s5_d, l1_s5_w_glu, l1_s5_b_glu, l1_w_out, l1_ln1_g, l1_ln1_b, l1_moe_w_gate, l1_moe_w_up, l1_moe_w_down, l1_ln2_g, l1_ln2_b):
    bsz, t, d = x.shape
    assert bsz == 1 and d == D_MODEL
    x0 = x.reshape(t, d)
    vec = lambda p: p.reshape(1, -1).astype(F32)
    rwt = router_w.T.astype(F32)
    rb = router_b.reshape(N_EXPERTS, 1).astype(F32)

    lambda_init = 0.8 - 0.6 * math.exp(-0.3 * 0)
    diff_w = DIFF_HEADS * DIFF_V_DIM
    gla_qk, gla_v = GLA_HEADS * GLA_DK, GLA_HEADS * GLA_DV
    c = [0]
    for width in (diff_w, diff_w, diff_w, gla_qk, gla_qk, gla_v, GLA_GATE_RANK, gla_v):
        c.append(c[-1] + width)
    q_scale = math.log2(math.e) * DIFF_HEAD_DIM ** -0.5
    w0 = jnp.concatenate([l0_w_in[:, :c[1]] * q_scale, l0_w_in[:, c[1]:c[6]], l0_w_in[:, c[7]:c[8]],
                          l0_w_in[:, c[6]:c[7]], jnp.zeros((d, L0_GB_PAD - GLA_GATE_RANK), l0_w_in.dtype)],
                         axis=1).astype(BF16)
    h0 = _matmul(x0, w0, 1024, L0_IN_TN, BF16)
    lam_vec = jnp.stack([l0_diff_lam_q1, l0_diff_lam_k1, l0_diff_lam_q2, l0_diff_lam_k2]).astype(F32)
    o_a = _diff_attention(h0, lam_vec, vec(l0_diff_subln_g), lambda_init, 1024)
    wg_pad = jnp.concatenate([l0_gla_w_gate, jnp.zeros((LANES - GLA_GATE_RANK, GLA_HEADS * GLA_DK), F32)],
                             axis=0).astype(BF16)
    o_b = _gla(h0, wg_pad, vec(l0_gla_b_gate), vec(l0_gla_norm_g), 256)
    x1, eidx, ew = _outproj_ln_router(o_a, o_b, l0_w_out, x0, vec(l0_ln1_g), vec(l0_ln1_b),
                                      rwt, rb, 512, 128)
    x2 = _moe_block(x1, eidx, ew, l0_moe_w_gate, l0_moe_w_up, l0_moe_w_down, vec(l0_ln2_g), vec(l0_ln2_b), 256)

    sb_w = SB_HEADS * SB_HEAD_DIM
    w1 = jnp.concatenate([l1_w_in[:, :sb_w] * SB_HEAD_DIM ** -0.5, l1_w_in[:, sb_w:]], axis=1).astype(BF16)
    h1 = _matmul(x2, w1, 1024, 1024, BF16)
    o_c = _stick_breaking(h1, 256)
    lbr, lbi, bbr, bbi = _s5_prep(l1_s5_a_re, l1_s5_a_im, l1_s5_log_step, l1_s5_b_re, l1_s5_b_im)
    bd, cd, pw8 = _s5_tile_params(lbr[::S5_GROUP], lbi[::S5_GROUP], bbr.reshape(S5_GROUPS, S5_GROUP, S5_STATE),
                                  bbi.reshape(S5_GROUPS, S5_GROUP, S5_STATE),
                                  l1_s5_c_re.astype(F32), l1_s5_c_im.astype(F32))
    y_s5 = _s5_core(h1, bd, cd, pw8, 256)
    o_d = _s5_glu(y_s5, h1, vec(l1_s5_d), l1_s5_w_glu.astype(BF16), vec(l1_s5_b_glu), 512)
    x3, eidx1, ew1 = _outproj_ln_router(o_c, o_d, l1_w_out, x2, vec(l1_ln1_g), vec(l1_ln1_b),
                                        rwt, rb, 512, 128)
    x4 = _moe_block(x3, eidx1, ew1, l1_moe_w_gate, l1_moe_w_up, l1_moe_w_down, vec(l1_ln2_g), vec(l1_ln2_b), 256)
    return x4.reshape(bsz, t, d)
```

```python
import functools
import math

import jax
import jax.numpy as jnp
from jax import lax
from jax.experimental import pallas as pl
from jax.experimental.pallas import tpu as pltpu

F32 = jnp.float32
BF16 = jnp.bfloat16

D_MODEL = 2048
DEPTH = 2
DIFF_HEADS = 8
DIFF_HEAD_DIM = 64
DIFF_V_DIM = 128
GLA_HEADS = 4
GLA_DK = 128
GLA_DV = 256
GLA_GATE_RANK = 16
GLA_GATE_TEMP = 16.0
GLA_CHUNK = 64
SB_HEADS = 8
SB_HEAD_DIM = 128
S5_CHANNELS = 1024
S5_GROUP = 16
S5_GROUPS = 64
S5_STATE = 64
N_EXPERTS = 16
N_GROUPS = 4
EXPERTS_PER_GROUP = 4
D_EXPERT = 640
DEEPNORM_ALPHA = (2.0 * DEPTH) ** 0.25
LN_EPS = 1e-5

LANES = 128
SUBLANES = 8
VMEM_LIMIT = 56 * 1024 * 1024
SB_LOG_FLOOR = 120.0
SB_HEADS_PER_STEP = 4

MXU_N = 256
L0_GB_PAD = MXU_N
L0_COL = {"qa": 0, "ka": 1024, "va": 2048, "qb": 3072, "kb": 3584, "vb": 4096, "rb": 5120, "gb": 6144}
L0_IN_TN = 5 * MXU_N

S5_KT = 4
S5_KT_CH = S5_CHANNELS // S5_KT
S5_KT_STATES = (S5_GROUPS // S5_KT) * S5_STATE
S5_SCAN_W = 512
S5_L = 8


def _params(sem, flags=None):
    return pltpu.CompilerParams(dimension_semantics=sem, vmem_limit_bytes=VMEM_LIMIT, flags=flags)


def _nt_dot(a, b):
    return lax.dot_general(a, b, (((1,), (1,)), ((), ())), preferred_element_type=F32)


def _dot(a, b):
    return jnp.dot(a, b, preferred_element_type=F32)


def _split_dot_left(m, x):
    hi = x.astype(BF16)
    lo = (x - hi.astype(F32)).astype(BF16)
    return _dot(m, hi) + _dot(m, lo)


def _layer_norm(v, g, b):
    mu = jnp.mean(v, axis=-1, keepdims=True)
    d = v - mu
    var = jnp.mean(d * d, axis=-1, keepdims=True)
    return d * lax.rsqrt(var + LN_EPS) * g + b


def _mm_kernel(a_ref, b_ref, o_ref, a_bf):
    @pl.when(pl.program_id(1) == 0)
    def _():
        a_bf[...] = a_ref[...].astype(BF16)

    o_ref[...] = _dot(a_bf[...], b_ref[...]).astype(o_ref.dtype)


def _matmul(a, b, tm, tn, out_dtype):
    m, k = a.shape
    n = b.shape[1]
    tm = min(tm, m)
    return pl.pallas_call(
        _mm_kernel,
        grid=(m // tm, n // tn),
        in_specs=[pl.BlockSpec((tm, k), lambda i, j: (i, 0)),
                  pl.BlockSpec((k, tn), lambda i, j: (0, j))],
        out_specs=pl.BlockSpec((tm, tn), lambda i, j: (i, j)),
        out_shape=jax.ShapeDtypeStruct((m, n), out_dtype),
        scratch_shapes=[pltpu.VMEM((tm, k), BF16)],
        compiler_params=_params(("parallel", "arbitrary")),
        name="in_proj",
    )(a, b)


def _diff_attn_kernel(lam_ref, g_ref, q_ref, k_ref, v_ref, o_ref, vx_ref, m_ref, acc_ref, sa_ref, sb_ref, *, tq,
                      lambda_init):
    i = pl.program_id(1)
    dv = DIFF_V_DIM

    @pl.when(i == 0)
    def _():
        vx_ref[:, :dv] = v_ref[...]
        vx_ref[:, dv:] = jnp.ones((vx_ref.shape[0], dv), BF16)

    m_ref[...] = jnp.full(m_ref.shape, -jnp.inf, F32)
    acc_ref[...] = jnp.zeros(acc_ref.shape, F32)
    q = q_ref[...]
    lane = lax.broadcasted_iota(jnp.int32, q.shape, 1)
    zero = jnp.zeros_like(q)
    halves = (jnp.where(lane < DIFF_HEAD_DIM, q, zero), jnp.where(lane >= DIFF_HEAD_DIM, q, zero))

    def scores(j, mi):
        off = pl.multiple_of(j * tq, tq)
        return _nt_dot(halves[mi], k_ref[pl.ds(off, tq), :])

    def softmax_pv(j, get_s, masked):
        off = pl.multiple_of(j * tq, tq)
        vx = vx_ref[pl.ds(off, tq), :]
        if masked:
            row = lax.broadcasted_iota(jnp.int32, (tq, tq), 0)
            col = lax.broadcasted_iota(jnp.int32, (tq, tq), 1)
            causal = col <= row
        for mi in range(2):
            s = get_s(mi)
            if masked:
                s = jnp.where(causal, s, -jnp.inf)
            m_prev = m_ref[mi]
            m_new = jnp.maximum(m_prev, jnp.max(s, axis=1, keepdims=True))
            p = jnp.exp2(s - jnp.tile(m_new, (1, tq // LANES)))
            alpha = jnp.exp2(m_prev - m_new)
            acc_ref[mi] = jnp.tile(alpha, (1, 2)) * acc_ref[mi] + _dot(p.astype(BF16), vx)
            m_ref[mi] = m_new

    def scores_to(j, s_ref):
        for mi in range(2):
            s_ref[mi] = scores(j, mi)

    scores_to(0, sa_ref)

    def pair(t, carry):
        j = 2 * t
        scores_to(j + 1, sb_ref)
        softmax_pv(j, lambda mi: sa_ref[mi], False)
        scores_to(j + 2, sa_ref)
        softmax_pv(j + 1, lambda mi: sb_ref[mi], False)
        return carry

    lax.fori_loop(0, i // 2, pair, 0)

    @pl.when(i % 2 == 1)
    def _():
        softmax_pv(i - 1, lambda mi: sa_ref[mi], False)
        scores_to(i, sa_ref)

    softmax_pv(i, lambda mi: sa_ref[mi], True)

    lv = lam_ref[...]
    lam = (jnp.exp(jnp.sum(lv[0:1] * lv[1:2], axis=1, keepdims=True))
           - jnp.exp(jnp.sum(lv[2:3] * lv[3:4], axis=1, keepdims=True)) + lambda_init)
    a0 = acc_ref[0]
    a1 = acc_ref[1]
    o = a0[:, :dv] / a0[:, dv:] - lam * (a1[:, :dv] / a1[:, dv:])
    ms = jnp.mean(o * o, axis=-1, keepdims=True)
    o = o * lax.rsqrt(ms + LN_EPS) * g_ref[...] * (1.0 - lambda_init)
    o_ref[...] = o.astype(o_ref.dtype)


def _diff_attention(h0, lam_vec, subln_g, lambda_init, tq):
    t = h0.shape[0]
    tq = min(tq, t)
    nq = t // tq
    kern = functools.partial(_diff_attn_kernel, tq=tq, lambda_init=lambda_init)
    return pl.pallas_call(
        kern,
        grid=(DIFF_HEADS, nq),
        in_specs=[
            pl.BlockSpec((4, DIFF_HEAD_DIM), lambda h, i: (0, 0)),
            pl.BlockSpec((1, DIFF_V_DIM), lambda h, i: (0, 0)),
            pl.BlockSpec((tq, LANES), lambda h, i: (i, h)),
            pl.BlockSpec((t, LANES), lambda h, i: (0, DIFF_HEADS + h)),
            pl.BlockSpec((t, LANES), lambda h, i: (0, 2 * DIFF_HEADS + h)),
        ],
        out_specs=pl.BlockSpec((tq, LANES), lambda h, i: (i, h)),
        out_shape=jax.ShapeDtypeStruct((t, DIFF_HEADS * DIFF_V_DIM), BF16),
        scratch_shapes=[pltpu.VMEM((t, 2 * DIFF_V_DIM), BF16), pltpu.VMEM((2, tq, LANES), F32),
                        pltpu.VMEM((2, tq, 2 * DIFF_V_DIM), F32),
                        pltpu.VMEM((2, tq, tq), F32), pltpu.VMEM((2, tq, tq), F32)],
        compiler_params=_params(("arbitrary", "arbitrary")),
        name="diff_attn",
    )(lam_vec, subln_g, h0, h0, h0)


def _gla_kernel(q_ref, k_ref, v_ref, r_ref, gb_ref, wg_ref, bg_ref, ng_ref, o_ref, st_ref, *, tg):
    @pl.when(pl.program_id(0) == 0)
    def _():
        st_ref[...] = jnp.zeros(st_ref.shape, F32)

    c = GLA_CHUNK
    nch = tg // c
    gate = _dot(gb_ref[...], wg_ref[...]) + bg_ref[...]
    log_a = -(jnp.maximum(-gate, 0.0) + jnp.log(1.0 + jnp.exp(-jnp.abs(gate)))) / GLA_GATE_TEMP
    row = lax.broadcasted_iota(jnp.int32, (tg, tg), 0)
    col = lax.broadcasted_iota(jnp.int32, (tg, tg), 1)
    tril = jnp.logical_and(row >= col, row // c == col // c)
    tri = jnp.where(tril, 1.0, 0.0).astype(BF16)
    for h in range(GLA_HEADS):
        ksl = slice(h * GLA_DK, (h + 1) * GLA_DK)
        vsl = slice(h * GLA_DV, (h + 1) * GLA_DV)
        b = _split_dot_left(tri, log_a[:, ksl])
        b_last = jnp.concatenate(
            [jnp.broadcast_to(b[(ci + 1) * c - 1:(ci + 1) * c, :], (c, GLA_DK)) for ci in range(nch)], axis=0)
        qf = q_ref[:, ksl].astype(F32) * (GLA_DK ** -0.5)
        kf = k_ref[:, ksl].astype(F32)
        v = v_ref[:, vsl]
        q_dec = (qf * jnp.exp(b)).astype(BF16)
        k_dec = (kf * jnp.exp(-b)).astype(BF16)
        k_end = (kf * jnp.exp(b_last - b)).astype(BF16)
        scores = jnp.where(tril, _nt_dot(q_dec, k_dec), 0.0)
        o_intra = _dot(scores.astype(BF16), v)
        for ci in range(nch):
            rs = slice(ci * c, (ci + 1) * c)
            st = st_ref[h]
            o = o_intra[rs, :] + _nt_dot(q_dec[rs, :], st.astype(BF16))
            kv_t = lax.dot_general(v[rs, :], k_end[rs, :], (((0,), (0,)), ((), ())), preferred_element_type=F32)
            st_ref[h] = jnp.exp(b_last[rs, :][0:1, :]) * st + kv_t
            ms = jnp.mean(o * o, axis=-1, keepdims=True)
            o = o * lax.rsqrt(ms + LN_EPS) * ng_ref[...]
            r = r_ref[rs, vsl].astype(F32)
            o_ref[rs, vsl] = (o * (r / (1.0 + jnp.exp(-r)))).astype(o_ref.dtype)


def _gla(h0, wg_pad, b_gate, norm_g, tg):
    t = h0.shape[0]
    tg = min(tg, t)
    qk = GLA_HEADS * GLA_DK
    vw = GLA_HEADS * GLA_DV
    return pl.pallas_call(
        functools.partial(_gla_kernel, tg=tg),
        grid=(t // tg,),
        in_specs=[
            pl.BlockSpec((tg, qk), lambda i: (i, L0_COL["qb"] // qk)),
            pl.BlockSpec((tg, qk), lambda i: (i, L0_COL["kb"] // qk)),
            pl.BlockSpec((tg, vw), lambda i: (i, L0_COL["vb"] // vw)),
            pl.BlockSpec((tg, vw), lambda i: (i, L0_COL["rb"] // vw)),
            pl.BlockSpec((tg, LANES), lambda i: (i, L0_COL["gb"] // LANES)),
            pl.BlockSpec((LANES, qk), lambda i: (0, 0)),
            pl.BlockSpec((1, qk), lambda i: (0, 0)),
            pl.BlockSpec((1, GLA_DV), lambda i: (0, 0)),
        ],
        out_specs=pl.BlockSpec((tg, vw), lambda i: (i, 0)),
        out_shape=jax.ShapeDtypeStruct((t, vw), BF16),
        scratch_shapes=[pltpu.VMEM((GLA_HEADS, GLA_DV, GLA_DK), F32)],
        compiler_params=_params(("arbitrary",)),
        name="gla",
    )(h0, h0, h0, h0, h0, wg_pad, b_gate, norm_g)


def _sb_kernel(q_ref, k_ref, v_ref, o_ref, acc_ref, c_ref, *, tq):
    i = pl.program_id(1)
    hd = SB_HEAD_DIM
    acc_ref[...] = jnp.zeros(acc_ref.shape, F32)
    c_ref[...] = jnp.zeros(c_ref.shape, F32)
    row = lax.broadcasted_iota(jnp.int32, (tq, tq), 0)
    col = lax.broadcasted_iota(jnp.int32, (tq, tq), 1)
    strict = col < row
    tri = jnp.where(row > col, 1.0, 0.0).astype(BF16)

    def block(jb, masked):
        off = pl.multiple_of(jb * tq, tq)
        cmax = None
        for hh in range(SB_HEADS_PER_STEP):
            cols = slice(hh * hd, (hh + 1) * hd)
            k = k_ref[pl.ds(off, tq), cols]
            v = v_ref[pl.ds(off, tq), cols]
            z = _nt_dot(q_ref[:, cols], k)
            lsn = -(jnp.maximum(z, 0.0) + jnp.log(1.0 + jnp.exp(-jnp.abs(z))))
            lm = jnp.where(strict, lsn, 0.0) if masked else lsn
            excl = _dot(lm.astype(BF16), tri) + c_ref[hh]
            w = jnp.exp(z + lsn + excl)
            if masked:
                w = jnp.where(strict, w, 0.0)
            acc_ref[:, cols] += _dot(w.astype(BF16), v)
            c_new = c_ref[hh] + jnp.sum(lm, axis=1, keepdims=True)
            c_ref[hh] = c_new
            hmax = jnp.max(c_new)
            cmax = hmax if cmax is None else jnp.maximum(cmax, hmax)
        return cmax

    cmax = block(i, True)

    def cond(carry):
        jb, cm = carry
        return jnp.logical_and(jb >= 0, cm > -SB_LOG_FLOOR)

    def body(carry):
        jb, _ = carry
        return jb - 1, block(jb, False)

    lax.while_loop(cond, body, (i - 1, cmax))
    o_ref[...] = acc_ref[...].astype(o_ref.dtype)


def _stick_breaking(h1, tq):
    t = h1.shape[0]
    tq = min(tq, t)
    ng = SB_HEADS // SB_HEADS_PER_STEP
    wd = SB_HEADS_PER_STEP * SB_HEAD_DIM
    return pl.pallas_call(
        functools.partial(_sb_kernel, tq=tq),
        grid=(ng, t // tq),
        in_specs=[
            pl.BlockSpec((tq, wd), lambda h, i: (i, h)),
            pl.BlockSpec((t, wd), lambda h, i: (0, ng + h), pipeline_mode=pl.Buffered(1)),
            pl.BlockSpec((t, wd), lambda h, i: (0, 2 * ng + h), pipeline_mode=pl.Buffered(1)),
        ],
        out_specs=pl.BlockSpec((tq, wd), lambda h, i: (i, h)),
        out_shape=jax.ShapeDtypeStruct((t, SB_HEADS * SB_HEAD_DIM), BF16),
        scratch_shapes=[pltpu.VMEM((tq, wd), F32), pltpu.VMEM((SB_HEADS_PER_STEP, tq, 1), F32)],
        compiler_params=_params(("parallel", "arbitrary")),
        name="stick_breaking",
    )(h1, h1, h1)


def _s5_prep_kernel(are_ref, aim_ref, ls_ref, bre_ref, bim_ref, lbr_ref, lbi_ref, bbr_ref, bbi_ref):
    step = jnp.exp(ls_ref[...])
    lam_re = are_ref[...]
    lam_im = aim_ref[...]
    mag = jnp.exp(step * lam_re)
    lb_re = mag * jnp.cos(step * lam_im)
    lb_im = mag * jnp.sin(step * lam_im)
    den = lam_re * lam_re + lam_im * lam_im
    n_re = lb_re - 1.0
    f_re = (n_re * lam_re + lb_im * lam_im) / den
    f_im = (lb_im * lam_re - n_re * lam_im) / den
    lbr_ref[...] = lb_re
    lbi_ref[...] = lb_im
    bbr_ref[...] = f_re * bre_ref[...] - f_im * bim_ref[...]
    bbi_ref[...] = f_re * bim_ref[...] + f_im * bre_ref[...]


def _s5_prep(a_re, a_im, log_step, b_re, b_im):
    rows = S5_GROUPS * S5_GROUP
    rep = lambda p: jnp.repeat(p, S5_GROUP, axis=0)
    to_rows = lambda b: jnp.transpose(b, (0, 2, 1)).reshape(rows, S5_STATE)
    spec = pl.BlockSpec((rows, S5_STATE), lambda: (0, 0))
    shp = jax.ShapeDtypeStruct((rows, S5_STATE), F32)
    return pl.pallas_call(
        _s5_prep_kernel,
        in_specs=[spec, spec, pl.BlockSpec((rows, 1), lambda: (0, 0)), spec, spec],
        out_specs=[spec] * 4,
        out_shape=[shp] * 4,
        name="s5_prep",
    )(rep(a_re), rep(a_im), rep(log_step[:, None]), to_rows(b_re), to_rows(b_im))


def _cmul(ar, ai, br, bi):
    return ar * br - ai * bi, ar * bi + ai * br


def _block_diag_tiles(m, rows_per_group, cols_per_group):
    gpt = S5_GROUPS // S5_KT
    m = m.reshape(S5_KT, gpt, rows_per_group, cols_per_group)
    eye = jnp.eye(gpt, dtype=m.dtype)
    bd = jnp.einsum("kgrc,gh->kgrhc", m, eye)
    return bd.reshape(S5_KT, gpt * rows_per_group, gpt * cols_per_group)


def _s5_tile_params(lb_re, lb_im, bb_re, bb_im, c_re, c_im):
    bd = jnp.concatenate([_block_diag_tiles(bb_re, S5_GROUP, S5_STATE),
                          _block_diag_tiles(bb_im, S5_GROUP, S5_STATE)], axis=2)
    cd = jnp.concatenate([_block_diag_tiles(jnp.transpose(c_re, (0, 2, 1)), S5_STATE, S5_GROUP),
                          _block_diag_tiles(-jnp.transpose(c_im, (0, 2, 1)), S5_STATE, S5_GROUP)], axis=1)
    pr = [jnp.ones_like(lb_re)]
    pi = [jnp.zeros_like(lb_re)]
    for _ in range(S5_L):
        nr, ni = _cmul(pr[-1], pi[-1], lb_re, lb_im)
        pr.append(nr)
        pi.append(ni)

    def table(p):
        rows = jnp.transpose(jnp.stack(p).reshape(S5_L + 1, S5_KT, S5_KT_STATES), (1, 0, 2))
        return jnp.pad(rows, ((0, 0), (0, LANES - (S5_L + 1)), (0, 0)))

    return bd, cd, jnp.stack([table(pr), table(pi)], axis=1)


def _s5_core_kernel(u_ref, bd_ref, cd_ref, pw8_ref, y_ref, hs_ref, hp_ref, pw_ref, st_ref, uf_ref,
                    mt_s, bbs_s, ccs_s, *, tcr):
    ns = S5_KT_STATES
    pad = SUBLANES
    cw = S5_KT_CH

    @pl.when(pl.program_id(1) == 0)
    def _():
        b_re = bd_ref[:, :ns]
        b_im = bd_ref[:, ns:]
        cd_hi = cd_ref[...].astype(BF16)
        cd_lo = (cd_ref[...] - cd_hi.astype(F32)).astype(BF16)
        for k in range(S5_L):
            a = pw8_ref[0, k:k + 1, :]
            b = pw8_ref[1, k:k + 1, :]
            dk = jnp.concatenate([b_re * a - b_im * b, b_re * b + b_im * a], axis=1)
            s = S5_L - 1 - k
            bbs_s[s * cw:(s + 1) * cw, :] = dk.astype(BF16)
            dk_hi = dk.astype(BF16)
            dk_lo = (dk - dk_hi.astype(F32)).astype(BF16)
            kk = (_dot(dk_hi, cd_hi) + (_dot(dk_lo, cd_hi) + _dot(dk_hi, cd_lo))).astype(BF16)
            for s0 in range(S5_L - k):
                mt_s[s0 * cw:(s0 + 1) * cw, (s0 + k) * cw:(s0 + k + 1) * cw] = kk
        pw_t = (jnp.transpose(pw8_ref[0]), jnp.transpose(pw8_ref[1]))
        c_re = cd_ref[:ns, :]
        c_im = cd_ref[ns:, :]
        for i in range(S5_L):
            a = pw_t[0][:, i + 1:i + 2]
            b = pw_t[1][:, i + 1:i + 2]
            ccs_s[:ns, i * cw:(i + 1) * cw] = (c_re * a + c_im * b).astype(BF16)
            ccs_s[ns:, i * cw:(i + 1) * cw] = (c_im * a - c_re * b).astype(BF16)

        hp_ref[...] = jnp.zeros(hp_ref.shape, F32)
        hs_ref[0:pad, :] = jnp.zeros((pad, 2 * ns), F32)
        l1r = jnp.broadcast_to(pw8_ref[0, S5_L:S5_L + 1, :], (SUBLANES, ns))
        l1i = jnp.broadcast_to(pw8_ref[1, S5_L:S5_L + 1, :], (SUBLANES, ns))
        l2r, l2i = _cmul(l1r, l1i, l1r, l1i)
        l4r, l4i = _cmul(l2r, l2i, l2r, l2i)
        l8r, l8i = _cmul(l4r, l4i, l4r, l4i)
        ridx = lax.broadcasted_iota(jnp.int32, l1r.shape, 0)
        n = ridx + 1
        pr = jnp.ones_like(l1r)
        pi = jnp.zeros_like(l1r)
        for bit, (lr, li) in ((1, (l1r, l1i)), (2, (l2r, l2i)), (4, (l4r, l4i)), (8, (l8r, l8i))):
            nr, ni = _cmul(pr, pi, lr, li)
            take = (n & bit) != 0
            pr = jnp.where(take, nr, pr)
            pi = jnp.where(take, ni, pi)
        pw_ref[0] = pr
        pw_ref[1] = pi
        for s, (lr, li) in enumerate(((l1r, l1i), (l2r, l2i), (l4r, l4i))):
            keep = ridx >= (1 << s)
            st_ref[2 * s] = jnp.where(keep, lr, 0.0)
            st_ref[2 * s + 1] = jnp.where(keep, li, 0.0)

    nl = S5_KT_CH // LANES
    for h in range(nl):
        uf_ref[h] = u_ref[:, h * LANES:(h + 1) * LANES].astype(F32)
    uc = jnp.concatenate([uf_ref[h, pl.ds(s, tcr, stride=S5_L), :].astype(BF16)
                          for s in range(S5_L) for h in range(nl)], axis=1)
    hs_ref[pad:pad + tcr, :] = _dot(uc, bbs_s[...])
    w = S5_SCAN_W
    for cc in range(ns // w):
        re0 = cc * w
        im0 = ns + cc * w

        def body(r, carry, re0=re0, im0=im0):
            hpr, hpi = carry
            rows = pl.ds(pl.multiple_of(pad + r * SUBLANES, SUBLANES), SUBLANES)
            xr = hs_ref[rows, re0:re0 + w]
            xi = hs_ref[rows, im0:im0 + w]
            for s in range(3):
                sr = pltpu.roll(xr, 1 << s, 0)
                si = pltpu.roll(xi, 1 << s, 0)
                ar = st_ref[2 * s, :, re0:re0 + w]
                ai = st_ref[2 * s + 1, :, re0:re0 + w]
                xr, xi = xr + ar * sr - ai * si, xi + ar * si + ai * sr
            pr = pw_ref[0, :, re0:re0 + w]
            pi = pw_ref[1, :, re0:re0 + w]
            xr, xi = xr + pr * hpr - pi * hpi, xi + pr * hpi + pi * hpr
            hs_ref[rows, re0:re0 + w] = xr
            hs_ref[rows, im0:im0 + w] = xi
            return (jnp.broadcast_to(xr[SUBLANES - 1:SUBLANES, :], (SUBLANES, w)),
                    jnp.broadcast_to(xi[SUBLANES - 1:SUBLANES, :], (SUBLANES, w)))

        hpr, hpi = lax.fori_loop(0, tcr // SUBLANES, body, (hp_ref[:, re0:re0 + w], hp_ref[:, im0:im0 + w]))
        hp_ref[:, re0:re0 + w] = hpr
        hp_ref[:, im0:im0 + w] = hpi

    h_prev = hs_ref[pad - 1:pad - 1 + tcr, :]
    y_in = _dot(h_prev.astype(BF16), ccs_s[...])
    for i in range(S5_L):
        k_hi = (i + 1) * cw
        y_i = y_in[:, i * cw:k_hi] + _dot(uc[:, :k_hi], mt_s[:k_hi, i * cw:k_hi])
        for h in range(nl):
            uf_ref[h, pl.ds(i, tcr, stride=S5_L), :] = y_i[:, h * LANES:(h + 1) * LANES]
    for h in range(nl):
        y_ref[:, h * LANES:(h + 1) * LANES] = uf_ref[h]
    hs_ref[pad - 1:pad, :] = hs_ref[pad + tcr - 1:pad + tcr, :]


def _s5_core(h1, bd, cd, pw8, tcr):
    t = h1.shape[0]
    width = S5_L * S5_KT_CH
    tcr = min(tcr, t // S5_L)
    tc = tcr * S5_L
    ns = S5_KT_STATES
    u_col0 = 3 * S5_CHANNELS // S5_KT_CH
    once = pl.Buffered(1)
    return pl.pallas_call(
        functools.partial(_s5_core_kernel, tcr=tcr),
        grid=(S5_KT, t // tc),
        in_specs=[
            pl.BlockSpec((tc, S5_KT_CH), lambda k, i: (i, u_col0 + k)),
            pl.BlockSpec((None, S5_KT_CH, 2 * ns), lambda k, i: (k, 0, 0), pipeline_mode=once),
            pl.BlockSpec((None, 2 * ns, S5_KT_CH), lambda k, i: (k, 0, 0), pipeline_mode=once),
            pl.BlockSpec((None, 2, LANES, ns), lambda k, i: (k, 0, 0, 0), pipeline_mode=once),
        ],
        out_specs=pl.BlockSpec((tc, S5_KT_CH), lambda k, i: (i, k)),
        out_shape=jax.ShapeDtypeStruct((t, S5_CHANNELS), F32),
        scratch_shapes=[pltpu.VMEM((tcr + SUBLANES, 2 * ns), F32), pltpu.VMEM((SUBLANES, 2 * ns), F32),
                        pltpu.VMEM((2, SUBLANES, ns), F32), pltpu.VMEM((6, SUBLANES, ns), F32),
                        pltpu.VMEM((S5_KT_CH // LANES, tc, LANES), F32),
                        pltpu.VMEM((width, width), BF16), pltpu.VMEM((width, 2 * ns), BF16),
                        pltpu.VMEM((2 * ns, width), BF16)],
        compiler_params=_params(("arbitrary", "arbitrary")),
        name="s5_core",
    )(h1, bd, cd, pw8)


def _s5_glu_kernel(y_ref, u_ref, d_ref, wglu_ref, bglu_ref, o_ref):
    y = y_ref[...] + d_ref[...] * u_ref[...].astype(F32)
    g = 0.5 * y * (1.0 + jnp.tanh(math.sqrt(2.0 / math.pi) * (y + 0.044715 * (y * y * y))))
    gl = _dot(g.astype(BF16), wglu_ref[...]) + bglu_ref[...]
    o_ref[...] = (g / (1.0 + jnp.exp(-gl))).astype(o_ref.dtype)


def _s5_glu(y, h1, d_skip, w_glu, b_glu, tm):
    t = h1.shape[0]
    tm = min(tm, t)
    full = lambda shape: pl.BlockSpec(shape, lambda i: (0,) * len(shape))
    return pl.pallas_call(
        _s5_glu_kernel,
        grid=(t // tm,),
        in_specs=[pl.BlockSpec((tm, S5_CHANNELS), lambda i: (i, 0)),
                  pl.BlockSpec((tm, S5_CHANNELS), lambda i: (i, 3)),
                  full((1, S5_CHANNELS)), full((S5_CHANNELS, S5_CHANNELS)), full((1, S5_CHANNELS))],
        out_specs=pl.BlockSpec((tm, S5_CHANNELS), lambda i: (i, 0)),
        out_shape=jax.ShapeDtypeStruct((t, S5_CHANNELS), BF16),
        compiler_params=_params(("parallel",)),
        name="s5_glu",
    )(y, h1, d_skip, w_glu, b_glu)


def _route(probs, sel):
    rows = [sel[e:e + 1, :] for e in range(N_EXPERTS)]
    prow = [probs[e:e + 1, :] for e in range(N_EXPERTS)]
    best_score = None
    for g in range(N_GROUPS):
        a = rows[g * EXPERTS_PER_GROUP:(g + 1) * EXPERTS_PER_GROUP]
        score = None
        for x in range(EXPERTS_PER_GROUP):
            for y in range(x + 1, EXPERTS_PER_GROUP):
                pair = a[x] + a[y]
                score = pair if score is None else jnp.maximum(score, pair)
        if best_score is None:
            best_score, gbest = score, jnp.zeros(score.shape, jnp.int32)
            cs = list(a)
            cp = prow[0:EXPERTS_PER_GROUP]
        else:
            better = score > best_score
            best_score = jnp.where(better, score, best_score)
            gbest = jnp.where(better, g, gbest)
            cs = [jnp.where(better, a[x], cs[x]) for x in range(EXPERTS_PER_GROUP)]
            cp = [jnp.where(better, prow[g * EXPERTS_PER_GROUP + x], cp[x]) for x in range(EXPERTS_PER_GROUP)]

    def first_argmax(vals):
        bv, bi = vals[0], jnp.zeros(vals[0].shape, jnp.int32)
        for x in range(1, len(vals)):
            better = vals[x] > bv
            bv = jnp.where(better, vals[x], bv)
            bi = jnp.where(better, x, bi)
        return bi

    i1 = first_argmax(cs)
    i2 = first_argmax([jnp.where(i1 == x, -jnp.inf, cs[x]) for x in range(EXPERTS_PER_GROUP)])

    def pick(idx):
        out = cp[0]
        for x in range(1, EXPERTS_PER_GROUP):
            out = jnp.where(idx == x, cp[x], out)
        return out

    p1, p2 = pick(i1), pick(i2)
    tot = p1 + p2
    return gbest * EXPERTS_PER_GROUP + i1, gbest * EXPERTS_PER_GROUP + i2, p1 / tot, p2 / tot


def _ln_router_tail(v, g_ref, b_ref, rwh_ref, rwl_ref, rb_ref):
    xn = _layer_norm(v, g_ref[...], b_ref[...])
    xh = xn.astype(BF16)
    xl = (xn - xh.astype(F32)).astype(BF16)
    logits = _nt_dot(rwh_ref[...], xh) + (_nt_dot(rwl_ref[...], xh) + _nt_dot(rwh_ref[...], xl))
    mx = jnp.max(logits, axis=0, keepdims=True)
    ex = jnp.exp(logits - mx)
    probs = ex / jnp.sum(ex, axis=0, keepdims=True)
    e1, e2, w1, w2 = _route(probs, probs + rb_ref[...])
    n = e1.shape[1]
    ei = jnp.concatenate([e1, e2, jnp.zeros((SUBLANES - 2, n), jnp.int32)], axis=0)
    ew = jnp.concatenate([w1, w2, jnp.zeros((SUBLANES - 2, n), F32)], axis=0)
    return xn, ei, ew


def _outproj_kernel(ma_ref, mb_ref, w_ref, xr_ref, g_ref, b_ref, rwh_ref, rwl_ref, rb_ref,
                    x_ref, ei_ref, ew_ref, w_bf, *, sub):
    @pl.when(pl.program_id(0) == 0)
    def _():
        w_bf[...] = w_ref[...].astype(BF16)

    half = w_bf.shape[0] // 2
    for s in range(x_ref.shape[0] // sub):
        rows = slice(s * sub, (s + 1) * sub)
        v = (DEEPNORM_ALPHA * xr_ref[rows, :] + _dot(ma_ref[rows, :], w_bf[:half, :])
             + _dot(mb_ref[rows, :], w_bf[half:, :]))
        xn, ei, ew = _ln_router_tail(v, g_ref, b_ref, rwh_ref, rwl_ref, rb_ref)
        x_ref[rows, :] = xn
        ei_ref[:, rows] = ei
        ew_ref[:, rows] = ew


def _outproj_ln_router(mix_a, mix_b, w_out, x_res, ln_g, ln_b, rwt, rb, tm, sub):
    t = x_res.shape[0]
    rwt_hi = rwt.astype(BF16)
    rwt_lo = (rwt - rwt_hi.astype(F32)).astype(BF16)
    tm = min(tm, t)
    sub = min(sub, tm)
    half = w_out.shape[0] // 2
    row = lambda w: pl.BlockSpec((tm, w), lambda i: (i, 0))
    full = lambda shape: pl.BlockSpec(shape, lambda i: (0,) * len(shape))
    return pl.pallas_call(
        functools.partial(_outproj_kernel, sub=sub),
        grid=(t // tm,),
        in_specs=[row(half), row(half),
                  pl.BlockSpec((2 * half, D_MODEL), lambda i: (0, 0), pipeline_mode=pl.Buffered(1)),
                  row(D_MODEL), full((1, D_MODEL)), full((1, D_MODEL)),
                  full((N_EXPERTS, D_MODEL)), full((N_EXPERTS, D_MODEL)), full((N_EXPERTS, 1))],
        out_specs=[row(D_MODEL),
                   pl.BlockSpec((SUBLANES, tm), lambda i: (0, i)), pl.BlockSpec((SUBLANES, tm), lambda i: (0, i))],
        out_shape=[jax.ShapeDtypeStruct((t, D_MODEL), F32),
                   jax.ShapeDtypeStruct((SUBLANES, t), jnp.int32), jax.ShapeDtypeStruct((SUBLANES, t), F32)],
        scratch_shapes=[pltpu.VMEM((2 * half, D_MODEL), BF16)],
        compiler_params=_params(("arbitrary",)),
        name="outproj_ln_router",
    )(mix_a, mix_b, w_out, x_res, ln_g, ln_b, rwt_hi, rwt_lo, rb)


def _rank_kernel(e_ref, rank_ref, cnt_ref, base_ref, *, tn):
    @pl.when(pl.program_id(0) == 0)
    def _():
        base_ref[...] = jnp.zeros(base_ref.shape, F32)

    e = e_ref[...]
    rows = lax.broadcasted_iota(jnp.int32, (N_EXPERTS, tn), 0)
    oh1 = rows == e[0:1, :]
    oh2 = rows == e[1:2, :]
    oh = jnp.where(jnp.logical_or(oh1, oh2), 1.0, 0.0)
    r = lax.broadcasted_iota(jnp.int32, (tn, tn), 0)
    c = lax.broadcasted_iota(jnp.int32, (tn, tn), 1)
    earlier = jnp.where(r < c, 1.0, 0.0).astype(BF16)
    base = base_ref[...]
    before = _dot(oh.astype(BF16), earlier) + jnp.tile(base, (1, tn // LANES))
    r1 = jnp.sum(jnp.where(oh1, before, 0.0), axis=0, keepdims=True)
    r2 = jnp.sum(jnp.where(oh2, before, 0.0), axis=0, keepdims=True)
    rank_ref[...] = jnp.concatenate([r1, r2, jnp.zeros((SUBLANES - 2, tn), F32)], axis=0).astype(jnp.int32)
    base = base + jnp.sum(oh, axis=1, keepdims=True)
    base_ref[...] = base
    cnt_ref[...] = base.astype(jnp.int32)


def _rank(eidx, tn):
    t = eidx.shape[1]
    tn = min(tn, t)
    return pl.pallas_call(
        functools.partial(_rank_kernel, tn=tn),
        grid=(t // tn,),
        in_specs=[pl.BlockSpec((SUBLANES, tn), lambda i: (0, i))],
        out_specs=[pl.BlockSpec((SUBLANES, tn), lambda i: (0, i)),
                   pl.BlockSpec((N_EXPERTS, LANES), lambda i: (0, 0))],
        out_shape=[jax.ShapeDtypeStruct((SUBLANES, t), jnp.int32),
                   jax.ShapeDtypeStruct((N_EXPERTS, LANES), jnp.int32)],
        scratch_shapes=[pltpu.VMEM((N_EXPERTS, LANES), F32)],
        compiler_params=_params(("arbitrary",)),
        name="moe_rank",
    )(eidx)


def _moe_kernel(te_ref, nu_ref, xs_ref, wg_ref, wu_ref, wd_ref, y_ref, wgu_s, wd_s):
    tile = pl.program_id(0)
    used = tile < nu_ref[0]
    fresh = jnp.logical_or(tile == 0, te_ref[tile] != te_ref[jnp.maximum(tile - 1, 0)])

    @pl.when(jnp.logical_and(used, fresh))
    def _():
        wgu_s[:, :D_EXPERT] = wg_ref[...].astype(BF16)
        wgu_s[:, D_EXPERT:] = wu_ref[...].astype(BF16)
        wd_s[...] = wd_ref[...].astype(BF16)

    @pl.when(used)
    def _():
        hgu = _dot(xs_ref[...].astype(BF16), wgu_s[...])
        hg = hgu[:, :D_EXPERT]
        hu = hgu[:, D_EXPERT:]
        h = (hg / (1.0 + jnp.exp(-hg))) * hu
        y_ref[...] = _dot(h.astype(BF16), wd_s[...])

    @pl.when(jnp.logical_not(used))
    def _():
        y_ref[...] = jnp.zeros(y_ref.shape, F32)


def _moe(tile_expert, n_used, xs, w_gate, w_up, w_down, tm):
    t_pad = xs.shape[0]
    grid_spec = pltpu.PrefetchScalarGridSpec(
        num_scalar_prefetch=2,
        grid=(t_pad // tm,),
        in_specs=[
            pl.BlockSpec((tm, D_MODEL), lambda i, te, nu: (i, 0)),
            pl.BlockSpec((None, D_MODEL, D_EXPERT), lambda i, te, nu: (te[i], 0, 0)),
            pl.BlockSpec((None, D_MODEL, D_EXPERT), lambda i, te, nu: (te[i], 0, 0)),
            pl.BlockSpec((None, D_EXPERT, D_MODEL), lambda i, te, nu: (te[i], 0, 0)),
        ],
        out_specs=pl.BlockSpec((tm, D_MODEL), lambda i, te, nu: (i, 0)),
        scratch_shapes=[pltpu.VMEM((D_MODEL, 2 * D_EXPERT), BF16), pltpu.VMEM((D_EXPERT, D_MODEL), BF16)],
    )
    return pl.pallas_call(
        _moe_kernel,
        grid_spec=grid_spec,
        out_shape=jax.ShapeDtypeStruct((t_pad, D_MODEL), F32),
        compiler_params=_params(("arbitrary",)),
        name="moe",
    )(tile_expert, n_used, xs, w_gate, w_up, w_down)


def _moe_ln_kernel(x_ref, y1_ref, y2_ref, w_ref, g_ref, b_ref, o_ref):
    w = w_ref[...]
    v = DEEPNORM_ALPHA * x_ref[...] + (w[:, 0:1] * y1_ref[...] + w[:, 1:2] * y2_ref[...])
    o_ref[...] = _layer_norm(v, g_ref[...], b_ref[...])


def _moe_ln(x, y12, w_cols, ln_g, ln_b, tm):
    t = x.shape[0]
    tm = min(tm, t)
    nb = t // tm
    row = pl.BlockSpec((tm, D_MODEL), lambda i: (i, 0))
    second = pl.BlockSpec((tm, D_MODEL), lambda i: (nb + i, 0))
    vec = pl.BlockSpec((1, D_MODEL), lambda i: (0, 0))
    return pl.pallas_call(
        _moe_ln_kernel,
        grid=(nb,),
        in_specs=[row, row, second, pl.BlockSpec((tm, SUBLANES), lambda i: (i, 0)), vec, vec],
        out_specs=row,
        out_shape=jax.ShapeDtypeStruct((t, D_MODEL), F32),
        compiler_params=_params(("parallel",)),
        name="moe_ln",
    )(x, y12, y12, w_cols, ln_g, ln_b)


def _moe_block(x, eidx, ew, w_gate, w_up, w_down, ln_g, ln_b, tm):
    t = x.shape[0]
    tm = min(tm, t)
    rank, cnt = _rank(eidx, 512)
    counts = cnt[:, 0]
    padded = ((counts + tm - 1) // tm) * tm
    ends = jnp.cumsum(padded)
    starts = ends - padded
    e2 = eidx[:2]
    start_of = jnp.zeros_like(e2)
    for k in range(N_EXPERTS):
        start_of = jnp.where(e2 == k, starts[k], start_of)
    dest = start_of + rank[:2]
    t_pad = 2 * t + N_EXPERTS * tm
    tok = jnp.tile(jnp.arange(t, dtype=jnp.int32), 2)
    src_tok = (jnp.arange(t_pad, dtype=jnp.int32) % t).at[dest.reshape(2 * t)].set(
        tok, mode="promise_in_bounds", unique_indices=True)
    tile_start = jnp.arange(t_pad // tm, dtype=jnp.int32) * tm
    tile_expert = jnp.minimum(jnp.sum((ends[None, :] <= tile_start[:, None]).astype(jnp.int32), axis=1),
                              N_EXPERTS - 1)
    n_used = (ends[-1] // tm).astype(jnp.int32).reshape(1)
    xs = x.at[src_tok].get(mode="promise_in_bounds")
    ys = _moe(tile_expert, n_used, xs, w_gate, w_up, w_down, tm)
    y12 = ys.at[dest.reshape(2 * t)].get(mode="promise_in_bounds")
    return _moe_ln(x, y12, ew.T, ln_g, ln_b, tm)


def kernel(x, router_w, router_b, l0_w_in, l0_diff_lam_q1, l0_diff_lam_k1, l0_diff_lam_q2, l0_diff_lam_k2, l0_diff_subln_g, l0_gla_w_gate, l0_gla_b_gate, l0_gla_norm_g, l0_w_out, l0_ln1_g, l0_ln1_b, l0_moe_w_gate, l0_moe_w_up, l0_moe_w_down, l0_ln2_g, l0_ln2_b, l1_w_in, l1_s5_a_re, l1_s5_a_im, l1_s5_log_step, l1_s5_b_re, l1_s5_b_im, l1_s5_c_re, l1_s5_c_im, l1_s5_d, l1_s5_w_glu, l1_s5_b_glu, l1_w_out, l1_ln1_g, l1_ln1_b, l1_moe_w_gate, l1_moe_w_up, l1_moe_w_down, l1_ln2_g, l1_ln2_b):
    bsz, t, d = x.shape
    assert bsz == 1 and d == D_MODEL
    x0 = x.reshape(t, d)
    vec = lambda p: p.reshape(1, -1).astype(F32)
    rwt = router_w.T.astype(F32)
    rb = router_b.reshape(N_EXPERTS, 1).astype(F32)

    lambda_init = 0.8 - 0.6 * math.exp(-0.3 * 0)
    diff_w = DIFF_HEADS * DIFF_V_DIM
    gla_qk, gla_v = GLA_HEADS * GLA_DK, GLA_HEADS * GLA_DV
    c = [0]
    for width in (diff_w, diff_w, diff_w, gla_qk, gla_qk, gla_v, GLA_GATE_RANK, gla_v):
        c.append(c[-1] + width)
    q_scale = math.log2(math.e) * DIFF_HEAD_DIM ** -0.5
    w0 = jnp.concatenate([l0_w_in[:, :c[1]] * q_scale, l0_w_in[:, c[1]:c[6]], l0_w_in[:, c[7]:c[8]],
                          l0_w_in[:, c[6]:c[7]], jnp.zeros((d, L0_GB_PAD - GLA_GATE_RANK), l0_w_in.dtype)],
                         axis=1).astype(BF16)
    h0 = _matmul(x0, w0, 1024, L0_IN_TN, BF16)
    lam_vec = jnp.stack([l0_diff_lam_q1, l0_diff_lam_k1, l0_diff_lam_q2, l0_diff_lam_k2]).astype(F32)
    o_a = _diff_attention(h0, lam_vec, vec(l0_diff_subln_g), lambda_init, 1024)
    wg_pad = jnp.concatenate([l0_gla_w_gate, jnp.zeros((LANES - GLA_GATE_RANK, GLA_HEADS * GLA_DK), F32)],
                             axis=0).astype(BF16)
    o_b = _gla(h0, wg_pad, vec(l0_gla_b_gate), vec(l0_gla_norm_g), 256)
    x1, eidx, ew = _outproj_ln_router(o_a, o_b, l0_w_out, x0, vec(l0_ln1_g), vec(l0_ln1_b),
                                      rwt, rb, 512, 128)
    x2 = _moe_block(x1, eidx, ew, l0_moe_w_gate, l0_moe_w_up, l0_moe_w_down, vec(l0_ln2_g), vec(l0_ln2_b), 256)

    sb_w = SB_HEADS * SB_HEAD_DIM
    w1 = jnp.concatenate([l1_w_in[:, :sb_w] * SB_HEAD_DIM ** -0.5, l1_w_in[:, sb_w:]], axis=1).astype(BF16)
    h1 = _matmul(x2, w1, 1024, 1024, BF16)
    o_c = _stick_breaking(h1, 256)
    lbr, lbi, bbr, bbi = _s5_prep(l1_s5_a_re, l1_s5_a_im, l1_s5_log_step, l1_s5_b_re, l1_s5_b_im)
    bd, cd, pw8 = _s5_tile_params(lbr[::S5_GROUP], lbi[::S5_GROUP], bbr.reshape(S5_GROUPS, S5_GROUP, S5_STATE),
                                  bbi.reshape(S5_GROUPS, S5_GROUP, S5_STATE),
                                  l1_s5_c_re.astype(F32), l1_s5_c_im.astype(F32))
    y_s5 = _s5_core(h1, bd, cd, pw8, 256)
    o_d = _s5_glu(y_s5, h1, vec(l1_s5_d), l1_s5_w_glu.astype(BF16), vec(l1_s5_b_glu), 512)
    x3, eidx1, ew1 = _outproj_ln_router(o_c, o_d, l1_w_out, x2, vec(l1_ln1_g), vec(l1_ln1_b),
                                        rwt, rb, 512, 128)
    x4 = _moe_block(x3, eidx1, ew1, l1_moe_w_gate, l1_moe_w_up, l1_moe_w_down, vec(l1_ln2_g), vec(l1_ln2_b), 256)
    return x4.reshape(bsz, t, d)
```

```python
import functools
import math

import jax
import jax.numpy as jnp
from jax import lax
from jax.experimental import pallas as pl
from jax.experimental.pallas import tpu as pltpu

F32 = jnp.float32
BF16 = jnp.bfloat16

D_MODEL = 2048
DEPTH = 2
DIFF_HEADS = 8
DIFF_HEAD_DIM = 64
DIFF_V_DIM = 128
GLA_HEADS = 4
GLA_DK = 128
GLA_DV = 256
GLA_GATE_RANK = 16
GLA_GATE_TEMP = 16.0
GLA_CHUNK = 64
SB_HEADS = 8
SB_HEAD_DIM = 128
S5_CHANNELS = 1024
S5_GROUP = 16
S5_GROUPS = 64
S5_STATE = 64
N_EXPERTS = 16
N_GROUPS = 4
EXPERTS_PER_GROUP = 4
D_EXPERT = 640
DEEPNORM_ALPHA = (2.0 * DEPTH) ** 0.25
LN_EPS = 1e-5

LANES = 128
SUBLANES = 8
VMEM_LIMIT = 56 * 1024 * 1024
SB_LOG_FLOOR = 120.0
SB_HEADS_PER_STEP = 2

MXU_N = 256
L0_GB_PAD = MXU_N
L0_COL = {"qa": 0, "ka": 1024, "va": 2048, "qb": 3072, "kb": 3584, "vb": 4096, "rb": 5120, "gb": 6144}
L0_IN_TN = 5 * MXU_N

S5_KT = 4
S5_KT_CH = S5_CHANNELS // S5_KT
S5_KT_STATES = (S5_GROUPS // S5_KT) * S5_STATE
S5_SCAN_W = 512
S5_L = 8


def _params(sem, flags=None):
    return pltpu.CompilerParams(dimension_semantics=sem, vmem_limit_bytes=VMEM_LIMIT, flags=flags)


def _nt_dot(a, b):
    return lax.dot_general(a, b, (((1,), (1,)), ((), ())), preferred_element_type=F32)


def _dot(a, b):
    return jnp.dot(a, b, preferred_element_type=F32)


def _split_dot_left(m, x):
    hi = x.astype(BF16)
    lo = (x - hi.astype(F32)).astype(BF16)
    return _dot(m, hi) + _dot(m, lo)


def _layer_norm(v, g, b):
    mu = jnp.mean(v, axis=-1, keepdims=True)
    d = v - mu
    var = jnp.mean(d * d, axis=-1, keepdims=True)
    return d * lax.rsqrt(var + LN_EPS) * g + b


def _mm_kernel(a_ref, b_ref, o_ref, a_bf):
    @pl.when(pl.program_id(1) == 0)
    def _():
        a_bf[...] = a_ref[...].astype(BF16)

    o_ref[...] = _dot(a_bf[...], b_ref[...]).astype(o_ref.dtype)


def _matmul(a, b, tm, tn, out_dtype):
    m, k = a.shape
    n = b.shape[1]
    tm = min(tm, m)
    return pl.pallas_call(
        _mm_kernel,
        grid=(m // tm, n // tn),
        in_specs=[pl.BlockSpec((tm, k), lambda i, j: (i, 0)),
                  pl.BlockSpec((k, tn), lambda i, j: (0, j))],
        out_specs=pl.BlockSpec((tm, tn), lambda i, j: (i, j)),
        out_shape=jax.ShapeDtypeStruct((m, n), out_dtype),
        scratch_shapes=[pltpu.VMEM((tm, k), BF16)],
        compiler_params=_params(("parallel", "arbitrary")),
        name="in_proj",
    )(a, b)


def _diff_attn_kernel(lam_ref, g_ref, q_ref, k_ref, v_ref, o_ref, vx_ref, m_ref, acc_ref, sa_ref, sb_ref, *, tq,
                      lambda_init):
    i = pl.program_id(1)
    dv = DIFF_V_DIM

    @pl.when(i == 0)
    def _():
        vx_ref[:, :dv] = v_ref[...]
        vx_ref[:, dv:] = jnp.ones((vx_ref.shape[0], dv), BF16)

    m_ref[...] = jnp.full(m_ref.shape, -jnp.inf, F32)
    acc_ref[...] = jnp.zeros(acc_ref.shape, F32)
    q = q_ref[...]
    lane = lax.broadcasted_iota(jnp.int32, q.shape, 1)
    zero = jnp.zeros_like(q)
    halves = (jnp.where(lane < DIFF_HEAD_DIM, q, zero), jnp.where(lane >= DIFF_HEAD_DIM, q, zero))

    def scores(j, mi):
        off = pl.multiple_of(j * tq, tq)
        return _nt_dot(halves[mi], k_ref[pl.ds(off, tq), :])

    def softmax_pv(j, get_s, masked):
        off = pl.multiple_of(j * tq, tq)
        vx = vx_ref[pl.ds(off, tq), :]
        if masked:
            row = lax.broadcasted_iota(jnp.int32, (tq, tq), 0)
            col = lax.broadcasted_iota(jnp.int32, (tq, tq), 1)
            causal = col <= row
        for mi in range(2):
            s = get_s(mi)
            if masked:
                s = jnp.where(causal, s, -jnp.inf)
            m_prev = m_ref[mi]
            m_new = jnp.maximum(m_prev, jnp.max(s, axis=1, keepdims=True))
            p = jnp.exp2(s - jnp.tile(m_new, (1, tq // LANES)))
            alpha = jnp.exp2(m_prev - m_new)
            acc_ref[mi] = jnp.tile(alpha, (1, 2)) * acc_ref[mi] + _dot(p.astype(BF16), vx)
            m_ref[mi] = m_new

    def scores_to(j, s_ref):
        for mi in range(2):
            s_ref[mi] = scores(j, mi)

    scores_to(0, sa_ref)

    def pair(t, carry):
        j = 2 * t
        scores_to(j + 1, sb_ref)
        softmax_pv(j, lambda mi: sa_ref[mi], False)
        scores_to(j + 2, sa_ref)
        softmax_pv(j + 1, lambda mi: sb_ref[mi], False)
        return carry

    lax.fori_loop(0, i // 2, pair, 0)

    @pl.when(i % 2 == 1)
    def _():
        softmax_pv(i - 1, lambda mi: sa_ref[mi], False)
        scores_to(i, sa_ref)

    softmax_pv(i, lambda mi: sa_ref[mi], True)

    lv = lam_ref[...]
    lam = (jnp.exp(jnp.sum(lv[0:1] * lv[1:2], axis=1, keepdims=True))
           - jnp.exp(jnp.sum(lv[2:3] * lv[3:4], axis=1, keepdims=True)) + lambda_init)
    a0 = acc_ref[0]
    a1 = acc_ref[1]
    o = a0[:, :dv] / a0[:, dv:] - lam * (a1[:, :dv] / a1[:, dv:])
    ms = jnp.mean(o * o, axis=-1, keepdims=True)
    o = o * lax.rsqrt(ms + LN_EPS) * g_ref[...] * (1.0 - lambda_init)
    o_ref[...] = o.astype(o_ref.dtype)


def _diff_attention(h0, lam_vec, subln_g, lambda_init, tq):
    t = h0.shape[0]
    tq = min(tq, t)
    nq = t // tq
    kern = functools.partial(_diff_attn_kernel, tq=tq, lambda_init=lambda_init)
    return pl.pallas_call(
        kern,
        grid=(DIFF_HEADS, nq),
        in_specs=[
            pl.BlockSpec((4, DIFF_HEAD_DIM), lambda h, i: (0, 0)),
            pl.BlockSpec((1, DIFF_V_DIM), lambda h, i: (0, 0)),
            pl.BlockSpec((tq, LANES), lambda h, i: (i, h)),
            pl.BlockSpec((t, LANES), lambda h, i: (0, DIFF_HEADS + h)),
            pl.BlockSpec((t, LANES), lambda h, i: (0, 2 * DIFF_HEADS + h)),
        ],
        out_specs=pl.BlockSpec((tq, LANES), lambda h, i: (i, h)),
        out_shape=jax.ShapeDtypeStruct((t, DIFF_HEADS * DIFF_V_DIM), BF16),
        scratch_shapes=[pltpu.VMEM((t, 2 * DIFF_V_DIM), BF16), pltpu.VMEM((2, tq, LANES), F32),
                        pltpu.VMEM((2, tq, 2 * DIFF_V_DIM), F32),
                        pltpu.VMEM((2, tq, tq), F32), pltpu.VMEM((2, tq, tq), F32)],
        compiler_params=_params(("arbitrary", "arbitrary")),
        name="diff_attn",
    )(lam_vec, subln_g, h0, h0, h0)


def _gla_kernel(q_ref, k_ref, v_ref, r_ref, gb_ref, wg_ref, bg_ref, ng_ref, o_ref, st_ref, *, tg):
    @pl.when(pl.program_id(0) == 0)
    def _():
        st_ref[...] = jnp.zeros(st_ref.shape, F32)

    c = GLA_CHUNK
    nch = tg // c
    gate = _dot(gb_ref[...], wg_ref[...]) + bg_ref[...]
    log_a = -(jnp.maximum(-gate, 0.0) + jnp.log(1.0 + jnp.exp(-jnp.abs(gate)))) / GLA_GATE_TEMP
    row = lax.broadcasted_iota(jnp.int32, (tg, tg), 0)
    col = lax.broadcasted_iota(jnp.int32, (tg, tg), 1)
    tril = jnp.logical_and(row >= col, row // c == col // c)
    tri = jnp.where(tril, 1.0, 0.0).astype(BF16)
    for h in range(GLA_HEADS):
        ksl = slice(h * GLA_DK, (h + 1) * GLA_DK)
        vsl = slice(h * GLA_DV, (h + 1) * GLA_DV)
        b = _split_dot_left(tri, log_a[:, ksl])
        b_last = jnp.concatenate(
            [jnp.broadcast_to(b[(ci + 1) * c - 1:(ci + 1) * c, :], (c, GLA_DK)) for ci in range(nch)], axis=0)
        qf = q_ref[:, ksl].astype(F32) * (GLA_DK ** -0.5)
        kf = k_ref[:, ksl].astype(F32)
        v = v_ref[:, vsl]
        q_dec = (qf * jnp.exp(b)).astype(BF16)
        k_dec = (kf * jnp.exp(-b)).astype(BF16)
        k_end = (kf * jnp.exp(b_last - b)).astype(BF16)
        scores = jnp.where(tril, _nt_dot(q_dec, k_dec), 0.0)
        o_intra = _dot(scores.astype(BF16), v)
        for ci in range(nch):
            rs = slice(ci * c, (ci + 1) * c)
            st = st_ref[h]
            o = o_intra[rs, :] + _nt_dot(q_dec[rs, :], st.astype(BF16))
            kv_t = lax.dot_general(v[rs, :], k_end[rs, :], (((0,), (0,)), ((), ())), preferred_element_type=F32)
            st_ref[h] = jnp.exp(b_last[rs, :][0:1, :]) * st + kv_t
            ms = jnp.mean(o * o, axis=-1, keepdims=True)
            o = o * lax.rsqrt(ms + LN_EPS) * ng_ref[...]
            r = r_ref[rs, vsl].astype(F32)
            o_ref[rs, vsl] = (o * (r / (1.0 + jnp.exp(-r)))).astype(o_ref.dtype)


def _gla(h0, wg_pad, b_gate, norm_g, tg):
    t = h0.shape[0]
    tg = min(tg, t)
    qk = GLA_HEADS * GLA_DK
    vw = GLA_HEADS * GLA_DV
    return pl.pallas_call(
        functools.partial(_gla_kernel, tg=tg),
        grid=(t // tg,),
        in_specs=[
            pl.BlockSpec((tg, qk), lambda i: (i, L0_COL["qb"] // qk)),
            pl.BlockSpec((tg, qk), lambda i: (i, L0_COL["kb"] // qk)),
            pl.BlockSpec((tg, vw), lambda i: (i, L0_COL["vb"] // vw)),
            pl.BlockSpec((tg, vw), lambda i: (i, L0_COL["rb"] // vw)),
            pl.BlockSpec((tg, LANES), lambda i: (i, L0_COL["gb"] // LANES)),
            pl.BlockSpec((LANES, qk), lambda i: (0, 0)),
            pl.BlockSpec((1, qk), lambda i: (0, 0)),
            pl.BlockSpec((1, GLA_DV), lambda i: (0, 0)),
        ],
        out_specs=pl.BlockSpec((tg, vw), lambda i: (i, 0)),
        out_shape=jax.ShapeDtypeStruct((t, vw), BF16),
        scratch_shapes=[pltpu.VMEM((GLA_HEADS, GLA_DV, GLA_DK), F32)],
        compiler_params=_params(("arbitrary",)),
        name="gla",
    )(h0, h0, h0, h0, h0, wg_pad, b_gate, norm_g)


def _sb_kernel(q_ref, k_ref, v_ref, o_ref, acc_ref, c_ref, *, tq):
    i = pl.program_id(1)
    hd = SB_HEAD_DIM
    acc_ref[...] = jnp.zeros(acc_ref.shape, F32)
    c_ref[...] = jnp.zeros(c_ref.shape, F32)
    row = lax.broadcasted_iota(jnp.int32, (tq, tq), 0)
    col = lax.broadcasted_iota(jnp.int32, (tq, tq), 1)
    strict = col < row
    tri = jnp.where(row > col, 1.0, 0.0).astype(BF16)

    def block(jb, masked):
        off = pl.multiple_of(jb * tq, tq)
        cmax = None
        for hh in range(SB_HEADS_PER_STEP):
            cols = slice(hh * hd, (hh + 1) * hd)
            k = k_ref[pl.ds(off, tq), cols]
            v = v_ref[pl.ds(off, tq), cols]
            z = _nt_dot(q_ref[:, cols], k)
            lsn = -(jnp.maximum(z, 0.0) + jnp.log(1.0 + jnp.exp(-jnp.abs(z))))
            lm = jnp.where(strict, lsn, 0.0) if masked else lsn
            excl = _dot(lm.astype(BF16), tri) + c_ref[hh]
            w = jnp.exp(z + lsn + excl)
            if masked:
                w = jnp.where(strict, w, 0.0)
            acc_ref[:, cols] += _dot(w.astype(BF16), v)
            c_new = c_ref[hh] + jnp.sum(lm, axis=1, keepdims=True)
            c_ref[hh] = c_new
            hmax = jnp.max(c_new)
            cmax = hmax if cmax is None else jnp.maximum(cmax, hmax)
        return cmax

    cmax = block(i, True)

    def cond(carry):
        jb, cm = carry
        return jnp.logical_and(jb >= 0, cm > -SB_LOG_FLOOR)

    def body(carry):
        jb, _ = carry
        return jb - 1, block(jb, False)

    lax.while_loop(cond, body, (i - 1, cmax))
    o_ref[...] = acc_ref[...].astype(o_ref.dtype)


def _stick_breaking(h1, tq):
    t = h1.shape[0]
    tq = min(tq, t)
    ng = SB_HEADS // SB_HEADS_PER_STEP
    wd = SB_HEADS_PER_STEP * SB_HEAD_DIM
    return pl.pallas_call(
        functools.partial(_sb_kernel, tq=tq),
        grid=(ng, t // tq),
        in_specs=[
            pl.BlockSpec((tq, wd), lambda h, i: (i, h)),
            pl.BlockSpec((t, wd), lambda h, i: (0, ng + h), pipeline_mode=pl.Buffered(1)),
            pl.BlockSpec((t, wd), lambda h, i: (0, 2 * ng + h), pipeline_mode=pl.Buffered(1)),
        ],
        out_specs=pl.BlockSpec((tq, wd), lambda h, i: (i, h)),
        out_shape=jax.ShapeDtypeStruct((t, SB_HEADS * SB_HEAD_DIM), BF16),
        scratch_shapes=[pltpu.VMEM((tq, wd), F32), pltpu.VMEM((SB_HEADS_PER_STEP, tq, 1), F32)],
        compiler_params=_params(("parallel", "arbitrary")),
        name="stick_breaking",
    )(h1, h1, h1)


def _s5_prep_kernel(are_ref, aim_ref, ls_ref, bre_ref, bim_ref, lbr_ref, lbi_ref, bbr_ref, bbi_ref):
    step = jnp.exp(ls_ref[...])
    lam_re = are_ref[...]
    lam_im = aim_ref[...]
    mag = jnp.exp(step * lam_re)
    lb_re = mag * jnp.cos(step * lam_im)
    lb_im = mag * jnp.sin(step * lam_im)
    den = lam_re * lam_re + lam_im * lam_im
    n_re = lb_re - 1.0
    f_re = (n_re * lam_re + lb_im * lam_im) / den
    f_im = (lb_im * lam_re - n_re * lam_im) / den
    lbr_ref[...] = lb_re
    lbi_ref[...] = lb_im
    bbr_ref[...] = f_re * bre_ref[...] - f_im * bim_ref[...]
    bbi_ref[...] = f_re * bim_ref[...] + f_im * bre_ref[...]


def _s5_prep(a_re, a_im, log_step, b_re, b_im):
    rows = S5_GROUPS * S5_GROUP
    rep = lambda p: jnp.repeat(p, S5_GROUP, axis=0)
    to_rows = lambda b: jnp.transpose(b, (0, 2, 1)).reshape(rows, S5_STATE)
    spec = pl.BlockSpec((rows, S5_STATE), lambda: (0, 0))
    shp = jax.ShapeDtypeStruct((rows, S5_STATE), F32)
    return pl.pallas_call(
        _s5_prep_kernel,
        in_specs=[spec, spec, pl.BlockSpec((rows, 1), lambda: (0, 0)), spec, spec],
        out_specs=[spec] * 4,
        out_shape=[shp] * 4,
        name="s5_prep",
    )(rep(a_re), rep(a_im), rep(log_step[:, None]), to_rows(b_re), to_rows(b_im))


def _cmul(ar, ai, br, bi):
    return ar * br - ai * bi, ar * bi + ai * br


def _block_diag_tiles(m, rows_per_group, cols_per_group):
    gpt = S5_GROUPS // S5_KT
    m = m.reshape(S5_KT, gpt, rows_per_group, cols_per_group)
    eye = jnp.eye(gpt, dtype=m.dtype)
    bd = jnp.einsum("kgrc,gh->kgrhc", m, eye)
    return bd.reshape(S5_KT, gpt * rows_per_group, gpt * cols_per_group)


def _s5_tile_params(lb_re, lb_im, bb_re, bb_im, c_re, c_im):
    bd = jnp.concatenate([_block_diag_tiles(bb_re, S5_GROUP, S5_STATE),
                          _block_diag_tiles(bb_im, S5_GROUP, S5_STATE)], axis=2)
    cd = jnp.concatenate([_block_diag_tiles(jnp.transpose(c_re, (0, 2, 1)), S5_STATE, S5_GROUP),
                          _block_diag_tiles(-jnp.transpose(c_im, (0, 2, 1)), S5_STATE, S5_GROUP)], axis=1)
    pr = [jnp.ones_like(lb_re)]
    pi = [jnp.zeros_like(lb_re)]
    for _ in range(S5_L):
        nr, ni = _cmul(pr[-1], pi[-1], lb_re, lb_im)
        pr.append(nr)
        pi.append(ni)

    def table(p):
        rows = jnp.transpose(jnp.stack(p).reshape(S5_L + 1, S5_KT, S5_KT_STATES), (1, 0, 2))
        return jnp.pad(rows, ((0, 0), (0, LANES - (S5_L + 1)), (0, 0)))

    return bd, cd, jnp.stack([table(pr), table(pi)], axis=1)


def _s5_core_kernel(u_ref, bd_ref, cd_ref, pw8_ref, y_ref, hs_ref, hp_ref, pw_ref, st_ref, uf_ref,
                    mt_s, bbs_s, ccs_s, *, tcr):
    ns = S5_KT_STATES
    pad = SUBLANES
    cw = S5_KT_CH

    @pl.when(pl.program_id(1) == 0)
    def _():
        b_re = bd_ref[:, :ns]
        b_im = bd_ref[:, ns:]
        cd_hi = cd_ref[...].astype(BF16)
        cd_lo = (cd_ref[...] - cd_hi.astype(F32)).astype(BF16)
        for k in range(S5_L):
            a = pw8_ref[0, k:k + 1, :]
            b = pw8_ref[1, k:k + 1, :]
            dk = jnp.concatenate([b_re * a - b_im * b, b_re * b + b_im * a], axis=1)
            s = S5_L - 1 - k
            bbs_s[s * cw:(s + 1) * cw, :] = dk.astype(BF16)
            dk_hi = dk.astype(BF16)
            dk_lo = (dk - dk_hi.astype(F32)).astype(BF16)
            kk = (_dot(dk_hi, cd_hi) + (_dot(dk_lo, cd_hi) + _dot(dk_hi, cd_lo))).astype(BF16)
            for s0 in range(S5_L - k):
                mt_s[s0 * cw:(s0 + 1) * cw, (s0 + k) * cw:(s0 + k + 1) * cw] = kk
        pw_t = (jnp.transpose(pw8_ref[0]), jnp.transpose(pw8_ref[1]))
        c_re = cd_ref[:ns, :]
        c_im = cd_ref[ns:, :]
        for i in range(S5_L):
            a = pw_t[0][:, i + 1:i + 2]
            b = pw_t[1][:, i + 1:i + 2]
            ccs_s[:ns, i * cw:(i + 1) * cw] = (c_re * a + c_im * b).astype(BF16)
            ccs_s[ns:, i * cw:(i + 1) * cw] = (c_im * a - c_re * b).astype(BF16)

        hp_ref[...] = jnp.zeros(hp_ref.shape, F32)
        hs_ref[0:pad, :] = jnp.zeros((pad, 2 * ns), F32)
        l1r = jnp.broadcast_to(pw8_ref[0, S5_L:S5_L + 1, :], (SUBLANES, ns))
        l1i = jnp.broadcast_to(pw8_ref[1, S5_L:S5_L + 1, :], (SUBLANES, ns))
        l2r, l2i = _cmul(l1r, l1i, l1r, l1i)
        l4r, l4i = _cmul(l2r, l2i, l2r, l2i)
        l8r, l8i = _cmul(l4r, l4i, l4r, l4i)
        ridx = lax.broadcasted_iota(jnp.int32, l1r.shape, 0)
        n = ridx + 1
        pr = jnp.ones_like(l1r)
        pi = jnp.zeros_like(l1r)
        for bit, (lr, li) in ((1, (l1r, l1i)), (2, (l2r, l2i)), (4, (l4r, l4i)), (8, (l8r, l8i))):
            nr, ni = _cmul(pr, pi, lr, li)
            take = (n & bit) != 0
            pr = jnp.where(take, nr, pr)
            pi = jnp.where(take, ni, pi)
        pw_ref[0] = pr
        pw_ref[1] = pi
        for s, (lr, li) in enumerate(((l1r, l1i), (l2r, l2i), (l4r, l4i))):
            keep = ridx >= (1 << s)
            st_ref[2 * s] = jnp.where(keep, lr, 0.0)
            st_ref[2 * s + 1] = jnp.where(keep, li, 0.0)

    nl = S5_KT_CH // LANES
    for h in range(nl):
        uf_ref[h] = u_ref[:, h * LANES:(h + 1) * LANES].astype(F32)
    uc = jnp.concatenate([uf_ref[h, pl.ds(s, tcr, stride=S5_L), :].astype(BF16)
                          for s in range(S5_L) for h in range(nl)], axis=1)
    hs_ref[pad:pad + tcr, :] = _dot(uc, bbs_s[...])
    w = S5_SCAN_W
    for cc in range(ns // w):
        re0 = cc * w
        im0 = ns + cc * w

        def body(r, carry, re0=re0, im0=im0):
            hpr, hpi = carry
            rows = pl.ds(pl.multiple_of(pad + r * SUBLANES, SUBLANES), SUBLANES)
            xr = hs_ref[rows, re0:re0 + w]
            xi = hs_ref[rows, im0:im0 + w]
            for s in range(3):
                sr = pltpu.roll(xr, 1 << s, 0)
                si = pltpu.roll(xi, 1 << s, 0)
                ar = st_ref[2 * s, :, re0:re0 + w]
                ai = st_ref[2 * s + 1, :, re0:re0 + w]
                xr, xi = xr + ar * sr - ai * si, xi + ar * si + ai * sr
            pr = pw_ref[0, :, re0:re0 + w]
            pi = pw_ref[1, :, re0:re0 + w]
            xr, xi = xr + pr * hpr - pi * hpi, xi + pr * hpi + pi * hpr
            hs_ref[rows, re0:re0 + w] = xr
            hs_ref[rows, im0:im0 + w] = xi
            return (jnp.broadcast_to(xr[SUBLANES - 1:SUBLANES, :], (SUBLANES, w)),
                    jnp.broadcast_to(xi[SUBLANES - 1:SUBLANES, :], (SUBLANES, w)))

        hpr, hpi = lax.fori_loop(0, tcr // SUBLANES, body, (hp_ref[:, re0:re0 + w], hp_ref[:, im0:im0 + w]))
        hp_ref[:, re0:re0 + w] = hpr
        hp_ref[:, im0:im0 + w] = hpi

    h_prev = hs_ref[pad - 1:pad - 1 + tcr, :]
    y_in = _dot(h_prev.astype(BF16), ccs_s[...])
    for i in range(S5_L):
        k_hi = (i + 1) * cw
        y_i = y_in[:, i * cw:k_hi] + _dot(uc[:, :k_hi], mt_s[:k_hi, i * cw:k_hi])
        for h in range(nl):
            uf_ref[h, pl.ds(i, tcr, stride=S5_L), :] = y_i[:, h * LANES:(h + 1) * LANES]
    for h in range(nl):
        y_ref[:, h * LANES:(h + 1) * LANES] = uf_ref[h]
    hs_ref[pad - 1:pad, :] = hs_ref[pad + tcr - 1:pad + tcr, :]


def _s5_core(h1, bd, cd, pw8, tcr):
    t = h1.shape[0]
    width = S5_L * S5_KT_CH
    tcr = min(tcr, t // S5_L)
    tc = tcr * S5_L
    ns = S5_KT_STATES
    u_col0 = 3 * S5_CHANNELS // S5_KT_CH
    once = pl.Buffered(1)
    return pl.pallas_call(
        functools.partial(_s5_core_kernel, tcr=tcr),
        grid=(S5_KT, t // tc),
        in_specs=[
            pl.BlockSpec((tc, S5_KT_CH), lambda k, i: (i, u_col0 + k)),
            pl.BlockSpec((None, S5_KT_CH, 2 * ns), lambda k, i: (k, 0, 0), pipeline_mode=once),
            pl.BlockSpec((None, 2 * ns, S5_KT_CH), lambda k, i: (k, 0, 0), pipeline_mode=once),
            pl.BlockSpec((None, 2, LANES, ns), lambda k, i: (k, 0, 0, 0), pipeline_mode=once),
        ],
        out_specs=pl.BlockSpec((tc, S5_KT_CH), lambda k, i: (i, k)),
        out_shape=jax.ShapeDtypeStruct((t, S5_CHANNELS), F32),
        scratch_shapes=[pltpu.VMEM((tcr + SUBLANES, 2 * ns), F32), pltpu.VMEM((SUBLANES, 2 * ns), F32),
                        pltpu.VMEM((2, SUBLANES, ns), F32), pltpu.VMEM((6, SUBLANES, ns), F32),
                        pltpu.VMEM((S5_KT_CH // LANES, tc, LANES), F32),
                        pltpu.VMEM((width, width), BF16), pltpu.VMEM((width, 2 * ns), BF16),
                        pltpu.VMEM((2 * ns, width), BF16)],
        compiler_params=_params(("arbitrary", "arbitrary")),
        name="s5_core",
    )(h1, bd, cd, pw8)


def _s5_glu_kernel(y_ref, u_ref, d_ref, wglu_ref, bglu_ref, o_ref):
    y = y_ref[...] + d_ref[...] * u_ref[...].astype(F32)
    g = 0.5 * y * (1.0 + jnp.tanh(math.sqrt(2.0 / math.pi) * (y + 0.044715 * (y * y * y))))
    gl = _dot(g.astype(BF16), wglu_ref[...]) + bglu_ref[...]
    o_ref[...] = (g / (1.0 + jnp.exp(-gl))).astype(o_ref.dtype)


def _s5_glu(y, h1, d_skip, w_glu, b_glu, tm):
    t = h1.shape[0]
    tm = min(tm, t)
    full = lambda shape: pl.BlockSpec(shape, lambda i: (0,) * len(shape))
    return pl.pallas_call(
        _s5_glu_kernel,
        grid=(t // tm,),
        in_specs=[pl.BlockSpec((tm, S5_CHANNELS), lambda i: (i, 0)),
                  pl.BlockSpec((tm, S5_CHANNELS), lambda i: (i, 3)),
                  full((1, S5_CHANNELS)), full((S5_CHANNELS, S5_CHANNELS)), full((1, S5_CHANNELS))],
        out_specs=pl.BlockSpec((tm, S5_CHANNELS), lambda i: (i, 0)),
        out_shape=jax.ShapeDtypeStruct((t, S5_CHANNELS), BF16),
        compiler_params=_params(("parallel",)),
        name="s5_glu",
    )(y, h1, d_skip, w_glu, b_glu)


def _route(probs, sel):
    rows = [sel[e:e + 1, :] for e in range(N_EXPERTS)]
    prow = [probs[e:e + 1, :] for e in range(N_EXPERTS)]
    best_score = None
    for g in range(N_GROUPS):
        a = rows[g * EXPERTS_PER_GROUP:(g + 1) * EXPERTS_PER_GROUP]
        score = None
        for x in range(EXPERTS_PER_GROUP):
            for y in range(x + 1, EXPERTS_PER_GROUP):
                pair = a[x] + a[y]
                score = pair if score is None else jnp.maximum(score, pair)
        if best_score is None:
            best_score, gbest = score, jnp.zeros(score.shape, jnp.int32)
            cs = list(a)
            cp = prow[0:EXPERTS_PER_GROUP]
        else:
            better = score > best_score
            best_score = jnp.where(better, score, best_score)
            gbest = jnp.where(better, g, gbest)
            cs = [jnp.where(better, a[x], cs[x]) for x in range(EXPERTS_PER_GROUP)]
            cp = [jnp.where(better, prow[g * EXPERTS_PER_GROUP + x], cp[x]) for x in range(EXPERTS_PER_GROUP)]

    def first_argmax(vals):
        bv, bi = vals[0], jnp.zeros(vals[0].shape, jnp.int32)
        for x in range(1, len(vals)):
            better = vals[x] > bv
            bv = jnp.where(better, vals[x], bv)
            bi = jnp.where(better, x, bi)
        return bi

    i1 = first_argmax(cs)
    i2 = first_argmax([jnp.where(i1 == x, -jnp.inf, cs[x]) for x in range(EXPERTS_PER_GROUP)])

    def pick(idx):
        out = cp[0]
        for x in range(1, EXPERTS_PER_GROUP):
            out = jnp.where(idx == x, cp[x], out)
        return out

    p1, p2 = pick(i1), pick(i2)
    tot = p1 + p2
    return gbest * EXPERTS_PER_GROUP + i1, gbest * EXPERTS_PER_GROUP + i2, p1 / tot, p2 / tot


def _ln_router_tail(v, g_ref, b_ref, rwh_ref, rwl_ref, rb_ref):
    xn = _layer_norm(v, g_ref[...], b_ref[...])
    xh = xn.astype(BF16)
    xl = (xn - xh.astype(F32)).astype(BF16)
    logits = _nt_dot(rwh_ref[...], xh) + (_nt_dot(rwl_ref[...], xh) + _nt_dot(rwh_ref[...], xl))
    mx = jnp.max(logits, axis=0, keepdims=True)
    ex = jnp.exp(logits - mx)
    probs = ex / jnp.sum(ex, axis=0, keepdims=True)
    e1, e2, w1, w2 = _route(probs, probs + rb_ref[...])
    n = e1.shape[1]
    ei = jnp.concatenate([e1, e2, jnp.zeros((SUBLANES - 2, n), jnp.int32)], axis=0)
    ew = jnp.concatenate([w1, w2, jnp.zeros((SUBLANES - 2, n), F32)], axis=0)
    return xn, ei, ew


def _outproj_kernel(ma_ref, mb_ref, w_ref, xr_ref, g_ref, b_ref, rwh_ref, rwl_ref, rb_ref,
                    x_ref, ei_ref, ew_ref, w_bf, *, sub):
    @pl.when(pl.program_id(0) == 0)
    def _():
        w_bf[...] = w_ref[...].astype(BF16)

    half = w_bf.shape[0] // 2
    for s in range(x_ref.shape[0] // sub):
        rows = slice(s * sub, (s + 1) * sub)
        v = (DEEPNORM_ALPHA * xr_ref[rows, :] + _dot(ma_ref[rows, :], w_bf[:half, :])
             + _dot(mb_ref[rows, :], w_bf[half:, :]))
        xn, ei, ew = _ln_router_tail(v, g_ref, b_ref, rwh_ref, rwl_ref, rb_ref)
        x_ref[rows, :] = xn
        ei_ref[:, rows] = ei
        ew_ref[:, rows] = ew


def _outproj_ln_router(mix_a, mix_b, w_out, x_res, ln_g, ln_b, rwt, rb, tm, sub):
    t = x_res.shape[0]
    rwt_hi = rwt.astype(BF16)
    rwt_lo = (rwt - rwt_hi.astype(F32)).astype(BF16)
    tm = min(tm, t)
    sub = min(sub, tm)
    half = w_out.shape[0] // 2
    row = lambda w: pl.BlockSpec((tm, w), lambda i: (i, 0))
    full = lambda shape: pl.BlockSpec(shape, lambda i: (0,) * len(shape))
    return pl.pallas_call(
        functools.partial(_outproj_kernel, sub=sub),
        grid=(t // tm,),
        in_specs=[row(half), row(half),
                  pl.BlockSpec((2 * half, D_MODEL), lambda i: (0, 0), pipeline_mode=pl.Buffered(1)),
                  row(D_MODEL), full((1, D_MODEL)), full((1, D_MODEL)),
                  full((N_EXPERTS, D_MODEL)), full((N_EXPERTS, D_MODEL)), full((N_EXPERTS, 1))],
        out_specs=[row(D_MODEL),
                   pl.BlockSpec((SUBLANES, tm), lambda i: (0, i)), pl.BlockSpec((SUBLANES, tm), lambda i: (0, i))],
        out_shape=[jax.ShapeDtypeStruct((t, D_MODEL), F32),
                   jax.ShapeDtypeStruct((SUBLANES, t), jnp.int32), jax.ShapeDtypeStruct((SUBLANES, t), F32)],
        scratch_shapes=[pltpu.VMEM((2 * half, D_MODEL), BF16)],
        compiler_params=_params(("arbitrary",)),
        name="outproj_ln_router",
    )(mix_a, mix_b, w_out, x_res, ln_g, ln_b, rwt_hi, rwt_lo, rb)


def _rank_kernel(e_ref, rank_ref, cnt_ref, base_ref, *, tn):
    @pl.when(pl.program_id(0) == 0)
    def _():
        base_ref[...] = jnp.zeros(base_ref.shape, F32)

    e = e_ref[...]
    rows = lax.broadcasted_iota(jnp.int32, (N_EXPERTS, tn), 0)
    oh1 = rows == e[0:1, :]
    oh2 = rows == e[1:2, :]
    oh = jnp.where(jnp.logical_or(oh1, oh2), 1.0, 0.0)
    r = lax.broadcasted_iota(jnp.int32, (tn, tn), 0)
    c = lax.broadcasted_iota(jnp.int32, (tn, tn), 1)
    earlier = jnp.where(r < c, 1.0, 0.0).astype(BF16)
    base = base_ref[...]
    before = _dot(oh.astype(BF16), earlier) + jnp.tile(base, (1, tn // LANES))
    r1 = jnp.sum(jnp.where(oh1, before, 0.0), axis=0, keepdims=True)
    r2 = jnp.sum(jnp.where(oh2, before, 0.0), axis=0, keepdims=True)
    rank_ref[...] = jnp.concatenate([r1, r2, jnp.zeros((SUBLANES - 2, tn), F32)], axis=0).astype(jnp.int32)
    base = base + jnp.sum(oh, axis=1, keepdims=True)
    base_ref[...] = base
    cnt_ref[...] = base.astype(jnp.int32)


def _rank(eidx, tn):
    t = eidx.shape[1]
    tn = min(tn, t)
    return pl.pallas_call(
        functools.partial(_rank_kernel, tn=tn),
        grid=(t // tn,),
        in_specs=[pl.BlockSpec((SUBLANES, tn), lambda i: (0, i))],
        out_specs=[pl.BlockSpec((SUBLANES, tn), lambda i: (0, i)),
                   pl.BlockSpec((N_EXPERTS, LANES), lambda i: (0, 0))],
        out_shape=[jax.ShapeDtypeStruct((SUBLANES, t), jnp.int32),
                   jax.ShapeDtypeStruct((N_EXPERTS, LANES), jnp.int32)],
        scratch_shapes=[pltpu.VMEM((N_EXPERTS, LANES), F32)],
        compiler_params=_params(("arbitrary",)),
        name="moe_rank",
    )(eidx)


def _moe_kernel(te_ref, nu_ref, xs_ref, wg_ref, wu_ref, wd_ref, y_ref, wgu_s, wd_s):
    tile = pl.program_id(0)
    used = tile < nu_ref[0]
    fresh = jnp.logical_or(tile == 0, te_ref[tile] != te_ref[jnp.maximum(tile - 1, 0)])

    @pl.when(jnp.logical_and(used, fresh))
    def _():
        wgu_s[:, :D_EXPERT] = wg_ref[...].astype(BF16)
        wgu_s[:, D_EXPERT:] = wu_ref[...].astype(BF16)
        wd_s[...] = wd_ref[...].astype(BF16)

    @pl.when(used)
    def _():
        hgu = _dot(xs_ref[...].astype(BF16), wgu_s[...])
        hg = hgu[:, :D_EXPERT]
        hu = hgu[:, D_EXPERT:]
        h = (hg / (1.0 + jnp.exp(-hg))) * hu
        y_ref[...] = _dot(h.astype(BF16), wd_s[...])

    @pl.when(jnp.logical_not(used))
    def _():
        y_ref[...] = jnp.zeros(y_ref.shape, F32)


def _moe(tile_expert, n_used, xs, w_gate, w_up, w_down, tm):
    t_pad = xs.shape[0]
    grid_spec = pltpu.PrefetchScalarGridSpec(
        num_scalar_prefetch=2,
        grid=(t_pad // tm,),
        in_specs=[
            pl.BlockSpec((tm, D_MODEL), lambda i, te, nu: (i, 0)),
            pl.BlockSpec((None, D_MODEL, D_EXPERT), lambda i, te, nu: (te[i], 0, 0)),
            pl.BlockSpec((None, D_MODEL, D_EXPERT), lambda i, te, nu: (te[i], 0, 0)),
            pl.BlockSpec((None, D_EXPERT, D_MODEL), lambda i, te, nu: (te[i], 0, 0)),
        ],
        out_specs=pl.BlockSpec((tm, D_MODEL), lambda i, te, nu: (i, 0)),
        scratch_shapes=[pltpu.VMEM((D_MODEL, 2 * D_EXPERT), BF16), pltpu.VMEM((D_EXPERT, D_MODEL), BF16)],
    )
    return pl.pallas_call(
        _moe_kernel,
        grid_spec=grid_spec,
        out_shape=jax.ShapeDtypeStruct((t_pad, D_MODEL), F32),
        compiler_params=_params(("arbitrary",)),
        name="moe",
    )(tile_expert, n_used, xs, w_gate, w_up, w_down)


def _moe_ln_kernel(x_ref, y1_ref, y2_ref, w_ref, g_ref, b_ref, o_ref):
    w = w_ref[...]
    v = DEEPNORM_ALPHA * x_ref[...] + (w[:, 0:1] * y1_ref[...] + w[:, 1:2] * y2_ref[...])
    o_ref[...] = _layer_norm(v, g_ref[...], b_ref[...])


def _moe_ln(x, y12, w_cols, ln_g, ln_b, tm):
    t = x.shape[0]
    tm = min(tm, t)
    nb = t // tm
    row = pl.BlockSpec((tm, D_MODEL), lambda i: (i, 0))
    second = pl.BlockSpec((tm, D_MODEL), lambda i: (nb + i, 0))
    vec = pl.BlockSpec((1, D_MODEL), lambda i: (0, 0))
    return pl.pallas_call(
        _moe_ln_kernel,
        grid=(nb,),
        in_specs=[row, row, second, pl.BlockSpec((tm, SUBLANES), lambda i: (i, 0)), vec, vec],
        out_specs=row,
        out_shape=jax.ShapeDtypeStruct((t, D_MODEL), F32),
        compiler_params=_params(("parallel",)),
        name="moe_ln",
    )(x, y12, y12, w_cols, ln_g, ln_b)


def _moe_block(x, eidx, ew, w_gate, w_up, w_down, ln_g, ln_b, tm):
    t = x.shape[0]
    tm = min(tm, t)
    rank, cnt = _rank(eidx, 512)
    counts = cnt[:, 0]
    padded = ((counts + tm - 1) // tm) * tm
    ends = jnp.cumsum(padded)
    starts = ends - padded
    e2 = eidx[:2]
    start_of = jnp.zeros_like(e2)
    for k in range(N_EXPERTS):
        start_of = jnp.where(e2 == k, starts[k], start_of)
    dest = start_of + rank[:2]
    t_pad = 2 * t + N_EXPERTS * tm
    tok = jnp.tile(jnp.arange(t, dtype=jnp.int32), 2)
    src_tok = (jnp.arange(t_pad, dtype=jnp.int32) % t).at[dest.reshape(2 * t)].set(
        tok, mode="promise_in_bounds", unique_indices=True)
    tile_start = jnp.arange(t_pad // tm, dtype=jnp.int32) * tm
    tile_expert = jnp.minimum(jnp.sum((ends[None, :] <= tile_start[:, None]).astype(jnp.int32), axis=1),
                              N_EXPERTS - 1)
    n_used = (ends[-1] // tm).astype(jnp.int32).reshape(1)
    xs = x.at[src_tok].get(mode="promise_in_bounds")
    ys = _moe(tile_expert, n_used, xs, w_gate, w_up, w_down, tm)
    y12 = ys.at[dest.reshape(2 * t)].get(mode="promise_in_bounds")
    return _moe_ln(x, y12, ew.T, ln_g, ln_b, tm)


def kernel(x, router_w, router_b, l0_w_in, l0_diff_lam_q1, l0_diff_lam_k1, l0_diff_lam_q2, l0_diff_lam_k2, l0_diff_subln_g, l0_gla_w_gate, l0_gla_b_gate, l0_gla_norm_g, l0_w_out, l0_ln1_g, l0_ln1_b, l0_moe_w_gate, l0_moe_w_up, l0_moe_w_down, l0_ln2_g, l0_ln2_b, l1_w_in, l1_s5_a_re, l1_s5_a_im, l1_s5_log_step, l1_s5_b_re, l1_s5_b_im, l1_s5_c_re, l1_s5_c_im, l1_s5_d, l1_s5_w_glu, l1_s5_b_glu, l1_w_out, l1_ln1_g, l1_ln1_b, l1_moe_w_gate, l1_moe_w_up, l1_moe_w_down, l1_ln2_g, l1_ln2_b):
    bsz, t, d = x.shape
    assert bsz == 1 and d == D_MODEL
    x0 = x.reshape(t, d)
    vec = lambda p: p.reshape(1, -1).astype(F32)
    rwt = router_w.T.astype(F32)
    rb = router_b.reshape(N_EXPERTS, 1).astype(F32)

    lambda_init = 0.8 - 0.6 * math.exp(-0.3 * 0)
    diff_w = DIFF_HEADS * DIFF_V_DIM
    gla_qk, gla_v = GLA_HEADS * GLA_DK, GLA_HEADS * GLA_DV
    c = [0]
    for width in (diff_w, diff_w, diff_w, gla_qk, gla_qk, gla_v, GLA_GATE_RANK, gla_v):
        c.append(c[-1] + width)
    q_scale = math.log2(math.e) * DIFF_HEAD_DIM ** -0.5
    w0 = jnp.concatenate([l0_w_in[:, :c[1]] * q_scale, l0_w_in[:, c[1]:c[6]], l0_w_in[:, c[7]:c[8]],
                          l0_w_in[:, c[6]:c[7]], jnp.zeros((d, L0_GB_PAD - GLA_GATE_RANK), l0_w_in.dtype)],
                         axis=1).astype(BF16)
    h0 = _matmul(x0, w0, 1024, L0_IN_TN, BF16)
    lam_vec = jnp.stack([l0_diff_lam_q1, l0_diff_lam_k1, l0_diff_lam_q2, l0_diff_lam_k2]).astype(F32)
    o_a = _diff_attention(h0, lam_vec, vec(l0_diff_subln_g), lambda_init, 1024)
    wg_pad = jnp.concatenate([l0_gla_w_gate, jnp.zeros((LANES - GLA_GATE_RANK, GLA_HEADS * GLA_DK), F32)],
                             axis=0).astype(BF16)
    o_b = _gla(h0, wg_pad, vec(l0_gla_b_gate), vec(l0_gla_norm_g), 256)
    x1, eidx, ew = _outproj_ln_router(o_a, o_b, l0_w_out, x0, vec(l0_ln1_g), vec(l0_ln1_b),
                                      rwt, rb, 512, 128)
    x2 = _moe_block(x1, eidx, ew, l0_moe_w_gate, l0_moe_w_up, l0_moe_w_down, vec(l0_ln2_g), vec(l0_ln2_b), 256)

    sb_w = SB_HEADS * SB_HEAD_DIM
    w1 = jnp.concatenate([l1_w_in[:, :sb_w] * SB_HEAD_DIM ** -0.5, l1_w_in[:, sb_w:]], axis=1).astype(BF16)
    h1 = _matmul(x2, w1, 1024, 1024, BF16)
    o_c = _stick_breaking(h1, 256)
    lbr, lbi, bbr, bbi = _s5_prep(l1_s5_a_re, l1_s5_a_im, l1_s5_log_step, l1_s5_b_re, l1_s5_b_im)
    bd, cd, pw8 = _s5_tile_params(lbr[::S5_GROUP], lbi[::S5_GROUP], bbr.reshape(S5_GROUPS, S5_GROUP, S5_STATE),
                                  bbi.reshape(S5_GROUPS, S5_GROUP, S5_STATE),
                                  l1_s5_c_re.astype(F32), l1_s5_c_im.astype(F32))
    y_s5 = _s5_core(h1, bd, cd, pw8, 256)
    o_d = _s5_glu(y_s5, h1, vec(l1_s5_d), l1_s5_w_glu.astype(BF16), vec(l1_s5_b_glu), 512)
    x3, eidx1, ew1 = _outproj_ln_router(o_c, o_d, l1_w_out, x2, vec(l1_ln1_g), vec(l1_ln1_b),
                                        rwt, rb, 512, 128)
    x4 = _moe_block(x3, eidx1, ew1, l1_moe_w_gate, l1_moe_w_up, l1_moe_w_down, vec(l1_ln2_g), vec(l1_ln2_b), 256)
    return x4.reshape(bsz, t, d)
```
